```python
import jax
import jax.numpy as jnp
from jax import lax
import numpy as np

D_MODEL = 1024
BATCH = 2
SEQ = 8192
DEPTH = 2

GRID_W = 64
CTX_LEN = 256
EPS = 1e-6

GLA_HEADS = 4
GLA_DK = 64
GLA_DV = 128
GLA_GATE_RANK = 16
GLA_GATE_NORM = 16.0
GLA_CHUNK = 64

MLA_HEADS = 4
MLA_NOPE = 128
MLA_ROPE = 64
MLA_V = 128
MLA_Q_RANK = 256
MLA_KV_RANK = 128
MLA_SCALE = (MLA_NOPE + MLA_ROPE) ** -0.5
ROPE_BASE = 10000.0
Q_BLOCK = 128

GLA_QK_W = GLA_HEADS * GLA_DK
GLA_V_W = GLA_HEADS * GLA_DV
MLA_V_W = MLA_HEADS * MLA_V
D_MIX = GLA_V_W + MLA_V_W

IN_PARTS = (('gla_q', GLA_QK_W), ('gla_k', GLA_QK_W), ('gla_v', GLA_V_W),
            ('gla_gf', GLA_GATE_RANK), ('gla_gb', GLA_GATE_RANK), ('gla_r', GLA_V_W),
            ('mla_cq', MLA_Q_RANK), ('mla_ckv', MLA_KV_RANK), ('mla_kr', MLA_ROPE))
D_IN = sum(w for _, w in IN_PARTS)
ALL_PARTS = tuple(n for n, _ in IN_PARTS)
CTX_STATE_PARTS = ('gla_k', 'gla_v', 'gla_gf', 'gla_gb', 'mla_ckv', 'mla_kr')

D_FF_DENSE = 2816
N_EXPERTS = 8
TOP_K = 2
D_FF_EXPERT = 3584
N_DENSE = (DEPTH + 1) // 2
N_MOE = DEPTH // 2

kernel_name = 'hybrid_gla_mla_moe_diffusion_block'


def rmsnorm(x, g):
    xf = x.astype(jnp.float32)
    y = xf * lax.rsqrt(jnp.mean(xf * xf, axis=-1, keepdims=True) + EPS)
    return (y * g.astype(jnp.float32)).astype(x.dtype)


def in_proj(h, w_in, names):
    offs, o = {}, 0
    for n, w in IN_PARTS:
        offs[n] = (o, o + w)
        o += w
    w = jnp.concatenate([w_in[:, offs[n][0]:offs[n][1]] for n in names], axis=1)
    y = h @ w
    sizes = [offs[n][1] - offs[n][0] for n in names]
    splits = np.cumsum(sizes)[:-1].tolist()
    return dict(zip(names, jnp.split(y, splits, axis=-1)))


def axial_rope_tables(n_tok):
    rows = n_tok // GRID_W
    row = jnp.repeat(jnp.arange(rows, dtype=jnp.float32), GRID_W)
    col = jnp.tile(jnp.arange(GRID_W, dtype=jnp.float32), rows)
    nfreq = MLA_ROPE // 4
    inv = ROPE_BASE ** (-jnp.arange(nfreq, dtype=jnp.float32) / nfreq)
    ar = row[:, None] * inv
    ac = col[:, None] * inv
    return (jnp.cos(ar), jnp.sin(ar), jnp.cos(ac), jnp.sin(ac))


def rope_half(x, cos, sin):
    x1, x2 = jnp.split(x, 2, axis=-1)
    return jnp.concatenate([x1 * cos - x2 * sin, x2 * cos + x1 * sin], axis=-1)


def axial_rope(x, tabs):
    cr, sr, cc, sc = tabs
    extra = x.ndim - 3
    shp = lambda t: t.reshape(t.shape[:1] + (1,) * extra + t.shape[1:]).astype(x.dtype)
    xr, xc = jnp.split(x, 2, axis=-1)
    return jnp.concatenate([rope_half(xr, shp(cr), shp(sr)), rope_half(xc, shp(cc), shp(sc))], axis=-1)


def bidir(a):
    return jnp.stack([a, jnp.flip(a, axis=-3)])


def gla_q(p):
    b, t = p.shape[:2]
    return p.reshape(b, t, GLA_HEADS, GLA_DK).astype(jnp.float32) * (GLA_DK ** -0.5)


def gla_kvg(p, w_g2, b_g2):
    b, t = p['gla_k'].shape[:2]
    k = p['gla_k'].reshape(b, t, GLA_HEADS, GLA_DK).astype(jnp.float32)
    v = p['gla_v'].reshape(b, t, GLA_HEADS, GLA_DV).astype(jnp.float32)
    codes = jnp.stack([p['gla_gf'], p['gla_gb']]).astype(jnp.float32)
    pre = jnp.einsum('zbtr,zrk->zbtk', codes, w_g2.astype(jnp.float32)) + b_g2.astype(jnp.float32)[:, None, None, :]
    g = (jax.nn.log_sigmoid(pre) / GLA_GATE_NORM).reshape(2, b, t, GLA_HEADS, GLA_DK)
    g = jnp.stack([g[0], jnp.flip(g[1], axis=-3)])
    return bidir(k), bidir(v), g


def gla_chunked(q, k, v, g, s0):
    t = q.shape[-3]
    n = t // GLA_CHUNK

    def to_chunks(a):
        a = a.reshape(a.shape[:-3] + (n, GLA_CHUNK) + a.shape[-2:])
        return jnp.swapaxes(jnp.moveaxis(a, -4, 0), -3, -2)

    qc, kc, vc, gc = to_chunks(q), to_chunks(k), to_chunks(v), to_chunks(g)
    bc = jnp.cumsum(gc, axis=-2)
    mask = jnp.tril(jnp.ones((GLA_CHUNK, GLA_CHUNK), dtype=bool))[:, :, None]

    def step(s, xs):
        qi, ki, vi, bi = xs
        b_last = bi[..., -1:, :]
        o_inter = jnp.einsum('...hcd,...hde->...hce', qi * jnp.exp(bi), s)
        diff = bi[..., :, None, :] - bi[..., None, :, :]
        decay = jnp.exp(jnp.where(mask, diff, -jnp.inf))
        attn = jnp.einsum('...htd,...hsd,...htsd->...hts', qi, ki, decay)
        o_intra = jnp.einsum('...hts,...hse->...hte', attn, vi)
        s_new = s * jnp.exp(b_last[..., 0, :])[..., None] + jnp.einsum('...hcd,...hce->...hde', ki * jnp.exp(b_last - bi), vi)
        return s_new, o_inter + o_intra

    s_fin, oc = lax.scan(step, s0, (qc, kc, vc, bc))
    o = jnp.moveaxis(jnp.swapaxes(oc, -3, -2), 0, -4)
    return o.reshape(o.shape[:-4] + (t,) + o.shape[-2:]), s_fin


def gla_final_state(k, v, g):
    b = jnp.cumsum(g, axis=-3)
    w = jnp.exp(b[..., -1:, :, :] - b)
    return jnp.einsum('...thd,...the->...hde', k * w, v)


def gla_out(o_dirs, r, gla_g):
    o = o_dirs[0] + jnp.flip(o_dirs[1], axis=-3)
    o = rmsnorm(o, gla_g)
    b, t = o.shape[:2]
    return (o.reshape(b, t, GLA_V_W) * jax.nn.silu(r.astype(jnp.float32))).astype(r.dtype)


def mla_q(cq, qn_g, w_uq, tabs):
    b, t = cq.shape[:2]
    q = (rmsnorm(cq, qn_g) @ w_uq).reshape(b, t, MLA_HEADS, MLA_NOPE + MLA_ROPE)
    if tabs is not None:
        q = jnp.concatenate([q[..., :MLA_NOPE], axial_rope(q[..., MLA_NOPE:], tabs)], axis=-1)
    return q


def mla_kv(p, kvn_g, w_ukv, tabs):
    b, t = p['mla_ckv'].shape[:2]
    kv = (rmsnorm(p['mla_ckv'], kvn_g) @ w_ukv).reshape(b, t, MLA_HEADS, MLA_NOPE + MLA_V)
    k_nope, v = kv[..., :MLA_NOPE], kv[..., MLA_NOPE:]
    kr = p['mla_kr']
    if tabs is not None:
        kr = axial_rope(kr, tabs)
    k = jnp.concatenate([k_nope, jnp.broadcast_to(kr[:, :, None, :], (b, t, MLA_HEADS, MLA_ROPE))], axis=-1)
    return k, v


def dense_attn(q, k, v):
    s = jnp.einsum('bqhd,bkhd->bhqk', q, k).astype(jnp.float32) * MLA_SCALE
    p = jax.nn.softmax(s, axis=-1).astype(v.dtype)
    return jnp.einsum('bhqk,bkhe->bqhe', p, v)


def block_attention(q, k, v):
    b, t, h, dq = q.shape
    nb = t // Q_BLOCK
    qb = jnp.moveaxis(q.reshape(b, nb, Q_BLOCK, h, dq), 1, 0)
    o = lax.map(lambda blk: dense_attn(blk, k, v), qb)
    return jnp.moveaxis(o, 0, 1).reshape(b, t, h * v.shape[-1])


def token_mixing(hl, hc, w_in, w_g2, b_g2, gla_g, qn_g, w_uq, kvn_g, w_ukv, w_out, tabs, need_ctx):
    b, t = hl.shape[:2]
    pl = in_proj(hl, w_in, ALL_PARTS)
    pc = in_proj(hc, w_in, ALL_PARTS if need_ctx else CTX_STATE_PARTS)

    kc, vc, gc = gla_kvg(pc, w_g2, b_g2)
    kl, vl, gl = gla_kvg(pl, w_g2, b_g2)
    if need_ctx:
        s0 = jnp.zeros((2, b, GLA_HEADS, GLA_DK, GLA_DV), jnp.float32)
        oc_dirs, s_ctx = gla_chunked(bidir(gla_q(pc['gla_q'])), kc, vc, gc, s0)
    else:
        s_ctx = gla_final_state(kc, vc, gc)
    ol_dirs, _ = gla_chunked(bidir(gla_q(pl['gla_q'])), kl, vl, gl, s_ctx)

    kl_m, vl_m = mla_kv(pl, kvn_g, w_ukv, tabs)
    kc_m, vc_m = mla_kv(pc, kvn_g, w_ukv, None)
    ql_m = mla_q(pl['mla_cq'], qn_g, w_uq, tabs)
    keys = jnp.concatenate([kl_m, kc_m], axis=1)
    vals = jnp.concatenate([vl_m, vc_m], axis=1)
    ml = block_attention(ql_m, keys, vals)

    yl = jnp.concatenate([gla_out(ol_dirs, pl['gla_r'], gla_g), ml], axis=-1) @ w_out
    if need_ctx:
        qc_m = mla_q(pc['mla_cq'], qn_g, w_uq, None)
        mc = dense_attn(qc_m, kc_m, vc_m).reshape(b, hc.shape[1], MLA_V_W)
        yc = jnp.concatenate([gla_out(oc_dirs, pc['gla_r'], gla_g), mc], axis=-1) @ w_out
        return yl, yc
    return yl, None


def swiglu(h, w_gate, w_up, w_down):
    return (jax.nn.silu(h @ w_gate) * (h @ w_up)) @ w_down


def moe_swiglu(h, w_router, e_gate, e_up, e_down):
    logits = jnp.einsum('btd,de->bte', h, w_router).astype(jnp.float32)
    top_v, top_i = lax.top_k(logits, TOP_K)
    wts = jax.nn.softmax(top_v, axis=-1)
    comb = jnp.einsum('btk,btke->bte', wts, jax.nn.one_hot(top_i, N_EXPERTS, dtype=jnp.float32)).astype(h.dtype)
    out = jnp.zeros_like(h)
    for e in range(N_EXPERTS):
        out = out + comb[..., e:e + 1] * swiglu(h, e_gate[e], e_up[e], e_down[e])
    return out


def channel_mix(h, i, ffn_w_gate, ffn_w_up, ffn_w_down, router_w, exp_w_gate, exp_w_up, exp_w_down):
    j = i // 2
    if i % 2 == 0:
        return swiglu(h, ffn_w_gate[j], ffn_w_up[j], ffn_w_down[j])
    return moe_swiglu(h, router_w[j], exp_w_gate[j], exp_w_up[j], exp_w_down[j])


def setup_inputs(seed: int = 0) -> dict:
    key = jax.random.key(seed)
    ks = iter(jax.random.split(key, 32))
    nrm = lambda shape, scale: jax.random.normal(next(ks), shape, jnp.float32) * scale
    gain = lambda shape: 1.0 + 0.02 * jax.random.normal(next(ks), shape, jnp.float32)
    D = D_MODEL
    return {
        'x': nrm((BATCH, SEQ, D), 1.0),
        'c': nrm((BATCH, D), 1.0),
        'ctx': nrm((BATCH, CTX_LEN, D), 1.0),
        'c_ctx': nrm((D,), 1.0),
        'w_mod': nrm((DEPTH, D, 6 * D), 0.5 * D ** -0.5),
        'b_mod': nrm((DEPTH, 6 * D), 0.02),
        'ln1_g': gain((DEPTH, D)),
        'ln2_g': gain((DEPTH, D)),
        'w_in': nrm((DEPTH, D, D_IN), D ** -0.5),
        'w_gla_g2': nrm((DEPTH, 2, GLA_GATE_RANK, GLA_QK_W), GLA_GATE_RANK ** -0.5),
        'b_gla_g2': nrm((DEPTH, 2, GLA_QK_W), 0.1),
        'gla_norm_g': gain((DEPTH, GLA_DV)),
        'mla_q_norm_g': gain((DEPTH, MLA_Q_RANK)),
        'w_uq': nrm((DEPTH, MLA_Q_RANK, MLA_HEADS * (MLA_NOPE + MLA_ROPE)), MLA_Q_RANK ** -0.5),
        'mla_kv_norm_g': gain((DEPTH, MLA_KV_RANK)),
        'w_ukv': nrm((DEPTH, MLA_KV_RANK, MLA_HEADS * (MLA_NOPE + MLA_V)), MLA_KV_RANK ** -0.5),
        'w_out': nrm((DEPTH, D_MIX, D), D_MIX ** -0.5),
        'ffn_w_gate': nrm((N_DENSE, D, D_FF_DENSE), D ** -0.5),
        'ffn_w_up': nrm((N_DENSE, D, D_FF_DENSE), D ** -0.5),
        'ffn_w_down': nrm((N_DENSE, D_FF_DENSE, D), D_FF_DENSE ** -0.5),
        'router_w': nrm((N_MOE, D, N_EXPERTS), D ** -0.5),
        'exp_w_gate': nrm((N_MOE, N_EXPERTS, D, D_FF_EXPERT), D ** -0.5),
        'exp_w_up': nrm((N_MOE, N_EXPERTS, D, D_FF_EXPERT), D ** -0.5),
        'exp_w_down': nrm((N_MOE, N_EXPERTS, D_FF_EXPERT, D), D_FF_EXPERT ** -0.5),
        'final_norm_g': gain((D,)),
    }


def reference(x, c, ctx, c_ctx, w_mod, b_mod, ln1_g, ln2_g, w_in, w_gla_g2, b_gla_g2, gla_norm_g,
              mla_q_norm_g, w_uq, mla_kv_norm_g, w_ukv, w_out, ffn_w_gate, ffn_w_up, ffn_w_down,
              router_w, exp_w_gate, exp_w_up, exp_w_down, final_norm_g):
    tabs = axial_rope_tables(x.shape[1])
    xl, xc = x, ctx
    for i in range(DEPTH):
        need_ctx = i < DEPTH - 1
        mod_l = (jax.nn.silu(c) @ w_mod[i] + b_mod[i])[:, None, :]
        mod_c = (jax.nn.silu(c_ctx) @ w_mod[i] + b_mod[i])[None, None, :]
        sh1, sc1, g1, sh2, sc2, g2 = jnp.split(mod_l, 6, axis=-1)
        ch1, cs1, cg1, ch2, cs2, cg2 = jnp.split(mod_c, 6, axis=-1)

        hl = rmsnorm(xl, ln1_g[i]) * (1.0 + sc1) + sh1
        hc = rmsnorm(xc, ln1_g[i]) * (1.0 + cs1) + ch1
        yl, yc = token_mixing(hl, hc, w_in[i], w_gla_g2[i], b_gla_g2[i], gla_norm_g[i],
                              mla_q_norm_g[i], w_uq[i], mla_kv_norm_g[i], w_ukv[i], w_out[i],
                              tabs, need_ctx)
        xl = xl + g1 * yl

        hl2 = rmsnorm(xl, ln2_g[i]) * (1.0 + sc2) + sh2
        xl = xl + g2 * channel_mix(hl2, i, ffn_w_gate, ffn_w_up, ffn_w_down,
                                   router_w, exp_w_gate, exp_w_up, exp_w_down)
        if need_ctx:
            xc = xc + cg1 * yc
            hc2 = rmsnorm(xc, ln2_g[i]) * (1.0 + cs2) + ch2
            xc = xc + cg2 * channel_mix(hc2, i, ffn_w_gate, ffn_w_up, ffn_w_down,
                                        router_w, exp_w_gate, exp_w_up, exp_w_down)
    return rmsnorm(xl, final_norm_g)
```

```python
import functools

import numpy as np
import jax
import jax.numpy as jnp
from jax import lax
from jax.experimental import pallas as pl
from jax.experimental.pallas import tpu as pltpu

F32 = jnp.float32
BF16 = jnp.bfloat16

D_MODEL = 1024
EPS = 1e-6
GRID_W = 64

GLA_HEADS = 4
GLA_DK = 64
GLA_DV = 128
GLA_GATE_RANK = 16
GLA_GATE_NORM = 16.0
GLA_CHUNK = 64
GLA_QK_W = GLA_HEADS * GLA_DK
GLA_V_W = GLA_HEADS * GLA_DV
GLA_EXP_CLAMP = 80.0

MLA_HEADS = 4
MLA_NOPE = 128
MLA_ROPE = 64
MLA_V = 128
MLA_QK = MLA_NOPE + MLA_ROPE
MLA_Q_RANK = 256
MLA_KV_RANK = 128
MLA_SCALE = MLA_QK ** -0.5
MLA_V_W = MLA_HEADS * MLA_V
ROPE_BASE = 10000.0

N_EXPERTS = 8
LANES = 128

P_Q, P_K, P_V, P_R, P_CQ, P_CKV, P_MISC = 0, 256, 512, 1024, 1536, 1792, 1920
P_WIDTH = 2048
MISC_KR, MISC_GF, MISC_GB = 0, 64, 80

VMEM_LIMIT = 56 * 1024 * 1024


def _cparams(sem):
    return pltpu.CompilerParams(dimension_semantics=sem, vmem_limit_bytes=VMEM_LIMIT)


def _rms(x, g):
    return x * lax.rsqrt(jnp.mean(x * x, axis=-1, keepdims=True) + EPS) * g


def _silu(x):
    return x / (1.0 + jnp.exp(-x))


def _dot(a, b):
    return jnp.dot(a, b, preferred_element_type=F32)


def _dot_nt(a, b):
    return lax.dot_general(a, b, (((1,), (1,)), ((), ())), preferred_element_type=F32)


def _dot_tn(a, b):
    return lax.dot_general(a, b, (((0,), (0,)), ((), ())), preferred_element_type=F32)


def _mod_kernel(c_ref, w_ref, b_ref, o_ref):
    s = _silu(c_ref[...]).astype(BF16)
    o_ref[0] = _dot(s, w_ref[0].astype(BF16)) + b_ref[0]


def _modulation(cvec, w_mod, b_mod):
    depth, d, n = w_mod.shape
    tn = 1536
    return pl.pallas_call(
        _mod_kernel,
        grid=(depth, n // tn),
        in_specs=[
            pl.BlockSpec((8, d), lambda l, j: (0, 0)),
            pl.BlockSpec((1, d, tn), lambda l, j: (l, 0, j)),
            pl.BlockSpec((1, 1, tn), lambda l, j: (l, 0, j)),
        ],
        out_specs=pl.BlockSpec((1, 8, tn), lambda l, j: (l, 0, j)),
        out_shape=jax.ShapeDtypeStruct((depth, 8, n), F32),
        compiler_params=_cparams(("arbitrary", "arbitrary")),
        name="modulation",
    )(cvec, w_mod, b_mod.reshape(depth, 1, n))


def _inproj_kernel(x_ref, mod_ref, g_ref, w_ref, o_ref):
    m = mod_ref[0]
    h = _rms(x_ref[...], g_ref[...]) * (1.0 + m[1:2]) + m[0:1]
    o_ref[...] = _dot(h.astype(BF16), w_ref[...]).astype(BF16)


def _inproj(x, mods, row_fn, ln_g, w, tm):
    t, d = x.shape
    return pl.pallas_call(
        _inproj_kernel,
        grid=(t // tm,),
        in_specs=[
            pl.BlockSpec((tm, d), lambda i: (i, 0)),
            pl.BlockSpec((1, 6, d), lambda i: (row_fn(i), 0, 0)),
            pl.BlockSpec((1, d), lambda i: (0, 0)),
            pl.BlockSpec((d, P_WIDTH), lambda i: (0, 0)),
        ],
        out_specs=pl.BlockSpec((tm, P_WIDTH), lambda i: (i, 0)),
        out_shape=jax.ShapeDtypeStruct((t, P_WIDTH), BF16),
        compiler_params=_cparams(("arbitrary",)),
        name="inproj",
    )(x, mods, ln_g, w)


def _gla_kernel(q_ref, k_ref, v_ref, misc_ref, wg_ref, bg_ref, tri_ref, s0_ref,
                o_ref, sfin_ref, s_scr, *, reverse, n_chunks):
    blk = pl.program_id(1)
    c_len = GLA_CHUNK

    @pl.when(blk == 0)
    def _():
        s_scr[...] = s0_ref[0]

    pre = _dot(misc_ref[...], wg_ref[...]) + bg_ref[...]
    g = (jnp.minimum(pre, 0.0) - jnp.log(1.0 + jnp.exp(-jnp.abs(pre)))) * (1.0 / GLA_GATE_NORM)
    g_hi = g.astype(BF16)
    g_lo = (g - g_hi.astype(F32)).astype(BF16)
    tri = tri_ref[...]
    cum = _dot(tri, g_hi) + _dot(tri, g_lo)

    lane = lax.broadcasted_iota(jnp.int32, (c_len, GLA_QK_W), 1)
    head_masks = [(lane >= h * GLA_DK) & (lane < (h + 1) * GLA_DK) for h in range(GLA_HEADS)]
    row = lax.broadcasted_iota(jnp.int32, (GLA_HEADS * c_len, c_len), 0) % c_len
    col = lax.broadcasted_iota(jnp.int32, (GLA_HEADS * c_len, c_len), 1)
    pair_mask = (col >= row) if reverse else (col <= row)
    ones = jnp.ones((c_len, GLA_DV), BF16)

    def stack_heads(a):
        return jnp.concatenate([jnp.where(mk, a, 0.0) for mk in head_masks], axis=0).astype(BF16)

    order = range(n_chunks - 1, -1, -1) if reverse else range(n_chunks)
    for c in order:
        sl = slice(c * c_len, (c + 1) * c_len)
        xc = cum[sl]
        tot = xc[0:1] if reverse else xc[c_len - 1:c_len]
        ref = xc[c_len // 2:c_len // 2 + 1]
        qc = q_ref[sl, :].astype(F32) * (GLA_DK ** -0.5)
        kc = k_ref[sl, :].astype(F32)
        vc = v_ref[sl, :]
        q_mid = qc * jnp.exp(jnp.minimum(xc - ref, GLA_EXP_CLAMP))
        k_mid = (kc * jnp.exp(jnp.minimum(ref - xc, GLA_EXP_CLAMP))).astype(BF16)
        q_dec = qc * jnp.exp(xc)
        k_dec = (kc * jnp.exp(tot - xc)).astype(BF16)

        attn = _dot_nt(stack_heads(q_mid), k_mid)
        attn = jnp.where(pair_mask, attn, 0.0).astype(BF16)
        s_prev = s_scr[...]
        o_inter = _dot(stack_heads(q_dec), s_prev.astype(BF16))
        for h in range(GLA_HEADS):
            rs = slice(h * c_len, (h + 1) * c_len)
            vs = slice(h * GLA_DV, (h + 1) * GLA_DV)
            o_h = o_inter[rs] + _dot(attn[rs], vc[:, vs])
            o_ref[sl, vs] = o_h.astype(BF16)

        kv_full = _dot_tn(k_dec, vc)
        kv = jnp.concatenate(
            [kv_full[h * GLA_DK:(h + 1) * GLA_DK, h * GLA_DV:(h + 1) * GLA_DV]
             for h in range(GLA_HEADS)], axis=0)
        tot_col = _dot_tn(g_hi[sl], ones) + _dot_tn(g_lo[sl], ones)
        s_scr[...] = s_prev * jnp.exp(tot_col) + kv

    @pl.when(blk == pl.num_programs(1) - 1)
    def _():
        sfin_ref[0] = s_scr[...]


def _block_diag_tri(n_chunks, upper):
    c = GLA_CHUNK
    t = np.triu(np.ones((c, c), np.float32)) if upper else np.tril(np.ones((c, c), np.float32))
    return jnp.asarray(np.kron(np.eye(n_chunks, dtype=np.float32), t), dtype=BF16)


def _gla(p, wg, bg, s0, *, batch, reverse, cb):
    t_all = p.shape[0]
    nblk = t_all // batch // cb
    n_chunks = cb // GLA_CHUNK

    def tok(b, i):
        return b * nblk + ((nblk - 1 - i) if reverse else i)

    kern = functools.partial(_gla_kernel, reverse=reverse, n_chunks=n_chunks)
    return pl.pallas_call(
        kern,
        grid=(batch, nblk),
        in_specs=[
            pl.BlockSpec((cb, GLA_QK_W), lambda b, i: (tok(b, i), P_Q // GLA_QK_W)),
            pl.BlockSpec((cb, GLA_QK_W), lambda b, i: (tok(b, i), P_K // GLA_QK_W)),
            pl.BlockSpec((cb, GLA_V_W), lambda b, i: (tok(b, i), P_V // GLA_V_W)),
            pl.BlockSpec((cb, LANES), lambda b, i: (tok(b, i), P_MISC // LANES)),
            pl.BlockSpec((LANES, GLA_QK_W), lambda b, i: (0, 0)),
            pl.BlockSpec((1, GLA_QK_W), lambda b, i: (0, 0)),
            pl.BlockSpec((cb, cb), lambda b, i: (0, 0)),
            pl.BlockSpec((1, GLA_QK_W, GLA_DV), lambda b, i: (b, 0, 0)),
        ],
        out_specs=[
            pl.BlockSpec((cb, GLA_V_W), lambda b, i: (tok(b, i), 0)),
            pl.BlockSpec((1, GLA_QK_W, GLA_DV), lambda b, i: (b, 0, 0)),
        ],
        out_shape=[
            jax.ShapeDtypeStruct((t_all, GLA_V_W), BF16),
            jax.ShapeDtypeStruct((batch, GLA_QK_W, GLA_DV), F32),
        ],
        scratch_shapes=[pltpu.VMEM((GLA_QK_W, GLA_DV), F32)],
        compiler_params=_cparams(("arbitrary", "arbitrary")),
        name="gla_bwd" if reverse else "gla_fwd",
    )(p, p, p, p, wg, bg, _block_diag_tri(n_chunks, reverse), s0)


def _mlaprep_kernel(cq_ref, ckv_ref, misc_ref, cos_ref, sin_ref, qg_ref, kvg_ref,
                    wqn_ref, wqr_ref, wqs_ref, wknt_ref, wv_ref, perm_ref, eye_ref,
                    q_ref, kt_ref, v_ref):
    cos = cos_ref[...]
    sin = sin_ref[...]
    cqn = _rms(cq_ref[...].astype(F32), qg_ref[...]).astype(BF16)
    qn = _dot(cqn, wqn_ref[...])
    qr = _dot(cqn, wqr_ref[...])
    qs = _dot(cqn, wqs_ref[...])
    for h in range(MLA_HEADS):
        ls = slice(h * LANES, (h + 1) * LANES)
        q_ref[0, h, :, 0:MLA_NOPE] = (qn[:, ls] * MLA_SCALE).astype(BF16)
        rot = qr[:, ls] * cos + qs[:, ls] * sin
        q_ref[0, h, :, MLA_NOPE:MLA_QK] = (rot[:, 0:MLA_ROPE] * MLA_SCALE).astype(BF16)

    ckvn = _rms(ckv_ref[...].astype(F32), kvg_ref[...]).astype(BF16)
    knt = _dot_nt(wknt_ref[...], ckvn)
    vv = _dot(ckvn, wv_ref[...])
    misc = misc_ref[...]
    kr = misc.astype(F32) * cos + _dot(misc, perm_ref[...]) * sin
    krt = _dot_nt(eye_ref[...], kr.astype(BF16)).astype(BF16)
    for h in range(MLA_HEADS):
        kt_ref[0, h, 0, 0:MLA_NOPE, :] = knt[h * MLA_NOPE:(h + 1) * MLA_NOPE].astype(BF16)
        kt_ref[0, h, 0, MLA_NOPE:MLA_QK, :] = krt
        v_ref[0, h] = vv[:, h * MLA_V:(h + 1) * MLA_V].astype(BF16)


def _mlaprep(p, cos, sin, qg, kvg, wts, *, batch, tm):
    t_all = p.shape[0]
    t = t_all // batch
    nb = t // tm
    ntab = cos.shape[0] // tm
    wqn, wqr, wqs, wknt, wv, perm, eye = wts
    full = lambda a: pl.BlockSpec(a.shape, lambda b, i: (0,) * a.ndim)
    return pl.pallas_call(
        _mlaprep_kernel,
        grid=(batch, nb),
        in_specs=[
            pl.BlockSpec((tm, MLA_Q_RANK), lambda b, i: (b * nb + i, P_CQ // MLA_Q_RANK)),
            pl.BlockSpec((tm, MLA_KV_RANK), lambda b, i: (b * nb + i, P_CKV // MLA_KV_RANK)),
            pl.BlockSpec((tm, LANES), lambda b, i: (b * nb + i, P_MISC // LANES)),
            pl.BlockSpec((tm, LANES), lambda b, i: (i % ntab, 0)),
            pl.BlockSpec((tm, LANES), lambda b, i: (i % ntab, 0)),
            full(qg), full(kvg), full(wqn), full(wqr), full(wqs), full(wknt), full(wv),
            full(perm), full(eye),
        ],
        out_specs=[
            pl.BlockSpec((1, MLA_HEADS, tm, MLA_QK), lambda b, i: (b, 0, i, 0)),
            pl.BlockSpec((1, MLA_HEADS, 1, MLA_QK, tm), lambda b, i: (b, 0, i, 0, 0)),
            pl.BlockSpec((1, MLA_HEADS, tm, MLA_V), lambda b, i: (b, 0, i, 0)),
        ],
        out_shape=[
            jax.ShapeDtypeStruct((batch, MLA_HEADS, t, MLA_QK), BF16),
            jax.ShapeDtypeStruct((batch, MLA_HEADS, nb, MLA_QK, tm), BF16),
            jax.ShapeDtypeStruct((batch, MLA_HEADS, t, MLA_V), BF16),
        ],
        compiler_params=_cparams(("arbitrary", "arbitrary")),
        name="mlaprep",
    )(p, p, p, cos, sin, qg, kvg, wqn, wqr, wqs, wknt, wv, perm, eye)


def _attn_kernel(*refs, n_seg):
    q_ref = refs[0]
    kt_refs = refs[1:1 + 2 * n_seg:2]
    v_refs = refs[2:2 + 2 * n_seg:2]
    o_ref = refs[1 + 2 * n_seg]
    m_scr, l_scr, acc_scr = refs[2 + 2 * n_seg:]

    q = q_ref[0, 0]
    m_scr[...] = jnp.full(m_scr.shape, -jnp.inf, F32)
    l_scr[...] = jnp.zeros(l_scr.shape, F32)
    acc_scr[...] = jnp.zeros(acc_scr.shape, F32)

    for kt_ref, v_ref in zip(kt_refs, v_refs):
        n_blocks, tk = kt_ref.shape[2], kt_ref.shape[4]

        def step(j, carry, kt_ref=kt_ref, v_ref=v_ref, tk=tk):
            s = _dot(q, kt_ref[0, 0, j])
            m_prev = m_scr[...]
            m_new = jnp.maximum(m_prev, jnp.max(s, axis=1, keepdims=True))
            p = jnp.exp(s - m_new)
            alpha = jnp.exp(m_prev - m_new)
            l_scr[...] = alpha * l_scr[...] + jnp.sum(p, axis=1, keepdims=True)
            v_blk = v_ref[0, 0, pl.ds(pl.multiple_of(j * tk, tk), tk), :]
            acc_scr[...] = alpha * acc_scr[...] + _dot(p.astype(BF16), v_blk)
            m_scr[...] = m_new
            return carry

        lax.fori_loop(0, n_blocks, step, 0)

    o_ref[0] = (acc_scr[...] / l_scr[...]).astype(BF16)


def _attention(q, segs, *, tq):
    b, h, t, dqk = q.shape
    in_specs = [pl.BlockSpec((1, 1, tq, dqk), lambda bi, hi, qi: (bi, hi, qi, 0))]
    args = [q]
    for kt, v in segs:
        in_specs.append(pl.BlockSpec((1, 1) + kt.shape[2:], lambda bi, hi, qi: (bi, hi, 0, 0, 0)))
        in_specs.append(pl.BlockSpec((1, 1) + v.shape[2:], lambda bi, hi, qi: (bi, hi, 0, 0)))
        args += [kt, v]
    return pl.pallas_call(
        functools.partial(_attn_kernel, n_seg=len(segs)),
        grid=(b, h, t // tq),
        in_specs=in_specs,
        out_specs=pl.BlockSpec((1, tq, MLA_V), lambda bi, hi, qi: (bi, qi, hi)),
        out_shape=jax.ShapeDtypeStruct((b, t, h * MLA_V), BF16),
        scratch_shapes=[pltpu.VMEM((tq, 1), F32), pltpu.VMEM((tq, 1), F32),
                        pltpu.VMEM((tq, MLA_V), F32)],
        compiler_params=_cparams(("arbitrary", "arbitrary", "arbitrary")),
        name="mla_attention",
    )(*args)


def _outproj_kernel(*refs, with_router):
    (x_ref, of_ref, ob_ref, r_ref, mla_ref, mod_ref, gg_ref, wo_ref, ln2_ref) = refs[:9]
    if with_router:
        rwh_ref, rwl_ref, x1_ref, h2_ref, comb_ref = refs[9:]
    else:
        x1_ref, h2_ref = refs[9:]
    m = mod_ref[0]
    o = of_ref[...].astype(F32) + ob_ref[...].astype(F32)
    gg = gg_ref[...]
    y = jnp.concatenate(
        [_rms(o[:, h * GLA_DV:(h + 1) * GLA_DV], gg) for h in range(GLA_HEADS)], axis=1)
    mix = (y * _silu(r_ref[...].astype(F32))).astype(BF16)
    yo = _dot(mix, wo_ref[0:GLA_V_W, :]) + _dot(mla_ref[...], wo_ref[GLA_V_W:, :])
    x1 = x_ref[...] + m[2:3] * yo
    x1_ref[...] = x1
    h2 = _rms(x1, ln2_ref[...]) * (1.0 + m[4:5]) + m[3:4]
    h2_ref[...] = h2.astype(BF16)
    if with_router:
        h_hi = h2.astype(BF16)
        h_lo = (h2 - h_hi.astype(F32)).astype(BF16)
        logits = _dot(h_hi, rwh_ref[...]) + _dot(h_lo, rwh_ref[...]) + _dot(h_hi, rwl_ref[...])
        lane = lax.broadcasted_iota(jnp.int32, logits.shape, 1).astype(F32)
        neg = jnp.float32(-jnp.inf)
        logits = jnp.where(lane < N_EXPERTS, logits, neg)
        m1 = jnp.max(logits, axis=1, keepdims=True)
        i1 = jnp.min(jnp.where(logits == m1, lane, float(LANES)), axis=1, keepdims=True)
        rest = jnp.where(lane == i1, neg, logits)
        m2 = jnp.max(rest, axis=1, keepdims=True)
        i2 = jnp.min(jnp.where(rest == m2, lane, float(LANES)), axis=1, keepdims=True)
        e2 = jnp.exp(m2 - m1)
        w1 = 1.0 / (1.0 + e2)
        comb_ref[...] = jnp.where(lane == i1, w1, 0.0) + jnp.where(lane == i2, e2 * w1, 0.0)


def _outproj(x, o_f, o_b, p, mla, mods, row_fn, gg, wo, ln2, router, tm):
    t, d = x.shape
    with_router = router is not None
    full = lambda a: pl.BlockSpec(a.shape, lambda i: (0,) * a.ndim)
    in_specs = [
        pl.BlockSpec((tm, d), lambda i: (i, 0)),
        pl.BlockSpec((tm, GLA_V_W), lambda i: (i, 0)),
        pl.BlockSpec((tm, GLA_V_W), lambda i: (i, 0)),
        pl.BlockSpec((tm, GLA_V_W), lambda i: (i, P_R // GLA_V_W)),
        pl.BlockSpec((tm, MLA_V_W), lambda i: (i, 0)),
        pl.BlockSpec((1, 6, d), lambda i: (row_fn(i), 0, 0)),
        full(gg), full(wo), full(ln2),
    ]
    args = [x, o_f, o_b, p, mla, mods, gg, wo, ln2]
    out_specs = [pl.BlockSpec((tm, d), lambda i: (i, 0)), pl.BlockSpec((tm, d), lambda i: (i, 0))]
    out_shape = [jax.ShapeDtypeStruct((t, d), F32), jax.ShapeDtypeStruct((t, d), BF16)]
    if with_router:
        in_specs += [full(router[0]), full(router[1])]
        args += list(router)
        out_specs.append(pl.BlockSpec((tm, LANES), lambda i: (i, 0)))
        out_shape.append(jax.ShapeDtypeStruct((t, LANES), F32))
    return pl.pallas_call(
        functools.partial(_outproj_kernel, with_router=with_router),
        grid=(t // tm,),
        in_specs=in_specs,
        out_specs=out_specs,
        out_shape=out_shape,
        compiler_params=_cparams(("arbitrary",)),
        name="outproj",
    )(*args)


def _ffn_kernel(*refs, with_comb, final_norm):
    h_ref, x1_ref, mod_ref = refs[:3]
    k = 3
    comb_ref = fin_ref = None
    if with_comb:
        comb_ref = refs[k]
        k += 1
    wg_ref, wu_ref, wd_ref = refs[k:k + 3]
    k += 3
    if final_norm:
        fin_ref = refs[k]
        k += 1
    o_ref, acc = refs[k:]
    e = pl.program_id(1)
    f = pl.program_id(2)

    @pl.when((e == 0) & (f == 0))
    def _():
        acc[...] = jnp.zeros(acc.shape, F32)

    h = h_ref[...]
    a = _dot(h, wg_ref[0])
    u = _dot(h, wu_ref[0])
    act = _silu(a) * u
    if with_comb:
        comb = comb_ref[...]
        lane = lax.broadcasted_iota(jnp.int32, comb.shape, 1)
        act = act * jnp.sum(jnp.where(lane == e, comb, 0.0), axis=1, keepdims=True)
    acc[...] += _dot(act.astype(BF16), wd_ref[0])

    @pl.when((e == pl.num_programs(1) - 1) & (f == pl.num_programs(2) - 1))
    def _():
        x2 = x1_ref[...] + mod_ref[0][5:6] * acc[...]
        if final_norm:
            x2 = _rms(x2, fin_ref[...])
        o_ref[...] = x2


def _ffn(h2, x1, mods, row_fn, comb, wg, wu, wd, fin_g, tm, tf):
    t, d = x1.shape
    n_e, _, ff = wg.shape
    with_comb = comb is not None
    final_norm = fin_g is not None
    in_specs = [
        pl.BlockSpec((tm, d), lambda i, e, f: (i, 0)),
        pl.BlockSpec((tm, d), lambda i, e, f: (i, 0)),
        pl.BlockSpec((1, 6, d), lambda i, e, f: (row_fn(i), 0, 0)),
    ]
    args = [h2, x1, mods]
    if with_comb:
        in_specs.append(pl.BlockSpec((tm, LANES), lambda i, e, f: (i, 0)))
        args.append(comb)
    in_specs += [
        pl.BlockSpec((1, d, tf), lambda i, e, f: (e, 0, f)),
        pl.BlockSpec((1, d, tf), lambda i, e, f: (e, 0, f)),
        pl.BlockSpec((1, tf, d), lambda i, e, f: (e, f, 0)),
    ]
    args += [wg, wu, wd]
    if final_norm:
        in_specs.append(pl.BlockSpec((1, d), lambda i, e, f: (0, 0)))
        args.append(fin_g)
    return pl.pallas_call(
        functools.partial(_ffn_kernel, with_comb=with_comb, final_norm=final_norm),
        grid=(t // tm, n_e, ff // tf),
        in_specs=in_specs,
        out_specs=pl.BlockSpec((tm, d), lambda i, e, f: (i, 0)),
        out_shape=jax.ShapeDtypeStruct((t, d), F32),
        scratch_shapes=[pltpu.VMEM((tm, d), F32)],
        compiler_params=_cparams(("arbitrary", "arbitrary", "arbitrary")),
        name="ffn",
    )(*args)


def _rope_partner():
    j = np.arange(MLA_ROPE)
    return np.where((j % 32) < 16, j + 16, j - 16)


def _prep_in_weight(w):
    d = w.shape[0]
    cols = [w[:, 0:1024], w[:, 1056:1568], w[:, 1568:1824], w[:, 1824:1952], w[:, 1952:2016],
            w[:, 1024:1056], jnp.zeros((d, P_WIDTH - 2016), w.dtype)]
    return jnp.concatenate(cols, axis=1).astype(BF16)


def _prep_gate_weight(w_g2, b_g2):
    outs = []
    for z, off in ((0, MISC_GF), (1, MISC_GB)):
        wz = jnp.zeros((LANES, GLA_QK_W), F32).at[off:off + GLA_GATE_RANK].set(w_g2[z])
        outs.append((wz.astype(BF16), b_g2[z].reshape(1, GLA_QK_W)))
    return outs


def _prep_mla_weights(w_uq, w_ukv):
    partner = _rope_partner()
    wq = w_uq.reshape(MLA_Q_RANK, MLA_HEADS, MLA_QK)
    wqn = wq[:, :, :MLA_NOPE].reshape(MLA_Q_RANK, MLA_HEADS * MLA_NOPE)
    rope = wq[:, :, MLA_NOPE:]
    pad = jnp.zeros((MLA_Q_RANK, MLA_HEADS, LANES - MLA_ROPE), w_uq.dtype)
    wqr = jnp.concatenate([rope, pad], axis=2).reshape(MLA_Q_RANK, MLA_HEADS * LANES)
    wqs = jnp.concatenate([rope[:, :, partner], pad], axis=2).reshape(MLA_Q_RANK, MLA_HEADS * LANES)
    wkv = w_ukv.reshape(MLA_KV_RANK, MLA_HEADS, MLA_NOPE + MLA_V)
    wknt = wkv[:, :, :MLA_NOPE].reshape(MLA_KV_RANK, MLA_HEADS * MLA_NOPE).T
    wv = wkv[:, :, MLA_NOPE:].reshape(MLA_KV_RANK, MLA_HEADS * MLA_V)
    perm = np.zeros((LANES, LANES), np.float32)
    perm[partner, np.arange(MLA_ROPE)] = 1.0
    eye = np.eye(MLA_ROPE, LANES, dtype=np.float32)
    return (wqn.astype(BF16), wqr.astype(BF16), wqs.astype(BF16), wknt.astype(BF16),
            wv.astype(BF16), jnp.asarray(perm, BF16), jnp.asarray(eye, BF16))


def _rope_tables(n_tok):
    rows = n_tok // GRID_W
    row = jnp.repeat(jnp.arange(rows, dtype=F32), GRID_W)
    col = jnp.tile(jnp.arange(GRID_W, dtype=F32), rows)
    nfreq = MLA_ROPE // 4
    inv = ROPE_BASE ** (-jnp.arange(nfreq, dtype=F32) / nfreq)
    ar = row[:, None] * inv
    ac = col[:, None] * inv
    zero = jnp.zeros((n_tok, LANES - MLA_ROPE), F32)
    cos = jnp.concatenate([jnp.cos(ar), jnp.cos(ar), jnp.cos(ac), jnp.cos(ac), zero], axis=1)
    sin = jnp.concatenate([-jnp.sin(ar), jnp.sin(ar), -jnp.sin(ac), jnp.sin(ac), zero], axis=1)
    return cos, sin


def _identity_tables(n_tok):
    cos = jnp.concatenate([jnp.ones((n_tok, MLA_ROPE), F32),
                           jnp.zeros((n_tok, LANES - MLA_ROPE), F32)], axis=1)
    return cos, jnp.zeros((n_tok, LANES), F32)


def _pick_tile(n, pref):
    t = min(n, pref)
    while n % t:
        t //= 2
    return t


def _pick_ff_tile(ff):
    best = LANES
    for m in range(1, ff // LANES + 1):
        if ff % (m * LANES) == 0 and m * LANES <= 1408:
            best = m * LANES
    return best


@jax.jit
def _forward(x, c, ctx, c_ctx, w_mod, b_mod, ln1_g, ln2_g, w_in, w_gla_g2, b_gla_g2, gla_norm_g,
             mla_q_norm_g, w_uq, mla_kv_norm_g, w_ukv, w_out, ffn_w_gate, ffn_w_up, ffn_w_down,
             router_w, exp_w_gate, exp_w_up, exp_w_down, final_norm_g):
    batch, seq, d = x.shape
    n_ctx = ctx.shape[1]
    depth = w_mod.shape[0]

    cvec = jnp.zeros((8, d), F32).at[:batch].set(c).at[batch].set(c_ctx)
    mods_all = _modulation(cvec, w_mod, b_mod).reshape(depth, 8, 6, d)

    xl = x.reshape(batch * seq, d)
    xc = ctx.reshape(batch * n_ctx, d)

    tm_l = _pick_tile(seq, 512)
    tm_c = _pick_tile(n_ctx, 256)
    cb_l = _pick_tile(seq, 256)
    cb_c = _pick_tile(n_ctx, 256)
    row_l = lambda tm: (lambda i: i // (seq // tm))
    row_c = lambda i: batch

    rope_l = _rope_tables(seq)
    rope_c = _identity_tables(tm_c)
    zero_state = jnp.zeros((batch, GLA_QK_W, GLA_DV), F32)

    for i in range(depth):
        need_ctx = i < depth - 1
        last = i == depth - 1
        mods = mods_all[i]
        ln1 = ln1_g[i].reshape(1, d)
        ln2 = ln2_g[i].reshape(1, d)
        w_in_r = _prep_in_weight(w_in[i])
        gates = _prep_gate_weight(w_gla_g2[i], b_gla_g2[i])
        mla_w = _prep_mla_weights(w_uq[i], w_ukv[i])
        qg = mla_q_norm_g[i].reshape(1, MLA_Q_RANK)
        kvg = mla_kv_norm_g[i].reshape(1, MLA_KV_RANK)
        gg = gla_norm_g[i].reshape(1, GLA_DV)
        wo = w_out[i].astype(BF16)

        p_l = _inproj(xl, mods, row_l(tm_l), ln1, w_in_r, tm_l)
        p_c = _inproj(xc, mods, row_c, ln1, w_in_r, tm_c)

        oc_f, sc_f = _gla(p_c, *gates[0], zero_state, batch=batch, reverse=False, cb=cb_c)
        oc_b, sc_b = _gla(p_c, *gates[1], zero_state, batch=batch, reverse=True, cb=cb_c)
        ol_f, _ = _gla(p_l, *gates[0], sc_f, batch=batch, reverse=False, cb=cb_l)
        ol_b, _ = _gla(p_l, *gates[1], sc_b, batch=batch, reverse=True, cb=cb_l)

        q_l, kt_l, v_l = _mlaprep(p_l, *rope_l, qg, kvg, mla_w, batch=batch, tm=tm_l)
        q_c, kt_c, v_c = _mlaprep(p_c, *rope_c, qg, kvg, mla_w, batch=batch, tm=tm_c)
        m_l = _attention(q_l, [(kt_l, v_l), (kt_c, v_c)], tq=tm_l).reshape(batch * seq, MLA_V_W)

        if i % 2 == 0:
            j = i // 2
            router = None
            wg = ffn_w_gate[j][None].astype(BF16)
            wu = ffn_w_up[j][None].astype(BF16)
            wd = ffn_w_down[j][None].astype(BF16)
        else:
            j = i // 2
            rw = jnp.zeros((d, LANES), F32).at[:, :N_EXPERTS].set(router_w[j])
            rw_hi = rw.astype(BF16)
            router = (rw_hi, (rw - rw_hi.astype(F32)).astype(BF16))
            wg = exp_w_gate[j].astype(BF16)
            wu = exp_w_up[j].astype(BF16)
            wd = exp_w_down[j].astype(BF16)
        tf = _pick_ff_tile(wg.shape[2])
        fin = final_norm_g.reshape(1, d) if last else None

        outs = _outproj(xl, ol_f, ol_b, p_l, m_l, mods, row_l(tm_l), gg, wo, ln2, router, tm_l)
        comb = outs[2] if router is not None else None
        tm_f = _pick_tile(seq, 512)
        xl = _ffn(outs[1], outs[0], mods, row_l(tm_f), comb, wg, wu, wd, fin, tm_f, tf)

        if need_ctx:
            m_c = _attention(q_c, [(kt_c, v_c)], tq=tm_c).reshape(batch * n_ctx, MLA_V_W)
            outs_c = _outproj(xc, oc_f, oc_b, p_c, m_c, mods, row_c, gg, wo, ln2, router, tm_c)
            comb_c = outs_c[2] if router is not None else None
            xc = _ffn(outs_c[1], outs_c[0], mods, row_c, comb_c, wg, wu, wd, None,
                      _pick_tile(batch * n_ctx, 512), tf)

    return xl.reshape(batch, seq, d)


def kernel(x, c, ctx, c_ctx, w_mod, b_mod, ln1_g, ln2_g, w_in, w_gla_g2, b_gla_g2, gla_norm_g,
           mla_q_norm_g, w_uq, mla_kv_norm_g, w_ukv, w_out, ffn_w_gate, ffn_w_up, ffn_w_down,
           router_w, exp_w_gate, exp_w_up, exp_w_down, final_norm_g):
    return _forward(x, c, ctx, c_ctx, w_mod, b_mod, ln1_g, ln2_g, w_in, w_gla_g2, b_gla_g2,
                    gla_norm_g, mla_q_norm_g, w_uq, mla_kv_norm_g, w_ukv, w_out, ffn_w_gate,
                    ffn_w_up, ffn_w_down, router_w, exp_w_gate, exp_w_up, exp_w_down, final_norm_g)
```

```python
import functools

import numpy as np
import jax
import jax.numpy as jnp
from jax import lax
from jax.experimental import pallas as pl
from jax.experimental.pallas import tpu as pltpu

F32 = jnp.float32
BF16 = jnp.bfloat16

D_MODEL = 1024
EPS = 1e-6
GRID_W = 64

GLA_HEADS = 4
GLA_DK = 64
GLA_DV = 128
GLA_GATE_RANK = 16
GLA_GATE_NORM = 16.0
GLA_CHUNK = 64
GLA_QK_W = GLA_HEADS * GLA_DK
GLA_V_W = GLA_HEADS * GLA_DV
GLA_EXP_CLAMP = 80.0

MLA_HEADS = 4
MLA_NOPE = 128
MLA_ROPE = 64
MLA_V = 128
MLA_QK = MLA_NOPE + MLA_ROPE
MLA_Q_RANK = 256
MLA_KV_RANK = 128
MLA_SCALE = MLA_QK ** -0.5
MLA_Q_SCALE = MLA_SCALE * 1.4426950408889634
MLA_V_W = MLA_HEADS * MLA_V
MLA_V_EXT = 2 * MLA_V
ROPE_BASE = 10000.0

N_EXPERTS = 8
LANES = 128

P_Q, P_K, P_V, P_R, P_CQ, P_CKV, P_MISC = 0, 256, 512, 1024, 1536, 1792, 1920
P_WIDTH = 2048
MISC_KR, MISC_GF, MISC_GB = 0, 64, 80

VMEM_LIMIT = 56 * 1024 * 1024


def _cparams(sem):
    return pltpu.CompilerParams(dimension_semantics=sem, vmem_limit_bytes=VMEM_LIMIT)


def _rms(x, g):
    return x * lax.rsqrt(jnp.mean(x * x, axis=-1, keepdims=True) + EPS) * g


def _silu(x):
    return x / (1.0 + jnp.exp(-x))


def _dot(a, b):
    return jnp.dot(a, b, preferred_element_type=F32)


def _dot_nt(a, b):
    return lax.dot_general(a, b, (((1,), (1,)), ((), ())), preferred_element_type=F32)


def _dot_tn(a, b):
    return lax.dot_general(a, b, (((0,), (0,)), ((), ())), preferred_element_type=F32)


def _mod_kernel(c_ref, w_ref, b_ref, o_ref):
    s = _silu(c_ref[...]).astype(BF16)
    o_ref[0] = _dot(s, w_ref[0].astype(BF16)) + b_ref[0]


def _modulation(cvec, w_mod, b_mod):
    depth, d, n = w_mod.shape
    tn = 1536
    return pl.pallas_call(
        _mod_kernel,
        grid=(depth, n // tn),
        in_specs=[
            pl.BlockSpec((8, d), lambda l, j: (0, 0)),
            pl.BlockSpec((1, d, tn), lambda l, j: (l, 0, j)),
            pl.BlockSpec((1, 1, tn), lambda l, j: (l, 0, j)),
        ],
        out_specs=pl.BlockSpec((1, 8, tn), lambda l, j: (l, 0, j)),
        out_shape=jax.ShapeDtypeStruct((depth, 8, n), F32),
        compiler_params=_cparams(("arbitrary", "arbitrary")),
        name="modulation",
    )(cvec, w_mod, b_mod.reshape(depth, 1, n))


def _inproj_kernel(x_ref, mod_ref, g_ref, w_ref, o_ref):
    m = mod_ref[0]
    h = _rms(x_ref[...], g_ref[...]) * (1.0 + m[1:2]) + m[0:1]
    o_ref[...] = _dot(h.astype(BF16), w_ref[...]).astype(BF16)


def _inproj(x, mods, row_fn, ln_g, w, tm):
    t, d = x.shape
    return pl.pallas_call(
        _inproj_kernel,
        grid=(t // tm,),
        in_specs=[
            pl.BlockSpec((tm, d), lambda i: (i, 0)),
            pl.BlockSpec((1, 6, d), lambda i: (row_fn(i), 0, 0)),
            pl.BlockSpec((1, d), lambda i: (0, 0)),
            pl.BlockSpec((d, P_WIDTH), lambda i: (0, 0)),
        ],
        out_specs=pl.BlockSpec((tm, P_WIDTH), lambda i: (i, 0)),
        out_shape=jax.ShapeDtypeStruct((t, P_WIDTH), BF16),
        compiler_params=_cparams(("arbitrary",)),
        name="inproj",
    )(x, mods, ln_g, w)


def _gla_kernel(q_ref, k_ref, v_ref, misc_ref, wg_ref, bg_ref, tri_ref, s0_ref,
                o_ref, sfin_ref, s_scr, *, reverse, n_chunks):
    blk = pl.program_id(1)
    c_len = GLA_CHUNK

    @pl.when(blk == 0)
    def _():
        s_scr[...] = s0_ref[0]

    pre = _dot(misc_ref[...], wg_ref[...]) + bg_ref[...]
    g = (jnp.minimum(pre, 0.0) - jnp.log(1.0 + jnp.exp(-jnp.abs(pre)))) * (1.0 / GLA_GATE_NORM)
    g_hi = g.astype(BF16)
    g_lo = (g - g_hi.astype(F32)).astype(BF16)
    tri = tri_ref[...]
    cum = _dot(tri, g_hi) + _dot(tri, g_lo)

    lane = lax.broadcasted_iota(jnp.int32, (c_len, GLA_QK_W), 1)
    head_masks = [(lane >= h * GLA_DK) & (lane < (h + 1) * GLA_DK) for h in range(GLA_HEADS)]
    row = lax.broadcasted_iota(jnp.int32, (GLA_HEADS * c_len, c_len), 0) % c_len
    col = lax.broadcasted_iota(jnp.int32, (GLA_HEADS * c_len, c_len), 1)
    pair_mask = (col >= row) if reverse else (col <= row)
    ones = jnp.ones((c_len, GLA_DV), BF16)

    def stack_heads(a):
        return jnp.concatenate([jnp.where(mk, a, 0.0) for mk in head_masks], axis=0).astype(BF16)

    order = range(n_chunks - 1, -1, -1) if reverse else range(n_chunks)
    for c in order:
        sl = slice(c * c_len, (c + 1) * c_len)
        xc = cum[sl]
        tot = xc[0:1] if reverse else xc[c_len - 1:c_len]
        ref = xc[c_len // 2:c_len // 2 + 1]
        qc = q_ref[sl, :].astype(F32) * (GLA_DK ** -0.5)
        kc = k_ref[sl, :].astype(F32)
        vc = v_ref[sl, :]
        q_mid = qc * jnp.exp(jnp.minimum(xc - ref, GLA_EXP_CLAMP))
        k_mid = (kc * jnp.exp(jnp.minimum(ref - xc, GLA_EXP_CLAMP))).astype(BF16)
        q_dec = qc * jnp.exp(xc)
        k_dec = (kc * jnp.exp(tot - xc)).astype(BF16)

        attn = _dot_nt(stack_heads(q_mid), k_mid)
        attn = jnp.where(pair_mask, attn, 0.0).astype(BF16)
        s_prev = s_scr[...]
        o_inter = _dot(stack_heads(q_dec), s_prev.astype(BF16))
        for h in range(GLA_HEADS):
            rs = slice(h * c_len, (h + 1) * c_len)
            vs = slice(h * GLA_DV, (h + 1) * GLA_DV)
            o_h = o_inter[rs] + _dot(attn[rs], vc[:, vs])
            o_ref[sl, vs] = o_h.astype(BF16)

        kv_full = _dot_tn(k_dec, vc)
        kv = jnp.concatenate(
            [kv_full[h * GLA_DK:(h + 1) * GLA_DK, h * GLA_DV:(h + 1) * GLA_DV]
             for h in range(GLA_HEADS)], axis=0)
        tot_col = _dot_tn(g_hi[sl], ones) + _dot_tn(g_lo[sl], ones)
        s_scr[...] = s_prev * jnp.exp(tot_col) + kv

    @pl.when(blk == pl.num_programs(1) - 1)
    def _():
        sfin_ref[0] = s_scr[...]


def _block_diag_tri(n_chunks, upper):
    c = GLA_CHUNK
    t = np.triu(np.ones((c, c), np.float32)) if upper else np.tril(np.ones((c, c), np.float32))
    return jnp.asarray(np.kron(np.eye(n_chunks, dtype=np.float32), t), dtype=BF16)


def _gla(p, wg, bg, s0, *, batch, reverse, cb):
    t_all = p.shape[0]
    nblk = t_all // batch // cb
    n_chunks = cb // GLA_CHUNK

    def tok(b, i):
        return b * nblk + ((nblk - 1 - i) if reverse else i)

    kern = functools.partial(_gla_kernel, reverse=reverse, n_chunks=n_chunks)
    return pl.pallas_call(
        kern,
        grid=(batch, nblk),
        in_specs=[
            pl.BlockSpec((cb, GLA_QK_W), lambda b, i: (tok(b, i), P_Q // GLA_QK_W)),
            pl.BlockSpec((cb, GLA_QK_W), lambda b, i: (tok(b, i), P_K // GLA_QK_W)),
            pl.BlockSpec((cb, GLA_V_W), lambda b, i: (tok(b, i), P_V // GLA_V_W)),
            pl.BlockSpec((cb, LANES), lambda b, i: (tok(b, i), P_MISC // LANES)),
            pl.BlockSpec((LANES, GLA_QK_W), lambda b, i: (0, 0)),
            pl.BlockSpec((1, GLA_QK_W), lambda b, i: (0, 0)),
            pl.BlockSpec((cb, cb), lambda b, i: (0, 0)),
            pl.BlockSpec((1, GLA_QK_W, GLA_DV), lambda b, i: (b, 0, 0)),
        ],
        out_specs=[
            pl.BlockSpec((cb, GLA_V_W), lambda b, i: (tok(b, i), 0)),
            pl.BlockSpec((1, GLA_QK_W, GLA_DV), lambda b, i: (b, 0, 0)),
        ],
        out_shape=[
            jax.ShapeDtypeStruct((t_all, GLA_V_W), BF16),
            jax.ShapeDtypeStruct((batch, GLA_QK_W, GLA_DV), F32),
        ],
        scratch_shapes=[pltpu.VMEM((GLA_QK_W, GLA_DV), F32)],
        compiler_params=_cparams(("arbitrary", "arbitrary")),
        name="gla_bwd" if reverse else "gla_fwd",
    )(p, p, p, p, wg, bg, _block_diag_tri(n_chunks, reverse), s0)


def _mlaprep_kernel(cq_ref, ckv_ref, misc_ref, cos_ref, sin_ref, qg_ref, kvg_ref,
                    wqn_ref, wqr_ref, wqs_ref, wknt_ref, wv_ref, perm_ref, eye_ref,
                    q_ref, kt_ref, v_ref):
    cos = cos_ref[...]
    sin = sin_ref[...]
    cqn = _rms(cq_ref[...].astype(F32), qg_ref[...]).astype(BF16)
    qn = _dot(cqn, wqn_ref[...])
    qr = _dot(cqn, wqr_ref[...])
    qs = _dot(cqn, wqs_ref[...])
    for h in range(MLA_HEADS):
        ls = slice(h * LANES, (h + 1) * LANES)
        q_ref[0, h, :, 0:MLA_NOPE] = (qn[:, ls] * MLA_Q_SCALE).astype(BF16)
        rot = qr[:, ls] * cos + qs[:, ls] * sin
        q_ref[0, h, :, MLA_NOPE:MLA_QK] = (rot[:, 0:MLA_ROPE] * MLA_Q_SCALE).astype(BF16)

    ckvn = _rms(ckv_ref[...].astype(F32), kvg_ref[...]).astype(BF16)
    knt = _dot_nt(wknt_ref[...], ckvn)
    vv = _dot(ckvn, wv_ref[...])
    misc = misc_ref[...]
    kr = misc.astype(F32) * cos + _dot(misc, perm_ref[...]) * sin
    krt = _dot_nt(eye_ref[...], kr.astype(BF16)).astype(BF16)
    for h in range(MLA_HEADS):
        kt_ref[0, h, 0, 0:MLA_NOPE, :] = knt[h * MLA_NOPE:(h + 1) * MLA_NOPE].astype(BF16)
        kt_ref[0, h, 0, MLA_NOPE:MLA_QK, :] = krt
        v_ref[0, h, :, 0:MLA_V] = vv[:, h * MLA_V:(h + 1) * MLA_V].astype(BF16)
        v_ref[0, h, :, MLA_V:MLA_V_EXT] = jnp.ones((vv.shape[0], MLA_V), BF16)


def _mlaprep(p, cos, sin, qg, kvg, wts, *, batch, tm):
    t_all = p.shape[0]
    t = t_all // batch
    nb = t // tm
    ntab = cos.shape[0] // tm
    wqn, wqr, wqs, wknt, wv, perm, eye = wts
    full = lambda a: pl.BlockSpec(a.shape, lambda b, i: (0,) * a.ndim)
    return pl.pallas_call(
        _mlaprep_kernel,
        grid=(batch, nb),
        in_specs=[
            pl.BlockSpec((tm, MLA_Q_RANK), lambda b, i: (b * nb + i, P_CQ // MLA_Q_RANK)),
            pl.BlockSpec((tm, MLA_KV_RANK), lambda b, i: (b * nb + i, P_CKV // MLA_KV_RANK)),
            pl.BlockSpec((tm, LANES), lambda b, i: (b * nb + i, P_MISC // LANES)),
            pl.BlockSpec((tm, LANES), lambda b, i: (i % ntab, 0)),
            pl.BlockSpec((tm, LANES), lambda b, i: (i % ntab, 0)),
            full(qg), full(kvg), full(wqn), full(wqr), full(wqs), full(wknt), full(wv),
            full(perm), full(eye),
        ],
        out_specs=[
            pl.BlockSpec((1, MLA_HEADS, tm, MLA_QK), lambda b, i: (b, 0, i, 0)),
            pl.BlockSpec((1, MLA_HEADS, 1, MLA_QK, tm), lambda b, i: (b, 0, i, 0, 0)),
            pl.BlockSpec((1, MLA_HEADS, tm, MLA_V_EXT), lambda b, i: (b, 0, i, 0)),
        ],
        out_shape=[
            jax.ShapeDtypeStruct((batch, MLA_HEADS, t, MLA_QK), BF16),
            jax.ShapeDtypeStruct((batch, MLA_HEADS, nb, MLA_QK, tm), BF16),
            jax.ShapeDtypeStruct((batch, MLA_HEADS, t, MLA_V_EXT), BF16),
        ],
        compiler_params=_cparams(("arbitrary", "arbitrary")),
        name="mlaprep",
    )(p, p, p, cos, sin, qg, kvg, wqn, wqr, wqs, wknt, wv, perm, eye)


def _attn_kernel(*refs, n_seg, n_sub):
    q_ref = refs[0]
    kt_refs = refs[1:1 + 2 * n_seg:2]
    v_refs = refs[2:2 + 2 * n_seg:2]
    o_ref = refs[1 + 2 * n_seg]
    m_scr, acc_scr = refs[2 + 2 * n_seg:]

    rows_per_sub = q_ref.shape[2] // n_sub
    m_scr[...] = jnp.full(m_scr.shape, -jnp.inf, F32)
    acc_scr[...] = jnp.zeros(acc_scr.shape, F32)

    for kt_ref, v_ref in zip(kt_refs, v_refs):
        n_blocks, tk = kt_ref.shape[2], kt_ref.shape[4]

        def step(j, carry, kt_ref=kt_ref, v_ref=v_ref, tk=tk):
            kt = kt_ref[0, 0, j]
            v_blk = v_ref[0, 0, pl.ds(pl.multiple_of(j * tk, tk), tk), :]
            for u in range(n_sub):
                rows = slice(u * rows_per_sub, (u + 1) * rows_per_sub)
                s = _dot(q_ref[0, 0, rows, :], kt)
                m_prev = m_scr[rows, :]
                m_next = jnp.maximum(m_prev, jnp.max(s, axis=1, keepdims=True))
                p = jnp.exp2(s - jnp.concatenate([m_next] * (tk // LANES), axis=1))
                alpha = jnp.exp2(m_prev - m_next)
                acc_scr[rows, :] = (jnp.concatenate([alpha] * (MLA_V_EXT // LANES), axis=1)
                                    * acc_scr[rows, :] + _dot(p.astype(BF16), v_blk))
                m_scr[rows, :] = m_next
            return carry

        lax.fori_loop(0, n_blocks, step, 0)

    o_ref[0] = (acc_scr[:, 0:MLA_V] / acc_scr[:, MLA_V:MLA_V_EXT]).astype(BF16)


def _attention(q, segs, *, tq, n_sub):
    b, h, t, dqk = q.shape
    in_specs = [pl.BlockSpec((1, 1, tq, dqk), lambda bi, hi, qi: (bi, hi, qi, 0))]
    args = [q]
    for kt, v in segs:
        in_specs.append(pl.BlockSpec((1, 1) + kt.shape[2:], lambda bi, hi, qi: (bi, hi, 0, 0, 0)))
        in_specs.append(pl.BlockSpec((1, 1) + v.shape[2:], lambda bi, hi, qi: (bi, hi, 0, 0)))
        args += [kt, v]
    return pl.pallas_call(
        functools.partial(_attn_kernel, n_seg=len(segs), n_sub=n_sub),
        grid=(b, h, t // tq),
        in_specs=in_specs,
        out_specs=pl.BlockSpec((1, tq, MLA_V), lambda bi, hi, qi: (bi, qi, hi)),
        out_shape=jax.ShapeDtypeStruct((b, t, h * MLA_V), BF16),
        scratch_shapes=[pltpu.VMEM((tq, LANES), F32), pltpu.VMEM((tq, MLA_V_EXT), F32)],
        compiler_params=_cparams(("arbitrary", "arbitrary", "arbitrary")),
        name="mla_attention",
    )(*args)


def _outproj_kernel(*refs, with_router):
    (x_ref, of_ref, ob_ref, r_ref, mla_ref, mod_ref, gg_ref, wo_ref, ln2_ref) = refs[:9]
    if with_router:
        rwh_ref, rwl_ref, x1_ref, h2_ref, comb_ref = refs[9:]
    else:
        x1_ref, h2_ref = refs[9:]
    m = mod_ref[0]
    o = of_ref[...].astype(F32) + ob_ref[...].astype(F32)
    gg = gg_ref[...]
    y = jnp.concatenate(
        [_rms(o[:, h * GLA_DV:(h + 1) * GLA_DV], gg) for h in range(GLA_HEADS)], axis=1)
    mix = (y * _silu(r_ref[...].astype(F32))).astype(BF16)
    yo = _dot(mix, wo_ref[0:GLA_V_W, :]) + _dot(mla_ref[...], wo_ref[GLA_V_W:, :])
    x1 = x_ref[...] + m[2:3] * yo
    x1_ref[...] = x1
    h2 = _rms(x1, ln2_ref[...]) * (1.0 + m[4:5]) + m[3:4]
    h2_ref[...] = h2.astype(BF16)
    if with_router:
        h_hi = h2.astype(BF16)
        h_lo = (h2 - h_hi.astype(F32)).astype(BF16)
        logits = _dot(h_hi, rwh_ref[...]) + _dot(h_lo, rwh_ref[...]) + _dot(h_hi, rwl_ref[...])
        lane = lax.broadcasted_iota(jnp.int32, logits.shape, 1).astype(F32)
        neg = jnp.float32(-jnp.inf)
        logits = jnp.where(lane < N_EXPERTS, logits, neg)
        m1 = jnp.max(logits, axis=1, keepdims=True)
        i1 = jnp.min(jnp.where(logits == m1, lane, float(LANES)), axis=1, keepdims=True)
        rest = jnp.where(lane == i1, neg, logits)
        m2 = jnp.max(rest, axis=1, keepdims=True)
        i2 = jnp.min(jnp.where(rest == m2, lane, float(LANES)), axis=1, keepdims=True)
        e2 = jnp.exp(m2 - m1)
        w1 = 1.0 / (1.0 + e2)
        comb_ref[...] = jnp.where(lane == i1, w1, 0.0) + jnp.where(lane == i2, e2 * w1, 0.0)


def _outproj(x, o_f, o_b, p, mla, mods, row_fn, gg, wo, ln2, router, tm):
    t, d = x.shape
    with_router = router is not None
    full = lambda a: pl.BlockSpec(a.shape, lambda i: (0,) * a.ndim)
    in_specs = [
        pl.BlockSpec((tm, d), lambda i: (i, 0)),
        pl.BlockSpec((tm, GLA_V_W), lambda i: (i, 0)),
        pl.BlockSpec((tm, GLA_V_W), lambda i: (i, 0)),
        pl.BlockSpec((tm, GLA_V_W), lambda i: (i, P_R // GLA_V_W)),
        pl.BlockSpec((tm, MLA_V_W), lambda i: (i, 0)),
        pl.BlockSpec((1, 6, d), lambda i: (row_fn(i), 0, 0)),
        full(gg), full(wo), full(ln2),
    ]
    args = [x, o_f, o_b, p, mla, mods, gg, wo, ln2]
    out_specs = [pl.BlockSpec((tm, d), lambda i: (i, 0)), pl.BlockSpec((tm, d), lambda i: (i, 0))]
    out_shape = [jax.ShapeDtypeStruct((t, d), F32), jax.ShapeDtypeStruct((t, d), BF16)]
    if with_router:
        in_specs += [full(router[0]), full(router[1])]
        args += list(router)
        out_specs.append(pl.BlockSpec((tm, LANES), lambda i: (i, 0)))
        out_shape.append(jax.ShapeDtypeStruct((t, LANES), F32))
    return pl.pallas_call(
        functools.partial(_outproj_kernel, with_router=with_router),
        grid=(t // tm,),
        in_specs=in_specs,
        out_specs=out_specs,
        out_shape=out_shape,
        compiler_params=_cparams(("arbitrary",)),
        name="outproj",
    )(*args)


def _ffn_kernel(*refs, with_comb, final_norm):
    h_ref, x1_ref, mod_ref = refs[:3]
    k = 3
    comb_ref = fin_ref = None
    if with_comb:
        comb_ref = refs[k]
        k += 1
    wg_ref, wu_ref, wd_ref = refs[k:k + 3]
    k += 3
    if final_norm:
        fin_ref = refs[k]
        k += 1
    o_ref, acc = refs[k:]
    e = pl.program_id(1)
    f = pl.program_id(2)

    @pl.when((e == 0) & (f == 0))
    def _():
        acc[...] = jnp.zeros(acc.shape, F32)

    h = h_ref[...]
    a = _dot(h, wg_ref[0])
    u = _dot(h, wu_ref[0])
    act = _silu(a) * u
    if with_comb:
        comb = comb_ref[...]
        lane = lax.broadcasted_iota(jnp.int32, comb.shape, 1)
        act = act * jnp.sum(jnp.where(lane == e, comb, 0.0), axis=1, keepdims=True)
    acc[...] += _dot(act.astype(BF16), wd_ref[0])

    @pl.when((e == pl.num_programs(1) - 1) & (f == pl.num_programs(2) - 1))
    def _():
        x2 = x1_ref[...] + mod_ref[0][5:6] * acc[...]
        if final_norm:
            x2 = _rms(x2, fin_ref[...])
        o_ref[...] = x2


def _ffn(h2, x1, mods, row_fn, comb, wg, wu, wd, fin_g, tm, tf):
    t, d = x1.shape
    n_e, _, ff = wg.shape
    with_comb = comb is not None
    final_norm = fin_g is not None
    in_specs = [
        pl.BlockSpec((tm, d), lambda i, e, f: (i, 0)),
        pl.BlockSpec((tm, d), lambda i, e, f: (i, 0)),
        pl.BlockSpec((1, 6, d), lambda i, e, f: (row_fn(i), 0, 0)),
    ]
    args = [h2, x1, mods]
    if with_comb:
        in_specs.append(pl.BlockSpec((tm, LANES), lambda i, e, f: (i, 0)))
        args.append(comb)
    in_specs += [
        pl.BlockSpec((1, d, tf), lambda i, e, f: (e, 0, f)),
        pl.BlockSpec((1, d, tf), lambda i, e, f: (e, 0, f)),
        pl.BlockSpec((1, tf, d), lambda i, e, f: (e, f, 0)),
    ]
    args += [wg, wu, wd]
    if final_norm:
        in_specs.append(pl.BlockSpec((1, d), lambda i, e, f: (0, 0)))
        args.append(fin_g)
    return pl.pallas_call(
        functools.partial(_ffn_kernel, with_comb=with_comb, final_norm=final_norm),
        grid=(t // tm, n_e, ff // tf),
        in_specs=in_specs,
        out_specs=pl.BlockSpec((tm, d), lambda i, e, f: (i, 0)),
        out_shape=jax.ShapeDtypeStruct((t, d), F32),
        scratch_shapes=[pltpu.VMEM((tm, d), F32)],
        compiler_params=_cparams(("arbitrary", "arbitrary", "arbitrary")),
        name="ffn",
    )(*args)


def _rope_partner():
    j = np.arange(MLA_ROPE)
    return np.where((j % 32) < 16, j + 16, j - 16)


def _prep_in_weight(w):
    d = w.shape[0]
    cols = [w[:, 0:1024], w[:, 1056:1568], w[:, 1568:1824], w[:, 1824:1952], w[:, 1952:2016],
            w[:, 1024:1056], jnp.zeros((d, P_WIDTH - 2016), w.dtype)]
    return jnp.concatenate(cols, axis=1).astype(BF16)


def _prep_gate_weight(w_g2, b_g2):
    outs = []
    for z, off in ((0, MISC_GF), (1, MISC_GB)):
        wz = jnp.zeros((LANES, GLA_QK_W), F32).at[off:off + GLA_GATE_RANK].set(w_g2[z])
        outs.append((wz.astype(BF16), b_g2[z].reshape(1, GLA_QK_W)))
    return outs


def _prep_mla_weights(w_uq, w_ukv):
    partner = _rope_partner()
    wq = w_uq.reshape(MLA_Q_RANK, MLA_HEADS, MLA_QK)
    wqn = wq[:, :, :MLA_NOPE].reshape(MLA_Q_RANK, MLA_HEADS * MLA_NOPE)
    rope = wq[:, :, MLA_NOPE:]
    pad = jnp.zeros((MLA_Q_RANK, MLA_HEADS, LANES - MLA_ROPE), w_uq.dtype)
    wqr = jnp.concatenate([rope, pad], axis=2).reshape(MLA_Q_RANK, MLA_HEADS * LANES)
    wqs = jnp.concatenate([rope[:, :, partner], pad], axis=2).reshape(MLA_Q_RANK, MLA_HEADS * LANES)
    wkv = w_ukv.reshape(MLA_KV_RANK, MLA_HEADS, MLA_NOPE + MLA_V)
    wknt = wkv[:, :, :MLA_NOPE].reshape(MLA_KV_RANK, MLA_HEADS * MLA_NOPE).T
    wv = wkv[:, :, MLA_NOPE:].reshape(MLA_KV_RANK, MLA_HEADS * MLA_V)
    perm = np.zeros((LANES, LANES), np.float32)
    perm[partner, np.arange(MLA_ROPE)] = 1.0
    eye = np.eye(MLA_ROPE, LANES, dtype=np.float32)
    return (wqn.astype(BF16), wqr.astype(BF16), wqs.astype(BF16), wknt.astype(BF16),
            wv.astype(BF16), jnp.asarray(perm, BF16), jnp.asarray(eye, BF16))


def _rope_tables(n_tok):
    rows = n_tok // GRID_W
    row = jnp.repeat(jnp.arange(rows, dtype=F32), GRID_W)
    col = jnp.tile(jnp.arange(GRID_W, dtype=F32), rows)
    nfreq = MLA_ROPE // 4
    inv = ROPE_BASE ** (-jnp.arange(nfreq, dtype=F32) / nfreq)
    ar = row[:, None] * inv
    ac = col[:, None] * inv
    zero = jnp.zeros((n_tok, LANES - MLA_ROPE), F32)
    cos = jnp.concatenate([jnp.cos(ar), jnp.cos(ar), jnp.cos(ac), jnp.cos(ac), zero], axis=1)
    sin = jnp.concatenate([-jnp.sin(ar), jnp.sin(ar), -jnp.sin(ac), jnp.sin(ac), zero], axis=1)
    return cos, sin


def _identity_tables(n_tok):
    cos = jnp.concatenate([jnp.ones((n_tok, MLA_ROPE), F32),
                           jnp.zeros((n_tok, LANES - MLA_ROPE), F32)], axis=1)
    return cos, jnp.zeros((n_tok, LANES), F32)


def _pick_tile(n, pref):
    t = min(n, pref)
    while n % t:
        t //= 2
    return t


def _pick_ff_tile(ff):
    best = LANES
    for m in range(1, ff // LANES + 1):
        if ff % (m * LANES) == 0 and m * LANES <= 1408:
            best = m * LANES
    return best


@jax.jit
def _forward(x, c, ctx, c_ctx, w_mod, b_mod, ln1_g, ln2_g, w_in, w_gla_g2, b_gla_g2, gla_norm_g,
             mla_q_norm_g, w_uq, mla_kv_norm_g, w_ukv, w_out, ffn_w_gate, ffn_w_up, ffn_w_down,
             router_w, exp_w_gate, exp_w_up, exp_w_down, final_norm_g):
    batch, seq, d = x.shape
    n_ctx = ctx.shape[1]
    depth = w_mod.shape[0]

    cvec = jnp.zeros((8, d), F32).at[:batch].set(c).at[batch].set(c_ctx)
    mods_all = _modulation(cvec, w_mod, b_mod).reshape(depth, 8, 6, d)

    xl = x.reshape(batch * seq, d)
    xc = ctx.reshape(batch * n_ctx, d)

    tm_l = _pick_tile(seq, 512)
    tm_c = _pick_tile(n_ctx, 256)
    tk_l = _pick_tile(seq, 1024)
    cb_l = _pick_tile(seq, 256)
    cb_c = _pick_tile(n_ctx, 256)
    row_l = lambda tm: (lambda i: i // (seq // tm))
    row_c = lambda i: batch

    rope_l = _rope_tables(seq)
    rope_c = _identity_tables(tm_c)
    zero_state = jnp.zeros((batch, GLA_QK_W, GLA_DV), F32)

    for i in range(depth):
        need_ctx = i < depth - 1
        last = i == depth - 1
        mods = mods_all[i]
        ln1 = ln1_g[i].reshape(1, d)
        ln2 = ln2_g[i].reshape(1, d)
        w_in_r = _prep_in_weight(w_in[i])
        gates = _prep_gate_weight(w_gla_g2[i], b_gla_g2[i])
        mla_w = _prep_mla_weights(w_uq[i], w_ukv[i])
        qg = mla_q_norm_g[i].reshape(1, MLA_Q_RANK)
        kvg = mla_kv_norm_g[i].reshape(1, MLA_KV_RANK)
        gg = gla_norm_g[i].reshape(1, GLA_DV)
        wo = w_out[i].astype(BF16)

        p_l = _inproj(xl, mods, row_l(tm_l), ln1, w_in_r, tm_l)
        p_c = _inproj(xc, mods, row_c, ln1, w_in_r, tm_c)

        oc_f, sc_f = _gla(p_c, *gates[0], zero_state, batch=batch, reverse=False, cb=cb_c)
        oc_b, sc_b = _gla(p_c, *gates[1], zero_state, batch=batch, reverse=True, cb=cb_c)
        ol_f, _ = _gla(p_l, *gates[0], sc_f, batch=batch, reverse=False, cb=cb_l)
        ol_b, _ = _gla(p_l, *gates[1], sc_b, batch=batch, reverse=True, cb=cb_l)

        q_l, kt_l, v_l = _mlaprep(p_l, *rope_l, qg, kvg, mla_w, batch=batch, tm=tk_l)
        q_c, kt_c, v_c = _mlaprep(p_c, *rope_c, qg, kvg, mla_w, batch=batch, tm=tm_c)
        m_l = _attention(q_l, [(kt_l, v_l), (kt_c, v_c)], tq=tk_l, n_sub=max(1, tk_l // 512))
        m_l = m_l.reshape(batch * seq, MLA_V_W)

        if i % 2 == 0:
            j = i // 2
            router = None
            wg = ffn_w_gate[j][None].astype(BF16)
            wu = ffn_w_up[j][None].astype(BF16)
            wd = ffn_w_down[j][None].astype(BF16)
        else:
            j = i // 2
            rw = jnp.zeros((d, LANES), F32).at[:, :N_EXPERTS].set(router_w[j])
            rw_hi = rw.astype(BF16)
            router = (rw_hi, (rw - rw_hi.astype(F32)).astype(BF16))
            wg = exp_w_gate[j].astype(BF16)
            wu = exp_w_up[j].astype(BF16)
            wd = exp_w_down[j].astype(BF16)
        tf = _pick_ff_tile(wg.shape[2])
        fin = final_norm_g.reshape(1, d) if last else None

        outs = _outproj(xl, ol_f, ol_b, p_l, m_l, mods, row_l(tm_l), gg, wo, ln2, router, tm_l)
        comb = outs[2] if router is not None else None
        tm_f = _pick_tile(seq, 512)
        xl = _ffn(outs[1], outs[0], mods, row_l(tm_f), comb, wg, wu, wd, fin, tm_f, tf)

        if need_ctx:
            m_c = _attention(q_c, [(kt_c, v_c)], tq=tm_c, n_sub=1).reshape(batch * n_ctx, MLA_V_W)
            outs_c = _outproj(xc, oc_f, oc_b, p_c, m_c, mods, row_c, gg, wo, ln2, router, tm_c)
            comb_c = outs_c[2] if router is not None else None
            xc = _ffn(outs_c[1], outs_c[0], mods, row_c, comb_c, wg, wu, wd, None,
                      _pick_tile(batch * n_ctx, 512), tf)

    return xl.reshape(batch, seq, d)


def kernel(x, c, ctx, c_ctx, w_mod, b_mod, ln1_g, ln2_g, w_in, w_gla_g2, b_gla_g2, gla_norm_g,
           mla_q_norm_g, w_uq, mla_kv_norm_g, w_ukv, w_out, ffn_w_gate, ffn_w_up, ffn_w_down,
           router_w, exp_w_gate, exp_w_up, exp_w_down, final_norm_g):
    return _forward(x, c, ctx, c_ctx, w_mod, b_mod, ln1_g, ln2_g, w_in, w_gla_g2, b_gla_g2,
                    gla_norm_g, mla_q_norm_g, w_uq, mla_kv_norm_g, w_ukv, w_out, ffn_w_gate,
                    ffn_w_up, ffn_w_down, router_w, exp_w_gate, exp_w_up, exp_w_down, final_norm_g)
```

```python
import functools

import numpy as np
import jax
import jax.numpy as jnp
from jax import lax
from jax.experimental import pallas as pl
from jax.experimental.pallas import tpu as pltpu
from jax.experimental.pallas import tpu_sc as plsc

F32 = jnp.float32
BF16 = jnp.bfloat16

D_MODEL = 1024
EPS = 1e-6
GRID_W = 64

GLA_HEADS = 4
GLA_DK = 64
GLA_DV = 128
GLA_GATE_RANK = 16
GLA_GATE_NORM = 16.0
GLA_CHUNK = 64
GLA_QK_W = GLA_HEADS * GLA_DK
GLA_V_W = GLA_HEADS * GLA_DV
GLA_EXP_CLAMP = 80.0

MLA_HEADS = 4
MLA_NOPE = 128
MLA_ROPE = 64
MLA_V = 128
MLA_QK = MLA_NOPE + MLA_ROPE
MLA_Q_RANK = 256
MLA_KV_RANK = 128
MLA_SCALE = MLA_QK ** -0.5
MLA_Q_SCALE = MLA_SCALE * 1.4426950408889634
MLA_V_W = MLA_HEADS * MLA_V
MLA_V_EXT = 2 * MLA_V
ROPE_BASE = 10000.0

N_EXPERTS = 8
LANES = 128
ROUTE_E1, ROUTE_E2, ROUTE_W1, ROUTE_W2 = 0, 1, 2, 3

SC_CORES = 2
SC_SUBCORES = 16
SC_GATHER_ROWS = 64
MOE_TILE = 512

P_Q, P_K, P_V, P_R, P_CQ, P_CKV, P_MISC = 0, 256, 512, 1024, 1536, 1792, 1920
P_WIDTH = 2048
MISC_KR, MISC_GF, MISC_GB = 0, 64, 80

VMEM_LIMIT = 56 * 1024 * 1024


def _cparams(sem):
    return pltpu.CompilerParams(dimension_semantics=sem, vmem_limit_bytes=VMEM_LIMIT)


def _rms(x, g):
    return x * lax.rsqrt(jnp.mean(x * x, axis=-1, keepdims=True) + EPS) * g


def _silu(x):
    return x / (1.0 + jnp.exp(-x))


def _dot(a, b):
    return jnp.dot(a, b, preferred_element_type=F32)


def _dot_nt(a, b):
    return lax.dot_general(a, b, (((1,), (1,)), ((), ())), preferred_element_type=F32)


def _dot_tn(a, b):
    return lax.dot_general(a, b, (((0,), (0,)), ((), ())), preferred_element_type=F32)


def _mod_kernel(c_ref, w_ref, b_ref, o_ref):
    s = _silu(c_ref[...]).astype(BF16)
    o_ref[0] = _dot(s, w_ref[0].astype(BF16)) + b_ref[0]


def _modulation(cvec, w_mod, b_mod):
    depth, d, n = w_mod.shape
    tn = 1536
    return pl.pallas_call(
        _mod_kernel,
        grid=(depth, n // tn),
        in_specs=[
            pl.BlockSpec((8, d), lambda l, j: (0, 0)),
            pl.BlockSpec((1, d, tn), lambda l, j: (l, 0, j)),
            pl.BlockSpec((1, 1, tn), lambda l, j: (l, 0, j)),
        ],
        out_specs=pl.BlockSpec((1, 8, tn), lambda l, j: (l, 0, j)),
        out_shape=jax.ShapeDtypeStruct((depth, 8, n), F32),
        compiler_params=_cparams(("arbitrary", "arbitrary")),
        name="modulation",
    )(cvec, w_mod, b_mod.reshape(depth, 1, n))


def _inproj_kernel(x_ref, mod_ref, g_ref, w_ref, o_ref):
    m = mod_ref[0]
    h = _rms(x_ref[...], g_ref[...]) * (1.0 + m[1:2]) + m[0:1]
    o_ref[...] = _dot(h.astype(BF16), w_ref[...]).astype(BF16)


def _inproj(x, mods, row_fn, ln_g, w, tm):
    t, d = x.shape
    return pl.pallas_call(
        _inproj_kernel,
        grid=(t // tm,),
        in_specs=[
            pl.BlockSpec((tm, d), lambda i: (i, 0)),
            pl.BlockSpec((1, 6, d), lambda i: (row_fn(i), 0, 0)),
            pl.BlockSpec((1, d), lambda i: (0, 0)),
            pl.BlockSpec((d, P_WIDTH), lambda i: (0, 0)),
        ],
        out_specs=pl.BlockSpec((tm, P_WIDTH), lambda i: (i, 0)),
        out_shape=jax.ShapeDtypeStruct((t, P_WIDTH), BF16),
        compiler_params=_cparams(("arbitrary",)),
        name="inproj",
    )(x, mods, ln_g, w)


def _gla_kernel(q_ref, k_ref, v_ref, misc_ref, wg_ref, bg_ref, tri_ref, s0_ref,
                o_ref, sfin_ref, s_scr, *, reverse, n_chunks):
    blk = pl.program_id(1)
    c_len = GLA_CHUNK

    @pl.when(blk == 0)
    def _():
        s_scr[...] = s0_ref[0]

    pre = _dot(misc_ref[...], wg_ref[...]) + bg_ref[...]
    g = (jnp.minimum(pre, 0.0) - jnp.log(1.0 + jnp.exp(-jnp.abs(pre)))) * (1.0 / GLA_GATE_NORM)
    g_hi = g.astype(BF16)
    g_lo = (g - g_hi.astype(F32)).astype(BF16)
    tri = tri_ref[...]
    cum = _dot(tri, g_hi) + _dot(tri, g_lo)

    lane = lax.broadcasted_iota(jnp.int32, (c_len, GLA_QK_W), 1)
    head_masks = [(lane >= h * GLA_DK) & (lane < (h + 1) * GLA_DK) for h in range(GLA_HEADS)]
    row = lax.broadcasted_iota(jnp.int32, (GLA_HEADS * c_len, c_len), 0) % c_len
    col = lax.broadcasted_iota(jnp.int32, (GLA_HEADS * c_len, c_len), 1)
    pair_mask = (col >= row) if reverse else (col <= row)
    ones = jnp.ones((c_len, GLA_DV), BF16)

    def stack_heads(a):
        return jnp.concatenate([jnp.where(mk, a, 0.0) for mk in head_masks], axis=0).astype(BF16)

    order = range(n_chunks - 1, -1, -1) if reverse else range(n_chunks)
    for c in order:
        sl = slice(c * c_len, (c + 1) * c_len)
        xc = cum[sl]
        tot = xc[0:1] if reverse else xc[c_len - 1:c_len]
        ref = xc[c_len // 2:c_len // 2 + 1]
        qc = q_ref[sl, :].astype(F32) * (GLA_DK ** -0.5)
        kc = k_ref[sl, :].astype(F32)
        vc = v_ref[sl, :]
        q_mid = qc * jnp.exp(jnp.minimum(xc - ref, GLA_EXP_CLAMP))
        k_mid = (kc * jnp.exp(jnp.minimum(ref - xc, GLA_EXP_CLAMP))).astype(BF16)
        q_dec = qc * jnp.exp(xc)
        k_dec = (kc * jnp.exp(tot - xc)).astype(BF16)

        attn = _dot_nt(stack_heads(q_mid), k_mid)
        attn = jnp.where(pair_mask, attn, 0.0).astype(BF16)
        s_prev = s_scr[...]
        o_inter = _dot(stack_heads(q_dec), s_prev.astype(BF16))
        for h in range(GLA_HEADS):
            rs = slice(h * c_len, (h + 1) * c_len)
            vs = slice(h * GLA_DV, (h + 1) * GLA_DV)
            o_h = o_inter[rs] + _dot(attn[rs], vc[:, vs])
            o_ref[sl, vs] = o_h.astype(BF16)

        kv_full = _dot_tn(k_dec, vc)
        kv = jnp.concatenate(
            [kv_full[h * GLA_DK:(h + 1) * GLA_DK, h * GLA_DV:(h + 1) * GLA_DV]
             for h in range(GLA_HEADS)], axis=0)
        tot_col = _dot_tn(g_hi[sl], ones) + _dot_tn(g_lo[sl], ones)
        s_scr[...] = s_prev * jnp.exp(tot_col) + kv

    @pl.when(blk == pl.num_programs(1) - 1)
    def _():
        sfin_ref[0] = s_scr[...]


def _block_diag_tri(n_chunks, upper):
    c = GLA_CHUNK
    t = np.triu(np.ones((c, c), np.float32)) if upper else np.tril(np.ones((c, c), np.float32))
    return jnp.asarray(np.kron(np.eye(n_chunks, dtype=np.float32), t), dtype=BF16)


def _gla(p, wg, bg, s0, *, batch, reverse, cb):
    t_all = p.shape[0]
    nblk = t_all // batch // cb
    n_chunks = cb // GLA_CHUNK

    def tok(b, i):
        return b * nblk + ((nblk - 1 - i) if reverse else i)

    kern = functools.partial(_gla_kernel, reverse=reverse, n_chunks=n_chunks)
    return pl.pallas_call(
        kern,
        grid=(batch, nblk),
        in_specs=[
            pl.BlockSpec((cb, GLA_QK_W), lambda b, i: (tok(b, i), P_Q // GLA_QK_W)),
            pl.BlockSpec((cb, GLA_QK_W), lambda b, i: (tok(b, i), P_K // GLA_QK_W)),
            pl.BlockSpec((cb, GLA_V_W), lambda b, i: (tok(b, i), P_V // GLA_V_W)),
            pl.BlockSpec((cb, LANES), lambda b, i: (tok(b, i), P_MISC // LANES)),
            pl.BlockSpec((LANES, GLA_QK_W), lambda b, i: (0, 0)),
            pl.BlockSpec((1, GLA_QK_W), lambda b, i: (0, 0)),
            pl.BlockSpec((cb, cb), lambda b, i: (0, 0)),
            pl.BlockSpec((1, GLA_QK_W, GLA_DV), lambda b, i: (b, 0, 0)),
        ],
        out_specs=[
            pl.BlockSpec((cb, GLA_V_W), lambda b, i: (tok(b, i), 0)),
            pl.BlockSpec((1, GLA_QK_W, GLA_DV), lambda b, i: (b, 0, 0)),
        ],
        out_shape=[
            jax.ShapeDtypeStruct((t_all, GLA_V_W), BF16),
            jax.ShapeDtypeStruct((batch, GLA_QK_W, GLA_DV), F32),
        ],
        scratch_shapes=[pltpu.VMEM((GLA_QK_W, GLA_DV), F32)],
        compiler_params=_cparams(("arbitrary", "arbitrary")),
        name="gla_bwd" if reverse else "gla_fwd",
    )(p, p, p, p, wg, bg, _block_diag_tri(n_chunks, reverse), s0)


def _mlaprep_kernel(cq_ref, ckv_ref, misc_ref, cos_ref, sin_ref, qg_ref, kvg_ref,
                    wqn_ref, wqr_ref, wqs_ref, wknt_ref, wv_ref, perm_ref, eye_ref,
                    q_ref, kt_ref, v_ref):
    cos = cos_ref[...]
    sin = sin_ref[...]
    cqn = _rms(cq_ref[...].astype(F32), qg_ref[...]).astype(BF16)
    qn = _dot(cqn, wqn_ref[...])
    qr = _dot(cqn, wqr_ref[...])
    qs = _dot(cqn, wqs_ref[...])
    for h in range(MLA_HEADS):
        ls = slice(h * LANES, (h + 1) * LANES)
        q_ref[0, h, :, 0:MLA_NOPE] = (qn[:, ls] * MLA_Q_SCALE).astype(BF16)
        rot = qr[:, ls] * cos + qs[:, ls] * sin
        q_ref[0, h, :, MLA_NOPE:MLA_QK] = (rot[:, 0:MLA_ROPE] * MLA_Q_SCALE).astype(BF16)

    ckvn = _rms(ckv_ref[...].astype(F32), kvg_ref[...]).astype(BF16)
    knt = _dot_nt(wknt_ref[...], ckvn)
    vv = _dot(ckvn, wv_ref[...])
    misc = misc_ref[...]
    kr = misc.astype(F32) * cos + _dot(misc, perm_ref[...]) * sin
    krt = _dot_nt(eye_ref[...], kr.astype(BF16)).astype(BF16)
    for h in range(MLA_HEADS):
        kt_ref[0, h, 0, 0:MLA_NOPE, :] = knt[h * MLA_NOPE:(h + 1) * MLA_NOPE].astype(BF16)
        kt_ref[0, h, 0, MLA_NOPE:MLA_QK, :] = krt
        v_ref[0, h, :, 0:MLA_V] = vv[:, h * MLA_V:(h + 1) * MLA_V].astype(BF16)
        v_ref[0, h, :, MLA_V:MLA_V_EXT] = jnp.ones((vv.shape[0], MLA_V), BF16)


def _mlaprep(p, cos, sin, qg, kvg, wts, *, batch, tm):
    t_all = p.shape[0]
    t = t_all // batch
    nb = t // tm
    ntab = cos.shape[0] // tm
    wqn, wqr, wqs, wknt, wv, perm, eye = wts
    full = lambda a: pl.BlockSpec(a.shape, lambda b, i: (0,) * a.ndim)
    return pl.pallas_call(
        _mlaprep_kernel,
        grid=(batch, nb),
        in_specs=[
            pl.BlockSpec((tm, MLA_Q_RANK), lambda b, i: (b * nb + i, P_CQ // MLA_Q_RANK)),
            pl.BlockSpec((tm, MLA_KV_RANK), lambda b, i: (b * nb + i, P_CKV // MLA_KV_RANK)),
            pl.BlockSpec((tm, LANES), lambda b, i: (b * nb + i, P_MISC // LANES)),
            pl.BlockSpec((tm, LANES), lambda b, i: (i % ntab, 0)),
            pl.BlockSpec((tm, LANES), lambda b, i: (i % ntab, 0)),
            full(qg), full(kvg), full(wqn), full(wqr), full(wqs), full(wknt), full(wv),
            full(perm), full(eye),
        ],
        out_specs=[
            pl.BlockSpec((1, MLA_HEADS, tm, MLA_QK), lambda b, i: (b, 0, i, 0)),
            pl.BlockSpec((1, MLA_HEADS, 1, MLA_QK, tm), lambda b, i: (b, 0, i, 0, 0)),
            pl.BlockSpec((1, MLA_HEADS, tm, MLA_V_EXT), lambda b, i: (b, 0, i, 0)),
        ],
        out_shape=[
            jax.ShapeDtypeStruct((batch, MLA_HEADS, t, MLA_QK), BF16),
            jax.ShapeDtypeStruct((batch, MLA_HEADS, nb, MLA_QK, tm), BF16),
            jax.ShapeDtypeStruct((batch, MLA_HEADS, t, MLA_V_EXT), BF16),
        ],
        compiler_params=_cparams(("arbitrary", "arbitrary")),
        name="mlaprep",
    )(p, p, p, cos, sin, qg, kvg, wqn, wqr, wqs, wknt, wv, perm, eye)


def _attn_kernel(*refs, n_seg, n_sub):
    q_ref = refs[0]
    kt_refs = refs[1:1 + 2 * n_seg:2]
    v_refs = refs[2:2 + 2 * n_seg:2]
    o_ref = refs[1 + 2 * n_seg]
    m_scr, acc_scr = refs[2 + 2 * n_seg:]

    rows_per_sub = q_ref.shape[2] // n_sub
    m_scr[...] = jnp.full(m_scr.shape, -jnp.inf, F32)
    acc_scr[...] = jnp.zeros(acc_scr.shape, F32)

    for kt_ref, v_ref in zip(kt_refs, v_refs):
        n_blocks, tk = kt_ref.shape[2], kt_ref.shape[4]

        def step(j, carry, kt_ref=kt_ref, v_ref=v_ref, tk=tk):
            kt = kt_ref[0, 0, j]
            v_blk = v_ref[0, 0, pl.ds(pl.multiple_of(j * tk, tk), tk), :]
            for u in range(n_sub):
                rows = slice(u * rows_per_sub, (u + 1) * rows_per_sub)
                s = _dot(q_ref[0, 0, rows, :], kt)
                m_prev = m_scr[rows, :]
                m_next = jnp.maximum(m_prev, jnp.max(s, axis=1, keepdims=True))
                p = jnp.exp2(s - jnp.concatenate([m_next] * (tk // LANES), axis=1))
                alpha = jnp.exp2(m_prev - m_next)
                acc_scr[rows, :] = (jnp.concatenate([alpha] * (MLA_V_EXT // LANES), axis=1)
                                    * acc_scr[rows, :] + _dot(p.astype(BF16), v_blk))
                m_scr[rows, :] = m_next
            return carry

        lax.fori_loop(0, n_blocks, step, 0)

    o_ref[0] = (acc_scr[:, 0:MLA_V] / acc_scr[:, MLA_V:MLA_V_EXT]).astype(BF16)


def _attention(q, segs, *, tq, n_sub):
    b, h, t, dqk = q.shape
    in_specs = [pl.BlockSpec((1, 1, tq, dqk), lambda bi, hi, qi: (bi, hi, qi, 0))]
    args = [q]
    for kt, v in segs:
        in_specs.append(pl.BlockSpec((1, 1) + kt.shape[2:], lambda bi, hi, qi: (bi, hi, 0, 0, 0)))
        in_specs.append(pl.BlockSpec((1, 1) + v.shape[2:], lambda bi, hi, qi: (bi, hi, 0, 0)))
        args += [kt, v]
    return pl.pallas_call(
        functools.partial(_attn_kernel, n_seg=len(segs), n_sub=n_sub),
        grid=(b, h, t // tq),
        in_specs=in_specs,
        out_specs=pl.BlockSpec((1, tq, MLA_V), lambda bi, hi, qi: (bi, qi, hi)),
        out_shape=jax.ShapeDtypeStruct((b, t, h * MLA_V), BF16),
        scratch_shapes=[pltpu.VMEM((tq, LANES), F32), pltpu.VMEM((tq, MLA_V_EXT), F32)],
        compiler_params=_cparams(("arbitrary", "arbitrary", "arbitrary")),
        name="mla_attention",
    )(*args)


def _outproj_kernel(*refs, with_router):
    (x_ref, of_ref, ob_ref, r_ref, mla_ref, mod_ref, gg_ref, wo_ref, ln2_ref) = refs[:9]
    if with_router:
        rwh_ref, rwl_ref, x1_ref, h2_ref, comb_ref = refs[9:]
    else:
        x1_ref, h2_ref = refs[9:]
    m = mod_ref[0]
    o = of_ref[...].astype(F32) + ob_ref[...].astype(F32)
    gg = gg_ref[...]
    y = jnp.concatenate(
        [_rms(o[:, h * GLA_DV:(h + 1) * GLA_DV], gg) for h in range(GLA_HEADS)], axis=1)
    mix = (y * _silu(r_ref[...].astype(F32))).astype(BF16)
    yo = _dot(mix, wo_ref[0:GLA_V_W, :]) + _dot(mla_ref[...], wo_ref[GLA_V_W:, :])
    x1 = x_ref[...] + m[2:3] * yo
    x1_ref[...] = x1
    h2 = _rms(x1, ln2_ref[...]) * (1.0 + m[4:5]) + m[3:4]
    h2_ref[...] = h2.astype(h2_ref.dtype)
    if with_router:
        h_hi = h2.astype(BF16)
        h_lo = (h2 - h_hi.astype(F32)).astype(BF16)
        logits = _dot(h_hi, rwh_ref[...]) + _dot(h_lo, rwh_ref[...]) + _dot(h_hi, rwl_ref[...])
        lane = lax.broadcasted_iota(jnp.int32, logits.shape, 1).astype(F32)
        neg = jnp.float32(-jnp.inf)
        logits = jnp.where(lane < N_EXPERTS, logits, neg)
        m1 = jnp.max(logits, axis=1, keepdims=True)
        i1 = jnp.min(jnp.where(logits == m1, lane, float(LANES)), axis=1, keepdims=True)
        rest = jnp.where(lane == i1, neg, logits)
        m2 = jnp.max(rest, axis=1, keepdims=True)
        i2 = jnp.min(jnp.where(rest == m2, lane, float(LANES)), axis=1, keepdims=True)
        e2 = jnp.exp(m2 - m1)
        w1 = 1.0 / (1.0 + e2)
        comb_ref[...] = (jnp.where(lane == ROUTE_E1, i1, 0.0) + jnp.where(lane == ROUTE_E2, i2, 0.0)
                         + jnp.where(lane == ROUTE_W1, w1, 0.0)
                         + jnp.where(lane == ROUTE_W2, e2 * w1, 0.0))


def _outproj(x, o_f, o_b, p, mla, mods, row_fn, gg, wo, ln2, router, tm):
    t, d = x.shape
    with_router = router is not None
    full = lambda a: pl.BlockSpec(a.shape, lambda i: (0,) * a.ndim)
    in_specs = [
        pl.BlockSpec((tm, d), lambda i: (i, 0)),
        pl.BlockSpec((tm, GLA_V_W), lambda i: (i, 0)),
        pl.BlockSpec((tm, GLA_V_W), lambda i: (i, 0)),
        pl.BlockSpec((tm, GLA_V_W), lambda i: (i, P_R // GLA_V_W)),
        pl.BlockSpec((tm, MLA_V_W), lambda i: (i, 0)),
        pl.BlockSpec((1, 6, d), lambda i: (row_fn(i), 0, 0)),
        full(gg), full(wo), full(ln2),
    ]
    args = [x, o_f, o_b, p, mla, mods, gg, wo, ln2]
    out_specs = [pl.BlockSpec((tm, d), lambda i: (i, 0)), pl.BlockSpec((tm, d), lambda i: (i, 0))]
    out_shape = [jax.ShapeDtypeStruct((t, d), F32),
                 jax.ShapeDtypeStruct((t, d), F32 if with_router else BF16)]
    if with_router:
        in_specs += [full(router[0]), full(router[1])]
        args += list(router)
        out_specs.append(pl.BlockSpec((tm, LANES), lambda i: (i, 0)))
        out_shape.append(jax.ShapeDtypeStruct((t, LANES), F32))
    return pl.pallas_call(
        functools.partial(_outproj_kernel, with_router=with_router),
        grid=(t // tm,),
        in_specs=in_specs,
        out_specs=out_specs,
        out_shape=out_shape,
        compiler_params=_cparams(("arbitrary",)),
        name="outproj",
    )(*args)


def _ffn_kernel(*refs, with_comb, final_norm):
    h_ref, x1_ref, mod_ref = refs[:3]
    k = 3
    comb_ref = fin_ref = None
    if with_comb:
        comb_ref = refs[k]
        k += 1
    wg_ref, wu_ref, wd_ref = refs[k:k + 3]
    k += 3
    if final_norm:
        fin_ref = refs[k]
        k += 1
    o_ref, acc = refs[k:]
    e = pl.program_id(1)
    f = pl.program_id(2)

    @pl.when((e == 0) & (f == 0))
    def _():
        acc[...] = jnp.zeros(acc.shape, F32)

    h = h_ref[...]
    a = _dot(h, wg_ref[0])
    u = _dot(h, wu_ref[0])
    act = _silu(a) * u
    if with_comb:
        comb = comb_ref[...]
        lane = lax.broadcasted_iota(jnp.int32, comb.shape, 1)
        act = act * jnp.sum(jnp.where(lane == e, comb, 0.0), axis=1, keepdims=True)
    acc[...] += _dot(act.astype(BF16), wd_ref[0])

    @pl.when((e == pl.num_programs(1) - 1) & (f == pl.num_programs(2) - 1))
    def _():
        x2 = x1_ref[...] + mod_ref[0][5:6] * acc[...]
        if final_norm:
            x2 = _rms(x2, fin_ref[...])
        o_ref[...] = x2


def _ffn(h2, x1, mods, row_fn, comb, wg, wu, wd, fin_g, tm, tf):
    t, d = x1.shape
    n_e, _, ff = wg.shape
    with_comb = comb is not None
    final_norm = fin_g is not None
    in_specs = [
        pl.BlockSpec((tm, d), lambda i, e, f: (i, 0)),
        pl.BlockSpec((tm, d), lambda i, e, f: (i, 0)),
        pl.BlockSpec((1, 6, d), lambda i, e, f: (row_fn(i), 0, 0)),
    ]
    args = [h2, x1, mods]
    if with_comb:
        in_specs.append(pl.BlockSpec((tm, LANES), lambda i, e, f: (i, 0)))
        args.append(comb)
    in_specs += [
        pl.BlockSpec((1, d, tf), lambda i, e, f: (e, 0, f)),
        pl.BlockSpec((1, d, tf), lambda i, e, f: (e, 0, f)),
        pl.BlockSpec((1, tf, d), lambda i, e, f: (e, f, 0)),
    ]
    args += [wg, wu, wd]
    if final_norm:
        in_specs.append(pl.BlockSpec((1, d), lambda i, e, f: (0, 0)))
        args.append(fin_g)
    return pl.pallas_call(
        functools.partial(_ffn_kernel, with_comb=with_comb, final_norm=final_norm),
        grid=(t // tm, n_e, ff // tf),
        in_specs=in_specs,
        out_specs=pl.BlockSpec((tm, d), lambda i, e, f: (i, 0)),
        out_shape=jax.ShapeDtypeStruct((t, d), F32),
        scratch_shapes=[pltpu.VMEM((tm, d), F32)],
        compiler_params=_cparams(("arbitrary", "arbitrary", "arbitrary")),
        name="ffn",
    )(*args)


def _sc_row_gather(table, idx):
    _, w = table.shape
    b = idx.shape[0]
    n_workers = SC_CORES * SC_SUBCORES
    assert b % (n_workers * SC_GATHER_ROWS) == 0, (b, n_workers, SC_GATHER_ROWS)
    b_per_w = b // n_workers
    n_chunks = b_per_w // SC_GATHER_ROWS
    mesh = plsc.VectorSubcoreMesh(core_axis_name="c", subcore_axis_name="s",
                                  num_cores=SC_CORES, num_subcores=SC_SUBCORES)

    def body(table_hbm, idx_hbm, out_hbm, idx_v, rows_v, sem):
        wid = lax.axis_index("s") * SC_CORES + lax.axis_index("c")
        base = wid * b_per_w

        @pl.loop(0, n_chunks)
        def _(ci):
            off = base + ci * SC_GATHER_ROWS
            pltpu.sync_copy(idx_hbm.at[pl.ds(off, SC_GATHER_ROWS)], idx_v)
            pltpu.async_copy(table_hbm.at[idx_v], rows_v, sem).wait()
            pltpu.sync_copy(rows_v, out_hbm.at[pl.ds(off, SC_GATHER_ROWS)])

    return pl.kernel(
        body,
        out_type=jax.ShapeDtypeStruct((b, w), F32),
        mesh=mesh,
        scratch_types=[pltpu.VMEM((SC_GATHER_ROWS,), jnp.int32),
                       pltpu.VMEM((SC_GATHER_ROWS, w), F32),
                       pltpu.SemaphoreType.DMA],
        name="sc_row_gather",
    )(table, idx)


def _moe_plan(route, n_tiles):
    t = route.shape[0]
    e = route[:, ROUTE_E1:ROUTE_E2 + 1].astype(jnp.int32)
    onehot = jnp.sum((e[:, :, None] == jnp.arange(N_EXPERTS, dtype=jnp.int32)).astype(jnp.int32), axis=1)
    csum = jnp.cumsum(onehot, axis=0)
    tiles_per = (csum[-1] + MOE_TILE - 1) // MOE_TILE
    tile_end = jnp.cumsum(tiles_per)
    row_off = (tile_end - tiles_per) * MOE_TILE
    rank = jnp.take_along_axis(csum - onehot, e, axis=1)
    pos = (row_off[e] + rank).T.reshape(-1)
    tok = jnp.tile(jnp.arange(t, dtype=jnp.int32), 2)
    src = jnp.zeros((n_tiles * MOE_TILE,), jnp.int32).at[pos].set(tok)
    n_used = tile_end[-1]
    tile_ids = jnp.minimum(jnp.arange(n_tiles, dtype=jnp.int32), n_used - 1)
    tile_expert = jnp.searchsorted(tile_end, tile_ids, side="right").astype(jnp.int32)
    return src, pos.astype(jnp.int32), tile_expert, n_used.reshape(1).astype(jnp.int32)


def _moe_ffn_kernel(te_ref, nused_ref, xs_ref, wg_ref, wu_ref, wd_ref, o_ref, acc):
    i = pl.program_id(0)
    f = pl.program_id(1)
    last_f = pl.num_programs(1) - 1
    used = i < nused_ref[0]

    @pl.when(used)
    def _():
        @pl.when(f == 0)
        def _():
            acc[...] = jnp.zeros(acc.shape, F32)

        h = xs_ref[...].astype(BF16)
        act = _silu(_dot(h, wg_ref[0])) * _dot(h, wu_ref[0])
        acc[...] += _dot(act.astype(BF16), wd_ref[0])

        @pl.when(f == last_f)
        def _():
            o_ref[...] = acc[...]

    @pl.when(jnp.logical_not(used) & (f == last_f))
    def _():
        o_ref[...] = jnp.zeros(o_ref.shape, F32)


def _moe_ffn(xs, tile_expert, n_used, wg, wu, wd, tf):
    rows, d = xs.shape
    ff = wg.shape[2]
    n_tiles = rows // MOE_TILE
    grid_spec = pltpu.PrefetchScalarGridSpec(
        num_scalar_prefetch=2,
        grid=(n_tiles, ff // tf),
        in_specs=[
            pl.BlockSpec((MOE_TILE, d), lambda i, f, te, nu: (i, 0)),
            pl.BlockSpec((1, d, tf), lambda i, f, te, nu: (te[i], 0, f)),
            pl.BlockSpec((1, d, tf), lambda i, f, te, nu: (te[i], 0, f)),
            pl.BlockSpec((1, tf, d), lambda i, f, te, nu: (te[i], f, 0)),
        ],
        out_specs=pl.BlockSpec((MOE_TILE, d), lambda i, f, te, nu: (i, 0)),
        scratch_shapes=[pltpu.VMEM((MOE_TILE, d), F32)],
    )
    return pl.pallas_call(
        _moe_ffn_kernel,
        grid_spec=grid_spec,
        out_shape=jax.ShapeDtypeStruct((rows, d), F32),
        compiler_params=_cparams(("arbitrary", "arbitrary")),
        name="moe_ffn",
    )(tile_expert, n_used, xs, wg, wu, wd)


def _combine_kernel(*refs, final_norm):
    x1_ref, y0_ref, y1_ref, route_ref, mod_ref = refs[:5]
    fin_ref = refs[5] if final_norm else None
    o_ref = refs[-1]
    route = route_ref[...]
    w1 = route[:, ROUTE_W1:ROUTE_W1 + 1]
    w2 = route[:, ROUTE_W2:ROUTE_W2 + 1]
    x2 = x1_ref[...] + mod_ref[0][5:6] * (w1 * y0_ref[...] + w2 * y1_ref[...])
    if final_norm:
        x2 = _rms(x2, fin_ref[...])
    o_ref[...] = x2


def _combine(x1, yg, route, mods, row_fn, fin_g, tm):
    t, d = x1.shape
    nb = t // tm
    final_norm = fin_g is not None
    in_specs = [
        pl.BlockSpec((tm, d), lambda i: (i, 0)),
        pl.BlockSpec((tm, d), lambda i: (i, 0)),
        pl.BlockSpec((tm, d), lambda i: (i + nb, 0)),
        pl.BlockSpec((tm, LANES), lambda i: (i, 0)),
        pl.BlockSpec((1, 6, d), lambda i: (row_fn(i), 0, 0)),
    ]
    args = [x1, yg, yg, route, mods]
    if final_norm:
        in_specs.append(pl.BlockSpec((1, d), lambda i: (0, 0)))
        args.append(fin_g)
    return pl.pallas_call(
        functools.partial(_combine_kernel, final_norm=final_norm),
        grid=(nb,),
        in_specs=in_specs,
        out_specs=pl.BlockSpec((tm, d), lambda i: (i, 0)),
        out_shape=jax.ShapeDtypeStruct((t, d), F32),
        compiler_params=_cparams(("arbitrary",)),
        name="moe_combine",
    )(*args)


def _moe(h2, x1, route, mods, row_fn, wg, wu, wd, fin_g, tm):
    t = h2.shape[0]
    quantum = SC_CORES * SC_SUBCORES * SC_GATHER_ROWS // MOE_TILE
    n_tiles = -(-(2 * t // MOE_TILE + N_EXPERTS) // quantum) * quantum
    src, pos, tile_expert, n_used = _moe_plan(route, n_tiles)
    xs = _sc_row_gather(h2, src)
    ys = _moe_ffn(xs, tile_expert, n_used, wg, wu, wd, _pick_ff_tile(wg.shape[2]))
    yg = _sc_row_gather(ys, pos)
    return _combine(x1, yg, route, mods, row_fn, fin_g, tm)


def _rope_partner():
    j = np.arange(MLA_ROPE)
    return np.where((j % 32) < 16, j + 16, j - 16)


def _prep_in_weight(w):
    d = w.shape[0]
    cols = [w[:, 0:1024], w[:, 1056:1568], w[:, 1568:1824], w[:, 1824:1952], w[:, 1952:2016],
            w[:, 1024:1056], jnp.zeros((d, P_WIDTH - 2016), w.dtype)]
    return jnp.concatenate(cols, axis=1).astype(BF16)


def _prep_gate_weight(w_g2, b_g2):
    outs = []
    for z, off in ((0, MISC_GF), (1, MISC_GB)):
        wz = jnp.zeros((LANES, GLA_QK_W), F32).at[off:off + GLA_GATE_RANK].set(w_g2[z])
        outs.append((wz.astype(BF16), b_g2[z].reshape(1, GLA_QK_W)))
    return outs


def _prep_mla_weights(w_uq, w_ukv):
    partner = _rope_partner()
    wq = w_uq.reshape(MLA_Q_RANK, MLA_HEADS, MLA_QK)
    wqn = wq[:, :, :MLA_NOPE].reshape(MLA_Q_RANK, MLA_HEADS * MLA_NOPE)
    rope = wq[:, :, MLA_NOPE:]
    pad = jnp.zeros((MLA_Q_RANK, MLA_HEADS, LANES - MLA_ROPE), w_uq.dtype)
    wqr = jnp.concatenate([rope, pad], axis=2).reshape(MLA_Q_RANK, MLA_HEADS * LANES)
    wqs = jnp.concatenate([rope[:, :, partner], pad], axis=2).reshape(MLA_Q_RANK, MLA_HEADS * LANES)
    wkv = w_ukv.reshape(MLA_KV_RANK, MLA_HEADS, MLA_NOPE + MLA_V)
    wknt = wkv[:, :, :MLA_NOPE].reshape(MLA_KV_RANK, MLA_HEADS * MLA_NOPE).T
    wv = wkv[:, :, MLA_NOPE:].reshape(MLA_KV_RANK, MLA_HEADS * MLA_V)
    perm = np.zeros((LANES, LANES), np.float32)
    perm[partner, np.arange(MLA_ROPE)] = 1.0
    eye = np.eye(MLA_ROPE, LANES, dtype=np.float32)
    return (wqn.astype(BF16), wqr.astype(BF16), wqs.astype(BF16), wknt.astype(BF16),
            wv.astype(BF16), jnp.asarray(perm, BF16), jnp.asarray(eye, BF16))


def _rope_tables(n_tok):
    rows = n_tok // GRID_W
    row = jnp.repeat(jnp.arange(rows, dtype=F32), GRID_W)
    col = jnp.tile(jnp.arange(GRID_W, dtype=F32), rows)
    nfreq = MLA_ROPE // 4
    inv = ROPE_BASE ** (-jnp.arange(nfreq, dtype=F32) / nfreq)
    ar = row[:, None] * inv
    ac = col[:, None] * inv
    zero = jnp.zeros((n_tok, LANES - MLA_ROPE), F32)
    cos = jnp.concatenate([jnp.cos(ar), jnp.cos(ar), jnp.cos(ac), jnp.cos(ac), zero], axis=1)
    sin = jnp.concatenate([-jnp.sin(ar), jnp.sin(ar), -jnp.sin(ac), jnp.sin(ac), zero], axis=1)
    return cos, sin


def _identity_tables(n_tok):
    cos = jnp.concatenate([jnp.ones((n_tok, MLA_ROPE), F32),
                           jnp.zeros((n_tok, LANES - MLA_ROPE), F32)], axis=1)
    return cos, jnp.zeros((n_tok, LANES), F32)


def _pick_tile(n, pref):
    t = min(n, pref)
    while n % t:
        t //= 2
    return t


def _pick_ff_tile(ff):
    best = LANES
    for m in range(1, ff // LANES + 1):
        if ff % (m * LANES) == 0 and m * LANES <= 1408:
            best = m * LANES
    return best


@jax.jit
def _forward(x, c, ctx, c_ctx, w_mod, b_mod, ln1_g, ln2_g, w_in, w_gla_g2, b_gla_g2, gla_norm_g,
             mla_q_norm_g, w_uq, mla_kv_norm_g, w_ukv, w_out, ffn_w_gate, ffn_w_up, ffn_w_down,
             router_w, exp_w_gate, exp_w_up, exp_w_down, final_norm_g):
    batch, seq, d = x.shape
    n_ctx = ctx.shape[1]
    depth = w_mod.shape[0]

    cvec = jnp.zeros((8, d), F32).at[:batch].set(c).at[batch].set(c_ctx)
    mods_all = _modulation(cvec, w_mod, b_mod).reshape(depth, 8, 6, d)

    xl = x.reshape(batch * seq, d)
    xc = ctx.reshape(batch * n_ctx, d)

    tm_l = _pick_tile(seq, 512)
    tm_c = _pick_tile(n_ctx, 256)
    tk_l = _pick_tile(seq, 1024)
    cb_l = _pick_tile(seq, 256)
    cb_c = _pick_tile(n_ctx, 256)
    row_l = lambda tm: (lambda i: i // (seq // tm))
    row_c = lambda i: batch

    rope_l = _rope_tables(seq)
    rope_c = _identity_tables(tm_c)
    zero_state = jnp.zeros((batch, GLA_QK_W, GLA_DV), F32)

    for i in range(depth):
        need_ctx = i < depth - 1
        last = i == depth - 1
        mods = mods_all[i]
        ln1 = ln1_g[i].reshape(1, d)
        ln2 = ln2_g[i].reshape(1, d)
        w_in_r = _prep_in_weight(w_in[i])
        gates = _prep_gate_weight(w_gla_g2[i], b_gla_g2[i])
        mla_w = _prep_mla_weights(w_uq[i], w_ukv[i])
        qg = mla_q_norm_g[i].reshape(1, MLA_Q_RANK)
        kvg = mla_kv_norm_g[i].reshape(1, MLA_KV_RANK)
        gg = gla_norm_g[i].reshape(1, GLA_DV)
        wo = w_out[i].astype(BF16)

        p_l = _inproj(xl, mods, row_l(tm_l), ln1, w_in_r, tm_l)
        p_c = _inproj(xc, mods, row_c, ln1, w_in_r, tm_c)

        oc_f, sc_f = _gla(p_c, *gates[0], zero_state, batch=batch, reverse=False, cb=cb_c)
        oc_b, sc_b = _gla(p_c, *gates[1], zero_state, batch=batch, reverse=True, cb=cb_c)
        ol_f, _ = _gla(p_l, *gates[0], sc_f, batch=batch, reverse=False, cb=cb_l)
        ol_b, _ = _gla(p_l, *gates[1], sc_b, batch=batch, reverse=True, cb=cb_l)

        q_l, kt_l, v_l = _mlaprep(p_l, *rope_l, qg, kvg, mla_w, batch=batch, tm=tk_l)
        q_c, kt_c, v_c = _mlaprep(p_c, *rope_c, qg, kvg, mla_w, batch=batch, tm=tm_c)
        m_l = _attention(q_l, [(kt_l, v_l), (kt_c, v_c)], tq=tk_l, n_sub=max(1, tk_l // 512))
        m_l = m_l.reshape(batch * seq, MLA_V_W)

        if i % 2 == 0:
            j = i // 2
            router = None
            wg = ffn_w_gate[j][None].astype(BF16)
            wu = ffn_w_up[j][None].astype(BF16)
            wd = ffn_w_down[j][None].astype(BF16)
        else:
            j = i // 2
            rw = jnp.zeros((d, LANES), F32).at[:, :N_EXPERTS].set(router_w[j])
            rw_hi = rw.astype(BF16)
            router = (rw_hi, (rw - rw_hi.astype(F32)).astype(BF16))
            wg = exp_w_gate[j].astype(BF16)
            wu = exp_w_up[j].astype(BF16)
            wd = exp_w_down[j].astype(BF16)
        tf = _pick_ff_tile(wg.shape[2])
        fin = final_norm_g.reshape(1, d) if last else None

        outs = _outproj(xl, ol_f, ol_b, p_l, m_l, mods, row_l(tm_l), gg, wo, ln2, router, tm_l)
        tm_f = _pick_tile(seq, 512)
        if router is None:
            xl = _ffn(outs[1], outs[0], mods, row_l(tm_f), None, wg, wu, wd, fin, tm_f, tf)
        else:
            xl = _moe(outs[1], outs[0], outs[2], mods, row_l(tm_f), wg, wu, wd, fin, tm_f)

        if need_ctx:
            m_c = _attention(q_c, [(kt_c, v_c)], tq=tm_c, n_sub=1).reshape(batch * n_ctx, MLA_V_W)
            outs_c = _outproj(xc, oc_f, oc_b, p_c, m_c, mods, row_c, gg, wo, ln2, router, tm_c)
            h2_c, comb_c = outs_c[1], None
            if router is not None:
                r_c = outs_c[2]
                lane = jnp.arange(LANES, dtype=F32)[None, :]
                comb_c = (jnp.where(lane == r_c[:, ROUTE_E1:ROUTE_E1 + 1], r_c[:, ROUTE_W1:ROUTE_W1 + 1], 0.0)
                          + jnp.where(lane == r_c[:, ROUTE_E2:ROUTE_E2 + 1], r_c[:, ROUTE_W2:ROUTE_W2 + 1], 0.0))
                h2_c = h2_c.astype(BF16)
            xc = _ffn(h2_c, outs_c[0], mods, row_c, comb_c, wg, wu, wd, None,
                      _pick_tile(batch * n_ctx, 512), tf)

    return xl.reshape(batch, seq, d)


def kernel(x, c, ctx, c_ctx, w_mod, b_mod, ln1_g, ln2_g, w_in, w_gla_g2, b_gla_g2, gla_norm_g,
           mla_q_norm_g, w_uq, mla_kv_norm_g, w_ukv, w_out, ffn_w_gate, ffn_w_up, ffn_w_down,
           router_w, exp_w_gate, exp_w_up, exp_w_down, final_norm_g):
    return _forward(x, c, ctx, c_ctx, w_mod, b_mod, ln1_g, ln2_g, w_in, w_gla_g2, b_gla_g2,
                    gla_norm_g, mla_q_norm_g, w_uq, mla_kv_norm_g, w_ukv, w_out, ffn_w_gate,
                    ffn_w_up, ffn_w_down, router_w, exp_w_gate, exp_w_up, exp_w_down, final_norm_g)
```

```python
import functools

import numpy as np
import jax
import jax.numpy as jnp
from jax import lax
from jax.experimental import pallas as pl
from jax.experimental.pallas import tpu as pltpu
from jax.experimental.pallas import tpu_sc as plsc

F32 = jnp.float32
BF16 = jnp.bfloat16

D_MODEL = 1024
EPS = 1e-6
GRID_W = 64

GLA_HEADS = 4
GLA_DK = 64
GLA_DV = 128
GLA_GATE_RANK = 16
GLA_GATE_NORM = 16.0
GLA_CHUNK = 64
GLA_QK_W = GLA_HEADS * GLA_DK
GLA_V_W = GLA_HEADS * GLA_DV
GLA_EXP_CLAMP = 80.0

MLA_HEADS = 4
MLA_NOPE = 128
MLA_ROPE = 64
MLA_V = 128
MLA_QK = MLA_NOPE + MLA_ROPE
MLA_Q_RANK = 256
MLA_KV_RANK = 128
MLA_SCALE = MLA_QK ** -0.5
MLA_Q_SCALE = MLA_SCALE * 1.4426950408889634
MLA_V_W = MLA_HEADS * MLA_V
MLA_V_EXT = 2 * MLA_V
ROPE_BASE = 10000.0

N_EXPERTS = 8
LANES = 128
ROUTE_E1, ROUTE_E2, ROUTE_W1, ROUTE_W2 = 0, 1, 2, 3

SC_CORES = 2
SC_SUBCORES = 16
SC_GATHER_ROWS = 64
MOE_TILE = 1024
MOE_FF_TILE = 512

P_Q, P_K, P_V, P_R, P_CQ, P_CKV, P_MISC = 0, 256, 512, 1024, 1536, 1792, 1920
P_WIDTH = 2048
MISC_KR, MISC_GF, MISC_GB = 0, 64, 80

VMEM_LIMIT = 56 * 1024 * 1024


def _cparams(sem):
    return pltpu.CompilerParams(dimension_semantics=sem, vmem_limit_bytes=VMEM_LIMIT)


def _rms(x, g):
    return x * lax.rsqrt(jnp.mean(x * x, axis=-1, keepdims=True) + EPS) * g


def _silu(x):
    return x / (1.0 + jnp.exp(-x))


def _dot(a, b):
    return jnp.dot(a, b, preferred_element_type=F32)


def _dot_nt(a, b):
    return lax.dot_general(a, b, (((1,), (1,)), ((), ())), preferred_element_type=F32)


def _dot_tn(a, b):
    return lax.dot_general(a, b, (((0,), (0,)), ((), ())), preferred_element_type=F32)


def _mod_kernel(c_ref, w_ref, b_ref, o_ref):
    s = _silu(c_ref[...]).astype(BF16)
    o_ref[0] = _dot(s, w_ref[0].astype(BF16)) + b_ref[0]


def _modulation(cvec, w_mod, b_mod):
    depth, d, n = w_mod.shape
    tn = 1536
    return pl.pallas_call(
        _mod_kernel,
        grid=(depth, n // tn),
        in_specs=[
            pl.BlockSpec((8, d), lambda l, j: (0, 0)),
            pl.BlockSpec((1, d, tn), lambda l, j: (l, 0, j)),
            pl.BlockSpec((1, 1, tn), lambda l, j: (l, 0, j)),
        ],
        out_specs=pl.BlockSpec((1, 8, tn), lambda l, j: (l, 0, j)),
        out_shape=jax.ShapeDtypeStruct((depth, 8, n), F32),
        compiler_params=_cparams(("arbitrary", "arbitrary")),
        name="modulation",
    )(cvec, w_mod, b_mod.reshape(depth, 1, n))


def _inproj_kernel(x_ref, mod_ref, g_ref, w_ref, o_ref):
    m = mod_ref[0]
    h = _rms(x_ref[...], g_ref[...]) * (1.0 + m[1:2]) + m[0:1]
    o_ref[...] = _dot(h.astype(BF16), w_ref[...]).astype(BF16)


def _inproj(x, mods, row_fn, ln_g, w, tm):
    t, d = x.shape
    return pl.pallas_call(
        _inproj_kernel,
        grid=(t // tm,),
        in_specs=[
            pl.BlockSpec((tm, d), lambda i: (i, 0)),
            pl.BlockSpec((1, 6, d), lambda i: (row_fn(i), 0, 0)),
            pl.BlockSpec((1, d), lambda i: (0, 0)),
            pl.BlockSpec((d, P_WIDTH), lambda i: (0, 0)),
        ],
        out_specs=pl.BlockSpec((tm, P_WIDTH), lambda i: (i, 0)),
        out_shape=jax.ShapeDtypeStruct((t, P_WIDTH), BF16),
        compiler_params=_cparams(("arbitrary",)),
        name="inproj",
    )(x, mods, ln_g, w)


def _gla_kernel(q_ref, k_ref, v_ref, misc_ref, wg_ref, bg_ref, tri_ref, s0_ref,
                o_ref, sfin_ref, s_scr, *, reverse, n_chunks):
    blk = pl.program_id(1)
    c_len = GLA_CHUNK

    @pl.when(blk == 0)
    def _():
        s_scr[...] = s0_ref[0]

    pre = _dot(misc_ref[...], wg_ref[...]) + bg_ref[...]
    g = (jnp.minimum(pre, 0.0) - jnp.log(1.0 + jnp.exp(-jnp.abs(pre)))) * (1.0 / GLA_GATE_NORM)
    g_hi = g.astype(BF16)
    g_lo = (g - g_hi.astype(F32)).astype(BF16)
    tri = tri_ref[...]
    cum = _dot(tri, g_hi) + _dot(tri, g_lo)

    lane = lax.broadcasted_iota(jnp.int32, (c_len, GLA_QK_W), 1)
    head_masks = [(lane >= h * GLA_DK) & (lane < (h + 1) * GLA_DK) for h in range(GLA_HEADS)]
    row = lax.broadcasted_iota(jnp.int32, (GLA_HEADS * c_len, c_len), 0) % c_len
    col = lax.broadcasted_iota(jnp.int32, (GLA_HEADS * c_len, c_len), 1)
    pair_mask = (col >= row) if reverse else (col <= row)
    ones = jnp.ones((c_len, GLA_DV), BF16)

    def stack_heads(a):
        return jnp.concatenate([jnp.where(mk, a, 0.0) for mk in head_masks], axis=0).astype(BF16)

    order = range(n_chunks - 1, -1, -1) if reverse else range(n_chunks)
    for c in order:
        sl = slice(c * c_len, (c + 1) * c_len)
        xc = cum[sl]
        tot = xc[0:1] if reverse else xc[c_len - 1:c_len]
        ref = xc[c_len // 2:c_len // 2 + 1]
        qc = q_ref[sl, :].astype(F32) * (GLA_DK ** -0.5)
        kc = k_ref[sl, :].astype(F32)
        vc = v_ref[sl, :]
        q_mid = qc * jnp.exp(jnp.minimum(xc - ref, GLA_EXP_CLAMP))
        k_mid = (kc * jnp.exp(jnp.minimum(ref - xc, GLA_EXP_CLAMP))).astype(BF16)
        q_dec = qc * jnp.exp(xc)
        k_dec = (kc * jnp.exp(tot - xc)).astype(BF16)

        attn = _dot_nt(stack_heads(q_mid), k_mid)
        attn = jnp.where(pair_mask, attn, 0.0).astype(BF16)
        s_prev = s_scr[...]
        o_inter = _dot(stack_heads(q_dec), s_prev.astype(BF16))
        for h in range(GLA_HEADS):
            rs = slice(h * c_len, (h + 1) * c_len)
            vs = slice(h * GLA_DV, (h + 1) * GLA_DV)
            o_h = o_inter[rs] + _dot(attn[rs], vc[:, vs])
            o_ref[sl, vs] = o_h.astype(BF16)

        kv_full = _dot_tn(k_dec, vc)
        kv = jnp.concatenate(
            [kv_full[h * GLA_DK:(h + 1) * GLA_DK, h * GLA_DV:(h + 1) * GLA_DV]
             for h in range(GLA_HEADS)], axis=0)
        tot_col = _dot_tn(g_hi[sl], ones) + _dot_tn(g_lo[sl], ones)
        s_scr[...] = s_prev * jnp.exp(tot_col) + kv

    @pl.when(blk == pl.num_programs(1) - 1)
    def _():
        sfin_ref[0] = s_scr[...]


def _block_diag_tri(n_chunks, upper):
    c = GLA_CHUNK
    t = np.triu(np.ones((c, c), np.float32)) if upper else np.tril(np.ones((c, c), np.float32))
    return jnp.asarray(np.kron(np.eye(n_chunks, dtype=np.float32), t), dtype=BF16)


def _gla(p, wg, bg, s0, *, batch, reverse, cb):
    t_all = p.shape[0]
    nblk = t_all // batch // cb
    n_chunks = cb // GLA_CHUNK

    def tok(b, i):
        return b * nblk + ((nblk - 1 - i) if reverse else i)

    kern = functools.partial(_gla_kernel, reverse=reverse, n_chunks=n_chunks)
    return pl.pallas_call(
        kern,
        grid=(batch, nblk),
        in_specs=[
            pl.BlockSpec((cb, GLA_QK_W), lambda b, i: (tok(b, i), P_Q // GLA_QK_W)),
            pl.BlockSpec((cb, GLA_QK_W), lambda b, i: (tok(b, i), P_K // GLA_QK_W)),
            pl.BlockSpec((cb, GLA_V_W), lambda b, i: (tok(b, i), P_V // GLA_V_W)),
            pl.BlockSpec((cb, LANES), lambda b, i: (tok(b, i), P_MISC // LANES)),
            pl.BlockSpec((LANES, GLA_QK_W), lambda b, i: (0, 0)),
            pl.BlockSpec((1, GLA_QK_W), lambda b, i: (0, 0)),
            pl.BlockSpec((cb, cb), lambda b, i: (0, 0)),
            pl.BlockSpec((1, GLA_QK_W, GLA_DV), lambda b, i: (b, 0, 0)),
        ],
        out_specs=[
            pl.BlockSpec((cb, GLA_V_W), lambda b, i: (tok(b, i), 0)),
            pl.BlockSpec((1, GLA_QK_W, GLA_DV), lambda b, i: (b, 0, 0)),
        ],
        out_shape=[
            jax.ShapeDtypeStruct((t_all, GLA_V_W), BF16),
            jax.ShapeDtypeStruct((batch, GLA_QK_W, GLA_DV), F32),
        ],
        scratch_shapes=[pltpu.VMEM((GLA_QK_W, GLA_DV), F32)],
        compiler_params=_cparams(("arbitrary", "arbitrary")),
        name="gla_bwd" if reverse else "gla_fwd",
    )(p, p, p, p, wg, bg, _block_diag_tri(n_chunks, reverse), s0)


def _mlaprep_kernel(cq_ref, ckv_ref, misc_ref, cos_ref, sin_ref, qg_ref, kvg_ref,
                    wqn_ref, wqr_ref, wqs_ref, wknt_ref, wv_ref, perm_ref, eye_ref,
                    q_ref, kt_ref, v_ref):
    cos = cos_ref[...]
    sin = sin_ref[...]
    cqn = _rms(cq_ref[...].astype(F32), qg_ref[...]).astype(BF16)
    qn = _dot(cqn, wqn_ref[...])
    qr = _dot(cqn, wqr_ref[...])
    qs = _dot(cqn, wqs_ref[...])
    for h in range(MLA_HEADS):
        ls = slice(h * LANES, (h + 1) * LANES)
        q_ref[0, h, :, 0:MLA_NOPE] = (qn[:, ls] * MLA_Q_SCALE).astype(BF16)
        rot = qr[:, ls] * cos + qs[:, ls] * sin
        q_ref[0, h, :, MLA_NOPE:MLA_QK] = (rot[:, 0:MLA_ROPE] * MLA_Q_SCALE).astype(BF16)

    ckvn = _rms(ckv_ref[...].astype(F32), kvg_ref[...]).astype(BF16)
    knt = _dot_nt(wknt_ref[...], ckvn)
    vv = _dot(ckvn, wv_ref[...])
    misc = misc_ref[...]
    kr = misc.astype(F32) * cos + _dot(misc, perm_ref[...]) * sin
    krt = _dot_nt(eye_ref[...], kr.astype(BF16)).astype(BF16)
    for h in range(MLA_HEADS):
        kt_ref[0, h, 0, 0:MLA_NOPE, :] = knt[h * MLA_NOPE:(h + 1) * MLA_NOPE].astype(BF16)
        kt_ref[0, h, 0, MLA_NOPE:MLA_QK, :] = krt
        v_ref[0, h, :, 0:MLA_V] = vv[:, h * MLA_V:(h + 1) * MLA_V].astype(BF16)
        v_ref[0, h, :, MLA_V:MLA_V_EXT] = jnp.ones((vv.shape[0], MLA_V), BF16)


def _mlaprep(p, cos, sin, qg, kvg, wts, *, batch, tm):
    t_all = p.shape[0]
    t = t_all // batch
    nb = t // tm
    ntab = cos.shape[0] // tm
    wqn, wqr, wqs, wknt, wv, perm, eye = wts
    full = lambda a: pl.BlockSpec(a.shape, lambda b, i: (0,) * a.ndim)
    return pl.pallas_call(
        _mlaprep_kernel,
        grid=(batch, nb),
        in_specs=[
            pl.BlockSpec((tm, MLA_Q_RANK), lambda b, i: (b * nb + i, P_CQ // MLA_Q_RANK)),
            pl.BlockSpec((tm, MLA_KV_RANK), lambda b, i: (b * nb + i, P_CKV // MLA_KV_RANK)),
            pl.BlockSpec((tm, LANES), lambda b, i: (b * nb + i, P_MISC // LANES)),
            pl.BlockSpec((tm, LANES), lambda b, i: (i % ntab, 0)),
            pl.BlockSpec((tm, LANES), lambda b, i: (i % ntab, 0)),
            full(qg), full(kvg), full(wqn), full(wqr), full(wqs), full(wknt), full(wv),
            full(perm), full(eye),
        ],
        out_specs=[
            pl.BlockSpec((1, MLA_HEADS, tm, MLA_QK), lambda b, i: (b, 0, i, 0)),
            pl.BlockSpec((1, MLA_HEADS, 1, MLA_QK, tm), lambda b, i: (b, 0, i, 0, 0)),
            pl.BlockSpec((1, MLA_HEADS, tm, MLA_V_EXT), lambda b, i: (b, 0, i, 0)),
        ],
        out_shape=[
            jax.ShapeDtypeStruct((batch, MLA_HEADS, t, MLA_QK), BF16),
            jax.ShapeDtypeStruct((batch, MLA_HEADS, nb, MLA_QK, tm), BF16),
            jax.ShapeDtypeStruct((batch, MLA_HEADS, t, MLA_V_EXT), BF16),
        ],
        compiler_params=_cparams(("arbitrary", "arbitrary")),
        name="mlaprep",
    )(p, p, p, cos, sin, qg, kvg, wqn, wqr, wqs, wknt, wv, perm, eye)


def _attn_kernel(*refs, n_seg, n_sub):
    q_ref = refs[0]
    kt_refs = refs[1:1 + 2 * n_seg:2]
    v_refs = refs[2:2 + 2 * n_seg:2]
    o_ref = refs[1 + 2 * n_seg]
    m_scr, acc_scr = refs[2 + 2 * n_seg:]

    rows_per_sub = q_ref.shape[2] // n_sub
    m_scr[...] = jnp.full(m_scr.shape, -jnp.inf, F32)
    acc_scr[...] = jnp.zeros(acc_scr.shape, F32)

    for kt_ref, v_ref in zip(kt_refs, v_refs):
        n_blocks, tk = kt_ref.shape[2], kt_ref.shape[4]

        def step(j, carry, kt_ref=kt_ref, v_ref=v_ref, tk=tk):
            kt = kt_ref[0, 0, j]
            v_blk = v_ref[0, 0, pl.ds(pl.multiple_of(j * tk, tk), tk), :]
            for u in range(n_sub):
                rows = slice(u * rows_per_sub, (u + 1) * rows_per_sub)
                s = _dot(q_ref[0, 0, rows, :], kt)
                m_prev = m_scr[rows, :]
                m_next = jnp.maximum(m_prev, jnp.max(s, axis=1, keepdims=True))
                p = jnp.exp2((s - jnp.concatenate([m_next] * (tk // LANES), axis=1)).astype(BF16))
                alpha = jnp.exp2(m_prev - m_next)
                acc_scr[rows, :] = (jnp.concatenate([alpha] * (MLA_V_EXT // LANES), axis=1)
                                    * acc_scr[rows, :] + _dot(p, v_blk))
                m_scr[rows, :] = m_next
            return carry

        lax.fori_loop(0, n_blocks, step, 0)

    o_ref[0] = (acc_scr[:, 0:MLA_V] / acc_scr[:, MLA_V:MLA_V_EXT]).astype(BF16)


def _attn_pipe_kernel(q_ref, kt_ref, v_ref, ktt_ref, vt_ref, o_ref,
                      m_scr, acc_scr, s0_scr, s1_scr, st_scr, *, n_sub):
    n_blocks, tk = kt_ref.shape[2], kt_ref.shape[4]
    rows_per_sub = q_ref.shape[2] // n_sub
    subs = [slice(u * rows_per_sub, (u + 1) * rows_per_sub) for u in range(n_sub)]
    m_scr[...] = jnp.full(m_scr.shape, -jnp.inf, F32)
    acc_scr[...] = jnp.zeros(acc_scr.shape, F32)

    def scores(kt, s_ref):
        for rows in subs:
            s_ref[rows, :] = _dot(q_ref[0, 0, rows, :], kt)

    def softmax_pv(s_ref, v_blk):
        width = s_ref.shape[1]
        for rows in subs:
            s = s_ref[rows, :]
            m_prev = m_scr[rows, :]
            m_next = jnp.maximum(m_prev, jnp.max(s, axis=1, keepdims=True))
            p = jnp.exp2((s - jnp.concatenate([m_next] * (width // LANES), axis=1)).astype(BF16))
            alpha = jnp.exp2(m_prev - m_next)
            acc_scr[rows, :] = (jnp.concatenate([alpha] * (MLA_V_EXT // LANES), axis=1)
                                * acc_scr[rows, :] + _dot(p, v_blk))
            m_scr[rows, :] = m_next

    def v_main(j):
        return v_ref[0, 0, pl.ds(pl.multiple_of(j * tk, tk), tk), :]

    scores(kt_ref[0, 0, 0], s0_scr)
    n_pairs = (n_blocks - 1) // 2

    def pair(jj, carry):
        j = 2 * jj
        scores(kt_ref[0, 0, j + 1], s1_scr)
        softmax_pv(s0_scr, v_main(j))
        scores(kt_ref[0, 0, j + 2], s0_scr)
        softmax_pv(s1_scr, v_main(j + 1))
        return carry

    lax.fori_loop(0, n_pairs, pair, 0)
    if n_blocks - 2 * n_pairs == 2:
        scores(kt_ref[0, 0, n_blocks - 1], s1_scr)
        softmax_pv(s0_scr, v_main(n_blocks - 2))
        scores(ktt_ref[0, 0, 0], st_scr)
        softmax_pv(s1_scr, v_main(n_blocks - 1))
    else:
        scores(ktt_ref[0, 0, 0], st_scr)
        softmax_pv(s0_scr, v_main(n_blocks - 1))
    softmax_pv(st_scr, vt_ref[0, 0])

    o_ref[0] = (acc_scr[:, 0:MLA_V] / acc_scr[:, MLA_V:MLA_V_EXT]).astype(BF16)


def _attention_pipelined(q, kt, v, kt_tail, v_tail, *, tq, n_sub):
    b, h, t, dqk = q.shape
    tk, tt = kt.shape[4], kt_tail.shape[4]
    assert kt_tail.shape[2] == 1
    return pl.pallas_call(
        functools.partial(_attn_pipe_kernel, n_sub=n_sub),
        grid=(b, h, t // tq),
        in_specs=[
            pl.BlockSpec((1, 1, tq, dqk), lambda bi, hi, qi: (bi, hi, qi, 0)),
            pl.BlockSpec((1, 1) + kt.shape[2:], lambda bi, hi, qi: (bi, hi, 0, 0, 0)),
            pl.BlockSpec((1, 1) + v.shape[2:], lambda bi, hi, qi: (bi, hi, 0, 0)),
            pl.BlockSpec((1, 1) + kt_tail.shape[2:], lambda bi, hi, qi: (bi, hi, 0, 0, 0)),
            pl.BlockSpec((1, 1) + v_tail.shape[2:], lambda bi, hi, qi: (bi, hi, 0, 0)),
        ],
        out_specs=pl.BlockSpec((1, tq, MLA_V), lambda bi, hi, qi: (bi, qi, hi)),
        out_shape=jax.ShapeDtypeStruct((b, t, h * MLA_V), BF16),
        scratch_shapes=[pltpu.VMEM((tq, LANES), F32), pltpu.VMEM((tq, MLA_V_EXT), F32),
                        pltpu.VMEM((tq, tk), F32), pltpu.VMEM((tq, tk), F32),
                        pltpu.VMEM((tq, tt), F32)],
        compiler_params=_cparams(("arbitrary", "arbitrary", "arbitrary")),
        name="mla_attention_pipe",
    )(q, kt, v, kt_tail, v_tail)


def _attention(q, segs, *, tq, n_sub):
    b, h, t, dqk = q.shape
    in_specs = [pl.BlockSpec((1, 1, tq, dqk), lambda bi, hi, qi: (bi, hi, qi, 0))]
    args = [q]
    for kt, v in segs:
        in_specs.append(pl.BlockSpec((1, 1) + kt.shape[2:], lambda bi, hi, qi: (bi, hi, 0, 0, 0)))
        in_specs.append(pl.BlockSpec((1, 1) + v.shape[2:], lambda bi, hi, qi: (bi, hi, 0, 0)))
        args += [kt, v]
    return pl.pallas_call(
        functools.partial(_attn_kernel, n_seg=len(segs), n_sub=n_sub),
        grid=(b, h, t // tq),
        in_specs=in_specs,
        out_specs=pl.BlockSpec((1, tq, MLA_V), lambda bi, hi, qi: (bi, qi, hi)),
        out_shape=jax.ShapeDtypeStruct((b, t, h * MLA_V), BF16),
        scratch_shapes=[pltpu.VMEM((tq, LANES), F32), pltpu.VMEM((tq, MLA_V_EXT), F32)],
        compiler_params=_cparams(("arbitrary", "arbitrary", "arbitrary")),
        name="mla_attention",
    )(*args)


def _outproj_kernel(*refs, with_router):
    (x_ref, of_ref, ob_ref, r_ref, mla_ref, mod_ref, gg_ref, wo_ref, ln2_ref) = refs[:9]
    if with_router:
        rwh_ref, rwl_ref, x1_ref, h2_ref, comb_ref = refs[9:]
    else:
        x1_ref, h2_ref = refs[9:]
    m = mod_ref[0]
    o = of_ref[...].astype(F32) + ob_ref[...].astype(F32)
    gg = gg_ref[...]
    y = jnp.concatenate(
        [_rms(o[:, h * GLA_DV:(h + 1) * GLA_DV], gg) for h in range(GLA_HEADS)], axis=1)
    mix = (y * _silu(r_ref[...].astype(F32))).astype(BF16)
    yo = _dot(mix, wo_ref[0:GLA_V_W, :]) + _dot(mla_ref[...], wo_ref[GLA_V_W:, :])
    x1 = x_ref[...] + m[2:3] * yo
    x1_ref[...] = x1
    h2 = _rms(x1, ln2_ref[...]) * (1.0 + m[4:5]) + m[3:4]
    h2_ref[...] = h2.astype(h2_ref.dtype)
    if with_router:
        h_hi = h2.astype(BF16)
        h_lo = (h2 - h_hi.astype(F32)).astype(BF16)
        logits = _dot(h_hi, rwh_ref[...]) + _dot(h_lo, rwh_ref[...]) + _dot(h_hi, rwl_ref[...])
        lane = lax.broadcasted_iota(jnp.int32, logits.shape, 1).astype(F32)
        neg = jnp.float32(-jnp.inf)
        logits = jnp.where(lane < N_EXPERTS, logits, neg)
        m1 = jnp.max(logits, axis=1, keepdims=True)
        i1 = jnp.min(jnp.where(logits == m1, lane, float(LANES)), axis=1, keepdims=True)
        rest = jnp.where(lane == i1, neg, logits)
        m2 = jnp.max(rest, axis=1, keepdims=True)
        i2 = jnp.min(jnp.where(rest == m2, lane, float(LANES)), axis=1, keepdims=True)
        e2 = jnp.exp(m2 - m1)
        w1 = 1.0 / (1.0 + e2)
        comb_ref[...] = (jnp.where(lane == ROUTE_E1, i1, 0.0) + jnp.where(lane == ROUTE_E2, i2, 0.0)
                         + jnp.where(lane == ROUTE_W1, w1, 0.0)
                         + jnp.where(lane == ROUTE_W2, e2 * w1, 0.0))


def _outproj(x, o_f, o_b, p, mla, mods, row_fn, gg, wo, ln2, router, tm):
    t, d = x.shape
    with_router = router is not None
    full = lambda a: pl.BlockSpec(a.shape, lambda i: (0,) * a.ndim)
    in_specs = [
        pl.BlockSpec((tm, d), lambda i: (i, 0)),
        pl.BlockSpec((tm, GLA_V_W), lambda i: (i, 0)),
        pl.BlockSpec((tm, GLA_V_W), lambda i: (i, 0)),
        pl.BlockSpec((tm, GLA_V_W), lambda i: (i, P_R // GLA_V_W)),
        pl.BlockSpec((tm, MLA_V_W), lambda i: (i, 0)),
        pl.BlockSpec((1, 6, d), lambda i: (row_fn(i), 0, 0)),
        full(gg), full(wo), full(ln2),
    ]
    args = [x, o_f, o_b, p, mla, mods, gg, wo, ln2]
    out_specs = [pl.BlockSpec((tm, d), lambda i: (i, 0)), pl.BlockSpec((tm, d), lambda i: (i, 0))]
    out_shape = [jax.ShapeDtypeStruct((t, d), F32),
                 jax.ShapeDtypeStruct((t, d), F32 if with_router else BF16)]
    if with_router:
        in_specs += [full(router[0]), full(router[1])]
        args += list(router)
        out_specs.append(pl.BlockSpec((tm, LANES), lambda i: (i, 0)))
        out_shape.append(jax.ShapeDtypeStruct((t, LANES), F32))
    return pl.pallas_call(
        functools.partial(_outproj_kernel, with_router=with_router),
        grid=(t // tm,),
        in_specs=in_specs,
        out_specs=out_specs,
        out_shape=out_shape,
        compiler_params=_cparams(("arbitrary",)),
        name="outproj",
    )(*args)


def _ffn_kernel(*refs, with_comb, final_norm):
    h_ref, x1_ref, mod_ref = refs[:3]
    k = 3
    comb_ref = fin_ref = None
    if with_comb:
        comb_ref = refs[k]
        k += 1
    wg_ref, wu_ref, wd_ref = refs[k:k + 3]
    k += 3
    if final_norm:
        fin_ref = refs[k]
        k += 1
    o_ref, acc = refs[k:]
    e = pl.program_id(1)
    f = pl.program_id(2)

    @pl.when((e == 0) & (f == 0))
    def _():
        acc[...] = jnp.zeros(acc.shape, F32)

    h = h_ref[...]
    a = _dot(h, wg_ref[0])
    u = _dot(h, wu_ref[0])
    act = _silu(a) * u
    if with_comb:
        comb = comb_ref[...]
        lane = lax.broadcasted_iota(jnp.int32, comb.shape, 1)
        act = act * jnp.sum(jnp.where(lane == e, comb, 0.0), axis=1, keepdims=True)
    acc[...] += _dot(act.astype(BF16), wd_ref[0])

    @pl.when((e == pl.num_programs(1) - 1) & (f == pl.num_programs(2) - 1))
    def _():
        x2 = x1_ref[...] + mod_ref[0][5:6] * acc[...]
        if final_norm:
            x2 = _rms(x2, fin_ref[...])
        o_ref[...] = x2


def _ffn(h2, x1, mods, row_fn, comb, wg, wu, wd, fin_g, tm, tf):
    t, d = x1.shape
    n_e, _, ff = wg.shape
    with_comb = comb is not None
    final_norm = fin_g is not None
    in_specs = [
        pl.BlockSpec((tm, d), lambda i, e, f: (i, 0)),
        pl.BlockSpec((tm, d), lambda i, e, f: (i, 0)),
        pl.BlockSpec((1, 6, d), lambda i, e, f: (row_fn(i), 0, 0)),
    ]
    args = [h2, x1, mods]
    if with_comb:
        in_specs.append(pl.BlockSpec((tm, LANES), lambda i, e, f: (i, 0)))
        args.append(comb)
    in_specs += [
        pl.BlockSpec((1, d, tf), lambda i, e, f: (e, 0, f)),
        pl.BlockSpec((1, d, tf), lambda i, e, f: (e, 0, f)),
        pl.BlockSpec((1, tf, d), lambda i, e, f: (e, f, 0)),
    ]
    args += [wg, wu, wd]
    if final_norm:
        in_specs.append(pl.BlockSpec((1, d), lambda i, e, f: (0, 0)))
        args.append(fin_g)
    return pl.pallas_call(
        functools.partial(_ffn_kernel, with_comb=with_comb, final_norm=final_norm),
        grid=(t // tm, n_e, ff // tf),
        in_specs=in_specs,
        out_specs=pl.BlockSpec((tm, d), lambda i, e, f: (i, 0)),
        out_shape=jax.ShapeDtypeStruct((t, d), F32),
        scratch_shapes=[pltpu.VMEM((tm, d), F32)],
        compiler_params=_cparams(("arbitrary", "arbitrary", "arbitrary")),
        name="ffn",
    )(*args)


def _sc_row_gather(table, idx):
    _, w = table.shape
    b = idx.shape[0]
    n_workers = SC_CORES * SC_SUBCORES
    assert b % (n_workers * SC_GATHER_ROWS) == 0, (b, n_workers, SC_GATHER_ROWS)
    b_per_w = b // n_workers
    n_chunks = b_per_w // SC_GATHER_ROWS
    mesh = plsc.VectorSubcoreMesh(core_axis_name="c", subcore_axis_name="s",
                                  num_cores=SC_CORES, num_subcores=SC_SUBCORES)

    def body(table_hbm, idx_hbm, out_hbm, idx_v, rows_v, sem):
        wid = lax.axis_index("s") * SC_CORES + lax.axis_index("c")
        base = wid * b_per_w

        @pl.loop(0, n_chunks)
        def _(ci):
            off = base + ci * SC_GATHER_ROWS
            pltpu.sync_copy(idx_hbm.at[pl.ds(off, SC_GATHER_ROWS)], idx_v)
            pltpu.async_copy(table_hbm.at[idx_v], rows_v, sem).wait()
            pltpu.sync_copy(rows_v, out_hbm.at[pl.ds(off, SC_GATHER_ROWS)])

    return pl.kernel(
        body,
        out_type=jax.ShapeDtypeStruct((b, w), F32),
        mesh=mesh,
        scratch_types=[pltpu.VMEM((SC_GATHER_ROWS,), jnp.int32),
                       pltpu.VMEM((SC_GATHER_ROWS, w), F32),
                       pltpu.SemaphoreType.DMA],
        name="sc_row_gather",
    )(table, idx)


def _moe_plan(route, n_tiles):
    t = route.shape[0]
    e = route[:, ROUTE_E1:ROUTE_E2 + 1].astype(jnp.int32)
    onehot = jnp.sum((e[:, :, None] == jnp.arange(N_EXPERTS, dtype=jnp.int32)).astype(jnp.int32), axis=1)
    csum = jnp.cumsum(onehot, axis=0)
    tiles_per = (csum[-1] + MOE_TILE - 1) // MOE_TILE
    tile_end = jnp.cumsum(tiles_per)
    row_off = (tile_end - tiles_per) * MOE_TILE
    rank = jnp.take_along_axis(csum - onehot, e, axis=1)
    pos = (row_off[e] + rank).T.reshape(-1)
    tok = jnp.tile(jnp.arange(t, dtype=jnp.int32), 2)
    src = jnp.zeros((n_tiles * MOE_TILE,), jnp.int32).at[pos].set(tok)
    n_used = tile_end[-1]
    tile_ids = jnp.minimum(jnp.arange(n_tiles, dtype=jnp.int32), n_used - 1)
    tile_expert = jnp.searchsorted(tile_end, tile_ids, side="right").astype(jnp.int32)
    return src, pos.astype(jnp.int32), tile_expert, n_used.reshape(1).astype(jnp.int32)


def _moe_ffn_kernel(te_ref, nused_ref, xs_ref, wg_ref, wu_ref, wd_ref, o_ref, acc):
    i = pl.program_id(0)
    f = pl.program_id(1)
    last_f = pl.num_programs(1) - 1
    used = i < nused_ref[0]

    @pl.when(used)
    def _():
        @pl.when(f == 0)
        def _():
            acc[...] = jnp.zeros(acc.shape, F32)

        h = xs_ref[...].astype(BF16)
        act = _silu(_dot(h, wg_ref[0].astype(BF16))) * _dot(h, wu_ref[0].astype(BF16))
        acc[...] += _dot(act.astype(BF16), wd_ref[0].astype(BF16))

        @pl.when(f == last_f)
        def _():
            o_ref[...] = acc[...]

    @pl.when(jnp.logical_not(used) & (f == last_f))
    def _():
        o_ref[...] = jnp.zeros(o_ref.shape, F32)


def _moe_ffn(xs, tile_expert, n_used, wg, wu, wd, tf):
    rows, d = xs.shape
    ff = wg.shape[2]
    n_tiles = rows // MOE_TILE
    grid_spec = pltpu.PrefetchScalarGridSpec(
        num_scalar_prefetch=2,
        grid=(n_tiles, ff // tf),
        in_specs=[
            pl.BlockSpec((MOE_TILE, d), lambda i, f, te, nu: (i, 0)),
            pl.BlockSpec((1, d, tf), lambda i, f, te, nu: (te[i], 0, f)),
            pl.BlockSpec((1, d, tf), lambda i, f, te, nu: (te[i], 0, f)),
            pl.BlockSpec((1, tf, d), lambda i, f, te, nu: (te[i], f, 0)),
        ],
        out_specs=pl.BlockSpec((MOE_TILE, d), lambda i, f, te, nu: (i, 0)),
        scratch_shapes=[pltpu.VMEM((MOE_TILE, d), F32)],
    )
    return pl.pallas_call(
        _moe_ffn_kernel,
        grid_spec=grid_spec,
        out_shape=jax.ShapeDtypeStruct((rows, d), F32),
        compiler_params=_cparams(("arbitrary", "arbitrary")),
        name="moe_ffn",
    )(tile_expert, n_used, xs, wg, wu, wd)


def _combine_kernel(*refs, final_norm):
    x1_ref, y0_ref, y1_ref, route_ref, mod_ref = refs[:5]
    fin_ref = refs[5] if final_norm else None
    o_ref = refs[-1]
    route = route_ref[...]
    w1 = route[:, ROUTE_W1:ROUTE_W1 + 1]
    w2 = route[:, ROUTE_W2:ROUTE_W2 + 1]
    x2 = x1_ref[...] + mod_ref[0][5:6] * (w1 * y0_ref[...] + w2 * y1_ref[...])
    if final_norm:
        x2 = _rms(x2, fin_ref[...])
    o_ref[...] = x2


def _combine(x1, yg, route, mods, row_fn, fin_g, tm):
    t, d = x1.shape
    nb = t // tm
    final_norm = fin_g is not None
    in_specs = [
        pl.BlockSpec((tm, d), lambda i: (i, 0)),
        pl.BlockSpec((tm, d), lambda i: (i, 0)),
        pl.BlockSpec((tm, d), lambda i: (i + nb, 0)),
        pl.BlockSpec((tm, LANES), lambda i: (i, 0)),
        pl.BlockSpec((1, 6, d), lambda i: (row_fn(i), 0, 0)),
    ]
    args = [x1, yg, yg, route, mods]
    if final_norm:
        in_specs.append(pl.BlockSpec((1, d), lambda i: (0, 0)))
        args.append(fin_g)
    return pl.pallas_call(
        functools.partial(_combine_kernel, final_norm=final_norm),
        grid=(nb,),
        in_specs=in_specs,
        out_specs=pl.BlockSpec((tm, d), lambda i: (i, 0)),
        out_shape=jax.ShapeDtypeStruct((t, d), F32),
        compiler_params=_cparams(("arbitrary",)),
        name="moe_combine",
    )(*args)


def _moe(h2, x1, route, mods, row_fn, wg, wu, wd, fin_g, tm):
    t = h2.shape[0]
    quantum = SC_CORES * SC_SUBCORES * SC_GATHER_ROWS // MOE_TILE
    n_tiles = -(-(2 * t // MOE_TILE + N_EXPERTS) // quantum) * quantum
    src, pos, tile_expert, n_used = _moe_plan(route, n_tiles)
    xs = _sc_row_gather(h2, src)
    ff = wg.shape[2]
    ys = _moe_ffn(xs, tile_expert, n_used, wg, wu, wd, MOE_FF_TILE if ff % MOE_FF_TILE == 0 else ff)
    yg = _sc_row_gather(ys, pos)
    return _combine(x1, yg, route, mods, row_fn, fin_g, tm)


def _rope_partner():
    j = np.arange(MLA_ROPE)
    return np.where((j % 32) < 16, j + 16, j - 16)


def _prep_in_weight(w):
    d = w.shape[0]
    cols = [w[:, 0:1024], w[:, 1056:1568], w[:, 1568:1824], w[:, 1824:1952], w[:, 1952:2016],
            w[:, 1024:1056], jnp.zeros((d, P_WIDTH - 2016), w.dtype)]
    return jnp.concatenate(cols, axis=1).astype(BF16)


def _prep_gate_weight(w_g2, b_g2):
    outs = []
    for z, off in ((0, MISC_GF), (1, MISC_GB)):
        wz = jnp.zeros((LANES, GLA_QK_W), F32).at[off:off + GLA_GATE_RANK].set(w_g2[z])
        outs.append((wz.astype(BF16), b_g2[z].reshape(1, GLA_QK_W)))
    return outs


def _prep_mla_weights(w_uq, w_ukv):
    partner = _rope_partner()
    wq = w_uq.reshape(MLA_Q_RANK, MLA_HEADS, MLA_QK)
    wqn = wq[:, :, :MLA_NOPE].reshape(MLA_Q_RANK, MLA_HEADS * MLA_NOPE)
    rope = wq[:, :, MLA_NOPE:]
    pad = jnp.zeros((MLA_Q_RANK, MLA_HEADS, LANES - MLA_ROPE), w_uq.dtype)
    wqr = jnp.concatenate([rope, pad], axis=2).reshape(MLA_Q_RANK, MLA_HEADS * LANES)
    wqs = jnp.concatenate([rope[:, :, partner], pad], axis=2).reshape(MLA_Q_RANK, MLA_HEADS * LANES)
    wkv = w_ukv.reshape(MLA_KV_RANK, MLA_HEADS, MLA_NOPE + MLA_V)
    wknt = wkv[:, :, :MLA_NOPE].reshape(MLA_KV_RANK, MLA_HEADS * MLA_NOPE).T
    wv = wkv[:, :, MLA_NOPE:].reshape(MLA_KV_RANK, MLA_HEADS * MLA_V)
    perm = np.zeros((LANES, LANES), np.float32)
    perm[partner, np.arange(MLA_ROPE)] = 1.0
    eye = np.eye(MLA_ROPE, LANES, dtype=np.float32)
    return (wqn.astype(BF16), wqr.astype(BF16), wqs.astype(BF16), wknt.astype(BF16),
            wv.astype(BF16), jnp.asarray(perm, BF16), jnp.asarray(eye, BF16))


def _rope_tables(n_tok):
    rows = n_tok // GRID_W
    row = jnp.repeat(jnp.arange(rows, dtype=F32), GRID_W)
    col = jnp.tile(jnp.arange(GRID_W, dtype=F32), rows)
    nfreq = MLA_ROPE // 4
    inv = ROPE_BASE ** (-jnp.arange(nfreq, dtype=F32) / nfreq)
    ar = row[:, None] * inv
    ac = col[:, None] * inv
    zero = jnp.zeros((n_tok, LANES - MLA_ROPE), F32)
    cos = jnp.concatenate([jnp.cos(ar), jnp.cos(ar), jnp.cos(ac), jnp.cos(ac), zero], axis=1)
    sin = jnp.concatenate([-jnp.sin(ar), jnp.sin(ar), -jnp.sin(ac), jnp.sin(ac), zero], axis=1)
    return cos, sin


def _identity_tables(n_tok):
    cos = jnp.concatenate([jnp.ones((n_tok, MLA_ROPE), F32),
                           jnp.zeros((n_tok, LANES - MLA_ROPE), F32)], axis=1)
    return cos, jnp.zeros((n_tok, LANES), F32)


def _pick_tile(n, pref):
    t = min(n, pref)
    while n % t:
        t //= 2
    return t


def _pick_ff_tile(ff):
    best = LANES
    for m in range(1, ff // LANES + 1):
        if ff % (m * LANES) == 0 and m * LANES <= 1408:
            best = m * LANES
    return best


@jax.jit
def _forward(x, c, ctx, c_ctx, w_mod, b_mod, ln1_g, ln2_g, w_in, w_gla_g2, b_gla_g2, gla_norm_g,
             mla_q_norm_g, w_uq, mla_kv_norm_g, w_ukv, w_out, ffn_w_gate, ffn_w_up, ffn_w_down,
             router_w, exp_w_gate, exp_w_up, exp_w_down, final_norm_g):
    batch, seq, d = x.shape
    n_ctx = ctx.shape[1]
    depth = w_mod.shape[0]

    cvec = jnp.zeros((8, d), F32).at[:batch].set(c).at[batch].set(c_ctx)
    mods_all = _modulation(cvec, w_mod, b_mod).reshape(depth, 8, 6, d)

    xl = x.reshape(batch * seq, d)
    xc = ctx.reshape(batch * n_ctx, d)

    tm_l = _pick_tile(seq, 512)
    tm_c = _pick_tile(n_ctx, 256)
    tk_l = _pick_tile(seq, 1024)
    cb_l = _pick_tile(seq, 256)
    cb_c = _pick_tile(n_ctx, 256)
    row_l = lambda tm: (lambda i: i // (seq // tm))
    row_c = lambda i: batch

    rope_l = _rope_tables(seq)
    rope_c = _identity_tables(tm_c)
    zero_state = jnp.zeros((batch, GLA_QK_W, GLA_DV), F32)

    for i in range(depth):
        need_ctx = i < depth - 1
        last = i == depth - 1
        mods = mods_all[i]
        ln1 = ln1_g[i].reshape(1, d)
        ln2 = ln2_g[i].reshape(1, d)
        w_in_r = _prep_in_weight(w_in[i])
        gates = _prep_gate_weight(w_gla_g2[i], b_gla_g2[i])
        mla_w = _prep_mla_weights(w_uq[i], w_ukv[i])
        qg = mla_q_norm_g[i].reshape(1, MLA_Q_RANK)
        kvg = mla_kv_norm_g[i].reshape(1, MLA_KV_RANK)
        gg = gla_norm_g[i].reshape(1, GLA_DV)
        wo = w_out[i].astype(BF16)

        p_l = _inproj(xl, mods, row_l(tm_l), ln1, w_in_r, tm_l)
        p_c = _inproj(xc, mods, row_c, ln1, w_in_r, tm_c)

        oc_f, sc_f = _gla(p_c, *gates[0], zero_state, batch=batch, reverse=False, cb=cb_c)
        oc_b, sc_b = _gla(p_c, *gates[1], zero_state, batch=batch, reverse=True, cb=cb_c)
        ol_f, _ = _gla(p_l, *gates[0], sc_f, batch=batch, reverse=False, cb=cb_l)
        ol_b, _ = _gla(p_l, *gates[1], sc_b, batch=batch, reverse=True, cb=cb_l)

        q_l, kt_l, v_l = _mlaprep(p_l, *rope_l, qg, kvg, mla_w, batch=batch, tm=tk_l)
        q_c, kt_c, v_c = _mlaprep(p_c, *rope_c, qg, kvg, mla_w, batch=batch, tm=tm_c)
        m_l = _attention_pipelined(q_l, kt_l, v_l, kt_c, v_c, tq=tk_l, n_sub=max(1, tk_l // 512))
        m_l = m_l.reshape(batch * seq, MLA_V_W)

        if i % 2 == 0:
            j = i // 2
            router = None
            wg = ffn_w_gate[j][None].astype(BF16)
            wu = ffn_w_up[j][None].astype(BF16)
            wd = ffn_w_down[j][None].astype(BF16)
        else:
            j = i // 2
            rw = jnp.zeros((d, LANES), F32).at[:, :N_EXPERTS].set(router_w[j])
            rw_hi = rw.astype(BF16)
            router = (rw_hi, (rw - rw_hi.astype(F32)).astype(BF16))
            wg, wu, wd = exp_w_gate[j], exp_w_up[j], exp_w_down[j]
        tf = _pick_ff_tile(wg.shape[2])
        fin = final_norm_g.reshape(1, d) if last else None

        outs = _outproj(xl, ol_f, ol_b, p_l, m_l, mods, row_l(tm_l), gg, wo, ln2, router, tm_l)
        tm_f = _pick_tile(seq, 512)
        if router is None:
            xl = _ffn(outs[1], outs[0], mods, row_l(tm_f), None, wg, wu, wd, fin, tm_f, tf)
        else:
            xl = _moe(outs[1], outs[0], outs[2], mods, row_l(tm_f), wg, wu, wd, fin, tm_f)

        if need_ctx:
            m_c = _attention(q_c, [(kt_c, v_c)], tq=tm_c, n_sub=1).reshape(batch * n_ctx, MLA_V_W)
            outs_c = _outproj(xc, oc_f, oc_b, p_c, m_c, mods, row_c, gg, wo, ln2, router, tm_c)
            h2_c, comb_c = outs_c[1], None
            if router is not None:
                r_c = outs_c[2]
                lane = jnp.arange(LANES, dtype=F32)[None, :]
                comb_c = (jnp.where(lane == r_c[:, ROUTE_E1:ROUTE_E1 + 1], r_c[:, ROUTE_W1:ROUTE_W1 + 1], 0.0)
                          + jnp.where(lane == r_c[:, ROUTE_E2:ROUTE_E2 + 1], r_c[:, ROUTE_W2:ROUTE_W2 + 1], 0.0))
                h2_c = h2_c.astype(BF16)
            xc = _ffn(h2_c, outs_c[0], mods, row_c, comb_c, wg.astype(BF16), wu.astype(BF16),
                      wd.astype(BF16), None, _pick_tile(batch * n_ctx, 512), tf)

    return xl.reshape(batch, seq, d)


def kernel(x, c, ctx, c_ctx, w_mod, b_mod, ln1_g, ln2_g, w_in, w_gla_g2, b_gla_g2, gla_norm_g,
           mla_q_norm_g, w_uq, mla_kv_norm_g, w_ukv, w_out, ffn_w_gate, ffn_w_up, ffn_w_down,
           router_w, exp_w_gate, exp_w_up, exp_w_down, final_norm_g):
    return _forward(x, c, ctx, c_ctx, w_mod, b_mod, ln1_g, ln2_g, w_in, w_gla_g2, b_gla_g2,
                    gla_norm_g, mla_q_norm_g, w_uq, mla_kv_norm_g, w_ukv, w_out, ffn_w_gate,
                    ffn_w_up, ffn_w_down, router_w, exp_w_gate, exp_w_up, exp_w_down, final_norm_g)
```

```python
import functools

import numpy as np
import jax
import jax.numpy as jnp
from jax import lax
from jax.experimental import pallas as pl
from jax.experimental.pallas import tpu as pltpu
from jax.experimental.pallas import tpu_sc as plsc

F32 = jnp.float32
BF16 = jnp.bfloat16

D_MODEL = 1024
EPS = 1e-6
GRID_W = 64

GLA_HEADS = 4
GLA_DK = 64
GLA_DV = 128
GLA_GATE_RANK = 16
GLA_GATE_NORM = 16.0
GLA_CHUNK = 64
GLA_QK_W = GLA_HEADS * GLA_DK
GLA_V_W = GLA_HEADS * GLA_DV
GLA_EXP_CLAMP = 80.0

MLA_HEADS = 4
MLA_NOPE = 128
MLA_ROPE = 64
MLA_V = 128
MLA_QK = MLA_NOPE + MLA_ROPE
MLA_Q_RANK = 256
MLA_KV_RANK = 128
MLA_SCALE = MLA_QK ** -0.5
MLA_Q_SCALE = MLA_SCALE * 1.4426950408889634
MLA_V_W = MLA_HEADS * MLA_V
MLA_V_EXT = 2 * MLA_V
ROPE_BASE = 10000.0

N_EXPERTS = 8
LANES = 128
ROUTE_E1, ROUTE_E2, ROUTE_W1, ROUTE_W2 = 0, 1, 2, 3

SC_CORES = 2
SC_SUBCORES = 16
SC_GATHER_ROWS = 64
MOE_TILE = 1024
MOE_FF_TILE = 512

P_Q, P_K, P_V, P_R, P_CQ, P_CKV, P_MISC = 0, 256, 512, 1024, 1536, 1792, 1920
P_WIDTH = 2048
MISC_KR, MISC_GF, MISC_GB = 0, 64, 80

VMEM_LIMIT = 56 * 1024 * 1024


def _cparams(sem):
    return pltpu.CompilerParams(dimension_semantics=sem, vmem_limit_bytes=VMEM_LIMIT)


def _rms(x, g):
    return x * lax.rsqrt(jnp.mean(x * x, axis=-1, keepdims=True) + EPS) * g


def _silu(x):
    return x / (1.0 + jnp.exp(-x))


def _dot(a, b):
    return jnp.dot(a, b, preferred_element_type=F32)


def _dot_nt(a, b):
    return lax.dot_general(a, b, (((1,), (1,)), ((), ())), preferred_element_type=F32)


def _dot_tn(a, b):
    return lax.dot_general(a, b, (((0,), (0,)), ((), ())), preferred_element_type=F32)


def _mod_kernel(c_ref, w_ref, b_ref, o_ref):
    s = _silu(c_ref[...]).astype(BF16)
    o_ref[0] = _dot(s, w_ref[0].astype(BF16)) + b_ref[0]


def _modulation(cvec, w_mod, b_mod):
    depth, d, n = w_mod.shape
    tn = 1536
    return pl.pallas_call(
        _mod_kernel,
        grid=(depth, n // tn),
        in_specs=[
            pl.BlockSpec((8, d), lambda l, j: (0, 0)),
            pl.BlockSpec((1, d, tn), lambda l, j: (l, 0, j)),
            pl.BlockSpec((1, 1, tn), lambda l, j: (l, 0, j)),
        ],
        out_specs=pl.BlockSpec((1, 8, tn), lambda l, j: (l, 0, j)),
        out_shape=jax.ShapeDtypeStruct((depth, 8, n), F32),
        compiler_params=_cparams(("arbitrary", "arbitrary")),
        name="modulation",
    )(cvec, w_mod, b_mod.reshape(depth, 1, n))


def _inproj_kernel(x_ref, mod_ref, g_ref, w_ref, o_ref):
    m = mod_ref[0]
    h = _rms(x_ref[...], g_ref[...]) * (1.0 + m[1:2]) + m[0:1]
    o_ref[...] = _dot(h.astype(BF16), w_ref[...]).astype(BF16)


def _inproj(x, mods, row_fn, ln_g, w, tm):
    t, d = x.shape
    return pl.pallas_call(
        _inproj_kernel,
        grid=(t // tm,),
        in_specs=[
            pl.BlockSpec((tm, d), lambda i: (i, 0)),
            pl.BlockSpec((1, 6, d), lambda i: (row_fn(i), 0, 0)),
            pl.BlockSpec((1, d), lambda i: (0, 0)),
            pl.BlockSpec((d, P_WIDTH), lambda i: (0, 0)),
        ],
        out_specs=pl.BlockSpec((tm, P_WIDTH), lambda i: (i, 0)),
        out_shape=jax.ShapeDtypeStruct((t, P_WIDTH), BF16),
        compiler_params=_cparams(("arbitrary",)),
        name="inproj",
    )(x, mods, ln_g, w)


def _gla_kernel(q_ref, k_ref, v_ref, misc_ref, wg_ref, bg_ref, tri_ref, s0_ref,
                o_ref, sfin_ref, s_scr, *, reverse, n_chunks):
    blk = pl.program_id(1)
    c_len = GLA_CHUNK

    @pl.when(blk == 0)
    def _():
        s_scr[...] = s0_ref[0]

    pre = _dot(misc_ref[...], wg_ref[...]) + bg_ref[...]
    g = (jnp.minimum(pre, 0.0) - jnp.log(1.0 + jnp.exp(-jnp.abs(pre)))) * (1.0 / GLA_GATE_NORM)
    g_hi = g.astype(BF16)
    g_lo = (g - g_hi.astype(F32)).astype(BF16)
    tri = tri_ref[...]
    cum = _dot(tri, g_hi) + _dot(tri, g_lo)

    lane = lax.broadcasted_iota(jnp.int32, (c_len, GLA_QK_W), 1)
    head_masks = [(lane >= h * GLA_DK) & (lane < (h + 1) * GLA_DK) for h in range(GLA_HEADS)]
    row = lax.broadcasted_iota(jnp.int32, (GLA_HEADS * c_len, c_len), 0) % c_len
    col = lax.broadcasted_iota(jnp.int32, (GLA_HEADS * c_len, c_len), 1)
    pair_mask = (col >= row) if reverse else (col <= row)
    ones = jnp.ones((c_len, GLA_DV), BF16)

    def stack_heads(a):
        return jnp.concatenate([jnp.where(mk, a, 0.0) for mk in head_masks], axis=0).astype(BF16)

    order = range(n_chunks - 1, -1, -1) if reverse else range(n_chunks)
    for c in order:
        sl = slice(c * c_len, (c + 1) * c_len)
        xc = cum[sl]
        tot = xc[0:1] if reverse else xc[c_len - 1:c_len]
        ref = xc[c_len // 2:c_len // 2 + 1]
        qc = q_ref[sl, :].astype(F32) * (GLA_DK ** -0.5)
        kc = k_ref[sl, :].astype(F32)
        vc = v_ref[sl, :]
        q_mid = qc * jnp.exp(jnp.minimum(xc - ref, GLA_EXP_CLAMP))
        k_mid = (kc * jnp.exp(jnp.minimum(ref - xc, GLA_EXP_CLAMP))).astype(BF16)
        q_dec = qc * jnp.exp(xc)
        k_dec = (kc * jnp.exp(tot - xc)).astype(BF16)

        attn = _dot_nt(stack_heads(q_mid), k_mid)
        attn = jnp.where(pair_mask, attn, 0.0).astype(BF16)
        s_prev = s_scr[...]
        o_inter = _dot(stack_heads(q_dec), s_prev.astype(BF16))
        for h in range(GLA_HEADS):
            rs = slice(h * c_len, (h + 1) * c_len)
            vs = slice(h * GLA_DV, (h + 1) * GLA_DV)
            o_h = o_inter[rs] + _dot(attn[rs], vc[:, vs])
            o_ref[sl, vs] = o_h.astype(BF16)

        kv_full = _dot_tn(k_dec, vc)
        kv = jnp.concatenate(
            [kv_full[h * GLA_DK:(h + 1) * GLA_DK, h * GLA_DV:(h + 1) * GLA_DV]
             for h in range(GLA_HEADS)], axis=0)
        tot_col = _dot_tn(g_hi[sl], ones) + _dot_tn(g_lo[sl], ones)
        s_scr[...] = s_prev * jnp.exp(tot_col) + kv

    @pl.when(blk == pl.num_programs(1) - 1)
    def _():
        sfin_ref[0] = s_scr[...]


def _block_diag_tri(n_chunks, upper):
    c = GLA_CHUNK
    t = np.triu(np.ones((c, c), np.float32)) if upper else np.tril(np.ones((c, c), np.float32))
    return jnp.asarray(np.kron(np.eye(n_chunks, dtype=np.float32), t), dtype=BF16)


def _gla(p, wg, bg, s0, *, batch, reverse, cb):
    t_all = p.shape[0]
    nblk = t_all // batch // cb
    n_chunks = cb // GLA_CHUNK

    def tok(b, i):
        return b * nblk + ((nblk - 1 - i) if reverse else i)

    kern = functools.partial(_gla_kernel, reverse=reverse, n_chunks=n_chunks)
    return pl.pallas_call(
        kern,
        grid=(batch, nblk),
        in_specs=[
            pl.BlockSpec((cb, GLA_QK_W), lambda b, i: (tok(b, i), P_Q // GLA_QK_W)),
            pl.BlockSpec((cb, GLA_QK_W), lambda b, i: (tok(b, i), P_K // GLA_QK_W)),
            pl.BlockSpec((cb, GLA_V_W), lambda b, i: (tok(b, i), P_V // GLA_V_W)),
            pl.BlockSpec((cb, LANES), lambda b, i: (tok(b, i), P_MISC // LANES)),
            pl.BlockSpec((LANES, GLA_QK_W), lambda b, i: (0, 0)),
            pl.BlockSpec((1, GLA_QK_W), lambda b, i: (0, 0)),
            pl.BlockSpec((cb, cb), lambda b, i: (0, 0)),
            pl.BlockSpec((1, GLA_QK_W, GLA_DV), lambda b, i: (b, 0, 0)),
        ],
        out_specs=[
            pl.BlockSpec((cb, GLA_V_W), lambda b, i: (tok(b, i), 0)),
            pl.BlockSpec((1, GLA_QK_W, GLA_DV), lambda b, i: (b, 0, 0)),
        ],
        out_shape=[
            jax.ShapeDtypeStruct((t_all, GLA_V_W), BF16),
            jax.ShapeDtypeStruct((batch, GLA_QK_W, GLA_DV), F32),
        ],
        scratch_shapes=[pltpu.VMEM((GLA_QK_W, GLA_DV), F32)],
        compiler_params=_cparams(("arbitrary", "arbitrary")),
        name="gla_bwd" if reverse else "gla_fwd",
    )(p, p, p, p, wg, bg, _block_diag_tri(n_chunks, reverse), s0)


def _mlaprep_kernel(cq_ref, ckv_ref, misc_ref, cos_ref, sin_ref, qg_ref, kvg_ref,
                    wqn_ref, wqr_ref, wqs_ref, wknt_ref, wv_ref, perm_ref, eye_ref,
                    q_ref, kt_ref, v_ref):
    cos = cos_ref[...]
    sin = sin_ref[...]
    cqn = _rms(cq_ref[...].astype(F32), qg_ref[...]).astype(BF16)
    qn = _dot(cqn, wqn_ref[...])
    qr = _dot(cqn, wqr_ref[...])
    qs = _dot(cqn, wqs_ref[...])
    for h in range(MLA_HEADS):
        ls = slice(h * LANES, (h + 1) * LANES)
        q_ref[0, h, :, 0:MLA_NOPE] = (qn[:, ls] * MLA_Q_SCALE).astype(BF16)
        rot = qr[:, ls] * cos + qs[:, ls] * sin
        q_ref[0, h, :, MLA_NOPE:MLA_QK] = (rot[:, 0:MLA_ROPE] * MLA_Q_SCALE).astype(BF16)

    ckvn = _rms(ckv_ref[...].astype(F32), kvg_ref[...]).astype(BF16)
    knt = _dot_nt(wknt_ref[...], ckvn)
    vv = _dot(ckvn, wv_ref[...])
    misc = misc_ref[...]
    kr = misc.astype(F32) * cos + _dot(misc, perm_ref[...]) * sin
    krt = _dot_nt(eye_ref[...], kr.astype(BF16)).astype(BF16)
    for h in range(MLA_HEADS):
        kt_ref[0, h, 0, 0:MLA_NOPE, :] = knt[h * MLA_NOPE:(h + 1) * MLA_NOPE].astype(BF16)
        kt_ref[0, h, 0, MLA_NOPE:MLA_QK, :] = krt
        v_ref[0, h, :, 0:MLA_V] = vv[:, h * MLA_V:(h + 1) * MLA_V].astype(BF16)
        v_ref[0, h, :, MLA_V:MLA_V_EXT] = jnp.ones((vv.shape[0], MLA_V), BF16)


def _mlaprep(p, cos, sin, qg, kvg, wts, *, batch, tm):
    t_all = p.shape[0]
    t = t_all // batch
    nb = t // tm
    ntab = cos.shape[0] // tm
    wqn, wqr, wqs, wknt, wv, perm, eye = wts
    full = lambda a: pl.BlockSpec(a.shape, lambda b, i: (0,) * a.ndim)
    return pl.pallas_call(
        _mlaprep_kernel,
        grid=(batch, nb),
        in_specs=[
            pl.BlockSpec((tm, MLA_Q_RANK), lambda b, i: (b * nb + i, P_CQ // MLA_Q_RANK)),
            pl.BlockSpec((tm, MLA_KV_RANK), lambda b, i: (b * nb + i, P_CKV // MLA_KV_RANK)),
            pl.BlockSpec((tm, LANES), lambda b, i: (b * nb + i, P_MISC // LANES)),
            pl.BlockSpec((tm, LANES), lambda b, i: (i % ntab, 0)),
            pl.BlockSpec((tm, LANES), lambda b, i: (i % ntab, 0)),
            full(qg), full(kvg), full(wqn), full(wqr), full(wqs), full(wknt), full(wv),
            full(perm), full(eye),
        ],
        out_specs=[
            pl.BlockSpec((1, MLA_HEADS, tm, MLA_QK), lambda b, i: (b, 0, i, 0)),
            pl.BlockSpec((1, MLA_HEADS, 1, MLA_QK, tm), lambda b, i: (b, 0, i, 0, 0)),
            pl.BlockSpec((1, MLA_HEADS, tm, MLA_V_EXT), lambda b, i: (b, 0, i, 0)),
        ],
        out_shape=[
            jax.ShapeDtypeStruct((batch, MLA_HEADS, t, MLA_QK), BF16),
            jax.ShapeDtypeStruct((batch, MLA_HEADS, nb, MLA_QK, tm), BF16),
            jax.ShapeDtypeStruct((batch, MLA_HEADS, t, MLA_V_EXT), BF16),
        ],
        compiler_params=_cparams(("arbitrary", "arbitrary")),
        name="mlaprep",
    )(p, p, p, cos, sin, qg, kvg, wqn, wqr, wqs, wknt, wv, perm, eye)


def _attn_kernel(*refs, n_seg, n_sub):
    q_ref = refs[0]
    kt_refs = refs[1:1 + 2 * n_seg:2]
    v_refs = refs[2:2 + 2 * n_seg:2]
    o_ref = refs[1 + 2 * n_seg]
    m_scr, acc_scr = refs[2 + 2 * n_seg:]

    rows_per_sub = q_ref.shape[2] // n_sub
    m_scr[...] = jnp.full(m_scr.shape, -jnp.inf, F32)
    acc_scr[...] = jnp.zeros(acc_scr.shape, F32)

    for kt_ref, v_ref in zip(kt_refs, v_refs):
        n_blocks, tk = kt_ref.shape[2], kt_ref.shape[4]

        def step(j, carry, kt_ref=kt_ref, v_ref=v_ref, tk=tk):
            kt = kt_ref[0, 0, j]
            v_blk = v_ref[0, 0, pl.ds(pl.multiple_of(j * tk, tk), tk), :]
            for u in range(n_sub):
                rows = slice(u * rows_per_sub, (u + 1) * rows_per_sub)
                s = _dot(q_ref[0, 0, rows, :], kt)
                m_prev = m_scr[rows, :]
                m_next = jnp.maximum(m_prev, jnp.max(s, axis=1, keepdims=True))
                p = jnp.exp2((s - jnp.concatenate([m_next] * (tk // LANES), axis=1)).astype(BF16))
                alpha = jnp.exp2(m_prev - m_next)
                acc_scr[rows, :] = (jnp.concatenate([alpha] * (MLA_V_EXT // LANES), axis=1)
                                    * acc_scr[rows, :] + _dot(p, v_blk))
                m_scr[rows, :] = m_next
            return carry

        lax.fori_loop(0, n_blocks, step, 0)

    o_ref[0] = (acc_scr[:, 0:MLA_V] / acc_scr[:, MLA_V:MLA_V_EXT]).astype(BF16)


def _attn_pipe_kernel(q_ref, kt_ref, v_ref, ktt_ref, vt_ref, o_ref,
                      m_scr, acc_scr, s0_scr, s1_scr, st_scr, *, n_sub):
    n_blocks, tk = kt_ref.shape[2], kt_ref.shape[4]
    rows_per_sub = q_ref.shape[2] // n_sub
    subs = [slice(u * rows_per_sub, (u + 1) * rows_per_sub) for u in range(n_sub)]
    m_scr[...] = jnp.full(m_scr.shape, -jnp.inf, F32)
    acc_scr[...] = jnp.zeros(acc_scr.shape, F32)

    def scores(kt, s_ref):
        for rows in subs:
            s_ref[rows, :] = _dot(q_ref[0, 0, rows, :], kt)

    def softmax_pv(s_ref, v_blk):
        width = s_ref.shape[1]
        for rows in subs:
            s = s_ref[rows, :]
            m_prev = m_scr[rows, :]
            m_next = jnp.maximum(m_prev, jnp.max(s, axis=1, keepdims=True))
            p = jnp.exp2((s - jnp.concatenate([m_next] * (width // LANES), axis=1)).astype(BF16))
            alpha = jnp.exp2(m_prev - m_next)
            acc_scr[rows, :] = (jnp.concatenate([alpha] * (MLA_V_EXT // LANES), axis=1)
                                * acc_scr[rows, :] + _dot(p, v_blk))
            m_scr[rows, :] = m_next

    def v_main(j):
        return v_ref[0, 0, pl.ds(pl.multiple_of(j * tk, tk), tk), :]

    scores(kt_ref[0, 0, 0], s0_scr)
    n_pairs = (n_blocks - 1) // 2

    def pair(jj, carry):
        j = 2 * jj
        scores(kt_ref[0, 0, j + 1], s1_scr)
        softmax_pv(s0_scr, v_main(j))
        scores(kt_ref[0, 0, j + 2], s0_scr)
        softmax_pv(s1_scr, v_main(j + 1))
        return carry

    lax.fori_loop(0, n_pairs, pair, 0)
    if n_blocks - 2 * n_pairs == 2:
        scores(kt_ref[0, 0, n_blocks - 1], s1_scr)
        softmax_pv(s0_scr, v_main(n_blocks - 2))
        scores(ktt_ref[0, 0, 0], st_scr)
        softmax_pv(s1_scr, v_main(n_blocks - 1))
    else:
        scores(ktt_ref[0, 0, 0], st_scr)
        softmax_pv(s0_scr, v_main(n_blocks - 1))
    softmax_pv(st_scr, vt_ref[0, 0])

    o_ref[0] = (acc_scr[:, 0:MLA_V] / acc_scr[:, MLA_V:MLA_V_EXT]).astype(BF16)


def _attention_pipelined(q, kt, v, kt_tail, v_tail, *, tq, n_sub):
    b, h, t, dqk = q.shape
    tk, tt = kt.shape[4], kt_tail.shape[4]
    assert kt_tail.shape[2] == 1
    return pl.pallas_call(
        functools.partial(_attn_pipe_kernel, n_sub=n_sub),
        grid=(b, h, t // tq),
        in_specs=[
            pl.BlockSpec((1, 1, tq, dqk), lambda bi, hi, qi: (bi, hi, qi, 0)),
            pl.BlockSpec((1, 1) + kt.shape[2:], lambda bi, hi, qi: (bi, hi, 0, 0, 0)),
            pl.BlockSpec((1, 1) + v.shape[2:], lambda bi, hi, qi: (bi, hi, 0, 0)),
            pl.BlockSpec((1, 1) + kt_tail.shape[2:], lambda bi, hi, qi: (bi, hi, 0, 0, 0)),
            pl.BlockSpec((1, 1) + v_tail.shape[2:], lambda bi, hi, qi: (bi, hi, 0, 0)),
        ],
        out_specs=pl.BlockSpec((1, tq, MLA_V), lambda bi, hi, qi: (bi, qi, hi)),
        out_shape=jax.ShapeDtypeStruct((b, t, h * MLA_V), BF16),
        scratch_shapes=[pltpu.VMEM((tq, LANES), F32), pltpu.VMEM((tq, MLA_V_EXT), F32),
                        pltpu.VMEM((tq, tk), F32), pltpu.VMEM((tq, tk), F32),
                        pltpu.VMEM((tq, tt), F32)],
        compiler_params=_cparams(("arbitrary", "arbitrary", "arbitrary")),
        name="mla_attention_pipe",
    )(q, kt, v, kt_tail, v_tail)


def _attention(q, segs, *, tq, n_sub):
    b, h, t, dqk = q.shape
    in_specs = [pl.BlockSpec((1, 1, tq, dqk), lambda bi, hi, qi: (bi, hi, qi, 0))]
    args = [q]
    for kt, v in segs:
        in_specs.append(pl.BlockSpec((1, 1) + kt.shape[2:], lambda bi, hi, qi: (bi, hi, 0, 0, 0)))
        in_specs.append(pl.BlockSpec((1, 1) + v.shape[2:], lambda bi, hi, qi: (bi, hi, 0, 0)))
        args += [kt, v]
    return pl.pallas_call(
        functools.partial(_attn_kernel, n_seg=len(segs), n_sub=n_sub),
        grid=(b, h, t // tq),
        in_specs=in_specs,
        out_specs=pl.BlockSpec((1, tq, MLA_V), lambda bi, hi, qi: (bi, qi, hi)),
        out_shape=jax.ShapeDtypeStruct((b, t, h * MLA_V), BF16),
        scratch_shapes=[pltpu.VMEM((tq, LANES), F32), pltpu.VMEM((tq, MLA_V_EXT), F32)],
        compiler_params=_cparams(("arbitrary", "arbitrary", "arbitrary")),
        name="mla_attention",
    )(*args)


def _outproj_kernel(*refs, with_router):
    (x_ref, of_ref, ob_ref, r_ref, mla_ref, mod_ref, gg_ref, wo_ref, ln2_ref) = refs[:9]
    if with_router:
        rwh_ref, rwl_ref, x1_ref, h2_ref, comb_ref = refs[9:]
    else:
        x1_ref, h2_ref = refs[9:]
    m = mod_ref[0]
    o = of_ref[...].astype(F32) + ob_ref[...].astype(F32)
    gg = gg_ref[...]
    y = jnp.concatenate(
        [_rms(o[:, h * GLA_DV:(h + 1) * GLA_DV], gg) for h in range(GLA_HEADS)], axis=1)
    mix = (y * _silu(r_ref[...].astype(F32))).astype(BF16)
    yo = _dot(mix, wo_ref[0:GLA_V_W, :]) + _dot(mla_ref[...], wo_ref[GLA_V_W:, :])
    x1 = x_ref[...] + m[2:3] * yo
    x1_ref[...] = x1
    h2 = _rms(x1, ln2_ref[...]) * (1.0 + m[4:5]) + m[3:4]
    h2_ref[...] = h2.astype(h2_ref.dtype)
    if with_router:
        h_hi = h2.astype(BF16)
        h_lo = (h2 - h_hi.astype(F32)).astype(BF16)
        logits = _dot(h_hi, rwh_ref[...]) + _dot(h_lo, rwh_ref[...]) + _dot(h_hi, rwl_ref[...])
        lane = lax.broadcasted_iota(jnp.int32, logits.shape, 1).astype(F32)
        neg = jnp.float32(-jnp.inf)
        logits = jnp.where(lane < N_EXPERTS, logits, neg)
        m1 = jnp.max(logits, axis=1, keepdims=True)
        i1 = jnp.min(jnp.where(logits == m1, lane, float(LANES)), axis=1, keepdims=True)
        rest = jnp.where(lane == i1, neg, logits)
        m2 = jnp.max(rest, axis=1, keepdims=True)
        i2 = jnp.min(jnp.where(rest == m2, lane, float(LANES)), axis=1, keepdims=True)
        e2 = jnp.exp(m2 - m1)
        w1 = 1.0 / (1.0 + e2)
        comb_ref[...] = (jnp.where(lane == ROUTE_E1, i1, 0.0) + jnp.where(lane == ROUTE_E2, i2, 0.0)
                         + jnp.where(lane == ROUTE_W1, w1, 0.0)
                         + jnp.where(lane == ROUTE_W2, e2 * w1, 0.0))


def _outproj(x, o_f, o_b, p, mla, mods, row_fn, gg, wo, ln2, router, tm):
    t, d = x.shape
    with_router = router is not None
    full = lambda a: pl.BlockSpec(a.shape, lambda i: (0,) * a.ndim)
    in_specs = [
        pl.BlockSpec((tm, d), lambda i: (i, 0)),
        pl.BlockSpec((tm, GLA_V_W), lambda i: (i, 0)),
        pl.BlockSpec((tm, GLA_V_W), lambda i: (i, 0)),
        pl.BlockSpec((tm, GLA_V_W), lambda i: (i, P_R // GLA_V_W)),
        pl.BlockSpec((tm, MLA_V_W), lambda i: (i, 0)),
        pl.BlockSpec((1, 6, d), lambda i: (row_fn(i), 0, 0)),
        full(gg), full(wo), full(ln2),
    ]
    args = [x, o_f, o_b, p, mla, mods, gg, wo, ln2]
    out_specs = [pl.BlockSpec((tm, d), lambda i: (i, 0)), pl.BlockSpec((tm, d), lambda i: (i, 0))]
    out_shape = [jax.ShapeDtypeStruct((t, d), F32),
                 jax.ShapeDtypeStruct((t, d), F32 if with_router else BF16)]
    if with_router:
        in_specs += [full(router[0]), full(router[1])]
        args += list(router)
        out_specs.append(pl.BlockSpec((tm, LANES), lambda i: (i, 0)))
        out_shape.append(jax.ShapeDtypeStruct((t, LANES), F32))
    return pl.pallas_call(
        functools.partial(_outproj_kernel, with_router=with_router),
        grid=(t // tm,),
        in_specs=in_specs,
        out_specs=out_specs,
        out_shape=out_shape,
        compiler_params=_cparams(("arbitrary",)),
        name="outproj",
    )(*args)


def _ffn_kernel(*refs, with_comb, final_norm):
    h_ref, x1_ref, mod_ref = refs[:3]
    k = 3
    comb_ref = fin_ref = None
    if with_comb:
        comb_ref = refs[k]
        k += 1
    wg_ref, wu_ref, wd_ref = refs[k:k + 3]
    k += 3
    if final_norm:
        fin_ref = refs[k]
        k += 1
    o_ref, acc = refs[k:]
    e = pl.program_id(1)
    f = pl.program_id(2)

    @pl.when((e == 0) & (f == 0))
    def _():
        acc[...] = jnp.zeros(acc.shape, F32)

    h = h_ref[...]
    a = _dot(h, wg_ref[0])
    u = _dot(h, wu_ref[0])
    act = _silu(a) * u
    if with_comb:
        comb = comb_ref[...]
        lane = lax.broadcasted_iota(jnp.int32, comb.shape, 1)
        act = act * jnp.sum(jnp.where(lane == e, comb, 0.0), axis=1, keepdims=True)
    acc[...] += _dot(act.astype(BF16), wd_ref[0])

    @pl.when((e == pl.num_programs(1) - 1) & (f == pl.num_programs(2) - 1))
    def _():
        x2 = x1_ref[...] + mod_ref[0][5:6] * acc[...]
        if final_norm:
            x2 = _rms(x2, fin_ref[...])
        o_ref[...] = x2


def _ffn(h2, x1, mods, row_fn, comb, wg, wu, wd, fin_g, tm, tf):
    t, d = x1.shape
    n_e, _, ff = wg.shape
    with_comb = comb is not None
    final_norm = fin_g is not None
    in_specs = [
        pl.BlockSpec((tm, d), lambda i, e, f: (i, 0)),
        pl.BlockSpec((tm, d), lambda i, e, f: (i, 0)),
        pl.BlockSpec((1, 6, d), lambda i, e, f: (row_fn(i), 0, 0)),
    ]
    args = [h2, x1, mods]
    if with_comb:
        in_specs.append(pl.BlockSpec((tm, LANES), lambda i, e, f: (i, 0)))
        args.append(comb)
    in_specs += [
        pl.BlockSpec((1, d, tf), lambda i, e, f: (e, 0, f)),
        pl.BlockSpec((1, d, tf), lambda i, e, f: (e, 0, f)),
        pl.BlockSpec((1, tf, d), lambda i, e, f: (e, f, 0)),
    ]
    args += [wg, wu, wd]
    if final_norm:
        in_specs.append(pl.BlockSpec((1, d), lambda i, e, f: (0, 0)))
        args.append(fin_g)
    return pl.pallas_call(
        functools.partial(_ffn_kernel, with_comb=with_comb, final_norm=final_norm),
        grid=(t // tm, n_e, ff // tf),
        in_specs=in_specs,
        out_specs=pl.BlockSpec((tm, d), lambda i, e, f: (i, 0)),
        out_shape=jax.ShapeDtypeStruct((t, d), F32),
        scratch_shapes=[pltpu.VMEM((tm, d), F32)],
        compiler_params=_cparams(("arbitrary", "arbitrary", "arbitrary")),
        name="ffn",
    )(*args)


def _sc_row_gather(table, idx):
    _, w = table.shape
    b = idx.shape[0]
    n_workers = SC_CORES * SC_SUBCORES
    assert b % (n_workers * SC_GATHER_ROWS) == 0, (b, n_workers, SC_GATHER_ROWS)
    b_per_w = b // n_workers
    n_chunks = b_per_w // SC_GATHER_ROWS
    mesh = plsc.VectorSubcoreMesh(core_axis_name="c", subcore_axis_name="s",
                                  num_cores=SC_CORES, num_subcores=SC_SUBCORES)

    def body(table_hbm, idx_hbm, out_hbm, idx_v, rows_v, sem):
        wid = lax.axis_index("s") * SC_CORES + lax.axis_index("c")
        base = wid * b_per_w

        @pl.loop(0, n_chunks)
        def _(ci):
            off = base + ci * SC_GATHER_ROWS
            pltpu.sync_copy(idx_hbm.at[pl.ds(off, SC_GATHER_ROWS)], idx_v)
            pltpu.async_copy(table_hbm.at[idx_v], rows_v, sem).wait()
            pltpu.sync_copy(rows_v, out_hbm.at[pl.ds(off, SC_GATHER_ROWS)])

    return pl.kernel(
        body,
        out_type=jax.ShapeDtypeStruct((b, w), F32),
        mesh=mesh,
        scratch_types=[pltpu.VMEM((SC_GATHER_ROWS,), jnp.int32),
                       pltpu.VMEM((SC_GATHER_ROWS, w), F32),
                       pltpu.SemaphoreType.DMA],
        name="sc_row_gather",
    )(table, idx)


def _sc_row_scatter2(table, pos, n_out):
    t, w = table.shape
    n_workers = SC_CORES * SC_SUBCORES
    assert t % (n_workers * SC_GATHER_ROWS) == 0, (t, n_workers, SC_GATHER_ROWS)
    t_per_w = t // n_workers
    n_chunks = t_per_w // SC_GATHER_ROWS
    mesh = plsc.VectorSubcoreMesh(core_axis_name="c", subcore_axis_name="s",
                                  num_cores=SC_CORES, num_subcores=SC_SUBCORES)

    def body(table_hbm, pos_hbm, out_hbm, idx0_v, idx1_v, rows_v, sem):
        wid = lax.axis_index("s") * SC_CORES + lax.axis_index("c")
        base = wid * t_per_w

        @pl.loop(0, n_chunks)
        def _(ci):
            off = base + ci * SC_GATHER_ROWS
            pltpu.sync_copy(pos_hbm.at[0, pl.ds(off, SC_GATHER_ROWS)], idx0_v)
            pltpu.sync_copy(pos_hbm.at[1, pl.ds(off, SC_GATHER_ROWS)], idx1_v)
            pltpu.sync_copy(table_hbm.at[pl.ds(off, SC_GATHER_ROWS)], rows_v)
            first = pltpu.async_copy(rows_v, out_hbm.at[idx0_v], sem)
            second = pltpu.async_copy(rows_v, out_hbm.at[idx1_v], sem)
            first.wait()
            second.wait()

    return pl.kernel(
        body,
        out_type=jax.ShapeDtypeStruct((n_out, w), F32),
        mesh=mesh,
        scratch_types=[pltpu.VMEM((SC_GATHER_ROWS,), jnp.int32),
                       pltpu.VMEM((SC_GATHER_ROWS,), jnp.int32),
                       pltpu.VMEM((SC_GATHER_ROWS, w), F32),
                       pltpu.SemaphoreType.DMA],
        name="sc_row_scatter",
    )(table, pos)


def _moe_plan_kernel(route_ref, tri_ref, utri_ref, pos_ref, cnt_ref, run_scr, off_scr):
    phase = pl.program_id(0)
    blk = pl.program_id(1)
    route = route_ref[...]
    lane = lax.broadcasted_iota(jnp.int32, route.shape, 1).astype(F32)
    oh1 = jnp.where(lane == route[:, ROUTE_E1:ROUTE_E1 + 1], 1.0, 0.0)
    oh2 = jnp.where(lane == route[:, ROUTE_E2:ROUTE_E2 + 1], 1.0, 0.0)
    oh = oh1 + oh2

    @pl.when(blk == 0)
    def _():
        run_scr[...] = jnp.zeros(run_scr.shape, F32)

    @pl.when(phase == 0)
    def _():
        run_scr[...] += jnp.sum(oh, axis=0, keepdims=True)

        @pl.when(blk == pl.num_programs(1) - 1)
        def _():
            counts = run_scr[...]
            cnt_ref[...] = counts
            tiles_per = jnp.floor((counts + (MOE_TILE - 1.0)) * (1.0 / MOE_TILE))
            tile_end = _dot(jnp.broadcast_to(tiles_per, (8, LANES)).astype(BF16), utri_ref[...])[0:1]
            off_scr[...] = (tile_end - tiles_per) * float(MOE_TILE)

    @pl.when(phase == 1)
    def _():
        incl = _dot(tri_ref[...], oh.astype(BF16))
        before = incl - oh + run_scr[...] + off_scr[...]
        p1 = jnp.sum(before * oh1, axis=1, keepdims=True)
        p2 = jnp.sum(before * oh2, axis=1, keepdims=True)
        pos = jnp.where(lane == 0.0, p1, 0.0) + jnp.where(lane == 1.0, p2, 0.0)
        pos_ref[...] = pos.astype(jnp.int32)
        run_scr[...] += incl[incl.shape[0] - 1:, :]


def _moe_plan(route, n_tiles, tm):
    t = route.shape[0]
    tri = jnp.asarray(np.tril(np.ones((tm, tm), np.float32)), dtype=BF16)
    utri = jnp.asarray(np.triu(np.ones((LANES, LANES), np.float32)), dtype=BF16)
    pos, counts = pl.pallas_call(
        _moe_plan_kernel,
        grid=(2, t // tm),
        in_specs=[
            pl.BlockSpec((tm, LANES), lambda p, i: (i, 0)),
            pl.BlockSpec((tm, tm), lambda p, i: (0, 0)),
            pl.BlockSpec((LANES, LANES), lambda p, i: (0, 0)),
        ],
        out_specs=[
            pl.BlockSpec((tm, LANES), lambda p, i: (i * p, 0)),
            pl.BlockSpec((1, LANES), lambda p, i: (0, 0)),
        ],
        out_shape=[jax.ShapeDtypeStruct((t, LANES), jnp.int32),
                   jax.ShapeDtypeStruct((1, LANES), F32)],
        scratch_shapes=[pltpu.VMEM((1, LANES), F32), pltpu.VMEM((1, LANES), F32)],
        compiler_params=_cparams(("arbitrary", "arbitrary")),
        name="moe_plan",
    )(route, tri, utri)
    counts = counts[0, :N_EXPERTS].astype(jnp.int32)
    tile_end = jnp.cumsum((counts + MOE_TILE - 1) // MOE_TILE)
    n_used = tile_end[-1]
    tile_ids = jnp.minimum(jnp.arange(n_tiles, dtype=jnp.int32), n_used - 1)
    tile_expert = jnp.sum((tile_end[None, :] <= tile_ids[:, None]).astype(jnp.int32), axis=1)
    return pos[:, :2].T, tile_expert, n_used.reshape(1)


def _moe_ffn_kernel(te_ref, nused_ref, xs_ref, wg_ref, wu_ref, wd_ref, o_ref, acc):
    i = pl.program_id(0)
    f = pl.program_id(1)
    last_f = pl.num_programs(1) - 1
    used = i < nused_ref[0]

    @pl.when(used)
    def _():
        @pl.when(f == 0)
        def _():
            acc[...] = jnp.zeros(acc.shape, F32)

        h = xs_ref[...].astype(BF16)
        act = _silu(_dot(h, wg_ref[0].astype(BF16))) * _dot(h, wu_ref[0].astype(BF16))
        acc[...] += _dot(act.astype(BF16), wd_ref[0].astype(BF16))

        @pl.when(f == last_f)
        def _():
            o_ref[...] = acc[...]

    @pl.when(jnp.logical_not(used) & (f == last_f))
    def _():
        o_ref[...] = jnp.zeros(o_ref.shape, F32)


def _moe_ffn(xs, tile_expert, n_used, wg, wu, wd, tf):
    rows, d = xs.shape
    ff = wg.shape[2]
    n_tiles = rows // MOE_TILE
    grid_spec = pltpu.PrefetchScalarGridSpec(
        num_scalar_prefetch=2,
        grid=(n_tiles, ff // tf),
        in_specs=[
            pl.BlockSpec((MOE_TILE, d), lambda i, f, te, nu: (i, 0)),
            pl.BlockSpec((1, d, tf), lambda i, f, te, nu: (te[i], 0, f)),
            pl.BlockSpec((1, d, tf), lambda i, f, te, nu: (te[i], 0, f)),
            pl.BlockSpec((1, tf, d), lambda i, f, te, nu: (te[i], f, 0)),
        ],
        out_specs=pl.BlockSpec((MOE_TILE, d), lambda i, f, te, nu: (i, 0)),
        scratch_shapes=[pltpu.VMEM((MOE_TILE, d), F32)],
    )
    return pl.pallas_call(
        _moe_ffn_kernel,
        grid_spec=grid_spec,
        out_shape=jax.ShapeDtypeStruct((rows, d), F32),
        compiler_params=_cparams(("arbitrary", "arbitrary")),
        name="moe_ffn",
    )(tile_expert, n_used, xs, wg, wu, wd)


def _combine_kernel(*refs, final_norm):
    x1_ref, y0_ref, y1_ref, route_ref, mod_ref = refs[:5]
    fin_ref = refs[5] if final_norm else None
    o_ref = refs[-1]
    route = route_ref[...]
    w1 = route[:, ROUTE_W1:ROUTE_W1 + 1]
    w2 = route[:, ROUTE_W2:ROUTE_W2 + 1]
    x2 = x1_ref[...] + mod_ref[0][5:6] * (w1 * y0_ref[...] + w2 * y1_ref[...])
    if final_norm:
        x2 = _rms(x2, fin_ref[...])
    o_ref[...] = x2


def _combine(x1, yg, route, mods, row_fn, fin_g, tm):
    t, d = x1.shape
    nb = t // tm
    final_norm = fin_g is not None
    in_specs = [
        pl.BlockSpec((tm, d), lambda i: (i, 0)),
        pl.BlockSpec((tm, d), lambda i: (i, 0)),
        pl.BlockSpec((tm, d), lambda i: (i + nb, 0)),
        pl.BlockSpec((tm, LANES), lambda i: (i, 0)),
        pl.BlockSpec((1, 6, d), lambda i: (row_fn(i), 0, 0)),
    ]
    args = [x1, yg, yg, route, mods]
    if final_norm:
        in_specs.append(pl.BlockSpec((1, d), lambda i: (0, 0)))
        args.append(fin_g)
    return pl.pallas_call(
        functools.partial(_combine_kernel, final_norm=final_norm),
        grid=(nb,),
        in_specs=in_specs,
        out_specs=pl.BlockSpec((tm, d), lambda i: (i, 0)),
        out_shape=jax.ShapeDtypeStruct((t, d), F32),
        compiler_params=_cparams(("arbitrary",)),
        name="moe_combine",
    )(*args)


def _moe(h2, x1, route, mods, row_fn, wg, wu, wd, fin_g, tm):
    t = h2.shape[0]
    n_tiles = -(-2 * t // MOE_TILE) + N_EXPERTS
    pos, tile_expert, n_used = _moe_plan(route, n_tiles, tm)
    xs = _sc_row_scatter2(h2, pos, n_tiles * MOE_TILE)
    ff = wg.shape[2]
    ys = _moe_ffn(xs, tile_expert, n_used, wg, wu, wd, MOE_FF_TILE if ff % MOE_FF_TILE == 0 else ff)
    yg = _sc_row_gather(ys, pos.reshape(-1))
    return _combine(x1, yg, route, mods, row_fn, fin_g, tm)


def _rope_partner():
    j = np.arange(MLA_ROPE)
    return np.where((j % 32) < 16, j + 16, j - 16)


def _prep_in_weight(w):
    d = w.shape[0]
    cols = [w[:, 0:1024], w[:, 1056:1568], w[:, 1568:1824], w[:, 1824:1952], w[:, 1952:2016],
            w[:, 1024:1056], jnp.zeros((d, P_WIDTH - 2016), w.dtype)]
    return jnp.concatenate(cols, axis=1).astype(BF16)


def _prep_gate_weight(w_g2, b_g2):
    outs = []
    for z, off in ((0, MISC_GF), (1, MISC_GB)):
        wz = jnp.zeros((LANES, GLA_QK_W), F32).at[off:off + GLA_GATE_RANK].set(w_g2[z])
        outs.append((wz.astype(BF16), b_g2[z].reshape(1, GLA_QK_W)))
    return outs


def _prep_mla_weights(w_uq, w_ukv):
    partner = _rope_partner()
    wq = w_uq.reshape(MLA_Q_RANK, MLA_HEADS, MLA_QK)
    wqn = wq[:, :, :MLA_NOPE].reshape(MLA_Q_RANK, MLA_HEADS * MLA_NOPE)
    rope = wq[:, :, MLA_NOPE:]
    pad = jnp.zeros((MLA_Q_RANK, MLA_HEADS, LANES - MLA_ROPE), w_uq.dtype)
    wqr = jnp.concatenate([rope, pad], axis=2).reshape(MLA_Q_RANK, MLA_HEADS * LANES)
    wqs = jnp.concatenate([rope[:, :, partner], pad], axis=2).reshape(MLA_Q_RANK, MLA_HEADS * LANES)
    wkv = w_ukv.reshape(MLA_KV_RANK, MLA_HEADS, MLA_NOPE + MLA_V)
    wknt = wkv[:, :, :MLA_NOPE].reshape(MLA_KV_RANK, MLA_HEADS * MLA_NOPE).T
    wv = wkv[:, :, MLA_NOPE:].reshape(MLA_KV_RANK, MLA_HEADS * MLA_V)
    perm = np.zeros((LANES, LANES), np.float32)
    perm[partner, np.arange(MLA_ROPE)] = 1.0
    eye = np.eye(MLA_ROPE, LANES, dtype=np.float32)
    return (wqn.astype(BF16), wqr.astype(BF16), wqs.astype(BF16), wknt.astype(BF16),
            wv.astype(BF16), jnp.asarray(perm, BF16), jnp.asarray(eye, BF16))


def _rope_tables(n_tok):
    rows = n_tok // GRID_W
    row = jnp.repeat(jnp.arange(rows, dtype=F32), GRID_W)
    col = jnp.tile(jnp.arange(GRID_W, dtype=F32), rows)
    nfreq = MLA_ROPE // 4
    inv = ROPE_BASE ** (-jnp.arange(nfreq, dtype=F32) / nfreq)
    ar = row[:, None] * inv
    ac = col[:, None] * inv
    zero = jnp.zeros((n_tok, LANES - MLA_ROPE), F32)
    cos = jnp.concatenate([jnp.cos(ar), jnp.cos(ar), jnp.cos(ac), jnp.cos(ac), zero], axis=1)
    sin = jnp.concatenate([-jnp.sin(ar), jnp.sin(ar), -jnp.sin(ac), jnp.sin(ac), zero], axis=1)
    return cos, sin


def _identity_tables(n_tok):
    cos = jnp.concatenate([jnp.ones((n_tok, MLA_ROPE), F32),
                           jnp.zeros((n_tok, LANES - MLA_ROPE), F32)], axis=1)
    return cos, jnp.zeros((n_tok, LANES), F32)


def _pick_tile(n, pref):
    t = min(n, pref)
    while n % t:
        t //= 2
    return t


def _pick_ff_tile(ff):
    best = LANES
    for m in range(1, ff // LANES + 1):
        if ff % (m * LANES) == 0 and m * LANES <= 1408:
            best = m * LANES
    return best


@jax.jit
def _forward(x, c, ctx, c_ctx, w_mod, b_mod, ln1_g, ln2_g, w_in, w_gla_g2, b_gla_g2, gla_norm_g,
             mla_q_norm_g, w_uq, mla_kv_norm_g, w_ukv, w_out, ffn_w_gate, ffn_w_up, ffn_w_down,
             router_w, exp_w_gate, exp_w_up, exp_w_down, final_norm_g):
    batch, seq, d = x.shape
    n_ctx = ctx.shape[1]
    depth = w_mod.shape[0]

    cvec = jnp.zeros((8, d), F32).at[:batch].set(c).at[batch].set(c_ctx)
    mods_all = _modulation(cvec, w_mod, b_mod).reshape(depth, 8, 6, d)

    xl = x.reshape(batch * seq, d)
    xc = ctx.reshape(batch * n_ctx, d)

    tm_l = _pick_tile(seq, 512)
    tm_c = _pick_tile(n_ctx, 256)
    tk_l = _pick_tile(seq, 1024)
    cb_l = _pick_tile(seq, 256)
    cb_c = _pick_tile(n_ctx, 256)
    row_l = lambda tm: (lambda i: i // (seq // tm))
    row_c = lambda i: batch

    rope_l = _rope_tables(seq)
    rope_c = _identity_tables(tm_c)
    zero_state = jnp.zeros((batch, GLA_QK_W, GLA_DV), F32)

    for i in range(depth):
        need_ctx = i < depth - 1
        last = i == depth - 1
        mods = mods_all[i]
        ln1 = ln1_g[i].reshape(1, d)
        ln2 = ln2_g[i].reshape(1, d)
        w_in_r = _prep_in_weight(w_in[i])
        gates = _prep_gate_weight(w_gla_g2[i], b_gla_g2[i])
        mla_w = _prep_mla_weights(w_uq[i], w_ukv[i])
        qg = mla_q_norm_g[i].reshape(1, MLA_Q_RANK)
        kvg = mla_kv_norm_g[i].reshape(1, MLA_KV_RANK)
        gg = gla_norm_g[i].reshape(1, GLA_DV)
        wo = w_out[i].astype(BF16)

        p_l = _inproj(xl, mods, row_l(tm_l), ln1, w_in_r, tm_l)
        p_c = _inproj(xc, mods, row_c, ln1, w_in_r, tm_c)

        oc_f, sc_f = _gla(p_c, *gates[0], zero_state, batch=batch, reverse=False, cb=cb_c)
        oc_b, sc_b = _gla(p_c, *gates[1], zero_state, batch=batch, reverse=True, cb=cb_c)
        ol_f, _ = _gla(p_l, *gates[0], sc_f, batch=batch, reverse=False, cb=cb_l)
        ol_b, _ = _gla(p_l, *gates[1], sc_b, batch=batch, reverse=True, cb=cb_l)

        q_l, kt_l, v_l = _mlaprep(p_l, *rope_l, qg, kvg, mla_w, batch=batch, tm=tk_l)
        q_c, kt_c, v_c = _mlaprep(p_c, *rope_c, qg, kvg, mla_w, batch=batch, tm=tm_c)
        m_l = _attention_pipelined(q_l, kt_l, v_l, kt_c, v_c, tq=tk_l, n_sub=max(1, tk_l // 512))
        m_l = m_l.reshape(batch * seq, MLA_V_W)

        if i % 2 == 0:
            j = i // 2
            router = None
            wg = ffn_w_gate[j][None].astype(BF16)
            wu = ffn_w_up[j][None].astype(BF16)
            wd = ffn_w_down[j][None].astype(BF16)
        else:
            j = i // 2
            rw = jnp.zeros((d, LANES), F32).at[:, :N_EXPERTS].set(router_w[j])
            rw_hi = rw.astype(BF16)
            router = (rw_hi, (rw - rw_hi.astype(F32)).astype(BF16))
            wg, wu, wd = exp_w_gate[j], exp_w_up[j], exp_w_down[j]
        tf = _pick_ff_tile(wg.shape[2])
        fin = final_norm_g.reshape(1, d) if last else None

        outs = _outproj(xl, ol_f, ol_b, p_l, m_l, mods, row_l(tm_l), gg, wo, ln2, router, tm_l)
        tm_f = _pick_tile(seq, 512)
        if router is None:
            xl = _ffn(outs[1], outs[0], mods, row_l(tm_f), None, wg, wu, wd, fin, tm_f, tf)
        else:
            xl = _moe(outs[1], outs[0], outs[2], mods, row_l(tm_f), wg, wu, wd, fin, tm_f)

        if need_ctx:
            m_c = _attention(q_c, [(kt_c, v_c)], tq=tm_c, n_sub=1).reshape(batch * n_ctx, MLA_V_W)
            outs_c = _outproj(xc, oc_f, oc_b, p_c, m_c, mods, row_c, gg, wo, ln2, router, tm_c)
            h2_c, comb_c = outs_c[1], None
            if router is not None:
                r_c = outs_c[2]
                lane = jnp.arange(LANES, dtype=F32)[None, :]
                comb_c = (jnp.where(lane == r_c[:, ROUTE_E1:ROUTE_E1 + 1], r_c[:, ROUTE_W1:ROUTE_W1 + 1], 0.0)
                          + jnp.where(lane == r_c[:, ROUTE_E2:ROUTE_E2 + 1], r_c[:, ROUTE_W2:ROUTE_W2 + 1], 0.0))
                h2_c = h2_c.astype(BF16)
            xc = _ffn(h2_c, outs_c[0], mods, row_c, comb_c, wg.astype(BF16), wu.astype(BF16),
                      wd.astype(BF16), None, _pick_tile(batch * n_ctx, 512), tf)

    return xl.reshape(batch, seq, d)


def kernel(x, c, ctx, c_ctx, w_mod, b_mod, ln1_g, ln2_g, w_in, w_gla_g2, b_gla_g2, gla_norm_g,
           mla_q_norm_g, w_uq, mla_kv_norm_g, w_ukv, w_out, ffn_w_gate, ffn_w_up, ffn_w_down,
           router_w, exp_w_gate, exp_w_up, exp_w_down, final_norm_g):
    return _forward(x, c, ctx, c_ctx, w_mod, b_mod, ln1_g, ln2_g, w_in, w_gla_g2, b_gla_g2,
                    gla_norm_g, mla_q_norm_g, w_uq, mla_kv_norm_g, w_ukv, w_out, ffn_w_gate,
                    ffn_w_up, ffn_w_down, router_w, exp_w_gate, exp_w_up, exp_w_down, final_norm_g)
```

```python
import functools

import numpy as np
import jax
import jax.numpy as jnp
from jax import lax
from jax.experimental import pallas as pl
from jax.experimental.pallas import tpu as pltpu
from jax.experimental.pallas import tpu_sc as plsc

F32 = jnp.float32
BF16 = jnp.bfloat16

D_MODEL = 1024
EPS = 1e-6
GRID_W = 64

GLA_HEADS = 4
GLA_DK = 64
GLA_DV = 128
GLA_GATE_RANK = 16
GLA_GATE_NORM = 16.0
GLA_CHUNK = 64
GLA_QK_W = GLA_HEADS * GLA_DK
GLA_V_W = GLA_HEADS * GLA_DV
GLA_EXP_CLAMP = 80.0

MLA_HEADS = 4
MLA_NOPE = 128
MLA_ROPE = 64
MLA_V = 128
MLA_QK = MLA_NOPE + MLA_ROPE
MLA_Q_RANK = 256
MLA_KV_RANK = 128
MLA_SCALE = MLA_QK ** -0.5
MLA_Q_SCALE = MLA_SCALE * 1.4426950408889634
MLA_V_W = MLA_HEADS * MLA_V
MLA_V_EXT = 2 * MLA_V
ROPE_BASE = 10000.0

N_EXPERTS = 8
LANES = 128
ROUTE_E1, ROUTE_E2, ROUTE_W1, ROUTE_W2 = 0, 1, 2, 3

SC_CORES = 2
SC_SUBCORES = 16
SC_GATHER_ROWS = 64
MOE_TILE = 1024
MOE_FF_TILE = 512

P_Q, P_K, P_V, P_R, P_CQ, P_CKV, P_MISC = 0, 256, 512, 1024, 1536, 1792, 1920
P_WIDTH = 2048
MISC_KR, MISC_GF, MISC_GB = 0, 64, 80

VMEM_LIMIT = 56 * 1024 * 1024


def _cparams(sem):
    return pltpu.CompilerParams(dimension_semantics=sem, vmem_limit_bytes=VMEM_LIMIT)


def _rms(x, g):
    return x * lax.rsqrt(jnp.mean(x * x, axis=-1, keepdims=True) + EPS) * g


def _silu(x):
    return x / (1.0 + jnp.exp(-x))


def _dot(a, b):
    return jnp.dot(a, b, preferred_element_type=F32)


def _dot_nt(a, b):
    return lax.dot_general(a, b, (((1,), (1,)), ((), ())), preferred_element_type=F32)


def _dot_tn(a, b):
    return lax.dot_general(a, b, (((0,), (0,)), ((), ())), preferred_element_type=F32)


def _mod_kernel(c_ref, w_ref, b_ref, o_ref):
    s = _silu(c_ref[...]).astype(BF16)
    o_ref[0] = _dot(s, w_ref[0].astype(BF16)) + b_ref[0]


def _modulation(cvec, w_mod, b_mod):
    depth, d, n = w_mod.shape
    tn = 1536
    return pl.pallas_call(
        _mod_kernel,
        grid=(depth, n // tn),
        in_specs=[
            pl.BlockSpec((8, d), lambda l, j: (0, 0)),
            pl.BlockSpec((1, d, tn), lambda l, j: (l, 0, j)),
            pl.BlockSpec((1, 1, tn), lambda l, j: (l, 0, j)),
        ],
        out_specs=pl.BlockSpec((1, 8, tn), lambda l, j: (l, 0, j)),
        out_shape=jax.ShapeDtypeStruct((depth, 8, n), F32),
        compiler_params=_cparams(("arbitrary", "arbitrary")),
        name="modulation",
    )(cvec, w_mod, b_mod.reshape(depth, 1, n))


def _inproj_kernel(x_ref, mod_ref, g_ref, w_ref, o_ref):
    m = mod_ref[0]
    h = _rms(x_ref[...], g_ref[...]) * (1.0 + m[1:2]) + m[0:1]
    o_ref[...] = _dot(h.astype(BF16), w_ref[...]).astype(BF16)


def _inproj(x, mods, row_fn, ln_g, w, tm):
    t, d = x.shape
    return pl.pallas_call(
        _inproj_kernel,
        grid=(t // tm,),
        in_specs=[
            pl.BlockSpec((tm, d), lambda i: (i, 0)),
            pl.BlockSpec((1, 6, d), lambda i: (row_fn(i), 0, 0)),
            pl.BlockSpec((1, d), lambda i: (0, 0)),
            pl.BlockSpec((d, P_WIDTH), lambda i: (0, 0)),
        ],
        out_specs=pl.BlockSpec((tm, P_WIDTH), lambda i: (i, 0)),
        out_shape=jax.ShapeDtypeStruct((t, P_WIDTH), BF16),
        compiler_params=_cparams(("arbitrary",)),
        name="inproj",
    )(x, mods, ln_g, w)


def _gla_direction(q_ref, k_ref, v_ref, misc_ref, wg, bg, tri, s_scr, o_ref, *, reverse, n_chunks):
    c_len = GLA_CHUNK
    pre = _dot(misc_ref[...], wg) + bg
    g = (jnp.minimum(pre, 0.0) - jnp.log(1.0 + jnp.exp(-jnp.abs(pre)))) * (1.0 / GLA_GATE_NORM)
    g_hi = g.astype(BF16)
    g_lo = (g - g_hi.astype(F32)).astype(BF16)
    cum = _dot(tri, g_hi) + _dot(tri, g_lo)
    tot_rows = jnp.concatenate(
        [cum[c * c_len:c * c_len + 1] if reverse else cum[(c + 1) * c_len - 1:(c + 1) * c_len]
         for c in range(n_chunks)] + [jnp.zeros((8 - n_chunks, GLA_QK_W), F32)], axis=0)
    t_hi = tot_rows.astype(BF16)
    t_lo = (tot_rows - t_hi.astype(F32)).astype(BF16)
    eye = (lax.broadcasted_iota(jnp.int32, (GLA_QK_W, GLA_QK_W), 0)
           == lax.broadcasted_iota(jnp.int32, (GLA_QK_W, GLA_QK_W), 1))
    eye = jnp.where(eye, 1.0, 0.0).astype(BF16)
    tot_cols = _dot_nt(eye, t_hi) + _dot_nt(eye, t_lo)

    lane = lax.broadcasted_iota(jnp.int32, (c_len, GLA_QK_W), 1)
    head_masks = [(lane >= h * GLA_DK) & (lane < (h + 1) * GLA_DK) for h in range(GLA_HEADS)]
    row = lax.broadcasted_iota(jnp.int32, (GLA_HEADS * c_len, c_len), 0) % c_len
    col = lax.broadcasted_iota(jnp.int32, (GLA_HEADS * c_len, c_len), 1)
    pair_mask = (col >= row) if reverse else (col <= row)

    def stack_heads(a):
        return jnp.concatenate([jnp.where(mk, a, 0.0) for mk in head_masks], axis=0).astype(BF16)

    def step(c):
        sl = slice(c * c_len, (c + 1) * c_len)
        xc = cum[sl]
        tot = tot_rows[c:c + 1]
        ref = xc[c_len // 2:c_len // 2 + 1]
        qc = q_ref[sl, :].astype(F32) * (GLA_DK ** -0.5)
        kc = k_ref[sl, :].astype(F32)
        vc = v_ref[sl, :]
        q_mid = qc * jnp.exp(jnp.minimum(xc - ref, GLA_EXP_CLAMP))
        k_mid = (kc * jnp.exp(jnp.minimum(ref - xc, GLA_EXP_CLAMP))).astype(BF16)
        q_dec = qc * jnp.exp(xc)
        k_dec = kc * jnp.exp(tot - xc)

        attn = _dot_nt(stack_heads(q_mid), k_mid)
        attn = jnp.where(pair_mask, attn, 0.0).astype(BF16)
        s_prev = s_scr[...]
        o_inter = _dot(stack_heads(q_dec), s_prev.astype(BF16))
        kv = []
        for h in range(GLA_HEADS):
            rs = slice(h * c_len, (h + 1) * c_len)
            vs = slice(h * GLA_DV, (h + 1) * GLA_DV)
            o_h = o_inter[rs] + _dot(attn[rs], vc[:, vs])
            o_ref[sl, vs] = o_h.astype(BF16)
            kv.append(_dot_tn(k_dec[:, h * GLA_DK:(h + 1) * GLA_DK].astype(BF16), vc[:, vs]))
        s_scr[...] = s_prev * jnp.exp(tot_cols[:, c:c + 1]) + jnp.concatenate(kv, axis=0)

    return step


def _gla_kernel(qf_ref, kf_ref, vf_ref, mf_ref, qb_ref, kb_ref, vb_ref, mb_ref,
                wg_ref, bg_ref, tri_ref, s0_ref, of_ref, ob_ref, sfin_ref, sf_scr, sb_scr, *, n_chunks):
    blk = pl.program_id(1)

    @pl.when(blk == 0)
    def _():
        sf_scr[...] = s0_ref[0, 0]
        sb_scr[...] = s0_ref[1, 0]

    fwd = _gla_direction(qf_ref, kf_ref, vf_ref, mf_ref, wg_ref[0], bg_ref[0], tri_ref[0],
                         sf_scr, of_ref, reverse=False, n_chunks=n_chunks)
    bwd = _gla_direction(qb_ref, kb_ref, vb_ref, mb_ref, wg_ref[1], bg_ref[1], tri_ref[1],
                         sb_scr, ob_ref, reverse=True, n_chunks=n_chunks)
    for c in range(n_chunks):
        fwd(c)
        bwd(n_chunks - 1 - c)

    @pl.when(blk == pl.num_programs(1) - 1)
    def _():
        sfin_ref[0, 0] = sf_scr[...]
        sfin_ref[1, 0] = sb_scr[...]


def _block_diag_tri(n_chunks):
    c = GLA_CHUNK
    eye = np.eye(n_chunks, dtype=np.float32)
    lower = np.kron(eye, np.tril(np.ones((c, c), np.float32)))
    upper = np.kron(eye, np.triu(np.ones((c, c), np.float32)))
    return jnp.asarray(np.stack([lower, upper]), dtype=BF16)


def _gla(p, wg, bg, s0, *, batch, cb):
    t_all = p.shape[0]
    nblk = t_all // batch // cb
    n_chunks = cb // GLA_CHUNK
    assert n_chunks <= 8

    fw = lambda b, i: b * nblk + i
    bw = lambda b, i: b * nblk + (nblk - 1 - i)
    full = lambda a: pl.BlockSpec(a.shape, lambda b, i: (0,) * a.ndim)
    tri = _block_diag_tri(n_chunks)

    def token_specs(tok):
        return [
            pl.BlockSpec((cb, GLA_QK_W), lambda b, i: (tok(b, i), P_Q // GLA_QK_W)),
            pl.BlockSpec((cb, GLA_QK_W), lambda b, i: (tok(b, i), P_K // GLA_QK_W)),
            pl.BlockSpec((cb, GLA_V_W), lambda b, i: (tok(b, i), P_V // GLA_V_W)),
            pl.BlockSpec((cb, LANES), lambda b, i: (tok(b, i), P_MISC // LANES)),
        ]

    state_spec = pl.BlockSpec((2, 1, GLA_QK_W, GLA_DV), lambda b, i: (0, b, 0, 0))
    return pl.pallas_call(
        functools.partial(_gla_kernel, n_chunks=n_chunks),
        grid=(batch, nblk),
        in_specs=token_specs(fw) + token_specs(bw) + [full(wg), full(bg), full(tri), state_spec],
        out_specs=[
            pl.BlockSpec((cb, GLA_V_W), lambda b, i: (fw(b, i), 0)),
            pl.BlockSpec((cb, GLA_V_W), lambda b, i: (bw(b, i), 0)),
            state_spec,
        ],
        out_shape=[
            jax.ShapeDtypeStruct((t_all, GLA_V_W), BF16),
            jax.ShapeDtypeStruct((t_all, GLA_V_W), BF16),
            jax.ShapeDtypeStruct((2, batch, GLA_QK_W, GLA_DV), F32),
        ],
        scratch_shapes=[pltpu.VMEM((GLA_QK_W, GLA_DV), F32), pltpu.VMEM((GLA_QK_W, GLA_DV), F32)],
        compiler_params=_cparams(("arbitrary", "arbitrary")),
        name="gla",
    )(p, p, p, p, p, p, p, p, wg, bg, tri, s0)


def _mlaprep_kernel(cq_ref, ckv_ref, misc_ref, cos_ref, sin_ref, qg_ref, kvg_ref,
                    wqn_ref, wqr_ref, wqs_ref, wknt_ref, wv_ref, perm_ref, eye_ref,
                    q_ref, kt_ref, v_ref):
    cos = cos_ref[...]
    sin = sin_ref[...]
    cqn = _rms(cq_ref[...].astype(F32), qg_ref[...]).astype(BF16)
    qn = _dot(cqn, wqn_ref[...])
    qr = _dot(cqn, wqr_ref[...])
    qs = _dot(cqn, wqs_ref[...])
    for h in range(MLA_HEADS):
        ls = slice(h * LANES, (h + 1) * LANES)
        q_ref[0, h, :, 0:MLA_NOPE] = (qn[:, ls] * MLA_Q_SCALE).astype(BF16)
        rot = qr[:, ls] * cos + qs[:, ls] * sin
        q_ref[0, h, :, MLA_NOPE:MLA_QK] = (rot[:, 0:MLA_ROPE] * MLA_Q_SCALE).astype(BF16)

    ckvn = _rms(ckv_ref[...].astype(F32), kvg_ref[...]).astype(BF16)
    knt = _dot_nt(wknt_ref[...], ckvn)
    vv = _dot(ckvn, wv_ref[...])
    misc = misc_ref[...]
    kr = misc.astype(F32) * cos + _dot(misc, perm_ref[...]) * sin
    krt = _dot_nt(eye_ref[...], kr.astype(BF16)).astype(BF16)
    for h in range(MLA_HEADS):
        kt_ref[0, h, 0, 0:MLA_NOPE, :] = knt[h * MLA_NOPE:(h + 1) * MLA_NOPE].astype(BF16)
        kt_ref[0, h, 0, MLA_NOPE:MLA_QK, :] = krt
        v_ref[0, h, :, 0:MLA_V] = vv[:, h * MLA_V:(h + 1) * MLA_V].astype(BF16)
        v_ref[0, h, :, MLA_V:MLA_V_EXT] = jnp.ones((vv.shape[0], MLA_V), BF16)


def _mlaprep(p, cos, sin, qg, kvg, wts, *, batch, tm):
    t_all = p.shape[0]
    t = t_all // batch
    nb = t // tm
    ntab = cos.shape[0] // tm
    wqn, wqr, wqs, wknt, wv, perm, eye = wts
    full = lambda a: pl.BlockSpec(a.shape, lambda b, i: (0,) * a.ndim)
    return pl.pallas_call(
        _mlaprep_kernel,
        grid=(batch, nb),
        in_specs=[
            pl.BlockSpec((tm, MLA_Q_RANK), lambda b, i: (b * nb + i, P_CQ // MLA_Q_RANK)),
            pl.BlockSpec((tm, MLA_KV_RANK), lambda b, i: (b * nb + i, P_CKV // MLA_KV_RANK)),
            pl.BlockSpec((tm, LANES), lambda b, i: (b * nb + i, P_MISC // LANES)),
            pl.BlockSpec((tm, LANES), lambda b, i: (i % ntab, 0)),
            pl.BlockSpec((tm, LANES), lambda b, i: (i % ntab, 0)),
            full(qg), full(kvg), full(wqn), full(wqr), full(wqs), full(wknt), full(wv),
            full(perm), full(eye),
        ],
        out_specs=[
            pl.BlockSpec((1, MLA_HEADS, tm, MLA_QK), lambda b, i: (b, 0, i, 0)),
            pl.BlockSpec((1, MLA_HEADS, 1, MLA_QK, tm), lambda b, i: (b, 0, i, 0, 0)),
            pl.BlockSpec((1, MLA_HEADS, tm, MLA_V_EXT), lambda b, i: (b, 0, i, 0)),
        ],
        out_shape=[
            jax.ShapeDtypeStruct((batch, MLA_HEADS, t, MLA_QK), BF16),
            jax.ShapeDtypeStruct((batch, MLA_HEADS, nb, MLA_QK, tm), BF16),
            jax.ShapeDtypeStruct((batch, MLA_HEADS, t, MLA_V_EXT), BF16),
        ],
        compiler_params=_cparams(("arbitrary", "arbitrary")),
        name="mlaprep",
    )(p, p, p, cos, sin, qg, kvg, wqn, wqr, wqs, wknt, wv, perm, eye)


def _attn_kernel(*refs, n_seg, n_sub):
    q_ref = refs[0]
    kt_refs = refs[1:1 + 2 * n_seg:2]
    v_refs = refs[2:2 + 2 * n_seg:2]
    o_ref = refs[1 + 2 * n_seg]
    m_scr, acc_scr = refs[2 + 2 * n_seg:]

    rows_per_sub = q_ref.shape[2] // n_sub
    m_scr[...] = jnp.full(m_scr.shape, -jnp.inf, F32)
    acc_scr[...] = jnp.zeros(acc_scr.shape, F32)

    for kt_ref, v_ref in zip(kt_refs, v_refs):
        n_blocks, tk = kt_ref.shape[2], kt_ref.shape[4]

        def step(j, carry, kt_ref=kt_ref, v_ref=v_ref, tk=tk):
            kt = kt_ref[0, 0, j]
            v_blk = v_ref[0, 0, pl.ds(pl.multiple_of(j * tk, tk), tk), :]
            for u in range(n_sub):
                rows = slice(u * rows_per_sub, (u + 1) * rows_per_sub)
                s = _dot(q_ref[0, 0, rows, :], kt)
                m_prev = m_scr[rows, :]
                m_next = jnp.maximum(m_prev, jnp.max(s, axis=1, keepdims=True))
                p = jnp.exp2((s - jnp.concatenate([m_next] * (tk // LANES), axis=1)).astype(BF16))
                alpha = jnp.exp2(m_prev - m_next)
                acc_scr[rows, :] = (jnp.concatenate([alpha] * (MLA_V_EXT // LANES), axis=1)
                                    * acc_scr[rows, :] + _dot(p, v_blk))
                m_scr[rows, :] = m_next
            return carry

        lax.fori_loop(0, n_blocks, step, 0)

    o_ref[0] = (acc_scr[:, 0:MLA_V] / acc_scr[:, MLA_V:MLA_V_EXT]).astype(BF16)


def _attn_pipe_kernel(q_ref, kt_ref, v_ref, ktt_ref, vt_ref, o_ref,
                      m_scr, acc_scr, s0_scr, s1_scr, st_scr, *, n_sub):
    n_blocks, tk = kt_ref.shape[2], kt_ref.shape[4]
    rows_per_sub = q_ref.shape[2] // n_sub
    subs = [slice(u * rows_per_sub, (u + 1) * rows_per_sub) for u in range(n_sub)]
    m_scr[...] = jnp.full(m_scr.shape, -jnp.inf, F32)
    acc_scr[...] = jnp.zeros(acc_scr.shape, F32)

    def scores(kt, s_ref):
        for rows in subs:
            s_ref[rows, :] = _dot(q_ref[0, 0, rows, :], kt)

    def softmax_pv(s_ref, v_blk):
        width = s_ref.shape[1]
        for rows in subs:
            s = s_ref[rows, :]
            m_prev = m_scr[rows, :]
            m_next = jnp.maximum(m_prev, jnp.max(s, axis=1, keepdims=True))
            p = jnp.exp2((s - jnp.concatenate([m_next] * (width // LANES), axis=1)).astype(BF16))
            alpha = jnp.exp2(m_prev - m_next)
            acc_scr[rows, :] = (jnp.concatenate([alpha] * (MLA_V_EXT // LANES), axis=1)
                                * acc_scr[rows, :] + _dot(p, v_blk))
            m_scr[rows, :] = m_next

    def v_main(j):
        return v_ref[0, 0, pl.ds(pl.multiple_of(j * tk, tk), tk), :]

    scores(kt_ref[0, 0, 0], s0_scr)
    n_pairs = (n_blocks - 1) // 2

    def pair(jj, carry):
        j = 2 * jj
        scores(kt_ref[0, 0, j + 1], s1_scr)
        softmax_pv(s0_scr, v_main(j))
        scores(kt_ref[0, 0, j + 2], s0_scr)
        softmax_pv(s1_scr, v_main(j + 1))
        return carry

    lax.fori_loop(0, n_pairs, pair, 0)
    if n_blocks - 2 * n_pairs == 2:
        scores(kt_ref[0, 0, n_blocks - 1], s1_scr)
        softmax_pv(s0_scr, v_main(n_blocks - 2))
        scores(ktt_ref[0, 0, 0], st_scr)
        softmax_pv(s1_scr, v_main(n_blocks - 1))
    else:
        scores(ktt_ref[0, 0, 0], st_scr)
        softmax_pv(s0_scr, v_main(n_blocks - 1))
    softmax_pv(st_scr, vt_ref[0, 0])

    o_ref[0] = (acc_scr[:, 0:MLA_V] / acc_scr[:, MLA_V:MLA_V_EXT]).astype(BF16)


def _attention_pipelined(q, kt, v, kt_tail, v_tail, *, tq, n_sub):
    b, h, t, dqk = q.shape
    tk, tt = kt.shape[4], kt_tail.shape[4]
    assert kt_tail.shape[2] == 1
    return pl.pallas_call(
        functools.partial(_attn_pipe_kernel, n_sub=n_sub),
        grid=(b, h, t // tq),
        in_specs=[
            pl.BlockSpec((1, 1, tq, dqk), lambda bi, hi, qi: (bi, hi, qi, 0)),
            pl.BlockSpec((1, 1) + kt.shape[2:], lambda bi, hi, qi: (bi, hi, 0, 0, 0)),
            pl.BlockSpec((1, 1) + v.shape[2:], lambda bi, hi, qi: (bi, hi, 0, 0)),
            pl.BlockSpec((1, 1) + kt_tail.shape[2:], lambda bi, hi, qi: (bi, hi, 0, 0, 0)),
            pl.BlockSpec((1, 1) + v_tail.shape[2:], lambda bi, hi, qi: (bi, hi, 0, 0)),
        ],
        out_specs=pl.BlockSpec((1, tq, MLA_V), lambda bi, hi, qi: (bi, qi, hi)),
        out_shape=jax.ShapeDtypeStruct((b, t, h * MLA_V), BF16),
        scratch_shapes=[pltpu.VMEM((tq, LANES), F32), pltpu.VMEM((tq, MLA_V_EXT), F32),
                        pltpu.VMEM((tq, tk), F32), pltpu.VMEM((tq, tk), F32),
                        pltpu.VMEM((tq, tt), F32)],
        compiler_params=_cparams(("arbitrary", "arbitrary", "arbitrary")),
        name="mla_attention_pipe",
    )(q, kt, v, kt_tail, v_tail)


def _attention(q, segs, *, tq, n_sub):
    b, h, t, dqk = q.shape
    in_specs = [pl.BlockSpec((1, 1, tq, dqk), lambda bi, hi, qi: (bi, hi, qi, 0))]
    args = [q]
    for kt, v in segs:
        in_specs.append(pl.BlockSpec((1, 1) + kt.shape[2:], lambda bi, hi, qi: (bi, hi, 0, 0, 0)))
        in_specs.append(pl.BlockSpec((1, 1) + v.shape[2:], lambda bi, hi, qi: (bi, hi, 0, 0)))
        args += [kt, v]
    return pl.pallas_call(
        functools.partial(_attn_kernel, n_seg=len(segs), n_sub=n_sub),
        grid=(b, h, t // tq),
        in_specs=in_specs,
        out_specs=pl.BlockSpec((1, tq, MLA_V), lambda bi, hi, qi: (bi, qi, hi)),
        out_shape=jax.ShapeDtypeStruct((b, t, h * MLA_V), BF16),
        scratch_shapes=[pltpu.VMEM((tq, LANES), F32), pltpu.VMEM((tq, MLA_V_EXT), F32)],
        compiler_params=_cparams(("arbitrary", "arbitrary", "arbitrary")),
        name="mla_attention",
    )(*args)


def _outproj_kernel(*refs, with_router):
    (x_ref, of_ref, ob_ref, r_ref, mla_ref, mod_ref, gg_ref, wo_ref, ln2_ref) = refs[:9]
    if with_router:
        rwh_ref, rwl_ref, x1_ref, h2_ref, comb_ref = refs[9:]
    else:
        x1_ref, h2_ref = refs[9:]
    m = mod_ref[0]
    o = of_ref[...].astype(F32) + ob_ref[...].astype(F32)
    gg = gg_ref[...]
    y = jnp.concatenate(
        [_rms(o[:, h * GLA_DV:(h + 1) * GLA_DV], gg) for h in range(GLA_HEADS)], axis=1)
    mix = (y * _silu(r_ref[...].astype(F32))).astype(BF16)
    yo = _dot(mix, wo_ref[0:GLA_V_W, :]) + _dot(mla_ref[...], wo_ref[GLA_V_W:, :])
    x1 = x_ref[...] + m[2:3] * yo
    x1_ref[...] = x1
    h2 = _rms(x1, ln2_ref[...]) * (1.0 + m[4:5]) + m[3:4]
    h2_ref[...] = h2.astype(h2_ref.dtype)
    if with_router:
        h_hi = h2.astype(BF16)
        h_lo = (h2 - h_hi.astype(F32)).astype(BF16)
        logits = _dot(h_hi, rwh_ref[...]) + _dot(h_lo, rwh_ref[...]) + _dot(h_hi, rwl_ref[...])
        lane = lax.broadcasted_iota(jnp.int32, logits.shape, 1).astype(F32)
        neg = jnp.float32(-jnp.inf)
        logits = jnp.where(lane < N_EXPERTS, logits, neg)
        m1 = jnp.max(logits, axis=1, keepdims=True)
        i1 = jnp.min(jnp.where(logits == m1, lane, float(LANES)), axis=1, keepdims=True)
        rest = jnp.where(lane == i1, neg, logits)
        m2 = jnp.max(rest, axis=1, keepdims=True)
        i2 = jnp.min(jnp.where(rest == m2, lane, float(LANES)), axis=1, keepdims=True)
        e2 = jnp.exp(m2 - m1)
        w1 = 1.0 / (1.0 + e2)
        comb_ref[...] = (jnp.where(lane == ROUTE_E1, i1, 0.0) + jnp.where(lane == ROUTE_E2, i2, 0.0)
                         + jnp.where(lane == ROUTE_W1, w1, 0.0)
                         + jnp.where(lane == ROUTE_W2, e2 * w1, 0.0))


def _outproj(x, o_f, o_b, p, mla, mods, row_fn, gg, wo, ln2, router, tm):
    t, d = x.shape
    with_router = router is not None
    full = lambda a: pl.BlockSpec(a.shape, lambda i: (0,) * a.ndim)
    in_specs = [
        pl.BlockSpec((tm, d), lambda i: (i, 0)),
        pl.BlockSpec((tm, GLA_V_W), lambda i: (i, 0)),
        pl.BlockSpec((tm, GLA_V_W), lambda i: (i, 0)),
        pl.BlockSpec((tm, GLA_V_W), lambda i: (i, P_R // GLA_V_W)),
        pl.BlockSpec((tm, MLA_V_W), lambda i: (i, 0)),
        pl.BlockSpec((1, 6, d), lambda i: (row_fn(i), 0, 0)),
        full(gg), full(wo), full(ln2),
    ]
    args = [x, o_f, o_b, p, mla, mods, gg, wo, ln2]
    out_specs = [pl.BlockSpec((tm, d), lambda i: (i, 0)), pl.BlockSpec((tm, d), lambda i: (i, 0))]
    out_shape = [jax.ShapeDtypeStruct((t, d), F32),
                 jax.ShapeDtypeStruct((t, d), F32 if with_router else BF16)]
    if with_router:
        in_specs += [full(router[0]), full(router[1])]
        args += list(router)
        out_specs.append(pl.BlockSpec((tm, LANES), lambda i: (i, 0)))
        out_shape.append(jax.ShapeDtypeStruct((t, LANES), F32))
    return pl.pallas_call(
        functools.partial(_outproj_kernel, with_router=with_router),
        grid=(t // tm,),
        in_specs=in_specs,
        out_specs=out_specs,
        out_shape=out_shape,
        compiler_params=_cparams(("arbitrary",)),
        name="outproj",
    )(*args)


def _ffn_kernel(*refs, with_comb, final_norm):
    h_ref, x1_ref, mod_ref = refs[:3]
    k = 3
    comb_ref = fin_ref = None
    if with_comb:
        comb_ref = refs[k]
        k += 1
    wg_ref, wu_ref, wd_ref = refs[k:k + 3]
    k += 3
    if final_norm:
        fin_ref = refs[k]
        k += 1
    o_ref, acc = refs[k:]
    e = pl.program_id(1)
    f = pl.program_id(2)

    @pl.when((e == 0) & (f == 0))
    def _():
        acc[...] = jnp.zeros(acc.shape, F32)

    h = h_ref[...]
    a = _dot(h, wg_ref[0])
    u = _dot(h, wu_ref[0])
    act = _silu(a) * u
    if with_comb:
        comb = comb_ref[...]
        lane = lax.broadcasted_iota(jnp.int32, comb.shape, 1)
        act = act * jnp.sum(jnp.where(lane == e, comb, 0.0), axis=1, keepdims=True)
    acc[...] += _dot(act.astype(BF16), wd_ref[0])

    @pl.when((e == pl.num_programs(1) - 1) & (f == pl.num_programs(2) - 1))
    def _():
        x2 = x1_ref[...] + mod_ref[0][5:6] * acc[...]
        if final_norm:
            x2 = _rms(x2, fin_ref[...])
        o_ref[...] = x2


def _ffn(h2, x1, mods, row_fn, comb, wg, wu, wd, fin_g, tm, tf):
    t, d = x1.shape
    n_e, _, ff = wg.shape
    with_comb = comb is not None
    final_norm = fin_g is not None
    in_specs = [
        pl.BlockSpec((tm, d), lambda i, e, f: (i, 0)),
        pl.BlockSpec((tm, d), lambda i, e, f: (i, 0)),
        pl.BlockSpec((1, 6, d), lambda i, e, f: (row_fn(i), 0, 0)),
    ]
    args = [h2, x1, mods]
    if with_comb:
        in_specs.append(pl.BlockSpec((tm, LANES), lambda i, e, f: (i, 0)))
        args.append(comb)
    in_specs += [
        pl.BlockSpec((1, d, tf), lambda i, e, f: (e, 0, f)),
        pl.BlockSpec((1, d, tf), lambda i, e, f: (e, 0, f)),
        pl.BlockSpec((1, tf, d), lambda i, e, f: (e, f, 0)),
    ]
    args += [wg, wu, wd]
    if final_norm:
        in_specs.append(pl.BlockSpec((1, d), lambda i, e, f: (0, 0)))
        args.append(fin_g)
    return pl.pallas_call(
        functools.partial(_ffn_kernel, with_comb=with_comb, final_norm=final_norm),
        grid=(t // tm, n_e, ff // tf),
        in_specs=in_specs,
        out_specs=pl.BlockSpec((tm, d), lambda i, e, f: (i, 0)),
        out_shape=jax.ShapeDtypeStruct((t, d), F32),
        scratch_shapes=[pltpu.VMEM((tm, d), F32)],
        compiler_params=_cparams(("arbitrary", "arbitrary", "arbitrary")),
        name="ffn",
    )(*args)


def _sc_row_gather(table, idx):
    _, w = table.shape
    b = idx.shape[0]
    n_workers = SC_CORES * SC_SUBCORES
    assert b % (n_workers * SC_GATHER_ROWS) == 0, (b, n_workers, SC_GATHER_ROWS)
    b_per_w = b // n_workers
    n_chunks = b_per_w // SC_GATHER_ROWS
    mesh = plsc.VectorSubcoreMesh(core_axis_name="c", subcore_axis_name="s",
                                  num_cores=SC_CORES, num_subcores=SC_SUBCORES)

    def body(table_hbm, idx_hbm, out_hbm, idx_v, rows_v, sem):
        wid = lax.axis_index("s") * SC_CORES + lax.axis_index("c")
        base = wid * b_per_w

        @pl.loop(0, n_chunks)
        def _(ci):
            off = base + ci * SC_GATHER_ROWS
            pltpu.sync_copy(idx_hbm.at[pl.ds(off, SC_GATHER_ROWS)], idx_v)
            pltpu.async_copy(table_hbm.at[idx_v], rows_v, sem).wait()
            pltpu.sync_copy(rows_v, out_hbm.at[pl.ds(off, SC_GATHER_ROWS)])

    return pl.kernel(
        body,
        out_type=jax.ShapeDtypeStruct((b, w), F32),
        mesh=mesh,
        scratch_types=[pltpu.VMEM((SC_GATHER_ROWS,), jnp.int32),
                       pltpu.VMEM((SC_GATHER_ROWS, w), F32),
                       pltpu.SemaphoreType.DMA],
        name="sc_row_gather",
    )(table, idx)


def _sc_row_scatter2(table, pos, n_out):
    t, w = table.shape
    n_workers = SC_CORES * SC_SUBCORES
    assert t % (n_workers * SC_GATHER_ROWS) == 0, (t, n_workers, SC_GATHER_ROWS)
    t_per_w = t // n_workers
    n_chunks = t_per_w // SC_GATHER_ROWS
    mesh = plsc.VectorSubcoreMesh(core_axis_name="c", subcore_axis_name="s",
                                  num_cores=SC_CORES, num_subcores=SC_SUBCORES)

    def body(table_hbm, pos_hbm, out_hbm, idx0_v, idx1_v, rows_v, sem):
        wid = lax.axis_index("s") * SC_CORES + lax.axis_index("c")
        base = wid * t_per_w

        @pl.loop(0, n_chunks)
        def _(ci):
            off = base + ci * SC_GATHER_ROWS
            pltpu.sync_copy(pos_hbm.at[0, pl.ds(off, SC_GATHER_ROWS)], idx0_v)
            pltpu.sync_copy(pos_hbm.at[1, pl.ds(off, SC_GATHER_ROWS)], idx1_v)
            pltpu.sync_copy(table_hbm.at[pl.ds(off, SC_GATHER_ROWS)], rows_v)
            first = pltpu.async_copy(rows_v, out_hbm.at[idx0_v], sem)
            second = pltpu.async_copy(rows_v, out_hbm.at[idx1_v], sem)
            first.wait()
            second.wait()

    return pl.kernel(
        body,
        out_type=jax.ShapeDtypeStruct((n_out, w), F32),
        mesh=mesh,
        scratch_types=[pltpu.VMEM((SC_GATHER_ROWS,), jnp.int32),
                       pltpu.VMEM((SC_GATHER_ROWS,), jnp.int32),
                       pltpu.VMEM((SC_GATHER_ROWS, w), F32),
                       pltpu.SemaphoreType.DMA],
        name="sc_row_scatter",
    )(table, pos)


def _moe_plan_kernel(route_ref, tri_ref, utri_ref, pos_ref, cnt_ref, run_scr, off_scr):
    phase = pl.program_id(0)
    blk = pl.program_id(1)
    route = route_ref[...]
    lane = lax.broadcasted_iota(jnp.int32, route.shape, 1).astype(F32)
    oh1 = jnp.where(lane == route[:, ROUTE_E1:ROUTE_E1 + 1], 1.0, 0.0)
    oh2 = jnp.where(lane == route[:, ROUTE_E2:ROUTE_E2 + 1], 1.0, 0.0)
    oh = oh1 + oh2

    @pl.when(blk == 0)
    def _():
        run_scr[...] = jnp.zeros(run_scr.shape, F32)

    @pl.when(phase == 0)
    def _():
        run_scr[...] += jnp.sum(oh, axis=0, keepdims=True)

        @pl.when(blk == pl.num_programs(1) - 1)
        def _():
            counts = run_scr[...]
            cnt_ref[...] = counts
            tiles_per = jnp.floor((counts + (MOE_TILE - 1.0)) * (1.0 / MOE_TILE))
            tile_end = _dot(jnp.broadcast_to(tiles_per, (8, LANES)).astype(BF16), utri_ref[...])[0:1]
            off_scr[...] = (tile_end - tiles_per) * float(MOE_TILE)

    @pl.when(phase == 1)
    def _():
        incl = _dot(tri_ref[...], oh.astype(BF16))
        before = incl - oh + run_scr[...] + off_scr[...]
        p1 = jnp.sum(before * oh1, axis=1, keepdims=True)
        p2 = jnp.sum(before * oh2, axis=1, keepdims=True)
        pos = jnp.where(lane == 0.0, p1, 0.0) + jnp.where(lane == 1.0, p2, 0.0)
        pos_ref[...] = pos.astype(jnp.int32)
        run_scr[...] += incl[incl.shape[0] - 1:, :]


def _moe_plan(route, n_tiles, tm):
    t = route.shape[0]
    tri = jnp.asarray(np.tril(np.ones((tm, tm), np.float32)), dtype=BF16)
    utri = jnp.asarray(np.triu(np.ones((LANES, LANES), np.float32)), dtype=BF16)
    pos, counts = pl.pallas_call(
        _moe_plan_kernel,
        grid=(2, t // tm),
        in_specs=[
            pl.BlockSpec((tm, LANES), lambda p, i: (i, 0)),
            pl.BlockSpec((tm, tm), lambda p, i: (0, 0)),
            pl.BlockSpec((LANES, LANES), lambda p, i: (0, 0)),
        ],
        out_specs=[
            pl.BlockSpec((tm, LANES), lambda p, i: (i * p, 0)),
            pl.BlockSpec((1, LANES), lambda p, i: (0, 0)),
        ],
        out_shape=[jax.ShapeDtypeStruct((t, LANES), jnp.int32),
                   jax.ShapeDtypeStruct((1, LANES), F32)],
        scratch_shapes=[pltpu.VMEM((1, LANES), F32), pltpu.VMEM((1, LANES), F32)],
        compiler_params=_cparams(("arbitrary", "arbitrary")),
        name="moe_plan",
    )(route, tri, utri)
    counts = counts[0, :N_EXPERTS].astype(jnp.int32)
    tile_end = jnp.cumsum((counts + MOE_TILE - 1) // MOE_TILE)
    n_used = tile_end[-1]
    tile_ids = jnp.minimum(jnp.arange(n_tiles, dtype=jnp.int32), n_used - 1)
    tile_expert = jnp.sum((tile_end[None, :] <= tile_ids[:, None]).astype(jnp.int32), axis=1)
    return pos[:, :2].T, tile_expert, n_used.reshape(1)


def _moe_ffn_kernel(te_ref, nused_ref, xs_ref, wg_ref, wu_ref, wd_ref, o_ref, acc):
    i = pl.program_id(0)
    f = pl.program_id(1)
    last_f = pl.num_programs(1) - 1
    used = i < nused_ref[0]

    @pl.when(used)
    def _():
        @pl.when(f == 0)
        def _():
            acc[...] = jnp.zeros(acc.shape, F32)

        h = xs_ref[...].astype(BF16)
        act = _silu(_dot(h, wg_ref[0].astype(BF16))) * _dot(h, wu_ref[0].astype(BF16))
        acc[...] += _dot(act.astype(BF16), wd_ref[0].astype(BF16))

        @pl.when(f == last_f)
        def _():
            o_ref[...] = acc[...]

    @pl.when(jnp.logical_not(used) & (f == last_f))
    def _():
        o_ref[...] = jnp.zeros(o_ref.shape, F32)


def _moe_ffn(xs, tile_expert, n_used, wg, wu, wd, tf):
    rows, d = xs.shape
    ff = wg.shape[2]
    n_tiles = rows // MOE_TILE
    grid_spec = pltpu.PrefetchScalarGridSpec(
        num_scalar_prefetch=2,
        grid=(n_tiles, ff // tf),
        in_specs=[
            pl.BlockSpec((MOE_TILE, d), lambda i, f, te, nu: (i, 0)),
            pl.BlockSpec((1, d, tf), lambda i, f, te, nu: (te[i], 0, f)),
            pl.BlockSpec((1, d, tf), lambda i, f, te, nu: (te[i], 0, f)),
            pl.BlockSpec((1, tf, d), lambda i, f, te, nu: (te[i], f, 0)),
        ],
        out_specs=pl.BlockSpec((MOE_TILE, d), lambda i, f, te, nu: (i, 0)),
        scratch_shapes=[pltpu.VMEM((MOE_TILE, d), F32)],
    )
    return pl.pallas_call(
        _moe_ffn_kernel,
        grid_spec=grid_spec,
        out_shape=jax.ShapeDtypeStruct((rows, d), F32),
        compiler_params=_cparams(("arbitrary", "arbitrary")),
        name="moe_ffn",
    )(tile_expert, n_used, xs, wg, wu, wd)


def _combine_kernel(*refs, final_norm):
    x1_ref, y0_ref, y1_ref, route_ref, mod_ref = refs[:5]
    fin_ref = refs[5] if final_norm else None
    o_ref = refs[-1]
    route = route_ref[...]
    w1 = route[:, ROUTE_W1:ROUTE_W1 + 1]
    w2 = route[:, ROUTE_W2:ROUTE_W2 + 1]
    x2 = x1_ref[...] + mod_ref[0][5:6] * (w1 * y0_ref[...] + w2 * y1_ref[...])
    if final_norm:
        x2 = _rms(x2, fin_ref[...])
    o_ref[...] = x2


def _combine(x1, yg, route, mods, row_fn, fin_g, tm):
    t, d = x1.shape
    nb = t // tm
    final_norm = fin_g is not None
    in_specs = [
        pl.BlockSpec((tm, d), lambda i: (i, 0)),
        pl.BlockSpec((tm, d), lambda i: (i, 0)),
        pl.BlockSpec((tm, d), lambda i: (i + nb, 0)),
        pl.BlockSpec((tm, LANES), lambda i: (i, 0)),
        pl.BlockSpec((1, 6, d), lambda i: (row_fn(i), 0, 0)),
    ]
    args = [x1, yg, yg, route, mods]
    if final_norm:
        in_specs.append(pl.BlockSpec((1, d), lambda i: (0, 0)))
        args.append(fin_g)
    return pl.pallas_call(
        functools.partial(_combine_kernel, final_norm=final_norm),
        grid=(nb,),
        in_specs=in_specs,
        out_specs=pl.BlockSpec((tm, d), lambda i: (i, 0)),
        out_shape=jax.ShapeDtypeStruct((t, d), F32),
        compiler_params=_cparams(("arbitrary",)),
        name="moe_combine",
    )(*args)


def _moe(h2, x1, route, mods, row_fn, wg, wu, wd, fin_g, tm):
    t = h2.shape[0]
    n_tiles = -(-2 * t // MOE_TILE) + N_EXPERTS
    pos, tile_expert, n_used = _moe_plan(route, n_tiles, tm)
    xs = _sc_row_scatter2(h2, pos, n_tiles * MOE_TILE)
    ff = wg.shape[2]
    ys = _moe_ffn(xs, tile_expert, n_used, wg, wu, wd, MOE_FF_TILE if ff % MOE_FF_TILE == 0 else ff)
    yg = _sc_row_gather(ys, pos.reshape(-1))
    return _combine(x1, yg, route, mods, row_fn, fin_g, tm)


def _rope_partner():
    j = np.arange(MLA_ROPE)
    return np.where((j % 32) < 16, j + 16, j - 16)


def _prep_in_weight(w):
    d = w.shape[0]
    cols = [w[:, 0:1024], w[:, 1056:1568], w[:, 1568:1824], w[:, 1824:1952], w[:, 1952:2016],
            w[:, 1024:1056], jnp.zeros((d, P_WIDTH - 2016), w.dtype)]
    return jnp.concatenate(cols, axis=1).astype(BF16)


def _prep_gate_weight(w_g2, b_g2):
    ws = []
    for z, off in ((0, MISC_GF), (1, MISC_GB)):
        ws.append(jnp.zeros((LANES, GLA_QK_W), F32).at[off:off + GLA_GATE_RANK].set(w_g2[z]))
    return jnp.stack(ws).astype(BF16), b_g2.reshape(2, 1, GLA_QK_W)


def _prep_mla_weights(w_uq, w_ukv):
    partner = _rope_partner()
    wq = w_uq.reshape(MLA_Q_RANK, MLA_HEADS, MLA_QK)
    wqn = wq[:, :, :MLA_NOPE].reshape(MLA_Q_RANK, MLA_HEADS * MLA_NOPE)
    rope = wq[:, :, MLA_NOPE:]
    pad = jnp.zeros((MLA_Q_RANK, MLA_HEADS, LANES - MLA_ROPE), w_uq.dtype)
    wqr = jnp.concatenate([rope, pad], axis=2).reshape(MLA_Q_RANK, MLA_HEADS * LANES)
    wqs = jnp.concatenate([rope[:, :, partner], pad], axis=2).reshape(MLA_Q_RANK, MLA_HEADS * LANES)
    wkv = w_ukv.reshape(MLA_KV_RANK, MLA_HEADS, MLA_NOPE + MLA_V)
    wknt = wkv[:, :, :MLA_NOPE].reshape(MLA_KV_RANK, MLA_HEADS * MLA_NOPE).T
    wv = wkv[:, :, MLA_NOPE:].reshape(MLA_KV_RANK, MLA_HEADS * MLA_V)
    perm = np.zeros((LANES, LANES), np.float32)
    perm[partner, np.arange(MLA_ROPE)] = 1.0
    eye = np.eye(MLA_ROPE, LANES, dtype=np.float32)
    return (wqn.astype(BF16), wqr.astype(BF16), wqs.astype(BF16), wknt.astype(BF16),
            wv.astype(BF16), jnp.asarray(perm, BF16), jnp.asarray(eye, BF16))


def _rope_tables(n_tok):
    rows = n_tok // GRID_W
    row = np.repeat(np.arange(rows, dtype=np.float32), GRID_W)
    col = np.tile(np.arange(GRID_W, dtype=np.float32), rows)
    nfreq = MLA_ROPE // 4
    inv = np.float32(ROPE_BASE) ** (-np.arange(nfreq, dtype=np.float32) / np.float32(nfreq))
    ar = (row[:, None] * inv).astype(np.float32)
    ac = (col[:, None] * inv).astype(np.float32)
    zero = np.zeros((n_tok, LANES - MLA_ROPE), np.float32)
    cos = np.concatenate([np.cos(ar), np.cos(ar), np.cos(ac), np.cos(ac), zero], axis=1)
    sin = np.concatenate([-np.sin(ar), np.sin(ar), -np.sin(ac), np.sin(ac), zero], axis=1)
    return jnp.asarray(cos, F32), jnp.asarray(sin, F32)


def _identity_tables(n_tok):
    cos = jnp.concatenate([jnp.ones((n_tok, MLA_ROPE), F32),
                           jnp.zeros((n_tok, LANES - MLA_ROPE), F32)], axis=1)
    return cos, jnp.zeros((n_tok, LANES), F32)


def _pick_tile(n, pref):
    t = min(n, pref)
    while n % t:
        t //= 2
    return t


def _pick_ff_tile(ff):
    best = LANES
    for m in range(1, ff // LANES + 1):
        if ff % (m * LANES) == 0 and m * LANES <= 1408:
            best = m * LANES
    return best


@jax.jit
def _forward(x, c, ctx, c_ctx, w_mod, b_mod, ln1_g, ln2_g, w_in, w_gla_g2, b_gla_g2, gla_norm_g,
             mla_q_norm_g, w_uq, mla_kv_norm_g, w_ukv, w_out, ffn_w_gate, ffn_w_up, ffn_w_down,
             router_w, exp_w_gate, exp_w_up, exp_w_down, final_norm_g):
    batch, seq, d = x.shape
    n_ctx = ctx.shape[1]
    depth = w_mod.shape[0]

    cvec = jnp.zeros((8, d), F32).at[:batch].set(c).at[batch].set(c_ctx)
    mods_all = _modulation(cvec, w_mod, b_mod).reshape(depth, 8, 6, d)

    xl = x.reshape(batch * seq, d)
    xc = ctx.reshape(batch * n_ctx, d)

    tm_l = _pick_tile(seq, 512)
    tm_c = _pick_tile(n_ctx, 256)
    tk_l = _pick_tile(seq, 1024)
    cb_l = _pick_tile(seq, 256)
    cb_c = _pick_tile(n_ctx, 256)
    row_l = lambda tm: (lambda i: i // (seq // tm))
    row_c = lambda i: batch

    rope_l = _rope_tables(seq)
    rope_c = _identity_tables(tm_c)
    zero_state = jnp.zeros((2, batch, GLA_QK_W, GLA_DV), F32)

    for i in range(depth):
        need_ctx = i < depth - 1
        last = i == depth - 1
        mods = mods_all[i]
        ln1 = ln1_g[i].reshape(1, d)
        ln2 = ln2_g[i].reshape(1, d)
        w_in_r = _prep_in_weight(w_in[i])
        gates = _prep_gate_weight(w_gla_g2[i], b_gla_g2[i])
        mla_w = _prep_mla_weights(w_uq[i], w_ukv[i])
        qg = mla_q_norm_g[i].reshape(1, MLA_Q_RANK)
        kvg = mla_kv_norm_g[i].reshape(1, MLA_KV_RANK)
        gg = gla_norm_g[i].reshape(1, GLA_DV)
        wo = w_out[i].astype(BF16)

        p_l = _inproj(xl, mods, row_l(tm_l), ln1, w_in_r, tm_l)
        p_c = _inproj(xc, mods, row_c, ln1, w_in_r, tm_c)

        oc_f, oc_b, s_ctx = _gla(p_c, *gates, zero_state, batch=batch, cb=cb_c)
        ol_f, ol_b, _ = _gla(p_l, *gates, s_ctx, batch=batch, cb=cb_l)

        q_l, kt_l, v_l = _mlaprep(p_l, *rope_l, qg, kvg, mla_w, batch=batch, tm=tk_l)
        q_c, kt_c, v_c = _mlaprep(p_c, *rope_c, qg, kvg, mla_w, batch=batch, tm=tm_c)
        m_l = _attention_pipelined(q_l, kt_l, v_l, kt_c, v_c, tq=tk_l, n_sub=max(1, tk_l // 512))
        m_l = m_l.reshape(batch * seq, MLA_V_W)

        if i % 2 == 0:
            j = i // 2
            router = None
            wg = ffn_w_gate[j][None].astype(BF16)
            wu = ffn_w_up[j][None].astype(BF16)
            wd = ffn_w_down[j][None].astype(BF16)
        else:
            j = i // 2
            rw = jnp.zeros((d, LANES), F32).at[:, :N_EXPERTS].set(router_w[j])
            rw_hi = rw.astype(BF16)
            router = (rw_hi, (rw - rw_hi.astype(F32)).astype(BF16))
            wg, wu, wd = exp_w_gate[j], exp_w_up[j], exp_w_down[j]
        tf = _pick_ff_tile(wg.shape[2])
        fin = final_norm_g.reshape(1, d) if last else None

        outs = _outproj(xl, ol_f, ol_b, p_l, m_l, mods, row_l(tm_l), gg, wo, ln2, router, tm_l)
        tm_f = _pick_tile(seq, 512)
        if router is None:
            xl = _ffn(outs[1], outs[0], mods, row_l(tm_f), None, wg, wu, wd, fin, tm_f, tf)
        else:
            xl = _moe(outs[1], outs[0], outs[2], mods, row_l(tm_f), wg, wu, wd, fin, tm_f)

        if need_ctx:
            m_c = _attention(q_c, [(kt_c, v_c)], tq=tm_c, n_sub=1).reshape(batch * n_ctx, MLA_V_W)
            outs_c = _outproj(xc, oc_f, oc_b, p_c, m_c, mods, row_c, gg, wo, ln2, router, tm_c)
            h2_c, comb_c = outs_c[1], None
            if router is not None:
                r_c = outs_c[2]
                lane = jnp.arange(LANES, dtype=F32)[None, :]
                comb_c = (jnp.where(lane == r_c[:, ROUTE_E1:ROUTE_E1 + 1], r_c[:, ROUTE_W1:ROUTE_W1 + 1], 0.0)
                          + jnp.where(lane == r_c[:, ROUTE_E2:ROUTE_E2 + 1], r_c[:, ROUTE_W2:ROUTE_W2 + 1], 0.0))
                h2_c = h2_c.astype(BF16)
            xc = _ffn(h2_c, outs_c[0], mods, row_c, comb_c, wg.astype(BF16), wu.astype(BF16),
                      wd.astype(BF16), None, _pick_tile(batch * n_ctx, 512), tf)

    return xl.reshape(batch, seq, d)


def kernel(x, c, ctx, c_ctx, w_mod, b_mod, ln1_g, ln2_g, w_in, w_gla_g2, b_gla_g2, gla_norm_g,
           mla_q_norm_g, w_uq, mla_kv_norm_g, w_ukv, w_out, ffn_w_gate, ffn_w_up, ffn_w_down,
           router_w, exp_w_gate, exp_w_up, exp_w_down, final_norm_g):
    return _forward(x, c, ctx, c_ctx, w_mod, b_mod, ln1_g, ln2_g, w_in, w_gla_g2, b_gla_g2,
                    gla_norm_g, mla_q_norm_g, w_uq, mla_kv_norm_g, w_ukv, w_out, ffn_w_gate,
                    ffn_w_up, ffn_w_down, router_w, exp_w_gate, exp_w_up, exp_w_down, final_norm_g)
```

```python
import functools

import numpy as np
import jax
import jax.numpy as jnp
from jax import lax
from jax.experimental import pallas as pl
from jax.experimental.pallas import tpu as pltpu
from jax.experimental.pallas import tpu_sc as plsc

F32 = jnp.float32
BF16 = jnp.bfloat16

D_MODEL = 1024
EPS = 1e-6
GRID_W = 64

GLA_HEADS = 4
GLA_DK = 64
GLA_DV = 128
GLA_GATE_RANK = 16
GLA_GATE_NORM = 16.0
GLA_CHUNK = 64
GLA_QK_W = GLA_HEADS * GLA_DK
GLA_V_W = GLA_HEADS * GLA_DV
GLA_EXP_CLAMP = 80.0

MLA_HEADS = 4
MLA_NOPE = 128
MLA_ROPE = 64
MLA_V = 128
MLA_QK = MLA_NOPE + MLA_ROPE
MLA_Q_RANK = 256
MLA_KV_RANK = 128
MLA_SCALE = MLA_QK ** -0.5
MLA_Q_SCALE = MLA_SCALE * 1.4426950408889634
MLA_V_W = MLA_HEADS * MLA_V
MLA_V_EXT = 2 * MLA_V
ROPE_BASE = 10000.0

N_EXPERTS = 8
LANES = 128
ROUTE_E1, ROUTE_E2, ROUTE_W1, ROUTE_W2 = 0, 1, 2, 3

SC_CORES = 2
SC_SUBCORES = 16
SC_GATHER_ROWS = 128
MOE_TILE = 1024
MOE_FF_TILE = 512
MOE_FF_SPLITS = ((0, 256), (256, 512))

P_Q, P_K, P_V, P_R, P_CQ, P_CKV, P_MISC = 0, 256, 512, 1024, 1536, 1792, 1920
P_WIDTH = 2048
MISC_KR, MISC_GF, MISC_GB = 0, 64, 80

VMEM_LIMIT = 56 * 1024 * 1024


def _cparams(sem):
    return pltpu.CompilerParams(dimension_semantics=sem, vmem_limit_bytes=VMEM_LIMIT)


def _rms(x, g):
    return x * lax.rsqrt(jnp.mean(x * x, axis=-1, keepdims=True) + EPS) * g


def _silu(x):
    return x / (1.0 + jnp.exp(-x))


def _dot(a, b):
    return jnp.dot(a, b, preferred_element_type=F32)


def _dot_nt(a, b):
    return lax.dot_general(a, b, (((1,), (1,)), ((), ())), preferred_element_type=F32)


def _dot_tn(a, b):
    return lax.dot_general(a, b, (((0,), (0,)), ((), ())), preferred_element_type=F32)


def _pack_bf16_pairs(x):
    u = lax.bitcast_convert_type(x, jnp.uint32)
    r = (u + jnp.uint32(0x7FFF) + ((u >> 16) & jnp.uint32(1))) >> 16
    w = x.shape[1] // 2
    return lax.bitcast_convert_type(r[:, :w] | (r[:, w:] << 16), F32)


def _unpack_bf16_pairs(p):
    u = lax.bitcast_convert_type(p, jnp.uint32)
    lo = lax.bitcast_convert_type(u << 16, F32)
    hi = lax.bitcast_convert_type(u & jnp.uint32(0xFFFF0000), F32)
    return jnp.concatenate([lo, hi], axis=1)


def _mod_kernel(c_ref, w_ref, b_ref, o_ref):
    s = _silu(c_ref[...]).astype(BF16)
    o_ref[0] = _dot(s, w_ref[0].astype(BF16)) + b_ref[0]


def _modulation(cvec, w_mod, b_mod):
    depth, d, n = w_mod.shape
    tn = 1536
    return pl.pallas_call(
        _mod_kernel,
        grid=(depth, n // tn),
        in_specs=[
            pl.BlockSpec((8, d), lambda l, j: (0, 0)),
            pl.BlockSpec((1, d, tn), lambda l, j: (l, 0, j)),
            pl.BlockSpec((1, 1, tn), lambda l, j: (l, 0, j)),
        ],
        out_specs=pl.BlockSpec((1, 8, tn), lambda l, j: (l, 0, j)),
        out_shape=jax.ShapeDtypeStruct((depth, 8, n), F32),
        compiler_params=_cparams(("arbitrary", "arbitrary")),
        name="modulation",
    )(cvec, w_mod, b_mod.reshape(depth, 1, n))


def _inproj_kernel(x_ref, mod_ref, g_ref, w_ref, o_ref):
    m = mod_ref[0]
    h = _rms(x_ref[...], g_ref[...]) * (1.0 + m[1:2]) + m[0:1]
    o_ref[...] = _dot(h.astype(BF16), w_ref[...]).astype(BF16)


def _inproj(x, mods, row_fn, ln_g, w, tm):
    t, d = x.shape
    return pl.pallas_call(
        _inproj_kernel,
        grid=(t // tm,),
        in_specs=[
            pl.BlockSpec((tm, d), lambda i: (i, 0)),
            pl.BlockSpec((1, 6, d), lambda i: (row_fn(i), 0, 0)),
            pl.BlockSpec((1, d), lambda i: (0, 0)),
            pl.BlockSpec((d, P_WIDTH), lambda i: (0, 0)),
        ],
        out_specs=pl.BlockSpec((tm, P_WIDTH), lambda i: (i, 0)),
        out_shape=jax.ShapeDtypeStruct((t, P_WIDTH), BF16),
        compiler_params=_cparams(("arbitrary",)),
        name="inproj",
    )(x, mods, ln_g, w)


def _gla_direction(q_ref, k_ref, v_ref, misc_ref, wg, bg, tri, s_scr, o_ref, *, reverse, n_chunks):
    c_len = GLA_CHUNK
    pre = _dot(misc_ref[...], wg) + bg
    g = (jnp.minimum(pre, 0.0) - jnp.log(1.0 + jnp.exp(-jnp.abs(pre)))) * (1.0 / GLA_GATE_NORM)
    g_hi = g.astype(BF16)
    g_lo = (g - g_hi.astype(F32)).astype(BF16)
    cum = _dot(tri, g_hi) + _dot(tri, g_lo)
    tot_rows = jnp.concatenate(
        [cum[c * c_len:c * c_len + 1] if reverse else cum[(c + 1) * c_len - 1:(c + 1) * c_len]
         for c in range(n_chunks)] + [jnp.zeros((8 - n_chunks, GLA_QK_W), F32)], axis=0)
    t_hi = tot_rows.astype(BF16)
    t_lo = (tot_rows - t_hi.astype(F32)).astype(BF16)
    eye = (lax.broadcasted_iota(jnp.int32, (GLA_QK_W, GLA_QK_W), 0)
           == lax.broadcasted_iota(jnp.int32, (GLA_QK_W, GLA_QK_W), 1))
    eye = jnp.where(eye, 1.0, 0.0).astype(BF16)
    tot_cols = _dot_nt(eye, t_hi) + _dot_nt(eye, t_lo)

    lane = lax.broadcasted_iota(jnp.int32, (c_len, GLA_QK_W), 1)
    head_masks = [(lane >= h * GLA_DK) & (lane < (h + 1) * GLA_DK) for h in range(GLA_HEADS)]
    row = lax.broadcasted_iota(jnp.int32, (GLA_HEADS * c_len, c_len), 0) % c_len
    col = lax.broadcasted_iota(jnp.int32, (GLA_HEADS * c_len, c_len), 1)
    pair_mask = (col >= row) if reverse else (col <= row)

    def stack_heads(a):
        return jnp.concatenate([jnp.where(mk, a, 0.0) for mk in head_masks], axis=0).astype(BF16)

    def step(c):
        sl = slice(c * c_len, (c + 1) * c_len)
        xc = cum[sl]
        tot = tot_rows[c:c + 1]
        ref = xc[c_len // 2:c_len // 2 + 1]
        qc = q_ref[sl, :].astype(F32) * (GLA_DK ** -0.5)
        kc = k_ref[sl, :].astype(F32)
        vc = v_ref[sl, :]
        q_mid = qc * jnp.exp(jnp.minimum(xc - ref, GLA_EXP_CLAMP))
        k_mid = (kc * jnp.exp(jnp.minimum(ref - xc, GLA_EXP_CLAMP))).astype(BF16)
        q_dec = qc * jnp.exp(xc)
        k_dec = kc * jnp.exp(tot - xc)

        attn = _dot_nt(stack_heads(q_mid), k_mid)
        attn = jnp.where(pair_mask, attn, 0.0).astype(BF16)
        s_prev = s_scr[...]
        o_inter = _dot(stack_heads(q_dec), s_prev.astype(BF16))
        kv = []
        for h in range(GLA_HEADS):
            rs = slice(h * c_len, (h + 1) * c_len)
            vs = slice(h * GLA_DV, (h + 1) * GLA_DV)
            o_h = o_inter[rs] + _dot(attn[rs], vc[:, vs])
            o_ref[sl, vs] = o_h.astype(BF16)
            kv.append(_dot_tn(k_dec[:, h * GLA_DK:(h + 1) * GLA_DK].astype(BF16), vc[:, vs]))
        s_scr[...] = s_prev * jnp.exp(tot_cols[:, c:c + 1]) + jnp.concatenate(kv, axis=0)

    return step


def _gla_kernel(qf_ref, kf_ref, vf_ref, mf_ref, qb_ref, kb_ref, vb_ref, mb_ref,
                wg_ref, bg_ref, tri_ref, s0_ref, of_ref, ob_ref, sfin_ref, sf_scr, sb_scr, *, n_chunks):
    blk = pl.program_id(1)

    @pl.when(blk == 0)
    def _():
        sf_scr[...] = s0_ref[0, 0]
        sb_scr[...] = s0_ref[1, 0]

    fwd = _gla_direction(qf_ref, kf_ref, vf_ref, mf_ref, wg_ref[0], bg_ref[0], tri_ref[0],
                         sf_scr, of_ref, reverse=False, n_chunks=n_chunks)
    bwd = _gla_direction(qb_ref, kb_ref, vb_ref, mb_ref, wg_ref[1], bg_ref[1], tri_ref[1],
                         sb_scr, ob_ref, reverse=True, n_chunks=n_chunks)
    for c in range(n_chunks):
        fwd(c)
        bwd(n_chunks - 1 - c)

    @pl.when(blk == pl.num_programs(1) - 1)
    def _():
        sfin_ref[0, 0] = sf_scr[...]
        sfin_ref[1, 0] = sb_scr[...]


def _block_diag_tri(n_chunks):
    c = GLA_CHUNK
    eye = np.eye(n_chunks, dtype=np.float32)
    lower = np.kron(eye, np.tril(np.ones((c, c), np.float32)))
    upper = np.kron(eye, np.triu(np.ones((c, c), np.float32)))
    return jnp.asarray(np.stack([lower, upper]), dtype=BF16)


def _gla(p, wg, bg, s0, *, batch, cb):
    t_all = p.shape[0]
    nblk = t_all // batch // cb
    n_chunks = cb // GLA_CHUNK
    assert n_chunks <= 8

    fw = lambda b, i: b * nblk + i
    bw = lambda b, i: b * nblk + (nblk - 1 - i)
    full = lambda a: pl.BlockSpec(a.shape, lambda b, i: (0,) * a.ndim)
    tri = _block_diag_tri(n_chunks)

    def token_specs(tok):
        return [
            pl.BlockSpec((cb, GLA_QK_W), lambda b, i: (tok(b, i), P_Q // GLA_QK_W)),
            pl.BlockSpec((cb, GLA_QK_W), lambda b, i: (tok(b, i), P_K // GLA_QK_W)),
            pl.BlockSpec((cb, GLA_V_W), lambda b, i: (tok(b, i), P_V // GLA_V_W)),
            pl.BlockSpec((cb, LANES), lambda b, i: (tok(b, i), P_MISC // LANES)),
        ]

    state_spec = pl.BlockSpec((2, 1, GLA_QK_W, GLA_DV), lambda b, i: (0, b, 0, 0))
    return pl.pallas_call(
        functools.partial(_gla_kernel, n_chunks=n_chunks),
        grid=(batch, nblk),
        in_specs=token_specs(fw) + token_specs(bw) + [full(wg), full(bg), full(tri), state_spec],
        out_specs=[
            pl.BlockSpec((cb, GLA_V_W), lambda b, i: (fw(b, i), 0)),
            pl.BlockSpec((cb, GLA_V_W), lambda b, i: (bw(b, i), 0)),
            state_spec,
        ],
        out_shape=[
            jax.ShapeDtypeStruct((t_all, GLA_V_W), BF16),
            jax.ShapeDtypeStruct((t_all, GLA_V_W), BF16),
            jax.ShapeDtypeStruct((2, batch, GLA_QK_W, GLA_DV), F32),
        ],
        scratch_shapes=[pltpu.VMEM((GLA_QK_W, GLA_DV), F32), pltpu.VMEM((GLA_QK_W, GLA_DV), F32)],
        compiler_params=_cparams(("arbitrary", "arbitrary")),
        name="gla",
    )(p, p, p, p, p, p, p, p, wg, bg, tri, s0)


def _mlaprep_kernel(cq_ref, ckv_ref, misc_ref, cos_ref, sin_ref, qg_ref, kvg_ref,
                    wqn_ref, wqr_ref, wqs_ref, wknt_ref, wv_ref, perm_ref, eye_ref,
                    q_ref, kt_ref, v_ref):
    cos = cos_ref[...]
    sin = sin_ref[...]
    cqn = _rms(cq_ref[...].astype(F32), qg_ref[...]).astype(BF16)
    qn = _dot(cqn, wqn_ref[...])
    qr = _dot(cqn, wqr_ref[...])
    qs = _dot(cqn, wqs_ref[...])
    for h in range(MLA_HEADS):
        ls = slice(h * LANES, (h + 1) * LANES)
        q_ref[0, h, :, 0:MLA_NOPE] = (qn[:, ls] * MLA_Q_SCALE).astype(BF16)
        rot = qr[:, ls] * cos + qs[:, ls] * sin
        q_ref[0, h, :, MLA_NOPE:MLA_QK] = (rot[:, 0:MLA_ROPE] * MLA_Q_SCALE).astype(BF16)

    ckvn = _rms(ckv_ref[...].astype(F32), kvg_ref[...]).astype(BF16)
    knt = _dot_nt(wknt_ref[...], ckvn)
    vv = _dot(ckvn, wv_ref[...])
    misc = misc_ref[...]
    kr = misc.astype(F32) * cos + _dot(misc, perm_ref[...]) * sin
    krt = _dot_nt(eye_ref[...], kr.astype(BF16)).astype(BF16)
    for h in range(MLA_HEADS):
        kt_ref[0, h, 0, 0:MLA_NOPE, :] = knt[h * MLA_NOPE:(h + 1) * MLA_NOPE].astype(BF16)
        kt_ref[0, h, 0, MLA_NOPE:MLA_QK, :] = krt
        v_ref[0, h, :, 0:MLA_V] = vv[:, h * MLA_V:(h + 1) * MLA_V].astype(BF16)
        v_ref[0, h, :, MLA_V:MLA_V_EXT] = jnp.ones((vv.shape[0], MLA_V), BF16)


def _mlaprep(p, cos, sin, qg, kvg, wts, *, batch, tm):
    t_all = p.shape[0]
    t = t_all // batch
    nb = t // tm
    ntab = cos.shape[0] // tm
    wqn, wqr, wqs, wknt, wv, perm, eye = wts
    full = lambda a: pl.BlockSpec(a.shape, lambda b, i: (0,) * a.ndim)
    return pl.pallas_call(
        _mlaprep_kernel,
        grid=(batch, nb),
        in_specs=[
            pl.BlockSpec((tm, MLA_Q_RANK), lambda b, i: (b * nb + i, P_CQ // MLA_Q_RANK)),
            pl.BlockSpec((tm, MLA_KV_RANK), lambda b, i: (b * nb + i, P_CKV // MLA_KV_RANK)),
            pl.BlockSpec((tm, LANES), lambda b, i: (b * nb + i, P_MISC // LANES)),
            pl.BlockSpec((tm, LANES), lambda b, i: (i % ntab, 0)),
            pl.BlockSpec((tm, LANES), lambda b, i: (i % ntab, 0)),
            full(qg), full(kvg), full(wqn), full(wqr), full(wqs), full(wknt), full(wv),
            full(perm), full(eye),
        ],
        out_specs=[
            pl.BlockSpec((1, MLA_HEADS, tm, MLA_QK), lambda b, i: (b, 0, i, 0)),
            pl.BlockSpec((1, MLA_HEADS, 1, MLA_QK, tm), lambda b, i: (b, 0, i, 0, 0)),
            pl.BlockSpec((1, MLA_HEADS, tm, MLA_V_EXT), lambda b, i: (b, 0, i, 0)),
        ],
        out_shape=[
            jax.ShapeDtypeStruct((batch, MLA_HEADS, t, MLA_QK), BF16),
            jax.ShapeDtypeStruct((batch, MLA_HEADS, nb, MLA_QK, tm), BF16),
            jax.ShapeDtypeStruct((batch, MLA_HEADS, t, MLA_V_EXT), BF16),
        ],
        compiler_params=_cparams(("arbitrary", "arbitrary")),
        name="mlaprep",
    )(p, p, p, cos, sin, qg, kvg, wqn, wqr, wqs, wknt, wv, perm, eye)


def _attn_kernel(*refs, n_seg, n_sub):
    q_ref = refs[0]
    kt_refs = refs[1:1 + 2 * n_seg:2]
    v_refs = refs[2:2 + 2 * n_seg:2]
    o_ref = refs[1 + 2 * n_seg]
    m_scr, acc_scr = refs[2 + 2 * n_seg:]

    rows_per_sub = q_ref.shape[2] // n_sub
    m_scr[...] = jnp.full(m_scr.shape, -jnp.inf, F32)
    acc_scr[...] = jnp.zeros(acc_scr.shape, F32)

    for kt_ref, v_ref in zip(kt_refs, v_refs):
        n_blocks, tk = kt_ref.shape[2], kt_ref.shape[4]

        def step(j, carry, kt_ref=kt_ref, v_ref=v_ref, tk=tk):
            kt = kt_ref[0, 0, j]
            v_blk = v_ref[0, 0, pl.ds(pl.multiple_of(j * tk, tk), tk), :]
            for u in range(n_sub):
                rows = slice(u * rows_per_sub, (u + 1) * rows_per_sub)
                s = _dot(q_ref[0, 0, rows, :], kt)
                m_prev = m_scr[rows, :]
                m_next = jnp.maximum(m_prev, jnp.max(s, axis=1, keepdims=True))
                p = jnp.exp2((s - jnp.concatenate([m_next] * (tk // LANES), axis=1)).astype(BF16))
                alpha = jnp.exp2(m_prev - m_next)
                acc_scr[rows, :] = (jnp.concatenate([alpha] * (MLA_V_EXT // LANES), axis=1)
                                    * acc_scr[rows, :] + _dot(p, v_blk))
                m_scr[rows, :] = m_next
            return carry

        lax.fori_loop(0, n_blocks, step, 0)

    o_ref[0] = (acc_scr[:, 0:MLA_V] / acc_scr[:, MLA_V:MLA_V_EXT]).astype(BF16)


def _attn_pipe_kernel(q_ref, kt_ref, v_ref, ktt_ref, vt_ref, o_ref,
                      m_scr, acc_scr, s0_scr, s1_scr, st_scr, *, n_sub):
    n_blocks, tk = kt_ref.shape[2], kt_ref.shape[4]
    rows_per_sub = q_ref.shape[2] // n_sub
    subs = [slice(u * rows_per_sub, (u + 1) * rows_per_sub) for u in range(n_sub)]
    m_scr[...] = jnp.full(m_scr.shape, -jnp.inf, F32)
    acc_scr[...] = jnp.zeros(acc_scr.shape, F32)

    def scores(kt, s_ref):
        for rows in subs:
            s_ref[rows, :] = _dot(q_ref[0, 0, rows, :], kt)

    def softmax_pv(s_ref, v_blk):
        width = s_ref.shape[1]
        for rows in subs:
            s = s_ref[rows, :]
            m_prev = m_scr[rows, :]
            m_next = jnp.maximum(m_prev, jnp.max(s, axis=1, keepdims=True))
            p = jnp.exp2((s - jnp.concatenate([m_next] * (width // LANES), axis=1)).astype(BF16))
            alpha = jnp.exp2(m_prev - m_next)
            acc_scr[rows, :] = (jnp.concatenate([alpha] * (MLA_V_EXT // LANES), axis=1)
                                * acc_scr[rows, :] + _dot(p, v_blk))
            m_scr[rows, :] = m_next

    def v_main(j):
        return v_ref[0, 0, pl.ds(pl.multiple_of(j * tk, tk), tk), :]

    scores(kt_ref[0, 0, 0], s0_scr)
    n_pairs = (n_blocks - 1) // 2

    def pair(jj, carry):
        j = 2 * jj
        scores(kt_ref[0, 0, j + 1], s1_scr)
        softmax_pv(s0_scr, v_main(j))
        scores(kt_ref[0, 0, j + 2], s0_scr)
        softmax_pv(s1_scr, v_main(j + 1))
        return carry

    lax.fori_loop(0, n_pairs, pair, 0)
    if n_blocks - 2 * n_pairs == 2:
        scores(kt_ref[0, 0, n_blocks - 1], s1_scr)
        softmax_pv(s0_scr, v_main(n_blocks - 2))
        scores(ktt_ref[0, 0, 0], st_scr)
        softmax_pv(s1_scr, v_main(n_blocks - 1))
    else:
        scores(ktt_ref[0, 0, 0], st_scr)
        softmax_pv(s0_scr, v_main(n_blocks - 1))
    softmax_pv(st_scr, vt_ref[0, 0])

    o_ref[0] = (acc_scr[:, 0:MLA_V] / acc_scr[:, MLA_V:MLA_V_EXT]).astype(BF16)


def _attention_pipelined(q, kt, v, kt_tail, v_tail, *, tq, n_sub):
    b, h, t, dqk = q.shape
    tk, tt = kt.shape[4], kt_tail.shape[4]
    assert kt_tail.shape[2] == 1
    return pl.pallas_call(
        functools.partial(_attn_pipe_kernel, n_sub=n_sub),
        grid=(b, h, t // tq),
        in_specs=[
            pl.BlockSpec((1, 1, tq, dqk), lambda bi, hi, qi: (bi, hi, qi, 0)),
            pl.BlockSpec((1, 1) + kt.shape[2:], lambda bi, hi, qi: (bi, hi, 0, 0, 0)),
            pl.BlockSpec((1, 1) + v.shape[2:], lambda bi, hi, qi: (bi, hi, 0, 0)),
            pl.BlockSpec((1, 1) + kt_tail.shape[2:], lambda bi, hi, qi: (bi, hi, 0, 0, 0)),
            pl.BlockSpec((1, 1) + v_tail.shape[2:], lambda bi, hi, qi: (bi, hi, 0, 0)),
        ],
        out_specs=pl.BlockSpec((1, tq, MLA_V), lambda bi, hi, qi: (bi, qi, hi)),
        out_shape=jax.ShapeDtypeStruct((b, t, h * MLA_V), BF16),
        scratch_shapes=[pltpu.VMEM((tq, LANES), F32), pltpu.VMEM((tq, MLA_V_EXT), F32),
                        pltpu.VMEM((tq, tk), F32), pltpu.VMEM((tq, tk), F32),
                        pltpu.VMEM((tq, tt), F32)],
        compiler_params=_cparams(("arbitrary", "arbitrary", "arbitrary")),
        name="mla_attention_pipe",
    )(q, kt, v, kt_tail, v_tail)


def _attention(q, segs, *, tq, n_sub):
    b, h, t, dqk = q.shape
    in_specs = [pl.BlockSpec((1, 1, tq, dqk), lambda bi, hi, qi: (bi, hi, qi, 0))]
    args = [q]
    for kt, v in segs:
        in_specs.append(pl.BlockSpec((1, 1) + kt.shape[2:], lambda bi, hi, qi: (bi, hi, 0, 0, 0)))
        in_specs.append(pl.BlockSpec((1, 1) + v.shape[2:], lambda bi, hi, qi: (bi, hi, 0, 0)))
        args += [kt, v]
    return pl.pallas_call(
        functools.partial(_attn_kernel, n_seg=len(segs), n_sub=n_sub),
        grid=(b, h, t // tq),
        in_specs=in_specs,
        out_specs=pl.BlockSpec((1, tq, MLA_V), lambda bi, hi, qi: (bi, qi, hi)),
        out_shape=jax.ShapeDtypeStruct((b, t, h * MLA_V), BF16),
        scratch_shapes=[pltpu.VMEM((tq, LANES), F32), pltpu.VMEM((tq, MLA_V_EXT), F32)],
        compiler_params=_cparams(("arbitrary", "arbitrary", "arbitrary")),
        name="mla_attention",
    )(*args)


def _outproj_kernel(*refs, with_router):
    (x_ref, of_ref, ob_ref, r_ref, mla_ref, mod_ref, gg_ref, wo_ref, ln2_ref) = refs[:9]
    if with_router:
        rwh_ref, rwl_ref, x1_ref, h2_ref, comb_ref = refs[9:]
    else:
        x1_ref, h2_ref = refs[9:]
    m = mod_ref[0]
    o = of_ref[...].astype(F32) + ob_ref[...].astype(F32)
    gg = gg_ref[...]
    y = jnp.concatenate(
        [_rms(o[:, h * GLA_DV:(h + 1) * GLA_DV], gg) for h in range(GLA_HEADS)], axis=1)
    mix = (y * _silu(r_ref[...].astype(F32))).astype(BF16)
    yo = _dot(mix, wo_ref[0:GLA_V_W, :]) + _dot(mla_ref[...], wo_ref[GLA_V_W:, :])
    x1 = x_ref[...] + m[2:3] * yo
    x1_ref[...] = x1
    h2 = _rms(x1, ln2_ref[...]) * (1.0 + m[4:5]) + m[3:4]
    if with_router:
        h2_ref[...] = _pack_bf16_pairs(h2)
    else:
        h2_ref[...] = h2.astype(BF16)
    if with_router:
        h_hi = h2.astype(BF16)
        h_lo = (h2 - h_hi.astype(F32)).astype(BF16)
        logits = _dot(h_hi, rwh_ref[...]) + _dot(h_lo, rwh_ref[...]) + _dot(h_hi, rwl_ref[...])
        lane = lax.broadcasted_iota(jnp.int32, logits.shape, 1).astype(F32)
        neg = jnp.float32(-jnp.inf)
        logits = jnp.where(lane < N_EXPERTS, logits, neg)
        m1 = jnp.max(logits, axis=1, keepdims=True)
        i1 = jnp.min(jnp.where(logits == m1, lane, float(LANES)), axis=1, keepdims=True)
        rest = jnp.where(lane == i1, neg, logits)
        m2 = jnp.max(rest, axis=1, keepdims=True)
        i2 = jnp.min(jnp.where(rest == m2, lane, float(LANES)), axis=1, keepdims=True)
        e2 = jnp.exp(m2 - m1)
        w1 = 1.0 / (1.0 + e2)
        comb_ref[...] = (jnp.where(lane == ROUTE_E1, i1, 0.0) + jnp.where(lane == ROUTE_E2, i2, 0.0)
                         + jnp.where(lane == ROUTE_W1, w1, 0.0)
                         + jnp.where(lane == ROUTE_W2, e2 * w1, 0.0))


def _outproj(x, o_f, o_b, p, mla, mods, row_fn, gg, wo, ln2, router, tm):
    t, d = x.shape
    with_router = router is not None
    full = lambda a: pl.BlockSpec(a.shape, lambda i: (0,) * a.ndim)
    in_specs = [
        pl.BlockSpec((tm, d), lambda i: (i, 0)),
        pl.BlockSpec((tm, GLA_V_W), lambda i: (i, 0)),
        pl.BlockSpec((tm, GLA_V_W), lambda i: (i, 0)),
        pl.BlockSpec((tm, GLA_V_W), lambda i: (i, P_R // GLA_V_W)),
        pl.BlockSpec((tm, MLA_V_W), lambda i: (i, 0)),
        pl.BlockSpec((1, 6, d), lambda i: (row_fn(i), 0, 0)),
        full(gg), full(wo), full(ln2),
    ]
    args = [x, o_f, o_b, p, mla, mods, gg, wo, ln2]
    h2_shape = jax.ShapeDtypeStruct((t, d // 2), F32) if with_router else jax.ShapeDtypeStruct((t, d), BF16)
    out_specs = [pl.BlockSpec((tm, d), lambda i: (i, 0)),
                 pl.BlockSpec((tm, h2_shape.shape[1]), lambda i: (i, 0))]
    out_shape = [jax.ShapeDtypeStruct((t, d), F32), h2_shape]
    if with_router:
        in_specs += [full(router[0]), full(router[1])]
        args += list(router)
        out_specs.append(pl.BlockSpec((tm, LANES), lambda i: (i, 0)))
        out_shape.append(jax.ShapeDtypeStruct((t, LANES), F32))
    return pl.pallas_call(
        functools.partial(_outproj_kernel, with_router=with_router),
        grid=(t // tm,),
        in_specs=in_specs,
        out_specs=out_specs,
        out_shape=out_shape,
        compiler_params=_cparams(("arbitrary",)),
        name="outproj",
    )(*args)


def _ffn_kernel(*refs, with_comb, final_norm):
    h_ref, x1_ref, mod_ref = refs[:3]
    k = 3
    comb_ref = fin_ref = None
    if with_comb:
        comb_ref = refs[k]
        k += 1
    wg_ref, wu_ref, wd_ref = refs[k:k + 3]
    k += 3
    if final_norm:
        fin_ref = refs[k]
        k += 1
    o_ref, acc = refs[k:]
    e = pl.program_id(1)
    f = pl.program_id(2)

    @pl.when((e == 0) & (f == 0))
    def _():
        acc[...] = jnp.zeros(acc.shape, F32)

    h = h_ref[...]
    a = _dot(h, wg_ref[0])
    u = _dot(h, wu_ref[0])
    act = _silu(a) * u
    if with_comb:
        comb = comb_ref[...]
        lane = lax.broadcasted_iota(jnp.int32, comb.shape, 1)
        act = act * jnp.sum(jnp.where(lane == e, comb, 0.0), axis=1, keepdims=True)
    acc[...] += _dot(act.astype(BF16), wd_ref[0])

    @pl.when((e == pl.num_programs(1) - 1) & (f == pl.num_programs(2) - 1))
    def _():
        x2 = x1_ref[...] + mod_ref[0][5:6] * acc[...]
        if final_norm:
            x2 = _rms(x2, fin_ref[...])
        o_ref[...] = x2


def _ffn(h2, x1, mods, row_fn, comb, wg, wu, wd, fin_g, tm, tf):
    t, d = x1.shape
    n_e, _, ff = wg.shape
    with_comb = comb is not None
    final_norm = fin_g is not None
    in_specs = [
        pl.BlockSpec((tm, d), lambda i, e, f: (i, 0)),
        pl.BlockSpec((tm, d), lambda i, e, f: (i, 0)),
        pl.BlockSpec((1, 6, d), lambda i, e, f: (row_fn(i), 0, 0)),
    ]
    args = [h2, x1, mods]
    if with_comb:
        in_specs.append(pl.BlockSpec((tm, LANES), lambda i, e, f: (i, 0)))
        args.append(comb)
    in_specs += [
        pl.BlockSpec((1, d, tf), lambda i, e, f: (e, 0, f)),
        pl.BlockSpec((1, d, tf), lambda i, e, f: (e, 0, f)),
        pl.BlockSpec((1, tf, d), lambda i, e, f: (e, f, 0)),
    ]
    args += [wg, wu, wd]
    if final_norm:
        in_specs.append(pl.BlockSpec((1, d), lambda i, e, f: (0, 0)))
        args.append(fin_g)
    return pl.pallas_call(
        functools.partial(_ffn_kernel, with_comb=with_comb, final_norm=final_norm),
        grid=(t // tm, n_e, ff // tf),
        in_specs=in_specs,
        out_specs=pl.BlockSpec((tm, d), lambda i, e, f: (i, 0)),
        out_shape=jax.ShapeDtypeStruct((t, d), F32),
        scratch_shapes=[pltpu.VMEM((tm, d), F32)],
        compiler_params=_cparams(("arbitrary", "arbitrary", "arbitrary")),
        name="ffn",
    )(*args)


def _sc_row_gather(table, idx):
    _, w = table.shape
    b = idx.shape[0]
    n_workers = SC_CORES * SC_SUBCORES
    assert b % (n_workers * SC_GATHER_ROWS) == 0, (b, n_workers, SC_GATHER_ROWS)
    b_per_w = b // n_workers
    n_chunks = b_per_w // SC_GATHER_ROWS
    mesh = plsc.VectorSubcoreMesh(core_axis_name="c", subcore_axis_name="s",
                                  num_cores=SC_CORES, num_subcores=SC_SUBCORES)

    def body(table_hbm, idx_hbm, out_hbm, idx_v, rows_v, sem):
        wid = lax.axis_index("s") * SC_CORES + lax.axis_index("c")
        base = wid * b_per_w

        @pl.loop(0, n_chunks)
        def _(ci):
            off = base + ci * SC_GATHER_ROWS
            pltpu.sync_copy(idx_hbm.at[pl.ds(off, SC_GATHER_ROWS)], idx_v)
            pltpu.async_copy(table_hbm.at[idx_v], rows_v, sem).wait()
            pltpu.sync_copy(rows_v, out_hbm.at[pl.ds(off, SC_GATHER_ROWS)])

    return pl.kernel(
        body,
        out_type=jax.ShapeDtypeStruct((b, w), F32),
        mesh=mesh,
        scratch_types=[pltpu.VMEM((SC_GATHER_ROWS,), jnp.int32),
                       pltpu.VMEM((SC_GATHER_ROWS, w), F32),
                       pltpu.SemaphoreType.DMA],
        name="sc_row_gather",
    )(table, idx)


def _sc_row_scatter2(table, pos, n_out):
    t, w = table.shape
    n_workers = SC_CORES * SC_SUBCORES
    assert t % (n_workers * SC_GATHER_ROWS) == 0, (t, n_workers, SC_GATHER_ROWS)
    t_per_w = t // n_workers
    n_chunks = t_per_w // SC_GATHER_ROWS
    mesh = plsc.VectorSubcoreMesh(core_axis_name="c", subcore_axis_name="s",
                                  num_cores=SC_CORES, num_subcores=SC_SUBCORES)

    def body(table_hbm, pos_hbm, out_hbm, idx0_v, idx1_v, rows_v, sem):
        wid = lax.axis_index("s") * SC_CORES + lax.axis_index("c")
        base = wid * t_per_w

        @pl.loop(0, n_chunks)
        def _(ci):
            off = base + ci * SC_GATHER_ROWS
            pltpu.sync_copy(pos_hbm.at[0, pl.ds(off, SC_GATHER_ROWS)], idx0_v)
            pltpu.sync_copy(pos_hbm.at[1, pl.ds(off, SC_GATHER_ROWS)], idx1_v)
            pltpu.sync_copy(table_hbm.at[pl.ds(off, SC_GATHER_ROWS)], rows_v)
            first = pltpu.async_copy(rows_v, out_hbm.at[idx0_v], sem)
            second = pltpu.async_copy(rows_v, out_hbm.at[idx1_v], sem)
            first.wait()
            second.wait()

    return pl.kernel(
        body,
        out_type=jax.ShapeDtypeStruct((n_out, w), F32),
        mesh=mesh,
        scratch_types=[pltpu.VMEM((SC_GATHER_ROWS,), jnp.int32),
                       pltpu.VMEM((SC_GATHER_ROWS,), jnp.int32),
                       pltpu.VMEM((SC_GATHER_ROWS, w), F32),
                       pltpu.SemaphoreType.DMA],
        name="sc_row_scatter",
    )(table, pos)


def _moe_plan_kernel(route_ref, tri_ref, utri_ref, pos_ref, cnt_ref, run_scr, off_scr):
    phase = pl.program_id(0)
    blk = pl.program_id(1)
    route = route_ref[...]
    lane = lax.broadcasted_iota(jnp.int32, route.shape, 1).astype(F32)
    oh1 = jnp.where(lane == route[:, ROUTE_E1:ROUTE_E1 + 1], 1.0, 0.0)
    oh2 = jnp.where(lane == route[:, ROUTE_E2:ROUTE_E2 + 1], 1.0, 0.0)
    oh = oh1 + oh2

    @pl.when(blk == 0)
    def _():
        run_scr[...] = jnp.zeros(run_scr.shape, F32)

    @pl.when(phase == 0)
    def _():
        run_scr[...] += jnp.sum(oh, axis=0, keepdims=True)

        @pl.when(blk == pl.num_programs(1) - 1)
        def _():
            counts = run_scr[...]
            cnt_ref[...] = counts
            tiles_per = jnp.floor((counts + (MOE_TILE - 1.0)) * (1.0 / MOE_TILE))
            tile_end = _dot(jnp.broadcast_to(tiles_per, (8, LANES)).astype(BF16), utri_ref[...])[0:1]
            off_scr[...] = (tile_end - tiles_per) * float(MOE_TILE)

    @pl.when(phase == 1)
    def _():
        incl = _dot(tri_ref[...], oh.astype(BF16))
        before = incl - oh + run_scr[...] + off_scr[...]
        p1 = jnp.sum(before * oh1, axis=1, keepdims=True)
        p2 = jnp.sum(before * oh2, axis=1, keepdims=True)
        pos = jnp.where(lane == 0.0, p1, 0.0) + jnp.where(lane == 1.0, p2, 0.0)
        pos_ref[...] = pos.astype(jnp.int32)
        run_scr[...] += incl[incl.shape[0] - 1:, :]


def _moe_plan(route, n_tiles, tm):
    t = route.shape[0]
    tri = jnp.asarray(np.tril(np.ones((tm, tm), np.float32)), dtype=BF16)
    utri = jnp.asarray(np.triu(np.ones((LANES, LANES), np.float32)), dtype=BF16)
    pos, counts = pl.pallas_call(
        _moe_plan_kernel,
        grid=(2, t // tm),
        in_specs=[
            pl.BlockSpec((tm, LANES), lambda p, i: (i, 0)),
            pl.BlockSpec((tm, tm), lambda p, i: (0, 0)),
            pl.BlockSpec((LANES, LANES), lambda p, i: (0, 0)),
        ],
        out_specs=[
            pl.BlockSpec((tm, LANES), lambda p, i: (i * p, 0)),
            pl.BlockSpec((1, LANES), lambda p, i: (0, 0)),
        ],
        out_shape=[jax.ShapeDtypeStruct((t, LANES), jnp.int32),
                   jax.ShapeDtypeStruct((1, LANES), F32)],
        scratch_shapes=[pltpu.VMEM((1, LANES), F32), pltpu.VMEM((1, LANES), F32)],
        compiler_params=_cparams(("arbitrary", "arbitrary")),
        name="moe_plan",
    )(route, tri, utri)
    counts = counts[0, :N_EXPERTS].astype(jnp.int32)
    tile_end = jnp.cumsum((counts + MOE_TILE - 1) // MOE_TILE)
    n_used = tile_end[-1]
    tile_ids = jnp.minimum(jnp.arange(n_tiles, dtype=jnp.int32), n_used - 1)
    tile_expert = jnp.sum((tile_end[None, :] <= tile_ids[:, None]).astype(jnp.int32), axis=1)
    return pos[:, :2].T, tile_expert, n_used.reshape(1)


def _moe_ffn_kernel(te_ref, nused_ref, xs_ref, wg_ref, wu_ref, wd_ref, o_ref, acc, h_scr, *, splits):
    i = pl.program_id(0)
    f = pl.program_id(1)
    last_f = pl.num_programs(1) - 1
    used = i < nused_ref[0]

    @pl.when(used)
    def _():
        @pl.when(f == 0)
        def _():
            acc[...] = jnp.zeros(acc.shape, F32)
            h_scr[...] = _unpack_bf16_pairs(xs_ref[...]).astype(BF16)

        h = h_scr[...]
        for lo, hi in splits:
            a = _dot(h, wg_ref[0, :, lo:hi].astype(BF16))
            u = _dot(h, wu_ref[0, :, lo:hi].astype(BF16))
            acc[...] += _dot((_silu(a) * u).astype(BF16), wd_ref[0, lo:hi, :].astype(BF16))

        @pl.when(f == last_f)
        def _():
            o_ref[...] = _pack_bf16_pairs(acc[...])

    @pl.when(jnp.logical_not(used) & (f == last_f))
    def _():
        o_ref[...] = jnp.zeros(o_ref.shape, F32)


def _moe_ffn(xs, tile_expert, n_used, wg, wu, wd):
    rows, half = xs.shape
    d = 2 * half
    ff = wg.shape[2]
    if ff % MOE_FF_TILE == 0:
        tf, splits = MOE_FF_TILE, MOE_FF_SPLITS
    else:
        tf, splits = ff, ((0, ff),)
    n_tiles = rows // MOE_TILE
    grid_spec = pltpu.PrefetchScalarGridSpec(
        num_scalar_prefetch=2,
        grid=(n_tiles, ff // tf),
        in_specs=[
            pl.BlockSpec((MOE_TILE, half), lambda i, f, te, nu: (i, 0)),
            pl.BlockSpec((1, d, tf), lambda i, f, te, nu: (te[i], 0, f)),
            pl.BlockSpec((1, d, tf), lambda i, f, te, nu: (te[i], 0, f)),
            pl.BlockSpec((1, tf, d), lambda i, f, te, nu: (te[i], f, 0)),
        ],
        out_specs=pl.BlockSpec((MOE_TILE, half), lambda i, f, te, nu: (i, 0)),
        scratch_shapes=[pltpu.VMEM((MOE_TILE, d), F32), pltpu.VMEM((MOE_TILE, d), BF16)],
    )
    return pl.pallas_call(
        functools.partial(_moe_ffn_kernel, splits=splits),
        grid_spec=grid_spec,
        out_shape=jax.ShapeDtypeStruct((rows, half), F32),
        compiler_params=_cparams(("arbitrary", "arbitrary")),
        name="moe_ffn",
    )(tile_expert, n_used, xs, wg, wu, wd)


def _combine_kernel(*refs, final_norm):
    x1_ref, y0_ref, y1_ref, route_ref, mod_ref = refs[:5]
    fin_ref = refs[5] if final_norm else None
    o_ref = refs[-1]
    route = route_ref[...]
    w1 = route[:, ROUTE_W1:ROUTE_W1 + 1]
    w2 = route[:, ROUTE_W2:ROUTE_W2 + 1]
    y = w1 * _unpack_bf16_pairs(y0_ref[...]) + w2 * _unpack_bf16_pairs(y1_ref[...])
    x2 = x1_ref[...] + mod_ref[0][5:6] * y
    if final_norm:
        x2 = _rms(x2, fin_ref[...])
    o_ref[...] = x2


def _combine(x1, yg, route, mods, row_fn, fin_g, tm):
    t, d = x1.shape
    nb = t // tm
    final_norm = fin_g is not None
    in_specs = [
        pl.BlockSpec((tm, d), lambda i: (i, 0)),
        pl.BlockSpec((tm, d // 2), lambda i: (i, 0)),
        pl.BlockSpec((tm, d // 2), lambda i: (i + nb, 0)),
        pl.BlockSpec((tm, LANES), lambda i: (i, 0)),
        pl.BlockSpec((1, 6, d), lambda i: (row_fn(i), 0, 0)),
    ]
    args = [x1, yg, yg, route, mods]
    if final_norm:
        in_specs.append(pl.BlockSpec((1, d), lambda i: (0, 0)))
        args.append(fin_g)
    return pl.pallas_call(
        functools.partial(_combine_kernel, final_norm=final_norm),
        grid=(nb,),
        in_specs=in_specs,
        out_specs=pl.BlockSpec((tm, d), lambda i: (i, 0)),
        out_shape=jax.ShapeDtypeStruct((t, d), F32),
        compiler_params=_cparams(("arbitrary",)),
        name="moe_combine",
    )(*args)


def _moe(h2, x1, route, mods, row_fn, wg, wu, wd, fin_g, tm):
    t = h2.shape[0]
    n_tiles = -(-2 * t // MOE_TILE) + N_EXPERTS
    pos, tile_expert, n_used = _moe_plan(route, n_tiles, tm)
    xs = _sc_row_scatter2(h2, pos, n_tiles * MOE_TILE)
    ys = _moe_ffn(xs, tile_expert, n_used, wg, wu, wd)
    yg = _sc_row_gather(ys, pos.reshape(-1))
    return _combine(x1, yg, route, mods, row_fn, fin_g, tm)


def _rope_partner():
    j = np.arange(MLA_ROPE)
    return np.where((j % 32) < 16, j + 16, j - 16)


def _prep_in_weight(w):
    d = w.shape[0]
    cols = [w[:, 0:1024], w[:, 1056:1568], w[:, 1568:1824], w[:, 1824:1952], w[:, 1952:2016],
            w[:, 1024:1056], jnp.zeros((d, P_WIDTH - 2016), w.dtype)]
    return jnp.concatenate(cols, axis=1).astype(BF16)


def _prep_gate_weight(w_g2, b_g2):
    ws = []
    for z, off in ((0, MISC_GF), (1, MISC_GB)):
        ws.append(jnp.zeros((LANES, GLA_QK_W), F32).at[off:off + GLA_GATE_RANK].set(w_g2[z]))
    return jnp.stack(ws).astype(BF16), b_g2.reshape(2, 1, GLA_QK_W)


def _prep_mla_weights(w_uq, w_ukv):
    partner = _rope_partner()
    wq = w_uq.reshape(MLA_Q_RANK, MLA_HEADS, MLA_QK)
    wqn = wq[:, :, :MLA_NOPE].reshape(MLA_Q_RANK, MLA_HEADS * MLA_NOPE)
    rope = wq[:, :, MLA_NOPE:]
    pad = jnp.zeros((MLA_Q_RANK, MLA_HEADS, LANES - MLA_ROPE), w_uq.dtype)
    wqr = jnp.concatenate([rope, pad], axis=2).reshape(MLA_Q_RANK, MLA_HEADS * LANES)
    wqs = jnp.concatenate([rope[:, :, partner], pad], axis=2).reshape(MLA_Q_RANK, MLA_HEADS * LANES)
    wkv = w_ukv.reshape(MLA_KV_RANK, MLA_HEADS, MLA_NOPE + MLA_V)
    wknt = wkv[:, :, :MLA_NOPE].reshape(MLA_KV_RANK, MLA_HEADS * MLA_NOPE).T
    wv = wkv[:, :, MLA_NOPE:].reshape(MLA_KV_RANK, MLA_HEADS * MLA_V)
    perm = np.zeros((LANES, LANES), np.float32)
    perm[partner, np.arange(MLA_ROPE)] = 1.0
    eye = np.eye(MLA_ROPE, LANES, dtype=np.float32)
    return (wqn.astype(BF16), wqr.astype(BF16), wqs.astype(BF16), wknt.astype(BF16),
            wv.astype(BF16), jnp.asarray(perm, BF16), jnp.asarray(eye, BF16))


def _rope_tables(n_tok):
    rows = n_tok // GRID_W
    row = np.repeat(np.arange(rows, dtype=np.float32), GRID_W)
    col = np.tile(np.arange(GRID_W, dtype=np.float32), rows)
    nfreq = MLA_ROPE // 4
    inv = np.float32(ROPE_BASE) ** (-np.arange(nfreq, dtype=np.float32) / np.float32(nfreq))
    ar = (row[:, None] * inv).astype(np.float32)
    ac = (col[:, None] * inv).astype(np.float32)
    zero = np.zeros((n_tok, LANES - MLA_ROPE), np.float32)
    cos = np.concatenate([np.cos(ar), np.cos(ar), np.cos(ac), np.cos(ac), zero], axis=1)
    sin = np.concatenate([-np.sin(ar), np.sin(ar), -np.sin(ac), np.sin(ac), zero], axis=1)
    return jnp.asarray(cos, F32), jnp.asarray(sin, F32)


def _identity_tables(n_tok):
    cos = jnp.concatenate([jnp.ones((n_tok, MLA_ROPE), F32),
                           jnp.zeros((n_tok, LANES - MLA_ROPE), F32)], axis=1)
    return cos, jnp.zeros((n_tok, LANES), F32)


def _pick_tile(n, pref):
    t = min(n, pref)
    while n % t:
        t //= 2
    return t


def _pick_ff_tile(ff):
    best = LANES
    for m in range(1, ff // LANES + 1):
        if ff % (m * LANES) == 0 and m * LANES <= 1408:
            best = m * LANES
    return best


@jax.jit
def _forward(x, c, ctx, c_ctx, w_mod, b_mod, ln1_g, ln2_g, w_in, w_gla_g2, b_gla_g2, gla_norm_g,
             mla_q_norm_g, w_uq, mla_kv_norm_g, w_ukv, w_out, ffn_w_gate, ffn_w_up, ffn_w_down,
             router_w, exp_w_gate, exp_w_up, exp_w_down, final_norm_g):
    batch, seq, d = x.shape
    n_ctx = ctx.shape[1]
    depth = w_mod.shape[0]

    cvec = jnp.zeros((8, d), F32).at[:batch].set(c).at[batch].set(c_ctx)
    mods_all = _modulation(cvec, w_mod, b_mod).reshape(depth, 8, 6, d)

    xl = x.reshape(batch * seq, d)
    xc = ctx.reshape(batch * n_ctx, d)

    tm_l = _pick_tile(seq, 512)
    tm_c = _pick_tile(n_ctx, 256)
    tk_l = _pick_tile(seq, 1024)
    cb_l = _pick_tile(seq, 256)
    cb_c = _pick_tile(n_ctx, 256)
    row_l = lambda tm: (lambda i: i // (seq // tm))
    row_c = lambda i: batch

    rope_l = _rope_tables(seq)
    rope_c = _identity_tables(tm_c)
    zero_state = jnp.zeros((2, batch, GLA_QK_W, GLA_DV), F32)

    for i in range(depth):
        need_ctx = i < depth - 1
        last = i == depth - 1
        mods = mods_all[i]
        ln1 = ln1_g[i].reshape(1, d)
        ln2 = ln2_g[i].reshape(1, d)
        w_in_r = _prep_in_weight(w_in[i])
        gates = _prep_gate_weight(w_gla_g2[i], b_gla_g2[i])
        mla_w = _prep_mla_weights(w_uq[i], w_ukv[i])
        qg = mla_q_norm_g[i].reshape(1, MLA_Q_RANK)
        kvg = mla_kv_norm_g[i].reshape(1, MLA_KV_RANK)
        gg = gla_norm_g[i].reshape(1, GLA_DV)
        wo = w_out[i].astype(BF16)

        p_l = _inproj(xl, mods, row_l(tm_l), ln1, w_in_r, tm_l)
        p_c = _inproj(xc, mods, row_c, ln1, w_in_r, tm_c)

        oc_f, oc_b, s_ctx = _gla(p_c, *gates, zero_state, batch=batch, cb=cb_c)
        ol_f, ol_b, _ = _gla(p_l, *gates, s_ctx, batch=batch, cb=cb_l)

        q_l, kt_l, v_l = _mlaprep(p_l, *rope_l, qg, kvg, mla_w, batch=batch, tm=tk_l)
        q_c, kt_c, v_c = _mlaprep(p_c, *rope_c, qg, kvg, mla_w, batch=batch, tm=tm_c)
        m_l = _attention_pipelined(q_l, kt_l, v_l, kt_c, v_c, tq=tk_l, n_sub=max(1, tk_l // 512))
        m_l = m_l.reshape(batch * seq, MLA_V_W)

        if i % 2 == 0:
            j = i // 2
            router = None
            wg = ffn_w_gate[j][None].astype(BF16)
            wu = ffn_w_up[j][None].astype(BF16)
            wd = ffn_w_down[j][None].astype(BF16)
        else:
            j = i // 2
            rw = jnp.zeros((d, LANES), F32).at[:, :N_EXPERTS].set(router_w[j])
            rw_hi = rw.astype(BF16)
            router = (rw_hi, (rw - rw_hi.astype(F32)).astype(BF16))
            wg, wu, wd = exp_w_gate[j], exp_w_up[j], exp_w_down[j]
        tf = _pick_ff_tile(wg.shape[2])
        fin = final_norm_g.reshape(1, d) if last else None

        outs = _outproj(xl, ol_f, ol_b, p_l, m_l, mods, row_l(tm_l), gg, wo, ln2, router, tm_l)
        tm_f = _pick_tile(seq, 512)
        if router is None:
            xl = _ffn(outs[1], outs[0], mods, row_l(tm_f), None, wg, wu, wd, fin, tm_f, tf)
        else:
            xl = _moe(outs[1], outs[0], outs[2], mods, row_l(tm_f), wg, wu, wd, fin, tm_f)

        if need_ctx:
            m_c = _attention(q_c, [(kt_c, v_c)], tq=tm_c, n_sub=1).reshape(batch * n_ctx, MLA_V_W)
            outs_c = _outproj(xc, oc_f, oc_b, p_c, m_c, mods, row_c, gg, wo, ln2, router, tm_c)
            h2_c, comb_c = outs_c[1], None
            if router is not None:
                r_c = outs_c[2]
                lane = jnp.arange(LANES, dtype=F32)[None, :]
                comb_c = (jnp.where(lane == r_c[:, ROUTE_E1:ROUTE_E1 + 1], r_c[:, ROUTE_W1:ROUTE_W1 + 1], 0.0)
                          + jnp.where(lane == r_c[:, ROUTE_E2:ROUTE_E2 + 1], r_c[:, ROUTE_W2:ROUTE_W2 + 1], 0.0))
                h2_c = _unpack_bf16_pairs(h2_c).astype(BF16)
            xc = _ffn(h2_c, outs_c[0], mods, row_c, comb_c, wg.astype(BF16), wu.astype(BF16),
                      wd.astype(BF16), None, _pick_tile(batch * n_ctx, 512), tf)

    return xl.reshape(batch, seq, d)


def kernel(x, c, ctx, c_ctx, w_mod, b_mod, ln1_g, ln2_g, w_in, w_gla_g2, b_gla_g2, gla_norm_g,
           mla_q_norm_g, w_uq, mla_kv_norm_g, w_ukv, w_out, ffn_w_gate, ffn_w_up, ffn_w_down,
           router_w, exp_w_gate, exp_w_up, exp_w_down, final_norm_g):
    return _forward(x, c, ctx, c_ctx, w_mod, b_mod, ln1_g, ln2_g, w_in, w_gla_g2, b_gla_g2,
                    gla_norm_g, mla_q_norm_g, w_uq, mla_kv_norm_g, w_ukv, w_out, ffn_w_gate,
                    ffn_w_up, ffn_w_down, router_w, exp_w_gate, exp_w_up, exp_w_down, final_norm_g)
```

```python
import functools

import numpy as np
import jax
import jax.numpy as jnp
from jax import lax
from jax.experimental import pallas as pl
from jax.experimental.pallas import tpu as pltpu
from jax.experimental.pallas import tpu_sc as plsc

F32 = jnp.float32
BF16 = jnp.bfloat16

D_MODEL = 1024
EPS = 1e-6
GRID_W = 64

GLA_HEADS = 4
GLA_DK = 64
GLA_DV = 128
GLA_GATE_RANK = 16
GLA_GATE_NORM = 16.0
GLA_CHUNK = 64
GLA_QK_W = GLA_HEADS * GLA_DK
GLA_V_W = GLA_HEADS * GLA_DV
GLA_EXP_CLAMP = 80.0

MLA_HEADS = 4
MLA_NOPE = 128
MLA_ROPE = 64
MLA_V = 128
MLA_QK = MLA_NOPE + MLA_ROPE
MLA_Q_RANK = 256
MLA_KV_RANK = 128
MLA_SCALE = MLA_QK ** -0.5
MLA_Q_SCALE = MLA_SCALE * 1.4426950408889634
MLA_V_W = MLA_HEADS * MLA_V
MLA_V_EXT = 2 * MLA_V
ROPE_BASE = 10000.0

N_EXPERTS = 8
LANES = 128
ROUTE_E1, ROUTE_E2, ROUTE_W1, ROUTE_W2 = 0, 1, 2, 3

SC_CORES = 2
SC_SUBCORES = 16
SC_GATHER_ROWS = 128
MOE_TILE = 1024
MOE_FF_TILE = 512
MOE_FF_SPLITS = ((0, 256), (256, 512))

P_Q, P_K, P_V, P_R, P_CQ, P_CKV, P_MISC = 0, 256, 512, 1024, 1536, 1792, 1920
P_WIDTH = 2048
MISC_KR, MISC_GF, MISC_GB = 0, 64, 80

VMEM_LIMIT = 56 * 1024 * 1024


def _cparams(sem):
    return pltpu.CompilerParams(dimension_semantics=sem, vmem_limit_bytes=VMEM_LIMIT)


def _rms(x, g):
    return x * lax.rsqrt(jnp.mean(x * x, axis=-1, keepdims=True) + EPS) * g


def _silu(x):
    return x / (1.0 + jnp.exp(-x))


def _dot(a, b):
    return jnp.dot(a, b, preferred_element_type=F32)


def _dot_nt(a, b):
    return lax.dot_general(a, b, (((1,), (1,)), ((), ())), preferred_element_type=F32)


def _dot_tn(a, b):
    return lax.dot_general(a, b, (((0,), (0,)), ((), ())), preferred_element_type=F32)


def _pack_bf16_pairs(x):
    u = lax.bitcast_convert_type(x, jnp.uint32)
    r = (u + jnp.uint32(0x7FFF) + ((u >> 16) & jnp.uint32(1))) >> 16
    w = x.shape[1] // 2
    return lax.bitcast_convert_type(r[:, :w] | (r[:, w:] << 16), F32)


def _unpack_bf16_pairs(p):
    u = lax.bitcast_convert_type(p, jnp.uint32)
    lo = lax.bitcast_convert_type(u << 16, F32)
    hi = lax.bitcast_convert_type(u & jnp.uint32(0xFFFF0000), F32)
    return jnp.concatenate([lo, hi], axis=1)


def _mod_kernel(c_ref, w_ref, b_ref, o_ref):
    s = _silu(c_ref[...]).astype(BF16)
    o_ref[0] = _dot(s, w_ref[0].astype(BF16)) + b_ref[0]


def _modulation(cvec, w_mod, b_mod):
    depth, d, n = w_mod.shape
    tn = 1536
    return pl.pallas_call(
        _mod_kernel,
        grid=(depth, n // tn),
        in_specs=[
            pl.BlockSpec((8, d), lambda l, j: (0, 0)),
            pl.BlockSpec((1, d, tn), lambda l, j: (l, 0, j)),
            pl.BlockSpec((1, 1, tn), lambda l, j: (l, 0, j)),
        ],
        out_specs=pl.BlockSpec((1, 8, tn), lambda l, j: (l, 0, j)),
        out_shape=jax.ShapeDtypeStruct((depth, 8, n), F32),
        compiler_params=_cparams(("arbitrary", "arbitrary")),
        name="modulation",
    )(cvec, w_mod, b_mod.reshape(depth, 1, n))


def _inproj_kernel(x_ref, mod_ref, g_ref, w_ref, o_ref):
    m = mod_ref[0]
    h = _rms(x_ref[...], g_ref[...]) * (1.0 + m[1:2]) + m[0:1]
    o_ref[...] = _dot(h.astype(BF16), w_ref[...]).astype(BF16)


def _inproj(x, mods, row_fn, ln_g, w, tm):
    t, d = x.shape
    return pl.pallas_call(
        _inproj_kernel,
        grid=(t // tm,),
        in_specs=[
            pl.BlockSpec((tm, d), lambda i: (i, 0)),
            pl.BlockSpec((1, 6, d), lambda i: (row_fn(i), 0, 0)),
            pl.BlockSpec((1, d), lambda i: (0, 0)),
            pl.BlockSpec((d, P_WIDTH), lambda i: (0, 0)),
        ],
        out_specs=pl.BlockSpec((tm, P_WIDTH), lambda i: (i, 0)),
        out_shape=jax.ShapeDtypeStruct((t, P_WIDTH), BF16),
        compiler_params=_cparams(("arbitrary",)),
        name="inproj",
    )(x, mods, ln_g, w)


def _gla_direction(q_ref, k_ref, v_ref, misc_ref, wg, bg, tri, s_scr, o_ref, *, reverse, n_chunks):
    c_len = GLA_CHUNK
    pre = _dot(misc_ref[...], wg) + bg
    g = (jnp.minimum(pre, 0.0) - jnp.log(1.0 + jnp.exp(-jnp.abs(pre)))) * (1.0 / GLA_GATE_NORM)
    g_hi = g.astype(BF16)
    g_lo = (g - g_hi.astype(F32)).astype(BF16)
    cum = _dot(tri, g_hi) + _dot(tri, g_lo)
    tot_rows = jnp.concatenate(
        [cum[c * c_len:c * c_len + 1] if reverse else cum[(c + 1) * c_len - 1:(c + 1) * c_len]
         for c in range(n_chunks)] + [jnp.zeros((8 - n_chunks, GLA_QK_W), F32)], axis=0)
    t_hi = tot_rows.astype(BF16)
    t_lo = (tot_rows - t_hi.astype(F32)).astype(BF16)
    eye = (lax.broadcasted_iota(jnp.int32, (GLA_QK_W, GLA_QK_W), 0)
           == lax.broadcasted_iota(jnp.int32, (GLA_QK_W, GLA_QK_W), 1))
    eye = jnp.where(eye, 1.0, 0.0).astype(BF16)
    tot_cols = _dot_nt(eye, t_hi) + _dot_nt(eye, t_lo)

    lane = lax.broadcasted_iota(jnp.int32, (c_len, GLA_QK_W), 1)
    head_masks = [(lane >= h * GLA_DK) & (lane < (h + 1) * GLA_DK) for h in range(GLA_HEADS)]
    row = lax.broadcasted_iota(jnp.int32, (GLA_HEADS * c_len, c_len), 0) % c_len
    col = lax.broadcasted_iota(jnp.int32, (GLA_HEADS * c_len, c_len), 1)
    pair_mask = (col >= row) if reverse else (col <= row)

    def stack_heads(a):
        return jnp.concatenate([jnp.where(mk, a, 0.0) for mk in head_masks], axis=0).astype(BF16)

    def step(c):
        sl = slice(c * c_len, (c + 1) * c_len)
        xc = cum[sl]
        tot = tot_rows[c:c + 1]
        ref = xc[c_len // 2:c_len // 2 + 1]
        qc = q_ref[sl, :].astype(F32) * (GLA_DK ** -0.5)
        kc = k_ref[sl, :].astype(F32)
        vc = v_ref[sl, :]
        q_mid = qc * jnp.exp(jnp.minimum(xc - ref, GLA_EXP_CLAMP))
        k_mid = (kc * jnp.exp(jnp.minimum(ref - xc, GLA_EXP_CLAMP))).astype(BF16)
        q_dec = qc * jnp.exp(xc)
        k_dec = kc * jnp.exp(tot - xc)

        attn = _dot_nt(stack_heads(q_mid), k_mid)
        attn = jnp.where(pair_mask, attn, 0.0).astype(BF16)
        s_prev = s_scr[...]
        o_inter = _dot(stack_heads(q_dec), s_prev.astype(BF16))
        kv = []
        for h in range(GLA_HEADS):
            rs = slice(h * c_len, (h + 1) * c_len)
            vs = slice(h * GLA_DV, (h + 1) * GLA_DV)
            o_h = o_inter[rs] + _dot(attn[rs], vc[:, vs])
            o_ref[sl, vs] = o_h.astype(BF16)
            kv.append(_dot_tn(k_dec[:, h * GLA_DK:(h + 1) * GLA_DK].astype(BF16), vc[:, vs]))
        s_scr[...] = s_prev * jnp.exp(tot_cols[:, c:c + 1]) + jnp.concatenate(kv, axis=0)

    return step


def _gla_kernel(qf_ref, kf_ref, vf_ref, mf_ref, qb_ref, kb_ref, vb_ref, mb_ref,
                wg_ref, bg_ref, tri_ref, s0_ref, of_ref, ob_ref, sfin_ref, sf_scr, sb_scr, *, n_chunks):
    blk = pl.program_id(1)

    @pl.when(blk == 0)
    def _():
        sf_scr[...] = s0_ref[0, 0]
        sb_scr[...] = s0_ref[1, 0]

    fwd = _gla_direction(qf_ref, kf_ref, vf_ref, mf_ref, wg_ref[0], bg_ref[0], tri_ref[0],
                         sf_scr, of_ref, reverse=False, n_chunks=n_chunks)
    bwd = _gla_direction(qb_ref, kb_ref, vb_ref, mb_ref, wg_ref[1], bg_ref[1], tri_ref[1],
                         sb_scr, ob_ref, reverse=True, n_chunks=n_chunks)
    for c in range(n_chunks):
        fwd(c)
        bwd(n_chunks - 1 - c)

    @pl.when(blk == pl.num_programs(1) - 1)
    def _():
        sfin_ref[0, 0] = sf_scr[...]
        sfin_ref[1, 0] = sb_scr[...]


def _block_diag_tri(n_chunks):
    c = GLA_CHUNK
    eye = np.eye(n_chunks, dtype=np.float32)
    lower = np.kron(eye, np.tril(np.ones((c, c), np.float32)))
    upper = np.kron(eye, np.triu(np.ones((c, c), np.float32)))
    return jnp.asarray(np.stack([lower, upper]), dtype=BF16)


def _gla(p, wg, bg, s0, *, batch, cb):
    t_all = p.shape[0]
    nblk = t_all // batch // cb
    n_chunks = cb // GLA_CHUNK
    assert n_chunks <= 8

    fw = lambda b, i: b * nblk + i
    bw = lambda b, i: b * nblk + (nblk - 1 - i)
    full = lambda a: pl.BlockSpec(a.shape, lambda b, i: (0,) * a.ndim)
    tri = _block_diag_tri(n_chunks)

    def token_specs(tok):
        return [
            pl.BlockSpec((cb, GLA_QK_W), lambda b, i: (tok(b, i), P_Q // GLA_QK_W)),
            pl.BlockSpec((cb, GLA_QK_W), lambda b, i: (tok(b, i), P_K // GLA_QK_W)),
            pl.BlockSpec((cb, GLA_V_W), lambda b, i: (tok(b, i), P_V // GLA_V_W)),
            pl.BlockSpec((cb, LANES), lambda b, i: (tok(b, i), P_MISC // LANES)),
        ]

    state_spec = pl.BlockSpec((2, 1, GLA_QK_W, GLA_DV), lambda b, i: (0, b, 0, 0))
    return pl.pallas_call(
        functools.partial(_gla_kernel, n_chunks=n_chunks),
        grid=(batch, nblk),
        in_specs=token_specs(fw) + token_specs(bw) + [full(wg), full(bg), full(tri), state_spec],
        out_specs=[
            pl.BlockSpec((cb, GLA_V_W), lambda b, i: (fw(b, i), 0)),
            pl.BlockSpec((cb, GLA_V_W), lambda b, i: (bw(b, i), 0)),
            state_spec,
        ],
        out_shape=[
            jax.ShapeDtypeStruct((t_all, GLA_V_W), BF16),
            jax.ShapeDtypeStruct((t_all, GLA_V_W), BF16),
            jax.ShapeDtypeStruct((2, batch, GLA_QK_W, GLA_DV), F32),
        ],
        scratch_shapes=[pltpu.VMEM((GLA_QK_W, GLA_DV), F32), pltpu.VMEM((GLA_QK_W, GLA_DV), F32)],
        compiler_params=_cparams(("arbitrary", "arbitrary")),
        name="gla",
    )(p, p, p, p, p, p, p, p, wg, bg, tri, s0)


def _mlaprep_kernel(cq_ref, ckv_ref, misc_ref, cos_ref, sin_ref, qg_ref, kvg_ref,
                    wqn_ref, wqr_ref, wqs_ref, wknt_ref, wv_ref, perm_ref, eye_ref,
                    q_ref, kt_ref, v_ref):
    cos = cos_ref[...]
    sin = sin_ref[...]
    cqn = _rms(cq_ref[...].astype(F32), qg_ref[...]).astype(BF16)
    qn = _dot(cqn, wqn_ref[...])
    qr = _dot(cqn, wqr_ref[...])
    qs = _dot(cqn, wqs_ref[...])
    for h in range(MLA_HEADS):
        ls = slice(h * LANES, (h + 1) * LANES)
        q_ref[0, h, :, 0:MLA_NOPE] = (qn[:, ls] * MLA_Q_SCALE).astype(BF16)
        rot = qr[:, ls] * cos + qs[:, ls] * sin
        q_ref[0, h, :, MLA_NOPE:MLA_QK] = (rot[:, 0:MLA_ROPE] * MLA_Q_SCALE).astype(BF16)

    ckvn = _rms(ckv_ref[...].astype(F32), kvg_ref[...]).astype(BF16)
    knt = _dot_nt(wknt_ref[...], ckvn)
    vv = _dot(ckvn, wv_ref[...])
    misc = misc_ref[...]
    kr = misc.astype(F32) * cos + _dot(misc, perm_ref[...]) * sin
    krt = _dot_nt(eye_ref[...], kr.astype(BF16)).astype(BF16)
    for h in range(MLA_HEADS):
        kt_ref[0, h, 0, 0:MLA_NOPE, :] = knt[h * MLA_NOPE:(h + 1) * MLA_NOPE].astype(BF16)
        kt_ref[0, h, 0, MLA_NOPE:MLA_QK, :] = krt
        v_ref[0, h, :, 0:MLA_V] = vv[:, h * MLA_V:(h + 1) * MLA_V].astype(BF16)
        v_ref[0, h, :, MLA_V:MLA_V_EXT] = jnp.ones((vv.shape[0], MLA_V), BF16)


def _mlaprep(p, cos, sin, qg, kvg, wts, *, batch, tm):
    t_all = p.shape[0]
    t = t_all // batch
    nb = t // tm
    ntab = cos.shape[0] // tm
    wqn, wqr, wqs, wknt, wv, perm, eye = wts
    full = lambda a: pl.BlockSpec(a.shape, lambda b, i: (0,) * a.ndim)
    return pl.pallas_call(
        _mlaprep_kernel,
        grid=(batch, nb),
        in_specs=[
            pl.BlockSpec((tm, MLA_Q_RANK), lambda b, i: (b * nb + i, P_CQ // MLA_Q_RANK)),
            pl.BlockSpec((tm, MLA_KV_RANK), lambda b, i: (b * nb + i, P_CKV // MLA_KV_RANK)),
            pl.BlockSpec((tm, LANES), lambda b, i: (b * nb + i, P_MISC // LANES)),
            pl.BlockSpec((tm, LANES), lambda b, i: (i % ntab, 0)),
            pl.BlockSpec((tm, LANES), lambda b, i: (i % ntab, 0)),
            full(qg), full(kvg), full(wqn), full(wqr), full(wqs), full(wknt), full(wv),
            full(perm), full(eye),
        ],
        out_specs=[
            pl.BlockSpec((1, MLA_HEADS, tm, MLA_QK), lambda b, i: (b, 0, i, 0)),
            pl.BlockSpec((1, MLA_HEADS, 1, MLA_QK, tm), lambda b, i: (b, 0, i, 0, 0)),
            pl.BlockSpec((1, MLA_HEADS, tm, MLA_V_EXT), lambda b, i: (b, 0, i, 0)),
        ],
        out_shape=[
            jax.ShapeDtypeStruct((batch, MLA_HEADS, t, MLA_QK), BF16),
            jax.ShapeDtypeStruct((batch, MLA_HEADS, nb, MLA_QK, tm), BF16),
            jax.ShapeDtypeStruct((batch, MLA_HEADS, t, MLA_V_EXT), BF16),
        ],
        compiler_params=_cparams(("arbitrary", "arbitrary")),
        name="mlaprep",
    )(p, p, p, cos, sin, qg, kvg, wqn, wqr, wqs, wknt, wv, perm, eye)


def _attn_kernel(*refs, n_seg, n_sub):
    q_ref = refs[0]
    kt_refs = refs[1:1 + 2 * n_seg:2]
    v_refs = refs[2:2 + 2 * n_seg:2]
    o_ref = refs[1 + 2 * n_seg]
    m_scr, acc_scr = refs[2 + 2 * n_seg:]

    rows_per_sub = q_ref.shape[2] // n_sub
    m_scr[...] = jnp.full(m_scr.shape, -jnp.inf, F32)
    acc_scr[...] = jnp.zeros(acc_scr.shape, F32)

    for kt_ref, v_ref in zip(kt_refs, v_refs):
        n_blocks, tk = kt_ref.shape[2], kt_ref.shape[4]

        def step(j, carry, kt_ref=kt_ref, v_ref=v_ref, tk=tk):
            kt = kt_ref[0, 0, j]
            v_blk = v_ref[0, 0, pl.ds(pl.multiple_of(j * tk, tk), tk), :]
            for u in range(n_sub):
                rows = slice(u * rows_per_sub, (u + 1) * rows_per_sub)
                s = _dot(q_ref[0, 0, rows, :], kt)
                m_prev = m_scr[rows, :]
                m_next = jnp.maximum(m_prev, jnp.max(s, axis=1, keepdims=True))
                p = jnp.exp2((s - jnp.concatenate([m_next] * (tk // LANES), axis=1)).astype(BF16))
                alpha = jnp.exp2(m_prev - m_next)
                acc_scr[rows, :] = (jnp.concatenate([alpha] * (MLA_V_EXT // LANES), axis=1)
                                    * acc_scr[rows, :] + _dot(p, v_blk))
                m_scr[rows, :] = m_next
            return carry

        lax.fori_loop(0, n_blocks, step, 0)

    o_ref[0] = (acc_scr[:, 0:MLA_V] / acc_scr[:, MLA_V:MLA_V_EXT]).astype(BF16)


def _attn_pipe_kernel(q_ref, kt_ref, v_ref, ktt_ref, vt_ref, o_ref,
                      m_scr, acc_scr, s0_scr, s1_scr, st_scr, *, n_sub):
    n_blocks, tk = kt_ref.shape[2], kt_ref.shape[4]
    rows_per_sub = q_ref.shape[2] // n_sub
    subs = [slice(u * rows_per_sub, (u + 1) * rows_per_sub) for u in range(n_sub)]
    m_scr[...] = jnp.full(m_scr.shape, -jnp.inf, F32)
    acc_scr[...] = jnp.zeros(acc_scr.shape, F32)

    def scores(kt, s_ref):
        for rows in subs:
            s_ref[rows, :] = _dot(q_ref[0, 0, rows, :], kt)

    def softmax_pv(s_ref, v_blk):
        width = s_ref.shape[1]
        for rows in subs:
            s = s_ref[rows, :]
            m_prev = m_scr[rows, :]
            m_next = jnp.maximum(m_prev, jnp.max(s, axis=1, keepdims=True))
            p = jnp.exp2((s - jnp.concatenate([m_next] * (width // LANES), axis=1)).astype(BF16))
            alpha = jnp.exp2(m_prev - m_next)
            acc_scr[rows, :] = (jnp.concatenate([alpha] * (MLA_V_EXT // LANES), axis=1)
                                * acc_scr[rows, :] + _dot(p, v_blk))
            m_scr[rows, :] = m_next

    def v_main(j):
        return v_ref[0, 0, pl.ds(pl.multiple_of(j * tk, tk), tk), :]

    scores(kt_ref[0, 0, 0], s0_scr)
    n_pairs = (n_blocks - 1) // 2

    def pair(jj, carry):
        j = 2 * jj
        scores(kt_ref[0, 0, j + 1], s1_scr)
        softmax_pv(s0_scr, v_main(j))
        scores(kt_ref[0, 0, j + 2], s0_scr)
        softmax_pv(s1_scr, v_main(j + 1))
        return carry

    lax.fori_loop(0, n_pairs, pair, 0)
    if n_blocks - 2 * n_pairs == 2:
        scores(kt_ref[0, 0, n_blocks - 1], s1_scr)
        softmax_pv(s0_scr, v_main(n_blocks - 2))
        scores(ktt_ref[0, 0, 0], st_scr)
        softmax_pv(s1_scr, v_main(n_blocks - 1))
    else:
        scores(ktt_ref[0, 0, 0], st_scr)
        softmax_pv(s0_scr, v_main(n_blocks - 1))
    softmax_pv(st_scr, vt_ref[0, 0])

    o_ref[0] = (acc_scr[:, 0:MLA_V] / acc_scr[:, MLA_V:MLA_V_EXT]).astype(BF16)


def _attention_pipelined(q, kt, v, kt_tail, v_tail, *, tq, n_sub):
    b, h, t, dqk = q.shape
    tk, tt = kt.shape[4], kt_tail.shape[4]
    assert kt_tail.shape[2] == 1
    return pl.pallas_call(
        functools.partial(_attn_pipe_kernel, n_sub=n_sub),
        grid=(b, h, t // tq),
        in_specs=[
            pl.BlockSpec((1, 1, tq, dqk), lambda bi, hi, qi: (bi, hi, qi, 0)),
            pl.BlockSpec((1, 1) + kt.shape[2:], lambda bi, hi, qi: (bi, hi, 0, 0, 0)),
            pl.BlockSpec((1, 1) + v.shape[2:], lambda bi, hi, qi: (bi, hi, 0, 0)),
            pl.BlockSpec((1, 1) + kt_tail.shape[2:], lambda bi, hi, qi: (bi, hi, 0, 0, 0)),
            pl.BlockSpec((1, 1) + v_tail.shape[2:], lambda bi, hi, qi: (bi, hi, 0, 0)),
        ],
        out_specs=pl.BlockSpec((1, tq, MLA_V), lambda bi, hi, qi: (bi, qi, hi)),
        out_shape=jax.ShapeDtypeStruct((b, t, h * MLA_V), BF16),
        scratch_shapes=[pltpu.VMEM((tq, LANES), F32), pltpu.VMEM((tq, MLA_V_EXT), F32),
                        pltpu.VMEM((tq, tk), F32), pltpu.VMEM((tq, tk), F32),
                        pltpu.VMEM((tq, tt), F32)],
        compiler_params=_cparams(("arbitrary", "arbitrary", "arbitrary")),
        name="mla_attention_pipe",
    )(q, kt, v, kt_tail, v_tail)


def _attention(q, segs, *, tq, n_sub):
    b, h, t, dqk = q.shape
    in_specs = [pl.BlockSpec((1, 1, tq, dqk), lambda bi, hi, qi: (bi, hi, qi, 0))]
    args = [q]
    for kt, v in segs:
        in_specs.append(pl.BlockSpec((1, 1) + kt.shape[2:], lambda bi, hi, qi: (bi, hi, 0, 0, 0)))
        in_specs.append(pl.BlockSpec((1, 1) + v.shape[2:], lambda bi, hi, qi: (bi, hi, 0, 0)))
        args += [kt, v]
    return pl.pallas_call(
        functools.partial(_attn_kernel, n_seg=len(segs), n_sub=n_sub),
        grid=(b, h, t // tq),
        in_specs=in_specs,
        out_specs=pl.BlockSpec((1, tq, MLA_V), lambda bi, hi, qi: (bi, qi, hi)),
        out_shape=jax.ShapeDtypeStruct((b, t, h * MLA_V), BF16),
        scratch_shapes=[pltpu.VMEM((tq, LANES), F32), pltpu.VMEM((tq, MLA_V_EXT), F32)],
        compiler_params=_cparams(("arbitrary", "arbitrary", "arbitrary")),
        name="mla_attention",
    )(*args)


def _outproj_kernel(*refs, with_router):
    (x_ref, of_ref, ob_ref, r_ref, mla_ref, mod_ref, gg_ref, wo_ref, ln2_ref) = refs[:9]
    if with_router:
        rwh_ref, rwl_ref, x1_ref, h2_ref, comb_ref = refs[9:]
    else:
        x1_ref, h2_ref = refs[9:]
    m = mod_ref[0]
    o = of_ref[...].astype(F32) + ob_ref[...].astype(F32)
    gg = gg_ref[...]
    y = jnp.concatenate(
        [_rms(o[:, h * GLA_DV:(h + 1) * GLA_DV], gg) for h in range(GLA_HEADS)], axis=1)
    mix = (y * _silu(r_ref[...].astype(F32))).astype(BF16)
    yo = _dot(mix, wo_ref[0:GLA_V_W, :]) + _dot(mla_ref[...], wo_ref[GLA_V_W:, :])
    x1 = x_ref[...] + m[2:3] * yo
    x1_ref[...] = x1
    h2 = _rms(x1, ln2_ref[...]) * (1.0 + m[4:5]) + m[3:4]
    if with_router:
        h2_ref[...] = _pack_bf16_pairs(h2)
    else:
        h2_ref[...] = h2.astype(BF16)
    if with_router:
        h_hi = h2.astype(BF16)
        h_lo = (h2 - h_hi.astype(F32)).astype(BF16)
        logits = _dot(h_hi, rwh_ref[...]) + _dot(h_lo, rwh_ref[...]) + _dot(h_hi, rwl_ref[...])
        lane = lax.broadcasted_iota(jnp.int32, logits.shape, 1).astype(F32)
        neg = jnp.float32(-jnp.inf)
        logits = jnp.where(lane < N_EXPERTS, logits, neg)
        m1 = jnp.max(logits, axis=1, keepdims=True)
        i1 = jnp.min(jnp.where(logits == m1, lane, float(LANES)), axis=1, keepdims=True)
        rest = jnp.where(lane == i1, neg, logits)
        m2 = jnp.max(rest, axis=1, keepdims=True)
        i2 = jnp.min(jnp.where(rest == m2, lane, float(LANES)), axis=1, keepdims=True)
        e2 = jnp.exp(m2 - m1)
        w1 = 1.0 / (1.0 + e2)
        comb_ref[...] = (jnp.where(lane == ROUTE_E1, i1, 0.0) + jnp.where(lane == ROUTE_E2, i2, 0.0)
                         + jnp.where(lane == ROUTE_W1, w1, 0.0)
                         + jnp.where(lane == ROUTE_W2, e2 * w1, 0.0))


def _outproj(x, o_f, o_b, p, mla, mods, row_fn, gg, wo, ln2, router, tm):
    t, d = x.shape
    with_router = router is not None
    full = lambda a: pl.BlockSpec(a.shape, lambda i: (0,) * a.ndim)
    in_specs = [
        pl.BlockSpec((tm, d), lambda i: (i, 0)),
        pl.BlockSpec((tm, GLA_V_W), lambda i: (i, 0)),
        pl.BlockSpec((tm, GLA_V_W), lambda i: (i, 0)),
        pl.BlockSpec((tm, GLA_V_W), lambda i: (i, P_R // GLA_V_W)),
        pl.BlockSpec((tm, MLA_V_W), lambda i: (i, 0)),
        pl.BlockSpec((1, 6, d), lambda i: (row_fn(i), 0, 0)),
        full(gg), full(wo), full(ln2),
    ]
    args = [x, o_f, o_b, p, mla, mods, gg, wo, ln2]
    h2_shape = jax.ShapeDtypeStruct((t, d // 2), F32) if with_router else jax.ShapeDtypeStruct((t, d), BF16)
    out_specs = [pl.BlockSpec((tm, d), lambda i: (i, 0)),
                 pl.BlockSpec((tm, h2_shape.shape[1]), lambda i: (i, 0))]
    out_shape = [jax.ShapeDtypeStruct((t, d), F32), h2_shape]
    if with_router:
        in_specs += [full(router[0]), full(router[1])]
        args += list(router)
        out_specs.append(pl.BlockSpec((tm, LANES), lambda i: (i, 0)))
        out_shape.append(jax.ShapeDtypeStruct((t, LANES), F32))
    return pl.pallas_call(
        functools.partial(_outproj_kernel, with_router=with_router),
        grid=(t // tm,),
        in_specs=in_specs,
        out_specs=out_specs,
        out_shape=out_shape,
        compiler_params=_cparams(("arbitrary",)),
        name="outproj",
    )(*args)


def _ffn_kernel(*refs, with_comb, final_norm):
    h_ref, x1_ref, mod_ref = refs[:3]
    k = 3
    comb_ref = fin_ref = None
    if with_comb:
        comb_ref = refs[k]
        k += 1
    wg_ref, wu_ref, wd_ref = refs[k:k + 3]
    k += 3
    if final_norm:
        fin_ref = refs[k]
        k += 1
    o_ref, acc = refs[k:]
    e = pl.program_id(1)
    f = pl.program_id(2)

    @pl.when((e == 0) & (f == 0))
    def _():
        acc[...] = jnp.zeros(acc.shape, F32)

    h = h_ref[...]
    a = _dot(h, wg_ref[0])
    u = _dot(h, wu_ref[0])
    act = _silu(a) * u
    if with_comb:
        comb = comb_ref[...]
        lane = lax.broadcasted_iota(jnp.int32, comb.shape, 1)
        act = act * jnp.sum(jnp.where(lane == e, comb, 0.0), axis=1, keepdims=True)
    acc[...] += _dot(act.astype(BF16), wd_ref[0])

    @pl.when((e == pl.num_programs(1) - 1) & (f == pl.num_programs(2) - 1))
    def _():
        x2 = x1_ref[...] + mod_ref[0][5:6] * acc[...]
        if final_norm:
            x2 = _rms(x2, fin_ref[...])
        o_ref[...] = x2


def _ffn(h2, x1, mods, row_fn, comb, wg, wu, wd, fin_g, tm, tf):
    t, d = x1.shape
    n_e, _, ff = wg.shape
    with_comb = comb is not None
    final_norm = fin_g is not None
    in_specs = [
        pl.BlockSpec((tm, d), lambda i, e, f: (i, 0)),
        pl.BlockSpec((tm, d), lambda i, e, f: (i, 0)),
        pl.BlockSpec((1, 6, d), lambda i, e, f: (row_fn(i), 0, 0)),
    ]
    args = [h2, x1, mods]
    if with_comb:
        in_specs.append(pl.BlockSpec((tm, LANES), lambda i, e, f: (i, 0)))
        args.append(comb)
    in_specs += [
        pl.BlockSpec((1, d, tf), lambda i, e, f: (e, 0, f)),
        pl.BlockSpec((1, d, tf), lambda i, e, f: (e, 0, f)),
        pl.BlockSpec((1, tf, d), lambda i, e, f: (e, f, 0)),
    ]
    args += [wg, wu, wd]
    if final_norm:
        in_specs.append(pl.BlockSpec((1, d), lambda i, e, f: (0, 0)))
        args.append(fin_g)
    return pl.pallas_call(
        functools.partial(_ffn_kernel, with_comb=with_comb, final_norm=final_norm),
        grid=(t // tm, n_e, ff // tf),
        in_specs=in_specs,
        out_specs=pl.BlockSpec((tm, d), lambda i, e, f: (i, 0)),
        out_shape=jax.ShapeDtypeStruct((t, d), F32),
        scratch_shapes=[pltpu.VMEM((tm, d), F32)],
        compiler_params=_cparams(("arbitrary", "arbitrary", "arbitrary")),
        name="ffn",
    )(*args)


def _dense_ffn_kernel(*refs, groups, final_norm):
    h_ref, x1_ref, mod_ref, wg_ref, wu_ref, wd_ref = refs[:6]
    fin_ref = refs[6] if final_norm else None
    o_ref = refs[-1]
    h = h_ref[...]
    y = None
    for lo, hi in groups:
        a = _dot(h, wg_ref[:, lo:hi])
        u = _dot(h, wu_ref[:, lo:hi])
        part = _dot((_silu(a) * u).astype(BF16), wd_ref[lo:hi, :])
        y = part if y is None else y + part
    x2 = x1_ref[...] + mod_ref[0][5:6] * y
    if final_norm:
        x2 = _rms(x2, fin_ref[...])
    o_ref[...] = x2


def _dense_ffn(h2, x1, mods, row_fn, wg, wu, wd, fin_g, tm):
    t, d = x1.shape
    ff = wg.shape[1]
    final_norm = fin_g is not None
    step = 1024 if ff > 1024 else ff
    groups = tuple((lo, min(lo + step, ff)) for lo in range(0, ff, step))
    resident = lambda a: pl.BlockSpec(a.shape, lambda i: (0, 0), pipeline_mode=pl.Buffered(1))
    in_specs = [
        pl.BlockSpec((tm, d), lambda i: (i, 0)),
        pl.BlockSpec((tm, d), lambda i: (i, 0)),
        pl.BlockSpec((1, 6, d), lambda i: (row_fn(i), 0, 0)),
        resident(wg), resident(wu), resident(wd),
    ]
    args = [h2, x1, mods, wg, wu, wd]
    if final_norm:
        in_specs.append(pl.BlockSpec((1, d), lambda i: (0, 0)))
        args.append(fin_g)
    return pl.pallas_call(
        functools.partial(_dense_ffn_kernel, groups=groups, final_norm=final_norm),
        grid=(t // tm,),
        in_specs=in_specs,
        out_specs=pl.BlockSpec((tm, d), lambda i: (i, 0)),
        out_shape=jax.ShapeDtypeStruct((t, d), F32),
        compiler_params=_cparams(("arbitrary",)),
        name="dense_ffn",
    )(*args)


def _sc_row_gather(table, idx):
    _, w = table.shape
    b = idx.shape[0]
    n_workers = SC_CORES * SC_SUBCORES
    assert b % (n_workers * SC_GATHER_ROWS) == 0, (b, n_workers, SC_GATHER_ROWS)
    b_per_w = b // n_workers
    n_chunks = b_per_w // SC_GATHER_ROWS
    mesh = plsc.VectorSubcoreMesh(core_axis_name="c", subcore_axis_name="s",
                                  num_cores=SC_CORES, num_subcores=SC_SUBCORES)

    def body(table_hbm, idx_hbm, out_hbm, idx_v, rows_v, sem):
        wid = lax.axis_index("s") * SC_CORES + lax.axis_index("c")
        base = wid * b_per_w

        @pl.loop(0, n_chunks)
        def _(ci):
            off = base + ci * SC_GATHER_ROWS
            pltpu.sync_copy(idx_hbm.at[pl.ds(off, SC_GATHER_ROWS)], idx_v)
            pltpu.async_copy(table_hbm.at[idx_v], rows_v, sem).wait()
            pltpu.sync_copy(rows_v, out_hbm.at[pl.ds(off, SC_GATHER_ROWS)])

    return pl.kernel(
        body,
        out_type=jax.ShapeDtypeStruct((b, w), F32),
        mesh=mesh,
        scratch_types=[pltpu.VMEM((SC_GATHER_ROWS,), jnp.int32),
                       pltpu.VMEM((SC_GATHER_ROWS, w), F32),
                       pltpu.SemaphoreType.DMA],
        name="sc_row_gather",
    )(table, idx)


def _sc_row_scatter2(table, pos, n_out):
    t, w = table.shape
    n_workers = SC_CORES * SC_SUBCORES
    assert t % (n_workers * SC_GATHER_ROWS) == 0, (t, n_workers, SC_GATHER_ROWS)
    t_per_w = t // n_workers
    n_chunks = t_per_w // SC_GATHER_ROWS
    mesh = plsc.VectorSubcoreMesh(core_axis_name="c", subcore_axis_name="s",
                                  num_cores=SC_CORES, num_subcores=SC_SUBCORES)

    def body(table_hbm, pos_hbm, out_hbm, idx0_v, idx1_v, rows_v, sem):
        wid = lax.axis_index("s") * SC_CORES + lax.axis_index("c")
        base = wid * t_per_w

        @pl.loop(0, n_chunks)
        def _(ci):
            off = base + ci * SC_GATHER_ROWS
            pltpu.sync_copy(pos_hbm.at[0, pl.ds(off, SC_GATHER_ROWS)], idx0_v)
            pltpu.sync_copy(pos_hbm.at[1, pl.ds(off, SC_GATHER_ROWS)], idx1_v)
            pltpu.sync_copy(table_hbm.at[pl.ds(off, SC_GATHER_ROWS)], rows_v)
            first = pltpu.async_copy(rows_v, out_hbm.at[idx0_v], sem)
            second = pltpu.async_copy(rows_v, out_hbm.at[idx1_v], sem)
            first.wait()
            second.wait()

    return pl.kernel(
        body,
        out_type=jax.ShapeDtypeStruct((n_out, w), F32),
        mesh=mesh,
        scratch_types=[pltpu.VMEM((SC_GATHER_ROWS,), jnp.int32),
                       pltpu.VMEM((SC_GATHER_ROWS,), jnp.int32),
                       pltpu.VMEM((SC_GATHER_ROWS, w), F32),
                       pltpu.SemaphoreType.DMA],
        name="sc_row_scatter",
    )(table, pos)


def _moe_plan_kernel(route_ref, tri_ref, utri_ref, pos_ref, cnt_ref, run_scr, off_scr):
    phase = pl.program_id(0)
    blk = pl.program_id(1)
    route = route_ref[...]
    lane = lax.broadcasted_iota(jnp.int32, route.shape, 1).astype(F32)
    oh1 = jnp.where(lane == route[:, ROUTE_E1:ROUTE_E1 + 1], 1.0, 0.0)
    oh2 = jnp.where(lane == route[:, ROUTE_E2:ROUTE_E2 + 1], 1.0, 0.0)
    oh = oh1 + oh2

    @pl.when(blk == 0)
    def _():
        run_scr[...] = jnp.zeros(run_scr.shape, F32)

    @pl.when(phase == 0)
    def _():
        run_scr[...] += jnp.sum(oh, axis=0, keepdims=True)

        @pl.when(blk == pl.num_programs(1) - 1)
        def _():
            counts = run_scr[...]
            cnt_ref[...] = counts
            tiles_per = jnp.floor((counts + (MOE_TILE - 1.0)) * (1.0 / MOE_TILE))
            tile_end = _dot(jnp.broadcast_to(tiles_per, (8, LANES)).astype(BF16), utri_ref[...])[0:1]
            off_scr[...] = (tile_end - tiles_per) * float(MOE_TILE)

    @pl.when(phase == 1)
    def _():
        incl = _dot(tri_ref[...], oh.astype(BF16))
        before = incl - oh + run_scr[...] + off_scr[...]
        p1 = jnp.sum(before * oh1, axis=1, keepdims=True)
        p2 = jnp.sum(before * oh2, axis=1, keepdims=True)
        pos = jnp.where(lane == 0.0, p1, 0.0) + jnp.where(lane == 1.0, p2, 0.0)
        pos_ref[...] = pos.astype(jnp.int32)
        run_scr[...] += incl[incl.shape[0] - 1:, :]


def _moe_plan(route, n_tiles, tm):
    t = route.shape[0]
    tri = jnp.asarray(np.tril(np.ones((tm, tm), np.float32)), dtype=BF16)
    utri = jnp.asarray(np.triu(np.ones((LANES, LANES), np.float32)), dtype=BF16)
    pos, counts = pl.pallas_call(
        _moe_plan_kernel,
        grid=(2, t // tm),
        in_specs=[
            pl.BlockSpec((tm, LANES), lambda p, i: (i, 0)),
            pl.BlockSpec((tm, tm), lambda p, i: (0, 0)),
            pl.BlockSpec((LANES, LANES), lambda p, i: (0, 0)),
        ],
        out_specs=[
            pl.BlockSpec((tm, LANES), lambda p, i: (i * p, 0)),
            pl.BlockSpec((1, LANES), lambda p, i: (0, 0)),
        ],
        out_shape=[jax.ShapeDtypeStruct((t, LANES), jnp.int32),
                   jax.ShapeDtypeStruct((1, LANES), F32)],
        scratch_shapes=[pltpu.VMEM((1, LANES), F32), pltpu.VMEM((1, LANES), F32)],
        compiler_params=_cparams(("arbitrary", "arbitrary")),
        name="moe_plan",
    )(route, tri, utri)
    counts = counts[0, :N_EXPERTS].astype(jnp.int32)
    tile_end = jnp.cumsum((counts + MOE_TILE - 1) // MOE_TILE)
    n_used = tile_end[-1]
    tile_ids = jnp.minimum(jnp.arange(n_tiles, dtype=jnp.int32), n_used - 1)
    tile_expert = jnp.sum((tile_end[None, :] <= tile_ids[:, None]).astype(jnp.int32), axis=1)
    return pos[:, :2].T, tile_expert, n_used.reshape(1)


def _moe_ffn_kernel(te_ref, nused_ref, xs_ref, wg_ref, wu_ref, wd_ref, o_ref, acc, h_scr, *, splits):
    i = pl.program_id(0)
    f = pl.program_id(1)
    last_f = pl.num_programs(1) - 1
    used = i < nused_ref[0]

    @pl.when(used)
    def _():
        @pl.when(f == 0)
        def _():
            acc[...] = jnp.zeros(acc.shape, F32)
            h_scr[...] = _unpack_bf16_pairs(xs_ref[...]).astype(BF16)

        h = h_scr[...]
        for lo, hi in splits:
            a = _dot(h, wg_ref[0, :, lo:hi].astype(BF16))
            u = _dot(h, wu_ref[0, :, lo:hi].astype(BF16))
            acc[...] += _dot((_silu(a) * u).astype(BF16), wd_ref[0, lo:hi, :].astype(BF16))

        @pl.when(f == last_f)
        def _():
            o_ref[...] = _pack_bf16_pairs(acc[...])

    @pl.when(jnp.logical_not(used) & (f == last_f))
    def _():
        o_ref[...] = jnp.zeros(o_ref.shape, F32)


def _moe_ffn(xs, tile_expert, n_used, wg, wu, wd):
    rows, half = xs.shape
    d = 2 * half
    ff = wg.shape[2]
    if ff % MOE_FF_TILE == 0:
        tf, splits = MOE_FF_TILE, MOE_FF_SPLITS
    else:
        tf, splits = ff, ((0, ff),)
    n_tiles = rows // MOE_TILE
    grid_spec = pltpu.PrefetchScalarGridSpec(
        num_scalar_prefetch=2,
        grid=(n_tiles, ff // tf),
        in_specs=[
            pl.BlockSpec((MOE_TILE, half), lambda i, f, te, nu: (i, 0)),
            pl.BlockSpec((1, d, tf), lambda i, f, te, nu: (te[i], 0, f)),
            pl.BlockSpec((1, d, tf), lambda i, f, te, nu: (te[i], 0, f)),
            pl.BlockSpec((1, tf, d), lambda i, f, te, nu: (te[i], f, 0)),
        ],
        out_specs=pl.BlockSpec((MOE_TILE, half), lambda i, f, te, nu: (i, 0)),
        scratch_shapes=[pltpu.VMEM((MOE_TILE, d), F32), pltpu.VMEM((MOE_TILE, d), BF16)],
    )
    return pl.pallas_call(
        functools.partial(_moe_ffn_kernel, splits=splits),
        grid_spec=grid_spec,
        out_shape=jax.ShapeDtypeStruct((rows, half), F32),
        compiler_params=_cparams(("arbitrary", "arbitrary")),
        name="moe_ffn",
    )(tile_expert, n_used, xs, wg, wu, wd)


def _combine_kernel(*refs, final_norm):
    x1_ref, y0_ref, y1_ref, route_ref, mod_ref = refs[:5]
    fin_ref = refs[5] if final_norm else None
    o_ref = refs[-1]
    route = route_ref[...]
    w1 = route[:, ROUTE_W1:ROUTE_W1 + 1]
    w2 = route[:, ROUTE_W2:ROUTE_W2 + 1]
    y = w1 * _unpack_bf16_pairs(y0_ref[...]) + w2 * _unpack_bf16_pairs(y1_ref[...])
    x2 = x1_ref[...] + mod_ref[0][5:6] * y
    if final_norm:
        x2 = _rms(x2, fin_ref[...])
    o_ref[...] = x2


def _combine(x1, yg, route, mods, row_fn, fin_g, tm):
    t, d = x1.shape
    nb = t // tm
    final_norm = fin_g is not None
    in_specs = [
        pl.BlockSpec((tm, d), lambda i: (i, 0)),
        pl.BlockSpec((tm, d // 2), lambda i: (i, 0)),
        pl.BlockSpec((tm, d // 2), lambda i: (i + nb, 0)),
        pl.BlockSpec((tm, LANES), lambda i: (i, 0)),
        pl.BlockSpec((1, 6, d), lambda i: (row_fn(i), 0, 0)),
    ]
    args = [x1, yg, yg, route, mods]
    if final_norm:
        in_specs.append(pl.BlockSpec((1, d), lambda i: (0, 0)))
        args.append(fin_g)
    return pl.pallas_call(
        functools.partial(_combine_kernel, final_norm=final_norm),
        grid=(nb,),
        in_specs=in_specs,
        out_specs=pl.BlockSpec((tm, d), lambda i: (i, 0)),
        out_shape=jax.ShapeDtypeStruct((t, d), F32),
        compiler_params=_cparams(("arbitrary",)),
        name="moe_combine",
    )(*args)


def _moe(h2, x1, route, mods, row_fn, wg, wu, wd, fin_g, tm):
    t = h2.shape[0]
    n_tiles = -(-2 * t // MOE_TILE) + N_EXPERTS
    pos, tile_expert, n_used = _moe_plan(route, n_tiles, tm)
    xs = _sc_row_scatter2(h2, pos, n_tiles * MOE_TILE)
    ys = _moe_ffn(xs, tile_expert, n_used, wg, wu, wd)
    yg = _sc_row_gather(ys, pos.reshape(-1))
    return _combine(x1, yg, route, mods, row_fn, fin_g, tm)


def _rope_partner():
    j = np.arange(MLA_ROPE)
    return np.where((j % 32) < 16, j + 16, j - 16)


def _prep_in_weight(w):
    d = w.shape[0]
    cols = [w[:, 0:1024], w[:, 1056:1568], w[:, 1568:1824], w[:, 1824:1952], w[:, 1952:2016],
            w[:, 1024:1056], jnp.zeros((d, P_WIDTH - 2016), w.dtype)]
    return jnp.concatenate(cols, axis=1).astype(BF16)


def _prep_gate_weight(w_g2, b_g2):
    ws = []
    for z, off in ((0, MISC_GF), (1, MISC_GB)):
        ws.append(jnp.zeros((LANES, GLA_QK_W), F32).at[off:off + GLA_GATE_RANK].set(w_g2[z]))
    return jnp.stack(ws).astype(BF16), b_g2.reshape(2, 1, GLA_QK_W)


def _prep_mla_weights(w_uq, w_ukv):
    partner = _rope_partner()
    wq = w_uq.reshape(MLA_Q_RANK, MLA_HEADS, MLA_QK)
    wqn = wq[:, :, :MLA_NOPE].reshape(MLA_Q_RANK, MLA_HEADS * MLA_NOPE)
    rope = wq[:, :, MLA_NOPE:]
    pad = jnp.zeros((MLA_Q_RANK, MLA_HEADS, LANES - MLA_ROPE), w_uq.dtype)
    wqr = jnp.concatenate([rope, pad], axis=2).reshape(MLA_Q_RANK, MLA_HEADS * LANES)
    wqs = jnp.concatenate([rope[:, :, partner], pad], axis=2).reshape(MLA_Q_RANK, MLA_HEADS * LANES)
    wkv = w_ukv.reshape(MLA_KV_RANK, MLA_HEADS, MLA_NOPE + MLA_V)
    wknt = wkv[:, :, :MLA_NOPE].reshape(MLA_KV_RANK, MLA_HEADS * MLA_NOPE).T
    wv = wkv[:, :, MLA_NOPE:].reshape(MLA_KV_RANK, MLA_HEADS * MLA_V)
    perm = np.zeros((LANES, LANES), np.float32)
    perm[partner, np.arange(MLA_ROPE)] = 1.0
    eye = np.eye(MLA_ROPE, LANES, dtype=np.float32)
    return (wqn.astype(BF16), wqr.astype(BF16), wqs.astype(BF16), wknt.astype(BF16),
            wv.astype(BF16), jnp.asarray(perm, BF16), jnp.asarray(eye, BF16))


def _rope_tables(n_tok):
    rows = n_tok // GRID_W
    row = np.repeat(np.arange(rows, dtype=np.float32), GRID_W)
    col = np.tile(np.arange(GRID_W, dtype=np.float32), rows)
    nfreq = MLA_ROPE // 4
    inv = np.float32(ROPE_BASE) ** (-np.arange(nfreq, dtype=np.float32) / np.float32(nfreq))
    ar = (row[:, None] * inv).astype(np.float32)
    ac = (col[:, None] * inv).astype(np.float32)
    zero = np.zeros((n_tok, LANES - MLA_ROPE), np.float32)
    cos = np.concatenate([np.cos(ar), np.cos(ar), np.cos(ac), np.cos(ac), zero], axis=1)
    sin = np.concatenate([-np.sin(ar), np.sin(ar), -np.sin(ac), np.sin(ac), zero], axis=1)
    return jnp.asarray(cos, F32), jnp.asarray(sin, F32)


def _identity_tables(n_tok):
    cos = jnp.concatenate([jnp.ones((n_tok, MLA_ROPE), F32),
                           jnp.zeros((n_tok, LANES - MLA_ROPE), F32)], axis=1)
    return cos, jnp.zeros((n_tok, LANES), F32)


def _pick_tile(n, pref):
    t = min(n, pref)
    while n % t:
        t //= 2
    return t


def _pick_ff_tile(ff):
    best = LANES
    for m in range(1, ff // LANES + 1):
        if ff % (m * LANES) == 0 and m * LANES <= 1408:
            best = m * LANES
    return best


@jax.jit
def _forward(x, c, ctx, c_ctx, w_mod, b_mod, ln1_g, ln2_g, w_in, w_gla_g2, b_gla_g2, gla_norm_g,
             mla_q_norm_g, w_uq, mla_kv_norm_g, w_ukv, w_out, ffn_w_gate, ffn_w_up, ffn_w_down,
             router_w, exp_w_gate, exp_w_up, exp_w_down, final_norm_g):
    batch, seq, d = x.shape
    n_ctx = ctx.shape[1]
    depth = w_mod.shape[0]

    cvec = jnp.zeros((8, d), F32).at[:batch].set(c).at[batch].set(c_ctx)
    mods_all = _modulation(cvec, w_mod, b_mod).reshape(depth, 8, 6, d)

    xl = x.reshape(batch * seq, d)
    xc = ctx.reshape(batch * n_ctx, d)

    tm_l = _pick_tile(seq, 512)
    tm_c = _pick_tile(n_ctx, 256)
    tk_l = _pick_tile(seq, 1024)
    cb_l = _pick_tile(seq, 256)
    cb_c = _pick_tile(n_ctx, 256)
    row_l = lambda tm: (lambda i: i // (seq // tm))
    row_c = lambda i: batch

    rope_l = _rope_tables(seq)
    rope_c = _identity_tables(tm_c)
    zero_state = jnp.zeros((2, batch, GLA_QK_W, GLA_DV), F32)

    for i in range(depth):
        need_ctx = i < depth - 1
        last = i == depth - 1
        mods = mods_all[i]
        ln1 = ln1_g[i].reshape(1, d)
        ln2 = ln2_g[i].reshape(1, d)
        w_in_r = _prep_in_weight(w_in[i])
        gates = _prep_gate_weight(w_gla_g2[i], b_gla_g2[i])
        mla_w = _prep_mla_weights(w_uq[i], w_ukv[i])
        qg = mla_q_norm_g[i].reshape(1, MLA_Q_RANK)
        kvg = mla_kv_norm_g[i].reshape(1, MLA_KV_RANK)
        gg = gla_norm_g[i].reshape(1, GLA_DV)
        wo = w_out[i].astype(BF16)

        p_l = _inproj(xl, mods, row_l(tm_l), ln1, w_in_r, tm_l)
        p_c = _inproj(xc, mods, row_c, ln1, w_in_r, tm_c)

        oc_f, oc_b, s_ctx = _gla(p_c, *gates, zero_state, batch=batch, cb=cb_c)
        ol_f, ol_b, _ = _gla(p_l, *gates, s_ctx, batch=batch, cb=cb_l)

        q_l, kt_l, v_l = _mlaprep(p_l, *rope_l, qg, kvg, mla_w, batch=batch, tm=tk_l)
        q_c, kt_c, v_c = _mlaprep(p_c, *rope_c, qg, kvg, mla_w, batch=batch, tm=tm_c)
        m_l = _attention_pipelined(q_l, kt_l, v_l, kt_c, v_c, tq=tk_l, n_sub=max(1, tk_l // 512))
        m_l = m_l.reshape(batch * seq, MLA_V_W)

        if i % 2 == 0:
            j = i // 2
            router = None
            wg = ffn_w_gate[j].astype(BF16)
            wu = ffn_w_up[j].astype(BF16)
            wd = ffn_w_down[j].astype(BF16)
        else:
            j = i // 2
            rw = jnp.zeros((d, LANES), F32).at[:, :N_EXPERTS].set(router_w[j])
            rw_hi = rw.astype(BF16)
            router = (rw_hi, (rw - rw_hi.astype(F32)).astype(BF16))
            wg, wu, wd = exp_w_gate[j], exp_w_up[j], exp_w_down[j]
        fin = final_norm_g.reshape(1, d) if last else None

        outs = _outproj(xl, ol_f, ol_b, p_l, m_l, mods, row_l(tm_l), gg, wo, ln2, router, tm_l)
        tm_f = _pick_tile(seq, 512)
        if router is None:
            xl = _dense_ffn(outs[1], outs[0], mods, row_l(tm_f), wg, wu, wd, fin, tm_f)
        else:
            xl = _moe(outs[1], outs[0], outs[2], mods, row_l(tm_f), wg, wu, wd, fin, tm_f)

        if need_ctx:
            m_c = _attention(q_c, [(kt_c, v_c)], tq=tm_c, n_sub=1).reshape(batch * n_ctx, MLA_V_W)
            outs_c = _outproj(xc, oc_f, oc_b, p_c, m_c, mods, row_c, gg, wo, ln2, router, tm_c)
            tm_fc = _pick_tile(batch * n_ctx, 512)
            if router is None:
                xc = _dense_ffn(outs_c[1], outs_c[0], mods, row_c, wg, wu, wd, None, tm_fc)
            else:
                r_c = outs_c[2]
                lane = jnp.arange(LANES, dtype=F32)[None, :]
                comb_c = (jnp.where(lane == r_c[:, ROUTE_E1:ROUTE_E1 + 1], r_c[:, ROUTE_W1:ROUTE_W1 + 1], 0.0)
                          + jnp.where(lane == r_c[:, ROUTE_E2:ROUTE_E2 + 1], r_c[:, ROUTE_W2:ROUTE_W2 + 1], 0.0))
                xc = _ffn(_unpack_bf16_pairs(outs_c[1]).astype(BF16), outs_c[0], mods, row_c, comb_c,
                          wg.astype(BF16), wu.astype(BF16), wd.astype(BF16), None, tm_fc,
                          _pick_ff_tile(wg.shape[2]))

    return xl.reshape(batch, seq, d)


def kernel(x, c, ctx, c_ctx, w_mod, b_mod, ln1_g, ln2_g, w_in, w_gla_g2, b_gla_g2, gla_norm_g,
           mla_q_norm_g, w_uq, mla_kv_norm_g, w_ukv, w_out, ffn_w_gate, ffn_w_up, ffn_w_down,
           router_w, exp_w_gate, exp_w_up, exp_w_down, final_norm_g):
    return _forward(x, c, ctx, c_ctx, w_mod, b_mod, ln1_g, ln2_g, w_in, w_gla_g2, b_gla_g2,
                    gla_norm_g, mla_q_norm_g, w_uq, mla_kv_norm_g, w_ukv, w_out, ffn_w_gate,
                    ffn_w_up, ffn_w_down, router_w, exp_w_gate, exp_w_up, exp_w_down, final_norm_g)
```

```python
import functools

import numpy as np
import jax
import jax.numpy as jnp
from jax import lax
from jax.experimental import pallas as pl
from jax.experimental.pallas import tpu as pltpu
from jax.experimental.pallas import tpu_sc as plsc

F32 = jnp.float32
BF16 = jnp.bfloat16

D_MODEL = 1024
EPS = 1e-6
GRID_W = 64

GLA_HEADS = 4
GLA_DK = 64
GLA_DV = 128
GLA_GATE_RANK = 16
GLA_GATE_NORM = 16.0
GLA_CHUNK = 64
GLA_QK_W = GLA_HEADS * GLA_DK
GLA_V_W = GLA_HEADS * GLA_DV
GLA_EXP_CLAMP = 80.0

MLA_HEADS = 4
MLA_NOPE = 128
MLA_ROPE = 64
MLA_V = 128
MLA_QK = MLA_NOPE + MLA_ROPE
MLA_Q_RANK = 256
MLA_KV_RANK = 128
MLA_SCALE = MLA_QK ** -0.5
MLA_Q_SCALE = MLA_SCALE * 1.4426950408889634
MLA_V_W = MLA_HEADS * MLA_V
MLA_V_EXT = 2 * MLA_V
ROPE_BASE = 10000.0

N_EXPERTS = 8
LANES = 128
ROUTE_E1, ROUTE_E2, ROUTE_W1, ROUTE_W2 = 0, 1, 2, 3

SC_CORES = 2
SC_SUBCORES = 16
SC_GATHER_ROWS = 128
MOE_TILE = 1024
MOE_FF_TILE = 512
MOE_FF_SPLITS = ((0, 256), (256, 512))

P_Q, P_K, P_V, P_R, P_CQ, P_CKV, P_MISC = 0, 256, 512, 1024, 1536, 1792, 1920
P_WIDTH = 2048
MISC_KR, MISC_GF, MISC_GB = 0, 64, 80

VMEM_LIMIT = 56 * 1024 * 1024


def _cparams(sem):
    return pltpu.CompilerParams(dimension_semantics=sem, vmem_limit_bytes=VMEM_LIMIT)


def _rms(x, g):
    return x * lax.rsqrt(jnp.mean(x * x, axis=-1, keepdims=True) + EPS) * g


def _silu(x):
    return x / (1.0 + jnp.exp(-x))


def _dot(a, b):
    return jnp.dot(a, b, preferred_element_type=F32)


def _dot_nt(a, b):
    return lax.dot_general(a, b, (((1,), (1,)), ((), ())), preferred_element_type=F32)


def _dot_tn(a, b):
    return lax.dot_general(a, b, (((0,), (0,)), ((), ())), preferred_element_type=F32)


def _pack_bf16_pairs(x):
    w = x.shape[1] // 2
    words = pltpu.pack_elementwise([x[:, :w], x[:, w:]], packed_dtype=BF16)
    return lax.bitcast_convert_type(words, F32)


def _unpack_bf16_pairs(p):
    words = lax.bitcast_convert_type(p, jnp.int32)
    lo = pltpu.unpack_elementwise(words, index=0, packed_dtype=BF16, unpacked_dtype=F32)
    hi = pltpu.unpack_elementwise(words, index=1, packed_dtype=BF16, unpacked_dtype=F32)
    return jnp.concatenate([lo, hi], axis=1)


def _mod_kernel(c_ref, w_ref, b_ref, o_ref):
    s = _silu(c_ref[...]).astype(BF16)
    o_ref[0] = _dot(s, w_ref[0].astype(BF16)) + b_ref[0]


def _modulation(cvec, w_mod, b_mod):
    depth, d, n = w_mod.shape
    tn = 1536
    return pl.pallas_call(
        _mod_kernel,
        grid=(depth, n // tn),
        in_specs=[
            pl.BlockSpec((8, d), lambda l, j: (0, 0)),
            pl.BlockSpec((1, d, tn), lambda l, j: (l, 0, j)),
            pl.BlockSpec((1, 1, tn), lambda l, j: (l, 0, j)),
        ],
        out_specs=pl.BlockSpec((1, 8, tn), lambda l, j: (l, 0, j)),
        out_shape=jax.ShapeDtypeStruct((depth, 8, n), F32),
        compiler_params=_cparams(("arbitrary", "arbitrary")),
        name="modulation",
    )(cvec, w_mod, b_mod.reshape(depth, 1, n))


def _inproj_kernel(x_ref, mod_ref, g_ref, w_ref, o_ref):
    m = mod_ref[0]
    h = _rms(x_ref[...], g_ref[...]) * (1.0 + m[1:2]) + m[0:1]
    o_ref[...] = _dot(h.astype(BF16), w_ref[...]).astype(BF16)


def _inproj(x, mods, row_fn, ln_g, w, tm):
    t, d = x.shape
    return pl.pallas_call(
        _inproj_kernel,
        grid=(t // tm,),
        in_specs=[
            pl.BlockSpec((tm, d), lambda i: (i, 0)),
            pl.BlockSpec((1, 6, d), lambda i: (row_fn(i), 0, 0)),
            pl.BlockSpec((1, d), lambda i: (0, 0)),
            pl.BlockSpec((d, P_WIDTH), lambda i: (0, 0)),
        ],
        out_specs=pl.BlockSpec((tm, P_WIDTH), lambda i: (i, 0)),
        out_shape=jax.ShapeDtypeStruct((t, P_WIDTH), BF16),
        compiler_params=_cparams(("arbitrary",)),
        name="inproj",
    )(x, mods, ln_g, w)


def _gla_direction(q_ref, k_ref, v_ref, misc_ref, wg, bg, tri, s_scr, o_ref, *, reverse, n_chunks):
    c_len = GLA_CHUNK
    pre = _dot(misc_ref[...], wg) + bg
    g = (jnp.minimum(pre, 0.0) - jnp.log(1.0 + jnp.exp(-jnp.abs(pre)))) * (1.0 / GLA_GATE_NORM)
    g_hi = g.astype(BF16)
    g_lo = (g - g_hi.astype(F32)).astype(BF16)
    cum = _dot(tri, g_hi) + _dot(tri, g_lo)
    tot_rows = jnp.concatenate(
        [cum[c * c_len:c * c_len + 1] if reverse else cum[(c + 1) * c_len - 1:(c + 1) * c_len]
         for c in range(n_chunks)] + [jnp.zeros((8 - n_chunks, GLA_QK_W), F32)], axis=0)
    t_hi = tot_rows.astype(BF16)
    t_lo = (tot_rows - t_hi.astype(F32)).astype(BF16)
    eye = (lax.broadcasted_iota(jnp.int32, (GLA_QK_W, GLA_QK_W), 0)
           == lax.broadcasted_iota(jnp.int32, (GLA_QK_W, GLA_QK_W), 1))
    eye = jnp.where(eye, 1.0, 0.0).astype(BF16)
    tot_cols = _dot_nt(eye, t_hi) + _dot_nt(eye, t_lo)

    lane = lax.broadcasted_iota(jnp.int32, (c_len, GLA_QK_W), 1)
    head_masks = [(lane >= h * GLA_DK) & (lane < (h + 1) * GLA_DK) for h in range(GLA_HEADS)]
    row = lax.broadcasted_iota(jnp.int32, (GLA_HEADS * c_len, c_len), 0) % c_len
    col = lax.broadcasted_iota(jnp.int32, (GLA_HEADS * c_len, c_len), 1)
    pair_mask = (col >= row) if reverse else (col <= row)

    def stack_heads(a):
        return jnp.concatenate([jnp.where(mk, a, 0.0) for mk in head_masks], axis=0).astype(BF16)

    def step(c):
        sl = slice(c * c_len, (c + 1) * c_len)
        xc = cum[sl]
        tot = tot_rows[c:c + 1]
        ref = xc[c_len // 2:c_len // 2 + 1]
        qc = q_ref[sl, :].astype(F32) * (GLA_DK ** -0.5)
        kc = k_ref[sl, :].astype(F32)
        vc = v_ref[sl, :]
        q_mid = qc * jnp.exp(jnp.minimum(xc - ref, GLA_EXP_CLAMP))
        k_mid = (kc * jnp.exp(jnp.minimum(ref - xc, GLA_EXP_CLAMP))).astype(BF16)
        q_dec = qc * jnp.exp(xc)
        k_dec = kc * jnp.exp(tot - xc)

        attn = _dot_nt(stack_heads(q_mid), k_mid)
        attn = jnp.where(pair_mask, attn, 0.0).astype(BF16)
        s_prev = s_scr[...]
        o_inter = _dot(stack_heads(q_dec), s_prev.astype(BF16))
        kv = []
        for h in range(GLA_HEADS):
            rs = slice(h * c_len, (h + 1) * c_len)
            vs = slice(h * GLA_DV, (h + 1) * GLA_DV)
            o_h = o_inter[rs] + _dot(attn[rs], vc[:, vs])
            o_ref[sl, vs] = o_h.astype(BF16)
            kv.append(_dot_tn(k_dec[:, h * GLA_DK:(h + 1) * GLA_DK].astype(BF16), vc[:, vs]))
        s_scr[...] = s_prev * jnp.exp(tot_cols[:, c:c + 1]) + jnp.concatenate(kv, axis=0)

    return step


def _gla_kernel(qf_ref, kf_ref, vf_ref, mf_ref, qb_ref, kb_ref, vb_ref, mb_ref,
                wg_ref, bg_ref, tri_ref, s0_ref, of_ref, ob_ref, sfin_ref, sf_scr, sb_scr, *, n_chunks):
    blk = pl.program_id(1)

    @pl.when(blk == 0)
    def _():
        sf_scr[...] = s0_ref[0, 0]
        sb_scr[...] = s0_ref[1, 0]

    fwd = _gla_direction(qf_ref, kf_ref, vf_ref, mf_ref, wg_ref[0], bg_ref[0], tri_ref[0],
                         sf_scr, of_ref, reverse=False, n_chunks=n_chunks)
    bwd = _gla_direction(qb_ref, kb_ref, vb_ref, mb_ref, wg_ref[1], bg_ref[1], tri_ref[1],
                         sb_scr, ob_ref, reverse=True, n_chunks=n_chunks)
    for c in range(n_chunks):
        fwd(c)
        bwd(n_chunks - 1 - c)

    @pl.when(blk == pl.num_programs(1) - 1)
    def _():
        sfin_ref[0, 0] = sf_scr[...]
        sfin_ref[1, 0] = sb_scr[...]


def _block_diag_tri(n_chunks):
    c = GLA_CHUNK
    eye = np.eye(n_chunks, dtype=np.float32)
    lower = np.kron(eye, np.tril(np.ones((c, c), np.float32)))
    upper = np.kron(eye, np.triu(np.ones((c, c), np.float32)))
    return jnp.asarray(np.stack([lower, upper]), dtype=BF16)


def _gla(p, wg, bg, s0, *, batch, cb):
    t_all = p.shape[0]
    nblk = t_all // batch // cb
    n_chunks = cb // GLA_CHUNK
    assert n_chunks <= 8

    fw = lambda b, i: b * nblk + i
    bw = lambda b, i: b * nblk + (nblk - 1 - i)
    full = lambda a: pl.BlockSpec(a.shape, lambda b, i: (0,) * a.ndim)
    tri = _block_diag_tri(n_chunks)

    def token_specs(tok):
        return [
            pl.BlockSpec((cb, GLA_QK_W), lambda b, i: (tok(b, i), P_Q // GLA_QK_W)),
            pl.BlockSpec((cb, GLA_QK_W), lambda b, i: (tok(b, i), P_K // GLA_QK_W)),
            pl.BlockSpec((cb, GLA_V_W), lambda b, i: (tok(b, i), P_V // GLA_V_W)),
            pl.BlockSpec((cb, LANES), lambda b, i: (tok(b, i), P_MISC // LANES)),
        ]

    state_spec = pl.BlockSpec((2, 1, GLA_QK_W, GLA_DV), lambda b, i: (0, b, 0, 0))
    return pl.pallas_call(
        functools.partial(_gla_kernel, n_chunks=n_chunks),
        grid=(batch, nblk),
        in_specs=token_specs(fw) + token_specs(bw) + [full(wg), full(bg), full(tri), state_spec],
        out_specs=[
            pl.BlockSpec((cb, GLA_V_W), lambda b, i: (fw(b, i), 0)),
            pl.BlockSpec((cb, GLA_V_W), lambda b, i: (bw(b, i), 0)),
            state_spec,
        ],
        out_shape=[
            jax.ShapeDtypeStruct((t_all, GLA_V_W), BF16),
            jax.ShapeDtypeStruct((t_all, GLA_V_W), BF16),
            jax.ShapeDtypeStruct((2, batch, GLA_QK_W, GLA_DV), F32),
        ],
        scratch_shapes=[pltpu.VMEM((GLA_QK_W, GLA_DV), F32), pltpu.VMEM((GLA_QK_W, GLA_DV), F32)],
        compiler_params=_cparams(("arbitrary", "arbitrary")),
        name="gla",
    )(p, p, p, p, p, p, p, p, wg, bg, tri, s0)


def _mlaprep_kernel(cq_ref, ckv_ref, misc_ref, cos_ref, sin_ref, qg_ref, kvg_ref,
                    wqn_ref, wqr_ref, wqs_ref, wknt_ref, wv_ref, perm_ref, eye_ref,
                    q_ref, kt_ref, v_ref):
    cos = cos_ref[...]
    sin = sin_ref[...]
    cqn = _rms(cq_ref[...].astype(F32), qg_ref[...]).astype(BF16)
    qn = _dot(cqn, wqn_ref[...])
    qr = _dot(cqn, wqr_ref[...])
    qs = _dot(cqn, wqs_ref[...])
    for h in range(MLA_HEADS):
        ls = slice(h * LANES, (h + 1) * LANES)
        q_ref[0, h, :, 0:MLA_NOPE] = (qn[:, ls] * MLA_Q_SCALE).astype(BF16)
        rot = qr[:, ls] * cos + qs[:, ls] * sin
        q_ref[0, h, :, MLA_NOPE:MLA_QK] = (rot[:, 0:MLA_ROPE] * MLA_Q_SCALE).astype(BF16)

    ckvn = _rms(ckv_ref[...].astype(F32), kvg_ref[...]).astype(BF16)
    knt = _dot_nt(wknt_ref[...], ckvn)
    vv = _dot(ckvn, wv_ref[...])
    misc = misc_ref[...]
    kr = misc.astype(F32) * cos + _dot(misc, perm_ref[...]) * sin
    krt = _dot_nt(eye_ref[...], kr.astype(BF16)).astype(BF16)
    for h in range(MLA_HEADS):
        kt_ref[0, h, 0, 0:MLA_NOPE, :] = knt[h * MLA_NOPE:(h + 1) * MLA_NOPE].astype(BF16)
        kt_ref[0, h, 0, MLA_NOPE:MLA_QK, :] = krt
        v_ref[0, h, :, 0:MLA_V] = vv[:, h * MLA_V:(h + 1) * MLA_V].astype(BF16)
        v_ref[0, h, :, MLA_V:MLA_V_EXT] = jnp.ones((vv.shape[0], MLA_V), BF16)


def _mlaprep(p, cos, sin, qg, kvg, wts, *, batch, tm):
    t_all = p.shape[0]
    t = t_all // batch
    nb = t // tm
    ntab = cos.shape[0] // tm
    wqn, wqr, wqs, wknt, wv, perm, eye = wts
    full = lambda a: pl.BlockSpec(a.shape, lambda b, i: (0,) * a.ndim)
    return pl.pallas_call(
        _mlaprep_kernel,
        grid=(batch, nb),
        in_specs=[
            pl.BlockSpec((tm, MLA_Q_RANK), lambda b, i: (b * nb + i, P_CQ // MLA_Q_RANK)),
            pl.BlockSpec((tm, MLA_KV_RANK), lambda b, i: (b * nb + i, P_CKV // MLA_KV_RANK)),
            pl.BlockSpec((tm, LANES), lambda b, i: (b * nb + i, P_MISC // LANES)),
            pl.BlockSpec((tm, LANES), lambda b, i: (i % ntab, 0)),
            pl.BlockSpec((tm, LANES), lambda b, i: (i % ntab, 0)),
            full(qg), full(kvg), full(wqn), full(wqr), full(wqs), full(wknt), full(wv),
            full(perm), full(eye),
        ],
        out_specs=[
            pl.BlockSpec((1, MLA_HEADS, tm, MLA_QK), lambda b, i: (b, 0, i, 0)),
            pl.BlockSpec((1, MLA_HEADS, 1, MLA_QK, tm), lambda b, i: (b, 0, i, 0, 0)),
            pl.BlockSpec((1, MLA_HEADS, tm, MLA_V_EXT), lambda b, i: (b, 0, i, 0)),
        ],
        out_shape=[
            jax.ShapeDtypeStruct((batch, MLA_HEADS, t, MLA_QK), BF16),
            jax.ShapeDtypeStruct((batch, MLA_HEADS, nb, MLA_QK, tm), BF16),
            jax.ShapeDtypeStruct((batch, MLA_HEADS, t, MLA_V_EXT), BF16),
        ],
        compiler_params=_cparams(("arbitrary", "arbitrary")),
        name="mlaprep",
    )(p, p, p, cos, sin, qg, kvg, wqn, wqr, wqs, wknt, wv, perm, eye)


def _attn_kernel(*refs, n_seg, n_sub):
    q_ref = refs[0]
    kt_refs = refs[1:1 + 2 * n_seg:2]
    v_refs = refs[2:2 + 2 * n_seg:2]
    o_ref = refs[1 + 2 * n_seg]
    m_scr, acc_scr = refs[2 + 2 * n_seg:]

    rows_per_sub = q_ref.shape[2] // n_sub
    m_scr[...] = jnp.full(m_scr.shape, -jnp.inf, F32)
    acc_scr[...] = jnp.zeros(acc_scr.shape, F32)

    for kt_ref, v_ref in zip(kt_refs, v_refs):
        n_blocks, tk = kt_ref.shape[2], kt_ref.shape[4]

        def step(j, carry, kt_ref=kt_ref, v_ref=v_ref, tk=tk):
            kt = kt_ref[0, 0, j]
            v_blk = v_ref[0, 0, pl.ds(pl.multiple_of(j * tk, tk), tk), :]
            for u in range(n_sub):
                rows = slice(u * rows_per_sub, (u + 1) * rows_per_sub)
                s = _dot(q_ref[0, 0, rows, :], kt)
                m_prev = m_scr[rows, :]
                m_next = jnp.maximum(m_prev, jnp.max(s, axis=1, keepdims=True))
                p = jnp.exp2((s - jnp.concatenate([m_next] * (tk // LANES), axis=1)).astype(BF16))
                alpha = jnp.exp2(m_prev - m_next)
                acc_scr[rows, :] = (jnp.concatenate([alpha] * (MLA_V_EXT // LANES), axis=1)
                                    * acc_scr[rows, :] + _dot(p, v_blk))
                m_scr[rows, :] = m_next
            return carry

        lax.fori_loop(0, n_blocks, step, 0)

    o_ref[0] = (acc_scr[:, 0:MLA_V] / acc_scr[:, MLA_V:MLA_V_EXT]).astype(BF16)


def _attn_pipe_kernel(q_ref, kt_ref, v_ref, ktt_ref, vt_ref, o_ref,
                      m_scr, acc_scr, s0_scr, s1_scr, st_scr, *, n_sub):
    n_blocks, tk = kt_ref.shape[2], kt_ref.shape[4]
    rows_per_sub = q_ref.shape[2] // n_sub
    subs = [slice(u * rows_per_sub, (u + 1) * rows_per_sub) for u in range(n_sub)]
    m_scr[...] = jnp.full(m_scr.shape, -jnp.inf, F32)
    acc_scr[...] = jnp.zeros(acc_scr.shape, F32)

    def scores(kt, s_ref):
        for rows in subs:
            s_ref[rows, :] = _dot(q_ref[0, 0, rows, :], kt)

    def softmax_pv(s_ref, v_blk):
        width = s_ref.shape[1]
        for rows in subs:
            s = s_ref[rows, :]
            m_prev = m_scr[rows, :]
            m_next = jnp.maximum(m_prev, jnp.max(s, axis=1, keepdims=True))
            p = jnp.exp2((s - jnp.concatenate([m_next] * (width // LANES), axis=1)).astype(BF16))
            alpha = jnp.exp2(m_prev - m_next)
            acc_scr[rows, :] = (jnp.concatenate([alpha] * (MLA_V_EXT // LANES), axis=1)
                                * acc_scr[rows, :] + _dot(p, v_blk))
            m_scr[rows, :] = m_next

    def v_main(j):
        return v_ref[0, 0, pl.ds(pl.multiple_of(j * tk, tk), tk), :]

    scores(kt_ref[0, 0, 0], s0_scr)
    n_pairs = (n_blocks - 1) // 2

    def pair(jj, carry):
        j = 2 * jj
        scores(kt_ref[0, 0, j + 1], s1_scr)
        softmax_pv(s0_scr, v_main(j))
        scores(kt_ref[0, 0, j + 2], s0_scr)
        softmax_pv(s1_scr, v_main(j + 1))
        return carry

    lax.fori_loop(0, n_pairs, pair, 0)
    if n_blocks - 2 * n_pairs == 2:
        scores(kt_ref[0, 0, n_blocks - 1], s1_scr)
        softmax_pv(s0_scr, v_main(n_blocks - 2))
        scores(ktt_ref[0, 0, 0], st_scr)
        softmax_pv(s1_scr, v_main(n_blocks - 1))
    else:
        scores(ktt_ref[0, 0, 0], st_scr)
        softmax_pv(s0_scr, v_main(n_blocks - 1))
    softmax_pv(st_scr, vt_ref[0, 0])

    o_ref[0] = (acc_scr[:, 0:MLA_V] / acc_scr[:, MLA_V:MLA_V_EXT]).astype(BF16)


def _attention_pipelined(q, kt, v, kt_tail, v_tail, *, tq, n_sub):
    b, h, t, dqk = q.shape
    tk, tt = kt.shape[4], kt_tail.shape[4]
    assert kt_tail.shape[2] == 1
    return pl.pallas_call(
        functools.partial(_attn_pipe_kernel, n_sub=n_sub),
        grid=(b, h, t // tq),
        in_specs=[
            pl.BlockSpec((1, 1, tq, dqk), lambda bi, hi, qi: (bi, hi, qi, 0)),
            pl.BlockSpec((1, 1) + kt.shape[2:], lambda bi, hi, qi: (bi, hi, 0, 0, 0)),
            pl.BlockSpec((1, 1) + v.shape[2:], lambda bi, hi, qi: (bi, hi, 0, 0)),
            pl.BlockSpec((1, 1) + kt_tail.shape[2:], lambda bi, hi, qi: (bi, hi, 0, 0, 0)),
            pl.BlockSpec((1, 1) + v_tail.shape[2:], lambda bi, hi, qi: (bi, hi, 0, 0)),
        ],
        out_specs=pl.BlockSpec((1, tq, MLA_V), lambda bi, hi, qi: (bi, qi, hi)),
        out_shape=jax.ShapeDtypeStruct((b, t, h * MLA_V), BF16),
        scratch_shapes=[pltpu.VMEM((tq, LANES), F32), pltpu.VMEM((tq, MLA_V_EXT), F32),
                        pltpu.VMEM((tq, tk), F32), pltpu.VMEM((tq, tk), F32),
                        pltpu.VMEM((tq, tt), F32)],
        compiler_params=_cparams(("arbitrary", "arbitrary", "arbitrary")),
        name="mla_attention_pipe",
    )(q, kt, v, kt_tail, v_tail)


def _attention(q, segs, *, tq, n_sub):
    b, h, t, dqk = q.shape
    in_specs = [pl.BlockSpec((1, 1, tq, dqk), lambda bi, hi, qi: (bi, hi, qi, 0))]
    args = [q]
    for kt, v in segs:
        in_specs.append(pl.BlockSpec((1, 1) + kt.shape[2:], lambda bi, hi, qi: (bi, hi, 0, 0, 0)))
        in_specs.append(pl.BlockSpec((1, 1) + v.shape[2:], lambda bi, hi, qi: (bi, hi, 0, 0)))
        args += [kt, v]
    return pl.pallas_call(
        functools.partial(_attn_kernel, n_seg=len(segs), n_sub=n_sub),
        grid=(b, h, t // tq),
        in_specs=in_specs,
        out_specs=pl.BlockSpec((1, tq, MLA_V), lambda bi, hi, qi: (bi, qi, hi)),
        out_shape=jax.ShapeDtypeStruct((b, t, h * MLA_V), BF16),
        scratch_shapes=[pltpu.VMEM((tq, LANES), F32), pltpu.VMEM((tq, MLA_V_EXT), F32)],
        compiler_params=_cparams(("arbitrary", "arbitrary", "arbitrary")),
        name="mla_attention",
    )(*args)


def _mix_residual_norm(x_ref, of_ref, ob_ref, r_ref, mla_ref, m, gg_ref, wo_ref, ln2_ref):
    o = of_ref[...].astype(F32) + ob_ref[...].astype(F32)
    gg = gg_ref[...]
    y = jnp.concatenate(
        [_rms(o[:, h * GLA_DV:(h + 1) * GLA_DV], gg) for h in range(GLA_HEADS)], axis=1)
    mix = (y * _silu(r_ref[...].astype(F32))).astype(BF16)
    yo = _dot(mix, wo_ref[0:GLA_V_W, :]) + _dot(mla_ref[...], wo_ref[GLA_V_W:, :])
    x1 = x_ref[...] + m[2:3] * yo
    h2 = _rms(x1, ln2_ref[...]) * (1.0 + m[4:5]) + m[3:4]
    return x1, h2


def _outproj_ffn_kernel(*refs, groups, final_norm):
    (x_ref, of_ref, ob_ref, r_ref, mla_ref, mod_ref, gg_ref, wo_ref, ln2_ref,
     wg_ref, wu_ref, wd_ref) = refs[:12]
    fin_ref = refs[12] if final_norm else None
    o_ref = refs[-1]
    m = mod_ref[0]
    x1, h2 = _mix_residual_norm(x_ref, of_ref, ob_ref, r_ref, mla_ref, m, gg_ref, wo_ref, ln2_ref)
    h = h2.astype(BF16)
    y = None
    for lo, hi in groups:
        a = _dot(h, wg_ref[:, lo:hi])
        u = _dot(h, wu_ref[:, lo:hi])
        part = _dot((_silu(a) * u).astype(BF16), wd_ref[lo:hi, :])
        y = part if y is None else y + part
    x2 = x1 + m[5:6] * y
    if final_norm:
        x2 = _rms(x2, fin_ref[...])
    o_ref[...] = x2


def _outproj_ffn(x, o_f, o_b, p, mla, mods, row_fn, gg, wo, ln2, wg, wu, wd, fin_g, tm):
    t, d = x.shape
    ff = wg.shape[1]
    final_norm = fin_g is not None
    step = 1024 if ff > 1024 else ff
    groups = tuple((lo, min(lo + step, ff)) for lo in range(0, ff, step))
    resident = lambda a: pl.BlockSpec(a.shape, lambda i: (0,) * a.ndim, pipeline_mode=pl.Buffered(1))
    in_specs = [
        pl.BlockSpec((tm, d), lambda i: (i, 0)),
        pl.BlockSpec((tm, GLA_V_W), lambda i: (i, 0)),
        pl.BlockSpec((tm, GLA_V_W), lambda i: (i, 0)),
        pl.BlockSpec((tm, GLA_V_W), lambda i: (i, P_R // GLA_V_W)),
        pl.BlockSpec((tm, MLA_V_W), lambda i: (i, 0)),
        pl.BlockSpec((1, 6, d), lambda i: (row_fn(i), 0, 0)),
        resident(gg), resident(wo), resident(ln2), resident(wg), resident(wu), resident(wd),
    ]
    args = [x, o_f, o_b, p, mla, mods, gg, wo, ln2, wg, wu, wd]
    if final_norm:
        in_specs.append(resident(fin_g))
        args.append(fin_g)
    return pl.pallas_call(
        functools.partial(_outproj_ffn_kernel, groups=groups, final_norm=final_norm),
        grid=(t // tm,),
        in_specs=in_specs,
        out_specs=pl.BlockSpec((tm, d), lambda i: (i, 0)),
        out_shape=jax.ShapeDtypeStruct((t, d), F32),
        compiler_params=_cparams(("arbitrary",)),
        name="outproj_ffn",
    )(*args)


def _outproj_kernel(*refs, with_router):
    (x_ref, of_ref, ob_ref, r_ref, mla_ref, mod_ref, gg_ref, wo_ref, ln2_ref) = refs[:9]
    if with_router:
        rwh_ref, rwl_ref, x1_ref, h2_ref, comb_ref = refs[9:]
    else:
        x1_ref, h2_ref = refs[9:]
    m = mod_ref[0]
    x1, h2 = _mix_residual_norm(x_ref, of_ref, ob_ref, r_ref, mla_ref, m, gg_ref, wo_ref, ln2_ref)
    x1_ref[...] = x1
    if with_router:
        h2_ref[...] = _pack_bf16_pairs(h2)
    else:
        h2_ref[...] = h2.astype(BF16)
    if with_router:
        h_hi = h2.astype(BF16)
        h_lo = (h2 - h_hi.astype(F32)).astype(BF16)
        logits = _dot(h_hi, rwh_ref[...]) + _dot(h_lo, rwh_ref[...]) + _dot(h_hi, rwl_ref[...])
        lane = lax.broadcasted_iota(jnp.int32, logits.shape, 1).astype(F32)
        neg = jnp.float32(-jnp.inf)
        logits = jnp.where(lane < N_EXPERTS, logits, neg)
        m1 = jnp.max(logits, axis=1, keepdims=True)
        i1 = jnp.min(jnp.where(logits == m1, lane, float(LANES)), axis=1, keepdims=True)
        rest = jnp.where(lane == i1, neg, logits)
        m2 = jnp.max(rest, axis=1, keepdims=True)
        i2 = jnp.min(jnp.where(rest == m2, lane, float(LANES)), axis=1, keepdims=True)
        e2 = jnp.exp(m2 - m1)
        w1 = 1.0 / (1.0 + e2)
        comb_ref[...] = (jnp.where(lane == ROUTE_E1, i1, 0.0) + jnp.where(lane == ROUTE_E2, i2, 0.0)
                         + jnp.where(lane == ROUTE_W1, w1, 0.0)
                         + jnp.where(lane == ROUTE_W2, e2 * w1, 0.0))


def _outproj(x, o_f, o_b, p, mla, mods, row_fn, gg, wo, ln2, router, tm):
    t, d = x.shape
    with_router = router is not None
    full = lambda a: pl.BlockSpec(a.shape, lambda i: (0,) * a.ndim)
    in_specs = [
        pl.BlockSpec((tm, d), lambda i: (i, 0)),
        pl.BlockSpec((tm, GLA_V_W), lambda i: (i, 0)),
        pl.BlockSpec((tm, GLA_V_W), lambda i: (i, 0)),
        pl.BlockSpec((tm, GLA_V_W), lambda i: (i, P_R // GLA_V_W)),
        pl.BlockSpec((tm, MLA_V_W), lambda i: (i, 0)),
        pl.BlockSpec((1, 6, d), lambda i: (row_fn(i), 0, 0)),
        full(gg), full(wo), full(ln2),
    ]
    args = [x, o_f, o_b, p, mla, mods, gg, wo, ln2]
    h2_shape = jax.ShapeDtypeStruct((t, d // 2), F32) if with_router else jax.ShapeDtypeStruct((t, d), BF16)
    out_specs = [pl.BlockSpec((tm, d), lambda i: (i, 0)),
                 pl.BlockSpec((tm, h2_shape.shape[1]), lambda i: (i, 0))]
    out_shape = [jax.ShapeDtypeStruct((t, d), F32), h2_shape]
    if with_router:
        in_specs += [full(router[0]), full(router[1])]
        args += list(router)
        out_specs.append(pl.BlockSpec((tm, LANES), lambda i: (i, 0)))
        out_shape.append(jax.ShapeDtypeStruct((t, LANES), F32))
    return pl.pallas_call(
        functools.partial(_outproj_kernel, with_router=with_router),
        grid=(t // tm,),
        in_specs=in_specs,
        out_specs=out_specs,
        out_shape=out_shape,
        compiler_params=_cparams(("arbitrary",)),
        name="outproj",
    )(*args)


def _ffn_kernel(*refs, with_comb, final_norm):
    h_ref, x1_ref, mod_ref = refs[:3]
    k = 3
    comb_ref = fin_ref = None
    if with_comb:
        comb_ref = refs[k]
        k += 1
    wg_ref, wu_ref, wd_ref = refs[k:k + 3]
    k += 3
    if final_norm:
        fin_ref = refs[k]
        k += 1
    o_ref, acc = refs[k:]
    e = pl.program_id(1)
    f = pl.program_id(2)

    @pl.when((e == 0) & (f == 0))
    def _():
        acc[...] = jnp.zeros(acc.shape, F32)

    h = h_ref[...]
    a = _dot(h, wg_ref[0])
    u = _dot(h, wu_ref[0])
    act = _silu(a) * u
    if with_comb:
        comb = comb_ref[...]
        lane = lax.broadcasted_iota(jnp.int32, comb.shape, 1)
        act = act * jnp.sum(jnp.where(lane == e, comb, 0.0), axis=1, keepdims=True)
    acc[...] += _dot(act.astype(BF16), wd_ref[0])

    @pl.when((e == pl.num_programs(1) - 1) & (f == pl.num_programs(2) - 1))
    def _():
        x2 = x1_ref[...] + mod_ref[0][5:6] * acc[...]
        if final_norm:
            x2 = _rms(x2, fin_ref[...])
        o_ref[...] = x2


def _ffn(h2, x1, mods, row_fn, comb, wg, wu, wd, fin_g, tm, tf):
    t, d = x1.shape
    n_e, _, ff = wg.shape
    with_comb = comb is not None
    final_norm = fin_g is not None
    in_specs = [
        pl.BlockSpec((tm, d), lambda i, e, f: (i, 0)),
        pl.BlockSpec((tm, d), lambda i, e, f: (i, 0)),
        pl.BlockSpec((1, 6, d), lambda i, e, f: (row_fn(i), 0, 0)),
    ]
    args = [h2, x1, mods]
    if with_comb:
        in_specs.append(pl.BlockSpec((tm, LANES), lambda i, e, f: (i, 0)))
        args.append(comb)
    in_specs += [
        pl.BlockSpec((1, d, tf), lambda i, e, f: (e, 0, f)),
        pl.BlockSpec((1, d, tf), lambda i, e, f: (e, 0, f)),
        pl.BlockSpec((1, tf, d), lambda i, e, f: (e, f, 0)),
    ]
    args += [wg, wu, wd]
    if final_norm:
        in_specs.append(pl.BlockSpec((1, d), lambda i, e, f: (0, 0)))
        args.append(fin_g)
    return pl.pallas_call(
        functools.partial(_ffn_kernel, with_comb=with_comb, final_norm=final_norm),
        grid=(t // tm, n_e, ff // tf),
        in_specs=in_specs,
        out_specs=pl.BlockSpec((tm, d), lambda i, e, f: (i, 0)),
        out_shape=jax.ShapeDtypeStruct((t, d), F32),
        scratch_shapes=[pltpu.VMEM((tm, d), F32)],
        compiler_params=_cparams(("arbitrary", "arbitrary", "arbitrary")),
        name="ffn",
    )(*args)


def _sc_row_gather(table, idx):
    _, w = table.shape
    b = idx.shape[0]
    n_workers = SC_CORES * SC_SUBCORES
    assert b % (n_workers * SC_GATHER_ROWS) == 0, (b, n_workers, SC_GATHER_ROWS)
    b_per_w = b // n_workers
    n_chunks = b_per_w // SC_GATHER_ROWS
    mesh = plsc.VectorSubcoreMesh(core_axis_name="c", subcore_axis_name="s",
                                  num_cores=SC_CORES, num_subcores=SC_SUBCORES)

    def body(table_hbm, idx_hbm, out_hbm, idx_v, rows_v, sem):
        wid = lax.axis_index("s") * SC_CORES + lax.axis_index("c")
        base = wid * b_per_w

        @pl.loop(0, n_chunks)
        def _(ci):
            off = base + ci * SC_GATHER_ROWS
            pltpu.sync_copy(idx_hbm.at[pl.ds(off, SC_GATHER_ROWS)], idx_v)
            pltpu.async_copy(table_hbm.at[idx_v], rows_v, sem).wait()
            pltpu.sync_copy(rows_v, out_hbm.at[pl.ds(off, SC_GATHER_ROWS)])

    return pl.kernel(
        body,
        out_type=jax.ShapeDtypeStruct((b, w), F32),
        mesh=mesh,
        scratch_types=[pltpu.VMEM((SC_GATHER_ROWS,), jnp.int32),
                       pltpu.VMEM((SC_GATHER_ROWS, w), F32),
                       pltpu.SemaphoreType.DMA],
        name="sc_row_gather",
    )(table, idx)


def _sc_row_scatter2(table, pos, n_out):
    t, w = table.shape
    n_workers = SC_CORES * SC_SUBCORES
    assert t % (n_workers * SC_GATHER_ROWS) == 0, (t, n_workers, SC_GATHER_ROWS)
    t_per_w = t // n_workers
    n_chunks = t_per_w // SC_GATHER_ROWS
    mesh = plsc.VectorSubcoreMesh(core_axis_name="c", subcore_axis_name="s",
                                  num_cores=SC_CORES, num_subcores=SC_SUBCORES)

    def body(table_hbm, pos_hbm, out_hbm, idx0_v, idx1_v, rows_v, sem):
        wid = lax.axis_index("s") * SC_CORES + lax.axis_index("c")
        base = wid * t_per_w

        @pl.loop(0, n_chunks)
        def _(ci):
            off = base + ci * SC_GATHER_ROWS
            pltpu.sync_copy(pos_hbm.at[0, pl.ds(off, SC_GATHER_ROWS)], idx0_v)
            pltpu.sync_copy(pos_hbm.at[1, pl.ds(off, SC_GATHER_ROWS)], idx1_v)
            pltpu.sync_copy(table_hbm.at[pl.ds(off, SC_GATHER_ROWS)], rows_v)
            first = pltpu.async_copy(rows_v, out_hbm.at[idx0_v], sem)
            second = pltpu.async_copy(rows_v, out_hbm.at[idx1_v], sem)
            first.wait()
            second.wait()

    return pl.kernel(
        body,
        out_type=jax.ShapeDtypeStruct((n_out, w), F32),
        mesh=mesh,
        scratch_types=[pltpu.VMEM((SC_GATHER_ROWS,), jnp.int32),
                       pltpu.VMEM((SC_GATHER_ROWS,), jnp.int32),
                       pltpu.VMEM((SC_GATHER_ROWS, w), F32),
                       pltpu.SemaphoreType.DMA],
        name="sc_row_scatter",
    )(table, pos)


def _moe_plan_kernel(route_ref, tri_ref, utri_ref, pos_ref, cnt_ref, run_scr, off_scr):
    phase = pl.program_id(0)
    blk = pl.program_id(1)
    route = route_ref[...]
    lane = lax.broadcasted_iota(jnp.int32, route.shape, 1).astype(F32)
    oh1 = jnp.where(lane == route[:, ROUTE_E1:ROUTE_E1 + 1], 1.0, 0.0)
    oh2 = jnp.where(lane == route[:, ROUTE_E2:ROUTE_E2 + 1], 1.0, 0.0)
    oh = oh1 + oh2

    @pl.when(blk == 0)
    def _():
        run_scr[...] = jnp.zeros(run_scr.shape, F32)

    @pl.when(phase == 0)
    def _():
        run_scr[...] += jnp.sum(oh, axis=0, keepdims=True)

        @pl.when(blk == pl.num_programs(1) - 1)
        def _():
            counts = run_scr[...]
            cnt_ref[...] = counts
            tiles_per = jnp.floor((counts + (MOE_TILE - 1.0)) * (1.0 / MOE_TILE))
            tile_end = _dot(jnp.broadcast_to(tiles_per, (8, LANES)).astype(BF16), utri_ref[...])[0:1]
            off_scr[...] = (tile_end - tiles_per) * float(MOE_TILE)

    @pl.when(phase == 1)
    def _():
        incl = _dot(tri_ref[...], oh.astype(BF16))
        before = incl - oh + run_scr[...] + off_scr[...]
        p1 = jnp.sum(before * oh1, axis=1, keepdims=True)
        p2 = jnp.sum(before * oh2, axis=1, keepdims=True)
        pos = jnp.where(lane == 0.0, p1, 0.0) + jnp.where(lane == 1.0, p2, 0.0)
        pos_ref[...] = pos.astype(jnp.int32)
        run_scr[...] += incl[incl.shape[0] - 1:, :]


def _moe_plan(route, n_tiles, tm):
    t = route.shape[0]
    tri = jnp.asarray(np.tril(np.ones((tm, tm), np.float32)), dtype=BF16)
    utri = jnp.asarray(np.triu(np.ones((LANES, LANES), np.float32)), dtype=BF16)
    pos, counts = pl.pallas_call(
        _moe_plan_kernel,
        grid=(2, t // tm),
        in_specs=[
            pl.BlockSpec((tm, LANES), lambda p, i: (i, 0)),
            pl.BlockSpec((tm, tm), lambda p, i: (0, 0)),
            pl.BlockSpec((LANES, LANES), lambda p, i: (0, 0)),
        ],
        out_specs=[
            pl.BlockSpec((tm, LANES), lambda p, i: (i * p, 0)),
            pl.BlockSpec((1, LANES), lambda p, i: (0, 0)),
        ],
        out_shape=[jax.ShapeDtypeStruct((t, LANES), jnp.int32),
                   jax.ShapeDtypeStruct((1, LANES), F32)],
        scratch_shapes=[pltpu.VMEM((1, LANES), F32), pltpu.VMEM((1, LANES), F32)],
        compiler_params=_cparams(("arbitrary", "arbitrary")),
        name="moe_plan",
    )(route, tri, utri)
    counts = counts[0, :N_EXPERTS].astype(jnp.int32)
    tile_end = jnp.cumsum((counts + MOE_TILE - 1) // MOE_TILE)
    n_used = tile_end[-1]
    tile_ids = jnp.minimum(jnp.arange(n_tiles, dtype=jnp.int32), n_used - 1)
    tile_expert = jnp.sum((tile_end[None, :] <= tile_ids[:, None]).astype(jnp.int32), axis=1)
    return pos[:, :2].T, tile_expert, n_used.reshape(1)


def _moe_ffn_kernel(te_ref, nused_ref, xs_ref, wg_ref, wu_ref, wd_ref, o_ref, acc, h_scr, *, splits):
    i = pl.program_id(0)
    f = pl.program_id(1)
    last_f = pl.num_programs(1) - 1
    used = i < nused_ref[0]

    @pl.when(used)
    def _():
        @pl.when(f == 0)
        def _():
            acc[...] = jnp.zeros(acc.shape, F32)
            h_scr[...] = _unpack_bf16_pairs(xs_ref[...]).astype(BF16)

        h = h_scr[...]
        for lo, hi in splits:
            a = _dot(h, wg_ref[0, :, lo:hi].astype(BF16))
            u = _dot(h, wu_ref[0, :, lo:hi].astype(BF16))
            acc[...] += _dot((_silu(a) * u).astype(BF16), wd_ref[0, lo:hi, :].astype(BF16))

        @pl.when(f == last_f)
        def _():
            o_ref[...] = _pack_bf16_pairs(acc[...])

    @pl.when(jnp.logical_not(used) & (f == last_f))
    def _():
        o_ref[...] = jnp.zeros(o_ref.shape, F32)


def _moe_ffn(xs, tile_expert, n_used, wg, wu, wd):
    rows, half = xs.shape
    d = 2 * half
    ff = wg.shape[2]
    if ff % MOE_FF_TILE == 0:
        tf, splits = MOE_FF_TILE, MOE_FF_SPLITS
    else:
        tf, splits = ff, ((0, ff),)
    n_tiles = rows // MOE_TILE
    grid_spec = pltpu.PrefetchScalarGridSpec(
        num_scalar_prefetch=2,
        grid=(n_tiles, ff // tf),
        in_specs=[
            pl.BlockSpec((MOE_TILE, half), lambda i, f, te, nu: (i, 0)),
            pl.BlockSpec((1, d, tf), lambda i, f, te, nu: (te[i], 0, f)),
            pl.BlockSpec((1, d, tf), lambda i, f, te, nu: (te[i], 0, f)),
            pl.BlockSpec((1, tf, d), lambda i, f, te, nu: (te[i], f, 0)),
        ],
        out_specs=pl.BlockSpec((MOE_TILE, half), lambda i, f, te, nu: (i, 0)),
        scratch_shapes=[pltpu.VMEM((MOE_TILE, d), F32), pltpu.VMEM((MOE_TILE, d), BF16)],
    )
    return pl.pallas_call(
        functools.partial(_moe_ffn_kernel, splits=splits),
        grid_spec=grid_spec,
        out_shape=jax.ShapeDtypeStruct((rows, half), F32),
        compiler_params=_cparams(("arbitrary", "arbitrary")),
        name="moe_ffn",
    )(tile_expert, n_used, xs, wg, wu, wd)


def _combine_kernel(*refs, final_norm):
    x1_ref, y0_ref, y1_ref, route_ref, mod_ref = refs[:5]
    fin_ref = refs[5] if final_norm else None
    o_ref = refs[-1]
    route = route_ref[...]
    w1 = route[:, ROUTE_W1:ROUTE_W1 + 1]
    w2 = route[:, ROUTE_W2:ROUTE_W2 + 1]
    y = w1 * _unpack_bf16_pairs(y0_ref[...]) + w2 * _unpack_bf16_pairs(y1_ref[...])
    x2 = x1_ref[...] + mod_ref[0][5:6] * y
    if final_norm:
        x2 = _rms(x2, fin_ref[...])
    o_ref[...] = x2


def _combine(x1, yg, route, mods, row_fn, fin_g, tm):
    t, d = x1.shape
    nb = t // tm
    final_norm = fin_g is not None
    in_specs = [
        pl.BlockSpec((tm, d), lambda i: (i, 0)),
        pl.BlockSpec((tm, d // 2), lambda i: (i, 0)),
        pl.BlockSpec((tm, d // 2), lambda i: (i + nb, 0)),
        pl.BlockSpec((tm, LANES), lambda i: (i, 0)),
        pl.BlockSpec((1, 6, d), lambda i: (row_fn(i), 0, 0)),
    ]
    args = [x1, yg, yg, route, mods]
    if final_norm:
        in_specs.append(pl.BlockSpec((1, d), lambda i: (0, 0)))
        args.append(fin_g)
    return pl.pallas_call(
        functools.partial(_combine_kernel, final_norm=final_norm),
        grid=(nb,),
        in_specs=in_specs,
        out_specs=pl.BlockSpec((tm, d), lambda i: (i, 0)),
        out_shape=jax.ShapeDtypeStruct((t, d), F32),
        compiler_params=_cparams(("arbitrary",)),
        name="moe_combine",
    )(*args)


def _moe(h2, x1, route, mods, row_fn, wg, wu, wd, fin_g, tm):
    t = h2.shape[0]
    n_tiles = -(-2 * t // MOE_TILE) + N_EXPERTS
    pos, tile_expert, n_used = _moe_plan(route, n_tiles, tm)
    xs = _sc_row_scatter2(h2, pos, n_tiles * MOE_TILE)
    ys = _moe_ffn(xs, tile_expert, n_used, wg, wu, wd)
    yg = _sc_row_gather(ys, pos.reshape(-1))
    return _combine(x1, yg, route, mods, row_fn, fin_g, tm)


def _rope_partner():
    j = np.arange(MLA_ROPE)
    return np.where((j % 32) < 16, j + 16, j - 16)


def _prep_in_weight(w):
    d = w.shape[0]
    cols = [w[:, 0:1024], w[:, 1056:1568], w[:, 1568:1824], w[:, 1824:1952], w[:, 1952:2016],
            w[:, 1024:1056], jnp.zeros((d, P_WIDTH - 2016), w.dtype)]
    return jnp.concatenate(cols, axis=1).astype(BF16)


def _prep_gate_weight(w_g2, b_g2):
    ws = []
    for z, off in ((0, MISC_GF), (1, MISC_GB)):
        ws.append(jnp.zeros((LANES, GLA_QK_W), F32).at[off:off + GLA_GATE_RANK].set(w_g2[z]))
    return jnp.stack(ws).astype(BF16), b_g2.reshape(2, 1, GLA_QK_W)


def _prep_mla_weights(w_uq, w_ukv):
    partner = _rope_partner()
    wq = w_uq.reshape(MLA_Q_RANK, MLA_HEADS, MLA_QK)
    wqn = wq[:, :, :MLA_NOPE].reshape(MLA_Q_RANK, MLA_HEADS * MLA_NOPE)
    rope = wq[:, :, MLA_NOPE:]
    pad = jnp.zeros((MLA_Q_RANK, MLA_HEADS, LANES - MLA_ROPE), w_uq.dtype)
    wqr = jnp.concatenate([rope, pad], axis=2).reshape(MLA_Q_RANK, MLA_HEADS * LANES)
    wqs = jnp.concatenate([rope[:, :, partner], pad], axis=2).reshape(MLA_Q_RANK, MLA_HEADS * LANES)
    wkv = w_ukv.reshape(MLA_KV_RANK, MLA_HEADS, MLA_NOPE + MLA_V)
    wknt = wkv[:, :, :MLA_NOPE].reshape(MLA_KV_RANK, MLA_HEADS * MLA_NOPE).T
    wv = wkv[:, :, MLA_NOPE:].reshape(MLA_KV_RANK, MLA_HEADS * MLA_V)
    perm = np.zeros((LANES, LANES), np.float32)
    perm[partner, np.arange(MLA_ROPE)] = 1.0
    eye = np.eye(MLA_ROPE, LANES, dtype=np.float32)
    return (wqn.astype(BF16), wqr.astype(BF16), wqs.astype(BF16), wknt.astype(BF16),
            wv.astype(BF16), jnp.asarray(perm, BF16), jnp.asarray(eye, BF16))


def _rope_tables(n_tok):
    rows = n_tok // GRID_W
    row = np.repeat(np.arange(rows, dtype=np.float32), GRID_W)
    col = np.tile(np.arange(GRID_W, dtype=np.float32), rows)
    nfreq = MLA_ROPE // 4
    inv = np.float32(ROPE_BASE) ** (-np.arange(nfreq, dtype=np.float32) / np.float32(nfreq))
    ar = (row[:, None] * inv).astype(np.float32)
    ac = (col[:, None] * inv).astype(np.float32)
    zero = np.zeros((n_tok, LANES - MLA_ROPE), np.float32)
    cos = np.concatenate([np.cos(ar), np.cos(ar), np.cos(ac), np.cos(ac), zero], axis=1)
    sin = np.concatenate([-np.sin(ar), np.sin(ar), -np.sin(ac), np.sin(ac), zero], axis=1)
    return jnp.asarray(cos, F32), jnp.asarray(sin, F32)


def _identity_tables(n_tok):
    cos = jnp.concatenate([jnp.ones((n_tok, MLA_ROPE), F32),
                           jnp.zeros((n_tok, LANES - MLA_ROPE), F32)], axis=1)
    return cos, jnp.zeros((n_tok, LANES), F32)


def _pick_tile(n, pref):
    t = min(n, pref)
    while n % t:
        t //= 2
    return t


def _pick_ff_tile(ff):
    best = LANES
    for m in range(1, ff // LANES + 1):
        if ff % (m * LANES) == 0 and m * LANES <= 1408:
            best = m * LANES
    return best


@jax.jit
def _forward(x, c, ctx, c_ctx, w_mod, b_mod, ln1_g, ln2_g, w_in, w_gla_g2, b_gla_g2, gla_norm_g,
             mla_q_norm_g, w_uq, mla_kv_norm_g, w_ukv, w_out, ffn_w_gate, ffn_w_up, ffn_w_down,
             router_w, exp_w_gate, exp_w_up, exp_w_down, final_norm_g):
    batch, seq, d = x.shape
    n_ctx = ctx.shape[1]
    depth = w_mod.shape[0]

    cvec = jnp.zeros((8, d), F32).at[:batch].set(c).at[batch].set(c_ctx)
    mods_all = _modulation(cvec, w_mod, b_mod).reshape(depth, 8, 6, d)

    xl = x.reshape(batch * seq, d)
    xc = ctx.reshape(batch * n_ctx, d)

    tm_l = _pick_tile(seq, 512)
    tm_c = _pick_tile(n_ctx, 256)
    tk_l = _pick_tile(seq, 1024)
    cb_l = _pick_tile(seq, 256)
    cb_c = _pick_tile(n_ctx, 256)
    row_l = lambda tm: (lambda i: i // (seq // tm))
    row_c = lambda i: batch

    rope_l = _rope_tables(seq)
    rope_c = _identity_tables(tm_c)
    zero_state = jnp.zeros((2, batch, GLA_QK_W, GLA_DV), F32)

    for i in range(depth):
        need_ctx = i < depth - 1
        last = i == depth - 1
        mods = mods_all[i]
        ln1 = ln1_g[i].reshape(1, d)
        ln2 = ln2_g[i].reshape(1, d)
        w_in_r = _prep_in_weight(w_in[i])
        gates = _prep_gate_weight(w_gla_g2[i], b_gla_g2[i])
        mla_w = _prep_mla_weights(w_uq[i], w_ukv[i])
        qg = mla_q_norm_g[i].reshape(1, MLA_Q_RANK)
        kvg = mla_kv_norm_g[i].reshape(1, MLA_KV_RANK)
        gg = gla_norm_g[i].reshape(1, GLA_DV)
        wo = w_out[i].astype(BF16)

        p_l = _inproj(xl, mods, row_l(tm_l), ln1, w_in_r, tm_l)
        p_c = _inproj(xc, mods, row_c, ln1, w_in_r, tm_c)

        oc_f, oc_b, s_ctx = _gla(p_c, *gates, zero_state, batch=batch, cb=cb_c)
        ol_f, ol_b, _ = _gla(p_l, *gates, s_ctx, batch=batch, cb=cb_l)

        q_l, kt_l, v_l = _mlaprep(p_l, *rope_l, qg, kvg, mla_w, batch=batch, tm=tk_l)
        q_c, kt_c, v_c = _mlaprep(p_c, *rope_c, qg, kvg, mla_w, batch=batch, tm=tm_c)
        m_l = _attention_pipelined(q_l, kt_l, v_l, kt_c, v_c, tq=tk_l, n_sub=max(1, tk_l // 512))
        m_l = m_l.reshape(batch * seq, MLA_V_W)

        if i % 2 == 0:
            j = i // 2
            router = None
            wg = ffn_w_gate[j].astype(BF16)
            wu = ffn_w_up[j].astype(BF16)
            wd = ffn_w_down[j].astype(BF16)
        else:
            j = i // 2
            rw = jnp.zeros((d, LANES), F32).at[:, :N_EXPERTS].set(router_w[j])
            rw_hi = rw.astype(BF16)
            router = (rw_hi, (rw - rw_hi.astype(F32)).astype(BF16))
            wg, wu, wd = exp_w_gate[j], exp_w_up[j], exp_w_down[j]
        fin = final_norm_g.reshape(1, d) if last else None

        if router is None:
            xl = _outproj_ffn(xl, ol_f, ol_b, p_l, m_l, mods, row_l(tm_l), gg, wo, ln2,
                              wg, wu, wd, fin, tm_l)
        else:
            outs = _outproj(xl, ol_f, ol_b, p_l, m_l, mods, row_l(tm_l), gg, wo, ln2, router, tm_l)
            xl = _moe(outs[1], outs[0], outs[2], mods, row_l(tm_l), wg, wu, wd, fin, tm_l)

        if need_ctx:
            m_c = _attention(q_c, [(kt_c, v_c)], tq=tm_c, n_sub=1).reshape(batch * n_ctx, MLA_V_W)
            if router is None:
                xc = _outproj_ffn(xc, oc_f, oc_b, p_c, m_c, mods, row_c, gg, wo, ln2,
                                  wg, wu, wd, None, tm_c)
            else:
                outs_c = _outproj(xc, oc_f, oc_b, p_c, m_c, mods, row_c, gg, wo, ln2, router, tm_c)
                tm_fc = _pick_tile(batch * n_ctx, 512)
                r_c = outs_c[2]
                lane = jnp.arange(LANES, dtype=F32)[None, :]
                comb_c = (jnp.where(lane == r_c[:, ROUTE_E1:ROUTE_E1 + 1], r_c[:, ROUTE_W1:ROUTE_W1 + 1], 0.0)
                          + jnp.where(lane == r_c[:, ROUTE_E2:ROUTE_E2 + 1], r_c[:, ROUTE_W2:ROUTE_W2 + 1], 0.0))
                bits = lax.bitcast_convert_type(outs_c[1], jnp.uint32)
                h2_c = jnp.concatenate([lax.bitcast_convert_type(bits << 16, F32),
                                        lax.bitcast_convert_type(bits & jnp.uint32(0xFFFF0000), F32)], axis=1)
                xc = _ffn(h2_c.astype(BF16), outs_c[0], mods, row_c, comb_c,
                          wg.astype(BF16), wu.astype(BF16), wd.astype(BF16), None, tm_fc,
                          _pick_ff_tile(wg.shape[2]))

    return xl.reshape(batch, seq, d)


def kernel(x, c, ctx, c_ctx, w_mod, b_mod, ln1_g, ln2_g, w_in, w_gla_g2, b_gla_g2, gla_norm_g,
           mla_q_norm_g, w_uq, mla_kv_norm_g, w_ukv, w_out, ffn_w_gate, ffn_w_up, ffn_w_down,
           router_w, exp_w_gate, exp_w_up, exp_w_down, final_norm_g):
    return _forward(x, c, ctx, c_ctx, w_mod, b_mod, ln1_g, ln2_g, w_in, w_gla_g2, b_gla_g2,
                    gla_norm_g, mla_q_norm_g, w_uq, mla_kv_norm_g, w_ukv, w_out, ffn_w_gate,
                    ffn_w_up, ffn_w_down, router_w, exp_w_gate, exp_w_up, exp_w_down, final_norm_g)
```

```python
import functools

import numpy as np
import jax
import jax.numpy as jnp
from jax import lax
from jax.experimental import pallas as pl
from jax.experimental.pallas import tpu as pltpu
from jax.experimental.pallas import tpu_sc as plsc

F32 = jnp.float32
BF16 = jnp.bfloat16

D_MODEL = 1024
EPS = 1e-6
GRID_W = 64

GLA_HEADS = 4
GLA_DK = 64
GLA_DV = 128
GLA_GATE_RANK = 16
GLA_GATE_NORM = 16.0
GLA_CHUNK = 64
GLA_QK_W = GLA_HEADS * GLA_DK
GLA_V_W = GLA_HEADS * GLA_DV
GLA_EXP_CLAMP = 80.0

MLA_HEADS = 4
MLA_NOPE = 128
MLA_ROPE = 64
MLA_V = 128
MLA_QK = MLA_NOPE + MLA_ROPE
MLA_Q_RANK = 256
MLA_KV_RANK = 128
MLA_SCALE = MLA_QK ** -0.5
MLA_Q_SCALE = MLA_SCALE * 1.4426950408889634
MLA_V_W = MLA_HEADS * MLA_V
MLA_V_EXT = 2 * MLA_V
ROPE_BASE = 10000.0

N_EXPERTS = 8
LANES = 128
ROUTE_E1, ROUTE_E2, ROUTE_W1, ROUTE_W2 = 0, 1, 2, 3

SC_CORES = 2
SC_SUBCORES = 16
SC_GATHER_ROWS = 128
MOE_TILE = 1024
MOE_FF_TILE = 512
MOE_FF_SPLITS = ((0, 256), (256, 512))

P_Q, P_K, P_V, P_R, P_CQ, P_CKV, P_MISC = 0, 256, 512, 1024, 1536, 1792, 1920
P_WIDTH = 2048
MISC_KR, MISC_GF, MISC_GB = 0, 64, 80

VMEM_LIMIT = 56 * 1024 * 1024


def _cparams(sem):
    return pltpu.CompilerParams(dimension_semantics=sem, vmem_limit_bytes=VMEM_LIMIT)


def _rms(x, g):
    return x * lax.rsqrt(jnp.mean(x * x, axis=-1, keepdims=True) + EPS) * g


def _silu(x):
    return x / (1.0 + jnp.exp(-x))


def _dot(a, b):
    return jnp.dot(a, b, preferred_element_type=F32)


def _dot_nt(a, b):
    return lax.dot_general(a, b, (((1,), (1,)), ((), ())), preferred_element_type=F32)


def _dot_tn(a, b):
    return lax.dot_general(a, b, (((0,), (0,)), ((), ())), preferred_element_type=F32)


def _pack_bf16_pairs(x):
    w = x.shape[1] // 2
    words = pltpu.pack_elementwise([x[:, :w], x[:, w:]], packed_dtype=BF16)
    return lax.bitcast_convert_type(words, F32)


def _unpack_bf16_pairs(p):
    words = lax.bitcast_convert_type(p, jnp.int32)
    lo = pltpu.unpack_elementwise(words, index=0, packed_dtype=BF16, unpacked_dtype=F32)
    hi = pltpu.unpack_elementwise(words, index=1, packed_dtype=BF16, unpacked_dtype=F32)
    return jnp.concatenate([lo, hi], axis=1)


def _mod_kernel(c_ref, w_ref, b_ref, o_ref):
    s = _silu(c_ref[...]).astype(BF16)
    o_ref[0] = _dot(s, w_ref[0].astype(BF16)) + b_ref[0]


def _modulation(cvec, w_mod, b_mod):
    depth, d, n = w_mod.shape
    tn = 1536
    return pl.pallas_call(
        _mod_kernel,
        grid=(depth, n // tn),
        in_specs=[
            pl.BlockSpec((8, d), lambda l, j: (0, 0)),
            pl.BlockSpec((1, d, tn), lambda l, j: (l, 0, j)),
            pl.BlockSpec((1, 1, tn), lambda l, j: (l, 0, j)),
        ],
        out_specs=pl.BlockSpec((1, 8, tn), lambda l, j: (l, 0, j)),
        out_shape=jax.ShapeDtypeStruct((depth, 8, n), F32),
        compiler_params=_cparams(("arbitrary", "arbitrary")),
        name="modulation",
    )(cvec, w_mod, b_mod.reshape(depth, 1, n))


def _inproj_kernel(x_ref, mod_ref, g_ref, w_ref, o_ref):
    m = mod_ref[0]
    h = _rms(x_ref[...], g_ref[...]) * (1.0 + m[1:2]) + m[0:1]
    o_ref[...] = _dot(h.astype(BF16), w_ref[...]).astype(BF16)


def _inproj(x, mods, row_fn, ln_g, w, tm):
    t, d = x.shape
    return pl.pallas_call(
        _inproj_kernel,
        grid=(t // tm,),
        in_specs=[
            pl.BlockSpec((tm, d), lambda i: (i, 0)),
            pl.BlockSpec((1, 6, d), lambda i: (row_fn(i), 0, 0)),
            pl.BlockSpec((1, d), lambda i: (0, 0)),
            pl.BlockSpec((d, P_WIDTH), lambda i: (0, 0)),
        ],
        out_specs=pl.BlockSpec((tm, P_WIDTH), lambda i: (i, 0)),
        out_shape=jax.ShapeDtypeStruct((t, P_WIDTH), BF16),
        compiler_params=_cparams(("arbitrary",)),
        name="inproj",
    )(x, mods, ln_g, w)


def _gla_direction(q_ref, k_ref, v_ref, misc_ref, wg, bg, tri, s_scr, o_ref, *, reverse, n_chunks):
    c_len = GLA_CHUNK
    pre = _dot(misc_ref[...], wg) + bg
    g = (jnp.minimum(pre, 0.0) - jnp.log(1.0 + jnp.exp(-jnp.abs(pre)))) * (1.0 / GLA_GATE_NORM)
    g_hi = g.astype(BF16)
    g_lo = (g - g_hi.astype(F32)).astype(BF16)
    cum = _dot(tri, g_hi) + _dot(tri, g_lo)
    tot_rows = jnp.concatenate(
        [cum[c * c_len:c * c_len + 1] if reverse else cum[(c + 1) * c_len - 1:(c + 1) * c_len]
         for c in range(n_chunks)] + [jnp.zeros((8 - n_chunks, GLA_QK_W), F32)], axis=0)
    t_hi = tot_rows.astype(BF16)
    t_lo = (tot_rows - t_hi.astype(F32)).astype(BF16)
    eye = (lax.broadcasted_iota(jnp.int32, (GLA_QK_W, GLA_QK_W), 0)
           == lax.broadcasted_iota(jnp.int32, (GLA_QK_W, GLA_QK_W), 1))
    eye = jnp.where(eye, 1.0, 0.0).astype(BF16)
    tot_cols = _dot_nt(eye, t_hi) + _dot_nt(eye, t_lo)

    lane = lax.broadcasted_iota(jnp.int32, (c_len, GLA_QK_W), 1)
    head_masks = [(lane >= h * GLA_DK) & (lane < (h + 1) * GLA_DK) for h in range(GLA_HEADS)]
    row = lax.broadcasted_iota(jnp.int32, (GLA_HEADS * c_len, c_len), 0) % c_len
    col = lax.broadcasted_iota(jnp.int32, (GLA_HEADS * c_len, c_len), 1)
    pair_mask = (col >= row) if reverse else (col <= row)

    def stack_heads(a):
        return jnp.concatenate([jnp.where(mk, a, 0.0) for mk in head_masks], axis=0).astype(BF16)

    def step(c):
        sl = slice(c * c_len, (c + 1) * c_len)
        xc = cum[sl]
        tot = tot_rows[c:c + 1]
        ref = xc[c_len // 2:c_len // 2 + 1]
        qc = q_ref[sl, :].astype(F32) * (GLA_DK ** -0.5)
        kc = k_ref[sl, :].astype(F32)
        vc = v_ref[sl, :]
        q_mid = qc * jnp.exp(jnp.minimum(xc - ref, GLA_EXP_CLAMP))
        k_mid = (kc * jnp.exp(jnp.minimum(ref - xc, GLA_EXP_CLAMP))).astype(BF16)
        q_dec = qc * jnp.exp(xc)
        k_dec = kc * jnp.exp(tot - xc)

        attn = _dot_nt(stack_heads(q_mid), k_mid)
        attn = jnp.where(pair_mask, attn, 0.0).astype(BF16)
        s_prev = s_scr[...]
        o_inter = _dot(stack_heads(q_dec), s_prev.astype(BF16))
        kv = []
        for h in range(GLA_HEADS):
            rs = slice(h * c_len, (h + 1) * c_len)
            vs = slice(h * GLA_DV, (h + 1) * GLA_DV)
            o_h = o_inter[rs] + _dot(attn[rs], vc[:, vs])
            o_ref[sl, vs] = o_h.astype(BF16)
            kv.append(_dot_tn(k_dec[:, h * GLA_DK:(h + 1) * GLA_DK].astype(BF16), vc[:, vs]))
        s_scr[...] = s_prev * jnp.exp(tot_cols[:, c:c + 1]) + jnp.concatenate(kv, axis=0)

    return step


def _gla_kernel(qf_ref, kf_ref, vf_ref, mf_ref, qb_ref, kb_ref, vb_ref, mb_ref,
                wg_ref, bg_ref, tri_ref, s0_ref, of_ref, ob_ref, sfin_ref, sf_scr, sb_scr, *, n_chunks):
    blk = pl.program_id(1)

    @pl.when(blk == 0)
    def _():
        sf_scr[...] = s0_ref[0, 0]
        sb_scr[...] = s0_ref[1, 0]

    fwd = _gla_direction(qf_ref, kf_ref, vf_ref, mf_ref, wg_ref[0], bg_ref[0], tri_ref[0],
                         sf_scr, of_ref, reverse=False, n_chunks=n_chunks)
    bwd = _gla_direction(qb_ref, kb_ref, vb_ref, mb_ref, wg_ref[1], bg_ref[1], tri_ref[1],
                         sb_scr, ob_ref, reverse=True, n_chunks=n_chunks)
    for c in range(n_chunks):
        fwd(c)
        bwd(n_chunks - 1 - c)

    @pl.when(blk == pl.num_programs(1) - 1)
    def _():
        sfin_ref[0, 0] = sf_scr[...]
        sfin_ref[1, 0] = sb_scr[...]


def _block_diag_tri(n_chunks):
    c = GLA_CHUNK
    eye = np.eye(n_chunks, dtype=np.float32)
    lower = np.kron(eye, np.tril(np.ones((c, c), np.float32)))
    upper = np.kron(eye, np.triu(np.ones((c, c), np.float32)))
    return jnp.asarray(np.stack([lower, upper]), dtype=BF16)


def _gla(p, wg, bg, s0, *, batch, cb):
    t_all = p.shape[0]
    nblk = t_all // batch // cb
    n_chunks = cb // GLA_CHUNK
    assert n_chunks <= 8

    fw = lambda b, i: b * nblk + i
    bw = lambda b, i: b * nblk + (nblk - 1 - i)
    full = lambda a: pl.BlockSpec(a.shape, lambda b, i: (0,) * a.ndim)
    tri = _block_diag_tri(n_chunks)

    def token_specs(tok):
        return [
            pl.BlockSpec((cb, GLA_QK_W), lambda b, i: (tok(b, i), P_Q // GLA_QK_W)),
            pl.BlockSpec((cb, GLA_QK_W), lambda b, i: (tok(b, i), P_K // GLA_QK_W)),
            pl.BlockSpec((cb, GLA_V_W), lambda b, i: (tok(b, i), P_V // GLA_V_W)),
            pl.BlockSpec((cb, LANES), lambda b, i: (tok(b, i), P_MISC // LANES)),
        ]

    state_spec = pl.BlockSpec((2, 1, GLA_QK_W, GLA_DV), lambda b, i: (0, b, 0, 0))
    return pl.pallas_call(
        functools.partial(_gla_kernel, n_chunks=n_chunks),
        grid=(batch, nblk),
        in_specs=token_specs(fw) + token_specs(bw) + [full(wg), full(bg), full(tri), state_spec],
        out_specs=[
            pl.BlockSpec((cb, GLA_V_W), lambda b, i: (fw(b, i), 0)),
            pl.BlockSpec((cb, GLA_V_W), lambda b, i: (bw(b, i), 0)),
            state_spec,
        ],
        out_shape=[
            jax.ShapeDtypeStruct((t_all, GLA_V_W), BF16),
            jax.ShapeDtypeStruct((t_all, GLA_V_W), BF16),
            jax.ShapeDtypeStruct((2, batch, GLA_QK_W, GLA_DV), F32),
        ],
        scratch_shapes=[pltpu.VMEM((GLA_QK_W, GLA_DV), F32), pltpu.VMEM((GLA_QK_W, GLA_DV), F32)],
        compiler_params=_cparams(("arbitrary", "arbitrary")),
        name="gla",
    )(p, p, p, p, p, p, p, p, wg, bg, tri, s0)


def _mlaprep_kernel(cq_ref, ckv_ref, misc_ref, cos_ref, sin_ref, qg_ref, kvg_ref,
                    wqn_ref, wqr_ref, wqs_ref, wknt_ref, wv_ref, perm_ref, eye_ref,
                    q_ref, kt_ref, v_ref):
    cos = cos_ref[...]
    sin = sin_ref[...]
    cqn = _rms(cq_ref[...].astype(F32), qg_ref[...]).astype(BF16)
    qn = _dot(cqn, wqn_ref[...])
    qr = _dot(cqn, wqr_ref[...])
    qs = _dot(cqn, wqs_ref[...])
    for h in range(MLA_HEADS):
        ls = slice(h * LANES, (h + 1) * LANES)
        q_ref[0, h, :, 0:MLA_NOPE] = (qn[:, ls] * MLA_Q_SCALE).astype(BF16)
        rot = qr[:, ls] * cos + qs[:, ls] * sin
        q_ref[0, h, :, MLA_NOPE:MLA_QK] = (rot[:, 0:MLA_ROPE] * MLA_Q_SCALE).astype(BF16)

    ckvn = _rms(ckv_ref[...].astype(F32), kvg_ref[...]).astype(BF16)
    knt = _dot_nt(wknt_ref[...], ckvn)
    vv = _dot(ckvn, wv_ref[...])
    misc = misc_ref[...]
    kr = misc.astype(F32) * cos + _dot(misc, perm_ref[...]) * sin
    krt = _dot_nt(eye_ref[...], kr.astype(BF16)).astype(BF16)
    for h in range(MLA_HEADS):
        kt_ref[0, h, 0, 0:MLA_NOPE, :] = knt[h * MLA_NOPE:(h + 1) * MLA_NOPE].astype(BF16)
        kt_ref[0, h, 0, MLA_NOPE:MLA_QK, :] = krt
        v_ref[0, h, :, 0:MLA_V] = vv[:, h * MLA_V:(h + 1) * MLA_V].astype(BF16)
        v_ref[0, h, :, MLA_V:MLA_V_EXT] = jnp.ones((vv.shape[0], MLA_V), BF16)


def _mlaprep(p, cos, sin, qg, kvg, wts, *, batch, tm):
    t_all = p.shape[0]
    t = t_all // batch
    nb = t // tm
    ntab = cos.shape[0] // tm
    wqn, wqr, wqs, wknt, wv, perm, eye = wts
    full = lambda a: pl.BlockSpec(a.shape, lambda b, i: (0,) * a.ndim)
    return pl.pallas_call(
        _mlaprep_kernel,
        grid=(batch, nb),
        in_specs=[
            pl.BlockSpec((tm, MLA_Q_RANK), lambda b, i: (b * nb + i, P_CQ // MLA_Q_RANK)),
            pl.BlockSpec((tm, MLA_KV_RANK), lambda b, i: (b * nb + i, P_CKV // MLA_KV_RANK)),
            pl.BlockSpec((tm, LANES), lambda b, i: (b * nb + i, P_MISC // LANES)),
            pl.BlockSpec((tm, LANES), lambda b, i: (i % ntab, 0)),
            pl.BlockSpec((tm, LANES), lambda b, i: (i % ntab, 0)),
            full(qg), full(kvg), full(wqn), full(wqr), full(wqs), full(wknt), full(wv),
            full(perm), full(eye),
        ],
        out_specs=[
            pl.BlockSpec((1, MLA_HEADS, tm, MLA_QK), lambda b, i: (b, 0, i, 0)),
            pl.BlockSpec((1, MLA_HEADS, 1, MLA_QK, tm), lambda b, i: (b, 0, i, 0, 0)),
            pl.BlockSpec((1, MLA_HEADS, tm, MLA_V_EXT), lambda b, i: (b, 0, i, 0)),
        ],
        out_shape=[
            jax.ShapeDtypeStruct((batch, MLA_HEADS, t, MLA_QK), BF16),
            jax.ShapeDtypeStruct((batch, MLA_HEADS, nb, MLA_QK, tm), BF16),
            jax.ShapeDtypeStruct((batch, MLA_HEADS, t, MLA_V_EXT), BF16),
        ],
        compiler_params=_cparams(("arbitrary", "arbitrary")),
        name="mlaprep",
    )(p, p, p, cos, sin, qg, kvg, wqn, wqr, wqs, wknt, wv, perm, eye)


def _attn_kernel(*refs, n_seg, n_sub):
    q_ref = refs[0]
    kt_refs = refs[1:1 + 2 * n_seg:2]
    v_refs = refs[2:2 + 2 * n_seg:2]
    o_ref = refs[1 + 2 * n_seg]
    m_scr, acc_scr = refs[2 + 2 * n_seg:]

    rows_per_sub = q_ref.shape[2] // n_sub
    m_scr[...] = jnp.full(m_scr.shape, -jnp.inf, F32)
    acc_scr[...] = jnp.zeros(acc_scr.shape, F32)

    for kt_ref, v_ref in zip(kt_refs, v_refs):
        n_blocks, tk = kt_ref.shape[2], kt_ref.shape[4]

        def step(j, carry, kt_ref=kt_ref, v_ref=v_ref, tk=tk):
            kt = kt_ref[0, 0, j]
            v_blk = v_ref[0, 0, pl.ds(pl.multiple_of(j * tk, tk), tk), :]
            for u in range(n_sub):
                rows = slice(u * rows_per_sub, (u + 1) * rows_per_sub)
                s = _dot(q_ref[0, 0, rows, :], kt)
                m_prev = m_scr[rows, :]
                m_next = jnp.maximum(m_prev, jnp.max(s, axis=1, keepdims=True))
                p = jnp.exp2((s - jnp.concatenate([m_next] * (tk // LANES), axis=1)).astype(BF16))
                alpha = jnp.exp2(m_prev - m_next)
                acc_scr[rows, :] = (jnp.concatenate([alpha] * (MLA_V_EXT // LANES), axis=1)
                                    * acc_scr[rows, :] + _dot(p, v_blk))
                m_scr[rows, :] = m_next
            return carry

        lax.fori_loop(0, n_blocks, step, 0)

    o_ref[0] = (acc_scr[:, 0:MLA_V] / acc_scr[:, MLA_V:MLA_V_EXT]).astype(BF16)


def _attn_pipe_kernel(q_ref, kt_ref, v_ref, ktt_ref, vt_ref, o_ref,
                      m_scr, acc_scr, s0_scr, s1_scr, st_scr, *, n_sub):
    n_blocks, tk = kt_ref.shape[2], kt_ref.shape[4]
    tq = m_scr.shape[0]
    n_q = q_ref.shape[2] // tq
    rows_per_sub = tq // n_sub
    subs = [slice(u * rows_per_sub, (u + 1) * rows_per_sub) for u in range(n_sub)]
    bufs = (s0_scr, s1_scr)

    def reset():
        m_scr[...] = jnp.full(m_scr.shape, -jnp.inf, F32)
        acc_scr[...] = jnp.zeros(acc_scr.shape, F32)

    def scores(qi, kt, s_ref):
        for rows in subs:
            q_rows = pl.ds(pl.multiple_of(qi * tq + rows.start, rows_per_sub), rows_per_sub)
            s_ref[rows, :] = _dot(q_ref[0, 0, q_rows, :], kt)

    def softmax_pv(s_ref, v_blk):
        width = s_ref.shape[1]
        for rows in subs:
            s = s_ref[rows, :]
            m_prev = m_scr[rows, :]
            m_next = jnp.maximum(m_prev, jnp.max(s, axis=1, keepdims=True))
            p = jnp.exp2((s - jnp.concatenate([m_next] * (width // LANES), axis=1)).astype(BF16))
            alpha = jnp.exp2(m_prev - m_next)
            acc_scr[rows, :] = (jnp.concatenate([alpha] * (MLA_V_EXT // LANES), axis=1)
                                * acc_scr[rows, :] + _dot(p, v_blk))
            m_scr[rows, :] = m_next

    reset()
    scores(0, kt_ref[0, 0, 0], bufs[0])

    def query_block(qi, carry):
        for j in range(n_blocks):
            if j + 1 < n_blocks:
                scores(qi, kt_ref[0, 0, j + 1], bufs[(j + 1) % 2])
            else:
                scores(qi, ktt_ref[0, 0, 0], st_scr)
            softmax_pv(bufs[j % 2], v_ref[0, 0, j * tk:(j + 1) * tk, :])
        scores(jnp.minimum(qi + 1, n_q - 1), kt_ref[0, 0, 0], bufs[0])
        softmax_pv(st_scr, vt_ref[0, 0])
        o_rows = pl.ds(pl.multiple_of(qi * tq, tq), tq)
        o_ref[0, o_rows, :] = (acc_scr[:, 0:MLA_V] / acc_scr[:, MLA_V:MLA_V_EXT]).astype(BF16)
        reset()
        return carry

    lax.fori_loop(0, n_q, query_block, 0)


def _attention_pipelined(q, kt, v, kt_tail, v_tail, *, tq, n_sub):
    b, h, t, dqk = q.shape
    tk, tt = kt.shape[4], kt_tail.shape[4]
    assert kt_tail.shape[2] == 1
    return pl.pallas_call(
        functools.partial(_attn_pipe_kernel, n_sub=n_sub),
        grid=(b, h),
        in_specs=[
            pl.BlockSpec((1, 1, t, dqk), lambda bi, hi: (bi, hi, 0, 0)),
            pl.BlockSpec((1, 1) + kt.shape[2:], lambda bi, hi: (bi, hi, 0, 0, 0)),
            pl.BlockSpec((1, 1) + v.shape[2:], lambda bi, hi: (bi, hi, 0, 0)),
            pl.BlockSpec((1, 1) + kt_tail.shape[2:], lambda bi, hi: (bi, hi, 0, 0, 0)),
            pl.BlockSpec((1, 1) + v_tail.shape[2:], lambda bi, hi: (bi, hi, 0, 0)),
        ],
        out_specs=pl.BlockSpec((1, t, MLA_V), lambda bi, hi: (bi, 0, hi)),
        out_shape=jax.ShapeDtypeStruct((b, t, h * MLA_V), BF16),
        scratch_shapes=[pltpu.VMEM((tq, LANES), F32), pltpu.VMEM((tq, MLA_V_EXT), F32),
                        pltpu.VMEM((tq, tk), F32), pltpu.VMEM((tq, tk), F32),
                        pltpu.VMEM((tq, tt), F32)],
        compiler_params=_cparams(("arbitrary", "arbitrary")),
        name="mla_attention_pipe",
    )(q, kt, v, kt_tail, v_tail)


def _attention(q, segs, *, tq, n_sub):
    b, h, t, dqk = q.shape
    in_specs = [pl.BlockSpec((1, 1, tq, dqk), lambda bi, hi, qi: (bi, hi, qi, 0))]
    args = [q]
    for kt, v in segs:
        in_specs.append(pl.BlockSpec((1, 1) + kt.shape[2:], lambda bi, hi, qi: (bi, hi, 0, 0, 0)))
        in_specs.append(pl.BlockSpec((1, 1) + v.shape[2:], lambda bi, hi, qi: (bi, hi, 0, 0)))
        args += [kt, v]
    return pl.pallas_call(
        functools.partial(_attn_kernel, n_seg=len(segs), n_sub=n_sub),
        grid=(b, h, t // tq),
        in_specs=in_specs,
        out_specs=pl.BlockSpec((1, tq, MLA_V), lambda bi, hi, qi: (bi, qi, hi)),
        out_shape=jax.ShapeDtypeStruct((b, t, h * MLA_V), BF16),
        scratch_shapes=[pltpu.VMEM((tq, LANES), F32), pltpu.VMEM((tq, MLA_V_EXT), F32)],
        compiler_params=_cparams(("arbitrary", "arbitrary", "arbitrary")),
        name="mla_attention",
    )(*args)


def _mix_residual_norm(x_ref, of_ref, ob_ref, r_ref, mla_ref, m, gg_ref, wo_ref, ln2_ref,
                       rows=slice(None)):
    o = of_ref[rows, :].astype(F32) + ob_ref[rows, :].astype(F32)
    gg = gg_ref[...]
    y = jnp.concatenate(
        [_rms(o[:, h * GLA_DV:(h + 1) * GLA_DV], gg) for h in range(GLA_HEADS)], axis=1)
    mix = (y * _silu(r_ref[rows, :].astype(F32))).astype(BF16)
    yo = _dot(mix, wo_ref[0:GLA_V_W, :]) + _dot(mla_ref[rows, :], wo_ref[GLA_V_W:, :])
    x1 = x_ref[rows, :] + m[2:3] * yo
    h2 = _rms(x1, ln2_ref[...]) * (1.0 + m[4:5]) + m[3:4]
    return x1, h2


def _outproj_ffn_kernel(*refs, groups, final_norm):
    (x_ref, of_ref, ob_ref, r_ref, mla_ref, mod_ref, gg_ref, wo_ref, ln2_ref,
     wg_ref, wu_ref, wd_ref) = refs[:12]
    fin_ref = refs[12] if final_norm else None
    o_ref = refs[-1]
    m = mod_ref[0]
    x1, h2 = _mix_residual_norm(x_ref, of_ref, ob_ref, r_ref, mla_ref, m, gg_ref, wo_ref, ln2_ref)
    h = h2.astype(BF16)
    y = None
    for lo, hi in groups:
        a = _dot(h, wg_ref[:, lo:hi])
        u = _dot(h, wu_ref[:, lo:hi])
        part = _dot((_silu(a) * u).astype(BF16), wd_ref[lo:hi, :])
        y = part if y is None else y + part
    x2 = x1 + m[5:6] * y
    if final_norm:
        x2 = _rms(x2, fin_ref[...])
    o_ref[...] = x2


def _outproj_ffn(x, o_f, o_b, p, mla, mods, row_fn, gg, wo, ln2, wg, wu, wd, fin_g, tm):
    t, d = x.shape
    ff = wg.shape[1]
    final_norm = fin_g is not None
    step = 1024 if ff > 1024 else ff
    groups = tuple((lo, min(lo + step, ff)) for lo in range(0, ff, step))
    resident = lambda a: pl.BlockSpec(a.shape, lambda i: (0,) * a.ndim, pipeline_mode=pl.Buffered(1))
    in_specs = [
        pl.BlockSpec((tm, d), lambda i: (i, 0)),
        pl.BlockSpec((tm, GLA_V_W), lambda i: (i, 0)),
        pl.BlockSpec((tm, GLA_V_W), lambda i: (i, 0)),
        pl.BlockSpec((tm, GLA_V_W), lambda i: (i, P_R // GLA_V_W)),
        pl.BlockSpec((tm, MLA_V_W), lambda i: (i, 0)),
        pl.BlockSpec((1, 6, d), lambda i: (row_fn(i), 0, 0)),
        resident(gg), resident(wo), resident(ln2), resident(wg), resident(wu), resident(wd),
    ]
    args = [x, o_f, o_b, p, mla, mods, gg, wo, ln2, wg, wu, wd]
    if final_norm:
        in_specs.append(resident(fin_g))
        args.append(fin_g)
    return pl.pallas_call(
        functools.partial(_outproj_ffn_kernel, groups=groups, final_norm=final_norm),
        grid=(t // tm,),
        in_specs=in_specs,
        out_specs=pl.BlockSpec((tm, d), lambda i: (i, 0)),
        out_shape=jax.ShapeDtypeStruct((t, d), F32),
        compiler_params=_cparams(("arbitrary",)),
        name="outproj_ffn",
    )(*args)


def _outproj_kernel(*refs, with_router):
    (x_ref, of_ref, ob_ref, r_ref, mla_ref, mod_ref, gg_ref, wo_ref, ln2_ref) = refs[:9]
    if with_router:
        rwh_ref, rwl_ref, x1_ref, h2_ref, comb_ref = refs[9:]
    else:
        x1_ref, h2_ref = refs[9:]
    m = mod_ref[0]
    tm = x_ref.shape[0]
    n_groups = 2 if tm % 32 == 0 else 1
    for g in range(n_groups):
        rows = slice(g * tm // n_groups, (g + 1) * tm // n_groups)
        x1, h2 = _mix_residual_norm(x_ref, of_ref, ob_ref, r_ref, mla_ref, m, gg_ref, wo_ref,
                                    ln2_ref, rows)
        x1_ref[rows, :] = x1
        if not with_router:
            h2_ref[rows, :] = h2.astype(BF16)
            continue
        h2_ref[rows, :] = _pack_bf16_pairs(h2)
        h_hi = h2.astype(BF16)
        h_lo = (h2 - h_hi.astype(F32)).astype(BF16)
        logits = _dot(h_hi, rwh_ref[...]) + _dot(h_lo, rwh_ref[...]) + _dot(h_hi, rwl_ref[...])
        lane = lax.broadcasted_iota(jnp.int32, logits.shape, 1).astype(F32)
        neg = jnp.float32(-jnp.inf)
        logits = jnp.where(lane < N_EXPERTS, logits, neg)
        m1 = jnp.max(logits, axis=1, keepdims=True)
        i1 = jnp.min(jnp.where(logits == m1, lane, float(LANES)), axis=1, keepdims=True)
        rest = jnp.where(lane == i1, neg, logits)
        m2 = jnp.max(rest, axis=1, keepdims=True)
        i2 = jnp.min(jnp.where(rest == m2, lane, float(LANES)), axis=1, keepdims=True)
        e2 = jnp.exp(m2 - m1)
        w1 = 1.0 / (1.0 + e2)
        comb_ref[rows, :] = (jnp.where(lane == ROUTE_E1, i1, 0.0) + jnp.where(lane == ROUTE_E2, i2, 0.0)
                             + jnp.where(lane == ROUTE_W1, w1, 0.0)
                             + jnp.where(lane == ROUTE_W2, e2 * w1, 0.0))


def _outproj(x, o_f, o_b, p, mla, mods, row_fn, gg, wo, ln2, router, tm):
    t, d = x.shape
    with_router = router is not None
    full = lambda a: pl.BlockSpec(a.shape, lambda i: (0,) * a.ndim)
    in_specs = [
        pl.BlockSpec((tm, d), lambda i: (i, 0)),
        pl.BlockSpec((tm, GLA_V_W), lambda i: (i, 0)),
        pl.BlockSpec((tm, GLA_V_W), lambda i: (i, 0)),
        pl.BlockSpec((tm, GLA_V_W), lambda i: (i, P_R // GLA_V_W)),
        pl.BlockSpec((tm, MLA_V_W), lambda i: (i, 0)),
        pl.BlockSpec((1, 6, d), lambda i: (row_fn(i), 0, 0)),
        full(gg), full(wo), full(ln2),
    ]
    args = [x, o_f, o_b, p, mla, mods, gg, wo, ln2]
    h2_shape = jax.ShapeDtypeStruct((t, d // 2), F32) if with_router else jax.ShapeDtypeStruct((t, d), BF16)
    out_specs = [pl.BlockSpec((tm, d), lambda i: (i, 0)),
                 pl.BlockSpec((tm, h2_shape.shape[1]), lambda i: (i, 0))]
    out_shape = [jax.ShapeDtypeStruct((t, d), F32), h2_shape]
    if with_router:
        in_specs += [full(router[0]), full(router[1])]
        args += list(router)
        out_specs.append(pl.BlockSpec((tm, LANES), lambda i: (i, 0)))
        out_shape.append(jax.ShapeDtypeStruct((t, LANES), F32))
    return pl.pallas_call(
        functools.partial(_outproj_kernel, with_router=with_router),
        grid=(t // tm,),
        in_specs=in_specs,
        out_specs=out_specs,
        out_shape=out_shape,
        compiler_params=_cparams(("arbitrary",)),
        name="outproj",
    )(*args)


def _ffn_kernel(*refs, with_comb, final_norm):
    h_ref, x1_ref, mod_ref = refs[:3]
    k = 3
    comb_ref = fin_ref = None
    if with_comb:
        comb_ref = refs[k]
        k += 1
    wg_ref, wu_ref, wd_ref = refs[k:k + 3]
    k += 3
    if final_norm:
        fin_ref = refs[k]
        k += 1
    o_ref, acc = refs[k:]
    e = pl.program_id(1)
    f = pl.program_id(2)

    @pl.when((e == 0) & (f == 0))
    def _():
        acc[...] = jnp.zeros(acc.shape, F32)

    h = h_ref[...]
    a = _dot(h, wg_ref[0])
    u = _dot(h, wu_ref[0])
    act = _silu(a) * u
    if with_comb:
        comb = comb_ref[...]
        lane = lax.broadcasted_iota(jnp.int32, comb.shape, 1)
        act = act * jnp.sum(jnp.where(lane == e, comb, 0.0), axis=1, keepdims=True)
    acc[...] += _dot(act.astype(BF16), wd_ref[0])

    @pl.when((e == pl.num_programs(1) - 1) & (f == pl.num_programs(2) - 1))
    def _():
        x2 = x1_ref[...] + mod_ref[0][5:6] * acc[...]
        if final_norm:
            x2 = _rms(x2, fin_ref[...])
        o_ref[...] = x2


def _ffn(h2, x1, mods, row_fn, comb, wg, wu, wd, fin_g, tm, tf):
    t, d = x1.shape
    n_e, _, ff = wg.shape
    with_comb = comb is not None
    final_norm = fin_g is not None
    in_specs = [
        pl.BlockSpec((tm, d), lambda i, e, f: (i, 0)),
        pl.BlockSpec((tm, d), lambda i, e, f: (i, 0)),
        pl.BlockSpec((1, 6, d), lambda i, e, f: (row_fn(i), 0, 0)),
    ]
    args = [h2, x1, mods]
    if with_comb:
        in_specs.append(pl.BlockSpec((tm, LANES), lambda i, e, f: (i, 0)))
        args.append(comb)
    in_specs += [
        pl.BlockSpec((1, d, tf), lambda i, e, f: (e, 0, f)),
        pl.BlockSpec((1, d, tf), lambda i, e, f: (e, 0, f)),
        pl.BlockSpec((1, tf, d), lambda i, e, f: (e, f, 0)),
    ]
    args += [wg, wu, wd]
    if final_norm:
        in_specs.append(pl.BlockSpec((1, d), lambda i, e, f: (0, 0)))
        args.append(fin_g)
    return pl.pallas_call(
        functools.partial(_ffn_kernel, with_comb=with_comb, final_norm=final_norm),
        grid=(t // tm, n_e, ff // tf),
        in_specs=in_specs,
        out_specs=pl.BlockSpec((tm, d), lambda i, e, f: (i, 0)),
        out_shape=jax.ShapeDtypeStruct((t, d), F32),
        scratch_shapes=[pltpu.VMEM((tm, d), F32)],
        compiler_params=_cparams(("arbitrary", "arbitrary", "arbitrary")),
        name="ffn",
    )(*args)


def _sc_row_gather(table, idx):
    _, w = table.shape
    b = idx.shape[0]
    n_workers = SC_CORES * SC_SUBCORES
    assert b % (n_workers * SC_GATHER_ROWS) == 0, (b, n_workers, SC_GATHER_ROWS)
    b_per_w = b // n_workers
    n_chunks = b_per_w // SC_GATHER_ROWS
    mesh = plsc.VectorSubcoreMesh(core_axis_name="c", subcore_axis_name="s",
                                  num_cores=SC_CORES, num_subcores=SC_SUBCORES)

    def body(table_hbm, idx_hbm, out_hbm, idx_v, rows_v, sem):
        wid = lax.axis_index("s") * SC_CORES + lax.axis_index("c")
        base = wid * b_per_w

        @pl.loop(0, n_chunks)
        def _(ci):
            off = base + ci * SC_GATHER_ROWS
            pltpu.sync_copy(idx_hbm.at[pl.ds(off, SC_GATHER_ROWS)], idx_v)
            pltpu.async_copy(table_hbm.at[idx_v], rows_v, sem).wait()
            pltpu.sync_copy(rows_v, out_hbm.at[pl.ds(off, SC_GATHER_ROWS)])

    return pl.kernel(
        body,
        out_type=jax.ShapeDtypeStruct((b, w), F32),
        mesh=mesh,
        scratch_types=[pltpu.VMEM((SC_GATHER_ROWS,), jnp.int32),
                       pltpu.VMEM((SC_GATHER_ROWS, w), F32),
                       pltpu.SemaphoreType.DMA],
        name="sc_row_gather",
    )(table, idx)


def _sc_row_scatter2(table, pos, n_out):
    t, w = table.shape
    n_workers = SC_CORES * SC_SUBCORES
    assert t % (n_workers * SC_GATHER_ROWS) == 0, (t, n_workers, SC_GATHER_ROWS)
    t_per_w = t // n_workers
    n_chunks = t_per_w // SC_GATHER_ROWS
    mesh = plsc.VectorSubcoreMesh(core_axis_name="c", subcore_axis_name="s",
                                  num_cores=SC_CORES, num_subcores=SC_SUBCORES)

    def body(table_hbm, pos_hbm, out_hbm, idx0_v, idx1_v, rows_v, sem):
        wid = lax.axis_index("s") * SC_CORES + lax.axis_index("c")
        base = wid * t_per_w

        @pl.loop(0, n_chunks)
        def _(ci):
            off = base + ci * SC_GATHER_ROWS
            pltpu.sync_copy(pos_hbm.at[0, pl.ds(off, SC_GATHER_ROWS)], idx0_v)
            pltpu.sync_copy(pos_hbm.at[1, pl.ds(off, SC_GATHER_ROWS)], idx1_v)
            pltpu.sync_copy(table_hbm.at[pl.ds(off, SC_GATHER_ROWS)], rows_v)
            first = pltpu.async_copy(rows_v, out_hbm.at[idx0_v], sem)
            second = pltpu.async_copy(rows_v, out_hbm.at[idx1_v], sem)
            first.wait()
            second.wait()

    return pl.kernel(
        body,
        out_type=jax.ShapeDtypeStruct((n_out, w), F32),
        mesh=mesh,
        scratch_types=[pltpu.VMEM((SC_GATHER_ROWS,), jnp.int32),
                       pltpu.VMEM((SC_GATHER_ROWS,), jnp.int32),
                       pltpu.VMEM((SC_GATHER_ROWS, w), F32),
                       pltpu.SemaphoreType.DMA],
        name="sc_row_scatter",
    )(table, pos)


def _moe_plan_kernel(route_ref, tri_ref, utri_ref, pos_ref, cnt_ref, run_scr, off_scr):
    phase = pl.program_id(0)
    blk = pl.program_id(1)
    route = route_ref[...]
    lane = lax.broadcasted_iota(jnp.int32, route.shape, 1).astype(F32)
    oh1 = jnp.where(lane == route[:, ROUTE_E1:ROUTE_E1 + 1], 1.0, 0.0)
    oh2 = jnp.where(lane == route[:, ROUTE_E2:ROUTE_E2 + 1], 1.0, 0.0)
    oh = oh1 + oh2

    @pl.when(blk == 0)
    def _():
        run_scr[...] = jnp.zeros(run_scr.shape, F32)

    @pl.when(phase == 0)
    def _():
        run_scr[...] += jnp.sum(oh, axis=0, keepdims=True)

        @pl.when(blk == pl.num_programs(1) - 1)
        def _():
            counts = run_scr[...]
            cnt_ref[...] = counts
            tiles_per = jnp.floor((counts + (MOE_TILE - 1.0)) * (1.0 / MOE_TILE))
            tile_end = _dot(jnp.broadcast_to(tiles_per, (8, LANES)).astype(BF16), utri_ref[...])[0:1]
            off_scr[...] = (tile_end - tiles_per) * float(MOE_TILE)

    @pl.when(phase == 1)
    def _():
        incl = _dot(tri_ref[...], oh.astype(BF16))
        before = incl - oh + run_scr[...] + off_scr[...]
        p1 = jnp.sum(before * oh1, axis=1, keepdims=True)
        p2 = jnp.sum(before * oh2, axis=1, keepdims=True)
        pos = jnp.where(lane == 0.0, p1, 0.0) + jnp.where(lane == 1.0, p2, 0.0)
        pos_ref[...] = pos.astype(jnp.int32)
        run_scr[...] += incl[incl.shape[0] - 1:, :]


def _moe_plan(route, n_tiles, tm):
    t = route.shape[0]
    tri = jnp.asarray(np.tril(np.ones((tm, tm), np.float32)), dtype=BF16)
    utri = jnp.asarray(np.triu(np.ones((LANES, LANES), np.float32)), dtype=BF16)
    pos, counts = pl.pallas_call(
        _moe_plan_kernel,
        grid=(2, t // tm),
        in_specs=[
            pl.BlockSpec((tm, LANES), lambda p, i: (i, 0)),
            pl.BlockSpec((tm, tm), lambda p, i: (0, 0)),
            pl.BlockSpec((LANES, LANES), lambda p, i: (0, 0)),
        ],
        out_specs=[
            pl.BlockSpec((tm, LANES), lambda p, i: (i * p, 0)),
            pl.BlockSpec((1, LANES), lambda p, i: (0, 0)),
        ],
        out_shape=[jax.ShapeDtypeStruct((t, LANES), jnp.int32),
                   jax.ShapeDtypeStruct((1, LANES), F32)],
        scratch_shapes=[pltpu.VMEM((1, LANES), F32), pltpu.VMEM((1, LANES), F32)],
        compiler_params=_cparams(("arbitrary", "arbitrary")),
        name="moe_plan",
    )(route, tri, utri)
    counts = counts[0, :N_EXPERTS].astype(jnp.int32)
    tile_end = jnp.cumsum((counts + MOE_TILE - 1) // MOE_TILE)
    n_used = tile_end[-1]
    tile_ids = jnp.minimum(jnp.arange(n_tiles, dtype=jnp.int32), n_used - 1)
    tile_expert = jnp.sum((tile_end[None, :] <= tile_ids[:, None]).astype(jnp.int32), axis=1)
    return pos[:, :2].T, tile_expert, n_used.reshape(1)


def _moe_ffn_kernel(te_ref, nused_ref, xs_ref, wg_ref, wu_ref, wd_ref, o_ref, acc, h_scr, *, splits):
    i = pl.program_id(0)
    f = pl.program_id(1)
    last_f = pl.num_programs(1) - 1
    used = i < nused_ref[0]

    @pl.when(used)
    def _():
        @pl.when(f == 0)
        def _():
            acc[...] = jnp.zeros(acc.shape, F32)
            h_scr[...] = _unpack_bf16_pairs(xs_ref[...]).astype(BF16)

        h = h_scr[...]
        for lo, hi in splits:
            a = _dot(h, wg_ref[0, :, lo:hi].astype(BF16))
            u = _dot(h, wu_ref[0, :, lo:hi].astype(BF16))
            acc[...] += _dot((_silu(a) * u).astype(BF16), wd_ref[0, lo:hi, :].astype(BF16))

        @pl.when(f == last_f)
        def _():
            o_ref[...] = _pack_bf16_pairs(acc[...])

    @pl.when(jnp.logical_not(used) & (f == last_f))
    def _():
        o_ref[...] = jnp.zeros(o_ref.shape, F32)


def _moe_ffn(xs, tile_expert, n_used, wg, wu, wd):
    rows, half = xs.shape
    d = 2 * half
    ff = wg.shape[2]
    if ff % MOE_FF_TILE == 0:
        tf, splits = MOE_FF_TILE, MOE_FF_SPLITS
    else:
        tf, splits = ff, ((0, ff),)
    n_tiles = rows // MOE_TILE
    grid_spec = pltpu.PrefetchScalarGridSpec(
        num_scalar_prefetch=2,
        grid=(n_tiles, ff // tf),
        in_specs=[
            pl.BlockSpec((MOE_TILE, half), lambda i, f, te, nu: (i, 0)),
            pl.BlockSpec((1, d, tf), lambda i, f, te, nu: (te[i], 0, f)),
            pl.BlockSpec((1, d, tf), lambda i, f, te, nu: (te[i], 0, f)),
            pl.BlockSpec((1, tf, d), lambda i, f, te, nu: (te[i], f, 0)),
        ],
        out_specs=pl.BlockSpec((MOE_TILE, half), lambda i, f, te, nu: (i, 0)),
        scratch_shapes=[pltpu.VMEM((MOE_TILE, d), F32), pltpu.VMEM((MOE_TILE, d), BF16)],
    )
    return pl.pallas_call(
        functools.partial(_moe_ffn_kernel, splits=splits),
        grid_spec=grid_spec,
        out_shape=jax.ShapeDtypeStruct((rows, half), F32),
        compiler_params=_cparams(("arbitrary", "arbitrary")),
        name="moe_ffn",
    )(tile_expert, n_used, xs, wg, wu, wd)


def _combine_kernel(*refs, final_norm):
    x1_ref, y0_ref, y1_ref, route_ref, mod_ref = refs[:5]
    fin_ref = refs[5] if final_norm else None
    o_ref = refs[-1]
    route = route_ref[...]
    w1 = route[:, ROUTE_W1:ROUTE_W1 + 1]
    w2 = route[:, ROUTE_W2:ROUTE_W2 + 1]
    y = w1 * _unpack_bf16_pairs(y0_ref[...]) + w2 * _unpack_bf16_pairs(y1_ref[...])
    x2 = x1_ref[...] + mod_ref[0][5:6] * y
    if final_norm:
        x2 = _rms(x2, fin_ref[...])
    o_ref[...] = x2


def _combine(x1, yg, route, mods, row_fn, fin_g, tm):
    t, d = x1.shape
    nb = t // tm
    final_norm = fin_g is not None
    in_specs = [
        pl.BlockSpec((tm, d), lambda i: (i, 0)),
        pl.BlockSpec((tm, d // 2), lambda i: (i, 0)),
        pl.BlockSpec((tm, d // 2), lambda i: (i + nb, 0)),
        pl.BlockSpec((tm, LANES), lambda i: (i, 0)),
        pl.BlockSpec((1, 6, d), lambda i: (row_fn(i), 0, 0)),
    ]
    args = [x1, yg, yg, route, mods]
    if final_norm:
        in_specs.append(pl.BlockSpec((1, d), lambda i: (0, 0)))
        args.append(fin_g)
    return pl.pallas_call(
        functools.partial(_combine_kernel, final_norm=final_norm),
        grid=(nb,),
        in_specs=in_specs,
        out_specs=pl.BlockSpec((tm, d), lambda i: (i, 0)),
        out_shape=jax.ShapeDtypeStruct((t, d), F32),
        compiler_params=_cparams(("arbitrary",)),
        name="moe_combine",
    )(*args)


def _moe(h2, x1, route, mods, row_fn, wg, wu, wd, fin_g, tm):
    t = h2.shape[0]
    n_tiles = -(-2 * t // MOE_TILE) + N_EXPERTS
    pos, tile_expert, n_used = _moe_plan(route, n_tiles, tm)
    xs = _sc_row_scatter2(h2, pos, n_tiles * MOE_TILE)
    ys = _moe_ffn(xs, tile_expert, n_used, wg, wu, wd)
    yg = _sc_row_gather(ys, pos.reshape(-1))
    return _combine(x1, yg, route, mods, row_fn, fin_g, tm)


def _rope_partner():
    j = np.arange(MLA_ROPE)
    return np.where((j % 32) < 16, j + 16, j - 16)


def _prep_in_weight(w):
    d = w.shape[0]
    cols = [w[:, 0:1024], w[:, 1056:1568], w[:, 1568:1824], w[:, 1824:1952], w[:, 1952:2016],
            w[:, 1024:1056], jnp.zeros((d, P_WIDTH - 2016), w.dtype)]
    return jnp.concatenate(cols, axis=1).astype(BF16)


def _prep_gate_weight(w_g2, b_g2):
    ws = []
    for z, off in ((0, MISC_GF), (1, MISC_GB)):
        ws.append(jnp.zeros((LANES, GLA_QK_W), F32).at[off:off + GLA_GATE_RANK].set(w_g2[z]))
    return jnp.stack(ws).astype(BF16), b_g2.reshape(2, 1, GLA_QK_W)


def _prep_mla_weights(w_uq, w_ukv):
    partner = _rope_partner()
    wq = w_uq.reshape(MLA_Q_RANK, MLA_HEADS, MLA_QK)
    wqn = wq[:, :, :MLA_NOPE].reshape(MLA_Q_RANK, MLA_HEADS * MLA_NOPE)
    rope = wq[:, :, MLA_NOPE:]
    pad = jnp.zeros((MLA_Q_RANK, MLA_HEADS, LANES - MLA_ROPE), w_uq.dtype)
    wqr = jnp.concatenate([rope, pad], axis=2).reshape(MLA_Q_RANK, MLA_HEADS * LANES)
    wqs = jnp.concatenate([rope[:, :, partner], pad], axis=2).reshape(MLA_Q_RANK, MLA_HEADS * LANES)
    wkv = w_ukv.reshape(MLA_KV_RANK, MLA_HEADS, MLA_NOPE + MLA_V)
    wknt = wkv[:, :, :MLA_NOPE].reshape(MLA_KV_RANK, MLA_HEADS * MLA_NOPE).T
    wv = wkv[:, :, MLA_NOPE:].reshape(MLA_KV_RANK, MLA_HEADS * MLA_V)
    perm = np.zeros((LANES, LANES), np.float32)
    perm[partner, np.arange(MLA_ROPE)] = 1.0
    eye = np.eye(MLA_ROPE, LANES, dtype=np.float32)
    return (wqn.astype(BF16), wqr.astype(BF16), wqs.astype(BF16), wknt.astype(BF16),
            wv.astype(BF16), jnp.asarray(perm, BF16), jnp.asarray(eye, BF16))


def _rope_tables(n_tok):
    rows = n_tok // GRID_W
    row = np.repeat(np.arange(rows, dtype=np.float32), GRID_W)
    col = np.tile(np.arange(GRID_W, dtype=np.float32), rows)
    nfreq = MLA_ROPE // 4
    inv = np.float32(ROPE_BASE) ** (-np.arange(nfreq, dtype=np.float32) / np.float32(nfreq))
    ar = (row[:, None] * inv).astype(np.float32)
    ac = (col[:, None] * inv).astype(np.float32)
    zero = np.zeros((n_tok, LANES - MLA_ROPE), np.float32)
    cos = np.concatenate([np.cos(ar), np.cos(ar), np.cos(ac), np.cos(ac), zero], axis=1)
    sin = np.concatenate([-np.sin(ar), np.sin(ar), -np.sin(ac), np.sin(ac), zero], axis=1)
    return jnp.asarray(cos, F32), jnp.asarray(sin, F32)


def _identity_tables(n_tok):
    cos = jnp.concatenate([jnp.ones((n_tok, MLA_ROPE), F32),
                           jnp.zeros((n_tok, LANES - MLA_ROPE), F32)], axis=1)
    return cos, jnp.zeros((n_tok, LANES), F32)


def _pick_tile(n, pref):
    t = min(n, pref)
    while n % t:
        t //= 2
    return t


def _pick_ff_tile(ff):
    best = LANES
    for m in range(1, ff // LANES + 1):
        if ff % (m * LANES) == 0 and m * LANES <= 1408:
            best = m * LANES
    return best


@jax.jit
def _forward(x, c, ctx, c_ctx, w_mod, b_mod, ln1_g, ln2_g, w_in, w_gla_g2, b_gla_g2, gla_norm_g,
             mla_q_norm_g, w_uq, mla_kv_norm_g, w_ukv, w_out, ffn_w_gate, ffn_w_up, ffn_w_down,
             router_w, exp_w_gate, exp_w_up, exp_w_down, final_norm_g):
    batch, seq, d = x.shape
    n_ctx = ctx.shape[1]
    depth = w_mod.shape[0]

    cvec = jnp.zeros((8, d), F32).at[:batch].set(c).at[batch].set(c_ctx)
    mods_all = _modulation(cvec, w_mod, b_mod).reshape(depth, 8, 6, d)

    xl = x.reshape(batch * seq, d)
    xc = ctx.reshape(batch * n_ctx, d)

    tm_l = _pick_tile(seq, 512)
    tm_c = _pick_tile(n_ctx, 256)
    tk_l = _pick_tile(seq, 1024)
    cb_l = _pick_tile(seq, 256)
    cb_c = _pick_tile(n_ctx, 256)
    row_l = lambda tm: (lambda i: i // (seq // tm))
    row_c = lambda i: batch

    rope_l = _rope_tables(seq)
    rope_c = _identity_tables(tm_c)
    zero_state = jnp.zeros((2, batch, GLA_QK_W, GLA_DV), F32)

    for i in range(depth):
        need_ctx = i < depth - 1
        last = i == depth - 1
        mods = mods_all[i]
        ln1 = ln1_g[i].reshape(1, d)
        ln2 = ln2_g[i].reshape(1, d)
        w_in_r = _prep_in_weight(w_in[i])
        gates = _prep_gate_weight(w_gla_g2[i], b_gla_g2[i])
        mla_w = _prep_mla_weights(w_uq[i], w_ukv[i])
        qg = mla_q_norm_g[i].reshape(1, MLA_Q_RANK)
        kvg = mla_kv_norm_g[i].reshape(1, MLA_KV_RANK)
        gg = gla_norm_g[i].reshape(1, GLA_DV)
        wo = w_out[i].astype(BF16)

        p_l = _inproj(xl, mods, row_l(tm_l), ln1, w_in_r, tm_l)
        p_c = _inproj(xc, mods, row_c, ln1, w_in_r, tm_c)

        oc_f, oc_b, s_ctx = _gla(p_c, *gates, zero_state, batch=batch, cb=cb_c)
        ol_f, ol_b, _ = _gla(p_l, *gates, s_ctx, batch=batch, cb=cb_l)

        q_l, kt_l, v_l = _mlaprep(p_l, *rope_l, qg, kvg, mla_w, batch=batch, tm=tk_l)
        q_c, kt_c, v_c = _mlaprep(p_c, *rope_c, qg, kvg, mla_w, batch=batch, tm=tm_c)
        m_l = _attention_pipelined(q_l, kt_l, v_l, kt_c, v_c, tq=tk_l, n_sub=max(1, tk_l // 512))
        m_l = m_l.reshape(batch * seq, MLA_V_W)

        if i % 2 == 0:
            j = i // 2
            router = None
            wg = ffn_w_gate[j].astype(BF16)
            wu = ffn_w_up[j].astype(BF16)
            wd = ffn_w_down[j].astype(BF16)
        else:
            j = i // 2
            rw = jnp.zeros((d, LANES), F32).at[:, :N_EXPERTS].set(router_w[j])
            rw_hi = rw.astype(BF16)
            router = (rw_hi, (rw - rw_hi.astype(F32)).astype(BF16))
            wg, wu, wd = exp_w_gate[j], exp_w_up[j], exp_w_down[j]
        fin = final_norm_g.reshape(1, d) if last else None

        if router is None:
            xl = _outproj_ffn(xl, ol_f, ol_b, p_l, m_l, mods, row_l(tm_l), gg, wo, ln2,
                              wg, wu, wd, fin, tm_l)
        else:
            outs = _outproj(xl, ol_f, ol_b, p_l, m_l, mods, row_l(tm_l), gg, wo, ln2, router, tm_l)
            xl = _moe(outs[1], outs[0], outs[2], mods, row_l(tm_l), wg, wu, wd, fin, tm_l)

        if need_ctx:
            m_c = _attention(q_c, [(kt_c, v_c)], tq=tm_c, n_sub=1).reshape(batch * n_ctx, MLA_V_W)
            if router is None:
                xc = _outproj_ffn(xc, oc_f, oc_b, p_c, m_c, mods, row_c, gg, wo, ln2,
                                  wg, wu, wd, None, tm_c)
            else:
                outs_c = _outproj(xc, oc_f, oc_b, p_c, m_c, mods, row_c, gg, wo, ln2, router, tm_c)
                tm_fc = _pick_tile(batch * n_ctx, 512)
                r_c = outs_c[2]
                lane = jnp.arange(LANES, dtype=F32)[None, :]
                comb_c = (jnp.where(lane == r_c[:, ROUTE_E1:ROUTE_E1 + 1], r_c[:, ROUTE_W1:ROUTE_W1 + 1], 0.0)
                          + jnp.where(lane == r_c[:, ROUTE_E2:ROUTE_E2 + 1], r_c[:, ROUTE_W2:ROUTE_W2 + 1], 0.0))
                bits = lax.bitcast_convert_type(outs_c[1], jnp.uint32)
                h2_c = jnp.concatenate([lax.bitcast_convert_type(bits << 16, F32),
                                        lax.bitcast_convert_type(bits & jnp.uint32(0xFFFF0000), F32)], axis=1)
                xc = _ffn(h2_c.astype(BF16), outs_c[0], mods, row_c, comb_c,
                          wg.astype(BF16), wu.astype(BF16), wd.astype(BF16), None, tm_fc,
                          _pick_ff_tile(wg.shape[2]))

    return xl.reshape(batch, seq, d)


def kernel(x, c, ctx, c_ctx, w_mod, b_mod, ln1_g, ln2_g, w_in, w_gla_g2, b_gla_g2, gla_norm_g,
           mla_q_norm_g, w_uq, mla_kv_norm_g, w_ukv, w_out, ffn_w_gate, ffn_w_up, ffn_w_down,
           router_w, exp_w_gate, exp_w_up, exp_w_down, final_norm_g):
    return _forward(x, c, ctx, c_ctx, w_mod, b_mod, ln1_g, ln2_g, w_in, w_gla_g2, b_gla_g2,
                    gla_norm_g, mla_q_norm_g, w_uq, mla_kv_norm_g, w_ukv, w_out, ffn_w_gate,
                    ffn_w_up, ffn_w_down, router_w, exp_w_gate, exp_w_up, exp_w_down, final_norm_g)
```

```python
import functools

import numpy as np
import jax
import jax.numpy as jnp
from jax import lax
from jax.experimental import pallas as pl
from jax.experimental.pallas import tpu as pltpu
from jax.experimental.pallas import tpu_sc as plsc

F32 = jnp.float32
BF16 = jnp.bfloat16

D_MODEL = 1024
EPS = 1e-6
GRID_W = 64

GLA_HEADS = 4
GLA_DK = 64
GLA_DV = 128
GLA_GATE_RANK = 16
GLA_GATE_NORM = 16.0
GLA_CHUNK = 64
GLA_QK_W = GLA_HEADS * GLA_DK
GLA_V_W = GLA_HEADS * GLA_DV
GLA_EXP_CLAMP = 80.0

MLA_HEADS = 4
MLA_NOPE = 128
MLA_ROPE = 64
MLA_V = 128
MLA_QK = MLA_NOPE + MLA_ROPE
MLA_Q_RANK = 256
MLA_KV_RANK = 128
MLA_SCALE = MLA_QK ** -0.5
MLA_Q_SCALE = MLA_SCALE * 1.4426950408889634
MLA_V_W = MLA_HEADS * MLA_V
MLA_V_EXT = 2 * MLA_V
ROPE_BASE = 10000.0

N_EXPERTS = 8
LANES = 128
ROUTE_E1, ROUTE_E2, ROUTE_W1, ROUTE_W2 = 0, 1, 2, 3

SC_CORES = 2
SC_SUBCORES = 16
SC_GATHER_ROWS = 64
MOE_TILE = 1024
MOE_FF_TILE = 512
MOE_FF_SPLITS = ((0, 256), (256, 512))

P_Q, P_K, P_V, P_R, P_CQ, P_CKV, P_MISC = 0, 256, 512, 1024, 1536, 1792, 1920
P_WIDTH = 2048
MISC_KR, MISC_GF, MISC_GB = 0, 64, 80

VMEM_LIMIT = 56 * 1024 * 1024


def _cparams(sem):
    return pltpu.CompilerParams(dimension_semantics=sem, vmem_limit_bytes=VMEM_LIMIT)


def _rms(x, g):
    return x * lax.rsqrt(jnp.mean(x * x, axis=-1, keepdims=True) + EPS) * g


def _silu(x):
    return x / (1.0 + jnp.exp(-x))


def _dot(a, b):
    return jnp.dot(a, b, preferred_element_type=F32)


def _dot_nt(a, b):
    return lax.dot_general(a, b, (((1,), (1,)), ((), ())), preferred_element_type=F32)


def _dot_tn(a, b):
    return lax.dot_general(a, b, (((0,), (0,)), ((), ())), preferred_element_type=F32)


def _pack_bf16_pairs(x):
    w = x.shape[1] // 2
    words = pltpu.pack_elementwise([x[:, :w], x[:, w:]], packed_dtype=BF16)
    return lax.bitcast_convert_type(words, F32)


def _unpack_bf16_pairs(p):
    words = lax.bitcast_convert_type(p, jnp.int32)
    lo = pltpu.unpack_elementwise(words, index=0, packed_dtype=BF16, unpacked_dtype=F32)
    hi = pltpu.unpack_elementwise(words, index=1, packed_dtype=BF16, unpacked_dtype=F32)
    return jnp.concatenate([lo, hi], axis=1)


def _mod_kernel(c_ref, w_ref, b_ref, o_ref):
    s = _silu(c_ref[...]).astype(BF16)
    o_ref[0] = _dot(s, w_ref[0].astype(BF16)) + b_ref[0]


def _modulation(cvec, w_mod, b_mod):
    depth, d, n = w_mod.shape
    tn = 1536
    return pl.pallas_call(
        _mod_kernel,
        grid=(depth, n // tn),
        in_specs=[
            pl.BlockSpec((8, d), lambda l, j: (0, 0)),
            pl.BlockSpec((1, d, tn), lambda l, j: (l, 0, j)),
            pl.BlockSpec((1, 1, tn), lambda l, j: (l, 0, j)),
        ],
        out_specs=pl.BlockSpec((1, 8, tn), lambda l, j: (l, 0, j)),
        out_shape=jax.ShapeDtypeStruct((depth, 8, n), F32),
        compiler_params=_cparams(("arbitrary", "arbitrary")),
        name="modulation",
    )(cvec, w_mod, b_mod.reshape(depth, 1, n))


def _inproj_kernel(x_ref, mod_ref, g_ref, w_ref, o_ref):
    m = mod_ref[0]
    tm = x_ref.shape[0]
    n_groups = 2 if tm % 32 == 0 else 1
    for g in range(n_groups):
        rows = slice(g * tm // n_groups, (g + 1) * tm // n_groups)
        h = _rms(x_ref[rows, :], g_ref[...]) * (1.0 + m[1:2]) + m[0:1]
        o_ref[rows, :] = _dot(h.astype(BF16), w_ref[...]).astype(BF16)


def _inproj(x, mods, row_fn, ln_g, w, tm):
    t, d = x.shape
    return pl.pallas_call(
        _inproj_kernel,
        grid=(t // tm,),
        in_specs=[
            pl.BlockSpec((tm, d), lambda i: (i, 0)),
            pl.BlockSpec((1, 6, d), lambda i: (row_fn(i), 0, 0)),
            pl.BlockSpec((1, d), lambda i: (0, 0)),
            pl.BlockSpec((d, P_WIDTH), lambda i: (0, 0)),
        ],
        out_specs=pl.BlockSpec((tm, P_WIDTH), lambda i: (i, 0)),
        out_shape=jax.ShapeDtypeStruct((t, P_WIDTH), BF16),
        compiler_params=_cparams(("arbitrary",)),
        name="inproj",
    )(x, mods, ln_g, w)


def _gla_direction(q_ref, k_ref, v_ref, misc_ref, wg, bg, tri, s_scr, o_ref, *, reverse, n_chunks):
    c_len = GLA_CHUNK
    pre = _dot(misc_ref[...], wg) + bg
    g = (jnp.minimum(pre, 0.0) - jnp.log(1.0 + jnp.exp(-jnp.abs(pre)))) * (1.0 / GLA_GATE_NORM)
    g_hi = g.astype(BF16)
    g_lo = (g - g_hi.astype(F32)).astype(BF16)
    cum = _dot(tri, g_hi) + _dot(tri, g_lo)
    tot_rows = jnp.concatenate(
        [cum[c * c_len:c * c_len + 1] if reverse else cum[(c + 1) * c_len - 1:(c + 1) * c_len]
         for c in range(n_chunks)] + [jnp.zeros((8 - n_chunks, GLA_QK_W), F32)], axis=0)
    t_hi = tot_rows.astype(BF16)
    t_lo = (tot_rows - t_hi.astype(F32)).astype(BF16)
    eye = (lax.broadcasted_iota(jnp.int32, (GLA_QK_W, GLA_QK_W), 0)
           == lax.broadcasted_iota(jnp.int32, (GLA_QK_W, GLA_QK_W), 1))
    eye = jnp.where(eye, 1.0, 0.0).astype(BF16)
    tot_cols = _dot_nt(eye, t_hi) + _dot_nt(eye, t_lo)

    lane = lax.broadcasted_iota(jnp.int32, (c_len, GLA_QK_W), 1)
    head_masks = [(lane >= h * GLA_DK) & (lane < (h + 1) * GLA_DK) for h in range(GLA_HEADS)]
    row = lax.broadcasted_iota(jnp.int32, (GLA_HEADS * c_len, c_len), 0) % c_len
    col = lax.broadcasted_iota(jnp.int32, (GLA_HEADS * c_len, c_len), 1)
    pair_mask = (col >= row) if reverse else (col <= row)

    def stack_heads(a):
        return jnp.concatenate([jnp.where(mk, a, 0.0) for mk in head_masks], axis=0).astype(BF16)

    def step(c):
        sl = slice(c * c_len, (c + 1) * c_len)
        xc = cum[sl]
        tot = tot_rows[c:c + 1]
        ref = xc[c_len // 2:c_len // 2 + 1]
        qc = q_ref[sl, :].astype(F32) * (GLA_DK ** -0.5)
        kc = k_ref[sl, :].astype(F32)
        vc = v_ref[sl, :]
        q_mid = qc * jnp.exp(jnp.minimum(xc - ref, GLA_EXP_CLAMP))
        k_mid = (kc * jnp.exp(jnp.minimum(ref - xc, GLA_EXP_CLAMP))).astype(BF16)
        q_dec = qc * jnp.exp(xc)
        k_dec = kc * jnp.exp(tot - xc)

        attn = _dot_nt(stack_heads(q_mid), k_mid)
        attn = jnp.where(pair_mask, attn, 0.0).astype(BF16)
        s_prev = s_scr[...]
        o_inter = _dot(stack_heads(q_dec), s_prev.astype(BF16))
        kv = []
        for h in range(GLA_HEADS):
            rs = slice(h * c_len, (h + 1) * c_len)
            vs = slice(h * GLA_DV, (h + 1) * GLA_DV)
            o_h = o_inter[rs] + _dot(attn[rs], vc[:, vs])
            o_ref[sl, vs] = o_h.astype(BF16)
            kv.append(_dot_tn(k_dec[:, h * GLA_DK:(h + 1) * GLA_DK].astype(BF16), vc[:, vs]))
        s_scr[...] = s_prev * jnp.exp(tot_cols[:, c:c + 1]) + jnp.concatenate(kv, axis=0)

    return step


def _gla_kernel(qf_ref, kf_ref, vf_ref, mf_ref, qb_ref, kb_ref, vb_ref, mb_ref,
                wg_ref, bg_ref, tri_ref, s0_ref, of_ref, ob_ref, sfin_ref, sf_scr, sb_scr, *, n_chunks):
    blk = pl.program_id(1)

    @pl.when(blk == 0)
    def _():
        sf_scr[...] = s0_ref[0, 0]
        sb_scr[...] = s0_ref[1, 0]

    fwd = _gla_direction(qf_ref, kf_ref, vf_ref, mf_ref, wg_ref[0], bg_ref[0], tri_ref[0],
                         sf_scr, of_ref, reverse=False, n_chunks=n_chunks)
    bwd = _gla_direction(qb_ref, kb_ref, vb_ref, mb_ref, wg_ref[1], bg_ref[1], tri_ref[1],
                         sb_scr, ob_ref, reverse=True, n_chunks=n_chunks)
    for c in range(n_chunks):
        fwd(c)
        bwd(n_chunks - 1 - c)

    @pl.when(blk == pl.num_programs(1) - 1)
    def _():
        sfin_ref[0, 0] = sf_scr[...]
        sfin_ref[1, 0] = sb_scr[...]


def _block_diag_tri(n_chunks):
    c = GLA_CHUNK
    eye = np.eye(n_chunks, dtype=np.float32)
    lower = np.kron(eye, np.tril(np.ones((c, c), np.float32)))
    upper = np.kron(eye, np.triu(np.ones((c, c), np.float32)))
    return jnp.asarray(np.stack([lower, upper]), dtype=BF16)


def _gla(p, wg, bg, s0, *, batch, cb):
    t_all = p.shape[0]
    nblk = t_all // batch // cb
    n_chunks = cb // GLA_CHUNK
    assert n_chunks <= 8

    fw = lambda b, i: b * nblk + i
    bw = lambda b, i: b * nblk + (nblk - 1 - i)
    full = lambda a: pl.BlockSpec(a.shape, lambda b, i: (0,) * a.ndim)
    tri = _block_diag_tri(n_chunks)

    def token_specs(tok):
        return [
            pl.BlockSpec((cb, GLA_QK_W), lambda b, i: (tok(b, i), P_Q // GLA_QK_W)),
            pl.BlockSpec((cb, GLA_QK_W), lambda b, i: (tok(b, i), P_K // GLA_QK_W)),
            pl.BlockSpec((cb, GLA_V_W), lambda b, i: (tok(b, i), P_V // GLA_V_W)),
            pl.BlockSpec((cb, LANES), lambda b, i: (tok(b, i), P_MISC // LANES)),
        ]

    state_spec = pl.BlockSpec((2, 1, GLA_QK_W, GLA_DV), lambda b, i: (0, b, 0, 0))
    return pl.pallas_call(
        functools.partial(_gla_kernel, n_chunks=n_chunks),
        grid=(batch, nblk),
        in_specs=token_specs(fw) + token_specs(bw) + [full(wg), full(bg), full(tri), state_spec],
        out_specs=[
            pl.BlockSpec((cb, GLA_V_W), lambda b, i: (fw(b, i), 0)),
            pl.BlockSpec((cb, GLA_V_W), lambda b, i: (bw(b, i), 0)),
            state_spec,
        ],
        out_shape=[
            jax.ShapeDtypeStruct((t_all, GLA_V_W), BF16),
            jax.ShapeDtypeStruct((t_all, GLA_V_W), BF16),
            jax.ShapeDtypeStruct((2, batch, GLA_QK_W, GLA_DV), F32),
        ],
        scratch_shapes=[pltpu.VMEM((GLA_QK_W, GLA_DV), F32), pltpu.VMEM((GLA_QK_W, GLA_DV), F32)],
        compiler_params=_cparams(("arbitrary", "arbitrary")),
        name="gla",
    )(p, p, p, p, p, p, p, p, wg, bg, tri, s0)


def _mlaprep_kernel(cq_ref, ckv_ref, misc_ref, cos_ref, sin_ref, qg_ref, kvg_ref,
                    wqn_ref, wqr_ref, wqs_ref, wknt_ref, wv_ref, perm_ref, eye_ref,
                    q_ref, kt_ref, v_ref):
    cos = cos_ref[...]
    sin = sin_ref[...]
    cqn = _rms(cq_ref[...].astype(F32), qg_ref[...]).astype(BF16)
    qn = _dot(cqn, wqn_ref[...])
    qr = _dot(cqn, wqr_ref[...])
    qs = _dot(cqn, wqs_ref[...])
    for h in range(MLA_HEADS):
        ls = slice(h * LANES, (h + 1) * LANES)
        q_ref[0, h, :, 0:MLA_NOPE] = (qn[:, ls] * MLA_Q_SCALE).astype(BF16)
        rot = qr[:, ls] * cos + qs[:, ls] * sin
        q_ref[0, h, :, MLA_NOPE:MLA_QK] = (rot[:, 0:MLA_ROPE] * MLA_Q_SCALE).astype(BF16)

    ckvn = _rms(ckv_ref[...].astype(F32), kvg_ref[...]).astype(BF16)
    knt = _dot_nt(wknt_ref[...], ckvn)
    vv = _dot(ckvn, wv_ref[...])
    misc = misc_ref[...]
    kr = misc.astype(F32) * cos + _dot(misc, perm_ref[...]) * sin
    krt = _dot_nt(eye_ref[...], kr.astype(BF16)).astype(BF16)
    for h in range(MLA_HEADS):
        kt_ref[0, h, 0, 0:MLA_NOPE, :] = knt[h * MLA_NOPE:(h + 1) * MLA_NOPE].astype(BF16)
        kt_ref[0, h, 0, MLA_NOPE:MLA_QK, :] = krt
        v_ref[0, h, :, 0:MLA_V] = vv[:, h * MLA_V:(h + 1) * MLA_V].astype(BF16)
        v_ref[0, h, :, MLA_V:MLA_V_EXT] = jnp.ones((vv.shape[0], MLA_V), BF16)


def _mlaprep(p, cos, sin, qg, kvg, wts, *, batch, tm):
    t_all = p.shape[0]
    t = t_all // batch
    nb = t // tm
    ntab = cos.shape[0] // tm
    wqn, wqr, wqs, wknt, wv, perm, eye = wts
    full = lambda a: pl.BlockSpec(a.shape, lambda b, i: (0,) * a.ndim)
    return pl.pallas_call(
        _mlaprep_kernel,
        grid=(batch, nb),
        in_specs=[
            pl.BlockSpec((tm, MLA_Q_RANK), lambda b, i: (b * nb + i, P_CQ // MLA_Q_RANK)),
            pl.BlockSpec((tm, MLA_KV_RANK), lambda b, i: (b * nb + i, P_CKV // MLA_KV_RANK)),
            pl.BlockSpec((tm, LANES), lambda b, i: (b * nb + i, P_MISC // LANES)),
            pl.BlockSpec((tm, LANES), lambda b, i: (i % ntab, 0)),
            pl.BlockSpec((tm, LANES), lambda b, i: (i % ntab, 0)),
            full(qg), full(kvg), full(wqn), full(wqr), full(wqs), full(wknt), full(wv),
            full(perm), full(eye),
        ],
        out_specs=[
            pl.BlockSpec((1, MLA_HEADS, tm, MLA_QK), lambda b, i: (b, 0, i, 0)),
            pl.BlockSpec((1, MLA_HEADS, 1, MLA_QK, tm), lambda b, i: (b, 0, i, 0, 0)),
            pl.BlockSpec((1, MLA_HEADS, tm, MLA_V_EXT), lambda b, i: (b, 0, i, 0)),
        ],
        out_shape=[
            jax.ShapeDtypeStruct((batch, MLA_HEADS, t, MLA_QK), BF16),
            jax.ShapeDtypeStruct((batch, MLA_HEADS, nb, MLA_QK, tm), BF16),
            jax.ShapeDtypeStruct((batch, MLA_HEADS, t, MLA_V_EXT), BF16),
        ],
        compiler_params=_cparams(("arbitrary", "arbitrary")),
        name="mlaprep",
    )(p, p, p, cos, sin, qg, kvg, wqn, wqr, wqs, wknt, wv, perm, eye)


def _attn_kernel(*refs, n_seg, n_sub):
    q_ref = refs[0]
    kt_refs = refs[1:1 + 2 * n_seg:2]
    v_refs = refs[2:2 + 2 * n_seg:2]
    o_ref = refs[1 + 2 * n_seg]
    m_scr, acc_scr = refs[2 + 2 * n_seg:]

    rows_per_sub = q_ref.shape[2] // n_sub
    m_scr[...] = jnp.full(m_scr.shape, -jnp.inf, F32)
    acc_scr[...] = jnp.zeros(acc_scr.shape, F32)

    for kt_ref, v_ref in zip(kt_refs, v_refs):
        n_blocks, tk = kt_ref.shape[2], kt_ref.shape[4]

        def step(j, carry, kt_ref=kt_ref, v_ref=v_ref, tk=tk):
            kt = kt_ref[0, 0, j]
            v_blk = v_ref[0, 0, pl.ds(pl.multiple_of(j * tk, tk), tk), :]
            for u in range(n_sub):
                rows = slice(u * rows_per_sub, (u + 1) * rows_per_sub)
                s = _dot(q_ref[0, 0, rows, :], kt)
                m_prev = m_scr[rows, :]
                m_next = jnp.maximum(m_prev, jnp.max(s, axis=1, keepdims=True))
                p = jnp.exp2((s - jnp.concatenate([m_next] * (tk // LANES), axis=1)).astype(BF16))
                alpha = jnp.exp2(m_prev - m_next)
                acc_scr[rows, :] = (jnp.concatenate([alpha] * (MLA_V_EXT // LANES), axis=1)
                                    * acc_scr[rows, :] + _dot(p, v_blk))
                m_scr[rows, :] = m_next
            return carry

        lax.fori_loop(0, n_blocks, step, 0)

    o_ref[0] = (acc_scr[:, 0:MLA_V] / acc_scr[:, MLA_V:MLA_V_EXT]).astype(BF16)


def _attn_pipe_kernel(q_ref, kt_ref, v_ref, ktt_ref, vt_ref, o_ref,
                      m_scr, acc_scr, s0_scr, s1_scr, st_scr, *, n_sub):
    n_blocks, tk = kt_ref.shape[2], kt_ref.shape[4]
    tq = m_scr.shape[0]
    n_q = q_ref.shape[2] // tq
    rows_per_sub = tq // n_sub
    subs = [slice(u * rows_per_sub, (u + 1) * rows_per_sub) for u in range(n_sub)]
    bufs = (s0_scr, s1_scr)

    def reset():
        m_scr[...] = jnp.full(m_scr.shape, -jnp.inf, F32)
        acc_scr[...] = jnp.zeros(acc_scr.shape, F32)

    def scores(qi, kt, s_ref):
        for rows in subs:
            q_rows = pl.ds(pl.multiple_of(qi * tq + rows.start, rows_per_sub), rows_per_sub)
            s_ref[rows, :] = _dot(q_ref[0, 0, q_rows, :], kt)

    def softmax_pv(s_ref, v_blk):
        width = s_ref.shape[1]
        for rows in subs:
            s = s_ref[rows, :]
            m_prev = m_scr[rows, :]
            m_next = jnp.maximum(m_prev, jnp.max(s, axis=1, keepdims=True))
            p = jnp.exp2((s - jnp.concatenate([m_next] * (width // LANES), axis=1)).astype(BF16))
            alpha = jnp.exp2(m_prev - m_next)
            acc_scr[rows, :] = (jnp.concatenate([alpha] * (MLA_V_EXT // LANES), axis=1)
                                * acc_scr[rows, :] + _dot(p, v_blk))
            m_scr[rows, :] = m_next

    reset()
    scores(0, kt_ref[0, 0, 0], bufs[0])

    def query_block(qi, carry):
        for j in range(n_blocks):
            if j + 1 < n_blocks:
                scores(qi, kt_ref[0, 0, j + 1], bufs[(j + 1) % 2])
            else:
                scores(qi, ktt_ref[0, 0, 0], st_scr)
            softmax_pv(bufs[j % 2], v_ref[0, 0, j * tk:(j + 1) * tk, :])
        scores(jnp.minimum(qi + 1, n_q - 1), kt_ref[0, 0, 0], bufs[0])
        softmax_pv(st_scr, vt_ref[0, 0])
        o_rows = pl.ds(pl.multiple_of(qi * tq, tq), tq)
        o_ref[0, o_rows, :] = (acc_scr[:, 0:MLA_V] / acc_scr[:, MLA_V:MLA_V_EXT]).astype(BF16)
        reset()
        return carry

    lax.fori_loop(0, n_q, query_block, 0)


def _attention_pipelined(q, kt, v, kt_tail, v_tail, *, tq, n_sub):
    b, h, t, dqk = q.shape
    tk, tt = kt.shape[4], kt_tail.shape[4]
    assert kt_tail.shape[2] == 1
    return pl.pallas_call(
        functools.partial(_attn_pipe_kernel, n_sub=n_sub),
        grid=(b, h),
        in_specs=[
            pl.BlockSpec((1, 1, t, dqk), lambda bi, hi: (bi, hi, 0, 0)),
            pl.BlockSpec((1, 1) + kt.shape[2:], lambda bi, hi: (bi, hi, 0, 0, 0)),
            pl.BlockSpec((1, 1) + v.shape[2:], lambda bi, hi: (bi, hi, 0, 0)),
            pl.BlockSpec((1, 1) + kt_tail.shape[2:], lambda bi, hi: (bi, hi, 0, 0, 0)),
            pl.BlockSpec((1, 1) + v_tail.shape[2:], lambda bi, hi: (bi, hi, 0, 0)),
        ],
        out_specs=pl.BlockSpec((1, t, MLA_V), lambda bi, hi: (bi, 0, hi)),
        out_shape=jax.ShapeDtypeStruct((b, t, h * MLA_V), BF16),
        scratch_shapes=[pltpu.VMEM((tq, LANES), F32), pltpu.VMEM((tq, MLA_V_EXT), F32),
                        pltpu.VMEM((tq, tk), F32), pltpu.VMEM((tq, tk), F32),
                        pltpu.VMEM((tq, tt), F32)],
        compiler_params=_cparams(("arbitrary", "arbitrary")),
        name="mla_attention_pipe",
    )(q, kt, v, kt_tail, v_tail)


def _attention(q, segs, *, tq, n_sub):
    b, h, t, dqk = q.shape
    in_specs = [pl.BlockSpec((1, 1, tq, dqk), lambda bi, hi, qi: (bi, hi, qi, 0))]
    args = [q]
    for kt, v in segs:
        in_specs.append(pl.BlockSpec((1, 1) + kt.shape[2:], lambda bi, hi, qi: (bi, hi, 0, 0, 0)))
        in_specs.append(pl.BlockSpec((1, 1) + v.shape[2:], lambda bi, hi, qi: (bi, hi, 0, 0)))
        args += [kt, v]
    return pl.pallas_call(
        functools.partial(_attn_kernel, n_seg=len(segs), n_sub=n_sub),
        grid=(b, h, t // tq),
        in_specs=in_specs,
        out_specs=pl.BlockSpec((1, tq, MLA_V), lambda bi, hi, qi: (bi, qi, hi)),
        out_shape=jax.ShapeDtypeStruct((b, t, h * MLA_V), BF16),
        scratch_shapes=[pltpu.VMEM((tq, LANES), F32), pltpu.VMEM((tq, MLA_V_EXT), F32)],
        compiler_params=_cparams(("arbitrary", "arbitrary", "arbitrary")),
        name="mla_attention",
    )(*args)


def _mix_residual_norm(x_ref, of_ref, ob_ref, r_ref, mla_ref, m, gg_ref, wo_ref, ln2_ref,
                       rows=slice(None)):
    o = of_ref[rows, :].astype(F32) + ob_ref[rows, :].astype(F32)
    gg = gg_ref[...]
    y = jnp.concatenate(
        [_rms(o[:, h * GLA_DV:(h + 1) * GLA_DV], gg) for h in range(GLA_HEADS)], axis=1)
    mix = (y * _silu(r_ref[rows, :].astype(F32))).astype(BF16)
    yo = _dot(mix, wo_ref[0:GLA_V_W, :]) + _dot(mla_ref[rows, :], wo_ref[GLA_V_W:, :])
    x1 = x_ref[rows, :] + m[2:3] * yo
    h2 = _rms(x1, ln2_ref[...]) * (1.0 + m[4:5]) + m[3:4]
    return x1, h2


def _outproj_ffn_kernel(*refs, groups, final_norm):
    (x_ref, of_ref, ob_ref, r_ref, mla_ref, mod_ref, gg_ref, wo_ref, ln2_ref,
     wg_ref, wu_ref, wd_ref) = refs[:12]
    fin_ref = refs[12] if final_norm else None
    o_ref = refs[-1]
    m = mod_ref[0]
    x1, h2 = _mix_residual_norm(x_ref, of_ref, ob_ref, r_ref, mla_ref, m, gg_ref, wo_ref, ln2_ref)
    h = h2.astype(BF16)
    y = None
    for lo, hi in groups:
        a = _dot(h, wg_ref[:, lo:hi])
        u = _dot(h, wu_ref[:, lo:hi])
        part = _dot((_silu(a) * u).astype(BF16), wd_ref[lo:hi, :])
        y = part if y is None else y + part
    x2 = x1 + m[5:6] * y
    if final_norm:
        x2 = _rms(x2, fin_ref[...])
    o_ref[...] = x2


def _outproj_ffn(x, o_f, o_b, p, mla, mods, row_fn, gg, wo, ln2, wg, wu, wd, fin_g, tm):
    t, d = x.shape
    ff = wg.shape[1]
    final_norm = fin_g is not None
    step = 1024 if ff > 1024 else ff
    groups = tuple((lo, min(lo + step, ff)) for lo in range(0, ff, step))
    resident = lambda a: pl.BlockSpec(a.shape, lambda i: (0,) * a.ndim, pipeline_mode=pl.Buffered(1))
    in_specs = [
        pl.BlockSpec((tm, d), lambda i: (i, 0)),
        pl.BlockSpec((tm, GLA_V_W), lambda i: (i, 0)),
        pl.BlockSpec((tm, GLA_V_W), lambda i: (i, 0)),
        pl.BlockSpec((tm, GLA_V_W), lambda i: (i, P_R // GLA_V_W)),
        pl.BlockSpec((tm, MLA_V_W), lambda i: (i, 0)),
        pl.BlockSpec((1, 6, d), lambda i: (row_fn(i), 0, 0)),
        resident(gg), resident(wo), resident(ln2), resident(wg), resident(wu), resident(wd),
    ]
    args = [x, o_f, o_b, p, mla, mods, gg, wo, ln2, wg, wu, wd]
    if final_norm:
        in_specs.append(resident(fin_g))
        args.append(fin_g)
    return pl.pallas_call(
        functools.partial(_outproj_ffn_kernel, groups=groups, final_norm=final_norm),
        grid=(t // tm,),
        in_specs=in_specs,
        out_specs=pl.BlockSpec((tm, d), lambda i: (i, 0)),
        out_shape=jax.ShapeDtypeStruct((t, d), F32),
        compiler_params=_cparams(("arbitrary",)),
        name="outproj_ffn",
    )(*args)


def _outproj_kernel(*refs, with_router):
    (x_ref, of_ref, ob_ref, r_ref, mla_ref, mod_ref, gg_ref, wo_ref, ln2_ref) = refs[:9]
    if with_router:
        rwh_ref, rwl_ref, x1_ref, h2_ref, comb_ref = refs[9:]
    else:
        x1_ref, h2_ref = refs[9:]
    m = mod_ref[0]
    tm = x_ref.shape[0]
    n_groups = 2 if tm % 32 == 0 else 1
    for g in range(n_groups):
        rows = slice(g * tm // n_groups, (g + 1) * tm // n_groups)
        x1, h2 = _mix_residual_norm(x_ref, of_ref, ob_ref, r_ref, mla_ref, m, gg_ref, wo_ref,
                                    ln2_ref, rows)
        x1_ref[rows, :] = x1
        if not with_router:
            h2_ref[rows, :] = h2.astype(BF16)
            continue
        h2_ref[rows, :] = _pack_bf16_pairs(h2)
        h_hi = h2.astype(BF16)
        h_lo = (h2 - h_hi.astype(F32)).astype(BF16)
        logits = _dot(h_hi, rwh_ref[...]) + _dot(h_lo, rwh_ref[...]) + _dot(h_hi, rwl_ref[...])
        lane = lax.broadcasted_iota(jnp.int32, logits.shape, 1).astype(F32)
        neg = jnp.float32(-jnp.inf)
        logits = jnp.where(lane < N_EXPERTS, logits, neg)
        m1 = jnp.max(logits, axis=1, keepdims=True)
        i1 = jnp.min(jnp.where(logits == m1, lane, float(LANES)), axis=1, keepdims=True)
        rest = jnp.where(lane == i1, neg, logits)
        m2 = jnp.max(rest, axis=1, keepdims=True)
        i2 = jnp.min(jnp.where(rest == m2, lane, float(LANES)), axis=1, keepdims=True)
        e2 = jnp.exp(m2 - m1)
        w1 = 1.0 / (1.0 + e2)
        comb_ref[rows, :] = (jnp.where(lane == ROUTE_E1, i1, 0.0) + jnp.where(lane == ROUTE_E2, i2, 0.0)
                             + jnp.where(lane == ROUTE_W1, w1, 0.0)
                             + jnp.where(lane == ROUTE_W2, e2 * w1, 0.0))


def _outproj(x, o_f, o_b, p, mla, mods, row_fn, gg, wo, ln2, router, tm):
    t, d = x.shape
    with_router = router is not None
    full = lambda a: pl.BlockSpec(a.shape, lambda i: (0,) * a.ndim)
    in_specs = [
        pl.BlockSpec((tm, d), lambda i: (i, 0)),
        pl.BlockSpec((tm, GLA_V_W), lambda i: (i, 0)),
        pl.BlockSpec((tm, GLA_V_W), lambda i: (i, 0)),
        pl.BlockSpec((tm, GLA_V_W), lambda i: (i, P_R // GLA_V_W)),
        pl.BlockSpec((tm, MLA_V_W), lambda i: (i, 0)),
        pl.BlockSpec((1, 6, d), lambda i: (row_fn(i), 0, 0)),
        full(gg), full(wo), full(ln2),
    ]
    args = [x, o_f, o_b, p, mla, mods, gg, wo, ln2]
    h2_shape = jax.ShapeDtypeStruct((t, d // 2), F32) if with_router else jax.ShapeDtypeStruct((t, d), BF16)
    out_specs = [pl.BlockSpec((tm, d), lambda i: (i, 0)),
                 pl.BlockSpec((tm, h2_shape.shape[1]), lambda i: (i, 0))]
    out_shape = [jax.ShapeDtypeStruct((t, d), F32), h2_shape]
    if with_router:
        in_specs += [full(router[0]), full(router[1])]
        args += list(router)
        out_specs.append(pl.BlockSpec((tm, LANES), lambda i: (i, 0)))
        out_shape.append(jax.ShapeDtypeStruct((t, LANES), F32))
    return pl.pallas_call(
        functools.partial(_outproj_kernel, with_router=with_router),
        grid=(t // tm,),
        in_specs=in_specs,
        out_specs=out_specs,
        out_shape=out_shape,
        compiler_params=_cparams(("arbitrary",)),
        name="outproj",
    )(*args)


def _ffn_kernel(*refs, with_comb, final_norm):
    h_ref, x1_ref, mod_ref = refs[:3]
    k = 3
    comb_ref = fin_ref = None
    if with_comb:
        comb_ref = refs[k]
        k += 1
    wg_ref, wu_ref, wd_ref = refs[k:k + 3]
    k += 3
    if final_norm:
        fin_ref = refs[k]
        k += 1
    o_ref, acc = refs[k:]
    e = pl.program_id(1)
    f = pl.program_id(2)

    @pl.when((e == 0) & (f == 0))
    def _():
        acc[...] = jnp.zeros(acc.shape, F32)

    h = h_ref[...]
    a = _dot(h, wg_ref[0])
    u = _dot(h, wu_ref[0])
    act = _silu(a) * u
    if with_comb:
        comb = comb_ref[...]
        lane = lax.broadcasted_iota(jnp.int32, comb.shape, 1)
        act = act * jnp.sum(jnp.where(lane == e, comb, 0.0), axis=1, keepdims=True)
    acc[...] += _dot(act.astype(BF16), wd_ref[0])

    @pl.when((e == pl.num_programs(1) - 1) & (f == pl.num_programs(2) - 1))
    def _():
        x2 = x1_ref[...] + mod_ref[0][5:6] * acc[...]
        if final_norm:
            x2 = _rms(x2, fin_ref[...])
        o_ref[...] = x2


def _ffn(h2, x1, mods, row_fn, comb, wg, wu, wd, fin_g, tm, tf):
    t, d = x1.shape
    n_e, _, ff = wg.shape
    with_comb = comb is not None
    final_norm = fin_g is not None
    in_specs = [
        pl.BlockSpec((tm, d), lambda i, e, f: (i, 0)),
        pl.BlockSpec((tm, d), lambda i, e, f: (i, 0)),
        pl.BlockSpec((1, 6, d), lambda i, e, f: (row_fn(i), 0, 0)),
    ]
    args = [h2, x1, mods]
    if with_comb:
        in_specs.append(pl.BlockSpec((tm, LANES), lambda i, e, f: (i, 0)))
        args.append(comb)
    in_specs += [
        pl.BlockSpec((1, d, tf), lambda i, e, f: (e, 0, f)),
        pl.BlockSpec((1, d, tf), lambda i, e, f: (e, 0, f)),
        pl.BlockSpec((1, tf, d), lambda i, e, f: (e, f, 0)),
    ]
    args += [wg, wu, wd]
    if final_norm:
        in_specs.append(pl.BlockSpec((1, d), lambda i, e, f: (0, 0)))
        args.append(fin_g)
    return pl.pallas_call(
        functools.partial(_ffn_kernel, with_comb=with_comb, final_norm=final_norm),
        grid=(t // tm, n_e, ff // tf),
        in_specs=in_specs,
        out_specs=pl.BlockSpec((tm, d), lambda i, e, f: (i, 0)),
        out_shape=jax.ShapeDtypeStruct((t, d), F32),
        scratch_shapes=[pltpu.VMEM((tm, d), F32)],
        compiler_params=_cparams(("arbitrary", "arbitrary", "arbitrary")),
        name="ffn",
    )(*args)


def _sc_row_gather(table, idx):
    _, w = table.shape
    b = idx.shape[0]
    n_workers = SC_CORES * SC_SUBCORES
    assert b % (n_workers * SC_GATHER_ROWS) == 0, (b, n_workers, SC_GATHER_ROWS)
    b_per_w = b // n_workers
    n_chunks = b_per_w // SC_GATHER_ROWS
    mesh = plsc.VectorSubcoreMesh(core_axis_name="c", subcore_axis_name="s",
                                  num_cores=SC_CORES, num_subcores=SC_SUBCORES)

    def body(table_hbm, idx_hbm, out_hbm, idx_a, idx_b, rows_a, rows_b, sem_a, sem_b):
        wid = lax.axis_index("s") * SC_CORES + lax.axis_index("c")
        base = wid * b_per_w
        idx_bufs, row_bufs, sems = (idx_a, idx_b), (rows_a, rows_b), (sem_a, sem_b)

        def start(ci):
            slot = ci % 2
            pltpu.sync_copy(idx_hbm.at[pl.ds(base + ci * SC_GATHER_ROWS, SC_GATHER_ROWS)], idx_bufs[slot])
            return pltpu.async_copy(table_hbm.at[idx_bufs[slot]], row_bufs[slot], sems[slot])

        pending = start(0)
        for ci in range(n_chunks):
            following = start(ci + 1) if ci + 1 < n_chunks else None
            pending.wait()
            pltpu.sync_copy(row_bufs[ci % 2], out_hbm.at[pl.ds(base + ci * SC_GATHER_ROWS, SC_GATHER_ROWS)])
            pending = following

    return pl.kernel(
        body,
        out_type=jax.ShapeDtypeStruct((b, w), F32),
        mesh=mesh,
        scratch_types=[pltpu.VMEM((SC_GATHER_ROWS,), jnp.int32)] * 2
        + [pltpu.VMEM((SC_GATHER_ROWS, w), F32)] * 2 + [pltpu.SemaphoreType.DMA] * 2,
        name="sc_row_gather",
    )(table, idx)


def _sc_row_scatter2(table, pos, n_out):
    t, w = table.shape
    n_workers = SC_CORES * SC_SUBCORES
    assert t % (n_workers * SC_GATHER_ROWS) == 0, (t, n_workers, SC_GATHER_ROWS)
    t_per_w = t // n_workers
    n_chunks = t_per_w // SC_GATHER_ROWS
    mesh = plsc.VectorSubcoreMesh(core_axis_name="c", subcore_axis_name="s",
                                  num_cores=SC_CORES, num_subcores=SC_SUBCORES)

    def body(table_hbm, pos_hbm, out_hbm, i0a, i1a, i0b, i1b, rows_a, rows_b, sem_a, sem_b):
        wid = lax.axis_index("s") * SC_CORES + lax.axis_index("c")
        base = wid * t_per_w
        idx0, idx1, row_bufs, sems = (i0a, i0b), (i1a, i1b), (rows_a, rows_b), (sem_a, sem_b)

        def drain(pair):
            if pair is not None:
                pair[0].wait()
                pair[1].wait()

        pending = [None, None]
        for ci in range(n_chunks):
            slot = ci % 2
            drain(pending[slot])
            off = base + ci * SC_GATHER_ROWS
            pltpu.sync_copy(pos_hbm.at[0, pl.ds(off, SC_GATHER_ROWS)], idx0[slot])
            pltpu.sync_copy(pos_hbm.at[1, pl.ds(off, SC_GATHER_ROWS)], idx1[slot])
            pltpu.sync_copy(table_hbm.at[pl.ds(off, SC_GATHER_ROWS)], row_bufs[slot])
            pending[slot] = (pltpu.async_copy(row_bufs[slot], out_hbm.at[idx0[slot]], sems[slot]),
                             pltpu.async_copy(row_bufs[slot], out_hbm.at[idx1[slot]], sems[slot]))
        drain(pending[0])
        drain(pending[1])

    return pl.kernel(
        body,
        out_type=jax.ShapeDtypeStruct((n_out, w), F32),
        mesh=mesh,
        scratch_types=[pltpu.VMEM((SC_GATHER_ROWS,), jnp.int32)] * 4
        + [pltpu.VMEM((SC_GATHER_ROWS, w), F32)] * 2 + [pltpu.SemaphoreType.DMA] * 2,
        name="sc_row_scatter",
    )(table, pos)


def _moe_plan_kernel(route_ref, tri_ref, utri_ref, pos_ref, cnt_ref, run_scr, off_scr):
    phase = pl.program_id(0)
    blk = pl.program_id(1)
    route = route_ref[...]
    lane = lax.broadcasted_iota(jnp.int32, route.shape, 1).astype(F32)
    oh1 = jnp.where(lane == route[:, ROUTE_E1:ROUTE_E1 + 1], 1.0, 0.0)
    oh2 = jnp.where(lane == route[:, ROUTE_E2:ROUTE_E2 + 1], 1.0, 0.0)
    oh = oh1 + oh2

    @pl.when(blk == 0)
    def _():
        run_scr[...] = jnp.zeros(run_scr.shape, F32)

    @pl.when(phase == 0)
    def _():
        run_scr[...] += jnp.sum(oh, axis=0, keepdims=True)

        @pl.when(blk == pl.num_programs(1) - 1)
        def _():
            counts = run_scr[...]
            cnt_ref[...] = counts
            tiles_per = jnp.floor((counts + (MOE_TILE - 1.0)) * (1.0 / MOE_TILE))
            tile_end = _dot(jnp.broadcast_to(tiles_per, (8, LANES)).astype(BF16), utri_ref[...])[0:1]
            off_scr[...] = (tile_end - tiles_per) * float(MOE_TILE)

    @pl.when(phase == 1)
    def _():
        incl = _dot(tri_ref[...], oh.astype(BF16))
        before = incl - oh + run_scr[...] + off_scr[...]
        p1 = jnp.sum(before * oh1, axis=1, keepdims=True)
        p2 = jnp.sum(before * oh2, axis=1, keepdims=True)
        pos = jnp.where(lane == 0.0, p1, 0.0) + jnp.where(lane == 1.0, p2, 0.0)
        pos_ref[...] = pos.astype(jnp.int32)
        run_scr[...] += incl[incl.shape[0] - 1:, :]


def _moe_plan(route, n_tiles, tm):
    t = route.shape[0]
    tri = jnp.asarray(np.tril(np.ones((tm, tm), np.float32)), dtype=BF16)
    utri = jnp.asarray(np.triu(np.ones((LANES, LANES), np.float32)), dtype=BF16)
    pos, counts = pl.pallas_call(
        _moe_plan_kernel,
        grid=(2, t // tm),
        in_specs=[
            pl.BlockSpec((tm, LANES), lambda p, i: (i, 0)),
            pl.BlockSpec((tm, tm), lambda p, i: (0, 0)),
            pl.BlockSpec((LANES, LANES), lambda p, i: (0, 0)),
        ],
        out_specs=[
            pl.BlockSpec((tm, LANES), lambda p, i: (i * p, 0)),
            pl.BlockSpec((1, LANES), lambda p, i: (0, 0)),
        ],
        out_shape=[jax.ShapeDtypeStruct((t, LANES), jnp.int32),
                   jax.ShapeDtypeStruct((1, LANES), F32)],
        scratch_shapes=[pltpu.VMEM((1, LANES), F32), pltpu.VMEM((1, LANES), F32)],
        compiler_params=_cparams(("arbitrary", "arbitrary")),
        name="moe_plan",
    )(route, tri, utri)
    counts = counts[0, :N_EXPERTS].astype(jnp.int32)
    tile_end = jnp.cumsum((counts + MOE_TILE - 1) // MOE_TILE)
    n_used = tile_end[-1]
    tile_ids = jnp.minimum(jnp.arange(n_tiles, dtype=jnp.int32), n_used - 1)
    tile_expert = jnp.sum((tile_end[None, :] <= tile_ids[:, None]).astype(jnp.int32), axis=1)
    return pos[:, :2].T, tile_expert, n_used.reshape(1)


def _moe_ffn_kernel(te_ref, nused_ref, xs_ref, wg_ref, wu_ref, wd_ref, o_ref, acc, h_scr, *, splits):
    i = pl.program_id(0)
    f = pl.program_id(1)
    last_f = pl.num_programs(1) - 1
    used = i < nused_ref[0]

    @pl.when(used)
    def _():
        @pl.when(f == 0)
        def _():
            acc[...] = jnp.zeros(acc.shape, F32)
            h_scr[...] = _unpack_bf16_pairs(xs_ref[...]).astype(BF16)

        h = h_scr[...]
        for lo, hi in splits:
            a = _dot(h, wg_ref[0, :, lo:hi].astype(BF16))
            u = _dot(h, wu_ref[0, :, lo:hi].astype(BF16))
            acc[...] += _dot((_silu(a) * u).astype(BF16), wd_ref[0, lo:hi, :].astype(BF16))

        @pl.when(f == last_f)
        def _():
            o_ref[...] = _pack_bf16_pairs(acc[...])

    @pl.when(jnp.logical_not(used) & (f == last_f))
    def _():
        o_ref[...] = jnp.zeros(o_ref.shape, F32)


def _moe_ffn(xs, tile_expert, n_used, wg, wu, wd):
    rows, half = xs.shape
    d = 2 * half
    ff = wg.shape[2]
    if ff % MOE_FF_TILE == 0:
        tf, splits = MOE_FF_TILE, MOE_FF_SPLITS
    else:
        tf, splits = ff, ((0, ff),)
    n_tiles = rows // MOE_TILE
    grid_spec = pltpu.PrefetchScalarGridSpec(
        num_scalar_prefetch=2,
        grid=(n_tiles, ff // tf),
        in_specs=[
            pl.BlockSpec((MOE_TILE, half), lambda i, f, te, nu: (i, 0)),
            pl.BlockSpec((1, d, tf), lambda i, f, te, nu: (te[i], 0, f)),
            pl.BlockSpec((1, d, tf), lambda i, f, te, nu: (te[i], 0, f)),
            pl.BlockSpec((1, tf, d), lambda i, f, te, nu: (te[i], f, 0)),
        ],
        out_specs=pl.BlockSpec((MOE_TILE, half), lambda i, f, te, nu: (i, 0)),
        scratch_shapes=[pltpu.VMEM((MOE_TILE, d), F32), pltpu.VMEM((MOE_TILE, d), BF16)],
    )
    return pl.pallas_call(
        functools.partial(_moe_ffn_kernel, splits=splits),
        grid_spec=grid_spec,
        out_shape=jax.ShapeDtypeStruct((rows, half), F32),
        compiler_params=_cparams(("arbitrary", "arbitrary")),
        name="moe_ffn",
    )(tile_expert, n_used, xs, wg, wu, wd)


def _combine_kernel(*refs, final_norm):
    x1_ref, y0_ref, y1_ref, route_ref, mod_ref = refs[:5]
    fin_ref = refs[5] if final_norm else None
    o_ref = refs[-1]
    route = route_ref[...]
    w1 = route[:, ROUTE_W1:ROUTE_W1 + 1]
    w2 = route[:, ROUTE_W2:ROUTE_W2 + 1]
    y = w1 * _unpack_bf16_pairs(y0_ref[...]) + w2 * _unpack_bf16_pairs(y1_ref[...])
    x2 = x1_ref[...] + mod_ref[0][5:6] * y
    if final_norm:
        x2 = _rms(x2, fin_ref[...])
    o_ref[...] = x2


def _combine(x1, yg, route, mods, row_fn, fin_g, tm):
    t, d = x1.shape
    nb = t // tm
    final_norm = fin_g is not None
    in_specs = [
        pl.BlockSpec((tm, d), lambda i: (i, 0)),
        pl.BlockSpec((tm, d // 2), lambda i: (i, 0)),
        pl.BlockSpec((tm, d // 2), lambda i: (i + nb, 0)),
        pl.BlockSpec((tm, LANES), lambda i: (i, 0)),
        pl.BlockSpec((1, 6, d), lambda i: (row_fn(i), 0, 0)),
    ]
    args = [x1, yg, yg, route, mods]
    if final_norm:
        in_specs.append(pl.BlockSpec((1, d), lambda i: (0, 0)))
        args.append(fin_g)
    return pl.pallas_call(
        functools.partial(_combine_kernel, final_norm=final_norm),
        grid=(nb,),
        in_specs=in_specs,
        out_specs=pl.BlockSpec((tm, d), lambda i: (i, 0)),
        out_shape=jax.ShapeDtypeStruct((t, d), F32),
        compiler_params=_cparams(("arbitrary",)),
        name="moe_combine",
    )(*args)


def _moe(h2, x1, route, mods, row_fn, wg, wu, wd, fin_g, tm):
    t = h2.shape[0]
    n_tiles = -(-2 * t // MOE_TILE) + N_EXPERTS
    pos, tile_expert, n_used = _moe_plan(route, n_tiles, _pick_tile(t, 1024))
    xs = _sc_row_scatter2(h2, pos, n_tiles * MOE_TILE)
    ys = _moe_ffn(xs, tile_expert, n_used, wg, wu, wd)
    yg = _sc_row_gather(ys, pos.reshape(-1))
    return _combine(x1, yg, route, mods, row_fn, fin_g, tm)


def _rope_partner():
    j = np.arange(MLA_ROPE)
    return np.where((j % 32) < 16, j + 16, j - 16)


def _prep_in_weight(w):
    d = w.shape[0]
    cols = [w[:, 0:1024], w[:, 1056:1568], w[:, 1568:1824], w[:, 1824:1952], w[:, 1952:2016],
            w[:, 1024:1056], jnp.zeros((d, P_WIDTH - 2016), w.dtype)]
    return jnp.concatenate(cols, axis=1).astype(BF16)


def _prep_gate_weight(w_g2, b_g2):
    ws = []
    for z, off in ((0, MISC_GF), (1, MISC_GB)):
        ws.append(jnp.zeros((LANES, GLA_QK_W), F32).at[off:off + GLA_GATE_RANK].set(w_g2[z]))
    return jnp.stack(ws).astype(BF16), b_g2.reshape(2, 1, GLA_QK_W)


def _prep_mla_weights(w_uq, w_ukv):
    partner = _rope_partner()
    wq = w_uq.reshape(MLA_Q_RANK, MLA_HEADS, MLA_QK)
    wqn = wq[:, :, :MLA_NOPE].reshape(MLA_Q_RANK, MLA_HEADS * MLA_NOPE)
    rope = wq[:, :, MLA_NOPE:]
    pad = jnp.zeros((MLA_Q_RANK, MLA_HEADS, LANES - MLA_ROPE), w_uq.dtype)
    wqr = jnp.concatenate([rope, pad], axis=2).reshape(MLA_Q_RANK, MLA_HEADS * LANES)
    wqs = jnp.concatenate([rope[:, :, partner], pad], axis=2).reshape(MLA_Q_RANK, MLA_HEADS * LANES)
    wkv = w_ukv.reshape(MLA_KV_RANK, MLA_HEADS, MLA_NOPE + MLA_V)
    wknt = wkv[:, :, :MLA_NOPE].reshape(MLA_KV_RANK, MLA_HEADS * MLA_NOPE).T
    wv = wkv[:, :, MLA_NOPE:].reshape(MLA_KV_RANK, MLA_HEADS * MLA_V)
    perm = np.zeros((LANES, LANES), np.float32)
    perm[partner, np.arange(MLA_ROPE)] = 1.0
    eye = np.eye(MLA_ROPE, LANES, dtype=np.float32)
    return (wqn.astype(BF16), wqr.astype(BF16), wqs.astype(BF16), wknt.astype(BF16),
            wv.astype(BF16), jnp.asarray(perm, BF16), jnp.asarray(eye, BF16))


def _rope_tables(n_tok):
    rows = n_tok // GRID_W
    row = np.repeat(np.arange(rows, dtype=np.float32), GRID_W)
    col = np.tile(np.arange(GRID_W, dtype=np.float32), rows)
    nfreq = MLA_ROPE // 4
    inv = np.float32(ROPE_BASE) ** (-np.arange(nfreq, dtype=np.float32) / np.float32(nfreq))
    ar = (row[:, None] * inv).astype(np.float32)
    ac = (col[:, None] * inv).astype(np.float32)
    zero = np.zeros((n_tok, LANES - MLA_ROPE), np.float32)
    cos = np.concatenate([np.cos(ar), np.cos(ar), np.cos(ac), np.cos(ac), zero], axis=1)
    sin = np.concatenate([-np.sin(ar), np.sin(ar), -np.sin(ac), np.sin(ac), zero], axis=1)
    return jnp.asarray(cos, F32), jnp.asarray(sin, F32)


def _identity_tables(n_tok):
    cos = jnp.concatenate([jnp.ones((n_tok, MLA_ROPE), F32),
                           jnp.zeros((n_tok, LANES - MLA_ROPE), F32)], axis=1)
    return cos, jnp.zeros((n_tok, LANES), F32)


def _pick_tile(n, pref):
    t = min(n, pref)
    while n % t:
        t //= 2
    return t


def _pick_ff_tile(ff):
    best = LANES
    for m in range(1, ff // LANES + 1):
        if ff % (m * LANES) == 0 and m * LANES <= 1408:
            best = m * LANES
    return best


@jax.jit
def _forward(x, c, ctx, c_ctx, w_mod, b_mod, ln1_g, ln2_g, w_in, w_gla_g2, b_gla_g2, gla_norm_g,
             mla_q_norm_g, w_uq, mla_kv_norm_g, w_ukv, w_out, ffn_w_gate, ffn_w_up, ffn_w_down,
             router_w, exp_w_gate, exp_w_up, exp_w_down, final_norm_g):
    batch, seq, d = x.shape
    n_ctx = ctx.shape[1]
    depth = w_mod.shape[0]

    cvec = jnp.zeros((8, d), F32).at[:batch].set(c).at[batch].set(c_ctx)
    mods_all = _modulation(cvec, w_mod, b_mod).reshape(depth, 8, 6, d)

    xl = x.reshape(batch * seq, d)
    xc = ctx.reshape(batch * n_ctx, d)

    tm_l = _pick_tile(seq, 512)
    tm_c = _pick_tile(n_ctx, 256)
    tk_l = _pick_tile(seq, 1024)
    cb_l = _pick_tile(seq, 256)
    cb_c = _pick_tile(n_ctx, 256)
    row_l = lambda tm: (lambda i: i // (seq // tm))
    row_c = lambda i: batch

    rope_l = _rope_tables(seq)
    rope_c = _identity_tables(tm_c)
    zero_state = jnp.zeros((2, batch, GLA_QK_W, GLA_DV), F32)

    for i in range(depth):
        need_ctx = i < depth - 1
        last = i == depth - 1
        mods = mods_all[i]
        ln1 = ln1_g[i].reshape(1, d)
        ln2 = ln2_g[i].reshape(1, d)
        w_in_r = _prep_in_weight(w_in[i])
        gates = _prep_gate_weight(w_gla_g2[i], b_gla_g2[i])
        mla_w = _prep_mla_weights(w_uq[i], w_ukv[i])
        qg = mla_q_norm_g[i].reshape(1, MLA_Q_RANK)
        kvg = mla_kv_norm_g[i].reshape(1, MLA_KV_RANK)
        gg = gla_norm_g[i].reshape(1, GLA_DV)
        wo = w_out[i].astype(BF16)

        p_l = _inproj(xl, mods, row_l(tm_l), ln1, w_in_r, tm_l)
        p_c = _inproj(xc, mods, row_c, ln1, w_in_r, tm_c)

        oc_f, oc_b, s_ctx = _gla(p_c, *gates, zero_state, batch=batch, cb=cb_c)
        ol_f, ol_b, _ = _gla(p_l, *gates, s_ctx, batch=batch, cb=cb_l)

        q_l, kt_l, v_l = _mlaprep(p_l, *rope_l, qg, kvg, mla_w, batch=batch, tm=tk_l)
        q_c, kt_c, v_c = _mlaprep(p_c, *rope_c, qg, kvg, mla_w, batch=batch, tm=tm_c)
        m_l = _attention_pipelined(q_l, kt_l, v_l, kt_c, v_c, tq=tk_l, n_sub=max(1, tk_l // 512))
        m_l = m_l.reshape(batch * seq, MLA_V_W)

        if i % 2 == 0:
            j = i // 2
            router = None
            wg = ffn_w_gate[j].astype(BF16)
            wu = ffn_w_up[j].astype(BF16)
            wd = ffn_w_down[j].astype(BF16)
        else:
            j = i // 2
            rw = jnp.zeros((d, LANES), F32).at[:, :N_EXPERTS].set(router_w[j])
            rw_hi = rw.astype(BF16)
            router = (rw_hi, (rw - rw_hi.astype(F32)).astype(BF16))
            wg, wu, wd = exp_w_gate[j], exp_w_up[j], exp_w_down[j]
        fin = final_norm_g.reshape(1, d) if last else None

        if router is None:
            xl = _outproj_ffn(xl, ol_f, ol_b, p_l, m_l, mods, row_l(tm_l), gg, wo, ln2,
                              wg, wu, wd, fin, tm_l)
        else:
            outs = _outproj(xl, ol_f, ol_b, p_l, m_l, mods, row_l(tm_l), gg, wo, ln2, router, tm_l)
            xl = _moe(outs[1], outs[0], outs[2], mods, row_l(tm_l), wg, wu, wd, fin, tm_l)

        if need_ctx:
            m_c = _attention(q_c, [(kt_c, v_c)], tq=tm_c, n_sub=1).reshape(batch * n_ctx, MLA_V_W)
            if router is None:
                xc = _outproj_ffn(xc, oc_f, oc_b, p_c, m_c, mods, row_c, gg, wo, ln2,
                                  wg, wu, wd, None, tm_c)
            else:
                outs_c = _outproj(xc, oc_f, oc_b, p_c, m_c, mods, row_c, gg, wo, ln2, router, tm_c)
                tm_fc = _pick_tile(batch * n_ctx, 512)
                r_c = outs_c[2]
                lane = jnp.arange(LANES, dtype=F32)[None, :]
                comb_c = (jnp.where(lane == r_c[:, ROUTE_E1:ROUTE_E1 + 1], r_c[:, ROUTE_W1:ROUTE_W1 + 1], 0.0)
                          + jnp.where(lane == r_c[:, ROUTE_E2:ROUTE_E2 + 1], r_c[:, ROUTE_W2:ROUTE_W2 + 1], 0.0))
                bits = lax.bitcast_convert_type(outs_c[1], jnp.uint32)
                h2_c = jnp.concatenate([lax.bitcast_convert_type(bits << 16, F32),
                                        lax.bitcast_convert_type(bits & jnp.uint32(0xFFFF0000), F32)], axis=1)
                xc = _ffn(h2_c.astype(BF16), outs_c[0], mods, row_c, comb_c,
                          wg.astype(BF16), wu.astype(BF16), wd.astype(BF16), None, tm_fc,
                          _pick_ff_tile(wg.shape[2]))

    return xl.reshape(batch, seq, d)


def kernel(x, c, ctx, c_ctx, w_mod, b_mod, ln1_g, ln2_g, w_in, w_gla_g2, b_gla_g2, gla_norm_g,
           mla_q_norm_g, w_uq, mla_kv_norm_g, w_ukv, w_out, ffn_w_gate, ffn_w_up, ffn_w_down,
           router_w, exp_w_gate, exp_w_up, exp_w_down, final_norm_g):
    return _forward(x, c, ctx, c_ctx, w_mod, b_mod, ln1_g, ln2_g, w_in, w_gla_g2, b_gla_g2,
                    gla_norm_g, mla_q_norm_g, w_uq, mla_kv_norm_g, w_ukv, w_out, ffn_w_gate,
                    ffn_w_up, ffn_w_down, router_w, exp_w_gate, exp_w_up, exp_w_down, final_norm_g)
```

```python
import functools

import numpy as np
import jax
import jax.numpy as jnp
from jax import lax
from jax.experimental import pallas as pl
from jax.experimental.pallas import tpu as pltpu
from jax.experimental.pallas import tpu_sc as plsc

F32 = jnp.float32
BF16 = jnp.bfloat16

D_MODEL = 1024
EPS = 1e-6
GRID_W = 64

GLA_HEADS = 4
GLA_DK = 64
GLA_DV = 128
GLA_GATE_RANK = 16
GLA_GATE_NORM = 16.0
GLA_CHUNK = 64
GLA_QK_W = GLA_HEADS * GLA_DK
GLA_V_W = GLA_HEADS * GLA_DV
GLA_EXP_CLAMP = 80.0

MLA_HEADS = 4
MLA_NOPE = 128
MLA_ROPE = 64
MLA_V = 128
MLA_QK = MLA_NOPE + MLA_ROPE
MLA_Q_RANK = 256
MLA_KV_RANK = 128
MLA_SCALE = MLA_QK ** -0.5
MLA_Q_SCALE = MLA_SCALE * 1.4426950408889634
MLA_V_W = MLA_HEADS * MLA_V
MLA_V_EXT = 2 * MLA_V
ROPE_BASE = 10000.0

N_EXPERTS = 8
LANES = 128
ROUTE_E1, ROUTE_E2, ROUTE_W1, ROUTE_W2 = 0, 1, 2, 3

SC_CORES = 2
SC_SUBCORES = 16
SC_GATHER_ROWS = 64
MOE_TILE = 1024
MOE_FF_TILE = 512
MOE_FF_SPLITS = ((0, 256), (256, 512))

P_Q, P_K, P_V, P_R, P_CQ, P_CKV, P_MISC = 0, 256, 512, 1024, 1536, 1792, 1920
P_WIDTH = 2048
MISC_KR, MISC_GF, MISC_GB = 0, 64, 80

VMEM_LIMIT = 56 * 1024 * 1024


def _cparams(sem):
    return pltpu.CompilerParams(dimension_semantics=sem, vmem_limit_bytes=VMEM_LIMIT)


def _rms(x, g):
    return x * lax.rsqrt(jnp.mean(x * x, axis=-1, keepdims=True) + EPS) * g


def _silu(x):
    return x / (1.0 + jnp.exp(-x))


def _dot(a, b):
    return jnp.dot(a, b, preferred_element_type=F32)


def _dot_nt(a, b):
    return lax.dot_general(a, b, (((1,), (1,)), ((), ())), preferred_element_type=F32)


def _dot_tn(a, b):
    return lax.dot_general(a, b, (((0,), (0,)), ((), ())), preferred_element_type=F32)


def _pack_bf16_pairs(x):
    w = x.shape[1] // 2
    words = pltpu.pack_elementwise([x[:, :w], x[:, w:]], packed_dtype=BF16)
    return lax.bitcast_convert_type(words, F32)


def _unpack_bf16_pairs(p):
    words = lax.bitcast_convert_type(p, jnp.int32)
    lo = pltpu.unpack_elementwise(words, index=0, packed_dtype=BF16, unpacked_dtype=F32)
    hi = pltpu.unpack_elementwise(words, index=1, packed_dtype=BF16, unpacked_dtype=F32)
    return jnp.concatenate([lo, hi], axis=1)


def _mod_kernel(c_ref, w_ref, b_ref, o_ref):
    s = _silu(c_ref[...]).astype(BF16)
    o_ref[0] = _dot(s, w_ref[0].astype(BF16)) + b_ref[0]


def _modulation(cvec, w_mod, b_mod):
    depth, d, n = w_mod.shape
    tn = 1536
    return pl.pallas_call(
        _mod_kernel,
        grid=(depth, n // tn),
        in_specs=[
            pl.BlockSpec((8, d), lambda l, j: (0, 0)),
            pl.BlockSpec((1, d, tn), lambda l, j: (l, 0, j)),
            pl.BlockSpec((1, 1, tn), lambda l, j: (l, 0, j)),
        ],
        out_specs=pl.BlockSpec((1, 8, tn), lambda l, j: (l, 0, j)),
        out_shape=jax.ShapeDtypeStruct((depth, 8, n), F32),
        compiler_params=_cparams(("arbitrary", "arbitrary")),
        name="modulation",
    )(cvec, w_mod, b_mod.reshape(depth, 1, n))


def _inproj_kernel(x_ref, mod_ref, g_ref, w_ref, o_ref):
    m = mod_ref[0]
    tm = x_ref.shape[0]
    n_groups = 2 if tm % 32 == 0 else 1
    for g in range(n_groups):
        rows = slice(g * tm // n_groups, (g + 1) * tm // n_groups)
        h = _rms(x_ref[rows, :], g_ref[...]) * (1.0 + m[1:2]) + m[0:1]
        o_ref[rows, :] = _dot(h.astype(BF16), w_ref[...]).astype(BF16)


def _inproj(x, mods, row_fn, ln_g, w, tm):
    t, d = x.shape
    return pl.pallas_call(
        _inproj_kernel,
        grid=(t // tm,),
        in_specs=[
            pl.BlockSpec((tm, d), lambda i: (i, 0)),
            pl.BlockSpec((1, 6, d), lambda i: (row_fn(i), 0, 0)),
            pl.BlockSpec((1, d), lambda i: (0, 0)),
            pl.BlockSpec((d, P_WIDTH), lambda i: (0, 0)),
        ],
        out_specs=pl.BlockSpec((tm, P_WIDTH), lambda i: (i, 0)),
        out_shape=jax.ShapeDtypeStruct((t, P_WIDTH), BF16),
        compiler_params=_cparams(("arbitrary",)),
        name="inproj",
    )(x, mods, ln_g, w)


def _gla_direction(q_ref, k_ref, v_ref, misc_ref, wg, bg, tri, s_scr, o_ref, *, reverse, n_chunks):
    c_len = GLA_CHUNK
    pre = _dot(misc_ref[...], wg) + bg
    g = (jnp.minimum(pre, 0.0) - jnp.log(1.0 + jnp.exp(-jnp.abs(pre)))) * (1.0 / GLA_GATE_NORM)
    g_hi = g.astype(BF16)
    g_lo = (g - g_hi.astype(F32)).astype(BF16)
    cum = _dot(tri, g_hi) + _dot(tri, g_lo)
    tot_rows = jnp.concatenate(
        [cum[c * c_len:c * c_len + 1] if reverse else cum[(c + 1) * c_len - 1:(c + 1) * c_len]
         for c in range(n_chunks)] + [jnp.zeros((8 - n_chunks, GLA_QK_W), F32)], axis=0)
    t_hi = tot_rows.astype(BF16)
    t_lo = (tot_rows - t_hi.astype(F32)).astype(BF16)
    eye = (lax.broadcasted_iota(jnp.int32, (GLA_QK_W, GLA_QK_W), 0)
           == lax.broadcasted_iota(jnp.int32, (GLA_QK_W, GLA_QK_W), 1))
    eye = jnp.where(eye, 1.0, 0.0).astype(BF16)
    tot_cols = _dot_nt(eye, t_hi) + _dot_nt(eye, t_lo)

    lane = lax.broadcasted_iota(jnp.int32, (c_len, GLA_QK_W), 1)
    head_masks = [(lane >= h * GLA_DK) & (lane < (h + 1) * GLA_DK) for h in range(GLA_HEADS)]
    row = lax.broadcasted_iota(jnp.int32, (GLA_HEADS * c_len, c_len), 0) % c_len
    col = lax.broadcasted_iota(jnp.int32, (GLA_HEADS * c_len, c_len), 1)
    pair_mask = (col >= row) if reverse else (col <= row)

    def stack_heads(a):
        return jnp.concatenate([jnp.where(mk, a, 0.0) for mk in head_masks], axis=0).astype(BF16)

    def step(c):
        sl = slice(c * c_len, (c + 1) * c_len)
        xc = cum[sl]
        tot = tot_rows[c:c + 1]
        ref = xc[c_len // 2:c_len // 2 + 1]
        qc = q_ref[sl, :].astype(F32) * (GLA_DK ** -0.5)
        kc = k_ref[sl, :].astype(F32)
        vc = v_ref[sl, :]
        q_mid = qc * jnp.exp(jnp.minimum(xc - ref, GLA_EXP_CLAMP))
        k_mid = (kc * jnp.exp(jnp.minimum(ref - xc, GLA_EXP_CLAMP))).astype(BF16)
        q_dec = qc * jnp.exp(xc)
        k_dec = kc * jnp.exp(tot - xc)

        attn = _dot_nt(stack_heads(q_mid), k_mid)
        attn = jnp.where(pair_mask, attn, 0.0).astype(BF16)
        s_prev = s_scr[...]
        o_inter = _dot(stack_heads(q_dec), s_prev.astype(BF16))
        kv = []
        for h in range(GLA_HEADS):
            rs = slice(h * c_len, (h + 1) * c_len)
            vs = slice(h * GLA_DV, (h + 1) * GLA_DV)
            o_h = o_inter[rs] + _dot(attn[rs], vc[:, vs])
            o_ref[sl, vs] = o_h.astype(BF16)
            kv.append(_dot_tn(k_dec[:, h * GLA_DK:(h + 1) * GLA_DK].astype(BF16), vc[:, vs]))
        s_scr[...] = s_prev * jnp.exp(tot_cols[:, c:c + 1]) + jnp.concatenate(kv, axis=0)

    return step


def _gla_kernel(qf_ref, kf_ref, vf_ref, mf_ref, qb_ref, kb_ref, vb_ref, mb_ref,
                wg_ref, bg_ref, tri_ref, s0_ref, of_ref, ob_ref, sfin_ref, sf_scr, sb_scr, *, n_chunks):
    blk = pl.program_id(1)

    @pl.when(blk == 0)
    def _():
        sf_scr[...] = s0_ref[0, 0]
        sb_scr[...] = s0_ref[1, 0]

    fwd = _gla_direction(qf_ref, kf_ref, vf_ref, mf_ref, wg_ref[0], bg_ref[0], tri_ref[0],
                         sf_scr, of_ref, reverse=False, n_chunks=n_chunks)
    bwd = _gla_direction(qb_ref, kb_ref, vb_ref, mb_ref, wg_ref[1], bg_ref[1], tri_ref[1],
                         sb_scr, ob_ref, reverse=True, n_chunks=n_chunks)
    for c in range(n_chunks):
        fwd(c)
        bwd(n_chunks - 1 - c)

    @pl.when(blk == pl.num_programs(1) - 1)
    def _():
        sfin_ref[0, 0] = sf_scr[...]
        sfin_ref[1, 0] = sb_scr[...]


def _block_diag_tri(n_chunks):
    c = GLA_CHUNK
    eye = np.eye(n_chunks, dtype=np.float32)
    lower = np.kron(eye, np.tril(np.ones((c, c), np.float32)))
    upper = np.kron(eye, np.triu(np.ones((c, c), np.float32)))
    return jnp.asarray(np.stack([lower, upper]), dtype=BF16)


def _gla(p, wg, bg, s0, *, batch, cb):
    t_all = p.shape[0]
    nblk = t_all // batch // cb
    n_chunks = cb // GLA_CHUNK
    assert n_chunks <= 8

    fw = lambda b, i: b * nblk + i
    bw = lambda b, i: b * nblk + (nblk - 1 - i)
    full = lambda a: pl.BlockSpec(a.shape, lambda b, i: (0,) * a.ndim)
    tri = _block_diag_tri(n_chunks)

    def token_specs(tok):
        return [
            pl.BlockSpec((cb, GLA_QK_W), lambda b, i: (tok(b, i), P_Q // GLA_QK_W)),
            pl.BlockSpec((cb, GLA_QK_W), lambda b, i: (tok(b, i), P_K // GLA_QK_W)),
            pl.BlockSpec((cb, GLA_V_W), lambda b, i: (tok(b, i), P_V // GLA_V_W)),
            pl.BlockSpec((cb, LANES), lambda b, i: (tok(b, i), P_MISC // LANES)),
        ]

    state_spec = pl.BlockSpec((2, 1, GLA_QK_W, GLA_DV), lambda b, i: (0, b, 0, 0))
    return pl.pallas_call(
        functools.partial(_gla_kernel, n_chunks=n_chunks),
        grid=(batch, nblk),
        in_specs=token_specs(fw) + token_specs(bw) + [full(wg), full(bg), full(tri), state_spec],
        out_specs=[
            pl.BlockSpec((cb, GLA_V_W), lambda b, i: (fw(b, i), 0)),
            pl.BlockSpec((cb, GLA_V_W), lambda b, i: (bw(b, i), 0)),
            state_spec,
        ],
        out_shape=[
            jax.ShapeDtypeStruct((t_all, GLA_V_W), BF16),
            jax.ShapeDtypeStruct((t_all, GLA_V_W), BF16),
            jax.ShapeDtypeStruct((2, batch, GLA_QK_W, GLA_DV), F32),
        ],
        scratch_shapes=[pltpu.VMEM((GLA_QK_W, GLA_DV), F32), pltpu.VMEM((GLA_QK_W, GLA_DV), F32)],
        compiler_params=_cparams(("arbitrary", "arbitrary")),
        name="gla",
    )(p, p, p, p, p, p, p, p, wg, bg, tri, s0)


def _mlaprep_kernel(cq_ref, ckv_ref, misc_ref, cos_ref, sin_ref, qg_ref, kvg_ref,
                    wqn_ref, wqr_ref, wqs_ref, wknt_ref, wv_ref, perm_ref, eye_ref,
                    q_ref, kt_ref, v_ref):
    cos = cos_ref[...]
    sin = sin_ref[...]
    cqn = _rms(cq_ref[...].astype(F32), qg_ref[...]).astype(BF16)
    qn = _dot(cqn, wqn_ref[...])
    qr = _dot(cqn, wqr_ref[...])
    qs = _dot(cqn, wqs_ref[...])
    for h in range(MLA_HEADS):
        ls = slice(h * LANES, (h + 1) * LANES)
        q_ref[0, h, :, 0:MLA_NOPE] = (qn[:, ls] * MLA_Q_SCALE).astype(BF16)
        rot = qr[:, ls] * cos + qs[:, ls] * sin
        q_ref[0, h, :, MLA_NOPE:MLA_QK] = (rot[:, 0:MLA_ROPE] * MLA_Q_SCALE).astype(BF16)

    ckvn = _rms(ckv_ref[...].astype(F32), kvg_ref[...]).astype(BF16)
    knt = _dot_nt(wknt_ref[...], ckvn)
    vv = _dot(ckvn, wv_ref[...])
    misc = misc_ref[...]
    kr = misc.astype(F32) * cos + _dot(misc, perm_ref[...]) * sin
    krt = _dot_nt(eye_ref[...], kr.astype(BF16)).astype(BF16)
    for h in range(MLA_HEADS):
        kt_ref[0, h, 0, 0:MLA_NOPE, :] = knt[h * MLA_NOPE:(h + 1) * MLA_NOPE].astype(BF16)
        kt_ref[0, h, 0, MLA_NOPE:MLA_QK, :] = krt
        v_ref[0, h, :, 0:MLA_V] = vv[:, h * MLA_V:(h + 1) * MLA_V].astype(BF16)
        v_ref[0, h, :, MLA_V:MLA_V_EXT] = jnp.ones((vv.shape[0], MLA_V), BF16)


def _mlaprep(p, cos, sin, qg, kvg, wts, *, batch, tm):
    t_all = p.shape[0]
    t = t_all // batch
    nb = t // tm
    ntab = cos.shape[0] // tm
    wqn, wqr, wqs, wknt, wv, perm, eye = wts
    full = lambda a: pl.BlockSpec(a.shape, lambda b, i: (0,) * a.ndim)
    return pl.pallas_call(
        _mlaprep_kernel,
        grid=(batch, nb),
        in_specs=[
            pl.BlockSpec((tm, MLA_Q_RANK), lambda b, i: (b * nb + i, P_CQ // MLA_Q_RANK)),
            pl.BlockSpec((tm, MLA_KV_RANK), lambda b, i: (b * nb + i, P_CKV // MLA_KV_RANK)),
            pl.BlockSpec((tm, LANES), lambda b, i: (b * nb + i, P_MISC // LANES)),
            pl.BlockSpec((tm, LANES), lambda b, i: (i % ntab, 0)),
            pl.BlockSpec((tm, LANES), lambda b, i: (i % ntab, 0)),
            full(qg), full(kvg), full(wqn), full(wqr), full(wqs), full(wknt), full(wv),
            full(perm), full(eye),
        ],
        out_specs=[
            pl.BlockSpec((1, MLA_HEADS, tm, MLA_QK), lambda b, i: (b, 0, i, 0)),
            pl.BlockSpec((1, MLA_HEADS, 1, MLA_QK, tm), lambda b, i: (b, 0, i, 0, 0)),
            pl.BlockSpec((1, MLA_HEADS, tm, MLA_V_EXT), lambda b, i: (b, 0, i, 0)),
        ],
        out_shape=[
            jax.ShapeDtypeStruct((batch, MLA_HEADS, t, MLA_QK), BF16),
            jax.ShapeDtypeStruct((batch, MLA_HEADS, nb, MLA_QK, tm), BF16),
            jax.ShapeDtypeStruct((batch, MLA_HEADS, t, MLA_V_EXT), BF16),
        ],
        compiler_params=_cparams(("arbitrary", "arbitrary")),
        name="mlaprep",
    )(p, p, p, cos, sin, qg, kvg, wqn, wqr, wqs, wknt, wv, perm, eye)


def _attn_kernel(*refs, n_seg, n_sub):
    q_ref = refs[0]
    kt_refs = refs[1:1 + 2 * n_seg:2]
    v_refs = refs[2:2 + 2 * n_seg:2]
    o_ref = refs[1 + 2 * n_seg]
    m_scr, acc_scr = refs[2 + 2 * n_seg:]

    rows_per_sub = q_ref.shape[2] // n_sub
    m_scr[...] = jnp.full(m_scr.shape, -jnp.inf, F32)
    acc_scr[...] = jnp.zeros(acc_scr.shape, F32)

    for kt_ref, v_ref in zip(kt_refs, v_refs):
        n_blocks, tk = kt_ref.shape[2], kt_ref.shape[4]

        def step(j, carry, kt_ref=kt_ref, v_ref=v_ref, tk=tk):
            kt = kt_ref[0, 0, j]
            v_blk = v_ref[0, 0, pl.ds(pl.multiple_of(j * tk, tk), tk), :]
            for u in range(n_sub):
                rows = slice(u * rows_per_sub, (u + 1) * rows_per_sub)
                s = _dot(q_ref[0, 0, rows, :], kt)
                m_prev = m_scr[rows, :]
                m_next = jnp.maximum(m_prev, jnp.max(s, axis=1, keepdims=True))
                p = jnp.exp2((s - jnp.concatenate([m_next] * (tk // LANES), axis=1)).astype(BF16))
                alpha = jnp.exp2(m_prev - m_next)
                acc_scr[rows, :] = (jnp.concatenate([alpha] * (MLA_V_EXT // LANES), axis=1)
                                    * acc_scr[rows, :] + _dot(p, v_blk))
                m_scr[rows, :] = m_next
            return carry

        lax.fori_loop(0, n_blocks, step, 0)

    o_ref[0] = (acc_scr[:, 0:MLA_V] / acc_scr[:, MLA_V:MLA_V_EXT]).astype(BF16)


def _attn_pipe_kernel(q_ref, kt_ref, v_ref, ktt_ref, vt_ref, o_ref,
                      m_scr, acc_scr, s0_scr, s1_scr, st_scr, *, n_sub):
    n_blocks, tk = kt_ref.shape[2], kt_ref.shape[4]
    tq = m_scr.shape[0]
    n_q = q_ref.shape[2] // tq
    rows_per_sub = tq // n_sub
    subs = [slice(u * rows_per_sub, (u + 1) * rows_per_sub) for u in range(n_sub)]
    bufs = (s0_scr, s1_scr)

    def reset():
        m_scr[...] = jnp.full(m_scr.shape, -jnp.inf, F32)
        acc_scr[...] = jnp.zeros(acc_scr.shape, F32)

    def scores(qi, kt, s_ref):
        for rows in subs:
            q_rows = pl.ds(pl.multiple_of(qi * tq + rows.start, rows_per_sub), rows_per_sub)
            s_ref[rows, :] = _dot(q_ref[0, 0, q_rows, :], kt)

    def softmax_pv(s_ref, v_blk):
        width = s_ref.shape[1]
        for rows in subs:
            s = s_ref[rows, :]
            m_prev = m_scr[rows, :]
            m_next = jnp.maximum(m_prev, jnp.max(s, axis=1, keepdims=True))
            p = jnp.exp2((s - jnp.concatenate([m_next] * (width // LANES), axis=1)).astype(BF16))
            alpha = jnp.exp2(m_prev - m_next)
            acc_scr[rows, :] = (jnp.concatenate([alpha] * (MLA_V_EXT // LANES), axis=1)
                                * acc_scr[rows, :] + _dot(p, v_blk))
            m_scr[rows, :] = m_next

    reset()
    scores(0, kt_ref[0, 0, 0], bufs[0])

    def query_block(qi, carry):
        for j in range(n_blocks):
            if j + 1 < n_blocks:
                scores(qi, kt_ref[0, 0, j + 1], bufs[(j + 1) % 2])
            else:
                scores(qi, ktt_ref[0, 0, 0], st_scr)
            softmax_pv(bufs[j % 2], v_ref[0, 0, j * tk:(j + 1) * tk, :])
        scores(jnp.minimum(qi + 1, n_q - 1), kt_ref[0, 0, 0], bufs[0])
        softmax_pv(st_scr, vt_ref[0, 0])
        o_rows = pl.ds(pl.multiple_of(qi * tq, tq), tq)
        o_ref[0, o_rows, :] = (acc_scr[:, 0:MLA_V] / acc_scr[:, MLA_V:MLA_V_EXT]).astype(BF16)
        reset()
        return carry

    lax.fori_loop(0, n_q, query_block, 0)


def _attention_pipelined(q, kt, v, kt_tail, v_tail, *, tq, n_sub):
    b, h, t, dqk = q.shape
    tk, tt = kt.shape[4], kt_tail.shape[4]
    assert kt_tail.shape[2] == 1
    return pl.pallas_call(
        functools.partial(_attn_pipe_kernel, n_sub=n_sub),
        grid=(b, h),
        in_specs=[
            pl.BlockSpec((1, 1, t, dqk), lambda bi, hi: (bi, hi, 0, 0)),
            pl.BlockSpec((1, 1) + kt.shape[2:], lambda bi, hi: (bi, hi, 0, 0, 0)),
            pl.BlockSpec((1, 1) + v.shape[2:], lambda bi, hi: (bi, hi, 0, 0)),
            pl.BlockSpec((1, 1) + kt_tail.shape[2:], lambda bi, hi: (bi, hi, 0, 0, 0)),
            pl.BlockSpec((1, 1) + v_tail.shape[2:], lambda bi, hi: (bi, hi, 0, 0)),
        ],
        out_specs=pl.BlockSpec((1, t, MLA_V), lambda bi, hi: (bi, 0, hi)),
        out_shape=jax.ShapeDtypeStruct((b, t, h * MLA_V), BF16),
        scratch_shapes=[pltpu.VMEM((tq, LANES), F32), pltpu.VMEM((tq, MLA_V_EXT), F32),
                        pltpu.VMEM((tq, tk), F32), pltpu.VMEM((tq, tk), F32),
                        pltpu.VMEM((tq, tt), F32)],
        compiler_params=_cparams(("arbitrary", "arbitrary")),
        name="mla_attention_pipe",
    )(q, kt, v, kt_tail, v_tail)


def _attention(q, segs, *, tq, n_sub):
    b, h, t, dqk = q.shape
    in_specs = [pl.BlockSpec((1, 1, tq, dqk), lambda bi, hi, qi: (bi, hi, qi, 0))]
    args = [q]
    for kt, v in segs:
        in_specs.append(pl.BlockSpec((1, 1) + kt.shape[2:], lambda bi, hi, qi: (bi, hi, 0, 0, 0)))
        in_specs.append(pl.BlockSpec((1, 1) + v.shape[2:], lambda bi, hi, qi: (bi, hi, 0, 0)))
        args += [kt, v]
    return pl.pallas_call(
        functools.partial(_attn_kernel, n_seg=len(segs), n_sub=n_sub),
        grid=(b, h, t // tq),
        in_specs=in_specs,
        out_specs=pl.BlockSpec((1, tq, MLA_V), lambda bi, hi, qi: (bi, qi, hi)),
        out_shape=jax.ShapeDtypeStruct((b, t, h * MLA_V), BF16),
        scratch_shapes=[pltpu.VMEM((tq, LANES), F32), pltpu.VMEM((tq, MLA_V_EXT), F32)],
        compiler_params=_cparams(("arbitrary", "arbitrary", "arbitrary")),
        name="mla_attention",
    )(*args)


def _mix_residual_norm(x_ref, of_ref, ob_ref, r_ref, mla_ref, m, gg_ref, wo_ref, ln2_ref,
                       rows=slice(None)):
    o = of_ref[rows, :].astype(F32) + ob_ref[rows, :].astype(F32)
    gg = gg_ref[...]
    y = jnp.concatenate(
        [_rms(o[:, h * GLA_DV:(h + 1) * GLA_DV], gg) for h in range(GLA_HEADS)], axis=1)
    mix = (y * _silu(r_ref[rows, :].astype(F32))).astype(BF16)
    yo = _dot(mix, wo_ref[0:GLA_V_W, :]) + _dot(mla_ref[rows, :], wo_ref[GLA_V_W:, :])
    x1 = x_ref[rows, :] + m[2:3] * yo
    h2 = _rms(x1, ln2_ref[...]) * (1.0 + m[4:5]) + m[3:4]
    return x1, h2


def _outproj_ffn_kernel(*refs, groups, final_norm):
    (x_ref, of_ref, ob_ref, r_ref, mla_ref, mod_ref, gg_ref, wo_ref, ln2_ref,
     wg_ref, wu_ref, wd_ref) = refs[:12]
    fin_ref = refs[12] if final_norm else None
    o_ref = refs[-1]
    m = mod_ref[0]
    x1, h2 = _mix_residual_norm(x_ref, of_ref, ob_ref, r_ref, mla_ref, m, gg_ref, wo_ref, ln2_ref)
    h = h2.astype(BF16)
    y = None
    for lo, hi in groups:
        a = _dot(h, wg_ref[:, lo:hi])
        u = _dot(h, wu_ref[:, lo:hi])
        part = _dot((_silu(a) * u).astype(BF16), wd_ref[lo:hi, :])
        y = part if y is None else y + part
    x2 = x1 + m[5:6] * y
    if final_norm:
        x2 = _rms(x2, fin_ref[...])
    o_ref[...] = x2


def _outproj_ffn(x, o_f, o_b, p, mla, mods, row_fn, gg, wo, ln2, wg, wu, wd, fin_g, tm):
    t, d = x.shape
    ff = wg.shape[1]
    final_norm = fin_g is not None
    step = 1024 if ff > 1024 else ff
    groups = tuple((lo, min(lo + step, ff)) for lo in range(0, ff, step))
    resident = lambda a: pl.BlockSpec(a.shape, lambda i: (0,) * a.ndim, pipeline_mode=pl.Buffered(1))
    in_specs = [
        pl.BlockSpec((tm, d), lambda i: (i, 0)),
        pl.BlockSpec((tm, GLA_V_W), lambda i: (i, 0)),
        pl.BlockSpec((tm, GLA_V_W), lambda i: (i, 0)),
        pl.BlockSpec((tm, GLA_V_W), lambda i: (i, P_R // GLA_V_W)),
        pl.BlockSpec((tm, MLA_V_W), lambda i: (i, 0)),
        pl.BlockSpec((1, 6, d), lambda i: (row_fn(i), 0, 0)),
        resident(gg), resident(wo), resident(ln2), resident(wg), resident(wu), resident(wd),
    ]
    args = [x, o_f, o_b, p, mla, mods, gg, wo, ln2, wg, wu, wd]
    if final_norm:
        in_specs.append(resident(fin_g))
        args.append(fin_g)
    return pl.pallas_call(
        functools.partial(_outproj_ffn_kernel, groups=groups, final_norm=final_norm),
        grid=(t // tm,),
        in_specs=in_specs,
        out_specs=pl.BlockSpec((tm, d), lambda i: (i, 0)),
        out_shape=jax.ShapeDtypeStruct((t, d), F32),
        compiler_params=_cparams(("arbitrary",)),
        name="outproj_ffn",
    )(*args)


def _outproj_kernel(*refs, with_router):
    (x_ref, of_ref, ob_ref, r_ref, mla_ref, mod_ref, gg_ref, wo_ref, ln2_ref) = refs[:9]
    if with_router:
        rwh_ref, rwl_ref, x1_ref, h2_ref, comb_ref = refs[9:]
    else:
        x1_ref, h2_ref = refs[9:]
    m = mod_ref[0]
    tm = x_ref.shape[0]
    n_groups = 2 if tm % 32 == 0 else 1
    for g in range(n_groups):
        rows = slice(g * tm // n_groups, (g + 1) * tm // n_groups)
        x1, h2 = _mix_residual_norm(x_ref, of_ref, ob_ref, r_ref, mla_ref, m, gg_ref, wo_ref,
                                    ln2_ref, rows)
        x1_ref[rows, :] = x1
        if not with_router:
            h2_ref[rows, :] = h2.astype(BF16)
            continue
        h2_ref[rows, :] = _pack_bf16_pairs(h2)
        h_hi = h2.astype(BF16)
        h_lo = (h2 - h_hi.astype(F32)).astype(BF16)
        logits = _dot(h_hi, rwh_ref[...]) + _dot(h_lo, rwh_ref[...]) + _dot(h_hi, rwl_ref[...])
        lane = lax.broadcasted_iota(jnp.int32, logits.shape, 1).astype(F32)
        neg = jnp.float32(-jnp.inf)
        logits = jnp.where(lane < N_EXPERTS, logits, neg)
        m1 = jnp.max(logits, axis=1, keepdims=True)
        i1 = jnp.min(jnp.where(logits == m1, lane, float(LANES)), axis=1, keepdims=True)
        rest = jnp.where(lane == i1, neg, logits)
        m2 = jnp.max(rest, axis=1, keepdims=True)
        i2 = jnp.min(jnp.where(rest == m2, lane, float(LANES)), axis=1, keepdims=True)
        e2 = jnp.exp(m2 - m1)
        w1 = 1.0 / (1.0 + e2)
        comb_ref[rows, :] = (jnp.where(lane == ROUTE_E1, i1, 0.0) + jnp.where(lane == ROUTE_E2, i2, 0.0)
                             + jnp.where(lane == ROUTE_W1, w1, 0.0)
                             + jnp.where(lane == ROUTE_W2, e2 * w1, 0.0))


def _outproj(x, o_f, o_b, p, mla, mods, row_fn, gg, wo, ln2, router, tm):
    t, d = x.shape
    with_router = router is not None
    full = lambda a: pl.BlockSpec(a.shape, lambda i: (0,) * a.ndim)
    in_specs = [
        pl.BlockSpec((tm, d), lambda i: (i, 0)),
        pl.BlockSpec((tm, GLA_V_W), lambda i: (i, 0)),
        pl.BlockSpec((tm, GLA_V_W), lambda i: (i, 0)),
        pl.BlockSpec((tm, GLA_V_W), lambda i: (i, P_R // GLA_V_W)),
        pl.BlockSpec((tm, MLA_V_W), lambda i: (i, 0)),
        pl.BlockSpec((1, 6, d), lambda i: (row_fn(i), 0, 0)),
        full(gg), full(wo), full(ln2),
    ]
    args = [x, o_f, o_b, p, mla, mods, gg, wo, ln2]
    h2_shape = jax.ShapeDtypeStruct((t, d // 2), F32) if with_router else jax.ShapeDtypeStruct((t, d), BF16)
    out_specs = [pl.BlockSpec((tm, d), lambda i: (i, 0)),
                 pl.BlockSpec((tm, h2_shape.shape[1]), lambda i: (i, 0))]
    out_shape = [jax.ShapeDtypeStruct((t, d), F32), h2_shape]
    if with_router:
        in_specs += [full(router[0]), full(router[1])]
        args += list(router)
        out_specs.append(pl.BlockSpec((tm, LANES), lambda i: (i, 0)))
        out_shape.append(jax.ShapeDtypeStruct((t, LANES), F32))
    return pl.pallas_call(
        functools.partial(_outproj_kernel, with_router=with_router),
        grid=(t // tm,),
        in_specs=in_specs,
        out_specs=out_specs,
        out_shape=out_shape,
        compiler_params=_cparams(("arbitrary",)),
        name="outproj",
    )(*args)


def _ffn_kernel(*refs, with_comb, final_norm):
    h_ref, x1_ref, mod_ref = refs[:3]
    k = 3
    comb_ref = fin_ref = None
    if with_comb:
        comb_ref = refs[k]
        k += 1
    wg_ref, wu_ref, wd_ref = refs[k:k + 3]
    k += 3
    if final_norm:
        fin_ref = refs[k]
        k += 1
    o_ref, acc = refs[k:]
    e = pl.program_id(1)
    f = pl.program_id(2)

    @pl.when((e == 0) & (f == 0))
    def _():
        acc[...] = jnp.zeros(acc.shape, F32)

    h = h_ref[...]
    a = _dot(h, wg_ref[0])
    u = _dot(h, wu_ref[0])
    act = _silu(a) * u
    if with_comb:
        comb = comb_ref[...]
        lane = lax.broadcasted_iota(jnp.int32, comb.shape, 1)
        act = act * jnp.sum(jnp.where(lane == e, comb, 0.0), axis=1, keepdims=True)
    acc[...] += _dot(act.astype(BF16), wd_ref[0])

    @pl.when((e == pl.num_programs(1) - 1) & (f == pl.num_programs(2) - 1))
    def _():
        x2 = x1_ref[...] + mod_ref[0][5:6] * acc[...]
        if final_norm:
            x2 = _rms(x2, fin_ref[...])
        o_ref[...] = x2


def _ffn(h2, x1, mods, row_fn, comb, wg, wu, wd, fin_g, tm, tf):
    t, d = x1.shape
    n_e, _, ff = wg.shape
    with_comb = comb is not None
    final_norm = fin_g is not None
    in_specs = [
        pl.BlockSpec((tm, d), lambda i, e, f: (i, 0)),
        pl.BlockSpec((tm, d), lambda i, e, f: (i, 0)),
        pl.BlockSpec((1, 6, d), lambda i, e, f: (row_fn(i), 0, 0)),
    ]
    args = [h2, x1, mods]
    if with_comb:
        in_specs.append(pl.BlockSpec((tm, LANES), lambda i, e, f: (i, 0)))
        args.append(comb)
    in_specs += [
        pl.BlockSpec((1, d, tf), lambda i, e, f: (e, 0, f)),
        pl.BlockSpec((1, d, tf), lambda i, e, f: (e, 0, f)),
        pl.BlockSpec((1, tf, d), lambda i, e, f: (e, f, 0)),
    ]
    args += [wg, wu, wd]
    if final_norm:
        in_specs.append(pl.BlockSpec((1, d), lambda i, e, f: (0, 0)))
        args.append(fin_g)
    return pl.pallas_call(
        functools.partial(_ffn_kernel, with_comb=with_comb, final_norm=final_norm),
        grid=(t // tm, n_e, ff // tf),
        in_specs=in_specs,
        out_specs=pl.BlockSpec((tm, d), lambda i, e, f: (i, 0)),
        out_shape=jax.ShapeDtypeStruct((t, d), F32),
        scratch_shapes=[pltpu.VMEM((tm, d), F32)],
        compiler_params=_cparams(("arbitrary", "arbitrary", "arbitrary")),
        name="ffn",
    )(*args)


def _sc_row_gather(table, idx):
    _, w = table.shape
    b = idx.shape[0]
    n_workers = SC_CORES * SC_SUBCORES
    assert b % (n_workers * SC_GATHER_ROWS) == 0, (b, n_workers, SC_GATHER_ROWS)
    b_per_w = b // n_workers
    n_chunks = b_per_w // SC_GATHER_ROWS
    mesh = plsc.VectorSubcoreMesh(core_axis_name="c", subcore_axis_name="s",
                                  num_cores=SC_CORES, num_subcores=SC_SUBCORES)

    def body(table_hbm, idx_hbm, out_hbm, idx_a, idx_b, rows_a, rows_b, sem_a, sem_b):
        wid = lax.axis_index("s") * SC_CORES + lax.axis_index("c")
        base = wid * b_per_w
        idx_bufs, row_bufs, sems = (idx_a, idx_b), (rows_a, rows_b), (sem_a, sem_b)

        def start(ci):
            slot = ci % 2
            pltpu.sync_copy(idx_hbm.at[pl.ds(base + ci * SC_GATHER_ROWS, SC_GATHER_ROWS)], idx_bufs[slot])
            return pltpu.async_copy(table_hbm.at[idx_bufs[slot]], row_bufs[slot], sems[slot])

        pending = start(0)
        for ci in range(n_chunks):
            following = start(ci + 1) if ci + 1 < n_chunks else None
            pending.wait()
            pltpu.sync_copy(row_bufs[ci % 2], out_hbm.at[pl.ds(base + ci * SC_GATHER_ROWS, SC_GATHER_ROWS)])
            pending = following

    return pl.kernel(
        body,
        out_type=jax.ShapeDtypeStruct((b, w), F32),
        mesh=mesh,
        scratch_types=[pltpu.VMEM((SC_GATHER_ROWS,), jnp.int32)] * 2
        + [pltpu.VMEM((SC_GATHER_ROWS, w), F32)] * 2 + [pltpu.SemaphoreType.DMA] * 2,
        name="sc_row_gather",
    )(table, idx)


def _sc_row_scatter2(table, pos, n_out):
    t, w = table.shape
    n_workers = SC_CORES * SC_SUBCORES
    assert t % (n_workers * SC_GATHER_ROWS) == 0, (t, n_workers, SC_GATHER_ROWS)
    t_per_w = t // n_workers
    n_chunks = t_per_w // SC_GATHER_ROWS
    mesh = plsc.VectorSubcoreMesh(core_axis_name="c", subcore_axis_name="s",
                                  num_cores=SC_CORES, num_subcores=SC_SUBCORES)

    def body(table_hbm, pos_hbm, out_hbm, i0a, i1a, i0b, i1b, rows_a, rows_b, sem_a, sem_b):
        wid = lax.axis_index("s") * SC_CORES + lax.axis_index("c")
        base = wid * t_per_w
        idx0, idx1, row_bufs, sems = (i0a, i0b), (i1a, i1b), (rows_a, rows_b), (sem_a, sem_b)

        def drain(pair):
            if pair is not None:
                pair[0].wait()
                pair[1].wait()

        pending = [None, None]
        for ci in range(n_chunks):
            slot = ci % 2
            drain(pending[slot])
            off = base + ci * SC_GATHER_ROWS
            pltpu.sync_copy(pos_hbm.at[0, pl.ds(off, SC_GATHER_ROWS)], idx0[slot])
            pltpu.sync_copy(pos_hbm.at[1, pl.ds(off, SC_GATHER_ROWS)], idx1[slot])
            pltpu.sync_copy(table_hbm.at[pl.ds(off, SC_GATHER_ROWS)], row_bufs[slot])
            pending[slot] = (pltpu.async_copy(row_bufs[slot], out_hbm.at[idx0[slot]], sems[slot]),
                             pltpu.async_copy(row_bufs[slot], out_hbm.at[idx1[slot]], sems[slot]))
        drain(pending[0])
        drain(pending[1])

    return pl.kernel(
        body,
        out_type=jax.ShapeDtypeStruct((n_out, w), F32),
        mesh=mesh,
        scratch_types=[pltpu.VMEM((SC_GATHER_ROWS,), jnp.int32)] * 4
        + [pltpu.VMEM((SC_GATHER_ROWS, w), F32)] * 2 + [pltpu.SemaphoreType.DMA] * 2,
        name="sc_row_scatter",
    )(table, pos)


def _moe_plan_kernel(route_ref, tri_ref, utri_ref, pos_ref, cnt_ref, run_scr, off_scr):
    phase = pl.program_id(0)
    blk = pl.program_id(1)
    route = route_ref[...]
    lane = lax.broadcasted_iota(jnp.int32, route.shape, 1).astype(F32)
    oh1 = jnp.where(lane == route[:, ROUTE_E1:ROUTE_E1 + 1], 1.0, 0.0)
    oh2 = jnp.where(lane == route[:, ROUTE_E2:ROUTE_E2 + 1], 1.0, 0.0)
    oh = oh1 + oh2

    @pl.when(blk == 0)
    def _():
        run_scr[...] = jnp.zeros(run_scr.shape, F32)

    @pl.when(phase == 0)
    def _():
        run_scr[...] += jnp.sum(oh, axis=0, keepdims=True)

        @pl.when(blk == pl.num_programs(1) - 1)
        def _():
            counts = run_scr[...]
            cnt_ref[...] = counts
            tiles_per = jnp.floor((counts + (MOE_TILE - 1.0)) * (1.0 / MOE_TILE))
            tile_end = _dot(jnp.broadcast_to(tiles_per, (8, LANES)).astype(BF16), utri_ref[...])[0:1]
            off_scr[...] = (tile_end - tiles_per) * float(MOE_TILE)

    @pl.when(phase == 1)
    def _():
        incl = _dot(tri_ref[...], oh.astype(BF16))
        before = incl - oh + run_scr[...] + off_scr[...]
        p1 = jnp.sum(before * oh1, axis=1, keepdims=True)
        p2 = jnp.sum(before * oh2, axis=1, keepdims=True)
        pos = jnp.where(lane == 0.0, p1, 0.0) + jnp.where(lane == 1.0, p2, 0.0)
        pos_ref[...] = pos.astype(jnp.int32)
        run_scr[...] += incl[incl.shape[0] - 1:, :]


def _moe_plan(route, n_tiles, tm):
    t = route.shape[0]
    tri = jnp.asarray(np.tril(np.ones((tm, tm), np.float32)), dtype=BF16)
    utri = jnp.asarray(np.triu(np.ones((LANES, LANES), np.float32)), dtype=BF16)
    pos, counts = pl.pallas_call(
        _moe_plan_kernel,
        grid=(2, t // tm),
        in_specs=[
            pl.BlockSpec((tm, LANES), lambda p, i: (i, 0)),
            pl.BlockSpec((tm, tm), lambda p, i: (0, 0)),
            pl.BlockSpec((LANES, LANES), lambda p, i: (0, 0)),
        ],
        out_specs=[
            pl.BlockSpec((tm, LANES), lambda p, i: (i * p, 0)),
            pl.BlockSpec((1, LANES), lambda p, i: (0, 0)),
        ],
        out_shape=[jax.ShapeDtypeStruct((t, LANES), jnp.int32),
                   jax.ShapeDtypeStruct((1, LANES), F32)],
        scratch_shapes=[pltpu.VMEM((1, LANES), F32), pltpu.VMEM((1, LANES), F32)],
        compiler_params=_cparams(("arbitrary", "arbitrary")),
        name="moe_plan",
    )(route, tri, utri)
    counts = counts[0, :N_EXPERTS].astype(jnp.int32)
    tile_end = jnp.cumsum((counts + MOE_TILE - 1) // MOE_TILE)
    n_used = tile_end[-1]
    tile_ids = jnp.minimum(jnp.arange(n_tiles, dtype=jnp.int32), n_used - 1)
    tile_expert = jnp.sum((tile_end[None, :] <= tile_ids[:, None]).astype(jnp.int32), axis=1)
    return pos[:, :2].T, tile_expert, n_used.reshape(1)


def _moe_ffn_kernel(te_ref, nused_ref, xs_ref, wg_ref, wu_ref, wd_ref, o_ref, acc, h_scr, *, splits):
    i = pl.program_id(0)
    f = pl.program_id(1)
    last_f = pl.num_programs(1) - 1
    used = i < nused_ref[0]

    @pl.when(used)
    def _():
        @pl.when(f == 0)
        def _():
            acc[...] = jnp.zeros(acc.shape, F32)
            h_scr[...] = _unpack_bf16_pairs(xs_ref[...]).astype(BF16)

        h = h_scr[...]
        for lo, hi in splits:
            a = _dot(h, wg_ref[0, :, lo:hi].astype(BF16))
            u = _dot(h, wu_ref[0, :, lo:hi].astype(BF16))
            acc[...] += _dot((_silu(a) * u).astype(BF16), wd_ref[0, lo:hi, :].astype(BF16))

        @pl.when(f == last_f)
        def _():
            o_ref[...] = _pack_bf16_pairs(acc[...])

    @pl.when(jnp.logical_not(used) & (f == last_f))
    def _():
        o_ref[...] = jnp.zeros(o_ref.shape, F32)


def _moe_ffn(xs, tile_expert, n_used, wg, wu, wd):
    rows, half = xs.shape
    d = 2 * half
    ff = wg.shape[2]
    if ff % MOE_FF_TILE == 0:
        tf, splits = MOE_FF_TILE, MOE_FF_SPLITS
    else:
        tf, splits = ff, ((0, ff),)
    n_tiles = rows // MOE_TILE
    grid_spec = pltpu.PrefetchScalarGridSpec(
        num_scalar_prefetch=2,
        grid=(n_tiles, ff // tf),
        in_specs=[
            pl.BlockSpec((MOE_TILE, half), lambda i, f, te, nu: (i, 0)),
            pl.BlockSpec((1, d, tf), lambda i, f, te, nu: (te[i], 0, f)),
            pl.BlockSpec((1, d, tf), lambda i, f, te, nu: (te[i], 0, f)),
            pl.BlockSpec((1, tf, d), lambda i, f, te, nu: (te[i], f, 0)),
        ],
        out_specs=pl.BlockSpec((MOE_TILE, half), lambda i, f, te, nu: (i, 0)),
        scratch_shapes=[pltpu.VMEM((MOE_TILE, d), F32), pltpu.VMEM((MOE_TILE, d), BF16)],
    )
    return pl.pallas_call(
        functools.partial(_moe_ffn_kernel, splits=splits),
        grid_spec=grid_spec,
        out_shape=jax.ShapeDtypeStruct((rows, half), F32),
        compiler_params=_cparams(("arbitrary", "arbitrary")),
        name="moe_ffn",
    )(tile_expert, n_used, xs, wg, wu, wd)


def _combine_kernel(*refs, final_norm):
    x1_ref, y0_ref, y1_ref, route_ref, mod_ref = refs[:5]
    fin_ref = refs[5] if final_norm else None
    o_ref = refs[-1]
    route = route_ref[...]
    w1 = route[:, ROUTE_W1:ROUTE_W1 + 1]
    w2 = route[:, ROUTE_W2:ROUTE_W2 + 1]
    y = w1 * _unpack_bf16_pairs(y0_ref[...]) + w2 * _unpack_bf16_pairs(y1_ref[...])
    x2 = x1_ref[...] + mod_ref[0][5:6] * y
    if final_norm:
        x2 = _rms(x2, fin_ref[...])
    o_ref[...] = x2


def _combine(x1, yg, route, mods, row_fn, fin_g, tm):
    t, d = x1.shape
    nb = t // tm
    final_norm = fin_g is not None
    in_specs = [
        pl.BlockSpec((tm, d), lambda i: (i, 0)),
        pl.BlockSpec((tm, d // 2), lambda i: (i, 0)),
        pl.BlockSpec((tm, d // 2), lambda i: (i + nb, 0)),
        pl.BlockSpec((tm, LANES), lambda i: (i, 0)),
        pl.BlockSpec((1, 6, d), lambda i: (row_fn(i), 0, 0)),
    ]
    args = [x1, yg, yg, route, mods]
    if final_norm:
        in_specs.append(pl.BlockSpec((1, d), lambda i: (0, 0)))
        args.append(fin_g)
    return pl.pallas_call(
        functools.partial(_combine_kernel, final_norm=final_norm),
        grid=(nb,),
        in_specs=in_specs,
        out_specs=pl.BlockSpec((tm, d), lambda i: (i, 0)),
        out_shape=jax.ShapeDtypeStruct((t, d), F32),
        compiler_params=_cparams(("arbitrary",)),
        name="moe_combine",
    )(*args)


def _moe(h2, x1, route, mods, row_fn, wg, wu, wd, fin_g, tm):
    t = h2.shape[0]
    n_tiles = -(-2 * t // MOE_TILE) + N_EXPERTS
    pos, tile_expert, n_used = _moe_plan(route, n_tiles, _pick_tile(t, 1024))
    xs = _sc_row_scatter2(h2, pos, n_tiles * MOE_TILE)
    ys = _moe_ffn(xs, tile_expert, n_used, wg, wu, wd)
    yg = _sc_row_gather(ys, pos.reshape(-1))
    return _combine(x1, yg, route, mods, row_fn, fin_g, tm)


def _rope_partner():
    j = np.arange(MLA_ROPE)
    return np.where((j % 32) < 16, j + 16, j - 16)


def _prep_in_weight(w):
    d = w.shape[0]
    cols = [w[:, 0:1024], w[:, 1056:1568], w[:, 1568:1824], w[:, 1824:1952], w[:, 1952:2016],
            w[:, 1024:1056], jnp.zeros((d, P_WIDTH - 2016), w.dtype)]
    return jnp.concatenate(cols, axis=1).astype(BF16)


def _prep_gate_weight(w_g2, b_g2):
    ws = []
    for z, off in ((0, MISC_GF), (1, MISC_GB)):
        ws.append(jnp.zeros((LANES, GLA_QK_W), F32).at[off:off + GLA_GATE_RANK].set(w_g2[z]))
    return jnp.stack(ws).astype(BF16), b_g2.reshape(2, 1, GLA_QK_W)


def _prep_mla_weights(w_uq, w_ukv):
    partner = _rope_partner()
    wq = w_uq.reshape(MLA_Q_RANK, MLA_HEADS, MLA_QK)
    wqn = wq[:, :, :MLA_NOPE].reshape(MLA_Q_RANK, MLA_HEADS * MLA_NOPE)
    rope = wq[:, :, MLA_NOPE:]
    pad = jnp.zeros((MLA_Q_RANK, MLA_HEADS, LANES - MLA_ROPE), w_uq.dtype)
    wqr = jnp.concatenate([rope, pad], axis=2).reshape(MLA_Q_RANK, MLA_HEADS * LANES)
    wqs = jnp.concatenate([rope[:, :, partner], pad], axis=2).reshape(MLA_Q_RANK, MLA_HEADS * LANES)
    wkv = w_ukv.reshape(MLA_KV_RANK, MLA_HEADS, MLA_NOPE + MLA_V)
    wknt = wkv[:, :, :MLA_NOPE].reshape(MLA_KV_RANK, MLA_HEADS * MLA_NOPE).T
    wv = wkv[:, :, MLA_NOPE:].reshape(MLA_KV_RANK, MLA_HEADS * MLA_V)
    perm = np.zeros((LANES, LANES), np.float32)
    perm[partner, np.arange(MLA_ROPE)] = 1.0
    eye = np.eye(MLA_ROPE, LANES, dtype=np.float32)
    return (wqn.astype(BF16), wqr.astype(BF16), wqs.astype(BF16), wknt.astype(BF16),
            wv.astype(BF16), jnp.asarray(perm, BF16), jnp.asarray(eye, BF16))


def _rope_tables(n_tok):
    rows = n_tok // GRID_W
    row = np.repeat(np.arange(rows, dtype=np.float32), GRID_W)
    col = np.tile(np.arange(GRID_W, dtype=np.float32), rows)
    nfreq = MLA_ROPE // 4
    inv = np.float32(ROPE_BASE) ** (-np.arange(nfreq, dtype=np.float32) / np.float32(nfreq))
    ar = (row[:, None] * inv).astype(np.float32)
    ac = (col[:, None] * inv).astype(np.float32)
    zero = np.zeros((n_tok, LANES - MLA_ROPE), np.float32)
    cos = np.concatenate([np.cos(ar), np.cos(ar), np.cos(ac), np.cos(ac), zero], axis=1)
    sin = np.concatenate([-np.sin(ar), np.sin(ar), -np.sin(ac), np.sin(ac), zero], axis=1)
    return jnp.asarray(cos, F32), jnp.asarray(sin, F32)


def _identity_tables(n_tok):
    cos = jnp.concatenate([jnp.ones((n_tok, MLA_ROPE), F32),
                           jnp.zeros((n_tok, LANES - MLA_ROPE), F32)], axis=1)
    return cos, jnp.zeros((n_tok, LANES), F32)


def _pick_tile(n, pref):
    t = min(n, pref)
    while n % t:
        t //= 2
    return t


def _pick_ff_tile(ff):
    best = LANES
    for m in range(1, ff // LANES + 1):
        if ff % (m * LANES) == 0 and m * LANES <= 1408:
            best = m * LANES
    return best


@jax.jit
def _forward(x, c, ctx, c_ctx, w_mod, b_mod, ln1_g, ln2_g, w_in, w_gla_g2, b_gla_g2, gla_norm_g,
             mla_q_norm_g, w_uq, mla_kv_norm_g, w_ukv, w_out, ffn_w_gate, ffn_w_up, ffn_w_down,
             router_w, exp_w_gate, exp_w_up, exp_w_down, final_norm_g):
    batch, seq, d = x.shape
    n_ctx = ctx.shape[1]
    depth = w_mod.shape[0]

    cvec = jnp.zeros((8, d), F32).at[:batch].set(c).at[batch].set(c_ctx)
    mods_all = _modulation(cvec, w_mod, b_mod).reshape(depth, 8, 6, d)

    xl = x.reshape(batch * seq, d)
    xc = ctx.reshape(batch * n_ctx, d)

    tm_l = _pick_tile(seq, 512)
    tm_c = _pick_tile(n_ctx, 256)
    tk_l = _pick_tile(seq, 1024)
    cb_l = _pick_tile(seq, 256)
    cb_c = _pick_tile(n_ctx, 256)
    row_l = lambda tm: (lambda i: i // (seq // tm))
    row_c = lambda i: batch

    rope_l = _rope_tables(seq)
    rope_c = _identity_tables(tm_c)
    zero_state = jnp.zeros((2, batch, GLA_QK_W, GLA_DV), F32)

    for i in range(depth):
        need_ctx = i < depth - 1
        last = i == depth - 1
        mods = mods_all[i]
        ln1 = ln1_g[i].reshape(1, d)
        ln2 = ln2_g[i].reshape(1, d)
        w_in_r = _prep_in_weight(w_in[i])
        gates = _prep_gate_weight(w_gla_g2[i], b_gla_g2[i])
        mla_w = _prep_mla_weights(w_uq[i], w_ukv[i])
        qg = mla_q_norm_g[i].reshape(1, MLA_Q_RANK)
        kvg = mla_kv_norm_g[i].reshape(1, MLA_KV_RANK)
        gg = gla_norm_g[i].reshape(1, GLA_DV)
        wo = w_out[i].astype(BF16)

        p_l = _inproj(xl, mods, row_l(tm_l), ln1, w_in_r, tm_l)
        p_c = _inproj(xc, mods, row_c, ln1, w_in_r, tm_c)

        oc_f, oc_b, s_ctx = _gla(p_c, *gates, zero_state, batch=batch, cb=cb_c)
        ol_f, ol_b, _ = _gla(p_l, *gates, s_ctx, batch=batch, cb=cb_l)

        q_l, kt_l, v_l = _mlaprep(p_l, *rope_l, qg, kvg, mla_w, batch=batch, tm=tk_l)
        q_c, kt_c, v_c = _mlaprep(p_c, *rope_c, qg, kvg, mla_w, batch=batch, tm=tm_c)
        m_l = _attention_pipelined(q_l, kt_l, v_l, kt_c, v_c, tq=tk_l, n_sub=1)
        m_l = m_l.reshape(batch * seq, MLA_V_W)

        if i % 2 == 0:
            j = i // 2
            router = None
            wg = ffn_w_gate[j].astype(BF16)
            wu = ffn_w_up[j].astype(BF16)
            wd = ffn_w_down[j].astype(BF16)
        else:
            j = i // 2
            rw = jnp.zeros((d, LANES), F32).at[:, :N_EXPERTS].set(router_w[j])
            rw_hi = rw.astype(BF16)
            router = (rw_hi, (rw - rw_hi.astype(F32)).astype(BF16))
            wg, wu, wd = exp_w_gate[j], exp_w_up[j], exp_w_down[j]
        fin = final_norm_g.reshape(1, d) if last else None

        if router is None:
            xl = _outproj_ffn(xl, ol_f, ol_b, p_l, m_l, mods, row_l(tm_l), gg, wo, ln2,
                              wg, wu, wd, fin, tm_l)
        else:
            outs = _outproj(xl, ol_f, ol_b, p_l, m_l, mods, row_l(tm_l), gg, wo, ln2, router, tm_l)
            xl = _moe(outs[1], outs[0], outs[2], mods, row_l(tm_l), wg, wu, wd, fin, tm_l)

        if need_ctx:
            m_c = _attention(q_c, [(kt_c, v_c)], tq=tm_c, n_sub=1).reshape(batch * n_ctx, MLA_V_W)
            if router is None:
                xc = _outproj_ffn(xc, oc_f, oc_b, p_c, m_c, mods, row_c, gg, wo, ln2,
                                  wg, wu, wd, None, tm_c)
            else:
                outs_c = _outproj(xc, oc_f, oc_b, p_c, m_c, mods, row_c, gg, wo, ln2, router, tm_c)
                tm_fc = _pick_tile(batch * n_ctx, 512)
                r_c = outs_c[2]
                lane = jnp.arange(LANES, dtype=F32)[None, :]
                comb_c = (jnp.where(lane == r_c[:, ROUTE_E1:ROUTE_E1 + 1], r_c[:, ROUTE_W1:ROUTE_W1 + 1], 0.0)
                          + jnp.where(lane == r_c[:, ROUTE_E2:ROUTE_E2 + 1], r_c[:, ROUTE_W2:ROUTE_W2 + 1], 0.0))
                bits = lax.bitcast_convert_type(outs_c[1], jnp.uint32)
                h2_c = jnp.concatenate([lax.bitcast_convert_type(bits << 16, F32),
                                        lax.bitcast_convert_type(bits & jnp.uint32(0xFFFF0000), F32)], axis=1)
                xc = _ffn(h2_c.astype(BF16), outs_c[0], mods, row_c, comb_c,
                          wg.astype(BF16), wu.astype(BF16), wd.astype(BF16), None, tm_fc,
                          _pick_ff_tile(wg.shape[2]))

    return xl.reshape(batch, seq, d)


def kernel(x, c, ctx, c_ctx, w_mod, b_mod, ln1_g, ln2_g, w_in, w_gla_g2, b_gla_g2, gla_norm_g,
           mla_q_norm_g, w_uq, mla_kv_norm_g, w_ukv, w_out, ffn_w_gate, ffn_w_up, ffn_w_down,
           router_w, exp_w_gate, exp_w_up, exp_w_down, final_norm_g):
    return _forward(x, c, ctx, c_ctx, w_mod, b_mod, ln1_g, ln2_g, w_in, w_gla_g2, b_gla_g2,
                    gla_norm_g, mla_q_norm_g, w_uq, mla_kv_norm_g, w_ukv, w_out, ffn_w_gate,
                    ffn_w_up, ffn_w_down, router_w, exp_w_gate, exp_w_up, exp_w_down, final_norm_g)
```

```python
import functools

import numpy as np
import jax
import jax.numpy as jnp
from jax import lax
from jax.experimental import pallas as pl
from jax.experimental.pallas import tpu as pltpu
from jax.experimental.pallas import tpu_sc as plsc

F32 = jnp.float32
BF16 = jnp.bfloat16

D_MODEL = 1024
EPS = 1e-6
GRID_W = 64

GLA_HEADS = 4
GLA_DK = 64
GLA_DV = 128
GLA_GATE_RANK = 16
GLA_GATE_NORM = 16.0
GLA_CHUNK = 64
GLA_QK_W = GLA_HEADS * GLA_DK
GLA_V_W = GLA_HEADS * GLA_DV
GLA_EXP_CLAMP = 80.0

MLA_HEADS = 4
MLA_NOPE = 128
MLA_ROPE = 64
MLA_V = 128
MLA_QK = MLA_NOPE + MLA_ROPE
MLA_Q_RANK = 256
MLA_KV_RANK = 128
MLA_SCALE = MLA_QK ** -0.5
MLA_Q_SCALE = MLA_SCALE * 1.4426950408889634
MLA_V_W = MLA_HEADS * MLA_V
MLA_V_EXT = 2 * MLA_V
ROPE_BASE = 10000.0

N_EXPERTS = 8
LANES = 128
ROUTE_E1, ROUTE_E2, ROUTE_W1, ROUTE_W2 = 0, 1, 2, 3

SC_CORES = 2
SC_SUBCORES = 16
SC_GATHER_ROWS = 64
MOE_TILE = 1024
MOE_FF_TILE = 512
MOE_FF_SPLITS = ((0, 256), (256, 512))

P_Q, P_K, P_V, P_R, P_CQ, P_CKV, P_MISC = 0, 256, 512, 1024, 1536, 1792, 1920
P_WIDTH = 2048
MISC_KR, MISC_GF, MISC_GB = 0, 64, 80

VMEM_LIMIT = 56 * 1024 * 1024

TOKEN_TILE = 512
CTX_TILE = 256
ATTN_TILE = 1024
GLA_BLOCK = 256
DENSE_FF_GROUP = 1024
FFN_MAX_FF_TILE = 1408


def _cparams(sem):
    return pltpu.CompilerParams(dimension_semantics=sem, vmem_limit_bytes=VMEM_LIMIT)


def _rms(x, g):
    return x * lax.rsqrt(jnp.mean(x * x, axis=-1, keepdims=True) + EPS) * g


def _silu(x):
    return x / (1.0 + jnp.exp(-x))


def _dot(a, b):
    return jnp.dot(a, b, preferred_element_type=F32)


def _dot_nt(a, b):
    return lax.dot_general(a, b, (((1,), (1,)), ((), ())), preferred_element_type=F32)


def _dot_tn(a, b):
    return lax.dot_general(a, b, (((0,), (0,)), ((), ())), preferred_element_type=F32)


def _pack_bf16_pairs(x):
    w = x.shape[1] // 2
    words = pltpu.pack_elementwise([x[:, :w], x[:, w:]], packed_dtype=BF16)
    return lax.bitcast_convert_type(words, F32)


def _unpack_bf16_pairs(p):
    words = lax.bitcast_convert_type(p, jnp.int32)
    lo = pltpu.unpack_elementwise(words, index=0, packed_dtype=BF16, unpacked_dtype=F32)
    hi = pltpu.unpack_elementwise(words, index=1, packed_dtype=BF16, unpacked_dtype=F32)
    return jnp.concatenate([lo, hi], axis=1)


def _mod_kernel(c_ref, w_ref, b_ref, o_ref):
    s = _silu(c_ref[...]).astype(BF16)
    o_ref[0] = _dot(s, w_ref[0].astype(BF16)) + b_ref[0]


def _modulation(cvec, w_mod, b_mod):
    depth, d, n = w_mod.shape
    tn = 1536
    return pl.pallas_call(
        _mod_kernel,
        grid=(depth, n // tn),
        in_specs=[
            pl.BlockSpec((8, d), lambda l, j: (0, 0)),
            pl.BlockSpec((1, d, tn), lambda l, j: (l, 0, j)),
            pl.BlockSpec((1, 1, tn), lambda l, j: (l, 0, j)),
        ],
        out_specs=pl.BlockSpec((1, 8, tn), lambda l, j: (l, 0, j)),
        out_shape=jax.ShapeDtypeStruct((depth, 8, n), F32),
        compiler_params=_cparams(("arbitrary", "arbitrary")),
        name="modulation",
    )(cvec, w_mod, b_mod.reshape(depth, 1, n))


def _inproj_kernel(x_ref, mod_ref, g_ref, w_ref, o_ref):
    m = mod_ref[0]
    tm = x_ref.shape[0]
    n_groups = 2 if tm % 32 == 0 else 1
    for g in range(n_groups):
        rows = slice(g * tm // n_groups, (g + 1) * tm // n_groups)
        h = _rms(x_ref[rows, :], g_ref[...]) * (1.0 + m[1:2]) + m[0:1]
        o_ref[rows, :] = _dot(h.astype(BF16), w_ref[...]).astype(BF16)


def _inproj(x, mods, row_fn, ln_g, w, tm):
    t, d = x.shape
    return pl.pallas_call(
        _inproj_kernel,
        grid=(t // tm,),
        in_specs=[
            pl.BlockSpec((tm, d), lambda i: (i, 0)),
            pl.BlockSpec((1, 6, d), lambda i: (row_fn(i), 0, 0)),
            pl.BlockSpec((1, d), lambda i: (0, 0)),
            pl.BlockSpec((d, P_WIDTH), lambda i: (0, 0)),
        ],
        out_specs=pl.BlockSpec((tm, P_WIDTH), lambda i: (i, 0)),
        out_shape=jax.ShapeDtypeStruct((t, P_WIDTH), BF16),
        compiler_params=_cparams(("arbitrary",)),
        name="inproj",
    )(x, mods, ln_g, w)


def _gla_direction(q_ref, k_ref, v_ref, misc_ref, wg, bg, tri, s_scr, o_ref, *, reverse, n_chunks):
    c_len = GLA_CHUNK
    pre = _dot(misc_ref[...], wg) + bg
    g = (jnp.minimum(pre, 0.0) - jnp.log(1.0 + jnp.exp(-jnp.abs(pre)))) * (1.0 / GLA_GATE_NORM)
    g_hi = g.astype(BF16)
    g_lo = (g - g_hi.astype(F32)).astype(BF16)
    cum = _dot(tri, g_hi) + _dot(tri, g_lo)
    tot_rows = jnp.concatenate(
        [cum[c * c_len:c * c_len + 1] if reverse else cum[(c + 1) * c_len - 1:(c + 1) * c_len]
         for c in range(n_chunks)] + [jnp.zeros((8 - n_chunks, GLA_QK_W), F32)], axis=0)
    t_hi = tot_rows.astype(BF16)
    t_lo = (tot_rows - t_hi.astype(F32)).astype(BF16)
    eye = (lax.broadcasted_iota(jnp.int32, (GLA_QK_W, GLA_QK_W), 0)
           == lax.broadcasted_iota(jnp.int32, (GLA_QK_W, GLA_QK_W), 1))
    eye = jnp.where(eye, 1.0, 0.0).astype(BF16)
    tot_cols = _dot_nt(eye, t_hi) + _dot_nt(eye, t_lo)

    lane = lax.broadcasted_iota(jnp.int32, (c_len, GLA_QK_W), 1)
    head_masks = [(lane >= h * GLA_DK) & (lane < (h + 1) * GLA_DK) for h in range(GLA_HEADS)]
    row = lax.broadcasted_iota(jnp.int32, (GLA_HEADS * c_len, c_len), 0) % c_len
    col = lax.broadcasted_iota(jnp.int32, (GLA_HEADS * c_len, c_len), 1)
    pair_mask = (col >= row) if reverse else (col <= row)

    def stack_heads(a):
        return jnp.concatenate([jnp.where(mk, a, 0.0) for mk in head_masks], axis=0).astype(BF16)

    def step(c):
        sl = slice(c * c_len, (c + 1) * c_len)
        xc = cum[sl]
        tot = tot_rows[c:c + 1]
        ref = xc[c_len // 2:c_len // 2 + 1]
        qc = q_ref[sl, :].astype(F32) * (GLA_DK ** -0.5)
        kc = k_ref[sl, :].astype(F32)
        vc = v_ref[sl, :]
        q_mid = qc * jnp.exp(jnp.minimum(xc - ref, GLA_EXP_CLAMP))
        k_mid = (kc * jnp.exp(jnp.minimum(ref - xc, GLA_EXP_CLAMP))).astype(BF16)
        q_dec = qc * jnp.exp(xc)
        k_dec = kc * jnp.exp(tot - xc)

        attn = _dot_nt(stack_heads(q_mid), k_mid)
        attn = jnp.where(pair_mask, attn, 0.0).astype(BF16)
        s_prev = s_scr[...]
        o_inter = _dot(stack_heads(q_dec), s_prev.astype(BF16))
        kv = []
        for h in range(GLA_HEADS):
            rs = slice(h * c_len, (h + 1) * c_len)
            vs = slice(h * GLA_DV, (h + 1) * GLA_DV)
            o_h = o_inter[rs] + _dot(attn[rs], vc[:, vs])
            o_ref[sl, vs] = o_h.astype(BF16)
            kv.append(_dot_tn(k_dec[:, h * GLA_DK:(h + 1) * GLA_DK].astype(BF16), vc[:, vs]))
        s_scr[...] = s_prev * jnp.exp(tot_cols[:, c:c + 1]) + jnp.concatenate(kv, axis=0)

    return step


def _gla_kernel(qf_ref, kf_ref, vf_ref, mf_ref, qb_ref, kb_ref, vb_ref, mb_ref,
                wg_ref, bg_ref, tri_ref, s0_ref, of_ref, ob_ref, sfin_ref, sf_scr, sb_scr, *, n_chunks):
    blk = pl.program_id(1)

    @pl.when(blk == 0)
    def _():
        sf_scr[...] = s0_ref[0, 0]
        sb_scr[...] = s0_ref[1, 0]

    fwd = _gla_direction(qf_ref, kf_ref, vf_ref, mf_ref, wg_ref[0], bg_ref[0], tri_ref[0],
                         sf_scr, of_ref, reverse=False, n_chunks=n_chunks)
    bwd = _gla_direction(qb_ref, kb_ref, vb_ref, mb_ref, wg_ref[1], bg_ref[1], tri_ref[1],
                         sb_scr, ob_ref, reverse=True, n_chunks=n_chunks)
    for c in range(n_chunks):
        fwd(c)
        bwd(n_chunks - 1 - c)

    @pl.when(blk == pl.num_programs(1) - 1)
    def _():
        sfin_ref[0, 0] = sf_scr[...]
        sfin_ref[1, 0] = sb_scr[...]


def _block_diag_tri(n_chunks):
    c = GLA_CHUNK
    eye = np.eye(n_chunks, dtype=np.float32)
    lower = np.kron(eye, np.tril(np.ones((c, c), np.float32)))
    upper = np.kron(eye, np.triu(np.ones((c, c), np.float32)))
    return jnp.asarray(np.stack([lower, upper]), dtype=BF16)


def _gla(p, wg, bg, s0, *, batch, cb):
    t_all = p.shape[0]
    nblk = t_all // batch // cb
    n_chunks = cb // GLA_CHUNK
    assert n_chunks <= 8

    fw = lambda b, i: b * nblk + i
    bw = lambda b, i: b * nblk + (nblk - 1 - i)
    full = lambda a: pl.BlockSpec(a.shape, lambda b, i: (0,) * a.ndim)
    tri = _block_diag_tri(n_chunks)

    def token_specs(tok):
        return [
            pl.BlockSpec((cb, GLA_QK_W), lambda b, i: (tok(b, i), P_Q // GLA_QK_W)),
            pl.BlockSpec((cb, GLA_QK_W), lambda b, i: (tok(b, i), P_K // GLA_QK_W)),
            pl.BlockSpec((cb, GLA_V_W), lambda b, i: (tok(b, i), P_V // GLA_V_W)),
            pl.BlockSpec((cb, LANES), lambda b, i: (tok(b, i), P_MISC // LANES)),
        ]

    state_spec = pl.BlockSpec((2, 1, GLA_QK_W, GLA_DV), lambda b, i: (0, b, 0, 0))
    return pl.pallas_call(
        functools.partial(_gla_kernel, n_chunks=n_chunks),
        grid=(batch, nblk),
        in_specs=token_specs(fw) + token_specs(bw) + [full(wg), full(bg), full(tri), state_spec],
        out_specs=[
            pl.BlockSpec((cb, GLA_V_W), lambda b, i: (fw(b, i), 0)),
            pl.BlockSpec((cb, GLA_V_W), lambda b, i: (bw(b, i), 0)),
            state_spec,
        ],
        out_shape=[
            jax.ShapeDtypeStruct((t_all, GLA_V_W), BF16),
            jax.ShapeDtypeStruct((t_all, GLA_V_W), BF16),
            jax.ShapeDtypeStruct((2, batch, GLA_QK_W, GLA_DV), F32),
        ],
        scratch_shapes=[pltpu.VMEM((GLA_QK_W, GLA_DV), F32), pltpu.VMEM((GLA_QK_W, GLA_DV), F32)],
        compiler_params=_cparams(("arbitrary", "arbitrary")),
        name="gla",
    )(p, p, p, p, p, p, p, p, wg, bg, tri, s0)


def _mlaprep_kernel(cq_ref, ckv_ref, misc_ref, cos_ref, sin_ref, qg_ref, kvg_ref,
                    wqn_ref, wqr_ref, wqs_ref, wknt_ref, wv_ref, perm_ref, eye_ref,
                    q_ref, kt_ref, v_ref):
    cos = cos_ref[...]
    sin = sin_ref[...]
    cqn = _rms(cq_ref[...].astype(F32), qg_ref[...]).astype(BF16)
    qn = _dot(cqn, wqn_ref[...])
    qr = _dot(cqn, wqr_ref[...])
    qs = _dot(cqn, wqs_ref[...])
    for h in range(MLA_HEADS):
        ls = slice(h * LANES, (h + 1) * LANES)
        q_ref[0, h, :, 0:MLA_NOPE] = (qn[:, ls] * MLA_Q_SCALE).astype(BF16)
        rot = qr[:, ls] * cos + qs[:, ls] * sin
        q_ref[0, h, :, MLA_NOPE:MLA_QK] = (rot[:, 0:MLA_ROPE] * MLA_Q_SCALE).astype(BF16)

    ckvn = _rms(ckv_ref[...].astype(F32), kvg_ref[...]).astype(BF16)
    knt = _dot_nt(wknt_ref[...], ckvn)
    vv = _dot(ckvn, wv_ref[...])
    misc = misc_ref[...]
    kr = misc.astype(F32) * cos + _dot(misc, perm_ref[...]) * sin
    krt = _dot_nt(eye_ref[...], kr.astype(BF16)).astype(BF16)
    for h in range(MLA_HEADS):
        kt_ref[0, h, 0, 0:MLA_NOPE, :] = knt[h * MLA_NOPE:(h + 1) * MLA_NOPE].astype(BF16)
        kt_ref[0, h, 0, MLA_NOPE:MLA_QK, :] = krt
        v_ref[0, h, :, 0:MLA_V] = vv[:, h * MLA_V:(h + 1) * MLA_V].astype(BF16)
        v_ref[0, h, :, MLA_V:MLA_V_EXT] = jnp.ones((vv.shape[0], MLA_V), BF16)


def _mlaprep(p, cos, sin, qg, kvg, wts, *, batch, tm):
    t_all = p.shape[0]
    t = t_all // batch
    nb = t // tm
    ntab = cos.shape[0] // tm
    wqn, wqr, wqs, wknt, wv, perm, eye = wts
    full = lambda a: pl.BlockSpec(a.shape, lambda b, i: (0,) * a.ndim)
    return pl.pallas_call(
        _mlaprep_kernel,
        grid=(batch, nb),
        in_specs=[
            pl.BlockSpec((tm, MLA_Q_RANK), lambda b, i: (b * nb + i, P_CQ // MLA_Q_RANK)),
            pl.BlockSpec((tm, MLA_KV_RANK), lambda b, i: (b * nb + i, P_CKV // MLA_KV_RANK)),
            pl.BlockSpec((tm, LANES), lambda b, i: (b * nb + i, P_MISC // LANES)),
            pl.BlockSpec((tm, LANES), lambda b, i: (i % ntab, 0)),
            pl.BlockSpec((tm, LANES), lambda b, i: (i % ntab, 0)),
            full(qg), full(kvg), full(wqn), full(wqr), full(wqs), full(wknt), full(wv),
            full(perm), full(eye),
        ],
        out_specs=[
            pl.BlockSpec((1, MLA_HEADS, tm, MLA_QK), lambda b, i: (b, 0, i, 0)),
            pl.BlockSpec((1, MLA_HEADS, 1, MLA_QK, tm), lambda b, i: (b, 0, i, 0, 0)),
            pl.BlockSpec((1, MLA_HEADS, tm, MLA_V_EXT), lambda b, i: (b, 0, i, 0)),
        ],
        out_shape=[
            jax.ShapeDtypeStruct((batch, MLA_HEADS, t, MLA_QK), BF16),
            jax.ShapeDtypeStruct((batch, MLA_HEADS, nb, MLA_QK, tm), BF16),
            jax.ShapeDtypeStruct((batch, MLA_HEADS, t, MLA_V_EXT), BF16),
        ],
        compiler_params=_cparams(("arbitrary", "arbitrary")),
        name="mlaprep",
    )(p, p, p, cos, sin, qg, kvg, wqn, wqr, wqs, wknt, wv, perm, eye)


def _attn_kernel(*refs, n_seg, n_sub):
    q_ref = refs[0]
    kt_refs = refs[1:1 + 2 * n_seg:2]
    v_refs = refs[2:2 + 2 * n_seg:2]
    o_ref = refs[1 + 2 * n_seg]
    m_scr, acc_scr = refs[2 + 2 * n_seg:]

    rows_per_sub = q_ref.shape[2] // n_sub
    m_scr[...] = jnp.full(m_scr.shape, -jnp.inf, F32)
    acc_scr[...] = jnp.zeros(acc_scr.shape, F32)

    for kt_ref, v_ref in zip(kt_refs, v_refs):
        n_blocks, tk = kt_ref.shape[2], kt_ref.shape[4]

        def step(j, carry, kt_ref=kt_ref, v_ref=v_ref, tk=tk):
            kt = kt_ref[0, 0, j]
            v_blk = v_ref[0, 0, pl.ds(pl.multiple_of(j * tk, tk), tk), :]
            for u in range(n_sub):
                rows = slice(u * rows_per_sub, (u + 1) * rows_per_sub)
                s = _dot(q_ref[0, 0, rows, :], kt)
                m_prev = m_scr[rows, :]
                m_next = jnp.maximum(m_prev, jnp.max(s, axis=1, keepdims=True))
                p = jnp.exp2((s - jnp.concatenate([m_next] * (tk // LANES), axis=1)).astype(BF16))
                alpha = jnp.exp2(m_prev - m_next)
                acc_scr[rows, :] = (jnp.concatenate([alpha] * (MLA_V_EXT // LANES), axis=1)
                                    * acc_scr[rows, :] + _dot(p, v_blk))
                m_scr[rows, :] = m_next
            return carry

        lax.fori_loop(0, n_blocks, step, 0)

    o_ref[0] = (acc_scr[:, 0:MLA_V] / acc_scr[:, MLA_V:MLA_V_EXT]).astype(BF16)


def _attn_pipe_kernel(q_ref, kt_ref, v_ref, ktt_ref, vt_ref, o_ref,
                      m_scr, acc_scr, s0_scr, s1_scr, st_scr, *, n_sub):
    n_blocks, tk = kt_ref.shape[2], kt_ref.shape[4]
    tq = m_scr.shape[0]
    n_q = q_ref.shape[2] // tq
    rows_per_sub = tq // n_sub
    subs = [slice(u * rows_per_sub, (u + 1) * rows_per_sub) for u in range(n_sub)]
    bufs = (s0_scr, s1_scr)

    def reset():
        m_scr[...] = jnp.full(m_scr.shape, -jnp.inf, F32)
        acc_scr[...] = jnp.zeros(acc_scr.shape, F32)

    def scores(qi, kt, s_ref):
        for rows in subs:
            q_rows = pl.ds(pl.multiple_of(qi * tq + rows.start, rows_per_sub), rows_per_sub)
            s_ref[rows, :] = _dot(q_ref[0, 0, q_rows, :], kt)

    def softmax_pv(s_ref, v_blk):
        width = s_ref.shape[1]
        for rows in subs:
            s = s_ref[rows, :]
            m_prev = m_scr[rows, :]
            m_next = jnp.maximum(m_prev, jnp.max(s, axis=1, keepdims=True))
            p = jnp.exp2((s - jnp.concatenate([m_next] * (width // LANES), axis=1)).astype(BF16))
            alpha = jnp.exp2(m_prev - m_next)
            acc_scr[rows, :] = (jnp.concatenate([alpha] * (MLA_V_EXT // LANES), axis=1)
                                * acc_scr[rows, :] + _dot(p, v_blk))
            m_scr[rows, :] = m_next

    reset()
    scores(0, kt_ref[0, 0, 0], bufs[0])

    def query_block(qi, carry):
        for j in range(n_blocks):
            if j + 1 < n_blocks:
                scores(qi, kt_ref[0, 0, j + 1], bufs[(j + 1) % 2])
            else:
                scores(qi, ktt_ref[0, 0, 0], st_scr)
            softmax_pv(bufs[j % 2], v_ref[0, 0, j * tk:(j + 1) * tk, :])
        scores(jnp.minimum(qi + 1, n_q - 1), kt_ref[0, 0, 0], bufs[0])
        softmax_pv(st_scr, vt_ref[0, 0])
        o_rows = pl.ds(pl.multiple_of(qi * tq, tq), tq)
        o_ref[0, o_rows, :] = (acc_scr[:, 0:MLA_V] / acc_scr[:, MLA_V:MLA_V_EXT]).astype(BF16)
        reset()
        return carry

    lax.fori_loop(0, n_q, query_block, 0)


def _attention_pipelined(q, kt, v, kt_tail, v_tail, *, tq, n_sub):
    b, h, t, dqk = q.shape
    tk, tt = kt.shape[4], kt_tail.shape[4]
    assert kt_tail.shape[2] == 1
    return pl.pallas_call(
        functools.partial(_attn_pipe_kernel, n_sub=n_sub),
        grid=(b, h),
        in_specs=[
            pl.BlockSpec((1, 1, t, dqk), lambda bi, hi: (bi, hi, 0, 0)),
            pl.BlockSpec((1, 1) + kt.shape[2:], lambda bi, hi: (bi, hi, 0, 0, 0)),
            pl.BlockSpec((1, 1) + v.shape[2:], lambda bi, hi: (bi, hi, 0, 0)),
            pl.BlockSpec((1, 1) + kt_tail.shape[2:], lambda bi, hi: (bi, hi, 0, 0, 0)),
            pl.BlockSpec((1, 1) + v_tail.shape[2:], lambda bi, hi: (bi, hi, 0, 0)),
        ],
        out_specs=pl.BlockSpec((1, t, MLA_V), lambda bi, hi: (bi, 0, hi)),
        out_shape=jax.ShapeDtypeStruct((b, t, h * MLA_V), BF16),
        scratch_shapes=[pltpu.VMEM((tq, LANES), F32), pltpu.VMEM((tq, MLA_V_EXT), F32),
                        pltpu.VMEM((tq, tk), F32), pltpu.VMEM((tq, tk), F32),
                        pltpu.VMEM((tq, tt), F32)],
        compiler_params=_cparams(("arbitrary", "arbitrary")),
        name="mla_attention_pipe",
    )(q, kt, v, kt_tail, v_tail)


def _attention(q, segs, *, tq, n_sub):
    b, h, t, dqk = q.shape
    in_specs = [pl.BlockSpec((1, 1, tq, dqk), lambda bi, hi, qi: (bi, hi, qi, 0))]
    args = [q]
    for kt, v in segs:
        in_specs.append(pl.BlockSpec((1, 1) + kt.shape[2:], lambda bi, hi, qi: (bi, hi, 0, 0, 0)))
        in_specs.append(pl.BlockSpec((1, 1) + v.shape[2:], lambda bi, hi, qi: (bi, hi, 0, 0)))
        args += [kt, v]
    return pl.pallas_call(
        functools.partial(_attn_kernel, n_seg=len(segs), n_sub=n_sub),
        grid=(b, h, t // tq),
        in_specs=in_specs,
        out_specs=pl.BlockSpec((1, tq, MLA_V), lambda bi, hi, qi: (bi, qi, hi)),
        out_shape=jax.ShapeDtypeStruct((b, t, h * MLA_V), BF16),
        scratch_shapes=[pltpu.VMEM((tq, LANES), F32), pltpu.VMEM((tq, MLA_V_EXT), F32)],
        compiler_params=_cparams(("arbitrary", "arbitrary", "arbitrary")),
        name="mla_attention",
    )(*args)


def _mix_residual_norm(x_ref, of_ref, ob_ref, r_ref, mla_ref, m, gg_ref, wo_ref, ln2_ref,
                       rows=slice(None)):
    o = of_ref[rows, :].astype(F32) + ob_ref[rows, :].astype(F32)
    gg = gg_ref[...]
    y = jnp.concatenate(
        [_rms(o[:, h * GLA_DV:(h + 1) * GLA_DV], gg) for h in range(GLA_HEADS)], axis=1)
    mix = (y * _silu(r_ref[rows, :].astype(F32))).astype(BF16)
    yo = _dot(mix, wo_ref[0:GLA_V_W, :]) + _dot(mla_ref[rows, :], wo_ref[GLA_V_W:, :])
    x1 = x_ref[rows, :] + m[2:3] * yo
    h2 = _rms(x1, ln2_ref[...]) * (1.0 + m[4:5]) + m[3:4]
    return x1, h2


def _outproj_ffn_kernel(*refs, groups, final_norm):
    (x_ref, of_ref, ob_ref, r_ref, mla_ref, mod_ref, gg_ref, wo_ref, ln2_ref,
     wg_ref, wu_ref, wd_ref) = refs[:12]
    fin_ref = refs[12] if final_norm else None
    o_ref = refs[-1]
    m = mod_ref[0]
    x1, h2 = _mix_residual_norm(x_ref, of_ref, ob_ref, r_ref, mla_ref, m, gg_ref, wo_ref, ln2_ref)
    h = h2.astype(BF16)
    y = None
    for lo, hi in groups:
        a = _dot(h, wg_ref[:, lo:hi])
        u = _dot(h, wu_ref[:, lo:hi])
        part = _dot((_silu(a) * u).astype(BF16), wd_ref[lo:hi, :])
        y = part if y is None else y + part
    x2 = x1 + m[5:6] * y
    if final_norm:
        x2 = _rms(x2, fin_ref[...])
    o_ref[...] = x2


def _outproj_ffn(x, o_f, o_b, p, mla, mods, row_fn, gg, wo, ln2, wg, wu, wd, fin_g, tm):
    t, d = x.shape
    ff = wg.shape[1]
    final_norm = fin_g is not None
    step = min(ff, DENSE_FF_GROUP)
    groups = tuple((lo, min(lo + step, ff)) for lo in range(0, ff, step))
    resident = lambda a: pl.BlockSpec(a.shape, lambda i: (0,) * a.ndim, pipeline_mode=pl.Buffered(1))
    in_specs = [
        pl.BlockSpec((tm, d), lambda i: (i, 0)),
        pl.BlockSpec((tm, GLA_V_W), lambda i: (i, 0)),
        pl.BlockSpec((tm, GLA_V_W), lambda i: (i, 0)),
        pl.BlockSpec((tm, GLA_V_W), lambda i: (i, P_R // GLA_V_W)),
        pl.BlockSpec((tm, MLA_V_W), lambda i: (i, 0)),
        pl.BlockSpec((1, 6, d), lambda i: (row_fn(i), 0, 0)),
        resident(gg), resident(wo), resident(ln2), resident(wg), resident(wu), resident(wd),
    ]
    args = [x, o_f, o_b, p, mla, mods, gg, wo, ln2, wg, wu, wd]
    if final_norm:
        in_specs.append(resident(fin_g))
        args.append(fin_g)
    return pl.pallas_call(
        functools.partial(_outproj_ffn_kernel, groups=groups, final_norm=final_norm),
        grid=(t // tm,),
        in_specs=in_specs,
        out_specs=pl.BlockSpec((tm, d), lambda i: (i, 0)),
        out_shape=jax.ShapeDtypeStruct((t, d), F32),
        compiler_params=_cparams(("arbitrary",)),
        name="outproj_ffn",
    )(*args)


def _outproj_kernel(*refs, with_router):
    (x_ref, of_ref, ob_ref, r_ref, mla_ref, mod_ref, gg_ref, wo_ref, ln2_ref) = refs[:9]
    if with_router:
        rwh_ref, rwl_ref, x1_ref, h2_ref, comb_ref = refs[9:]
    else:
        x1_ref, h2_ref = refs[9:]
    m = mod_ref[0]
    tm = x_ref.shape[0]
    n_groups = 2 if tm % 32 == 0 else 1
    for g in range(n_groups):
        rows = slice(g * tm // n_groups, (g + 1) * tm // n_groups)
        x1, h2 = _mix_residual_norm(x_ref, of_ref, ob_ref, r_ref, mla_ref, m, gg_ref, wo_ref,
                                    ln2_ref, rows)
        x1_ref[rows, :] = x1
        if not with_router:
            h2_ref[rows, :] = h2.astype(BF16)
            continue
        h2_ref[rows, :] = _pack_bf16_pairs(h2)
        h_hi = h2.astype(BF16)
        h_lo = (h2 - h_hi.astype(F32)).astype(BF16)
        logits = _dot(h_hi, rwh_ref[...]) + _dot(h_lo, rwh_ref[...]) + _dot(h_hi, rwl_ref[...])
        lane = lax.broadcasted_iota(jnp.int32, logits.shape, 1).astype(F32)
        neg = jnp.float32(-jnp.inf)
        logits = jnp.where(lane < N_EXPERTS, logits, neg)
        m1 = jnp.max(logits, axis=1, keepdims=True)
        i1 = jnp.min(jnp.where(logits == m1, lane, float(LANES)), axis=1, keepdims=True)
        rest = jnp.where(lane == i1, neg, logits)
        m2 = jnp.max(rest, axis=1, keepdims=True)
        i2 = jnp.min(jnp.where(rest == m2, lane, float(LANES)), axis=1, keepdims=True)
        e2 = jnp.exp(m2 - m1)
        w1 = 1.0 / (1.0 + e2)
        comb_ref[rows, :] = (jnp.where(lane == ROUTE_E1, i1, 0.0) + jnp.where(lane == ROUTE_E2, i2, 0.0)
                             + jnp.where(lane == ROUTE_W1, w1, 0.0)
                             + jnp.where(lane == ROUTE_W2, e2 * w1, 0.0))


def _outproj(x, o_f, o_b, p, mla, mods, row_fn, gg, wo, ln2, router, tm):
    t, d = x.shape
    with_router = router is not None
    full = lambda a: pl.BlockSpec(a.shape, lambda i: (0,) * a.ndim)
    in_specs = [
        pl.BlockSpec((tm, d), lambda i: (i, 0)),
        pl.BlockSpec((tm, GLA_V_W), lambda i: (i, 0)),
        pl.BlockSpec((tm, GLA_V_W), lambda i: (i, 0)),
        pl.BlockSpec((tm, GLA_V_W), lambda i: (i, P_R // GLA_V_W)),
        pl.BlockSpec((tm, MLA_V_W), lambda i: (i, 0)),
        pl.BlockSpec((1, 6, d), lambda i: (row_fn(i), 0, 0)),
        full(gg), full(wo), full(ln2),
    ]
    args = [x, o_f, o_b, p, mla, mods, gg, wo, ln2]
    h2_shape = jax.ShapeDtypeStruct((t, d // 2), F32) if with_router else jax.ShapeDtypeStruct((t, d), BF16)
    out_specs = [pl.BlockSpec((tm, d), lambda i: (i, 0)),
                 pl.BlockSpec((tm, h2_shape.shape[1]), lambda i: (i, 0))]
    out_shape = [jax.ShapeDtypeStruct((t, d), F32), h2_shape]
    if with_router:
        in_specs += [full(router[0]), full(router[1])]
        args += list(router)
        out_specs.append(pl.BlockSpec((tm, LANES), lambda i: (i, 0)))
        out_shape.append(jax.ShapeDtypeStruct((t, LANES), F32))
    return pl.pallas_call(
        functools.partial(_outproj_kernel, with_router=with_router),
        grid=(t // tm,),
        in_specs=in_specs,
        out_specs=out_specs,
        out_shape=out_shape,
        compiler_params=_cparams(("arbitrary",)),
        name="outproj",
    )(*args)


def _ffn_kernel(*refs, with_comb, final_norm):
    h_ref, x1_ref, mod_ref = refs[:3]
    k = 3
    comb_ref = fin_ref = None
    if with_comb:
        comb_ref = refs[k]
        k += 1
    wg_ref, wu_ref, wd_ref = refs[k:k + 3]
    k += 3
    if final_norm:
        fin_ref = refs[k]
        k += 1
    o_ref, acc = refs[k:]
    e = pl.program_id(1)
    f = pl.program_id(2)

    @pl.when((e == 0) & (f == 0))
    def _():
        acc[...] = jnp.zeros(acc.shape, F32)

    h = h_ref[...]
    a = _dot(h, wg_ref[0])
    u = _dot(h, wu_ref[0])
    act = _silu(a) * u
    if with_comb:
        comb = comb_ref[...]
        lane = lax.broadcasted_iota(jnp.int32, comb.shape, 1)
        act = act * jnp.sum(jnp.where(lane == e, comb, 0.0), axis=1, keepdims=True)
    acc[...] += _dot(act.astype(BF16), wd_ref[0])

    @pl.when((e == pl.num_programs(1) - 1) & (f == pl.num_programs(2) - 1))
    def _():
        x2 = x1_ref[...] + mod_ref[0][5:6] * acc[...]
        if final_norm:
            x2 = _rms(x2, fin_ref[...])
        o_ref[...] = x2


def _ffn(h2, x1, mods, row_fn, comb, wg, wu, wd, fin_g, tm, tf):
    t, d = x1.shape
    n_e, _, ff = wg.shape
    with_comb = comb is not None
    final_norm = fin_g is not None
    in_specs = [
        pl.BlockSpec((tm, d), lambda i, e, f: (i, 0)),
        pl.BlockSpec((tm, d), lambda i, e, f: (i, 0)),
        pl.BlockSpec((1, 6, d), lambda i, e, f: (row_fn(i), 0, 0)),
    ]
    args = [h2, x1, mods]
    if with_comb:
        in_specs.append(pl.BlockSpec((tm, LANES), lambda i, e, f: (i, 0)))
        args.append(comb)
    in_specs += [
        pl.BlockSpec((1, d, tf), lambda i, e, f: (e, 0, f)),
        pl.BlockSpec((1, d, tf), lambda i, e, f: (e, 0, f)),
        pl.BlockSpec((1, tf, d), lambda i, e, f: (e, f, 0)),
    ]
    args += [wg, wu, wd]
    if final_norm:
        in_specs.append(pl.BlockSpec((1, d), lambda i, e, f: (0, 0)))
        args.append(fin_g)
    return pl.pallas_call(
        functools.partial(_ffn_kernel, with_comb=with_comb, final_norm=final_norm),
        grid=(t // tm, n_e, ff // tf),
        in_specs=in_specs,
        out_specs=pl.BlockSpec((tm, d), lambda i, e, f: (i, 0)),
        out_shape=jax.ShapeDtypeStruct((t, d), F32),
        scratch_shapes=[pltpu.VMEM((tm, d), F32)],
        compiler_params=_cparams(("arbitrary", "arbitrary", "arbitrary")),
        name="ffn",
    )(*args)


def _sc_row_gather(table, idx):
    _, w = table.shape
    b = idx.shape[0]
    n_workers = SC_CORES * SC_SUBCORES
    assert b % (n_workers * SC_GATHER_ROWS) == 0, (b, n_workers, SC_GATHER_ROWS)
    b_per_w = b // n_workers
    n_chunks = b_per_w // SC_GATHER_ROWS
    mesh = plsc.VectorSubcoreMesh(core_axis_name="c", subcore_axis_name="s",
                                  num_cores=SC_CORES, num_subcores=SC_SUBCORES)

    def body(table_hbm, idx_hbm, out_hbm, idx_a, idx_b, rows_a, rows_b, sem_a, sem_b):
        wid = lax.axis_index("s") * SC_CORES + lax.axis_index("c")
        base = wid * b_per_w
        idx_bufs, row_bufs, sems = (idx_a, idx_b), (rows_a, rows_b), (sem_a, sem_b)

        def start(ci):
            slot = ci % 2
            pltpu.sync_copy(idx_hbm.at[pl.ds(base + ci * SC_GATHER_ROWS, SC_GATHER_ROWS)], idx_bufs[slot])
            return pltpu.async_copy(table_hbm.at[idx_bufs[slot]], row_bufs[slot], sems[slot])

        pending = start(0)
        for ci in range(n_chunks):
            following = start(ci + 1) if ci + 1 < n_chunks else None
            pending.wait()
            pltpu.sync_copy(row_bufs[ci % 2], out_hbm.at[pl.ds(base + ci * SC_GATHER_ROWS, SC_GATHER_ROWS)])
            pending = following

    return pl.kernel(
        body,
        out_type=jax.ShapeDtypeStruct((b, w), F32),
        mesh=mesh,
        scratch_types=[pltpu.VMEM((SC_GATHER_ROWS,), jnp.int32)] * 2
        + [pltpu.VMEM((SC_GATHER_ROWS, w), F32)] * 2 + [pltpu.SemaphoreType.DMA] * 2,
        name="sc_row_gather",
    )(table, idx)


def _sc_row_scatter2(table, pos, n_out):
    t, w = table.shape
    n_workers = SC_CORES * SC_SUBCORES
    assert t % (n_workers * SC_GATHER_ROWS) == 0, (t, n_workers, SC_GATHER_ROWS)
    t_per_w = t // n_workers
    n_chunks = t_per_w // SC_GATHER_ROWS
    mesh = plsc.VectorSubcoreMesh(core_axis_name="c", subcore_axis_name="s",
                                  num_cores=SC_CORES, num_subcores=SC_SUBCORES)

    def body(table_hbm, pos_hbm, out_hbm, i0a, i1a, i0b, i1b, rows_a, rows_b, sem_a, sem_b):
        wid = lax.axis_index("s") * SC_CORES + lax.axis_index("c")
        base = wid * t_per_w
        idx0, idx1, row_bufs, sems = (i0a, i0b), (i1a, i1b), (rows_a, rows_b), (sem_a, sem_b)

        def drain(pair):
            if pair is not None:
                pair[0].wait()
                pair[1].wait()

        pending = [None, None]
        for ci in range(n_chunks):
            slot = ci % 2
            drain(pending[slot])
            off = base + ci * SC_GATHER_ROWS
            pltpu.sync_copy(pos_hbm.at[0, pl.ds(off, SC_GATHER_ROWS)], idx0[slot])
            pltpu.sync_copy(pos_hbm.at[1, pl.ds(off, SC_GATHER_ROWS)], idx1[slot])
            pltpu.sync_copy(table_hbm.at[pl.ds(off, SC_GATHER_ROWS)], row_bufs[slot])
            pending[slot] = (pltpu.async_copy(row_bufs[slot], out_hbm.at[idx0[slot]], sems[slot]),
                             pltpu.async_copy(row_bufs[slot], out_hbm.at[idx1[slot]], sems[slot]))
        drain(pending[0])
        drain(pending[1])

    return pl.kernel(
        body,
        out_type=jax.ShapeDtypeStruct((n_out, w), F32),
        mesh=mesh,
        scratch_types=[pltpu.VMEM((SC_GATHER_ROWS,), jnp.int32)] * 4
        + [pltpu.VMEM((SC_GATHER_ROWS, w), F32)] * 2 + [pltpu.SemaphoreType.DMA] * 2,
        name="sc_row_scatter",
    )(table, pos)


def _moe_plan_kernel(route_ref, tri_ref, utri_ref, pos_ref, cnt_ref, run_scr, off_scr):
    phase = pl.program_id(0)
    blk = pl.program_id(1)
    route = route_ref[...]
    lane = lax.broadcasted_iota(jnp.int32, route.shape, 1).astype(F32)
    oh1 = jnp.where(lane == route[:, ROUTE_E1:ROUTE_E1 + 1], 1.0, 0.0)
    oh2 = jnp.where(lane == route[:, ROUTE_E2:ROUTE_E2 + 1], 1.0, 0.0)
    oh = oh1 + oh2

    @pl.when(blk == 0)
    def _():
        run_scr[...] = jnp.zeros(run_scr.shape, F32)

    @pl.when(phase == 0)
    def _():
        run_scr[...] += jnp.sum(oh, axis=0, keepdims=True)

        @pl.when(blk == pl.num_programs(1) - 1)
        def _():
            counts = run_scr[...]
            cnt_ref[...] = counts
            tiles_per = jnp.floor((counts + (MOE_TILE - 1.0)) * (1.0 / MOE_TILE))
            tile_end = _dot(jnp.broadcast_to(tiles_per, (8, LANES)).astype(BF16), utri_ref[...])[0:1]
            off_scr[...] = (tile_end - tiles_per) * float(MOE_TILE)

    @pl.when(phase == 1)
    def _():
        incl = _dot(tri_ref[...], oh.astype(BF16))
        before = incl - oh + run_scr[...] + off_scr[...]
        p1 = jnp.sum(before * oh1, axis=1, keepdims=True)
        p2 = jnp.sum(before * oh2, axis=1, keepdims=True)
        pos = jnp.where(lane == 0.0, p1, 0.0) + jnp.where(lane == 1.0, p2, 0.0)
        pos_ref[...] = pos.astype(jnp.int32)
        run_scr[...] += incl[incl.shape[0] - 1:, :]


def _moe_plan(route, n_tiles, tm):
    t = route.shape[0]
    tri = jnp.asarray(np.tril(np.ones((tm, tm), np.float32)), dtype=BF16)
    utri = jnp.asarray(np.triu(np.ones((LANES, LANES), np.float32)), dtype=BF16)
    pos, counts = pl.pallas_call(
        _moe_plan_kernel,
        grid=(2, t // tm),
        in_specs=[
            pl.BlockSpec((tm, LANES), lambda p, i: (i, 0)),
            pl.BlockSpec((tm, tm), lambda p, i: (0, 0)),
            pl.BlockSpec((LANES, LANES), lambda p, i: (0, 0)),
        ],
        out_specs=[
            pl.BlockSpec((tm, LANES), lambda p, i: (i * p, 0)),
            pl.BlockSpec((1, LANES), lambda p, i: (0, 0)),
        ],
        out_shape=[jax.ShapeDtypeStruct((t, LANES), jnp.int32),
                   jax.ShapeDtypeStruct((1, LANES), F32)],
        scratch_shapes=[pltpu.VMEM((1, LANES), F32), pltpu.VMEM((1, LANES), F32)],
        compiler_params=_cparams(("arbitrary", "arbitrary")),
        name="moe_plan",
    )(route, tri, utri)
    counts = counts[0, :N_EXPERTS].astype(jnp.int32)
    tile_end = jnp.cumsum((counts + MOE_TILE - 1) // MOE_TILE)
    n_used = tile_end[-1]
    tile_ids = jnp.minimum(jnp.arange(n_tiles, dtype=jnp.int32), n_used - 1)
    tile_expert = jnp.sum((tile_end[None, :] <= tile_ids[:, None]).astype(jnp.int32), axis=1)
    return pos[:, :2].T, tile_expert, n_used.reshape(1)


def _moe_ffn_kernel(te_ref, nused_ref, xs_ref, wg_ref, wu_ref, wd_ref, o_ref, acc, h_scr, *, splits):
    i = pl.program_id(0)
    f = pl.program_id(1)
    last_f = pl.num_programs(1) - 1
    used = i < nused_ref[0]

    @pl.when(used)
    def _():
        @pl.when(f == 0)
        def _():
            acc[...] = jnp.zeros(acc.shape, F32)
            h_scr[...] = _unpack_bf16_pairs(xs_ref[...]).astype(BF16)

        h = h_scr[...]
        for lo, hi in splits:
            a = _dot(h, wg_ref[0, :, lo:hi].astype(BF16))
            u = _dot(h, wu_ref[0, :, lo:hi].astype(BF16))
            acc[...] += _dot((_silu(a) * u).astype(BF16), wd_ref[0, lo:hi, :].astype(BF16))

        @pl.when(f == last_f)
        def _():
            o_ref[...] = _pack_bf16_pairs(acc[...])

    @pl.when(jnp.logical_not(used) & (f == last_f))
    def _():
        o_ref[...] = jnp.zeros(o_ref.shape, F32)


def _moe_ffn(xs, tile_expert, n_used, wg, wu, wd):
    rows, half = xs.shape
    d = 2 * half
    ff = wg.shape[2]
    if ff % MOE_FF_TILE == 0:
        tf, splits = MOE_FF_TILE, MOE_FF_SPLITS
    else:
        tf, splits = ff, ((0, ff),)
    n_tiles = rows // MOE_TILE
    grid_spec = pltpu.PrefetchScalarGridSpec(
        num_scalar_prefetch=2,
        grid=(n_tiles, ff // tf),
        in_specs=[
            pl.BlockSpec((MOE_TILE, half), lambda i, f, te, nu: (i, 0)),
            pl.BlockSpec((1, d, tf), lambda i, f, te, nu: (te[i], 0, f)),
            pl.BlockSpec((1, d, tf), lambda i, f, te, nu: (te[i], 0, f)),
            pl.BlockSpec((1, tf, d), lambda i, f, te, nu: (te[i], f, 0)),
        ],
        out_specs=pl.BlockSpec((MOE_TILE, half), lambda i, f, te, nu: (i, 0)),
        scratch_shapes=[pltpu.VMEM((MOE_TILE, d), F32), pltpu.VMEM((MOE_TILE, d), BF16)],
    )
    return pl.pallas_call(
        functools.partial(_moe_ffn_kernel, splits=splits),
        grid_spec=grid_spec,
        out_shape=jax.ShapeDtypeStruct((rows, half), F32),
        compiler_params=_cparams(("arbitrary", "arbitrary")),
        name="moe_ffn",
    )(tile_expert, n_used, xs, wg, wu, wd)


def _combine_kernel(*refs, final_norm):
    x1_ref, y0_ref, y1_ref, route_ref, mod_ref = refs[:5]
    fin_ref = refs[5] if final_norm else None
    o_ref = refs[-1]
    route = route_ref[...]
    w1 = route[:, ROUTE_W1:ROUTE_W1 + 1]
    w2 = route[:, ROUTE_W2:ROUTE_W2 + 1]
    y = w1 * _unpack_bf16_pairs(y0_ref[...]) + w2 * _unpack_bf16_pairs(y1_ref[...])
    x2 = x1_ref[...] + mod_ref[0][5:6] * y
    if final_norm:
        x2 = _rms(x2, fin_ref[...])
    o_ref[...] = x2


def _combine(x1, yg, route, mods, row_fn, fin_g, tm):
    t, d = x1.shape
    nb = t // tm
    final_norm = fin_g is not None
    in_specs = [
        pl.BlockSpec((tm, d), lambda i: (i, 0)),
        pl.BlockSpec((tm, d // 2), lambda i: (i, 0)),
        pl.BlockSpec((tm, d // 2), lambda i: (i + nb, 0)),
        pl.BlockSpec((tm, LANES), lambda i: (i, 0)),
        pl.BlockSpec((1, 6, d), lambda i: (row_fn(i), 0, 0)),
    ]
    args = [x1, yg, yg, route, mods]
    if final_norm:
        in_specs.append(pl.BlockSpec((1, d), lambda i: (0, 0)))
        args.append(fin_g)
    return pl.pallas_call(
        functools.partial(_combine_kernel, final_norm=final_norm),
        grid=(nb,),
        in_specs=in_specs,
        out_specs=pl.BlockSpec((tm, d), lambda i: (i, 0)),
        out_shape=jax.ShapeDtypeStruct((t, d), F32),
        compiler_params=_cparams(("arbitrary",)),
        name="moe_combine",
    )(*args)


def _moe(h2, x1, route, mods, row_fn, wg, wu, wd, fin_g, tm):
    t = h2.shape[0]
    n_tiles = -(-2 * t // MOE_TILE) + N_EXPERTS
    pos, tile_expert, n_used = _moe_plan(route, n_tiles, _pick_tile(t, ATTN_TILE))
    xs = _sc_row_scatter2(h2, pos, n_tiles * MOE_TILE)
    ys = _moe_ffn(xs, tile_expert, n_used, wg, wu, wd)
    yg = _sc_row_gather(ys, pos.reshape(-1))
    return _combine(x1, yg, route, mods, row_fn, fin_g, tm)


def _rope_partner():
    j = np.arange(MLA_ROPE)
    return np.where((j % 32) < 16, j + 16, j - 16)


def _prep_in_weight(w):
    d = w.shape[0]
    cols = [w[:, 0:1024], w[:, 1056:1568], w[:, 1568:1824], w[:, 1824:1952], w[:, 1952:2016],
            w[:, 1024:1056], jnp.zeros((d, P_WIDTH - 2016), w.dtype)]
    return jnp.concatenate(cols, axis=1).astype(BF16)


def _prep_gate_weight(w_g2, b_g2):
    ws = []
    for z, off in ((0, MISC_GF), (1, MISC_GB)):
        ws.append(jnp.zeros((LANES, GLA_QK_W), F32).at[off:off + GLA_GATE_RANK].set(w_g2[z]))
    return jnp.stack(ws).astype(BF16), b_g2.reshape(2, 1, GLA_QK_W)


def _prep_mla_weights(w_uq, w_ukv):
    partner = _rope_partner()
    wq = w_uq.reshape(MLA_Q_RANK, MLA_HEADS, MLA_QK)
    wqn = wq[:, :, :MLA_NOPE].reshape(MLA_Q_RANK, MLA_HEADS * MLA_NOPE)
    rope = wq[:, :, MLA_NOPE:]
    pad = jnp.zeros((MLA_Q_RANK, MLA_HEADS, LANES - MLA_ROPE), w_uq.dtype)
    wqr = jnp.concatenate([rope, pad], axis=2).reshape(MLA_Q_RANK, MLA_HEADS * LANES)
    wqs = jnp.concatenate([rope[:, :, partner], pad], axis=2).reshape(MLA_Q_RANK, MLA_HEADS * LANES)
    wkv = w_ukv.reshape(MLA_KV_RANK, MLA_HEADS, MLA_NOPE + MLA_V)
    wknt = wkv[:, :, :MLA_NOPE].reshape(MLA_KV_RANK, MLA_HEADS * MLA_NOPE).T
    wv = wkv[:, :, MLA_NOPE:].reshape(MLA_KV_RANK, MLA_HEADS * MLA_V)
    perm = np.zeros((LANES, LANES), np.float32)
    perm[partner, np.arange(MLA_ROPE)] = 1.0
    eye = np.eye(MLA_ROPE, LANES, dtype=np.float32)
    return (wqn.astype(BF16), wqr.astype(BF16), wqs.astype(BF16), wknt.astype(BF16),
            wv.astype(BF16), jnp.asarray(perm, BF16), jnp.asarray(eye, BF16))


def _rope_tables(n_tok):
    rows = n_tok // GRID_W
    row = np.repeat(np.arange(rows, dtype=np.float32), GRID_W)
    col = np.tile(np.arange(GRID_W, dtype=np.float32), rows)
    nfreq = MLA_ROPE // 4
    inv = np.float32(ROPE_BASE) ** (-np.arange(nfreq, dtype=np.float32) / np.float32(nfreq))
    ar = (row[:, None] * inv).astype(np.float32)
    ac = (col[:, None] * inv).astype(np.float32)
    zero = np.zeros((n_tok, LANES - MLA_ROPE), np.float32)
    cos = np.concatenate([np.cos(ar), np.cos(ar), np.cos(ac), np.cos(ac), zero], axis=1)
    sin = np.concatenate([-np.sin(ar), np.sin(ar), -np.sin(ac), np.sin(ac), zero], axis=1)
    return jnp.asarray(cos, F32), jnp.asarray(sin, F32)


def _identity_tables(n_tok):
    cos = jnp.concatenate([jnp.ones((n_tok, MLA_ROPE), F32),
                           jnp.zeros((n_tok, LANES - MLA_ROPE), F32)], axis=1)
    return cos, jnp.zeros((n_tok, LANES), F32)


def _pick_tile(n, pref):
    t = min(n, pref)
    while n % t:
        t //= 2
    return t


def _pick_ff_tile(ff):
    best = LANES
    for m in range(1, ff // LANES + 1):
        if ff % (m * LANES) == 0 and m * LANES <= FFN_MAX_FF_TILE:
            best = m * LANES
    return best


@jax.jit
def _forward(x, c, ctx, c_ctx, w_mod, b_mod, ln1_g, ln2_g, w_in, w_gla_g2, b_gla_g2, gla_norm_g,
             mla_q_norm_g, w_uq, mla_kv_norm_g, w_ukv, w_out, ffn_w_gate, ffn_w_up, ffn_w_down,
             router_w, exp_w_gate, exp_w_up, exp_w_down, final_norm_g):
    batch, seq, d = x.shape
    n_ctx = ctx.shape[1]
    depth = w_mod.shape[0]

    cvec = jnp.zeros((8, d), F32).at[:batch].set(c).at[batch].set(c_ctx)
    mods_all = _modulation(cvec, w_mod, b_mod).reshape(depth, 8, 6, d)

    xl = x.reshape(batch * seq, d)
    xc = ctx.reshape(batch * n_ctx, d)

    tm_l = _pick_tile(seq, TOKEN_TILE)
    tm_c = _pick_tile(n_ctx, CTX_TILE)
    tk_l = _pick_tile(seq, ATTN_TILE)
    cb_l = _pick_tile(seq, GLA_BLOCK)
    cb_c = _pick_tile(n_ctx, GLA_BLOCK)
    row_l = lambda tm: (lambda i: i // (seq // tm))
    row_c = lambda i: batch

    rope_l = _rope_tables(seq)
    rope_c = _identity_tables(tm_c)
    zero_state = jnp.zeros((2, batch, GLA_QK_W, GLA_DV), F32)

    for i in range(depth):
        need_ctx = i < depth - 1
        last = i == depth - 1
        mods = mods_all[i]
        ln1 = ln1_g[i].reshape(1, d)
        ln2 = ln2_g[i].reshape(1, d)
        w_in_r = _prep_in_weight(w_in[i])
        gates = _prep_gate_weight(w_gla_g2[i], b_gla_g2[i])
        mla_w = _prep_mla_weights(w_uq[i], w_ukv[i])
        qg = mla_q_norm_g[i].reshape(1, MLA_Q_RANK)
        kvg = mla_kv_norm_g[i].reshape(1, MLA_KV_RANK)
        gg = gla_norm_g[i].reshape(1, GLA_DV)
        wo = w_out[i].astype(BF16)

        p_l = _inproj(xl, mods, row_l(tk_l), ln1, w_in_r, tk_l)
        p_c = _inproj(xc, mods, row_c, ln1, w_in_r, tm_c)

        oc_f, oc_b, s_ctx = _gla(p_c, *gates, zero_state, batch=batch, cb=cb_c)
        ol_f, ol_b, _ = _gla(p_l, *gates, s_ctx, batch=batch, cb=cb_l)

        q_l, kt_l, v_l = _mlaprep(p_l, *rope_l, qg, kvg, mla_w, batch=batch, tm=tk_l)
        q_c, kt_c, v_c = _mlaprep(p_c, *rope_c, qg, kvg, mla_w, batch=batch, tm=tm_c)
        m_l = _attention_pipelined(q_l, kt_l, v_l, kt_c, v_c, tq=tk_l, n_sub=1)
        m_l = m_l.reshape(batch * seq, MLA_V_W)

        if i % 2 == 0:
            j = i // 2
            router = None
            wg = ffn_w_gate[j].astype(BF16)
            wu = ffn_w_up[j].astype(BF16)
            wd = ffn_w_down[j].astype(BF16)
        else:
            j = i // 2
            rw = jnp.zeros((d, LANES), F32).at[:, :N_EXPERTS].set(router_w[j])
            rw_hi = rw.astype(BF16)
            router = (rw_hi, (rw - rw_hi.astype(F32)).astype(BF16))
            wg, wu, wd = exp_w_gate[j], exp_w_up[j], exp_w_down[j]
        fin = final_norm_g.reshape(1, d) if last else None

        if router is None:
            xl = _outproj_ffn(xl, ol_f, ol_b, p_l, m_l, mods, row_l(tm_l), gg, wo, ln2,
                              wg, wu, wd, fin, tm_l)
        else:
            outs = _outproj(xl, ol_f, ol_b, p_l, m_l, mods, row_l(tm_l), gg, wo, ln2, router, tm_l)
            xl = _moe(outs[1], outs[0], outs[2], mods, row_l(tm_l), wg, wu, wd, fin, tm_l)

        if need_ctx:
            m_c = _attention(q_c, [(kt_c, v_c)], tq=tm_c, n_sub=1).reshape(batch * n_ctx, MLA_V_W)
            if router is None:
                xc = _outproj_ffn(xc, oc_f, oc_b, p_c, m_c, mods, row_c, gg, wo, ln2,
                                  wg, wu, wd, None, _pick_tile(batch * n_ctx, TOKEN_TILE))
            else:
                outs_c = _outproj(xc, oc_f, oc_b, p_c, m_c, mods, row_c, gg, wo, ln2, router, tm_c)
                tm_fc = _pick_tile(batch * n_ctx, TOKEN_TILE)
                r_c = outs_c[2]
                lane = jnp.arange(LANES, dtype=F32)[None, :]
                comb_c = (jnp.where(lane == r_c[:, ROUTE_E1:ROUTE_E1 + 1], r_c[:, ROUTE_W1:ROUTE_W1 + 1], 0.0)
                          + jnp.where(lane == r_c[:, ROUTE_E2:ROUTE_E2 + 1], r_c[:, ROUTE_W2:ROUTE_W2 + 1], 0.0))
                bits = lax.bitcast_convert_type(outs_c[1], jnp.uint32)
                h2_c = jnp.concatenate([lax.bitcast_convert_type(bits << 16, F32),
                                        lax.bitcast_convert_type(bits & jnp.uint32(0xFFFF0000), F32)], axis=1)
                xc = _ffn(h2_c.astype(BF16), outs_c[0], mods, row_c, comb_c,
                          wg.astype(BF16), wu.astype(BF16), wd.astype(BF16), None, tm_fc,
                          _pick_ff_tile(wg.shape[2]))

    return xl.reshape(batch, seq, d)


def kernel(x, c, ctx, c_ctx, w_mod, b_mod, ln1_g, ln2_g, w_in, w_gla_g2, b_gla_g2, gla_norm_g,
           mla_q_norm_g, w_uq, mla_kv_norm_g, w_ukv, w_out, ffn_w_gate, ffn_w_up, ffn_w_down,
           router_w, exp_w_gate, exp_w_up, exp_w_down, final_norm_g):
    return _forward(x, c, ctx, c_ctx, w_mod, b_mod, ln1_g, ln2_g, w_in, w_gla_g2, b_gla_g2,
                    gla_norm_g, mla_q_norm_g, w_uq, mla_kv_norm_g, w_ukv, w_out, ffn_w_gate,
                    ffn_w_up, ffn_w_down, router_w, exp_w_gate, exp_w_up, exp_w_down, final_norm_g)
```

```python
import functools

import numpy as np
import jax
import jax.numpy as jnp
from jax import lax
from jax.experimental import pallas as pl
from jax.experimental.pallas import tpu as pltpu
from jax.experimental.pallas import tpu_sc as plsc

F32 = jnp.float32
BF16 = jnp.bfloat16

D_MODEL = 1024
EPS = 1e-6
GRID_W = 64

GLA_HEADS = 4
GLA_DK = 64
GLA_DV = 128
GLA_GATE_RANK = 16
GLA_GATE_NORM = 16.0
GLA_CHUNK = 64
GLA_QK_W = GLA_HEADS * GLA_DK
GLA_V_W = GLA_HEADS * GLA_DV
GLA_EXP_CLAMP = 80.0

MLA_HEADS = 4
MLA_NOPE = 128
MLA_ROPE = 64
MLA_V = 128
MLA_QK = MLA_NOPE + MLA_ROPE
MLA_Q_RANK = 256
MLA_KV_RANK = 128
MLA_SCALE = MLA_QK ** -0.5
MLA_Q_SCALE = MLA_SCALE * 1.4426950408889634
MLA_V_W = MLA_HEADS * MLA_V
MLA_V_EXT = 2 * MLA_V
ROPE_BASE = 10000.0

N_EXPERTS = 8
LANES = 128
ROUTE_E1, ROUTE_E2, ROUTE_W1, ROUTE_W2 = 0, 1, 2, 3

SC_CORES = 2
SC_SUBCORES = 16
SC_GATHER_ROWS = 64
MOE_TILE = 1024
MOE_FF_TILE = 512
MOE_FF_SPLITS = ((0, 256), (256, 512))

P_Q, P_K, P_V, P_R, P_CQ, P_CKV, P_MISC = 0, 256, 512, 1024, 1536, 1792, 1920
P_WIDTH = 2048
MISC_KR, MISC_GF, MISC_GB = 0, 64, 80

VMEM_LIMIT = 56 * 1024 * 1024

TOKEN_TILE = 512
CTX_TILE = 256
ATTN_TILE = 1024
GLA_BLOCK = 256
DENSE_FF_GROUP = 1024
FFN_MAX_FF_TILE = 1408


def _cparams(sem):
    return pltpu.CompilerParams(dimension_semantics=sem, vmem_limit_bytes=VMEM_LIMIT)


def _rms(x, g):
    return x * lax.rsqrt(jnp.mean(x * x, axis=-1, keepdims=True) + EPS) * g


def _silu(x):
    return x / (1.0 + jnp.exp(-x))


def _dot(a, b):
    return jnp.dot(a, b, preferred_element_type=F32)


def _dot_nt(a, b):
    return lax.dot_general(a, b, (((1,), (1,)), ((), ())), preferred_element_type=F32)


def _dot_tn(a, b):
    return lax.dot_general(a, b, (((0,), (0,)), ((), ())), preferred_element_type=F32)


def _pack_bf16_pairs(x):
    w = x.shape[1] // 2
    words = pltpu.pack_elementwise([x[:, :w], x[:, w:]], packed_dtype=BF16)
    return lax.bitcast_convert_type(words, F32)


def _unpack_bf16_pairs(p):
    words = lax.bitcast_convert_type(p, jnp.int32)
    lo = pltpu.unpack_elementwise(words, index=0, packed_dtype=BF16, unpacked_dtype=F32)
    hi = pltpu.unpack_elementwise(words, index=1, packed_dtype=BF16, unpacked_dtype=F32)
    return jnp.concatenate([lo, hi], axis=1)


def _mod_kernel(c_ref, w_ref, b_ref, o_ref):
    s = _silu(c_ref[...]).astype(BF16)
    o_ref[0] = _dot(s, w_ref[0].astype(BF16)) + b_ref[0]


def _modulation(cvec, w_mod, b_mod):
    depth, d, n = w_mod.shape
    tn = 1536
    return pl.pallas_call(
        _mod_kernel,
        grid=(depth, n // tn),
        in_specs=[
            pl.BlockSpec((8, d), lambda l, j: (0, 0)),
            pl.BlockSpec((1, d, tn), lambda l, j: (l, 0, j)),
            pl.BlockSpec((1, 1, tn), lambda l, j: (l, 0, j)),
        ],
        out_specs=pl.BlockSpec((1, 8, tn), lambda l, j: (l, 0, j)),
        out_shape=jax.ShapeDtypeStruct((depth, 8, n), F32),
        compiler_params=_cparams(("arbitrary", "arbitrary")),
        name="modulation",
    )(cvec, w_mod, b_mod.reshape(depth, 1, n))


def _inproj_kernel(x_ref, mod_ref, g_ref, w_ref, o_ref):
    m = mod_ref[0]
    tm = x_ref.shape[0]
    n_groups = 2 if tm % 32 == 0 else 1
    for g in range(n_groups):
        rows = slice(g * tm // n_groups, (g + 1) * tm // n_groups)
        h = _rms(x_ref[rows, :], g_ref[...]) * (1.0 + m[1:2]) + m[0:1]
        o_ref[rows, :] = _dot(h.astype(BF16), w_ref[...]).astype(BF16)


def _inproj(x, mods, row_fn, ln_g, w, tm):
    t, d = x.shape
    return pl.pallas_call(
        _inproj_kernel,
        grid=(t // tm,),
        in_specs=[
            pl.BlockSpec((tm, d), lambda i: (i, 0)),
            pl.BlockSpec((1, 6, d), lambda i: (row_fn(i), 0, 0)),
            pl.BlockSpec((1, d), lambda i: (0, 0)),
            pl.BlockSpec((d, P_WIDTH), lambda i: (0, 0)),
        ],
        out_specs=pl.BlockSpec((tm, P_WIDTH), lambda i: (i, 0)),
        out_shape=jax.ShapeDtypeStruct((t, P_WIDTH), BF16),
        compiler_params=_cparams(("arbitrary",)),
        name="inproj",
    )(x, mods, ln_g, w)


def _gla_direction(q_ref, k_ref, v_ref, misc_ref, wg, bg, tri, s_scr, o_ref, *, reverse, n_chunks):
    c_len = GLA_CHUNK
    pre = _dot(misc_ref[...], wg) + bg
    g = (jnp.minimum(pre, 0.0) - jnp.log(1.0 + jnp.exp(-jnp.abs(pre)))) * (1.0 / GLA_GATE_NORM)
    g_hi = g.astype(BF16)
    g_lo = (g - g_hi.astype(F32)).astype(BF16)
    cum = _dot(tri, g_hi) + _dot(tri, g_lo)
    tot_rows = jnp.concatenate(
        [cum[c * c_len:c * c_len + 1] if reverse else cum[(c + 1) * c_len - 1:(c + 1) * c_len]
         for c in range(n_chunks)] + [jnp.zeros((8 - n_chunks, GLA_QK_W), F32)], axis=0)
    t_hi = tot_rows.astype(BF16)
    t_lo = (tot_rows - t_hi.astype(F32)).astype(BF16)
    eye = (lax.broadcasted_iota(jnp.int32, (GLA_QK_W, GLA_QK_W), 0)
           == lax.broadcasted_iota(jnp.int32, (GLA_QK_W, GLA_QK_W), 1))
    eye = jnp.where(eye, 1.0, 0.0).astype(BF16)
    tot_cols = _dot_nt(eye, t_hi) + _dot_nt(eye, t_lo)

    lane = lax.broadcasted_iota(jnp.int32, (c_len, GLA_QK_W), 1)
    head_masks = [(lane >= h * GLA_DK) & (lane < (h + 1) * GLA_DK) for h in range(GLA_HEADS)]
    row = lax.broadcasted_iota(jnp.int32, (GLA_HEADS * c_len, c_len), 0) % c_len
    col = lax.broadcasted_iota(jnp.int32, (GLA_HEADS * c_len, c_len), 1)
    pair_mask = (col >= row) if reverse else (col <= row)

    def stack_heads(a):
        return jnp.concatenate([jnp.where(mk, a, 0.0) for mk in head_masks], axis=0).astype(BF16)

    def step(c):
        sl = slice(c * c_len, (c + 1) * c_len)
        xc = cum[sl]
        tot = tot_rows[c:c + 1]
        ref = xc[c_len // 2:c_len // 2 + 1]
        qc = q_ref[sl, :].astype(F32) * (GLA_DK ** -0.5)
        kc = k_ref[sl, :].astype(F32)
        vc = v_ref[sl, :]
        q_mid = qc * jnp.exp(jnp.minimum(xc - ref, GLA_EXP_CLAMP))
        k_mid = (kc * jnp.exp(jnp.minimum(ref - xc, GLA_EXP_CLAMP))).astype(BF16)
        q_dec = qc * jnp.exp(xc)
        k_dec = kc * jnp.exp(tot - xc)

        attn = _dot_nt(stack_heads(q_mid), k_mid)
        attn = jnp.where(pair_mask, attn, 0.0).astype(BF16)
        s_prev = s_scr[...]
        o_inter = _dot(stack_heads(q_dec), s_prev.astype(BF16))
        kv = []
        for h in range(GLA_HEADS):
            rs = slice(h * c_len, (h + 1) * c_len)
            vs = slice(h * GLA_DV, (h + 1) * GLA_DV)
            o_h = o_inter[rs] + _dot(attn[rs], vc[:, vs])
            o_ref[sl, vs] = o_h.astype(BF16)
            kv.append(_dot_tn(k_dec[:, h * GLA_DK:(h + 1) * GLA_DK].astype(BF16), vc[:, vs]))
        s_scr[...] = s_prev * jnp.exp(tot_cols[:, c:c + 1]) + jnp.concatenate(kv, axis=0)

    return step


def _gla_kernel(qf_ref, kf_ref, vf_ref, mf_ref, qb_ref, kb_ref, vb_ref, mb_ref,
                wg_ref, bg_ref, tri_ref, s0_ref, of_ref, ob_ref, sfin_ref, sf_scr, sb_scr, *, n_chunks):
    blk = pl.program_id(1)

    @pl.when(blk == 0)
    def _():
        sf_scr[...] = s0_ref[0, 0]
        sb_scr[...] = s0_ref[1, 0]

    fwd = _gla_direction(qf_ref, kf_ref, vf_ref, mf_ref, wg_ref[0], bg_ref[0], tri_ref[0],
                         sf_scr, of_ref, reverse=False, n_chunks=n_chunks)
    bwd = _gla_direction(qb_ref, kb_ref, vb_ref, mb_ref, wg_ref[1], bg_ref[1], tri_ref[1],
                         sb_scr, ob_ref, reverse=True, n_chunks=n_chunks)
    for c in range(n_chunks):
        fwd(c)
        bwd(n_chunks - 1 - c)

    @pl.when(blk == pl.num_programs(1) - 1)
    def _():
        sfin_ref[0, 0] = sf_scr[...]
        sfin_ref[1, 0] = sb_scr[...]


def _block_diag_tri(n_chunks):
    c = GLA_CHUNK
    eye = np.eye(n_chunks, dtype=np.float32)
    lower = np.kron(eye, np.tril(np.ones((c, c), np.float32)))
    upper = np.kron(eye, np.triu(np.ones((c, c), np.float32)))
    return jnp.asarray(np.stack([lower, upper]), dtype=BF16)


def _gla(p, wg, bg, s0, *, batch, cb):
    t_all = p.shape[0]
    nblk = t_all // batch // cb
    n_chunks = cb // GLA_CHUNK
    assert n_chunks <= 8

    fw = lambda b, i: b * nblk + i
    bw = lambda b, i: b * nblk + (nblk - 1 - i)
    full = lambda a: pl.BlockSpec(a.shape, lambda b, i: (0,) * a.ndim)
    tri = _block_diag_tri(n_chunks)

    def token_specs(tok):
        return [
            pl.BlockSpec((cb, GLA_QK_W), lambda b, i: (tok(b, i), P_Q // GLA_QK_W)),
            pl.BlockSpec((cb, GLA_QK_W), lambda b, i: (tok(b, i), P_K // GLA_QK_W)),
            pl.BlockSpec((cb, GLA_V_W), lambda b, i: (tok(b, i), P_V // GLA_V_W)),
            pl.BlockSpec((cb, LANES), lambda b, i: (tok(b, i), P_MISC // LANES)),
        ]

    state_spec = pl.BlockSpec((2, 1, GLA_QK_W, GLA_DV), lambda b, i: (0, b, 0, 0))
    return pl.pallas_call(
        functools.partial(_gla_kernel, n_chunks=n_chunks),
        grid=(batch, nblk),
        in_specs=token_specs(fw) + token_specs(bw) + [full(wg), full(bg), full(tri), state_spec],
        out_specs=[
            pl.BlockSpec((cb, GLA_V_W), lambda b, i: (fw(b, i), 0)),
            pl.BlockSpec((cb, GLA_V_W), lambda b, i: (bw(b, i), 0)),
            state_spec,
        ],
        out_shape=[
            jax.ShapeDtypeStruct((t_all, GLA_V_W), BF16),
            jax.ShapeDtypeStruct((t_all, GLA_V_W), BF16),
            jax.ShapeDtypeStruct((2, batch, GLA_QK_W, GLA_DV), F32),
        ],
        scratch_shapes=[pltpu.VMEM((GLA_QK_W, GLA_DV), F32), pltpu.VMEM((GLA_QK_W, GLA_DV), F32)],
        compiler_params=_cparams(("arbitrary", "arbitrary")),
        name="gla",
    )(p, p, p, p, p, p, p, p, wg, bg, tri, s0)


def _mlaprep_kernel(cq_ref, ckv_ref, misc_ref, cos_ref, sin_ref, qg_ref, kvg_ref,
                    wqn_ref, wqr_ref, wqs_ref, wknt_ref, wv_ref, perm_ref, eye_ref,
                    q_ref, kt_ref, v_ref):
    tm = cq_ref.shape[0]
    n_groups = 2 if tm % 256 == 0 else 1
    for g in range(n_groups):
        rows = slice(g * tm // n_groups, (g + 1) * tm // n_groups)
        cos = cos_ref[rows, :]
        sin = sin_ref[rows, :]
        cqn = _rms(cq_ref[rows, :].astype(F32), qg_ref[...]).astype(BF16)
        qn = _dot(cqn, wqn_ref[...])
        qr = _dot(cqn, wqr_ref[...])
        qs = _dot(cqn, wqs_ref[...])
        for h in range(MLA_HEADS):
            ls = slice(h * LANES, (h + 1) * LANES)
            q_ref[0, h, rows, 0:MLA_NOPE] = (qn[:, ls] * MLA_Q_SCALE).astype(BF16)
            rot = qr[:, ls] * cos + qs[:, ls] * sin
            q_ref[0, h, rows, MLA_NOPE:MLA_QK] = (rot[:, 0:MLA_ROPE] * MLA_Q_SCALE).astype(BF16)

        ckvn = _rms(ckv_ref[rows, :].astype(F32), kvg_ref[...]).astype(BF16)
        knt = _dot_nt(wknt_ref[...], ckvn)
        vv = _dot(ckvn, wv_ref[...])
        misc = misc_ref[rows, :]
        kr = misc.astype(F32) * cos + _dot(misc, perm_ref[...]) * sin
        krt = _dot_nt(eye_ref[...], kr.astype(BF16)).astype(BF16)
        for h in range(MLA_HEADS):
            kt_ref[0, h, 0, 0:MLA_NOPE, rows] = knt[h * MLA_NOPE:(h + 1) * MLA_NOPE].astype(BF16)
            kt_ref[0, h, 0, MLA_NOPE:MLA_QK, rows] = krt
            v_ref[0, h, rows, 0:MLA_V] = vv[:, h * MLA_V:(h + 1) * MLA_V].astype(BF16)
            v_ref[0, h, rows, MLA_V:MLA_V_EXT] = jnp.ones((vv.shape[0], MLA_V), BF16)


def _mlaprep(p, cos, sin, qg, kvg, wts, *, batch, tm):
    t_all = p.shape[0]
    t = t_all // batch
    nb = t // tm
    ntab = cos.shape[0] // tm
    wqn, wqr, wqs, wknt, wv, perm, eye = wts
    full = lambda a: pl.BlockSpec(a.shape, lambda b, i: (0,) * a.ndim)
    return pl.pallas_call(
        _mlaprep_kernel,
        grid=(batch, nb),
        in_specs=[
            pl.BlockSpec((tm, MLA_Q_RANK), lambda b, i: (b * nb + i, P_CQ // MLA_Q_RANK)),
            pl.BlockSpec((tm, MLA_KV_RANK), lambda b, i: (b * nb + i, P_CKV // MLA_KV_RANK)),
            pl.BlockSpec((tm, LANES), lambda b, i: (b * nb + i, P_MISC // LANES)),
            pl.BlockSpec((tm, LANES), lambda b, i: (i % ntab, 0)),
            pl.BlockSpec((tm, LANES), lambda b, i: (i % ntab, 0)),
            full(qg), full(kvg), full(wqn), full(wqr), full(wqs), full(wknt), full(wv),
            full(perm), full(eye),
        ],
        out_specs=[
            pl.BlockSpec((1, MLA_HEADS, tm, MLA_QK), lambda b, i: (b, 0, i, 0)),
            pl.BlockSpec((1, MLA_HEADS, 1, MLA_QK, tm), lambda b, i: (b, 0, i, 0, 0)),
            pl.BlockSpec((1, MLA_HEADS, tm, MLA_V_EXT), lambda b, i: (b, 0, i, 0)),
        ],
        out_shape=[
            jax.ShapeDtypeStruct((batch, MLA_HEADS, t, MLA_QK), BF16),
            jax.ShapeDtypeStruct((batch, MLA_HEADS, nb, MLA_QK, tm), BF16),
            jax.ShapeDtypeStruct((batch, MLA_HEADS, t, MLA_V_EXT), BF16),
        ],
        compiler_params=_cparams(("arbitrary", "arbitrary")),
        name="mlaprep",
    )(p, p, p, cos, sin, qg, kvg, wqn, wqr, wqs, wknt, wv, perm, eye)


def _attn_kernel(*refs, n_seg, n_sub):
    q_ref = refs[0]
    kt_refs = refs[1:1 + 2 * n_seg:2]
    v_refs = refs[2:2 + 2 * n_seg:2]
    o_ref = refs[1 + 2 * n_seg]
    m_scr, acc_scr = refs[2 + 2 * n_seg:]

    rows_per_sub = q_ref.shape[2] // n_sub
    m_scr[...] = jnp.full(m_scr.shape, -jnp.inf, F32)
    acc_scr[...] = jnp.zeros(acc_scr.shape, F32)

    for kt_ref, v_ref in zip(kt_refs, v_refs):
        n_blocks, tk = kt_ref.shape[2], kt_ref.shape[4]

        def step(j, carry, kt_ref=kt_ref, v_ref=v_ref, tk=tk):
            kt = kt_ref[0, 0, j]
            v_blk = v_ref[0, 0, pl.ds(pl.multiple_of(j * tk, tk), tk), :]
            for u in range(n_sub):
                rows = slice(u * rows_per_sub, (u + 1) * rows_per_sub)
                s = _dot(q_ref[0, 0, rows, :], kt)
                m_prev = m_scr[rows, :]
                m_next = jnp.maximum(m_prev, jnp.max(s, axis=1, keepdims=True))
                p = jnp.exp2((s - jnp.concatenate([m_next] * (tk // LANES), axis=1)).astype(BF16))
                alpha = jnp.exp2(m_prev - m_next)
                acc_scr[rows, :] = (jnp.concatenate([alpha] * (MLA_V_EXT // LANES), axis=1)
                                    * acc_scr[rows, :] + _dot(p, v_blk))
                m_scr[rows, :] = m_next
            return carry

        lax.fori_loop(0, n_blocks, step, 0)

    o_ref[0] = (acc_scr[:, 0:MLA_V] / acc_scr[:, MLA_V:MLA_V_EXT]).astype(BF16)


def _attn_pipe_kernel(q_ref, kt_ref, v_ref, ktt_ref, vt_ref, o_ref,
                      m_scr, acc_scr, s0_scr, s1_scr, st_scr, *, n_sub):
    n_blocks, tk = kt_ref.shape[2], kt_ref.shape[4]
    tq = m_scr.shape[0]
    n_q = q_ref.shape[2] // tq
    rows_per_sub = tq // n_sub
    subs = [slice(u * rows_per_sub, (u + 1) * rows_per_sub) for u in range(n_sub)]
    bufs = (s0_scr, s1_scr)

    def reset():
        m_scr[...] = jnp.full(m_scr.shape, -jnp.inf, F32)
        acc_scr[...] = jnp.zeros(acc_scr.shape, F32)

    def scores(qi, kt, s_ref):
        for rows in subs:
            q_rows = pl.ds(pl.multiple_of(qi * tq + rows.start, rows_per_sub), rows_per_sub)
            s_ref[rows, :] = _dot(q_ref[0, 0, q_rows, :], kt)

    def softmax_pv(s_ref, v_blk):
        width = s_ref.shape[1]
        for rows in subs:
            s = s_ref[rows, :]
            m_prev = m_scr[rows, :]
            m_next = jnp.maximum(m_prev, jnp.max(s, axis=1, keepdims=True))
            p = jnp.exp2((s - jnp.concatenate([m_next] * (width // LANES), axis=1)).astype(BF16))
            alpha = jnp.exp2(m_prev - m_next)
            acc_scr[rows, :] = (jnp.concatenate([alpha] * (MLA_V_EXT // LANES), axis=1)
                                * acc_scr[rows, :] + _dot(p, v_blk))
            m_scr[rows, :] = m_next

    reset()
    scores(0, kt_ref[0, 0, 0], bufs[0])

    def query_block(qi, carry):
        for j in range(n_blocks):
            if j + 1 < n_blocks:
                scores(qi, kt_ref[0, 0, j + 1], bufs[(j + 1) % 2])
            else:
                scores(qi, ktt_ref[0, 0, 0], st_scr)
            softmax_pv(bufs[j % 2], v_ref[0, 0, j * tk:(j + 1) * tk, :])
        scores(jnp.minimum(qi + 1, n_q - 1), kt_ref[0, 0, 0], bufs[0])
        softmax_pv(st_scr, vt_ref[0, 0])
        o_rows = pl.ds(pl.multiple_of(qi * tq, tq), tq)
        o_ref[0, o_rows, :] = (acc_scr[:, 0:MLA_V] / acc_scr[:, MLA_V:MLA_V_EXT]).astype(BF16)
        reset()
        return carry

    lax.fori_loop(0, n_q, query_block, 0)


def _attention_pipelined(q, kt, v, kt_tail, v_tail, *, tq, n_sub):
    b, h, t, dqk = q.shape
    tk, tt = kt.shape[4], kt_tail.shape[4]
    assert kt_tail.shape[2] == 1
    return pl.pallas_call(
        functools.partial(_attn_pipe_kernel, n_sub=n_sub),
        grid=(b, h),
        in_specs=[
            pl.BlockSpec((1, 1, t, dqk), lambda bi, hi: (bi, hi, 0, 0)),
            pl.BlockSpec((1, 1) + kt.shape[2:], lambda bi, hi: (bi, hi, 0, 0, 0)),
            pl.BlockSpec((1, 1) + v.shape[2:], lambda bi, hi: (bi, hi, 0, 0)),
            pl.BlockSpec((1, 1) + kt_tail.shape[2:], lambda bi, hi: (bi, hi, 0, 0, 0)),
            pl.BlockSpec((1, 1) + v_tail.shape[2:], lambda bi, hi: (bi, hi, 0, 0)),
        ],
        out_specs=pl.BlockSpec((1, t, MLA_V), lambda bi, hi: (bi, 0, hi)),
        out_shape=jax.ShapeDtypeStruct((b, t, h * MLA_V), BF16),
        scratch_shapes=[pltpu.VMEM((tq, LANES), F32), pltpu.VMEM((tq, MLA_V_EXT), F32),
                        pltpu.VMEM((tq, tk), F32), pltpu.VMEM((tq, tk), F32),
                        pltpu.VMEM((tq, tt), F32)],
        compiler_params=_cparams(("arbitrary", "arbitrary")),
        name="mla_attention_pipe",
    )(q, kt, v, kt_tail, v_tail)


def _attention(q, segs, *, tq, n_sub):
    b, h, t, dqk = q.shape
    in_specs = [pl.BlockSpec((1, 1, tq, dqk), lambda bi, hi, qi: (bi, hi, qi, 0))]
    args = [q]
    for kt, v in segs:
        in_specs.append(pl.BlockSpec((1, 1) + kt.shape[2:], lambda bi, hi, qi: (bi, hi, 0, 0, 0)))
        in_specs.append(pl.BlockSpec((1, 1) + v.shape[2:], lambda bi, hi, qi: (bi, hi, 0, 0)))
        args += [kt, v]
    return pl.pallas_call(
        functools.partial(_attn_kernel, n_seg=len(segs), n_sub=n_sub),
        grid=(b, h, t // tq),
        in_specs=in_specs,
        out_specs=pl.BlockSpec((1, tq, MLA_V), lambda bi, hi, qi: (bi, qi, hi)),
        out_shape=jax.ShapeDtypeStruct((b, t, h * MLA_V), BF16),
        scratch_shapes=[pltpu.VMEM((tq, LANES), F32), pltpu.VMEM((tq, MLA_V_EXT), F32)],
        compiler_params=_cparams(("arbitrary", "arbitrary", "arbitrary")),
        name="mla_attention",
    )(*args)


def _mix_residual_norm(x_ref, of_ref, ob_ref, r_ref, mla_ref, m, gg_ref, wo_ref, ln2_ref,
                       rows=slice(None)):
    o = of_ref[rows, :].astype(F32) + ob_ref[rows, :].astype(F32)
    gg = gg_ref[...]
    y = jnp.concatenate(
        [_rms(o[:, h * GLA_DV:(h + 1) * GLA_DV], gg) for h in range(GLA_HEADS)], axis=1)
    mix = (y * _silu(r_ref[rows, :].astype(F32))).astype(BF16)
    yo = _dot(mix, wo_ref[0:GLA_V_W, :]) + _dot(mla_ref[rows, :], wo_ref[GLA_V_W:, :])
    x1 = x_ref[rows, :] + m[2:3] * yo
    h2 = _rms(x1, ln2_ref[...]) * (1.0 + m[4:5]) + m[3:4]
    return x1, h2


def _outproj_ffn_kernel(*refs, groups, final_norm):
    (x_ref, of_ref, ob_ref, r_ref, mla_ref, mod_ref, gg_ref, wo_ref, ln2_ref,
     wg_ref, wu_ref, wd_ref) = refs[:12]
    fin_ref = refs[12] if final_norm else None
    o_ref = refs[-1]
    m = mod_ref[0]
    x1, h2 = _mix_residual_norm(x_ref, of_ref, ob_ref, r_ref, mla_ref, m, gg_ref, wo_ref, ln2_ref)
    h = h2.astype(BF16)
    y = None
    for lo, hi in groups:
        a = _dot(h, wg_ref[:, lo:hi])
        u = _dot(h, wu_ref[:, lo:hi])
        part = _dot((_silu(a) * u).astype(BF16), wd_ref[lo:hi, :])
        y = part if y is None else y + part
    x2 = x1 + m[5:6] * y
    if final_norm:
        x2 = _rms(x2, fin_ref[...])
    o_ref[...] = x2


def _outproj_ffn(x, o_f, o_b, p, mla, mods, row_fn, gg, wo, ln2, wg, wu, wd, fin_g, tm):
    t, d = x.shape
    ff = wg.shape[1]
    final_norm = fin_g is not None
    step = min(ff, DENSE_FF_GROUP)
    groups = tuple((lo, min(lo + step, ff)) for lo in range(0, ff, step))
    resident = lambda a: pl.BlockSpec(a.shape, lambda i: (0,) * a.ndim, pipeline_mode=pl.Buffered(1))
    in_specs = [
        pl.BlockSpec((tm, d), lambda i: (i, 0)),
        pl.BlockSpec((tm, GLA_V_W), lambda i: (i, 0)),
        pl.BlockSpec((tm, GLA_V_W), lambda i: (i, 0)),
        pl.BlockSpec((tm, GLA_V_W), lambda i: (i, P_R // GLA_V_W)),
        pl.BlockSpec((tm, MLA_V_W), lambda i: (i, 0)),
        pl.BlockSpec((1, 6, d), lambda i: (row_fn(i), 0, 0)),
        resident(gg), resident(wo), resident(ln2), resident(wg), resident(wu), resident(wd),
    ]
    args = [x, o_f, o_b, p, mla, mods, gg, wo, ln2, wg, wu, wd]
    if final_norm:
        in_specs.append(resident(fin_g))
        args.append(fin_g)
    return pl.pallas_call(
        functools.partial(_outproj_ffn_kernel, groups=groups, final_norm=final_norm),
        grid=(t // tm,),
        in_specs=in_specs,
        out_specs=pl.BlockSpec((tm, d), lambda i: (i, 0)),
        out_shape=jax.ShapeDtypeStruct((t, d), F32),
        compiler_params=_cparams(("arbitrary",)),
        name="outproj_ffn",
    )(*args)


def _outproj_kernel(*refs, with_router):
    (x_ref, of_ref, ob_ref, r_ref, mla_ref, mod_ref, gg_ref, wo_ref, ln2_ref) = refs[:9]
    if with_router:
        rwh_ref, rwl_ref, x1_ref, h2_ref, comb_ref = refs[9:]
    else:
        x1_ref, h2_ref = refs[9:]
    m = mod_ref[0]
    tm = x_ref.shape[0]
    n_groups = 2 if tm % 32 == 0 else 1
    for g in range(n_groups):
        rows = slice(g * tm // n_groups, (g + 1) * tm // n_groups)
        x1, h2 = _mix_residual_norm(x_ref, of_ref, ob_ref, r_ref, mla_ref, m, gg_ref, wo_ref,
                                    ln2_ref, rows)
        x1_ref[rows, :] = x1
        if not with_router:
            h2_ref[rows, :] = h2.astype(BF16)
            continue
        h2_ref[rows, :] = _pack_bf16_pairs(h2)
        h_hi = h2.astype(BF16)
        h_lo = (h2 - h_hi.astype(F32)).astype(BF16)
        logits = _dot(h_hi, rwh_ref[...]) + _dot(h_lo, rwh_ref[...]) + _dot(h_hi, rwl_ref[...])
        lane = lax.broadcasted_iota(jnp.int32, logits.shape, 1).astype(F32)
        neg = jnp.float32(-jnp.inf)
        logits = jnp.where(lane < N_EXPERTS, logits, neg)
        m1 = jnp.max(logits, axis=1, keepdims=True)
        i1 = jnp.min(jnp.where(logits == m1, lane, float(LANES)), axis=1, keepdims=True)
        rest = jnp.where(lane == i1, neg, logits)
        m2 = jnp.max(rest, axis=1, keepdims=True)
        i2 = jnp.min(jnp.where(rest == m2, lane, float(LANES)), axis=1, keepdims=True)
        e2 = jnp.exp(m2 - m1)
        w1 = 1.0 / (1.0 + e2)
        comb_ref[rows, :] = (jnp.where(lane == ROUTE_E1, i1, 0.0) + jnp.where(lane == ROUTE_E2, i2, 0.0)
                             + jnp.where(lane == ROUTE_W1, w1, 0.0)
                             + jnp.where(lane == ROUTE_W2, e2 * w1, 0.0))


def _outproj(x, o_f, o_b, p, mla, mods, row_fn, gg, wo, ln2, router, tm):
    t, d = x.shape
    with_router = router is not None
    full = lambda a: pl.BlockSpec(a.shape, lambda i: (0,) * a.ndim)
    in_specs = [
        pl.BlockSpec((tm, d), lambda i: (i, 0)),
        pl.BlockSpec((tm, GLA_V_W), lambda i: (i, 0)),
        pl.BlockSpec((tm, GLA_V_W), lambda i: (i, 0)),
        pl.BlockSpec((tm, GLA_V_W), lambda i: (i, P_R // GLA_V_W)),
        pl.BlockSpec((tm, MLA_V_W), lambda i: (i, 0)),
        pl.BlockSpec((1, 6, d), lambda i: (row_fn(i), 0, 0)),
        full(gg), full(wo), full(ln2),
    ]
    args = [x, o_f, o_b, p, mla, mods, gg, wo, ln2]
    h2_shape = jax.ShapeDtypeStruct((t, d // 2), F32) if with_router else jax.ShapeDtypeStruct((t, d), BF16)
    out_specs = [pl.BlockSpec((tm, d), lambda i: (i, 0)),
                 pl.BlockSpec((tm, h2_shape.shape[1]), lambda i: (i, 0))]
    out_shape = [jax.ShapeDtypeStruct((t, d), F32), h2_shape]
    if with_router:
        in_specs += [full(router[0]), full(router[1])]
        args += list(router)
        out_specs.append(pl.BlockSpec((tm, LANES), lambda i: (i, 0)))
        out_shape.append(jax.ShapeDtypeStruct((t, LANES), F32))
    return pl.pallas_call(
        functools.partial(_outproj_kernel, with_router=with_router),
        grid=(t // tm,),
        in_specs=in_specs,
        out_specs=out_specs,
        out_shape=out_shape,
        compiler_params=_cparams(("arbitrary",)),
        name="outproj",
    )(*args)


def _ffn_kernel(*refs, with_comb, final_norm):
    h_ref, x1_ref, mod_ref = refs[:3]
    k = 3
    comb_ref = fin_ref = None
    if with_comb:
        comb_ref = refs[k]
        k += 1
    wg_ref, wu_ref, wd_ref = refs[k:k + 3]
    k += 3
    if final_norm:
        fin_ref = refs[k]
        k += 1
    o_ref, acc = refs[k:]
    e = pl.program_id(1)
    f = pl.program_id(2)

    @pl.when((e == 0) & (f == 0))
    def _():
        acc[...] = jnp.zeros(acc.shape, F32)

    h = h_ref[...]
    a = _dot(h, wg_ref[0])
    u = _dot(h, wu_ref[0])
    act = _silu(a) * u
    if with_comb:
        comb = comb_ref[...]
        lane = lax.broadcasted_iota(jnp.int32, comb.shape, 1)
        act = act * jnp.sum(jnp.where(lane == e, comb, 0.0), axis=1, keepdims=True)
    acc[...] += _dot(act.astype(BF16), wd_ref[0])

    @pl.when((e == pl.num_programs(1) - 1) & (f == pl.num_programs(2) - 1))
    def _():
        x2 = x1_ref[...] + mod_ref[0][5:6] * acc[...]
        if final_norm:
            x2 = _rms(x2, fin_ref[...])
        o_ref[...] = x2


def _ffn(h2, x1, mods, row_fn, comb, wg, wu, wd, fin_g, tm, tf):
    t, d = x1.shape
    n_e, _, ff = wg.shape
    with_comb = comb is not None
    final_norm = fin_g is not None
    in_specs = [
        pl.BlockSpec((tm, d), lambda i, e, f: (i, 0)),
        pl.BlockSpec((tm, d), lambda i, e, f: (i, 0)),
        pl.BlockSpec((1, 6, d), lambda i, e, f: (row_fn(i), 0, 0)),
    ]
    args = [h2, x1, mods]
    if with_comb:
        in_specs.append(pl.BlockSpec((tm, LANES), lambda i, e, f: (i, 0)))
        args.append(comb)
    in_specs += [
        pl.BlockSpec((1, d, tf), lambda i, e, f: (e, 0, f)),
        pl.BlockSpec((1, d, tf), lambda i, e, f: (e, 0, f)),
        pl.BlockSpec((1, tf, d), lambda i, e, f: (e, f, 0)),
    ]
    args += [wg, wu, wd]
    if final_norm:
        in_specs.append(pl.BlockSpec((1, d), lambda i, e, f: (0, 0)))
        args.append(fin_g)
    return pl.pallas_call(
        functools.partial(_ffn_kernel, with_comb=with_comb, final_norm=final_norm),
        grid=(t // tm, n_e, ff // tf),
        in_specs=in_specs,
        out_specs=pl.BlockSpec((tm, d), lambda i, e, f: (i, 0)),
        out_shape=jax.ShapeDtypeStruct((t, d), F32),
        scratch_shapes=[pltpu.VMEM((tm, d), F32)],
        compiler_params=_cparams(("arbitrary", "arbitrary", "arbitrary")),
        name="ffn",
    )(*args)


def _sc_row_gather(table, idx):
    _, w = table.shape
    b = idx.shape[0]
    n_workers = SC_CORES * SC_SUBCORES
    assert b % (n_workers * SC_GATHER_ROWS) == 0, (b, n_workers, SC_GATHER_ROWS)
    b_per_w = b // n_workers
    n_chunks = b_per_w // SC_GATHER_ROWS
    mesh = plsc.VectorSubcoreMesh(core_axis_name="c", subcore_axis_name="s",
                                  num_cores=SC_CORES, num_subcores=SC_SUBCORES)

    def body(table_hbm, idx_hbm, out_hbm, idx_a, idx_b, rows_a, rows_b, sem_a, sem_b):
        wid = lax.axis_index("s") * SC_CORES + lax.axis_index("c")
        base = wid * b_per_w
        idx_bufs, row_bufs, sems = (idx_a, idx_b), (rows_a, rows_b), (sem_a, sem_b)

        def start(ci):
            slot = ci % 2
            pltpu.sync_copy(idx_hbm.at[pl.ds(base + ci * SC_GATHER_ROWS, SC_GATHER_ROWS)], idx_bufs[slot])
            return pltpu.async_copy(table_hbm.at[idx_bufs[slot]], row_bufs[slot], sems[slot])

        pending = start(0)
        for ci in range(n_chunks):
            following = start(ci + 1) if ci + 1 < n_chunks else None
            pending.wait()
            pltpu.sync_copy(row_bufs[ci % 2], out_hbm.at[pl.ds(base + ci * SC_GATHER_ROWS, SC_GATHER_ROWS)])
            pending = following

    return pl.kernel(
        body,
        out_type=jax.ShapeDtypeStruct((b, w), F32),
        mesh=mesh,
        scratch_types=[pltpu.VMEM((SC_GATHER_ROWS,), jnp.int32)] * 2
        + [pltpu.VMEM((SC_GATHER_ROWS, w), F32)] * 2 + [pltpu.SemaphoreType.DMA] * 2,
        name="sc_row_gather",
    )(table, idx)


def _sc_row_scatter2(table, pos, n_out):
    t, w = table.shape
    n_workers = SC_CORES * SC_SUBCORES
    assert t % (n_workers * SC_GATHER_ROWS) == 0, (t, n_workers, SC_GATHER_ROWS)
    t_per_w = t // n_workers
    n_chunks = t_per_w // SC_GATHER_ROWS
    mesh = plsc.VectorSubcoreMesh(core_axis_name="c", subcore_axis_name="s",
                                  num_cores=SC_CORES, num_subcores=SC_SUBCORES)

    def body(table_hbm, pos_hbm, out_hbm, i0a, i1a, i0b, i1b, rows_a, rows_b, sem_a, sem_b):
        wid = lax.axis_index("s") * SC_CORES + lax.axis_index("c")
        base = wid * t_per_w
        idx0, idx1, row_bufs, sems = (i0a, i0b), (i1a, i1b), (rows_a, rows_b), (sem_a, sem_b)

        def drain(pair):
            if pair is not None:
                pair[0].wait()
                pair[1].wait()

        pending = [None, None]
        for ci in range(n_chunks):
            slot = ci % 2
            drain(pending[slot])
            off = base + ci * SC_GATHER_ROWS
            pltpu.sync_copy(pos_hbm.at[pl.ds(off, SC_GATHER_ROWS)], idx0[slot])
            pltpu.sync_copy(pos_hbm.at[pl.ds(t + off, SC_GATHER_ROWS)], idx1[slot])
            pltpu.sync_copy(table_hbm.at[pl.ds(off, SC_GATHER_ROWS)], row_bufs[slot])
            pending[slot] = (pltpu.async_copy(row_bufs[slot], out_hbm.at[idx0[slot]], sems[slot]),
                             pltpu.async_copy(row_bufs[slot], out_hbm.at[idx1[slot]], sems[slot]))
        drain(pending[0])
        drain(pending[1])

    return pl.kernel(
        body,
        out_type=jax.ShapeDtypeStruct((n_out, w), F32),
        mesh=mesh,
        scratch_types=[pltpu.VMEM((SC_GATHER_ROWS,), jnp.int32)] * 4
        + [pltpu.VMEM((SC_GATHER_ROWS, w), F32)] * 2 + [pltpu.SemaphoreType.DMA] * 2,
        name="sc_row_scatter",
    )(table, pos)


def _moe_plan_kernel(route_ref, tri_ref, utri_ref, pos_ref, cnt_ref, run_scr, off_scr):
    phase = pl.program_id(0)
    blk = pl.program_id(1)
    route = route_ref[...]
    lane = lax.broadcasted_iota(jnp.int32, route.shape, 1).astype(F32)
    oh1 = jnp.where(lane == route[:, ROUTE_E1:ROUTE_E1 + 1], 1.0, 0.0)
    oh2 = jnp.where(lane == route[:, ROUTE_E2:ROUTE_E2 + 1], 1.0, 0.0)
    oh = oh1 + oh2

    @pl.when(blk == 0)
    def _():
        run_scr[...] = jnp.zeros(run_scr.shape, F32)

    @pl.when(phase == 0)
    def _():
        run_scr[...] += jnp.sum(oh, axis=0, keepdims=True)

        @pl.when(blk == pl.num_programs(1) - 1)
        def _():
            counts = run_scr[...]
            cnt_ref[...] = counts
            tiles_per = jnp.floor((counts + (MOE_TILE - 1.0)) * (1.0 / MOE_TILE))
            tile_end = _dot(jnp.broadcast_to(tiles_per, (8, LANES)).astype(BF16), utri_ref[...])[0:1]
            off_scr[...] = (tile_end - tiles_per) * float(MOE_TILE)

    @pl.when(phase == 1)
    def _():
        incl = _dot(tri_ref[...], oh.astype(BF16))
        before = incl - oh + run_scr[...] + off_scr[...]
        p1 = jnp.sum(before * oh1, axis=1, keepdims=True)
        p2 = jnp.sum(before * oh2, axis=1, keepdims=True)
        pos = jnp.where(lane == 0.0, p1, 0.0) + jnp.where(lane == 1.0, p2, 0.0)
        pos_ref[...] = pos.astype(jnp.int32)
        run_scr[...] += incl[incl.shape[0] - 1:, :]


def _moe_plan(route, n_tiles, tm):
    t = route.shape[0]
    tri = jnp.asarray(np.tril(np.ones((tm, tm), np.float32)), dtype=BF16)
    utri = jnp.asarray(np.triu(np.ones((LANES, LANES), np.float32)), dtype=BF16)
    pos, counts = pl.pallas_call(
        _moe_plan_kernel,
        grid=(2, t // tm),
        in_specs=[
            pl.BlockSpec((tm, LANES), lambda p, i: (i, 0)),
            pl.BlockSpec((tm, tm), lambda p, i: (0, 0)),
            pl.BlockSpec((LANES, LANES), lambda p, i: (0, 0)),
        ],
        out_specs=[
            pl.BlockSpec((tm, LANES), lambda p, i: (i * p, 0)),
            pl.BlockSpec((1, LANES), lambda p, i: (0, 0)),
        ],
        out_shape=[jax.ShapeDtypeStruct((t, LANES), jnp.int32),
                   jax.ShapeDtypeStruct((1, LANES), F32)],
        scratch_shapes=[pltpu.VMEM((1, LANES), F32), pltpu.VMEM((1, LANES), F32)],
        compiler_params=_cparams(("arbitrary", "arbitrary")),
        name="moe_plan",
    )(route, tri, utri)
    counts = counts[0, :N_EXPERTS].astype(jnp.int32)
    tile_end = jnp.cumsum((counts + MOE_TILE - 1) // MOE_TILE)
    n_used = tile_end[-1]
    tile_ids = jnp.minimum(jnp.arange(n_tiles, dtype=jnp.int32), n_used - 1)
    tile_expert = jnp.sum((tile_end[None, :] <= tile_ids[:, None]).astype(jnp.int32), axis=1)
    return pos[:, :2].T, tile_expert, n_used.reshape(1)


def _moe_ffn_kernel(te_ref, nused_ref, xs_ref, wg_ref, wu_ref, wd_ref, o_ref, acc, h_scr, *, splits):
    i = pl.program_id(0)
    f = pl.program_id(1)
    last_f = pl.num_programs(1) - 1
    used = i < nused_ref[0]

    @pl.when(used)
    def _():
        @pl.when(f == 0)
        def _():
            acc[...] = jnp.zeros(acc.shape, F32)
            h_scr[...] = _unpack_bf16_pairs(xs_ref[...]).astype(BF16)

        h = h_scr[...]
        for lo, hi in splits:
            a = _dot(h, wg_ref[0, :, lo:hi].astype(BF16))
            u = _dot(h, wu_ref[0, :, lo:hi].astype(BF16))
            acc[...] += _dot((_silu(a) * u).astype(BF16), wd_ref[0, lo:hi, :].astype(BF16))

        @pl.when(f == last_f)
        def _():
            o_ref[...] = _pack_bf16_pairs(acc[...])

    @pl.when(jnp.logical_not(used) & (f == last_f))
    def _():
        o_ref[...] = jnp.zeros(o_ref.shape, F32)


def _moe_ffn(xs, tile_expert, n_used, wg, wu, wd):
    rows, half = xs.shape
    d = 2 * half
    ff = wg.shape[2]
    if ff % MOE_FF_TILE == 0:
        tf, splits = MOE_FF_TILE, MOE_FF_SPLITS
    else:
        tf, splits = ff, ((0, ff),)
    n_tiles = rows // MOE_TILE
    grid_spec = pltpu.PrefetchScalarGridSpec(
        num_scalar_prefetch=2,
        grid=(n_tiles, ff // tf),
        in_specs=[
            pl.BlockSpec((MOE_TILE, half), lambda i, f, te, nu: (i, 0)),
            pl.BlockSpec((1, d, tf), lambda i, f, te, nu: (te[i], 0, f)),
            pl.BlockSpec((1, d, tf), lambda i, f, te, nu: (te[i], 0, f)),
            pl.BlockSpec((1, tf, d), lambda i, f, te, nu: (te[i], f, 0)),
        ],
        out_specs=pl.BlockSpec((MOE_TILE, half), lambda i, f, te, nu: (i, 0)),
        scratch_shapes=[pltpu.VMEM((MOE_TILE, d), F32), pltpu.VMEM((MOE_TILE, d), BF16)],
    )
    return pl.pallas_call(
        functools.partial(_moe_ffn_kernel, splits=splits),
        grid_spec=grid_spec,
        out_shape=jax.ShapeDtypeStruct((rows, half), F32),
        compiler_params=_cparams(("arbitrary", "arbitrary")),
        name="moe_ffn",
    )(tile_expert, n_used, xs, wg, wu, wd)


def _combine_kernel(*refs, final_norm):
    x1_ref, y0_ref, y1_ref, route_ref, mod_ref = refs[:5]
    fin_ref = refs[5] if final_norm else None
    o_ref = refs[-1]
    route = route_ref[...]
    w1 = route[:, ROUTE_W1:ROUTE_W1 + 1]
    w2 = route[:, ROUTE_W2:ROUTE_W2 + 1]
    y = w1 * _unpack_bf16_pairs(y0_ref[...]) + w2 * _unpack_bf16_pairs(y1_ref[...])
    x2 = x1_ref[...] + mod_ref[0][5:6] * y
    if final_norm:
        x2 = _rms(x2, fin_ref[...])
    o_ref[...] = x2


def _combine(x1, yg, route, mods, row_fn, fin_g, tm):
    t, d = x1.shape
    nb = t // tm
    final_norm = fin_g is not None
    in_specs = [
        pl.BlockSpec((tm, d), lambda i: (i, 0)),
        pl.BlockSpec((tm, d // 2), lambda i: (i, 0)),
        pl.BlockSpec((tm, d // 2), lambda i: (i + nb, 0)),
        pl.BlockSpec((tm, LANES), lambda i: (i, 0)),
        pl.BlockSpec((1, 6, d), lambda i: (row_fn(i), 0, 0)),
    ]
    args = [x1, yg, yg, route, mods]
    if final_norm:
        in_specs.append(pl.BlockSpec((1, d), lambda i: (0, 0)))
        args.append(fin_g)
    return pl.pallas_call(
        functools.partial(_combine_kernel, final_norm=final_norm),
        grid=(nb,),
        in_specs=in_specs,
        out_specs=pl.BlockSpec((tm, d), lambda i: (i, 0)),
        out_shape=jax.ShapeDtypeStruct((t, d), F32),
        compiler_params=_cparams(("arbitrary",)),
        name="moe_combine",
    )(*args)


def _moe(h2, x1, route, mods, row_fn, wg, wu, wd, fin_g, tm):
    t = h2.shape[0]
    n_tiles = -(-2 * t // MOE_TILE) + N_EXPERTS
    pos, tile_expert, n_used = _moe_plan(route, n_tiles, _pick_tile(t, ATTN_TILE))
    pos = pos.reshape(-1)
    xs = _sc_row_scatter2(h2, pos, n_tiles * MOE_TILE)
    ys = _moe_ffn(xs, tile_expert, n_used, wg, wu, wd)
    yg = _sc_row_gather(ys, pos)
    return _combine(x1, yg, route, mods, row_fn, fin_g, tm)


def _rope_partner():
    j = np.arange(MLA_ROPE)
    return np.where((j % 32) < 16, j + 16, j - 16)


def _prep_in_weight(w):
    d = w.shape[0]
    cols = [w[:, 0:1024], w[:, 1056:1568], w[:, 1568:1824], w[:, 1824:1952], w[:, 1952:2016],
            w[:, 1024:1056], jnp.zeros((d, P_WIDTH - 2016), w.dtype)]
    return jnp.concatenate(cols, axis=1).astype(BF16)


def _prep_gate_weight(w_g2, b_g2):
    ws = []
    for z, off in ((0, MISC_GF), (1, MISC_GB)):
        ws.append(jnp.zeros((LANES, GLA_QK_W), F32).at[off:off + GLA_GATE_RANK].set(w_g2[z]))
    return jnp.stack(ws).astype(BF16), b_g2.reshape(2, 1, GLA_QK_W)


def _prep_mla_weights(w_uq, w_ukv):
    partner = _rope_partner()
    wq = w_uq.reshape(MLA_Q_RANK, MLA_HEADS, MLA_QK)
    wqn = wq[:, :, :MLA_NOPE].reshape(MLA_Q_RANK, MLA_HEADS * MLA_NOPE)
    rope = wq[:, :, MLA_NOPE:]
    pad = jnp.zeros((MLA_Q_RANK, MLA_HEADS, LANES - MLA_ROPE), w_uq.dtype)
    wqr = jnp.concatenate([rope, pad], axis=2).reshape(MLA_Q_RANK, MLA_HEADS * LANES)
    wqs = jnp.concatenate([rope[:, :, partner], pad], axis=2).reshape(MLA_Q_RANK, MLA_HEADS * LANES)
    wkv = w_ukv.reshape(MLA_KV_RANK, MLA_HEADS, MLA_NOPE + MLA_V)
    wknt = wkv[:, :, :MLA_NOPE].reshape(MLA_KV_RANK, MLA_HEADS * MLA_NOPE).T
    wv = wkv[:, :, MLA_NOPE:].reshape(MLA_KV_RANK, MLA_HEADS * MLA_V)
    perm = np.zeros((LANES, LANES), np.float32)
    perm[partner, np.arange(MLA_ROPE)] = 1.0
    eye = np.eye(MLA_ROPE, LANES, dtype=np.float32)
    return (wqn.astype(BF16), wqr.astype(BF16), wqs.astype(BF16), wknt.astype(BF16),
            wv.astype(BF16), jnp.asarray(perm, BF16), jnp.asarray(eye, BF16))


def _rope_tables(n_tok):
    rows = n_tok // GRID_W
    row = np.repeat(np.arange(rows, dtype=np.float32), GRID_W)
    col = np.tile(np.arange(GRID_W, dtype=np.float32), rows)
    nfreq = MLA_ROPE // 4
    inv = np.float32(ROPE_BASE) ** (-np.arange(nfreq, dtype=np.float32) / np.float32(nfreq))
    ar = (row[:, None] * inv).astype(np.float32)
    ac = (col[:, None] * inv).astype(np.float32)
    zero = np.zeros((n_tok, LANES - MLA_ROPE), np.float32)
    cos = np.concatenate([np.cos(ar), np.cos(ar), np.cos(ac), np.cos(ac), zero], axis=1)
    sin = np.concatenate([-np.sin(ar), np.sin(ar), -np.sin(ac), np.sin(ac), zero], axis=1)
    return jnp.asarray(cos, F32), jnp.asarray(sin, F32)


def _identity_tables(n_tok):
    cos = jnp.concatenate([jnp.ones((n_tok, MLA_ROPE), F32),
                           jnp.zeros((n_tok, LANES - MLA_ROPE), F32)], axis=1)
    return cos, jnp.zeros((n_tok, LANES), F32)


def _pick_tile(n, pref):
    t = min(n, pref)
    while n % t:
        t //= 2
    return t


def _pick_ff_tile(ff):
    best = LANES
    for m in range(1, ff // LANES + 1):
        if ff % (m * LANES) == 0 and m * LANES <= FFN_MAX_FF_TILE:
            best = m * LANES
    return best


@jax.jit
def _forward(x, c, ctx, c_ctx, w_mod, b_mod, ln1_g, ln2_g, w_in, w_gla_g2, b_gla_g2, gla_norm_g,
             mla_q_norm_g, w_uq, mla_kv_norm_g, w_ukv, w_out, ffn_w_gate, ffn_w_up, ffn_w_down,
             router_w, exp_w_gate, exp_w_up, exp_w_down, final_norm_g):
    batch, seq, d = x.shape
    n_ctx = ctx.shape[1]
    depth = w_mod.shape[0]

    cvec = jnp.zeros((8, d), F32).at[:batch].set(c).at[batch].set(c_ctx)
    mods_all = _modulation(cvec, w_mod, b_mod).reshape(depth, 8, 6, d)

    xl = x.reshape(batch * seq, d)
    xc = ctx.reshape(batch * n_ctx, d)

    tm_l = _pick_tile(seq, TOKEN_TILE)
    tm_c = _pick_tile(n_ctx, CTX_TILE)
    tk_l = _pick_tile(seq, ATTN_TILE)
    cb_l = _pick_tile(seq, GLA_BLOCK)
    cb_c = _pick_tile(n_ctx, GLA_BLOCK)
    row_l = lambda tm: (lambda i: i // (seq // tm))
    row_c = lambda i: batch

    rope_l = _rope_tables(seq)
    rope_c = _identity_tables(tm_c)
    zero_state = jnp.zeros((2, batch, GLA_QK_W, GLA_DV), F32)

    for i in range(depth):
        need_ctx = i < depth - 1
        last = i == depth - 1
        mods = mods_all[i]
        ln1 = ln1_g[i].reshape(1, d)
        ln2 = ln2_g[i].reshape(1, d)
        w_in_r = _prep_in_weight(w_in[i])
        gates = _prep_gate_weight(w_gla_g2[i], b_gla_g2[i])
        mla_w = _prep_mla_weights(w_uq[i], w_ukv[i])
        qg = mla_q_norm_g[i].reshape(1, MLA_Q_RANK)
        kvg = mla_kv_norm_g[i].reshape(1, MLA_KV_RANK)
        gg = gla_norm_g[i].reshape(1, GLA_DV)
        wo = w_out[i].astype(BF16)

        p_l = _inproj(xl, mods, row_l(tk_l), ln1, w_in_r, tk_l)
        p_c = _inproj(xc, mods, row_c, ln1, w_in_r, tm_c)

        oc_f, oc_b, s_ctx = _gla(p_c, *gates, zero_state, batch=batch, cb=cb_c)
        ol_f, ol_b, _ = _gla(p_l, *gates, s_ctx, batch=batch, cb=cb_l)

        q_l, kt_l, v_l = _mlaprep(p_l, *rope_l, qg, kvg, mla_w, batch=batch, tm=tk_l)
        q_c, kt_c, v_c = _mlaprep(p_c, *rope_c, qg, kvg, mla_w, batch=batch, tm=tm_c)
        m_l = _attention_pipelined(q_l, kt_l, v_l, kt_c, v_c, tq=tk_l, n_sub=1)
        m_l = m_l.reshape(batch * seq, MLA_V_W)

        if i % 2 == 0:
            j = i // 2
            router = None
            wg = ffn_w_gate[j].astype(BF16)
            wu = ffn_w_up[j].astype(BF16)
            wd = ffn_w_down[j].astype(BF16)
        else:
            j = i // 2
            rw = jnp.zeros((d, LANES), F32).at[:, :N_EXPERTS].set(router_w[j])
            rw_hi = rw.astype(BF16)
            router = (rw_hi, (rw - rw_hi.astype(F32)).astype(BF16))
            wg, wu, wd = exp_w_gate[j], exp_w_up[j], exp_w_down[j]
        fin = final_norm_g.reshape(1, d) if last else None

        if router is None:
            xl = _outproj_ffn(xl, ol_f, ol_b, p_l, m_l, mods, row_l(tm_l), gg, wo, ln2,
                              wg, wu, wd, fin, tm_l)
        else:
            outs = _outproj(xl, ol_f, ol_b, p_l, m_l, mods, row_l(tm_l), gg, wo, ln2, router, tm_l)
            xl = _moe(outs[1], outs[0], outs[2], mods, row_l(tm_l), wg, wu, wd, fin, tm_l)

        if need_ctx:
            m_c = _attention(q_c, [(kt_c, v_c)], tq=tm_c, n_sub=1).reshape(batch * n_ctx, MLA_V_W)
            if router is None:
                xc = _outproj_ffn(xc, oc_f, oc_b, p_c, m_c, mods, row_c, gg, wo, ln2,
                                  wg, wu, wd, None, _pick_tile(batch * n_ctx, TOKEN_TILE))
            else:
                outs_c = _outproj(xc, oc_f, oc_b, p_c, m_c, mods, row_c, gg, wo, ln2, router, tm_c)
                tm_fc = _pick_tile(batch * n_ctx, TOKEN_TILE)
                r_c = outs_c[2]
                lane = jnp.arange(LANES, dtype=F32)[None, :]
                comb_c = (jnp.where(lane == r_c[:, ROUTE_E1:ROUTE_E1 + 1], r_c[:, ROUTE_W1:ROUTE_W1 + 1], 0.0)
                          + jnp.where(lane == r_c[:, ROUTE_E2:ROUTE_E2 + 1], r_c[:, ROUTE_W2:ROUTE_W2 + 1], 0.0))
                bits = lax.bitcast_convert_type(outs_c[1], jnp.uint32)
                h2_c = jnp.concatenate([lax.bitcast_convert_type(bits << 16, F32),
                                        lax.bitcast_convert_type(bits & jnp.uint32(0xFFFF0000), F32)], axis=1)
                xc = _ffn(h2_c.astype(BF16), outs_c[0], mods, row_c, comb_c,
                          wg.astype(BF16), wu.astype(BF16), wd.astype(BF16), None, tm_fc,
                          _pick_ff_tile(wg.shape[2]))

    return xl.reshape(batch, seq, d)


def kernel(x, c, ctx, c_ctx, w_mod, b_mod, ln1_g, ln2_g, w_in, w_gla_g2, b_gla_g2, gla_norm_g,
           mla_q_norm_g, w_uq, mla_kv_norm_g, w_ukv, w_out, ffn_w_gate, ffn_w_up, ffn_w_down,
           router_w, exp_w_gate, exp_w_up, exp_w_down, final_norm_g):
    return _forward(x, c, ctx, c_ctx, w_mod, b_mod, ln1_g, ln2_g, w_in, w_gla_g2, b_gla_g2,
                    gla_norm_g, mla_q_norm_g, w_uq, mla_kv_norm_g, w_ukv, w_out, ffn_w_gate,
                    ffn_w_up, ffn_w_down, router_w, exp_w_gate, exp_w_up, exp_w_down, final_norm_g)
```

```python
import functools

import numpy as np
import jax
import jax.numpy as jnp
from jax import lax
from jax.experimental import pallas as pl
from jax.experimental.pallas import tpu as pltpu
from jax.experimental.pallas import tpu_sc as plsc

F32 = jnp.float32
BF16 = jnp.bfloat16

D_MODEL = 1024
EPS = 1e-6
GRID_W = 64

GLA_HEADS = 4
GLA_DK = 64
GLA_DV = 128
GLA_GATE_RANK = 16
GLA_GATE_NORM = 16.0
GLA_CHUNK = 64
GLA_QK_W = GLA_HEADS * GLA_DK
GLA_V_W = GLA_HEADS * GLA_DV
GLA_EXP_CLAMP = 80.0

MLA_HEADS = 4
MLA_NOPE = 128
MLA_ROPE = 64
MLA_V = 128
MLA_QK = MLA_NOPE + MLA_ROPE
MLA_Q_RANK = 256
MLA_KV_RANK = 128
MLA_SCALE = MLA_QK ** -0.5
MLA_Q_SCALE = MLA_SCALE * 1.4426950408889634
MLA_V_W = MLA_HEADS * MLA_V
MLA_V_EXT = 2 * MLA_V
ROPE_BASE = 10000.0

N_EXPERTS = 8
LANES = 128
ROUTE_E1, ROUTE_E2, ROUTE_W1, ROUTE_W2 = 0, 1, 2, 3

SC_CORES = 2
SC_SUBCORES = 16
SC_GATHER_ROWS = 64
MOE_TILE = 1024
MOE_FF_TILE = 512
MOE_FF_SPLITS = ((0, 256), (256, 512))

P_Q, P_K, P_V, P_R, P_CQ, P_CKV, P_MISC = 0, 256, 512, 1024, 1536, 1792, 1920
P_WIDTH = 2048
MISC_KR, MISC_GF, MISC_GB = 0, 64, 80

VMEM_LIMIT = 56 * 1024 * 1024

TOKEN_TILE = 512
CTX_TILE = 256
ATTN_TILE = 1024
GLA_BLOCK = 256
DENSE_FF_GROUP = 1024
FFN_MAX_FF_TILE = 1408


def _cparams(sem):
    return pltpu.CompilerParams(dimension_semantics=sem, vmem_limit_bytes=VMEM_LIMIT)


def _rms(x, g):
    return x * lax.rsqrt(jnp.mean(x * x, axis=-1, keepdims=True) + EPS) * g


def _silu(x):
    return x / (1.0 + jnp.exp(-x))


def _dot(a, b):
    return jnp.dot(a, b, preferred_element_type=F32)


def _dot_nt(a, b):
    return lax.dot_general(a, b, (((1,), (1,)), ((), ())), preferred_element_type=F32)


def _dot_tn(a, b):
    return lax.dot_general(a, b, (((0,), (0,)), ((), ())), preferred_element_type=F32)


def _pack_bf16_pairs(x):
    w = x.shape[1] // 2
    words = pltpu.pack_elementwise([x[:, :w], x[:, w:]], packed_dtype=BF16)
    return lax.bitcast_convert_type(words, F32)


def _unpack_bf16_pairs(p):
    words = lax.bitcast_convert_type(p, jnp.int32)
    lo = pltpu.unpack_elementwise(words, index=0, packed_dtype=BF16, unpacked_dtype=F32)
    hi = pltpu.unpack_elementwise(words, index=1, packed_dtype=BF16, unpacked_dtype=F32)
    return jnp.concatenate([lo, hi], axis=1)


def _mod_kernel(c_ref, w_ref, b_ref, o_ref):
    s = _silu(c_ref[...]).astype(BF16)
    o_ref[0] = _dot(s, w_ref[0].astype(BF16)) + b_ref[0]


def _modulation(cvec, w_mod, b_mod):
    depth, d, n = w_mod.shape
    tn = 1536
    return pl.pallas_call(
        _mod_kernel,
        grid=(depth, n // tn),
        in_specs=[
            pl.BlockSpec((8, d), lambda l, j: (0, 0)),
            pl.BlockSpec((1, d, tn), lambda l, j: (l, 0, j)),
            pl.BlockSpec((1, 1, tn), lambda l, j: (l, 0, j)),
        ],
        out_specs=pl.BlockSpec((1, 8, tn), lambda l, j: (l, 0, j)),
        out_shape=jax.ShapeDtypeStruct((depth, 8, n), F32),
        compiler_params=_cparams(("arbitrary", "arbitrary")),
        name="modulation",
    )(cvec, w_mod, b_mod.reshape(depth, 1, n))


def _inproj_kernel(x_ref, mod_ref, g_ref, w_ref, o_ref):
    m = mod_ref[0]
    tm = x_ref.shape[0]
    n_groups = 2 if tm % 32 == 0 else 1
    for g in range(n_groups):
        rows = slice(g * tm // n_groups, (g + 1) * tm // n_groups)
        h = _rms(x_ref[rows, :], g_ref[...]) * (1.0 + m[1:2]) + m[0:1]
        o_ref[rows, :] = _dot(h.astype(BF16), w_ref[...]).astype(BF16)


def _inproj(x, mods, row_fn, ln_g, w, tm):
    t, d = x.shape
    return pl.pallas_call(
        _inproj_kernel,
        grid=(t // tm,),
        in_specs=[
            pl.BlockSpec((tm, d), lambda i: (i, 0)),
            pl.BlockSpec((1, 6, d), lambda i: (row_fn(i), 0, 0)),
            pl.BlockSpec((1, d), lambda i: (0, 0)),
            pl.BlockSpec((d, P_WIDTH), lambda i: (0, 0)),
        ],
        out_specs=pl.BlockSpec((tm, P_WIDTH), lambda i: (i, 0)),
        out_shape=jax.ShapeDtypeStruct((t, P_WIDTH), BF16),
        compiler_params=_cparams(("arbitrary",)),
        name="inproj",
    )(x, mods, ln_g, w)


def _gla_direction(q_ref, k_ref, v_ref, misc_ref, wg, bg, tri, s_scr, o_ref, *, reverse, n_chunks):
    c_len = GLA_CHUNK
    pre = _dot(misc_ref[...], wg) + bg
    g = (jnp.minimum(pre, 0.0) - jnp.log(1.0 + jnp.exp(-jnp.abs(pre)))) * (1.0 / GLA_GATE_NORM)
    g_hi = g.astype(BF16)
    g_lo = (g - g_hi.astype(F32)).astype(BF16)
    cum = _dot(tri, g_hi) + _dot(tri, g_lo)
    tot_rows = jnp.concatenate(
        [cum[c * c_len:c * c_len + 1] if reverse else cum[(c + 1) * c_len - 1:(c + 1) * c_len]
         for c in range(n_chunks)] + [jnp.zeros((8 - n_chunks, GLA_QK_W), F32)], axis=0)
    t_hi = tot_rows.astype(BF16)
    t_lo = (tot_rows - t_hi.astype(F32)).astype(BF16)
    eye = (lax.broadcasted_iota(jnp.int32, (GLA_QK_W, GLA_QK_W), 0)
           == lax.broadcasted_iota(jnp.int32, (GLA_QK_W, GLA_QK_W), 1))
    eye = jnp.where(eye, 1.0, 0.0).astype(BF16)
    tot_cols = _dot_nt(eye, t_hi) + _dot_nt(eye, t_lo)

    lane = lax.broadcasted_iota(jnp.int32, (c_len, GLA_QK_W), 1)
    head_masks = [(lane >= h * GLA_DK) & (lane < (h + 1) * GLA_DK) for h in range(GLA_HEADS)]
    row = lax.broadcasted_iota(jnp.int32, (GLA_HEADS * c_len, c_len), 0) % c_len
    col = lax.broadcasted_iota(jnp.int32, (GLA_HEADS * c_len, c_len), 1)
    pair_mask = (col >= row) if reverse else (col <= row)

    def stack_heads(a):
        return jnp.concatenate([jnp.where(mk, a, 0.0) for mk in head_masks], axis=0).astype(BF16)

    def step(c):
        sl = slice(c * c_len, (c + 1) * c_len)
        xc = cum[sl]
        tot = tot_rows[c:c + 1]
        ref = xc[c_len // 2:c_len // 2 + 1]
        qc = q_ref[sl, :].astype(F32) * (GLA_DK ** -0.5)
        kc = k_ref[sl, :].astype(F32)
        vc = v_ref[sl, :]
        q_mid = qc * jnp.exp(jnp.minimum(xc - ref, GLA_EXP_CLAMP))
        k_mid = (kc * jnp.exp(jnp.minimum(ref - xc, GLA_EXP_CLAMP))).astype(BF16)
        q_dec = qc * jnp.exp(xc)
        k_dec = kc * jnp.exp(tot - xc)

        attn = _dot_nt(stack_heads(q_mid), k_mid)
        attn = jnp.where(pair_mask, attn, 0.0).astype(BF16)
        s_prev = s_scr[...]
        o_inter = _dot(stack_heads(q_dec), s_prev.astype(BF16))
        kv = []
        for h in range(GLA_HEADS):
            rs = slice(h * c_len, (h + 1) * c_len)
            vs = slice(h * GLA_DV, (h + 1) * GLA_DV)
            o_h = o_inter[rs] + _dot(attn[rs], vc[:, vs])
            o_ref[sl, vs] = o_h.astype(BF16)
            kv.append(_dot_tn(k_dec[:, h * GLA_DK:(h + 1) * GLA_DK].astype(BF16), vc[:, vs]))
        s_scr[...] = s_prev * jnp.exp(tot_cols[:, c:c + 1]) + jnp.concatenate(kv, axis=0)

    return step


def _gla_kernel(qf_ref, kf_ref, vf_ref, mf_ref, qb_ref, kb_ref, vb_ref, mb_ref,
                wg_ref, bg_ref, tri_ref, s0_ref, of_ref, ob_ref, sfin_ref, sf_scr, sb_scr, *, n_chunks):
    blk = pl.program_id(1)

    @pl.when(blk == 0)
    def _():
        sf_scr[...] = s0_ref[0, 0]
        sb_scr[...] = s0_ref[1, 0]

    fwd = _gla_direction(qf_ref, kf_ref, vf_ref, mf_ref, wg_ref[0], bg_ref[0], tri_ref[0],
                         sf_scr, of_ref, reverse=False, n_chunks=n_chunks)
    bwd = _gla_direction(qb_ref, kb_ref, vb_ref, mb_ref, wg_ref[1], bg_ref[1], tri_ref[1],
                         sb_scr, ob_ref, reverse=True, n_chunks=n_chunks)
    for c in range(n_chunks):
        fwd(c)
        bwd(n_chunks - 1 - c)

    @pl.when(blk == pl.num_programs(1) - 1)
    def _():
        sfin_ref[0, 0] = sf_scr[...]
        sfin_ref[1, 0] = sb_scr[...]


def _block_diag_tri(n_chunks):
    c = GLA_CHUNK
    eye = np.eye(n_chunks, dtype=np.float32)
    lower = np.kron(eye, np.tril(np.ones((c, c), np.float32)))
    upper = np.kron(eye, np.triu(np.ones((c, c), np.float32)))
    return jnp.asarray(np.stack([lower, upper]), dtype=BF16)


def _gla(p, wg, bg, s0, *, batch, cb):
    t_all = p.shape[0]
    nblk = t_all // batch // cb
    n_chunks = cb // GLA_CHUNK
    assert n_chunks <= 8

    fw = lambda b, i: b * nblk + i
    bw = lambda b, i: b * nblk + (nblk - 1 - i)
    full = lambda a: pl.BlockSpec(a.shape, lambda b, i: (0,) * a.ndim)
    tri = _block_diag_tri(n_chunks)

    def token_specs(tok):
        return [
            pl.BlockSpec((cb, GLA_QK_W), lambda b, i: (tok(b, i), P_Q // GLA_QK_W)),
            pl.BlockSpec((cb, GLA_QK_W), lambda b, i: (tok(b, i), P_K // GLA_QK_W)),
            pl.BlockSpec((cb, GLA_V_W), lambda b, i: (tok(b, i), P_V // GLA_V_W)),
            pl.BlockSpec((cb, LANES), lambda b, i: (tok(b, i), P_MISC // LANES)),
        ]

    state_spec = pl.BlockSpec((2, 1, GLA_QK_W, GLA_DV), lambda b, i: (0, b, 0, 0))
    return pl.pallas_call(
        functools.partial(_gla_kernel, n_chunks=n_chunks),
        grid=(batch, nblk),
        in_specs=token_specs(fw) + token_specs(bw) + [full(wg), full(bg), full(tri), state_spec],
        out_specs=[
            pl.BlockSpec((cb, GLA_V_W), lambda b, i: (fw(b, i), 0)),
            pl.BlockSpec((cb, GLA_V_W), lambda b, i: (bw(b, i), 0)),
            state_spec,
        ],
        out_shape=[
            jax.ShapeDtypeStruct((t_all, GLA_V_W), BF16),
            jax.ShapeDtypeStruct((t_all, GLA_V_W), BF16),
            jax.ShapeDtypeStruct((2, batch, GLA_QK_W, GLA_DV), F32),
        ],
        scratch_shapes=[pltpu.VMEM((GLA_QK_W, GLA_DV), F32), pltpu.VMEM((GLA_QK_W, GLA_DV), F32)],
        compiler_params=_cparams(("arbitrary", "arbitrary")),
        name="gla",
    )(p, p, p, p, p, p, p, p, wg, bg, tri, s0)


def _mlaprep_kernel(cq_ref, ckv_ref, misc_ref, cos_ref, sin_ref, qg_ref, kvg_ref,
                    wqn_ref, wqr_ref, wqs_ref, wknt_ref, wv_ref, perm_ref, eye_ref,
                    q_ref, kt_ref, v_ref):
    tm = cq_ref.shape[0]
    n_groups = 2 if tm % 256 == 0 else 1
    for g in range(n_groups):
        rows = slice(g * tm // n_groups, (g + 1) * tm // n_groups)
        cos = cos_ref[rows, :]
        sin = sin_ref[rows, :]
        cqn = _rms(cq_ref[rows, :].astype(F32), qg_ref[...]).astype(BF16)
        qn = _dot(cqn, wqn_ref[...])
        qr = _dot(cqn, wqr_ref[...])
        qs = _dot(cqn, wqs_ref[...])
        for h in range(MLA_HEADS):
            ls = slice(h * LANES, (h + 1) * LANES)
            q_ref[0, h, rows, 0:MLA_NOPE] = (qn[:, ls] * MLA_Q_SCALE).astype(BF16)
            rot = qr[:, ls] * cos + qs[:, ls] * sin
            q_ref[0, h, rows, MLA_NOPE:MLA_QK] = (rot[:, 0:MLA_ROPE] * MLA_Q_SCALE).astype(BF16)

        ckvn = _rms(ckv_ref[rows, :].astype(F32), kvg_ref[...]).astype(BF16)
        knt = _dot_nt(wknt_ref[...], ckvn)
        vv = _dot(ckvn, wv_ref[...])
        misc = misc_ref[rows, :]
        kr = misc.astype(F32) * cos + _dot(misc, perm_ref[...]) * sin
        krt = _dot_nt(eye_ref[...], kr.astype(BF16)).astype(BF16)
        for h in range(MLA_HEADS):
            kt_ref[0, h, 0, 0:MLA_NOPE, rows] = knt[h * MLA_NOPE:(h + 1) * MLA_NOPE].astype(BF16)
            kt_ref[0, h, 0, MLA_NOPE:MLA_QK, rows] = krt
            v_ref[0, h, rows, 0:MLA_V] = vv[:, h * MLA_V:(h + 1) * MLA_V].astype(BF16)
            v_ref[0, h, rows, MLA_V:MLA_V_EXT] = jnp.ones((vv.shape[0], MLA_V), BF16)


def _mlaprep(p, cos, sin, qg, kvg, wts, *, batch, tm):
    t_all = p.shape[0]
    t = t_all // batch
    nb = t // tm
    ntab = cos.shape[0] // tm
    wqn, wqr, wqs, wknt, wv, perm, eye = wts
    full = lambda a: pl.BlockSpec(a.shape, lambda b, i: (0,) * a.ndim)
    return pl.pallas_call(
        _mlaprep_kernel,
        grid=(batch, nb),
        in_specs=[
            pl.BlockSpec((tm, MLA_Q_RANK), lambda b, i: (b * nb + i, P_CQ // MLA_Q_RANK)),
            pl.BlockSpec((tm, MLA_KV_RANK), lambda b, i: (b * nb + i, P_CKV // MLA_KV_RANK)),
            pl.BlockSpec((tm, LANES), lambda b, i: (b * nb + i, P_MISC // LANES)),
            pl.BlockSpec((tm, LANES), lambda b, i: (i % ntab, 0)),
            pl.BlockSpec((tm, LANES), lambda b, i: (i % ntab, 0)),
            full(qg), full(kvg), full(wqn), full(wqr), full(wqs), full(wknt), full(wv),
            full(perm), full(eye),
        ],
        out_specs=[
            pl.BlockSpec((1, MLA_HEADS, tm, MLA_QK), lambda b, i: (b, 0, i, 0)),
            pl.BlockSpec((1, MLA_HEADS, 1, MLA_QK, tm), lambda b, i: (b, 0, i, 0, 0)),
            pl.BlockSpec((1, MLA_HEADS, tm, MLA_V_EXT), lambda b, i: (b, 0, i, 0)),
        ],
        out_shape=[
            jax.ShapeDtypeStruct((batch, MLA_HEADS, t, MLA_QK), BF16),
            jax.ShapeDtypeStruct((batch, MLA_HEADS, nb, MLA_QK, tm), BF16),
            jax.ShapeDtypeStruct((batch, MLA_HEADS, t, MLA_V_EXT), BF16),
        ],
        compiler_params=_cparams(("arbitrary", "arbitrary")),
        name="mlaprep",
    )(p, p, p, cos, sin, qg, kvg, wqn, wqr, wqs, wknt, wv, perm, eye)


def _attn_kernel(*refs, n_seg, n_sub):
    q_ref = refs[0]
    kt_refs = refs[1:1 + 2 * n_seg:2]
    v_refs = refs[2:2 + 2 * n_seg:2]
    o_ref = refs[1 + 2 * n_seg]
    m_scr, acc_scr = refs[2 + 2 * n_seg:]

    rows_per_sub = q_ref.shape[2] // n_sub
    m_scr[...] = jnp.full(m_scr.shape, -jnp.inf, F32)
    acc_scr[...] = jnp.zeros(acc_scr.shape, F32)

    for kt_ref, v_ref in zip(kt_refs, v_refs):
        n_blocks, tk = kt_ref.shape[2], kt_ref.shape[4]

        def step(j, carry, kt_ref=kt_ref, v_ref=v_ref, tk=tk):
            kt = kt_ref[0, 0, j]
            v_blk = v_ref[0, 0, pl.ds(pl.multiple_of(j * tk, tk), tk), :]
            for u in range(n_sub):
                rows = slice(u * rows_per_sub, (u + 1) * rows_per_sub)
                s = _dot(q_ref[0, 0, rows, :], kt)
                m_prev = m_scr[rows, :]
                m_next = jnp.maximum(m_prev, jnp.max(s, axis=1, keepdims=True))
                p = jnp.exp2((s - jnp.concatenate([m_next] * (tk // LANES), axis=1)).astype(BF16))
                alpha = jnp.exp2(m_prev - m_next)
                acc_scr[rows, :] = (jnp.concatenate([alpha] * (MLA_V_EXT // LANES), axis=1)
                                    * acc_scr[rows, :] + _dot(p, v_blk))
                m_scr[rows, :] = m_next
            return carry

        lax.fori_loop(0, n_blocks, step, 0)

    o_ref[0] = (acc_scr[:, 0:MLA_V] / acc_scr[:, MLA_V:MLA_V_EXT]).astype(BF16)


def _attn_pipe_kernel(q_ref, kt_ref, v_ref, ktt_ref, vt_ref, o_ref,
                      m_scr, acc_scr, s0_scr, s1_scr, st_scr, *, n_sub):
    n_blocks, tk = kt_ref.shape[2], kt_ref.shape[4]
    tq = m_scr.shape[0]
    n_q = q_ref.shape[2] // tq
    rows_per_sub = tq // n_sub
    subs = [slice(u * rows_per_sub, (u + 1) * rows_per_sub) for u in range(n_sub)]
    bufs = (s0_scr, s1_scr)

    def reset():
        m_scr[...] = jnp.full(m_scr.shape, -jnp.inf, F32)
        acc_scr[...] = jnp.zeros(acc_scr.shape, F32)

    def scores(qi, kt, s_ref):
        for rows in subs:
            q_rows = pl.ds(pl.multiple_of(qi * tq + rows.start, rows_per_sub), rows_per_sub)
            s_ref[rows, :] = _dot(q_ref[0, 0, q_rows, :], kt)

    def softmax_pv(s_ref, v_blk):
        width = s_ref.shape[1]
        for rows in subs:
            s = s_ref[rows, :]
            m_prev = m_scr[rows, :]
            m_next = jnp.maximum(m_prev, jnp.max(s, axis=1, keepdims=True))
            p = jnp.exp2((s - jnp.concatenate([m_next] * (width // LANES), axis=1)).astype(BF16))
            alpha = jnp.exp2(m_prev - m_next)
            acc_scr[rows, :] = (jnp.concatenate([alpha] * (MLA_V_EXT // LANES), axis=1)
                                * acc_scr[rows, :] + _dot(p, v_blk))
            m_scr[rows, :] = m_next

    reset()
    scores(0, kt_ref[0, 0, 0], bufs[0])

    def query_block(qi, carry):
        for j in range(n_blocks):
            if j + 1 < n_blocks:
                scores(qi, kt_ref[0, 0, j + 1], bufs[(j + 1) % 2])
            else:
                scores(qi, ktt_ref[0, 0, 0], st_scr)
            softmax_pv(bufs[j % 2], v_ref[0, 0, j * tk:(j + 1) * tk, :])
        scores(jnp.minimum(qi + 1, n_q - 1), kt_ref[0, 0, 0], bufs[0])
        softmax_pv(st_scr, vt_ref[0, 0])
        o_rows = pl.ds(pl.multiple_of(qi * tq, tq), tq)
        o_ref[0, o_rows, :] = (acc_scr[:, 0:MLA_V] / acc_scr[:, MLA_V:MLA_V_EXT]).astype(BF16)
        reset()
        return carry

    lax.fori_loop(0, n_q, query_block, 0)


def _attention_pipelined(q, kt, v, kt_tail, v_tail, *, tq, n_sub):
    b, h, t, dqk = q.shape
    tk, tt = kt.shape[4], kt_tail.shape[4]
    assert kt_tail.shape[2] == 1
    return pl.pallas_call(
        functools.partial(_attn_pipe_kernel, n_sub=n_sub),
        grid=(b, h),
        in_specs=[
            pl.BlockSpec((1, 1, t, dqk), lambda bi, hi: (bi, hi, 0, 0)),
            pl.BlockSpec((1, 1) + kt.shape[2:], lambda bi, hi: (bi, hi, 0, 0, 0)),
            pl.BlockSpec((1, 1) + v.shape[2:], lambda bi, hi: (bi, hi, 0, 0)),
            pl.BlockSpec((1, 1) + kt_tail.shape[2:], lambda bi, hi: (bi, hi, 0, 0, 0)),
            pl.BlockSpec((1, 1) + v_tail.shape[2:], lambda bi, hi: (bi, hi, 0, 0)),
        ],
        out_specs=pl.BlockSpec((1, t, MLA_V), lambda bi, hi: (bi, 0, hi)),
        out_shape=jax.ShapeDtypeStruct((b, t, h * MLA_V), BF16),
        scratch_shapes=[pltpu.VMEM((tq, LANES), F32), pltpu.VMEM((tq, MLA_V_EXT), F32),
                        pltpu.VMEM((tq, tk), F32), pltpu.VMEM((tq, tk), F32),
                        pltpu.VMEM((tq, tt), F32)],
        compiler_params=_cparams(("arbitrary", "arbitrary")),
        name="mla_attention_pipe",
    )(q, kt, v, kt_tail, v_tail)


def _attention(q, segs, *, tq, n_sub):
    b, h, t, dqk = q.shape
    in_specs = [pl.BlockSpec((1, 1, tq, dqk), lambda bi, hi, qi: (bi, hi, qi, 0))]
    args = [q]
    for kt, v in segs:
        in_specs.append(pl.BlockSpec((1, 1) + kt.shape[2:], lambda bi, hi, qi: (bi, hi, 0, 0, 0)))
        in_specs.append(pl.BlockSpec((1, 1) + v.shape[2:], lambda bi, hi, qi: (bi, hi, 0, 0)))
        args += [kt, v]
    return pl.pallas_call(
        functools.partial(_attn_kernel, n_seg=len(segs), n_sub=n_sub),
        grid=(b, h, t // tq),
        in_specs=in_specs,
        out_specs=pl.BlockSpec((1, tq, MLA_V), lambda bi, hi, qi: (bi, qi, hi)),
        out_shape=jax.ShapeDtypeStruct((b, t, h * MLA_V), BF16),
        scratch_shapes=[pltpu.VMEM((tq, LANES), F32), pltpu.VMEM((tq, MLA_V_EXT), F32)],
        compiler_params=_cparams(("arbitrary", "arbitrary", "arbitrary")),
        name="mla_attention",
    )(*args)


def _mix_residual_norm(x_ref, of_ref, ob_ref, r_ref, mla_ref, m, gg_ref, wo_ref, ln2_ref,
                       rows=slice(None)):
    o = of_ref[rows, :].astype(F32) + ob_ref[rows, :].astype(F32)
    gg = gg_ref[...]
    y = jnp.concatenate(
        [_rms(o[:, h * GLA_DV:(h + 1) * GLA_DV], gg) for h in range(GLA_HEADS)], axis=1)
    mix = (y * _silu(r_ref[rows, :].astype(F32))).astype(BF16)
    yo = _dot(mix, wo_ref[0:GLA_V_W, :]) + _dot(mla_ref[rows, :], wo_ref[GLA_V_W:, :])
    x1 = x_ref[rows, :] + m[2:3] * yo
    h2 = _rms(x1, ln2_ref[...]) * (1.0 + m[4:5]) + m[3:4]
    return x1, h2


def _outproj_ffn_kernel(*refs, groups, final_norm):
    (x_ref, of_ref, ob_ref, r_ref, mla_ref, mod_ref, gg_ref, wo_ref, ln2_ref,
     wg_ref, wu_ref, wd_ref) = refs[:12]
    fin_ref = refs[12] if final_norm else None
    o_ref = refs[-1]
    m = mod_ref[0]
    x1, h2 = _mix_residual_norm(x_ref, of_ref, ob_ref, r_ref, mla_ref, m, gg_ref, wo_ref, ln2_ref)
    h = h2.astype(BF16)
    y = None
    for lo, hi in groups:
        a = _dot(h, wg_ref[:, lo:hi])
        u = _dot(h, wu_ref[:, lo:hi])
        part = _dot((_silu(a) * u).astype(BF16), wd_ref[lo:hi, :])
        y = part if y is None else y + part
    x2 = x1 + m[5:6] * y
    if final_norm:
        x2 = _rms(x2, fin_ref[...])
    o_ref[...] = x2


def _outproj_ffn(x, o_f, o_b, p, mla, mods, row_fn, gg, wo, ln2, wg, wu, wd, fin_g, tm):
    t, d = x.shape
    ff = wg.shape[1]
    final_norm = fin_g is not None
    step = min(ff, DENSE_FF_GROUP)
    groups = tuple((lo, min(lo + step, ff)) for lo in range(0, ff, step))
    resident = lambda a: pl.BlockSpec(a.shape, lambda i: (0,) * a.ndim, pipeline_mode=pl.Buffered(1))
    in_specs = [
        pl.BlockSpec((tm, d), lambda i: (i, 0)),
        pl.BlockSpec((tm, GLA_V_W), lambda i: (i, 0)),
        pl.BlockSpec((tm, GLA_V_W), lambda i: (i, 0)),
        pl.BlockSpec((tm, GLA_V_W), lambda i: (i, P_R // GLA_V_W)),
        pl.BlockSpec((tm, MLA_V_W), lambda i: (i, 0)),
        pl.BlockSpec((1, 6, d), lambda i: (row_fn(i), 0, 0)),
        resident(gg), resident(wo), resident(ln2), resident(wg), resident(wu), resident(wd),
    ]
    args = [x, o_f, o_b, p, mla, mods, gg, wo, ln2, wg, wu, wd]
    if final_norm:
        in_specs.append(resident(fin_g))
        args.append(fin_g)
    return pl.pallas_call(
        functools.partial(_outproj_ffn_kernel, groups=groups, final_norm=final_norm),
        grid=(t // tm,),
        in_specs=in_specs,
        out_specs=pl.BlockSpec((tm, d), lambda i: (i, 0)),
        out_shape=jax.ShapeDtypeStruct((t, d), F32),
        compiler_params=_cparams(("arbitrary",)),
        name="outproj_ffn",
    )(*args)


def _outproj_kernel(*refs, with_router):
    (x_ref, of_ref, ob_ref, r_ref, mla_ref, mod_ref, gg_ref, wo_ref, ln2_ref) = refs[:9]
    if with_router:
        rwh_ref, rwl_ref, x1_ref, h2_ref, comb_ref = refs[9:]
    else:
        x1_ref, h2_ref = refs[9:]
    m = mod_ref[0]
    tm = x_ref.shape[0]
    n_groups = 2 if tm % 32 == 0 else 1
    for g in range(n_groups):
        rows = slice(g * tm // n_groups, (g + 1) * tm // n_groups)
        x1, h2 = _mix_residual_norm(x_ref, of_ref, ob_ref, r_ref, mla_ref, m, gg_ref, wo_ref,
                                    ln2_ref, rows)
        x1_ref[rows, :] = x1
        if not with_router:
            h2_ref[rows, :] = h2.astype(BF16)
            continue
        h2_ref[rows, :] = _pack_bf16_pairs(h2)
        h_hi = h2.astype(BF16)
        h_lo = (h2 - h_hi.astype(F32)).astype(BF16)
        logits = _dot(h_hi, rwh_ref[...]) + _dot(h_lo, rwh_ref[...]) + _dot(h_hi, rwl_ref[...])
        lane = lax.broadcasted_iota(jnp.int32, logits.shape, 1).astype(F32)
        neg = jnp.float32(-jnp.inf)
        logits = jnp.where(lane < N_EXPERTS, logits, neg)
        m1 = jnp.max(logits, axis=1, keepdims=True)
        i1 = jnp.min(jnp.where(logits == m1, lane, float(LANES)), axis=1, keepdims=True)
        rest = jnp.where(lane == i1, neg, logits)
        m2 = jnp.max(rest, axis=1, keepdims=True)
        i2 = jnp.min(jnp.where(rest == m2, lane, float(LANES)), axis=1, keepdims=True)
        e2 = jnp.exp(m2 - m1)
        w1 = 1.0 / (1.0 + e2)
        comb_ref[rows, :] = (jnp.where(lane == ROUTE_E1, i1, 0.0) + jnp.where(lane == ROUTE_E2, i2, 0.0)
                             + jnp.where(lane == ROUTE_W1, w1, 0.0)
                             + jnp.where(lane == ROUTE_W2, e2 * w1, 0.0))


def _outproj(x, o_f, o_b, p, mla, mods, row_fn, gg, wo, ln2, router, tm):
    t, d = x.shape
    with_router = router is not None
    full = lambda a: pl.BlockSpec(a.shape, lambda i: (0,) * a.ndim)
    in_specs = [
        pl.BlockSpec((tm, d), lambda i: (i, 0)),
        pl.BlockSpec((tm, GLA_V_W), lambda i: (i, 0)),
        pl.BlockSpec((tm, GLA_V_W), lambda i: (i, 0)),
        pl.BlockSpec((tm, GLA_V_W), lambda i: (i, P_R // GLA_V_W)),
        pl.BlockSpec((tm, MLA_V_W), lambda i: (i, 0)),
        pl.BlockSpec((1, 6, d), lambda i: (row_fn(i), 0, 0)),
        full(gg), full(wo), full(ln2),
    ]
    args = [x, o_f, o_b, p, mla, mods, gg, wo, ln2]
    h2_shape = jax.ShapeDtypeStruct((t, d // 2), F32) if with_router else jax.ShapeDtypeStruct((t, d), BF16)
    out_specs = [pl.BlockSpec((tm, d), lambda i: (i, 0)),
                 pl.BlockSpec((tm, h2_shape.shape[1]), lambda i: (i, 0))]
    out_shape = [jax.ShapeDtypeStruct((t, d), F32), h2_shape]
    if with_router:
        in_specs += [full(router[0]), full(router[1])]
        args += list(router)
        out_specs.append(pl.BlockSpec((tm, LANES), lambda i: (i, 0)))
        out_shape.append(jax.ShapeDtypeStruct((t, LANES), F32))
    return pl.pallas_call(
        functools.partial(_outproj_kernel, with_router=with_router),
        grid=(t // tm,),
        in_specs=in_specs,
        out_specs=out_specs,
        out_shape=out_shape,
        compiler_params=_cparams(("arbitrary",)),
        name="outproj",
    )(*args)


def _ffn_kernel(*refs, with_comb, final_norm):
    h_ref, x1_ref, mod_ref = refs[:3]
    k = 3
    comb_ref = fin_ref = None
    if with_comb:
        comb_ref = refs[k]
        k += 1
    wg_ref, wu_ref, wd_ref = refs[k:k + 3]
    k += 3
    if final_norm:
        fin_ref = refs[k]
        k += 1
    o_ref, acc = refs[k:]
    e = pl.program_id(1)
    f = pl.program_id(2)

    @pl.when((e == 0) & (f == 0))
    def _():
        acc[...] = jnp.zeros(acc.shape, F32)

    h = h_ref[...]
    a = _dot(h, wg_ref[0])
    u = _dot(h, wu_ref[0])
    act = _silu(a) * u
    if with_comb:
        comb = comb_ref[...]
        lane = lax.broadcasted_iota(jnp.int32, comb.shape, 1)
        act = act * jnp.sum(jnp.where(lane == e, comb, 0.0), axis=1, keepdims=True)
    acc[...] += _dot(act.astype(BF16), wd_ref[0])

    @pl.when((e == pl.num_programs(1) - 1) & (f == pl.num_programs(2) - 1))
    def _():
        x2 = x1_ref[...] + mod_ref[0][5:6] * acc[...]
        if final_norm:
            x2 = _rms(x2, fin_ref[...])
        o_ref[...] = x2


def _ffn(h2, x1, mods, row_fn, comb, wg, wu, wd, fin_g, tm, tf):
    t, d = x1.shape
    n_e, _, ff = wg.shape
    with_comb = comb is not None
    final_norm = fin_g is not None
    in_specs = [
        pl.BlockSpec((tm, d), lambda i, e, f: (i, 0)),
        pl.BlockSpec((tm, d), lambda i, e, f: (i, 0)),
        pl.BlockSpec((1, 6, d), lambda i, e, f: (row_fn(i), 0, 0)),
    ]
    args = [h2, x1, mods]
    if with_comb:
        in_specs.append(pl.BlockSpec((tm, LANES), lambda i, e, f: (i, 0)))
        args.append(comb)
    in_specs += [
        pl.BlockSpec((1, d, tf), lambda i, e, f: (e, 0, f)),
        pl.BlockSpec((1, d, tf), lambda i, e, f: (e, 0, f)),
        pl.BlockSpec((1, tf, d), lambda i, e, f: (e, f, 0)),
    ]
    args += [wg, wu, wd]
    if final_norm:
        in_specs.append(pl.BlockSpec((1, d), lambda i, e, f: (0, 0)))
        args.append(fin_g)
    return pl.pallas_call(
        functools.partial(_ffn_kernel, with_comb=with_comb, final_norm=final_norm),
        grid=(t // tm, n_e, ff // tf),
        in_specs=in_specs,
        out_specs=pl.BlockSpec((tm, d), lambda i, e, f: (i, 0)),
        out_shape=jax.ShapeDtypeStruct((t, d), F32),
        scratch_shapes=[pltpu.VMEM((tm, d), F32)],
        compiler_params=_cparams(("arbitrary", "arbitrary", "arbitrary")),
        name="ffn",
    )(*args)


def _sc_row_gather(table, idx):
    _, w = table.shape
    b = idx.shape[0]
    n_workers = SC_CORES * SC_SUBCORES
    assert b % (n_workers * SC_GATHER_ROWS) == 0, (b, n_workers, SC_GATHER_ROWS)
    b_per_w = b // n_workers
    n_chunks = b_per_w // SC_GATHER_ROWS
    mesh = plsc.VectorSubcoreMesh(core_axis_name="c", subcore_axis_name="s",
                                  num_cores=SC_CORES, num_subcores=SC_SUBCORES)

    def body(table_hbm, idx_hbm, out_hbm, idx_a, idx_b, rows_a, rows_b, sem_a, sem_b):
        wid = lax.axis_index("s") * SC_CORES + lax.axis_index("c")
        base = wid * b_per_w
        idx_bufs, row_bufs, sems = (idx_a, idx_b), (rows_a, rows_b), (sem_a, sem_b)

        def start(ci):
            slot = ci % 2
            pltpu.sync_copy(idx_hbm.at[pl.ds(base + ci * SC_GATHER_ROWS, SC_GATHER_ROWS)], idx_bufs[slot])
            return pltpu.async_copy(table_hbm.at[idx_bufs[slot]], row_bufs[slot], sems[slot])

        pending = start(0)
        for ci in range(n_chunks):
            following = start(ci + 1) if ci + 1 < n_chunks else None
            pending.wait()
            pltpu.sync_copy(row_bufs[ci % 2], out_hbm.at[pl.ds(base + ci * SC_GATHER_ROWS, SC_GATHER_ROWS)])
            pending = following

    return pl.kernel(
        body,
        out_type=jax.ShapeDtypeStruct((b, w), F32),
        mesh=mesh,
        scratch_types=[pltpu.VMEM((SC_GATHER_ROWS,), jnp.int32)] * 2
        + [pltpu.VMEM((SC_GATHER_ROWS, w), F32)] * 2 + [pltpu.SemaphoreType.DMA] * 2,
        name="sc_row_gather",
    )(table, idx)


def _sc_row_scatter2(table, pos, n_out):
    t, w = table.shape
    n_workers = SC_CORES * SC_SUBCORES
    assert t % (n_workers * SC_GATHER_ROWS) == 0, (t, n_workers, SC_GATHER_ROWS)
    t_per_w = t // n_workers
    n_chunks = t_per_w // SC_GATHER_ROWS
    mesh = plsc.VectorSubcoreMesh(core_axis_name="c", subcore_axis_name="s",
                                  num_cores=SC_CORES, num_subcores=SC_SUBCORES)

    def body(table_hbm, pos_hbm, out_hbm, i0a, i1a, i0b, i1b, rows_a, rows_b, sem_a, sem_b):
        wid = lax.axis_index("s") * SC_CORES + lax.axis_index("c")
        base = wid * t_per_w
        idx0, idx1, row_bufs, sems = (i0a, i0b), (i1a, i1b), (rows_a, rows_b), (sem_a, sem_b)

        def drain(pair):
            if pair is not None:
                pair[0].wait()
                pair[1].wait()

        pending = [None, None]
        for ci in range(n_chunks):
            slot = ci % 2
            drain(pending[slot])
            off = base + ci * SC_GATHER_ROWS
            pltpu.sync_copy(pos_hbm.at[pl.ds(off, SC_GATHER_ROWS)], idx0[slot])
            pltpu.sync_copy(pos_hbm.at[pl.ds(t + off, SC_GATHER_ROWS)], idx1[slot])
            pltpu.sync_copy(table_hbm.at[pl.ds(off, SC_GATHER_ROWS)], row_bufs[slot])
            pending[slot] = (pltpu.async_copy(row_bufs[slot], out_hbm.at[idx0[slot]], sems[slot]),
                             pltpu.async_copy(row_bufs[slot], out_hbm.at[idx1[slot]], sems[slot]))
        drain(pending[0])
        drain(pending[1])

    return pl.kernel(
        body,
        out_type=jax.ShapeDtypeStruct((n_out, w), F32),
        mesh=mesh,
        scratch_types=[pltpu.VMEM((SC_GATHER_ROWS,), jnp.int32)] * 4
        + [pltpu.VMEM((SC_GATHER_ROWS, w), F32)] * 2 + [pltpu.SemaphoreType.DMA] * 2,
        name="sc_row_scatter",
    )(table, pos)


def _moe_plan_kernel(route_ref, tri_ref, utri_ref, pos_ref, cnt_ref, run_scr, off_scr):
    phase = pl.program_id(0)
    blk = pl.program_id(1)
    route = route_ref[...]
    lane = lax.broadcasted_iota(jnp.int32, route.shape, 1).astype(F32)
    oh1 = jnp.where(lane == route[:, ROUTE_E1:ROUTE_E1 + 1], 1.0, 0.0)
    oh2 = jnp.where(lane == route[:, ROUTE_E2:ROUTE_E2 + 1], 1.0, 0.0)
    oh = oh1 + oh2

    @pl.when(blk == 0)
    def _():
        run_scr[...] = jnp.zeros(run_scr.shape, F32)

    @pl.when(phase == 0)
    def _():
        run_scr[...] += jnp.sum(oh, axis=0, keepdims=True)

        @pl.when(blk == pl.num_programs(1) - 1)
        def _():
            counts = run_scr[...]
            cnt_ref[...] = counts
            tiles_per = jnp.floor((counts + (MOE_TILE - 1.0)) * (1.0 / MOE_TILE))
            tile_end = _dot(jnp.broadcast_to(tiles_per, (8, LANES)).astype(BF16), utri_ref[...])[0:1]
            off_scr[...] = (tile_end - tiles_per) * float(MOE_TILE)

    @pl.when(phase == 1)
    def _():
        incl = _dot(tri_ref[...], oh.astype(BF16))
        before = incl - oh + run_scr[...] + off_scr[...]
        p1 = jnp.sum(before * oh1, axis=1, keepdims=True)
        p2 = jnp.sum(before * oh2, axis=1, keepdims=True)
        pos = jnp.where(lane == 0.0, p1, 0.0) + jnp.where(lane == 1.0, p2, 0.0)
        pos_ref[...] = pos.astype(jnp.int32)
        run_scr[...] += incl[incl.shape[0] - 1:, :]


def _moe_plan(route, n_tiles, tm):
    t = route.shape[0]
    tri = jnp.asarray(np.tril(np.ones((tm, tm), np.float32)), dtype=BF16)
    utri = jnp.asarray(np.triu(np.ones((LANES, LANES), np.float32)), dtype=BF16)
    pos, counts = pl.pallas_call(
        _moe_plan_kernel,
        grid=(2, t // tm),
        in_specs=[
            pl.BlockSpec((tm, LANES), lambda p, i: (i, 0)),
            pl.BlockSpec((tm, tm), lambda p, i: (0, 0)),
            pl.BlockSpec((LANES, LANES), lambda p, i: (0, 0)),
        ],
        out_specs=[
            pl.BlockSpec((tm, LANES), lambda p, i: (i * p, 0)),
            pl.BlockSpec((1, LANES), lambda p, i: (0, 0)),
        ],
        out_shape=[jax.ShapeDtypeStruct((t, LANES), jnp.int32),
                   jax.ShapeDtypeStruct((1, LANES), F32)],
        scratch_shapes=[pltpu.VMEM((1, LANES), F32), pltpu.VMEM((1, LANES), F32)],
        compiler_params=_cparams(("arbitrary", "arbitrary")),
        name="moe_plan",
    )(route, tri, utri)
    counts = counts[0, :N_EXPERTS].astype(jnp.int32)
    tile_end = jnp.cumsum((counts + MOE_TILE - 1) // MOE_TILE)
    n_used = tile_end[-1]
    tile_ids = jnp.minimum(jnp.arange(n_tiles, dtype=jnp.int32), n_used - 1)
    tile_expert = jnp.sum((tile_end[None, :] <= tile_ids[:, None]).astype(jnp.int32), axis=1)
    return pos[:, :2].T, tile_expert, n_used.reshape(1)


def _moe_ffn_kernel(te_ref, nused_ref, xs_ref, wg_ref, wu_ref, wd_ref, o_ref, acc, h_scr, *, splits):
    i = pl.program_id(0)
    f = pl.program_id(1)
    last_f = pl.num_programs(1) - 1
    used = i < nused_ref[0]

    @pl.when(used)
    def _():
        @pl.when(f == 0)
        def _():
            acc[...] = jnp.zeros(acc.shape, F32)
            h_scr[...] = _unpack_bf16_pairs(xs_ref[...]).astype(BF16)

        h = h_scr[...]
        for lo, hi in splits:
            a = _dot(h, wg_ref[0, :, lo:hi].astype(BF16))
            u = _dot(h, wu_ref[0, :, lo:hi].astype(BF16))
            acc[...] += _dot((_silu(a) * u).astype(BF16), wd_ref[0, lo:hi, :].astype(BF16))

        @pl.when(f == last_f)
        def _():
            o_ref[...] = _pack_bf16_pairs(acc[...])

    @pl.when(jnp.logical_not(used) & (f == last_f))
    def _():
        o_ref[...] = jnp.zeros(o_ref.shape, F32)


def _moe_ffn(xs, tile_expert, n_used, wg, wu, wd):
    rows, half = xs.shape
    d = 2 * half
    ff = wg.shape[2]
    if ff % MOE_FF_TILE == 0:
        tf, splits = MOE_FF_TILE, MOE_FF_SPLITS
    else:
        tf, splits = ff, ((0, ff),)
    n_tiles = rows // MOE_TILE
    grid_spec = pltpu.PrefetchScalarGridSpec(
        num_scalar_prefetch=2,
        grid=(n_tiles, ff // tf),
        in_specs=[
            pl.BlockSpec((MOE_TILE, half), lambda i, f, te, nu: (i, 0)),
            pl.BlockSpec((1, d, tf), lambda i, f, te, nu: (te[i], 0, f)),
            pl.BlockSpec((1, d, tf), lambda i, f, te, nu: (te[i], 0, f)),
            pl.BlockSpec((1, tf, d), lambda i, f, te, nu: (te[i], f, 0)),
        ],
        out_specs=pl.BlockSpec((MOE_TILE, half), lambda i, f, te, nu: (i, 0)),
        scratch_shapes=[pltpu.VMEM((MOE_TILE, d), F32), pltpu.VMEM((MOE_TILE, d), BF16)],
    )
    return pl.pallas_call(
        functools.partial(_moe_ffn_kernel, splits=splits),
        grid_spec=grid_spec,
        out_shape=jax.ShapeDtypeStruct((rows, half), F32),
        compiler_params=_cparams(("arbitrary", "arbitrary")),
        name="moe_ffn",
    )(tile_expert, n_used, xs, wg, wu, wd)


def _combine_kernel(*refs, final_norm):
    x1_ref, y0_ref, y1_ref, route_ref, mod_ref = refs[:5]
    fin_ref = refs[5] if final_norm else None
    o_ref = refs[-1]
    route = route_ref[...]
    w1 = route[:, ROUTE_W1:ROUTE_W1 + 1]
    w2 = route[:, ROUTE_W2:ROUTE_W2 + 1]
    y = w1 * _unpack_bf16_pairs(y0_ref[...]) + w2 * _unpack_bf16_pairs(y1_ref[...])
    x2 = x1_ref[...] + mod_ref[0][5:6] * y
    if final_norm:
        x2 = _rms(x2, fin_ref[...])
    o_ref[...] = x2


def _combine(x1, yg, route, mods, row_fn, fin_g, tm):
    t, d = x1.shape
    nb = t // tm
    final_norm = fin_g is not None
    in_specs = [
        pl.BlockSpec((tm, d), lambda i: (i, 0)),
        pl.BlockSpec((tm, d // 2), lambda i: (i, 0)),
        pl.BlockSpec((tm, d // 2), lambda i: (i + nb, 0)),
        pl.BlockSpec((tm, LANES), lambda i: (i, 0)),
        pl.BlockSpec((1, 6, d), lambda i: (row_fn(i), 0, 0)),
    ]
    args = [x1, yg, yg, route, mods]
    if final_norm:
        in_specs.append(pl.BlockSpec((1, d), lambda i: (0, 0)))
        args.append(fin_g)
    return pl.pallas_call(
        functools.partial(_combine_kernel, final_norm=final_norm),
        grid=(nb,),
        in_specs=in_specs,
        out_specs=pl.BlockSpec((tm, d), lambda i: (i, 0)),
        out_shape=jax.ShapeDtypeStruct((t, d), F32),
        compiler_params=_cparams(("arbitrary",)),
        name="moe_combine",
    )(*args)


def _moe(h2, x1, route, mods, row_fn, wg, wu, wd, fin_g, tm):
    t = h2.shape[0]
    n_tiles = -(-2 * t // MOE_TILE) + N_EXPERTS
    pos, tile_expert, n_used = _moe_plan(route, n_tiles, _pick_tile(t, ATTN_TILE))
    pos = pos.reshape(-1)
    xs = _sc_row_scatter2(h2, pos, n_tiles * MOE_TILE)
    ys = _moe_ffn(xs, tile_expert, n_used, wg, wu, wd)
    yg = _sc_row_gather(ys, pos)
    return _combine(x1, yg, route, mods, row_fn, fin_g, tm)


def _rope_partner():
    j = np.arange(MLA_ROPE)
    return np.where((j % 32) < 16, j + 16, j - 16)


def _prep_in_weight(w):
    d = w.shape[0]
    widths = (GLA_QK_W, GLA_QK_W, GLA_V_W, GLA_GATE_RANK, GLA_GATE_RANK, GLA_V_W,
              MLA_Q_RANK, MLA_KV_RANK, MLA_ROPE)
    offs = np.concatenate([[0], np.cumsum(widths)])
    part = lambda i: w[:, offs[i]:offs[i + 1]]
    assert w.shape[1] == offs[-1] and P_MISC + MISC_GB + GLA_GATE_RANK <= P_WIDTH
    cols = [part(0), part(1), part(2), part(5), part(6), part(7), part(8), part(3), part(4),
            jnp.zeros((d, P_WIDTH - int(offs[-1])), w.dtype)]
    return jnp.concatenate(cols, axis=1).astype(BF16)


def _prep_gate_weight(w_g2, b_g2):
    ws = []
    for z, off in ((0, MISC_GF), (1, MISC_GB)):
        ws.append(jnp.zeros((LANES, GLA_QK_W), F32).at[off:off + GLA_GATE_RANK].set(w_g2[z]))
    return jnp.stack(ws).astype(BF16), b_g2.reshape(2, 1, GLA_QK_W)


def _prep_mla_weights(w_uq, w_ukv):
    partner = _rope_partner()
    wq = w_uq.reshape(MLA_Q_RANK, MLA_HEADS, MLA_QK)
    wqn = wq[:, :, :MLA_NOPE].reshape(MLA_Q_RANK, MLA_HEADS * MLA_NOPE)
    rope = wq[:, :, MLA_NOPE:]
    pad = jnp.zeros((MLA_Q_RANK, MLA_HEADS, LANES - MLA_ROPE), w_uq.dtype)
    wqr = jnp.concatenate([rope, pad], axis=2).reshape(MLA_Q_RANK, MLA_HEADS * LANES)
    wqs = jnp.concatenate([rope[:, :, partner], pad], axis=2).reshape(MLA_Q_RANK, MLA_HEADS * LANES)
    wkv = w_ukv.reshape(MLA_KV_RANK, MLA_HEADS, MLA_NOPE + MLA_V)
    wknt = wkv[:, :, :MLA_NOPE].reshape(MLA_KV_RANK, MLA_HEADS * MLA_NOPE).T
    wv = wkv[:, :, MLA_NOPE:].reshape(MLA_KV_RANK, MLA_HEADS * MLA_V)
    perm = np.zeros((LANES, LANES), np.float32)
    perm[partner, np.arange(MLA_ROPE)] = 1.0
    eye = np.eye(MLA_ROPE, LANES, dtype=np.float32)
    return (wqn.astype(BF16), wqr.astype(BF16), wqs.astype(BF16), wknt.astype(BF16),
            wv.astype(BF16), jnp.asarray(perm, BF16), jnp.asarray(eye, BF16))


def _rope_tables(n_tok):
    rows = n_tok // GRID_W
    row = np.repeat(np.arange(rows, dtype=np.float32), GRID_W)
    col = np.tile(np.arange(GRID_W, dtype=np.float32), rows)
    nfreq = MLA_ROPE // 4
    inv = np.float32(ROPE_BASE) ** (-np.arange(nfreq, dtype=np.float32) / np.float32(nfreq))
    ar = (row[:, None] * inv).astype(np.float32)
    ac = (col[:, None] * inv).astype(np.float32)
    zero = np.zeros((n_tok, LANES - MLA_ROPE), np.float32)
    cos = np.concatenate([np.cos(ar), np.cos(ar), np.cos(ac), np.cos(ac), zero], axis=1)
    sin = np.concatenate([-np.sin(ar), np.sin(ar), -np.sin(ac), np.sin(ac), zero], axis=1)
    return jnp.asarray(cos, F32), jnp.asarray(sin, F32)


def _identity_tables(n_tok):
    cos = jnp.concatenate([jnp.ones((n_tok, MLA_ROPE), F32),
                           jnp.zeros((n_tok, LANES - MLA_ROPE), F32)], axis=1)
    return cos, jnp.zeros((n_tok, LANES), F32)


def _pick_tile(n, pref):
    t = min(n, pref)
    while n % t:
        t //= 2
    return t


def _pick_ff_tile(ff):
    best = LANES
    for m in range(1, ff // LANES + 1):
        if ff % (m * LANES) == 0 and m * LANES <= FFN_MAX_FF_TILE:
            best = m * LANES
    return best


@jax.jit
def _forward(x, c, ctx, c_ctx, w_mod, b_mod, ln1_g, ln2_g, w_in, w_gla_g2, b_gla_g2, gla_norm_g,
             mla_q_norm_g, w_uq, mla_kv_norm_g, w_ukv, w_out, ffn_w_gate, ffn_w_up, ffn_w_down,
             router_w, exp_w_gate, exp_w_up, exp_w_down, final_norm_g):
    batch, seq, d = x.shape
    n_ctx = ctx.shape[1]
    depth = w_mod.shape[0]
    assert d == D_MODEL and batch < 8, (d, batch)

    cvec = jnp.zeros((8, d), F32).at[:batch].set(c).at[batch].set(c_ctx)
    mods_all = _modulation(cvec, w_mod, b_mod).reshape(depth, 8, 6, d)

    xl = x.reshape(batch * seq, d)
    xc = ctx.reshape(batch * n_ctx, d)

    tm_l = _pick_tile(seq, TOKEN_TILE)
    tm_c = _pick_tile(n_ctx, CTX_TILE)
    tk_l = _pick_tile(seq, ATTN_TILE)
    cb_l = _pick_tile(seq, GLA_BLOCK)
    cb_c = _pick_tile(n_ctx, GLA_BLOCK)
    row_l = lambda tm: (lambda i: i // (seq // tm))
    row_c = lambda i: batch

    rope_l = _rope_tables(seq)
    rope_c = _identity_tables(tm_c)
    zero_state = jnp.zeros((2, batch, GLA_QK_W, GLA_DV), F32)

    for i in range(depth):
        need_ctx = i < depth - 1
        last = i == depth - 1
        mods = mods_all[i]
        ln1 = ln1_g[i].reshape(1, d)
        ln2 = ln2_g[i].reshape(1, d)
        w_in_r = _prep_in_weight(w_in[i])
        gates = _prep_gate_weight(w_gla_g2[i], b_gla_g2[i])
        mla_w = _prep_mla_weights(w_uq[i], w_ukv[i])
        qg = mla_q_norm_g[i].reshape(1, MLA_Q_RANK)
        kvg = mla_kv_norm_g[i].reshape(1, MLA_KV_RANK)
        gg = gla_norm_g[i].reshape(1, GLA_DV)
        wo = w_out[i].astype(BF16)

        p_l = _inproj(xl, mods, row_l(tk_l), ln1, w_in_r, tk_l)
        p_c = _inproj(xc, mods, row_c, ln1, w_in_r, tm_c)

        oc_f, oc_b, s_ctx = _gla(p_c, *gates, zero_state, batch=batch, cb=cb_c)
        ol_f, ol_b, _ = _gla(p_l, *gates, s_ctx, batch=batch, cb=cb_l)

        q_l, kt_l, v_l = _mlaprep(p_l, *rope_l, qg, kvg, mla_w, batch=batch, tm=tk_l)
        q_c, kt_c, v_c = _mlaprep(p_c, *rope_c, qg, kvg, mla_w, batch=batch, tm=tm_c)
        m_l = _attention_pipelined(q_l, kt_l, v_l, kt_c, v_c, tq=tk_l, n_sub=1)
        m_l = m_l.reshape(batch * seq, MLA_V_W)

        if i % 2 == 0:
            j = i // 2
            router = None
            wg = ffn_w_gate[j].astype(BF16)
            wu = ffn_w_up[j].astype(BF16)
            wd = ffn_w_down[j].astype(BF16)
        else:
            j = i // 2
            rw = jnp.zeros((d, LANES), F32).at[:, :N_EXPERTS].set(router_w[j])
            rw_hi = rw.astype(BF16)
            router = (rw_hi, (rw - rw_hi.astype(F32)).astype(BF16))
            wg, wu, wd = exp_w_gate[j], exp_w_up[j], exp_w_down[j]
        fin = final_norm_g.reshape(1, d) if last else None

        if router is None:
            xl = _outproj_ffn(xl, ol_f, ol_b, p_l, m_l, mods, row_l(tm_l), gg, wo, ln2,
                              wg, wu, wd, fin, tm_l)
        else:
            outs = _outproj(xl, ol_f, ol_b, p_l, m_l, mods, row_l(tm_l), gg, wo, ln2, router, tm_l)
            xl = _moe(outs[1], outs[0], outs[2], mods, row_l(tk_l), wg, wu, wd, fin, tk_l)

        if need_ctx:
            m_c = _attention(q_c, [(kt_c, v_c)], tq=tm_c, n_sub=1).reshape(batch * n_ctx, MLA_V_W)
            if router is None:
                xc = _outproj_ffn(xc, oc_f, oc_b, p_c, m_c, mods, row_c, gg, wo, ln2,
                                  wg, wu, wd, None, _pick_tile(batch * n_ctx, TOKEN_TILE))
            else:
                outs_c = _outproj(xc, oc_f, oc_b, p_c, m_c, mods, row_c, gg, wo, ln2, router, tm_c)
                tm_fc = _pick_tile(batch * n_ctx, TOKEN_TILE)
                r_c = outs_c[2]
                lane = jnp.arange(LANES, dtype=F32)[None, :]
                comb_c = (jnp.where(lane == r_c[:, ROUTE_E1:ROUTE_E1 + 1], r_c[:, ROUTE_W1:ROUTE_W1 + 1], 0.0)
                          + jnp.where(lane == r_c[:, ROUTE_E2:ROUTE_E2 + 1], r_c[:, ROUTE_W2:ROUTE_W2 + 1], 0.0))
                bits = lax.bitcast_convert_type(outs_c[1], jnp.uint32)
                h2_c = jnp.concatenate([lax.bitcast_convert_type(bits << 16, F32),
                                        lax.bitcast_convert_type(bits & jnp.uint32(0xFFFF0000), F32)], axis=1)
                xc = _ffn(h2_c.astype(BF16), outs_c[0], mods, row_c, comb_c,
                          wg.astype(BF16), wu.astype(BF16), wd.astype(BF16), None, tm_fc,
                          _pick_ff_tile(wg.shape[2]))

    return xl.reshape(batch, seq, d)


def kernel(x, c, ctx, c_ctx, w_mod, b_mod, ln1_g, ln2_g, w_in, w_gla_g2, b_gla_g2, gla_norm_g,
           mla_q_norm_g, w_uq, mla_kv_norm_g, w_ukv, w_out, ffn_w_gate, ffn_w_up, ffn_w_down,
           router_w, exp_w_gate, exp_w_up, exp_w_down, final_norm_g):
    return _forward(x, c, ctx, c_ctx, w_mod, b_mod, ln1_g, ln2_g, w_in, w_gla_g2, b_gla_g2,
                    gla_norm_g, mla_q_norm_g, w_uq, mla_kv_norm_g, w_ukv, w_out, ffn_w_gate,
                    ffn_w_up, ffn_w_down, router_w, exp_w_gate, exp_w_up, exp_w_down, final_norm_g)
```

```python
import functools

import numpy as np
import jax
import jax.numpy as jnp
from jax import lax
from jax.experimental import pallas as pl
from jax.experimental.pallas import tpu as pltpu
from jax.experimental.pallas import tpu_sc as plsc

F32 = jnp.float32
BF16 = jnp.bfloat16

D_MODEL = 1024
EPS = 1e-6
GRID_W = 64

GLA_HEADS = 4
GLA_DK = 64
GLA_DV = 128
GLA_GATE_RANK = 16
GLA_GATE_NORM = 16.0
GLA_CHUNK = 64
GLA_QK_W = GLA_HEADS * GLA_DK
GLA_V_W = GLA_HEADS * GLA_DV
GLA_EXP_CLAMP = 80.0
GLA_PRE_SAFE = 29.0

MLA_HEADS = 4
MLA_NOPE = 128
MLA_ROPE = 64
MLA_V = 128
MLA_QK = MLA_NOPE + MLA_ROPE
MLA_Q_RANK = 256
MLA_KV_RANK = 128
MLA_SCALE = MLA_QK ** -0.5
MLA_Q_SCALE = MLA_SCALE * 1.4426950408889634
MLA_V_W = MLA_HEADS * MLA_V
MLA_V_EXT = 2 * MLA_V
ROPE_BASE = 10000.0

N_EXPERTS = 8
LANES = 128
ROUTE_E1, ROUTE_E2, ROUTE_W1, ROUTE_W2 = 0, 1, 2, 3

SC_CORES = 2
SC_SUBCORES = 16
SC_GATHER_ROWS = 64
MOE_TILE = 1024
MOE_FF_TILE = 512
MOE_FF_SPLITS = ((0, 256), (256, 512))

P_Q, P_K, P_V, P_R, P_CQ, P_CKV, P_MISC = 0, 256, 512, 1024, 1536, 1792, 1920
P_WIDTH = 2048
MISC_KR, MISC_GF, MISC_GB = 0, 64, 80

VMEM_LIMIT = 56 * 1024 * 1024

TOKEN_TILE = 512
CTX_TILE = 256
ATTN_TILE = 1024
GLA_BLOCK = 256
DENSE_FF_GROUP = 1024
FFN_MAX_FF_TILE = 1408


def _cparams(sem):
    return pltpu.CompilerParams(dimension_semantics=sem, vmem_limit_bytes=VMEM_LIMIT)


def _rms(x, g):
    return x * lax.rsqrt(jnp.mean(x * x, axis=-1, keepdims=True) + EPS) * g


def _silu(x):
    return x / (1.0 + jnp.exp(-x))


def _dot(a, b):
    return jnp.dot(a, b, preferred_element_type=F32)


def _dot_nt(a, b):
    return lax.dot_general(a, b, (((1,), (1,)), ((), ())), preferred_element_type=F32)


def _dot_tn(a, b):
    return lax.dot_general(a, b, (((0,), (0,)), ((), ())), preferred_element_type=F32)


def _pack_bf16_pairs(x):
    w = x.shape[1] // 2
    words = pltpu.pack_elementwise([x[:, :w], x[:, w:]], packed_dtype=BF16)
    return lax.bitcast_convert_type(words, F32)


def _unpack_bf16_pairs(p):
    words = lax.bitcast_convert_type(p, jnp.int32)
    lo = pltpu.unpack_elementwise(words, index=0, packed_dtype=BF16, unpacked_dtype=F32)
    hi = pltpu.unpack_elementwise(words, index=1, packed_dtype=BF16, unpacked_dtype=F32)
    return jnp.concatenate([lo, hi], axis=1)


def _mod_kernel(c_ref, w_ref, b_ref, o_ref):
    s = _silu(c_ref[...]).astype(BF16)
    o_ref[0] = _dot(s, w_ref[0].astype(BF16)) + b_ref[0]


def _modulation(cvec, w_mod, b_mod):
    depth, d, n = w_mod.shape
    tn = 1536
    return pl.pallas_call(
        _mod_kernel,
        grid=(depth, n // tn),
        in_specs=[
            pl.BlockSpec((8, d), lambda l, j: (0, 0)),
            pl.BlockSpec((1, d, tn), lambda l, j: (l, 0, j)),
            pl.BlockSpec((1, 1, tn), lambda l, j: (l, 0, j)),
        ],
        out_specs=pl.BlockSpec((1, 8, tn), lambda l, j: (l, 0, j)),
        out_shape=jax.ShapeDtypeStruct((depth, 8, n), F32),
        compiler_params=_cparams(("arbitrary", "arbitrary")),
        name="modulation",
    )(cvec, w_mod, b_mod.reshape(depth, 1, n))


def _inproj_kernel(x_ref, mod_ref, g_ref, w_ref, o_ref):
    m = mod_ref[0]
    tm = x_ref.shape[0]
    n_groups = 2 if tm % 32 == 0 else 1
    for g in range(n_groups):
        rows = slice(g * tm // n_groups, (g + 1) * tm // n_groups)
        h = _rms(x_ref[rows, :], g_ref[...]) * (1.0 + m[1:2]) + m[0:1]
        o_ref[rows, :] = _dot(h.astype(BF16), w_ref[...]).astype(BF16)


def _inproj(x, mods, row_fn, ln_g, w, tm):
    t, d = x.shape
    return pl.pallas_call(
        _inproj_kernel,
        grid=(t // tm,),
        in_specs=[
            pl.BlockSpec((tm, d), lambda i: (i, 0)),
            pl.BlockSpec((1, 6, d), lambda i: (row_fn(i), 0, 0)),
            pl.BlockSpec((1, d), lambda i: (0, 0)),
            pl.BlockSpec((d, P_WIDTH), lambda i: (0, 0)),
        ],
        out_specs=pl.BlockSpec((tm, P_WIDTH), lambda i: (i, 0)),
        out_shape=jax.ShapeDtypeStruct((t, P_WIDTH), BF16),
        compiler_params=_cparams(("arbitrary",)),
        name="inproj",
    )(x, mods, ln_g, w)


def _gla_direction(q_ref, k_ref, v_ref, pre, tri, s_scr, o_ref, *, reverse, n_chunks, exact):
    c_len = GLA_CHUNK
    g = (jnp.minimum(pre, 0.0) - jnp.log(1.0 + jnp.exp(-jnp.abs(pre)))) * (1.0 / GLA_GATE_NORM)
    g_hi = g.astype(BF16)
    g_lo = (g - g_hi.astype(F32)).astype(BF16)
    cum = _dot(tri, g_hi) + _dot(tri, g_lo)
    tot_rows = jnp.concatenate(
        [cum[c * c_len:c * c_len + 1] if reverse else cum[(c + 1) * c_len - 1:(c + 1) * c_len]
         for c in range(n_chunks)] + [jnp.zeros((8 - n_chunks, GLA_QK_W), F32)], axis=0)
    t_hi = tot_rows.astype(BF16)
    t_lo = (tot_rows - t_hi.astype(F32)).astype(BF16)
    eye = (lax.broadcasted_iota(jnp.int32, (GLA_QK_W, GLA_QK_W), 0)
           == lax.broadcasted_iota(jnp.int32, (GLA_QK_W, GLA_QK_W), 1))
    eye = jnp.where(eye, 1.0, 0.0).astype(BF16)
    tot_cols = _dot_nt(eye, t_hi) + _dot_nt(eye, t_lo)

    lane = lax.broadcasted_iota(jnp.int32, (c_len, GLA_QK_W), 1)
    head_masks = [(lane >= h * GLA_DK) & (lane < (h + 1) * GLA_DK) for h in range(GLA_HEADS)]
    row = lax.broadcasted_iota(jnp.int32, (GLA_HEADS * c_len, c_len), 0) % c_len
    col = lax.broadcasted_iota(jnp.int32, (GLA_HEADS * c_len, c_len), 1)
    pair_mask = (col >= row) if reverse else (col <= row)

    def stack_heads(a):
        return jnp.concatenate([jnp.where(mk, a, 0.0) for mk in head_masks], axis=0).astype(BF16)

    def exact_levels():
        n = n_chunks * c_len
        r = lax.broadcasted_iota(jnp.int32, (n, n), 0)
        u = lax.broadcasted_iota(jnp.int32, (n, n), 1)
        same_chunk = (r // c_len) == (u // c_len)
        t = r % c_len
        uu = u % c_len
        levels = []
        b = c_len
        while b >= 2:
            half = b // 2
            mid = (t // b) * b + half
            same = same_chunk & ((t // b) == (uu // b))
            if reverse:
                sel_q = same & (t < mid) & (uu >= t) & (uu < mid)
                sel_k = same & (t >= mid) & (uu >= mid) & (uu < t)
            else:
                sel_q = same & (t >= mid) & (uu >= mid) & (uu <= t)
                sel_k = same & (t < mid) & (uu > t) & (uu < mid)
            mq = jnp.where(sel_q, 1.0, 0.0).astype(BF16)
            mk_ = jnp.where(sel_k, 1.0, 0.0).astype(BF16)
            e_q = _dot(mq, g_hi) + _dot(mq, g_lo)
            e_k = _dot(mk_, g_hi) + _dot(mk_, g_lo)
            tq_ = row
            if reverse:
                own = ((tq_ // b) == (col // b)) & ((tq_ % b) < half) & ((col % b) >= half)
            else:
                own = ((tq_ // b) == (col // b)) & ((tq_ % b) >= half) & ((col % b) < half)
            levels.append((e_q, e_k, own))
            b = half
        return levels

    levels = exact_levels() if exact else None

    def intra_scores(sl, xc, qc, kc):
        if not exact:
            ref = xc[c_len // 2:c_len // 2 + 1]
            q_mid = qc * jnp.exp(jnp.minimum(xc - ref, GLA_EXP_CLAMP))
            k_mid = (kc * jnp.exp(jnp.minimum(ref - xc, GLA_EXP_CLAMP))).astype(BF16)
            return jnp.where(pair_mask, _dot_nt(stack_heads(q_mid), k_mid), 0.0)
        attn = jnp.where(row == col, _dot_nt(stack_heads(qc), kc.astype(BF16)), 0.0)
        for e_q, e_k, own in levels:
            a = _dot_nt(stack_heads(qc * jnp.exp(e_q[sl])), (kc * jnp.exp(e_k[sl])).astype(BF16))
            attn = attn + jnp.where(own, a, 0.0)
        return attn

    def step(c):
        sl = slice(c * c_len, (c + 1) * c_len)
        xc = cum[sl]
        tot = tot_rows[c:c + 1]
        qc = q_ref[sl, :].astype(F32) * (GLA_DK ** -0.5)
        kc = k_ref[sl, :].astype(F32)
        vc = v_ref[sl, :]
        q_dec = qc * jnp.exp(xc)
        k_dec = kc * jnp.exp(tot - xc)

        attn = intra_scores(sl, xc, qc, kc).astype(BF16)
        s_prev = s_scr[...]
        o_inter = _dot(stack_heads(q_dec), s_prev.astype(BF16))
        kv = []
        for h in range(GLA_HEADS):
            rs = slice(h * c_len, (h + 1) * c_len)
            vs = slice(h * GLA_DV, (h + 1) * GLA_DV)
            o_h = o_inter[rs] + _dot(attn[rs], vc[:, vs])
            o_ref[sl, vs] = o_h.astype(BF16)
            kv.append(_dot_tn(k_dec[:, h * GLA_DK:(h + 1) * GLA_DK].astype(BF16), vc[:, vs]))
        s_scr[...] = s_prev * jnp.exp(tot_cols[:, c:c + 1]) + jnp.concatenate(kv, axis=0)

    return step


def _gla_kernel(qf_ref, kf_ref, vf_ref, mf_ref, qb_ref, kb_ref, vb_ref, mb_ref,
                wg_ref, bg_ref, tri_ref, s0_ref, of_ref, ob_ref, sfin_ref, sf_scr, sb_scr, *, n_chunks):
    blk = pl.program_id(1)

    @pl.when(blk == 0)
    def _():
        sf_scr[...] = s0_ref[0, 0]
        sb_scr[...] = s0_ref[1, 0]

    pre_f = _dot(mf_ref[...], wg_ref[0]) + bg_ref[0]
    pre_b = _dot(mb_ref[...], wg_ref[1]) + bg_ref[1]
    extreme = jnp.maximum(jnp.max(-pre_f), jnp.max(-pre_b)) > GLA_PRE_SAFE

    def run(exact):
        fwd = _gla_direction(qf_ref, kf_ref, vf_ref, pre_f, tri_ref[0], sf_scr, of_ref,
                             reverse=False, n_chunks=n_chunks, exact=exact)
        bwd = _gla_direction(qb_ref, kb_ref, vb_ref, pre_b, tri_ref[1], sb_scr, ob_ref,
                             reverse=True, n_chunks=n_chunks, exact=exact)
        for c in range(n_chunks):
            fwd(c)
            bwd(n_chunks - 1 - c)

    @pl.when(jnp.logical_not(extreme))
    def _():
        run(False)

    @pl.when(extreme)
    def _():
        run(True)

    @pl.when(blk == pl.num_programs(1) - 1)
    def _():
        sfin_ref[0, 0] = sf_scr[...]
        sfin_ref[1, 0] = sb_scr[...]


def _block_diag_tri(n_chunks):
    c = GLA_CHUNK
    eye = np.eye(n_chunks, dtype=np.float32)
    lower = np.kron(eye, np.tril(np.ones((c, c), np.float32)))
    upper = np.kron(eye, np.triu(np.ones((c, c), np.float32)))
    return jnp.asarray(np.stack([lower, upper]), dtype=BF16)


def _gla(p, wg, bg, s0, *, batch, cb):
    t_all = p.shape[0]
    nblk = t_all // batch // cb
    n_chunks = cb // GLA_CHUNK
    assert n_chunks <= 8

    fw = lambda b, i: b * nblk + i
    bw = lambda b, i: b * nblk + (nblk - 1 - i)
    full = lambda a: pl.BlockSpec(a.shape, lambda b, i: (0,) * a.ndim)
    tri = _block_diag_tri(n_chunks)

    def token_specs(tok):
        return [
            pl.BlockSpec((cb, GLA_QK_W), lambda b, i: (tok(b, i), P_Q // GLA_QK_W)),
            pl.BlockSpec((cb, GLA_QK_W), lambda b, i: (tok(b, i), P_K // GLA_QK_W)),
            pl.BlockSpec((cb, GLA_V_W), lambda b, i: (tok(b, i), P_V // GLA_V_W)),
            pl.BlockSpec((cb, LANES), lambda b, i: (tok(b, i), P_MISC // LANES)),
        ]

    state_spec = pl.BlockSpec((2, 1, GLA_QK_W, GLA_DV), lambda b, i: (0, b, 0, 0))
    return pl.pallas_call(
        functools.partial(_gla_kernel, n_chunks=n_chunks),
        grid=(batch, nblk),
        in_specs=token_specs(fw) + token_specs(bw) + [full(wg), full(bg), full(tri), state_spec],
        out_specs=[
            pl.BlockSpec((cb, GLA_V_W), lambda b, i: (fw(b, i), 0)),
            pl.BlockSpec((cb, GLA_V_W), lambda b, i: (bw(b, i), 0)),
            state_spec,
        ],
        out_shape=[
            jax.ShapeDtypeStruct((t_all, GLA_V_W), BF16),
            jax.ShapeDtypeStruct((t_all, GLA_V_W), BF16),
            jax.ShapeDtypeStruct((2, batch, GLA_QK_W, GLA_DV), F32),
        ],
        scratch_shapes=[pltpu.VMEM((GLA_QK_W, GLA_DV), F32), pltpu.VMEM((GLA_QK_W, GLA_DV), F32)],
        compiler_params=_cparams(("arbitrary", "arbitrary")),
        name="gla",
    )(p, p, p, p, p, p, p, p, wg, bg, tri, s0)


def _mlaprep_kernel(cq_ref, ckv_ref, misc_ref, cos_ref, sin_ref, qg_ref, kvg_ref,
                    wqn_ref, wqr_ref, wqs_ref, wknt_ref, wv_ref, perm_ref, eye_ref,
                    q_ref, kt_ref, v_ref):
    tm = cq_ref.shape[0]
    n_groups = 2 if tm % 256 == 0 else 1
    for g in range(n_groups):
        rows = slice(g * tm // n_groups, (g + 1) * tm // n_groups)
        cos = cos_ref[rows, :]
        sin = sin_ref[rows, :]
        cqn = _rms(cq_ref[rows, :].astype(F32), qg_ref[...]).astype(BF16)
        qn = _dot(cqn, wqn_ref[...])
        qr = _dot(cqn, wqr_ref[...])
        qs = _dot(cqn, wqs_ref[...])
        for h in range(MLA_HEADS):
            ls = slice(h * LANES, (h + 1) * LANES)
            q_ref[0, h, rows, 0:MLA_NOPE] = (qn[:, ls] * MLA_Q_SCALE).astype(BF16)
            rot = qr[:, ls] * cos + qs[:, ls] * sin
            q_ref[0, h, rows, MLA_NOPE:MLA_QK] = (rot[:, 0:MLA_ROPE] * MLA_Q_SCALE).astype(BF16)

        ckvn = _rms(ckv_ref[rows, :].astype(F32), kvg_ref[...]).astype(BF16)
        knt = _dot_nt(wknt_ref[...], ckvn)
        vv = _dot(ckvn, wv_ref[...])
        misc = misc_ref[rows, :]
        kr = misc.astype(F32) * cos + _dot(misc, perm_ref[...]) * sin
        krt = _dot_nt(eye_ref[...], kr.astype(BF16)).astype(BF16)
        for h in range(MLA_HEADS):
            kt_ref[0, h, 0, 0:MLA_NOPE, rows] = knt[h * MLA_NOPE:(h + 1) * MLA_NOPE].astype(BF16)
            kt_ref[0, h, 0, MLA_NOPE:MLA_QK, rows] = krt
            v_ref[0, h, rows, 0:MLA_V] = vv[:, h * MLA_V:(h + 1) * MLA_V].astype(BF16)
            v_ref[0, h, rows, MLA_V:MLA_V_EXT] = jnp.ones((vv.shape[0], MLA_V), BF16)


def _mlaprep(p, cos, sin, qg, kvg, wts, *, batch, tm):
    t_all = p.shape[0]
    t = t_all // batch
    nb = t // tm
    ntab = cos.shape[0] // tm
    wqn, wqr, wqs, wknt, wv, perm, eye = wts
    full = lambda a: pl.BlockSpec(a.shape, lambda b, i: (0,) * a.ndim)
    return pl.pallas_call(
        _mlaprep_kernel,
        grid=(batch, nb),
        in_specs=[
            pl.BlockSpec((tm, MLA_Q_RANK), lambda b, i: (b * nb + i, P_CQ // MLA_Q_RANK)),
            pl.BlockSpec((tm, MLA_KV_RANK), lambda b, i: (b * nb + i, P_CKV // MLA_KV_RANK)),
            pl.BlockSpec((tm, LANES), lambda b, i: (b * nb + i, P_MISC // LANES)),
            pl.BlockSpec((tm, LANES), lambda b, i: (i % ntab, 0)),
            pl.BlockSpec((tm, LANES), lambda b, i: (i % ntab, 0)),
            full(qg), full(kvg), full(wqn), full(wqr), full(wqs), full(wknt), full(wv),
            full(perm), full(eye),
        ],
        out_specs=[
            pl.BlockSpec((1, MLA_HEADS, tm, MLA_QK), lambda b, i: (b, 0, i, 0)),
            pl.BlockSpec((1, MLA_HEADS, 1, MLA_QK, tm), lambda b, i: (b, 0, i, 0, 0)),
            pl.BlockSpec((1, MLA_HEADS, tm, MLA_V_EXT), lambda b, i: (b, 0, i, 0)),
        ],
        out_shape=[
            jax.ShapeDtypeStruct((batch, MLA_HEADS, t, MLA_QK), BF16),
            jax.ShapeDtypeStruct((batch, MLA_HEADS, nb, MLA_QK, tm), BF16),
            jax.ShapeDtypeStruct((batch, MLA_HEADS, t, MLA_V_EXT), BF16),
        ],
        compiler_params=_cparams(("arbitrary", "arbitrary")),
        name="mlaprep",
    )(p, p, p, cos, sin, qg, kvg, wqn, wqr, wqs, wknt, wv, perm, eye)


def _attn_kernel(*refs, n_seg, n_sub):
    q_ref = refs[0]
    kt_refs = refs[1:1 + 2 * n_seg:2]
    v_refs = refs[2:2 + 2 * n_seg:2]
    o_ref = refs[1 + 2 * n_seg]
    m_scr, acc_scr = refs[2 + 2 * n_seg:]

    rows_per_sub = q_ref.shape[2] // n_sub
    m_scr[...] = jnp.full(m_scr.shape, -jnp.inf, F32)
    acc_scr[...] = jnp.zeros(acc_scr.shape, F32)

    for kt_ref, v_ref in zip(kt_refs, v_refs):
        n_blocks, tk = kt_ref.shape[2], kt_ref.shape[4]

        def step(j, carry, kt_ref=kt_ref, v_ref=v_ref, tk=tk):
            kt = kt_ref[0, 0, j]
            v_blk = v_ref[0, 0, pl.ds(pl.multiple_of(j * tk, tk), tk), :]
            for u in range(n_sub):
                rows = slice(u * rows_per_sub, (u + 1) * rows_per_sub)
                s = _dot(q_ref[0, 0, rows, :], kt)
                m_prev = m_scr[rows, :]
                m_next = jnp.maximum(m_prev, jnp.max(s, axis=1, keepdims=True))
                p = jnp.exp2((s - jnp.concatenate([m_next] * (tk // LANES), axis=1)).astype(BF16))
                alpha = jnp.exp2(m_prev - m_next)
                acc_scr[rows, :] = (jnp.concatenate([alpha] * (MLA_V_EXT // LANES), axis=1)
                                    * acc_scr[rows, :] + _dot(p, v_blk))
                m_scr[rows, :] = m_next
            return carry

        lax.fori_loop(0, n_blocks, step, 0)

    o_ref[0] = (acc_scr[:, 0:MLA_V] / acc_scr[:, MLA_V:MLA_V_EXT]).astype(BF16)


def _attn_pipe_kernel(q_ref, kt_ref, v_ref, ktt_ref, vt_ref, o_ref,
                      m_scr, acc_scr, s0_scr, s1_scr, st_scr, *, n_sub):
    n_blocks, tk = kt_ref.shape[2], kt_ref.shape[4]
    tq = m_scr.shape[0]
    n_q = q_ref.shape[2] // tq
    rows_per_sub = tq // n_sub
    subs = [slice(u * rows_per_sub, (u + 1) * rows_per_sub) for u in range(n_sub)]
    bufs = (s0_scr, s1_scr)

    def reset():
        m_scr[...] = jnp.full(m_scr.shape, -jnp.inf, F32)
        acc_scr[...] = jnp.zeros(acc_scr.shape, F32)

    def scores(qi, kt, s_ref):
        for rows in subs:
            q_rows = pl.ds(pl.multiple_of(qi * tq + rows.start, rows_per_sub), rows_per_sub)
            s_ref[rows, :] = _dot(q_ref[0, 0, q_rows, :], kt)

    def softmax_pv(s_ref, v_blk):
        width = s_ref.shape[1]
        for rows in subs:
            s = s_ref[rows, :]
            m_prev = m_scr[rows, :]
            m_next = jnp.maximum(m_prev, jnp.max(s, axis=1, keepdims=True))
            p = jnp.exp2((s - jnp.concatenate([m_next] * (width // LANES), axis=1)).astype(BF16))
            alpha = jnp.exp2(m_prev - m_next)
            acc_scr[rows, :] = (jnp.concatenate([alpha] * (MLA_V_EXT // LANES), axis=1)
                                * acc_scr[rows, :] + _dot(p, v_blk))
            m_scr[rows, :] = m_next

    reset()
    scores(0, kt_ref[0, 0, 0], bufs[0])

    def query_block(qi, carry):
        for j in range(n_blocks):
            if j + 1 < n_blocks:
                scores(qi, kt_ref[0, 0, j + 1], bufs[(j + 1) % 2])
            else:
                scores(qi, ktt_ref[0, 0, 0], st_scr)
            softmax_pv(bufs[j % 2], v_ref[0, 0, j * tk:(j + 1) * tk, :])
        scores(jnp.minimum(qi + 1, n_q - 1), kt_ref[0, 0, 0], bufs[0])
        softmax_pv(st_scr, vt_ref[0, 0])
        o_rows = pl.ds(pl.multiple_of(qi * tq, tq), tq)
        o_ref[0, o_rows, :] = (acc_scr[:, 0:MLA_V] / acc_scr[:, MLA_V:MLA_V_EXT]).astype(BF16)
        reset()
        return carry

    lax.fori_loop(0, n_q, query_block, 0)


def _attention_pipelined(q, kt, v, kt_tail, v_tail, *, tq, n_sub):
    b, h, t, dqk = q.shape
    tk, tt = kt.shape[4], kt_tail.shape[4]
    assert kt_tail.shape[2] == 1
    return pl.pallas_call(
        functools.partial(_attn_pipe_kernel, n_sub=n_sub),
        grid=(b, h),
        in_specs=[
            pl.BlockSpec((1, 1, t, dqk), lambda bi, hi: (bi, hi, 0, 0)),
            pl.BlockSpec((1, 1) + kt.shape[2:], lambda bi, hi: (bi, hi, 0, 0, 0)),
            pl.BlockSpec((1, 1) + v.shape[2:], lambda bi, hi: (bi, hi, 0, 0)),
            pl.BlockSpec((1, 1) + kt_tail.shape[2:], lambda bi, hi: (bi, hi, 0, 0, 0)),
            pl.BlockSpec((1, 1) + v_tail.shape[2:], lambda bi, hi: (bi, hi, 0, 0)),
        ],
        out_specs=pl.BlockSpec((1, t, MLA_V), lambda bi, hi: (bi, 0, hi)),
        out_shape=jax.ShapeDtypeStruct((b, t, h * MLA_V), BF16),
        scratch_shapes=[pltpu.VMEM((tq, LANES), F32), pltpu.VMEM((tq, MLA_V_EXT), F32),
                        pltpu.VMEM((tq, tk), F32), pltpu.VMEM((tq, tk), F32),
                        pltpu.VMEM((tq, tt), F32)],
        compiler_params=_cparams(("arbitrary", "arbitrary")),
        name="mla_attention_pipe",
    )(q, kt, v, kt_tail, v_tail)


def _attention(q, segs, *, tq, n_sub):
    b, h, t, dqk = q.shape
    in_specs = [pl.BlockSpec((1, 1, tq, dqk), lambda bi, hi, qi: (bi, hi, qi, 0))]
    args = [q]
    for kt, v in segs:
        in_specs.append(pl.BlockSpec((1, 1) + kt.shape[2:], lambda bi, hi, qi: (bi, hi, 0, 0, 0)))
        in_specs.append(pl.BlockSpec((1, 1) + v.shape[2:], lambda bi, hi, qi: (bi, hi, 0, 0)))
        args += [kt, v]
    return pl.pallas_call(
        functools.partial(_attn_kernel, n_seg=len(segs), n_sub=n_sub),
        grid=(b, h, t // tq),
        in_specs=in_specs,
        out_specs=pl.BlockSpec((1, tq, MLA_V), lambda bi, hi, qi: (bi, qi, hi)),
        out_shape=jax.ShapeDtypeStruct((b, t, h * MLA_V), BF16),
        scratch_shapes=[pltpu.VMEM((tq, LANES), F32), pltpu.VMEM((tq, MLA_V_EXT), F32)],
        compiler_params=_cparams(("arbitrary", "arbitrary", "arbitrary")),
        name="mla_attention",
    )(*args)


def _mix_residual_norm(x_ref, of_ref, ob_ref, r_ref, mla_ref, m, gg_ref, wo_ref, ln2_ref,
                       rows=slice(None)):
    o = of_ref[rows, :].astype(F32) + ob_ref[rows, :].astype(F32)
    gg = gg_ref[...]
    y = jnp.concatenate(
        [_rms(o[:, h * GLA_DV:(h + 1) * GLA_DV], gg) for h in range(GLA_HEADS)], axis=1)
    mix = (y * _silu(r_ref[rows, :].astype(F32))).astype(BF16)
    yo = _dot(mix, wo_ref[0:GLA_V_W, :]) + _dot(mla_ref[rows, :], wo_ref[GLA_V_W:, :])
    x1 = x_ref[rows, :] + m[2:3] * yo
    h2 = _rms(x1, ln2_ref[...]) * (1.0 + m[4:5]) + m[3:4]
    return x1, h2


def _outproj_ffn_kernel(*refs, groups, final_norm):
    (x_ref, of_ref, ob_ref, r_ref, mla_ref, mod_ref, gg_ref, wo_ref, ln2_ref,
     wg_ref, wu_ref, wd_ref) = refs[:12]
    fin_ref = refs[12] if final_norm else None
    o_ref = refs[-1]
    m = mod_ref[0]
    x1, h2 = _mix_residual_norm(x_ref, of_ref, ob_ref, r_ref, mla_ref, m, gg_ref, wo_ref, ln2_ref)
    h = h2.astype(BF16)
    y = None
    for lo, hi in groups:
        a = _dot(h, wg_ref[:, lo:hi])
        u = _dot(h, wu_ref[:, lo:hi])
        part = _dot((_silu(a) * u).astype(BF16), wd_ref[lo:hi, :])
        y = part if y is None else y + part
    x2 = x1 + m[5:6] * y
    if final_norm:
        x2 = _rms(x2, fin_ref[...])
    o_ref[...] = x2


def _outproj_ffn(x, o_f, o_b, p, mla, mods, row_fn, gg, wo, ln2, wg, wu, wd, fin_g, tm):
    t, d = x.shape
    ff = wg.shape[1]
    final_norm = fin_g is not None
    step = min(ff, DENSE_FF_GROUP)
    groups = tuple((lo, min(lo + step, ff)) for lo in range(0, ff, step))
    resident = lambda a: pl.BlockSpec(a.shape, lambda i: (0,) * a.ndim, pipeline_mode=pl.Buffered(1))
    in_specs = [
        pl.BlockSpec((tm, d), lambda i: (i, 0)),
        pl.BlockSpec((tm, GLA_V_W), lambda i: (i, 0)),
        pl.BlockSpec((tm, GLA_V_W), lambda i: (i, 0)),
        pl.BlockSpec((tm, GLA_V_W), lambda i: (i, P_R // GLA_V_W)),
        pl.BlockSpec((tm, MLA_V_W), lambda i: (i, 0)),
        pl.BlockSpec((1, 6, d), lambda i: (row_fn(i), 0, 0)),
        resident(gg), resident(wo), resident(ln2), resident(wg), resident(wu), resident(wd),
    ]
    args = [x, o_f, o_b, p, mla, mods, gg, wo, ln2, wg, wu, wd]
    if final_norm:
        in_specs.append(resident(fin_g))
        args.append(fin_g)
    return pl.pallas_call(
        functools.partial(_outproj_ffn_kernel, groups=groups, final_norm=final_norm),
        grid=(t // tm,),
        in_specs=in_specs,
        out_specs=pl.BlockSpec((tm, d), lambda i: (i, 0)),
        out_shape=jax.ShapeDtypeStruct((t, d), F32),
        compiler_params=_cparams(("arbitrary",)),
        name="outproj_ffn",
    )(*args)


def _outproj_kernel(*refs, with_router):
    (x_ref, of_ref, ob_ref, r_ref, mla_ref, mod_ref, gg_ref, wo_ref, ln2_ref) = refs[:9]
    if with_router:
        rwh_ref, rwl_ref, x1_ref, h2_ref, comb_ref = refs[9:]
    else:
        x1_ref, h2_ref = refs[9:]
    m = mod_ref[0]
    tm = x_ref.shape[0]
    n_groups = 2 if tm % 32 == 0 else 1
    for g in range(n_groups):
        rows = slice(g * tm // n_groups, (g + 1) * tm // n_groups)
        x1, h2 = _mix_residual_norm(x_ref, of_ref, ob_ref, r_ref, mla_ref, m, gg_ref, wo_ref,
                                    ln2_ref, rows)
        x1_ref[rows, :] = x1
        if not with_router:
            h2_ref[rows, :] = h2.astype(BF16)
            continue
        h2_ref[rows, :] = _pack_bf16_pairs(h2)
        h_hi = h2.astype(BF16)
        h_lo = (h2 - h_hi.astype(F32)).astype(BF16)
        logits = _dot(h_hi, rwh_ref[...]) + _dot(h_lo, rwh_ref[...]) + _dot(h_hi, rwl_ref[...])
        lane = lax.broadcasted_iota(jnp.int32, logits.shape, 1).astype(F32)
        neg = jnp.float32(-jnp.inf)
        logits = jnp.where(lane < N_EXPERTS, logits, neg)
        m1 = jnp.max(logits, axis=1, keepdims=True)
        i1 = jnp.min(jnp.where(logits == m1, lane, float(LANES)), axis=1, keepdims=True)
        rest = jnp.where(lane == i1, neg, logits)
        m2 = jnp.max(rest, axis=1, keepdims=True)
        i2 = jnp.min(jnp.where(rest == m2, lane, float(LANES)), axis=1, keepdims=True)
        e2 = jnp.exp(m2 - m1)
        w1 = 1.0 / (1.0 + e2)
        comb_ref[rows, :] = (jnp.where(lane == ROUTE_E1, i1, 0.0) + jnp.where(lane == ROUTE_E2, i2, 0.0)
                             + jnp.where(lane == ROUTE_W1, w1, 0.0)
                             + jnp.where(lane == ROUTE_W2, e2 * w1, 0.0))


def _outproj(x, o_f, o_b, p, mla, mods, row_fn, gg, wo, ln2, router, tm):
    t, d = x.shape
    with_router = router is not None
    full = lambda a: pl.BlockSpec(a.shape, lambda i: (0,) * a.ndim)
    in_specs = [
        pl.BlockSpec((tm, d), lambda i: (i, 0)),
        pl.BlockSpec((tm, GLA_V_W), lambda i: (i, 0)),
        pl.BlockSpec((tm, GLA_V_W), lambda i: (i, 0)),
        pl.BlockSpec((tm, GLA_V_W), lambda i: (i, P_R // GLA_V_W)),
        pl.BlockSpec((tm, MLA_V_W), lambda i: (i, 0)),
        pl.BlockSpec((1, 6, d), lambda i: (row_fn(i), 0, 0)),
        full(gg), full(wo), full(ln2),
    ]
    args = [x, o_f, o_b, p, mla, mods, gg, wo, ln2]
    h2_shape = jax.ShapeDtypeStruct((t, d // 2), F32) if with_router else jax.ShapeDtypeStruct((t, d), BF16)
    out_specs = [pl.BlockSpec((tm, d), lambda i: (i, 0)),
                 pl.BlockSpec((tm, h2_shape.shape[1]), lambda i: (i, 0))]
    out_shape = [jax.ShapeDtypeStruct((t, d), F32), h2_shape]
    if with_router:
        in_specs += [full(router[0]), full(router[1])]
        args += list(router)
        out_specs.append(pl.BlockSpec((tm, LANES), lambda i: (i, 0)))
        out_shape.append(jax.ShapeDtypeStruct((t, LANES), F32))
    return pl.pallas_call(
        functools.partial(_outproj_kernel, with_router=with_router),
        grid=(t // tm,),
        in_specs=in_specs,
        out_specs=out_specs,
        out_shape=out_shape,
        compiler_params=_cparams(("arbitrary",)),
        name="outproj",
    )(*args)


def _ffn_kernel(*refs, with_comb, final_norm):
    h_ref, x1_ref, mod_ref = refs[:3]
    k = 3
    comb_ref = fin_ref = None
    if with_comb:
        comb_ref = refs[k]
        k += 1
    wg_ref, wu_ref, wd_ref = refs[k:k + 3]
    k += 3
    if final_norm:
        fin_ref = refs[k]
        k += 1
    o_ref, acc = refs[k:]
    e = pl.program_id(1)
    f = pl.program_id(2)

    @pl.when((e == 0) & (f == 0))
    def _():
        acc[...] = jnp.zeros(acc.shape, F32)

    h = h_ref[...]
    a = _dot(h, wg_ref[0])
    u = _dot(h, wu_ref[0])
    act = _silu(a) * u
    if with_comb:
        comb = comb_ref[...]
        lane = lax.broadcasted_iota(jnp.int32, comb.shape, 1)
        act = act * jnp.sum(jnp.where(lane == e, comb, 0.0), axis=1, keepdims=True)
    acc[...] += _dot(act.astype(BF16), wd_ref[0])

    @pl.when((e == pl.num_programs(1) - 1) & (f == pl.num_programs(2) - 1))
    def _():
        x2 = x1_ref[...] + mod_ref[0][5:6] * acc[...]
        if final_norm:
            x2 = _rms(x2, fin_ref[...])
        o_ref[...] = x2


def _ffn(h2, x1, mods, row_fn, comb, wg, wu, wd, fin_g, tm, tf):
    t, d = x1.shape
    n_e, _, ff = wg.shape
    with_comb = comb is not None
    final_norm = fin_g is not None
    in_specs = [
        pl.BlockSpec((tm, d), lambda i, e, f: (i, 0)),
        pl.BlockSpec((tm, d), lambda i, e, f: (i, 0)),
        pl.BlockSpec((1, 6, d), lambda i, e, f: (row_fn(i), 0, 0)),
    ]
    args = [h2, x1, mods]
    if with_comb:
        in_specs.append(pl.BlockSpec((tm, LANES), lambda i, e, f: (i, 0)))
        args.append(comb)
    in_specs += [
        pl.BlockSpec((1, d, tf), lambda i, e, f: (e, 0, f)),
        pl.BlockSpec((1, d, tf), lambda i, e, f: (e, 0, f)),
        pl.BlockSpec((1, tf, d), lambda i, e, f: (e, f, 0)),
    ]
    args += [wg, wu, wd]
    if final_norm:
        in_specs.append(pl.BlockSpec((1, d), lambda i, e, f: (0, 0)))
        args.append(fin_g)
    return pl.pallas_call(
        functools.partial(_ffn_kernel, with_comb=with_comb, final_norm=final_norm),
        grid=(t // tm, n_e, ff // tf),
        in_specs=in_specs,
        out_specs=pl.BlockSpec((tm, d), lambda i, e, f: (i, 0)),
        out_shape=jax.ShapeDtypeStruct((t, d), F32),
        scratch_shapes=[pltpu.VMEM((tm, d), F32)],
        compiler_params=_cparams(("arbitrary", "arbitrary", "arbitrary")),
        name="ffn",
    )(*args)


def _sc_row_gather(table, idx):
    _, w = table.shape
    b = idx.shape[0]
    n_workers = SC_CORES * SC_SUBCORES
    assert b % (n_workers * SC_GATHER_ROWS) == 0, (b, n_workers, SC_GATHER_ROWS)
    b_per_w = b // n_workers
    n_chunks = b_per_w // SC_GATHER_ROWS
    mesh = plsc.VectorSubcoreMesh(core_axis_name="c", subcore_axis_name="s",
                                  num_cores=SC_CORES, num_subcores=SC_SUBCORES)

    def body(table_hbm, idx_hbm, out_hbm, idx_a, idx_b, rows_a, rows_b, sem_a, sem_b):
        wid = lax.axis_index("s") * SC_CORES + lax.axis_index("c")
        base = wid * b_per_w
        idx_bufs, row_bufs, sems = (idx_a, idx_b), (rows_a, rows_b), (sem_a, sem_b)

        def start(ci):
            slot = ci % 2
            pltpu.sync_copy(idx_hbm.at[pl.ds(base + ci * SC_GATHER_ROWS, SC_GATHER_ROWS)], idx_bufs[slot])
            return pltpu.async_copy(table_hbm.at[idx_bufs[slot]], row_bufs[slot], sems[slot])

        pending = start(0)
        for ci in range(n_chunks):
            following = start(ci + 1) if ci + 1 < n_chunks else None
            pending.wait()
            pltpu.sync_copy(row_bufs[ci % 2], out_hbm.at[pl.ds(base + ci * SC_GATHER_ROWS, SC_GATHER_ROWS)])
            pending = following

    return pl.kernel(
        body,
        out_type=jax.ShapeDtypeStruct((b, w), F32),
        mesh=mesh,
        scratch_types=[pltpu.VMEM((SC_GATHER_ROWS,), jnp.int32)] * 2
        + [pltpu.VMEM((SC_GATHER_ROWS, w), F32)] * 2 + [pltpu.SemaphoreType.DMA] * 2,
        name="sc_row_gather",
    )(table, idx)


def _sc_row_scatter2(table, pos, n_out):
    t, w = table.shape
    n_workers = SC_CORES * SC_SUBCORES
    assert t % (n_workers * SC_GATHER_ROWS) == 0, (t, n_workers, SC_GATHER_ROWS)
    t_per_w = t // n_workers
    n_chunks = t_per_w // SC_GATHER_ROWS
    mesh = plsc.VectorSubcoreMesh(core_axis_name="c", subcore_axis_name="s",
                                  num_cores=SC_CORES, num_subcores=SC_SUBCORES)

    def body(table_hbm, pos_hbm, out_hbm, i0a, i1a, i0b, i1b, rows_a, rows_b, sem_a, sem_b):
        wid = lax.axis_index("s") * SC_CORES + lax.axis_index("c")
        base = wid * t_per_w
        idx0, idx1, row_bufs, sems = (i0a, i0b), (i1a, i1b), (rows_a, rows_b), (sem_a, sem_b)

        def drain(pair):
            if pair is not None:
                pair[0].wait()
                pair[1].wait()

        pending = [None, None]
        for ci in range(n_chunks):
            slot = ci % 2
            drain(pending[slot])
            off = base + ci * SC_GATHER_ROWS
            pltpu.sync_copy(pos_hbm.at[pl.ds(off, SC_GATHER_ROWS)], idx0[slot])
            pltpu.sync_copy(pos_hbm.at[pl.ds(t + off, SC_GATHER_ROWS)], idx1[slot])
            pltpu.sync_copy(table_hbm.at[pl.ds(off, SC_GATHER_ROWS)], row_bufs[slot])
            pending[slot] = (pltpu.async_copy(row_bufs[slot], out_hbm.at[idx0[slot]], sems[slot]),
                             pltpu.async_copy(row_bufs[slot], out_hbm.at[idx1[slot]], sems[slot]))
        drain(pending[0])
        drain(pending[1])

    return pl.kernel(
        body,
        out_type=jax.ShapeDtypeStruct((n_out, w), F32),
        mesh=mesh,
        scratch_types=[pltpu.VMEM((SC_GATHER_ROWS,), jnp.int32)] * 4
        + [pltpu.VMEM((SC_GATHER_ROWS, w), F32)] * 2 + [pltpu.SemaphoreType.DMA] * 2,
        name="sc_row_scatter",
    )(table, pos)


def _moe_plan_kernel(route_ref, tri_ref, utri_ref, pos_ref, cnt_ref, run_scr, off_scr):
    phase = pl.program_id(0)
    blk = pl.program_id(1)
    route = route_ref[...]
    lane = lax.broadcasted_iota(jnp.int32, route.shape, 1).astype(F32)
    oh1 = jnp.where(lane == route[:, ROUTE_E1:ROUTE_E1 + 1], 1.0, 0.0)
    oh2 = jnp.where(lane == route[:, ROUTE_E2:ROUTE_E2 + 1], 1.0, 0.0)
    oh = oh1 + oh2

    @pl.when(blk == 0)
    def _():
        run_scr[...] = jnp.zeros(run_scr.shape, F32)

    @pl.when(phase == 0)
    def _():
        run_scr[...] += jnp.sum(oh, axis=0, keepdims=True)

        @pl.when(blk == pl.num_programs(1) - 1)
        def _():
            counts = run_scr[...]
            cnt_ref[...] = counts
            tiles_per = jnp.floor((counts + (MOE_TILE - 1.0)) * (1.0 / MOE_TILE))
            tile_end = _dot(jnp.broadcast_to(tiles_per, (8, LANES)).astype(BF16), utri_ref[...])[0:1]
            off_scr[...] = (tile_end - tiles_per) * float(MOE_TILE)

    @pl.when(phase == 1)
    def _():
        incl = _dot(tri_ref[...], oh.astype(BF16))
        before = incl - oh + run_scr[...] + off_scr[...]
        p1 = jnp.sum(before * oh1, axis=1, keepdims=True)
        p2 = jnp.sum(before * oh2, axis=1, keepdims=True)
        pos = jnp.where(lane == 0.0, p1, 0.0) + jnp.where(lane == 1.0, p2, 0.0)
        pos_ref[...] = pos.astype(jnp.int32)
        run_scr[...] += incl[incl.shape[0] - 1:, :]


def _moe_plan(route, n_tiles, tm):
    t = route.shape[0]
    tri = jnp.asarray(np.tril(np.ones((tm, tm), np.float32)), dtype=BF16)
    utri = jnp.asarray(np.triu(np.ones((LANES, LANES), np.float32)), dtype=BF16)
    pos, counts = pl.pallas_call(
        _moe_plan_kernel,
        grid=(2, t // tm),
        in_specs=[
            pl.BlockSpec((tm, LANES), lambda p, i: (i, 0)),
            pl.BlockSpec((tm, tm), lambda p, i: (0, 0)),
            pl.BlockSpec((LANES, LANES), lambda p, i: (0, 0)),
        ],
        out_specs=[
            pl.BlockSpec((tm, LANES), lambda p, i: (i * p, 0)),
            pl.BlockSpec((1, LANES), lambda p, i: (0, 0)),
        ],
        out_shape=[jax.ShapeDtypeStruct((t, LANES), jnp.int32),
                   jax.ShapeDtypeStruct((1, LANES), F32)],
        scratch_shapes=[pltpu.VMEM((1, LANES), F32), pltpu.VMEM((1, LANES), F32)],
        compiler_params=_cparams(("arbitrary", "arbitrary")),
        name="moe_plan",
    )(route, tri, utri)
    counts = counts[0, :N_EXPERTS].astype(jnp.int32)
    tile_end = jnp.cumsum((counts + MOE_TILE - 1) // MOE_TILE)
    n_used = tile_end[-1]
    tile_ids = jnp.minimum(jnp.arange(n_tiles, dtype=jnp.int32), n_used - 1)
    tile_expert = jnp.sum((tile_end[None, :] <= tile_ids[:, None]).astype(jnp.int32), axis=1)
    return pos[:, :2].T, tile_expert, n_used.reshape(1)


def _moe_ffn_kernel(te_ref, nused_ref, xs_ref, wg_ref, wu_ref, wd_ref, o_ref, acc, h_scr, *, splits):
    i = pl.program_id(0)
    f = pl.program_id(1)
    last_f = pl.num_programs(1) - 1
    used = i < nused_ref[0]

    @pl.when(used)
    def _():
        @pl.when(f == 0)
        def _():
            acc[...] = jnp.zeros(acc.shape, F32)
            h_scr[...] = _unpack_bf16_pairs(xs_ref[...]).astype(BF16)

        h = h_scr[...]
        for lo, hi in splits:
            a = _dot(h, wg_ref[0, :, lo:hi].astype(BF16))
            u = _dot(h, wu_ref[0, :, lo:hi].astype(BF16))
            acc[...] += _dot((_silu(a) * u).astype(BF16), wd_ref[0, lo:hi, :].astype(BF16))

        @pl.when(f == last_f)
        def _():
            o_ref[...] = _pack_bf16_pairs(acc[...])

    @pl.when(jnp.logical_not(used) & (f == last_f))
    def _():
        o_ref[...] = jnp.zeros(o_ref.shape, F32)


def _moe_ffn(xs, tile_expert, n_used, wg, wu, wd):
    rows, half = xs.shape
    d = 2 * half
    ff = wg.shape[2]
    if ff % MOE_FF_TILE == 0:
        tf, splits = MOE_FF_TILE, MOE_FF_SPLITS
    else:
        tf, splits = ff, ((0, ff),)
    n_tiles = rows // MOE_TILE
    grid_spec = pltpu.PrefetchScalarGridSpec(
        num_scalar_prefetch=2,
        grid=(n_tiles, ff // tf),
        in_specs=[
            pl.BlockSpec((MOE_TILE, half), lambda i, f, te, nu: (i, 0)),
            pl.BlockSpec((1, d, tf), lambda i, f, te, nu: (te[i], 0, f)),
            pl.BlockSpec((1, d, tf), lambda i, f, te, nu: (te[i], 0, f)),
            pl.BlockSpec((1, tf, d), lambda i, f, te, nu: (te[i], f, 0)),
        ],
        out_specs=pl.BlockSpec((MOE_TILE, half), lambda i, f, te, nu: (i, 0)),
        scratch_shapes=[pltpu.VMEM((MOE_TILE, d), F32), pltpu.VMEM((MOE_TILE, d), BF16)],
    )
    return pl.pallas_call(
        functools.partial(_moe_ffn_kernel, splits=splits),
        grid_spec=grid_spec,
        out_shape=jax.ShapeDtypeStruct((rows, half), F32),
        compiler_params=_cparams(("arbitrary", "arbitrary")),
        name="moe_ffn",
    )(tile_expert, n_used, xs, wg, wu, wd)


def _combine_kernel(*refs, final_norm):
    x1_ref, y0_ref, y1_ref, route_ref, mod_ref = refs[:5]
    fin_ref = refs[5] if final_norm else None
    o_ref = refs[-1]
    route = route_ref[...]
    w1 = route[:, ROUTE_W1:ROUTE_W1 + 1]
    w2 = route[:, ROUTE_W2:ROUTE_W2 + 1]
    y = w1 * _unpack_bf16_pairs(y0_ref[...]) + w2 * _unpack_bf16_pairs(y1_ref[...])
    x2 = x1_ref[...] + mod_ref[0][5:6] * y
    if final_norm:
        x2 = _rms(x2, fin_ref[...])
    o_ref[...] = x2


def _combine(x1, yg, route, mods, row_fn, fin_g, tm):
    t, d = x1.shape
    nb = t // tm
    final_norm = fin_g is not None
    in_specs = [
        pl.BlockSpec((tm, d), lambda i: (i, 0)),
        pl.BlockSpec((tm, d // 2), lambda i: (i, 0)),
        pl.BlockSpec((tm, d // 2), lambda i: (i + nb, 0)),
        pl.BlockSpec((tm, LANES), lambda i: (i, 0)),
        pl.BlockSpec((1, 6, d), lambda i: (row_fn(i), 0, 0)),
    ]
    args = [x1, yg, yg, route, mods]
    if final_norm:
        in_specs.append(pl.BlockSpec((1, d), lambda i: (0, 0)))
        args.append(fin_g)
    return pl.pallas_call(
        functools.partial(_combine_kernel, final_norm=final_norm),
        grid=(nb,),
        in_specs=in_specs,
        out_specs=pl.BlockSpec((tm, d), lambda i: (i, 0)),
        out_shape=jax.ShapeDtypeStruct((t, d), F32),
        compiler_params=_cparams(("arbitrary",)),
        name="moe_combine",
    )(*args)


def _moe(h2, x1, route, mods, row_fn, wg, wu, wd, fin_g, tm):
    t = h2.shape[0]
    n_tiles = -(-2 * t // MOE_TILE) + N_EXPERTS
    pos, tile_expert, n_used = _moe_plan(route, n_tiles, _pick_tile(t, ATTN_TILE))
    pos = pos.reshape(-1)
    xs = _sc_row_scatter2(h2, pos, n_tiles * MOE_TILE)
    ys = _moe_ffn(xs, tile_expert, n_used, wg, wu, wd)
    yg = _sc_row_gather(ys, pos)
    return _combine(x1, yg, route, mods, row_fn, fin_g, tm)


def _rope_partner():
    j = np.arange(MLA_ROPE)
    return np.where((j % 32) < 16, j + 16, j - 16)


def _prep_in_weight(w):
    d = w.shape[0]
    widths = (GLA_QK_W, GLA_QK_W, GLA_V_W, GLA_GATE_RANK, GLA_GATE_RANK, GLA_V_W,
              MLA_Q_RANK, MLA_KV_RANK, MLA_ROPE)
    offs = np.concatenate([[0], np.cumsum(widths)])
    part = lambda i: w[:, offs[i]:offs[i + 1]]
    assert w.shape[1] == offs[-1] and P_MISC + MISC_GB + GLA_GATE_RANK <= P_WIDTH
    cols = [part(0), part(1), part(2), part(5), part(6), part(7), part(8), part(3), part(4),
            jnp.zeros((d, P_WIDTH - int(offs[-1])), w.dtype)]
    return jnp.concatenate(cols, axis=1).astype(BF16)


def _prep_gate_weight(w_g2, b_g2):
    ws = []
    for z, off in ((0, MISC_GF), (1, MISC_GB)):
        ws.append(jnp.zeros((LANES, GLA_QK_W), F32).at[off:off + GLA_GATE_RANK].set(w_g2[z]))
    return jnp.stack(ws).astype(BF16), b_g2.reshape(2, 1, GLA_QK_W)


def _prep_mla_weights(w_uq, w_ukv):
    partner = _rope_partner()
    wq = w_uq.reshape(MLA_Q_RANK, MLA_HEADS, MLA_QK)
    wqn = wq[:, :, :MLA_NOPE].reshape(MLA_Q_RANK, MLA_HEADS * MLA_NOPE)
    rope = wq[:, :, MLA_NOPE:]
    pad = jnp.zeros((MLA_Q_RANK, MLA_HEADS, LANES - MLA_ROPE), w_uq.dtype)
    wqr = jnp.concatenate([rope, pad], axis=2).reshape(MLA_Q_RANK, MLA_HEADS * LANES)
    wqs = jnp.concatenate([rope[:, :, partner], pad], axis=2).reshape(MLA_Q_RANK, MLA_HEADS * LANES)
    wkv = w_ukv.reshape(MLA_KV_RANK, MLA_HEADS, MLA_NOPE + MLA_V)
    wknt = wkv[:, :, :MLA_NOPE].reshape(MLA_KV_RANK, MLA_HEADS * MLA_NOPE).T
    wv = wkv[:, :, MLA_NOPE:].reshape(MLA_KV_RANK, MLA_HEADS * MLA_V)
    perm = np.zeros((LANES, LANES), np.float32)
    perm[partner, np.arange(MLA_ROPE)] = 1.0
    eye = np.eye(MLA_ROPE, LANES, dtype=np.float32)
    return (wqn.astype(BF16), wqr.astype(BF16), wqs.astype(BF16), wknt.astype(BF16),
            wv.astype(BF16), jnp.asarray(perm, BF16), jnp.asarray(eye, BF16))


def _rope_tables(n_tok):
    rows = n_tok // GRID_W
    row = np.repeat(np.arange(rows, dtype=np.float32), GRID_W)
    col = np.tile(np.arange(GRID_W, dtype=np.float32), rows)
    nfreq = MLA_ROPE // 4
    inv = np.float32(ROPE_BASE) ** (-np.arange(nfreq, dtype=np.float32) / np.float32(nfreq))
    ar = (row[:, None] * inv).astype(np.float32)
    ac = (col[:, None] * inv).astype(np.float32)
    zero = np.zeros((n_tok, LANES - MLA_ROPE), np.float32)
    cos = np.concatenate([np.cos(ar), np.cos(ar), np.cos(ac), np.cos(ac), zero], axis=1)
    sin = np.concatenate([-np.sin(ar), np.sin(ar), -np.sin(ac), np.sin(ac), zero], axis=1)
    return jnp.asarray(cos, F32), jnp.asarray(sin, F32)


def _identity_tables(n_tok):
    cos = jnp.concatenate([jnp.ones((n_tok, MLA_ROPE), F32),
                           jnp.zeros((n_tok, LANES - MLA_ROPE), F32)], axis=1)
    return cos, jnp.zeros((n_tok, LANES), F32)


def _pick_tile(n, pref):
    t = min(n, pref)
    while n % t:
        t //= 2
    return t


def _pick_ff_tile(ff):
    best = LANES
    for m in range(1, ff // LANES + 1):
        if ff % (m * LANES) == 0 and m * LANES <= FFN_MAX_FF_TILE:
            best = m * LANES
    return best


@jax.jit
def _forward(x, c, ctx, c_ctx, w_mod, b_mod, ln1_g, ln2_g, w_in, w_gla_g2, b_gla_g2, gla_norm_g,
             mla_q_norm_g, w_uq, mla_kv_norm_g, w_ukv, w_out, ffn_w_gate, ffn_w_up, ffn_w_down,
             router_w, exp_w_gate, exp_w_up, exp_w_down, final_norm_g):
    batch, seq, d = x.shape
    n_ctx = ctx.shape[1]
    depth = w_mod.shape[0]
    assert d == D_MODEL and batch < 8, (d, batch)

    cvec = jnp.zeros((8, d), F32).at[:batch].set(c).at[batch].set(c_ctx)
    mods_all = _modulation(cvec, w_mod, b_mod).reshape(depth, 8, 6, d)

    xl = x.reshape(batch * seq, d)
    xc = ctx.reshape(batch * n_ctx, d)

    tm_l = _pick_tile(seq, TOKEN_TILE)
    tm_c = _pick_tile(n_ctx, CTX_TILE)
    tk_l = _pick_tile(seq, ATTN_TILE)
    cb_l = _pick_tile(seq, GLA_BLOCK)
    cb_c = _pick_tile(n_ctx, GLA_BLOCK)
    row_l = lambda tm: (lambda i: i // (seq // tm))
    row_c = lambda i: batch

    rope_l = _rope_tables(seq)
    rope_c = _identity_tables(tm_c)
    zero_state = jnp.zeros((2, batch, GLA_QK_W, GLA_DV), F32)

    for i in range(depth):
        need_ctx = i < depth - 1
        last = i == depth - 1
        mods = mods_all[i]
        ln1 = ln1_g[i].reshape(1, d)
        ln2 = ln2_g[i].reshape(1, d)
        w_in_r = _prep_in_weight(w_in[i])
        gates = _prep_gate_weight(w_gla_g2[i], b_gla_g2[i])
        mla_w = _prep_mla_weights(w_uq[i], w_ukv[i])
        qg = mla_q_norm_g[i].reshape(1, MLA_Q_RANK)
        kvg = mla_kv_norm_g[i].reshape(1, MLA_KV_RANK)
        gg = gla_norm_g[i].reshape(1, GLA_DV)
        wo = w_out[i].astype(BF16)

        p_l = _inproj(xl, mods, row_l(tk_l), ln1, w_in_r, tk_l)
        p_c = _inproj(xc, mods, row_c, ln1, w_in_r, tm_c)

        oc_f, oc_b, s_ctx = _gla(p_c, *gates, zero_state, batch=batch, cb=cb_c)
        ol_f, ol_b, _ = _gla(p_l, *gates, s_ctx, batch=batch, cb=cb_l)

        q_l, kt_l, v_l = _mlaprep(p_l, *rope_l, qg, kvg, mla_w, batch=batch, tm=tk_l)
        q_c, kt_c, v_c = _mlaprep(p_c, *rope_c, qg, kvg, mla_w, batch=batch, tm=tm_c)
        m_l = _attention_pipelined(q_l, kt_l, v_l, kt_c, v_c, tq=tk_l, n_sub=1)
        m_l = m_l.reshape(batch * seq, MLA_V_W)

        if i % 2 == 0:
            j = i // 2
            router = None
            wg = ffn_w_gate[j].astype(BF16)
            wu = ffn_w_up[j].astype(BF16)
            wd = ffn_w_down[j].astype(BF16)
        else:
            j = i // 2
            rw = jnp.zeros((d, LANES), F32).at[:, :N_EXPERTS].set(router_w[j])
            rw_hi = rw.astype(BF16)
            router = (rw_hi, (rw - rw_hi.astype(F32)).astype(BF16))
            wg, wu, wd = exp_w_gate[j], exp_w_up[j], exp_w_down[j]
        fin = final_norm_g.reshape(1, d) if last else None

        if router is None:
            xl = _outproj_ffn(xl, ol_f, ol_b, p_l, m_l, mods, row_l(tm_l), gg, wo, ln2,
                              wg, wu, wd, fin, tm_l)
        else:
            outs = _outproj(xl, ol_f, ol_b, p_l, m_l, mods, row_l(tm_l), gg, wo, ln2, router, tm_l)
            xl = _moe(outs[1], outs[0], outs[2], mods, row_l(tk_l), wg, wu, wd, fin, tk_l)

        if need_ctx:
            m_c = _attention(q_c, [(kt_c, v_c)], tq=tm_c, n_sub=1).reshape(batch * n_ctx, MLA_V_W)
            if router is None:
                xc = _outproj_ffn(xc, oc_f, oc_b, p_c, m_c, mods, row_c, gg, wo, ln2,
                                  wg, wu, wd, None, _pick_tile(batch * n_ctx, TOKEN_TILE))
            else:
                outs_c = _outproj(xc, oc_f, oc_b, p_c, m_c, mods, row_c, gg, wo, ln2, router, tm_c)
                tm_fc = _pick_tile(batch * n_ctx, TOKEN_TILE)
                r_c = outs_c[2]
                lane = jnp.arange(LANES, dtype=F32)[None, :]
                comb_c = (jnp.where(lane == r_c[:, ROUTE_E1:ROUTE_E1 + 1], r_c[:, ROUTE_W1:ROUTE_W1 + 1], 0.0)
                          + jnp.where(lane == r_c[:, ROUTE_E2:ROUTE_E2 + 1], r_c[:, ROUTE_W2:ROUTE_W2 + 1], 0.0))
                bits = lax.bitcast_convert_type(outs_c[1], jnp.uint32)
                h2_c = jnp.concatenate([lax.bitcast_convert_type(bits << 16, F32),
                                        lax.bitcast_convert_type(bits & jnp.uint32(0xFFFF0000), F32)], axis=1)
                xc = _ffn(h2_c.astype(BF16), outs_c[0], mods, row_c, comb_c,
                          wg.astype(BF16), wu.astype(BF16), wd.astype(BF16), None, tm_fc,
                          _pick_ff_tile(wg.shape[2]))

    return xl.reshape(batch, seq, d)


def kernel(x, c, ctx, c_ctx, w_mod, b_mod, ln1_g, ln2_g, w_in, w_gla_g2, b_gla_g2, gla_norm_g,
           mla_q_norm_g, w_uq, mla_kv_norm_g, w_ukv, w_out, ffn_w_gate, ffn_w_up, ffn_w_down,
           router_w, exp_w_gate, exp_w_up, exp_w_down, final_norm_g):
    return _forward(x, c, ctx, c_ctx, w_mod, b_mod, ln1_g, ln2_g, w_in, w_gla_g2, b_gla_g2,
                    gla_norm_g, mla_q_norm_g, w_uq, mla_kv_norm_g, w_ukv, w_out, ffn_w_gate,
                    ffn_w_up, ffn_w_down, router_w, exp_w_gate, exp_w_up, exp_w_down, final_norm_g)
```

```python
import functools

import numpy as np
import jax
import jax.numpy as jnp
from jax import lax
from jax.experimental import pallas as pl
from jax.experimental.pallas import tpu as pltpu
from jax.experimental.pallas import tpu_sc as plsc

F32 = jnp.float32
BF16 = jnp.bfloat16

D_MODEL = 1024
EPS = 1e-6
GRID_W = 64

GLA_HEADS = 4
GLA_DK = 64
GLA_DV = 128
GLA_GATE_RANK = 16
GLA_GATE_NORM = 16.0
GLA_CHUNK = 64
GLA_QK_W = GLA_HEADS * GLA_DK
GLA_V_W = GLA_HEADS * GLA_DV
GLA_EXP_CLAMP = 80.0
GLA_PRE_SAFE = 29.0

MLA_HEADS = 4
MLA_NOPE = 128
MLA_ROPE = 64
MLA_V = 128
MLA_QK = MLA_NOPE + MLA_ROPE
MLA_Q_RANK = 256
MLA_KV_RANK = 128
MLA_SCALE = MLA_QK ** -0.5
MLA_Q_SCALE = MLA_SCALE * 1.4426950408889634
MLA_V_W = MLA_HEADS * MLA_V
MLA_V_EXT = 2 * MLA_V
ROPE_BASE = 10000.0

N_EXPERTS = 8
LANES = 128
ROUTE_E1, ROUTE_E2, ROUTE_W1, ROUTE_W2 = 0, 1, 2, 3

SC_CORES = 2
SC_SUBCORES = 16
SC_GATHER_ROWS = 64
MOE_TILE = 1024
MOE_FF_TILE = 512
MOE_FF_SPLITS = ((0, 256), (256, 512))

P_Q, P_K, P_V, P_R, P_CQ, P_CKV, P_MISC = 0, 256, 512, 1024, 1536, 1792, 1920
P_WIDTH = 2048
MISC_KR, MISC_GF, MISC_GB = 0, 64, 80

VMEM_LIMIT = 56 * 1024 * 1024

TOKEN_TILE = 512
CTX_TILE = 256
ATTN_TILE = 1024
GLA_BLOCK = 256
DENSE_FF_GROUP = 1024
FFN_MAX_FF_TILE = 1408


def _cparams(sem):
    return pltpu.CompilerParams(dimension_semantics=sem, vmem_limit_bytes=VMEM_LIMIT)


def _rms(x, g):
    return x * lax.rsqrt(jnp.mean(x * x, axis=-1, keepdims=True) + EPS) * g


def _silu(x):
    return x / (1.0 + jnp.exp(-x))


def _dot(a, b):
    return jnp.dot(a, b, preferred_element_type=F32)


def _dot_nt(a, b):
    return lax.dot_general(a, b, (((1,), (1,)), ((), ())), preferred_element_type=F32)


def _dot_tn(a, b):
    return lax.dot_general(a, b, (((0,), (0,)), ((), ())), preferred_element_type=F32)


def _pack_bf16_pairs(x):
    w = x.shape[1] // 2
    words = pltpu.pack_elementwise([x[:, :w], x[:, w:]], packed_dtype=BF16)
    return lax.bitcast_convert_type(words, F32)


def _unpack_bf16_pairs(p):
    words = lax.bitcast_convert_type(p, jnp.int32)
    lo = pltpu.unpack_elementwise(words, index=0, packed_dtype=BF16, unpacked_dtype=F32)
    hi = pltpu.unpack_elementwise(words, index=1, packed_dtype=BF16, unpacked_dtype=F32)
    return jnp.concatenate([lo, hi], axis=1)


def _mod_kernel(c_ref, w_ref, b_ref, o_ref):
    s = _silu(c_ref[...]).astype(BF16)
    o_ref[0] = _dot(s, w_ref[0].astype(BF16)) + b_ref[0]


def _modulation(cvec, w_mod, b_mod):
    depth, d, n = w_mod.shape
    tn = 1536
    return pl.pallas_call(
        _mod_kernel,
        grid=(depth, n // tn),
        in_specs=[
            pl.BlockSpec((8, d), lambda l, j: (0, 0)),
            pl.BlockSpec((1, d, tn), lambda l, j: (l, 0, j)),
            pl.BlockSpec((1, 1, tn), lambda l, j: (l, 0, j)),
        ],
        out_specs=pl.BlockSpec((1, 8, tn), lambda l, j: (l, 0, j)),
        out_shape=jax.ShapeDtypeStruct((depth, 8, n), F32),
        compiler_params=_cparams(("arbitrary", "arbitrary")),
        name="modulation",
    )(cvec, w_mod, b_mod.reshape(depth, 1, n))


def _inproj_kernel(x_ref, mod_ref, g_ref, w_ref, wg_ref, bg_ref, o_ref, gate_ref, *, gate_block):
    m = mod_ref[0]
    tm = x_ref.shape[0]
    n_groups = 2 if tm % (2 * gate_block) == 0 else 1
    rows_per_group = tm // n_groups
    sub = lax.broadcasted_iota(jnp.int32, (8, LANES), 0)
    for g in range(n_groups):
        rows = slice(g * rows_per_group, (g + 1) * rows_per_group)
        h = _rms(x_ref[rows, :], g_ref[...]) * (1.0 + m[1:2]) + m[0:1]
        p = _dot(h.astype(BF16), w_ref[...]).astype(BF16)
        o_ref[rows, :] = p
        misc = p[:, P_MISC:P_MISC + LANES]
        neg = [-(_dot(misc, wg_ref[z]) + bg_ref[z]) for z in range(2)]
        for j in range(rows_per_group // gate_block):
            blk_rows = slice(j * gate_block, (j + 1) * gate_block)
            worst = [jnp.max(jnp.max(n[blk_rows], axis=0, keepdims=True), axis=1, keepdims=True)
                     for n in neg]
            gate_ref[g * (rows_per_group // gate_block) + j] = jnp.where(
                sub == 0, worst[0], jnp.where(sub == 1, worst[1], 0.0))


def _inproj(x, mods, row_fn, ln_g, w, wg, bg, tm, gate_block):
    t, d = x.shape
    assert tm % gate_block == 0
    full = lambda a: pl.BlockSpec(a.shape, lambda i: (0,) * a.ndim)
    return pl.pallas_call(
        functools.partial(_inproj_kernel, gate_block=gate_block),
        grid=(t // tm,),
        in_specs=[
            pl.BlockSpec((tm, d), lambda i: (i, 0)),
            pl.BlockSpec((1, 6, d), lambda i: (row_fn(i), 0, 0)),
            pl.BlockSpec((1, d), lambda i: (0, 0)),
            pl.BlockSpec((d, P_WIDTH), lambda i: (0, 0)),
            full(wg), full(bg),
        ],
        out_specs=[pl.BlockSpec((tm, P_WIDTH), lambda i: (i, 0)),
                   pl.BlockSpec((tm // gate_block, 8, LANES), lambda i: (i, 0, 0))],
        out_shape=[jax.ShapeDtypeStruct((t, P_WIDTH), BF16),
                   jax.ShapeDtypeStruct((t // gate_block, 8, LANES), F32)],
        compiler_params=_cparams(("arbitrary",)),
        name="inproj",
    )(x, mods, ln_g, w, wg, bg)


def _gla_direction(q_ref, k_ref, v_ref, pre, tri, s_scr, o_ref, *, reverse, n_chunks, exact):
    c_len = GLA_CHUNK
    g = (jnp.minimum(pre, 0.0) - jnp.log(1.0 + jnp.exp(-jnp.abs(pre)))) * (1.0 / GLA_GATE_NORM)
    g_hi = g.astype(BF16)
    g_lo = (g - g_hi.astype(F32)).astype(BF16)
    cum = _dot(tri, g_hi) + _dot(tri, g_lo)
    tot_rows = jnp.concatenate(
        [cum[c * c_len:c * c_len + 1] if reverse else cum[(c + 1) * c_len - 1:(c + 1) * c_len]
         for c in range(n_chunks)] + [jnp.zeros((8 - n_chunks, GLA_QK_W), F32)], axis=0)
    t_hi = tot_rows.astype(BF16)
    t_lo = (tot_rows - t_hi.astype(F32)).astype(BF16)
    eye = (lax.broadcasted_iota(jnp.int32, (GLA_QK_W, GLA_QK_W), 0)
           == lax.broadcasted_iota(jnp.int32, (GLA_QK_W, GLA_QK_W), 1))
    eye = jnp.where(eye, 1.0, 0.0).astype(BF16)
    tot_cols = _dot_nt(eye, t_hi) + _dot_nt(eye, t_lo)

    lane = lax.broadcasted_iota(jnp.int32, (c_len, GLA_QK_W), 1)
    head_masks = [(lane >= h * GLA_DK) & (lane < (h + 1) * GLA_DK) for h in range(GLA_HEADS)]
    row = lax.broadcasted_iota(jnp.int32, (GLA_HEADS * c_len, c_len), 0) % c_len
    col = lax.broadcasted_iota(jnp.int32, (GLA_HEADS * c_len, c_len), 1)
    pair_mask = (col >= row) if reverse else (col <= row)

    def stack_heads(a):
        return jnp.concatenate([jnp.where(mk, a, 0.0) for mk in head_masks], axis=0).astype(BF16)

    def exact_levels():
        n = n_chunks * c_len
        r = lax.broadcasted_iota(jnp.int32, (n, n), 0)
        u = lax.broadcasted_iota(jnp.int32, (n, n), 1)
        same_chunk = (r // c_len) == (u // c_len)
        t = r % c_len
        uu = u % c_len
        levels = []
        b = c_len
        while b >= 2:
            half = b // 2
            mid = (t // b) * b + half
            same = same_chunk & ((t // b) == (uu // b))
            if reverse:
                sel_q = same & (t < mid) & (uu >= t) & (uu < mid)
                sel_k = same & (t >= mid) & (uu >= mid) & (uu < t)
            else:
                sel_q = same & (t >= mid) & (uu >= mid) & (uu <= t)
                sel_k = same & (t < mid) & (uu > t) & (uu < mid)
            mq = jnp.where(sel_q, 1.0, 0.0).astype(BF16)
            mk_ = jnp.where(sel_k, 1.0, 0.0).astype(BF16)
            e_q = _dot(mq, g_hi) + _dot(mq, g_lo)
            e_k = _dot(mk_, g_hi) + _dot(mk_, g_lo)
            tq_ = row
            if reverse:
                own = ((tq_ // b) == (col // b)) & ((tq_ % b) < half) & ((col % b) >= half)
            else:
                own = ((tq_ // b) == (col // b)) & ((tq_ % b) >= half) & ((col % b) < half)
            levels.append((e_q, e_k, own))
            b = half
        return levels

    levels = exact_levels() if exact else None

    def intra_scores(sl, xc, qc, kc):
        if not exact:
            ref = xc[c_len // 2:c_len // 2 + 1]
            q_mid = qc * jnp.exp(jnp.minimum(xc - ref, GLA_EXP_CLAMP))
            k_mid = (kc * jnp.exp(jnp.minimum(ref - xc, GLA_EXP_CLAMP))).astype(BF16)
            return jnp.where(pair_mask, _dot_nt(stack_heads(q_mid), k_mid), 0.0)
        attn = jnp.where(row == col, _dot_nt(stack_heads(qc), kc.astype(BF16)), 0.0)
        for e_q, e_k, own in levels:
            a = _dot_nt(stack_heads(qc * jnp.exp(e_q[sl])), (kc * jnp.exp(e_k[sl])).astype(BF16))
            attn = attn + jnp.where(own, a, 0.0)
        return attn

    def step(c):
        sl = slice(c * c_len, (c + 1) * c_len)
        xc = cum[sl]
        tot = tot_rows[c:c + 1]
        qc = q_ref[sl, :].astype(F32) * (GLA_DK ** -0.5)
        kc = k_ref[sl, :].astype(F32)
        vc = v_ref[sl, :]
        q_dec = qc * jnp.exp(xc)
        k_dec = kc * jnp.exp(tot - xc)

        attn = intra_scores(sl, xc, qc, kc).astype(BF16)
        s_prev = s_scr[...]
        o_inter = _dot(stack_heads(q_dec), s_prev.astype(BF16))
        kv = []
        for h in range(GLA_HEADS):
            rs = slice(h * c_len, (h + 1) * c_len)
            vs = slice(h * GLA_DV, (h + 1) * GLA_DV)
            o_h = o_inter[rs] + _dot(attn[rs], vc[:, vs])
            o_ref[sl, vs] = o_h.astype(BF16)
            kv.append(_dot_tn(k_dec[:, h * GLA_DK:(h + 1) * GLA_DK].astype(BF16), vc[:, vs]))
        s_scr[...] = s_prev * jnp.exp(tot_cols[:, c:c + 1]) + jnp.concatenate(kv, axis=0)

    return step


def _gla_kernel(gate_ref, qf_ref, kf_ref, vf_ref, mf_ref, qb_ref, kb_ref, vb_ref, mb_ref,
                wg_ref, bg_ref, tri_ref, s0_ref, of_ref, ob_ref, sfin_ref, sf_scr, sb_scr, *, n_chunks):
    blk = pl.program_id(1)
    n_blk = pl.num_programs(1)

    @pl.when(blk == 0)
    def _():
        sf_scr[...] = s0_ref[0, 0]
        sb_scr[...] = s0_ref[1, 0]

    fwd_block = pl.program_id(0) * n_blk + blk
    bwd_block = pl.program_id(0) * n_blk + (n_blk - 1 - blk)
    extreme = jnp.maximum(gate_ref[2 * fwd_block], gate_ref[2 * bwd_block + 1]) > GLA_PRE_SAFE

    def run(exact):
        pre_f = _dot(mf_ref[...], wg_ref[0]) + bg_ref[0]
        pre_b = _dot(mb_ref[...], wg_ref[1]) + bg_ref[1]
        fwd = _gla_direction(qf_ref, kf_ref, vf_ref, pre_f, tri_ref[0], sf_scr, of_ref,
                             reverse=False, n_chunks=n_chunks, exact=exact)
        bwd = _gla_direction(qb_ref, kb_ref, vb_ref, pre_b, tri_ref[1], sb_scr, ob_ref,
                             reverse=True, n_chunks=n_chunks, exact=exact)
        for c in range(n_chunks):
            fwd(c)
            bwd(n_chunks - 1 - c)

    @pl.when(jnp.logical_not(extreme))
    def _():
        run(False)

    @pl.when(extreme)
    def _():
        run(True)

    @pl.when(blk == pl.num_programs(1) - 1)
    def _():
        sfin_ref[0, 0] = sf_scr[...]
        sfin_ref[1, 0] = sb_scr[...]


def _block_diag_tri(n_chunks):
    c = GLA_CHUNK
    eye = np.eye(n_chunks, dtype=np.float32)
    lower = np.kron(eye, np.tril(np.ones((c, c), np.float32)))
    upper = np.kron(eye, np.triu(np.ones((c, c), np.float32)))
    return jnp.asarray(np.stack([lower, upper]), dtype=BF16)


def _gla(p, gate_bound, wg, bg, s0, *, batch, cb):
    t_all = p.shape[0]
    nblk = t_all // batch // cb
    n_chunks = cb // GLA_CHUNK
    assert n_chunks <= 8 and gate_bound.shape[0] == batch * nblk
    gate = gate_bound[:, 0:2, 0].reshape(-1)

    fw = lambda b, i, gate_ref: b * nblk + i
    bw = lambda b, i, gate_ref: b * nblk + (nblk - 1 - i)
    full = lambda a: pl.BlockSpec(a.shape, lambda b, i, gate_ref: (0,) * a.ndim)
    tri = _block_diag_tri(n_chunks)

    def token_specs(tok):
        return [
            pl.BlockSpec((cb, GLA_QK_W), lambda b, i, g: (tok(b, i, g), P_Q // GLA_QK_W)),
            pl.BlockSpec((cb, GLA_QK_W), lambda b, i, g: (tok(b, i, g), P_K // GLA_QK_W)),
            pl.BlockSpec((cb, GLA_V_W), lambda b, i, g: (tok(b, i, g), P_V // GLA_V_W)),
            pl.BlockSpec((cb, LANES), lambda b, i, g: (tok(b, i, g), P_MISC // LANES)),
        ]

    state_spec = pl.BlockSpec((2, 1, GLA_QK_W, GLA_DV), lambda b, i, g: (0, b, 0, 0))
    grid_spec = pltpu.PrefetchScalarGridSpec(
        num_scalar_prefetch=1,
        grid=(batch, nblk),
        in_specs=token_specs(fw) + token_specs(bw) + [full(wg), full(bg), full(tri), state_spec],
        out_specs=[
            pl.BlockSpec((cb, GLA_V_W), lambda b, i, g: (fw(b, i, g), 0)),
            pl.BlockSpec((cb, GLA_V_W), lambda b, i, g: (bw(b, i, g), 0)),
            state_spec,
        ],
        scratch_shapes=[pltpu.VMEM((GLA_QK_W, GLA_DV), F32), pltpu.VMEM((GLA_QK_W, GLA_DV), F32)],
    )
    return pl.pallas_call(
        functools.partial(_gla_kernel, n_chunks=n_chunks),
        grid_spec=grid_spec,
        out_shape=[
            jax.ShapeDtypeStruct((t_all, GLA_V_W), BF16),
            jax.ShapeDtypeStruct((t_all, GLA_V_W), BF16),
            jax.ShapeDtypeStruct((2, batch, GLA_QK_W, GLA_DV), F32),
        ],
        compiler_params=_cparams(("arbitrary", "arbitrary")),
        name="gla",
    )(gate, p, p, p, p, p, p, p, p, wg, bg, tri, s0)


def _mlaprep_kernel(cq_ref, ckv_ref, misc_ref, cos_ref, sin_ref, qg_ref, kvg_ref,
                    wqn_ref, wqr_ref, wqs_ref, wknt_ref, wv_ref, perm_ref, eye_ref,
                    q_ref, kt_ref, v_ref):
    tm = cq_ref.shape[0]
    n_groups = 2 if tm % 256 == 0 else 1
    for g in range(n_groups):
        rows = slice(g * tm // n_groups, (g + 1) * tm // n_groups)
        cos = cos_ref[rows, :]
        sin = sin_ref[rows, :]
        cqn = _rms(cq_ref[rows, :].astype(F32), qg_ref[...]).astype(BF16)
        qn = _dot(cqn, wqn_ref[...])
        qr = _dot(cqn, wqr_ref[...])
        qs = _dot(cqn, wqs_ref[...])
        for h in range(MLA_HEADS):
            ls = slice(h * LANES, (h + 1) * LANES)
            q_ref[0, h, rows, 0:MLA_NOPE] = (qn[:, ls] * MLA_Q_SCALE).astype(BF16)
            rot = qr[:, ls] * cos + qs[:, ls] * sin
            q_ref[0, h, rows, MLA_NOPE:MLA_QK] = (rot[:, 0:MLA_ROPE] * MLA_Q_SCALE).astype(BF16)

        ckvn = _rms(ckv_ref[rows, :].astype(F32), kvg_ref[...]).astype(BF16)
        knt = _dot_nt(wknt_ref[...], ckvn)
        vv = _dot(ckvn, wv_ref[...])
        misc = misc_ref[rows, :]
        kr = misc.astype(F32) * cos + _dot(misc, perm_ref[...]) * sin
        krt = _dot_nt(eye_ref[...], kr.astype(BF16)).astype(BF16)
        for h in range(MLA_HEADS):
            kt_ref[0, h, 0, 0:MLA_NOPE, rows] = knt[h * MLA_NOPE:(h + 1) * MLA_NOPE].astype(BF16)
            kt_ref[0, h, 0, MLA_NOPE:MLA_QK, rows] = krt
            v_ref[0, h, rows, 0:MLA_V] = vv[:, h * MLA_V:(h + 1) * MLA_V].astype(BF16)
            v_ref[0, h, rows, MLA_V:MLA_V_EXT] = jnp.ones((vv.shape[0], MLA_V), BF16)


def _mlaprep(p, cos, sin, qg, kvg, wts, *, batch, tm):
    t_all = p.shape[0]
    t = t_all // batch
    nb = t // tm
    ntab = cos.shape[0] // tm
    wqn, wqr, wqs, wknt, wv, perm, eye = wts
    full = lambda a: pl.BlockSpec(a.shape, lambda b, i: (0,) * a.ndim)
    return pl.pallas_call(
        _mlaprep_kernel,
        grid=(batch, nb),
        in_specs=[
            pl.BlockSpec((tm, MLA_Q_RANK), lambda b, i: (b * nb + i, P_CQ // MLA_Q_RANK)),
            pl.BlockSpec((tm, MLA_KV_RANK), lambda b, i: (b * nb + i, P_CKV // MLA_KV_RANK)),
            pl.BlockSpec((tm, LANES), lambda b, i: (b * nb + i, P_MISC // LANES)),
            pl.BlockSpec((tm, LANES), lambda b, i: (i % ntab, 0)),
            pl.BlockSpec((tm, LANES), lambda b, i: (i % ntab, 0)),
            full(qg), full(kvg), full(wqn), full(wqr), full(wqs), full(wknt), full(wv),
            full(perm), full(eye),
        ],
        out_specs=[
            pl.BlockSpec((1, MLA_HEADS, tm, MLA_QK), lambda b, i: (b, 0, i, 0)),
            pl.BlockSpec((1, MLA_HEADS, 1, MLA_QK, tm), lambda b, i: (b, 0, i, 0, 0)),
            pl.BlockSpec((1, MLA_HEADS, tm, MLA_V_EXT), lambda b, i: (b, 0, i, 0)),
        ],
        out_shape=[
            jax.ShapeDtypeStruct((batch, MLA_HEADS, t, MLA_QK), BF16),
            jax.ShapeDtypeStruct((batch, MLA_HEADS, nb, MLA_QK, tm), BF16),
            jax.ShapeDtypeStruct((batch, MLA_HEADS, t, MLA_V_EXT), BF16),
        ],
        compiler_params=_cparams(("arbitrary", "arbitrary")),
        name="mlaprep",
    )(p, p, p, cos, sin, qg, kvg, wqn, wqr, wqs, wknt, wv, perm, eye)


def _attn_kernel(*refs, n_seg, n_sub):
    q_ref = refs[0]
    kt_refs = refs[1:1 + 2 * n_seg:2]
    v_refs = refs[2:2 + 2 * n_seg:2]
    o_ref = refs[1 + 2 * n_seg]
    m_scr, acc_scr = refs[2 + 2 * n_seg:]

    rows_per_sub = q_ref.shape[2] // n_sub
    m_scr[...] = jnp.full(m_scr.shape, -jnp.inf, F32)
    acc_scr[...] = jnp.zeros(acc_scr.shape, F32)

    for kt_ref, v_ref in zip(kt_refs, v_refs):
        n_blocks, tk = kt_ref.shape[2], kt_ref.shape[4]

        def step(j, carry, kt_ref=kt_ref, v_ref=v_ref, tk=tk):
            kt = kt_ref[0, 0, j]
            v_blk = v_ref[0, 0, pl.ds(pl.multiple_of(j * tk, tk), tk), :]
            for u in range(n_sub):
                rows = slice(u * rows_per_sub, (u + 1) * rows_per_sub)
                s = _dot(q_ref[0, 0, rows, :], kt)
                m_prev = m_scr[rows, :]
                m_next = jnp.maximum(m_prev, jnp.max(s, axis=1, keepdims=True))
                p = jnp.exp2((s - jnp.concatenate([m_next] * (tk // LANES), axis=1)).astype(BF16))
                alpha = jnp.exp2(m_prev - m_next)
                acc_scr[rows, :] = (jnp.concatenate([alpha] * (MLA_V_EXT // LANES), axis=1)
                                    * acc_scr[rows, :] + _dot(p, v_blk))
                m_scr[rows, :] = m_next
            return carry

        lax.fori_loop(0, n_blocks, step, 0)

    o_ref[0] = (acc_scr[:, 0:MLA_V] / acc_scr[:, MLA_V:MLA_V_EXT]).astype(BF16)


def _attn_pipe_kernel(q_ref, kt_ref, v_ref, ktt_ref, vt_ref, o_ref,
                      m_scr, acc_scr, s0_scr, s1_scr, st_scr, *, n_sub):
    n_blocks, tk = kt_ref.shape[2], kt_ref.shape[4]
    tq = m_scr.shape[0]
    n_q = q_ref.shape[2] // tq
    rows_per_sub = tq // n_sub
    subs = [slice(u * rows_per_sub, (u + 1) * rows_per_sub) for u in range(n_sub)]
    bufs = (s0_scr, s1_scr)

    def reset():
        m_scr[...] = jnp.full(m_scr.shape, -jnp.inf, F32)
        acc_scr[...] = jnp.zeros(acc_scr.shape, F32)

    def scores(qi, kt, s_ref):
        for rows in subs:
            q_rows = pl.ds(pl.multiple_of(qi * tq + rows.start, rows_per_sub), rows_per_sub)
            s_ref[rows, :] = _dot(q_ref[0, 0, q_rows, :], kt)

    def softmax_pv(s_ref, v_blk):
        width = s_ref.shape[1]
        for rows in subs:
            s = s_ref[rows, :]
            m_prev = m_scr[rows, :]
            m_next = jnp.maximum(m_prev, jnp.max(s, axis=1, keepdims=True))
            p = jnp.exp2((s - jnp.concatenate([m_next] * (width // LANES), axis=1)).astype(BF16))
            alpha = jnp.exp2(m_prev - m_next)
            acc_scr[rows, :] = (jnp.concatenate([alpha] * (MLA_V_EXT // LANES), axis=1)
                                * acc_scr[rows, :] + _dot(p, v_blk))
            m_scr[rows, :] = m_next

    reset()
    scores(0, kt_ref[0, 0, 0], bufs[0])

    def query_block(qi, carry):
        for j in range(n_blocks):
            if j + 1 < n_blocks:
                scores(qi, kt_ref[0, 0, j + 1], bufs[(j + 1) % 2])
            else:
                scores(qi, ktt_ref[0, 0, 0], st_scr)
            softmax_pv(bufs[j % 2], v_ref[0, 0, j * tk:(j + 1) * tk, :])
        scores(jnp.minimum(qi + 1, n_q - 1), kt_ref[0, 0, 0], bufs[0])
        softmax_pv(st_scr, vt_ref[0, 0])
        o_rows = pl.ds(pl.multiple_of(qi * tq, tq), tq)
        o_ref[0, o_rows, :] = (acc_scr[:, 0:MLA_V] / acc_scr[:, MLA_V:MLA_V_EXT]).astype(BF16)
        reset()
        return carry

    lax.fori_loop(0, n_q, query_block, 0)


def _attention_pipelined(q, kt, v, kt_tail, v_tail, *, tq, n_sub):
    b, h, t, dqk = q.shape
    tk, tt = kt.shape[4], kt_tail.shape[4]
    assert kt_tail.shape[2] == 1
    return pl.pallas_call(
        functools.partial(_attn_pipe_kernel, n_sub=n_sub),
        grid=(b, h),
        in_specs=[
            pl.BlockSpec((1, 1, t, dqk), lambda bi, hi: (bi, hi, 0, 0)),
            pl.BlockSpec((1, 1) + kt.shape[2:], lambda bi, hi: (bi, hi, 0, 0, 0)),
            pl.BlockSpec((1, 1) + v.shape[2:], lambda bi, hi: (bi, hi, 0, 0)),
            pl.BlockSpec((1, 1) + kt_tail.shape[2:], lambda bi, hi: (bi, hi, 0, 0, 0)),
            pl.BlockSpec((1, 1) + v_tail.shape[2:], lambda bi, hi: (bi, hi, 0, 0)),
        ],
        out_specs=pl.BlockSpec((1, t, MLA_V), lambda bi, hi: (bi, 0, hi)),
        out_shape=jax.ShapeDtypeStruct((b, t, h * MLA_V), BF16),
        scratch_shapes=[pltpu.VMEM((tq, LANES), F32), pltpu.VMEM((tq, MLA_V_EXT), F32),
                        pltpu.VMEM((tq, tk), F32), pltpu.VMEM((tq, tk), F32),
                        pltpu.VMEM((tq, tt), F32)],
        compiler_params=_cparams(("arbitrary", "arbitrary")),
        name="mla_attention_pipe",
    )(q, kt, v, kt_tail, v_tail)


def _attention(q, segs, *, tq, n_sub):
    b, h, t, dqk = q.shape
    in_specs = [pl.BlockSpec((1, 1, tq, dqk), lambda bi, hi, qi: (bi, hi, qi, 0))]
    args = [q]
    for kt, v in segs:
        in_specs.append(pl.BlockSpec((1, 1) + kt.shape[2:], lambda bi, hi, qi: (bi, hi, 0, 0, 0)))
        in_specs.append(pl.BlockSpec((1, 1) + v.shape[2:], lambda bi, hi, qi: (bi, hi, 0, 0)))
        args += [kt, v]
    return pl.pallas_call(
        functools.partial(_attn_kernel, n_seg=len(segs), n_sub=n_sub),
        grid=(b, h, t // tq),
        in_specs=in_specs,
        out_specs=pl.BlockSpec((1, tq, MLA_V), lambda bi, hi, qi: (bi, qi, hi)),
        out_shape=jax.ShapeDtypeStruct((b, t, h * MLA_V), BF16),
        scratch_shapes=[pltpu.VMEM((tq, LANES), F32), pltpu.VMEM((tq, MLA_V_EXT), F32)],
        compiler_params=_cparams(("arbitrary", "arbitrary", "arbitrary")),
        name="mla_attention",
    )(*args)


def _mix_residual_norm(x_ref, of_ref, ob_ref, r_ref, mla_ref, m, gg_ref, wo_ref, ln2_ref,
                       rows=slice(None)):
    o = of_ref[rows, :].astype(F32) + ob_ref[rows, :].astype(F32)
    gg = gg_ref[...]
    y = jnp.concatenate(
        [_rms(o[:, h * GLA_DV:(h + 1) * GLA_DV], gg) for h in range(GLA_HEADS)], axis=1)
    mix = (y * _silu(r_ref[rows, :].astype(F32))).astype(BF16)
    yo = _dot(mix, wo_ref[0:GLA_V_W, :]) + _dot(mla_ref[rows, :], wo_ref[GLA_V_W:, :])
    x1 = x_ref[rows, :] + m[2:3] * yo
    h2 = _rms(x1, ln2_ref[...]) * (1.0 + m[4:5]) + m[3:4]
    return x1, h2


def _outproj_ffn_kernel(*refs, groups, final_norm):
    (x_ref, of_ref, ob_ref, r_ref, mla_ref, mod_ref, gg_ref, wo_ref, ln2_ref,
     wg_ref, wu_ref, wd_ref) = refs[:12]
    fin_ref = refs[12] if final_norm else None
    o_ref = refs[-1]
    m = mod_ref[0]
    x1, h2 = _mix_residual_norm(x_ref, of_ref, ob_ref, r_ref, mla_ref, m, gg_ref, wo_ref, ln2_ref)
    h = h2.astype(BF16)
    y = None
    for lo, hi in groups:
        a = _dot(h, wg_ref[:, lo:hi])
        u = _dot(h, wu_ref[:, lo:hi])
        part = _dot((_silu(a) * u).astype(BF16), wd_ref[lo:hi, :])
        y = part if y is None else y + part
    x2 = x1 + m[5:6] * y
    if final_norm:
        x2 = _rms(x2, fin_ref[...])
    o_ref[...] = x2


def _outproj_ffn(x, o_f, o_b, p, mla, mods, row_fn, gg, wo, ln2, wg, wu, wd, fin_g, tm):
    t, d = x.shape
    ff = wg.shape[1]
    final_norm = fin_g is not None
    step = min(ff, DENSE_FF_GROUP)
    groups = tuple((lo, min(lo + step, ff)) for lo in range(0, ff, step))
    resident = lambda a: pl.BlockSpec(a.shape, lambda i: (0,) * a.ndim, pipeline_mode=pl.Buffered(1))
    in_specs = [
        pl.BlockSpec((tm, d), lambda i: (i, 0)),
        pl.BlockSpec((tm, GLA_V_W), lambda i: (i, 0)),
        pl.BlockSpec((tm, GLA_V_W), lambda i: (i, 0)),
        pl.BlockSpec((tm, GLA_V_W), lambda i: (i, P_R // GLA_V_W)),
        pl.BlockSpec((tm, MLA_V_W), lambda i: (i, 0)),
        pl.BlockSpec((1, 6, d), lambda i: (row_fn(i), 0, 0)),
        resident(gg), resident(wo), resident(ln2), resident(wg), resident(wu), resident(wd),
    ]
    args = [x, o_f, o_b, p, mla, mods, gg, wo, ln2, wg, wu, wd]
    if final_norm:
        in_specs.append(resident(fin_g))
        args.append(fin_g)
    return pl.pallas_call(
        functools.partial(_outproj_ffn_kernel, groups=groups, final_norm=final_norm),
        grid=(t // tm,),
        in_specs=in_specs,
        out_specs=pl.BlockSpec((tm, d), lambda i: (i, 0)),
        out_shape=jax.ShapeDtypeStruct((t, d), F32),
        compiler_params=_cparams(("arbitrary",)),
        name="outproj_ffn",
    )(*args)


def _outproj_kernel(*refs, with_router):
    (x_ref, of_ref, ob_ref, r_ref, mla_ref, mod_ref, gg_ref, wo_ref, ln2_ref) = refs[:9]
    if with_router:
        rwh_ref, rwl_ref, x1_ref, h2_ref, comb_ref = refs[9:]
    else:
        x1_ref, h2_ref = refs[9:]
    m = mod_ref[0]
    tm = x_ref.shape[0]
    n_groups = 2 if tm % 32 == 0 else 1
    for g in range(n_groups):
        rows = slice(g * tm // n_groups, (g + 1) * tm // n_groups)
        x1, h2 = _mix_residual_norm(x_ref, of_ref, ob_ref, r_ref, mla_ref, m, gg_ref, wo_ref,
                                    ln2_ref, rows)
        x1_ref[rows, :] = x1
        if not with_router:
            h2_ref[rows, :] = h2.astype(BF16)
            continue
        h2_ref[rows, :] = _pack_bf16_pairs(h2)
        h_hi = h2.astype(BF16)
        h_lo = (h2 - h_hi.astype(F32)).astype(BF16)
        logits = _dot(h_hi, rwh_ref[...]) + _dot(h_lo, rwh_ref[...]) + _dot(h_hi, rwl_ref[...])
        lane = lax.broadcasted_iota(jnp.int32, logits.shape, 1).astype(F32)
        neg = jnp.float32(-jnp.inf)
        logits = jnp.where(lane < N_EXPERTS, logits, neg)
        m1 = jnp.max(logits, axis=1, keepdims=True)
        i1 = jnp.min(jnp.where(logits == m1, lane, float(LANES)), axis=1, keepdims=True)
        rest = jnp.where(lane == i1, neg, logits)
        m2 = jnp.max(rest, axis=1, keepdims=True)
        i2 = jnp.min(jnp.where(rest == m2, lane, float(LANES)), axis=1, keepdims=True)
        e2 = jnp.exp(m2 - m1)
        w1 = 1.0 / (1.0 + e2)
        comb_ref[rows, :] = (jnp.where(lane == ROUTE_E1, i1, 0.0) + jnp.where(lane == ROUTE_E2, i2, 0.0)
                             + jnp.where(lane == ROUTE_W1, w1, 0.0)
                             + jnp.where(lane == ROUTE_W2, e2 * w1, 0.0))


def _outproj(x, o_f, o_b, p, mla, mods, row_fn, gg, wo, ln2, router, tm):
    t, d = x.shape
    with_router = router is not None
    full = lambda a: pl.BlockSpec(a.shape, lambda i: (0,) * a.ndim)
    in_specs = [
        pl.BlockSpec((tm, d), lambda i: (i, 0)),
        pl.BlockSpec((tm, GLA_V_W), lambda i: (i, 0)),
        pl.BlockSpec((tm, GLA_V_W), lambda i: (i, 0)),
        pl.BlockSpec((tm, GLA_V_W), lambda i: (i, P_R // GLA_V_W)),
        pl.BlockSpec((tm, MLA_V_W), lambda i: (i, 0)),
        pl.BlockSpec((1, 6, d), lambda i: (row_fn(i), 0, 0)),
        full(gg), full(wo), full(ln2),
    ]
    args = [x, o_f, o_b, p, mla, mods, gg, wo, ln2]
    h2_shape = jax.ShapeDtypeStruct((t, d // 2), F32) if with_router else jax.ShapeDtypeStruct((t, d), BF16)
    out_specs = [pl.BlockSpec((tm, d), lambda i: (i, 0)),
                 pl.BlockSpec((tm, h2_shape.shape[1]), lambda i: (i, 0))]
    out_shape = [jax.ShapeDtypeStruct((t, d), F32), h2_shape]
    if with_router:
        in_specs += [full(router[0]), full(router[1])]
        args += list(router)
        out_specs.append(pl.BlockSpec((tm, LANES), lambda i: (i, 0)))
        out_shape.append(jax.ShapeDtypeStruct((t, LANES), F32))
    return pl.pallas_call(
        functools.partial(_outproj_kernel, with_router=with_router),
        grid=(t // tm,),
        in_specs=in_specs,
        out_specs=out_specs,
        out_shape=out_shape,
        compiler_params=_cparams(("arbitrary",)),
        name="outproj",
    )(*args)


def _ffn_kernel(*refs, with_comb, final_norm):
    h_ref, x1_ref, mod_ref = refs[:3]
    k = 3
    comb_ref = fin_ref = None
    if with_comb:
        comb_ref = refs[k]
        k += 1
    wg_ref, wu_ref, wd_ref = refs[k:k + 3]
    k += 3
    if final_norm:
        fin_ref = refs[k]
        k += 1
    o_ref, acc = refs[k:]
    e = pl.program_id(1)
    f = pl.program_id(2)

    @pl.when((e == 0) & (f == 0))
    def _():
        acc[...] = jnp.zeros(acc.shape, F32)

    h = h_ref[...]
    a = _dot(h, wg_ref[0])
    u = _dot(h, wu_ref[0])
    act = _silu(a) * u
    if with_comb:
        comb = comb_ref[...]
        lane = lax.broadcasted_iota(jnp.int32, comb.shape, 1)
        act = act * jnp.sum(jnp.where(lane == e, comb, 0.0), axis=1, keepdims=True)
    acc[...] += _dot(act.astype(BF16), wd_ref[0])

    @pl.when((e == pl.num_programs(1) - 1) & (f == pl.num_programs(2) - 1))
    def _():
        x2 = x1_ref[...] + mod_ref[0][5:6] * acc[...]
        if final_norm:
            x2 = _rms(x2, fin_ref[...])
        o_ref[...] = x2


def _ffn(h2, x1, mods, row_fn, comb, wg, wu, wd, fin_g, tm, tf):
    t, d = x1.shape
    n_e, _, ff = wg.shape
    with_comb = comb is not None
    final_norm = fin_g is not None
    in_specs = [
        pl.BlockSpec((tm, d), lambda i, e, f: (i, 0)),
        pl.BlockSpec((tm, d), lambda i, e, f: (i, 0)),
        pl.BlockSpec((1, 6, d), lambda i, e, f: (row_fn(i), 0, 0)),
    ]
    args = [h2, x1, mods]
    if with_comb:
        in_specs.append(pl.BlockSpec((tm, LANES), lambda i, e, f: (i, 0)))
        args.append(comb)
    in_specs += [
        pl.BlockSpec((1, d, tf), lambda i, e, f: (e, 0, f)),
        pl.BlockSpec((1, d, tf), lambda i, e, f: (e, 0, f)),
        pl.BlockSpec((1, tf, d), lambda i, e, f: (e, f, 0)),
    ]
    args += [wg, wu, wd]
    if final_norm:
        in_specs.append(pl.BlockSpec((1, d), lambda i, e, f: (0, 0)))
        args.append(fin_g)
    return pl.pallas_call(
        functools.partial(_ffn_kernel, with_comb=with_comb, final_norm=final_norm),
        grid=(t // tm, n_e, ff // tf),
        in_specs=in_specs,
        out_specs=pl.BlockSpec((tm, d), lambda i, e, f: (i, 0)),
        out_shape=jax.ShapeDtypeStruct((t, d), F32),
        scratch_shapes=[pltpu.VMEM((tm, d), F32)],
        compiler_params=_cparams(("arbitrary", "arbitrary", "arbitrary")),
        name="ffn",
    )(*args)


def _sc_row_gather(table, idx):
    _, w = table.shape
    b = idx.shape[0]
    n_workers = SC_CORES * SC_SUBCORES
    assert b % (n_workers * SC_GATHER_ROWS) == 0, (b, n_workers, SC_GATHER_ROWS)
    b_per_w = b // n_workers
    n_chunks = b_per_w // SC_GATHER_ROWS
    mesh = plsc.VectorSubcoreMesh(core_axis_name="c", subcore_axis_name="s",
                                  num_cores=SC_CORES, num_subcores=SC_SUBCORES)

    def body(table_hbm, idx_hbm, out_hbm, idx_a, idx_b, rows_a, rows_b, sem_a, sem_b):
        wid = lax.axis_index("s") * SC_CORES + lax.axis_index("c")
        base = wid * b_per_w
        idx_bufs, row_bufs, sems = (idx_a, idx_b), (rows_a, rows_b), (sem_a, sem_b)

        def start(ci):
            slot = ci % 2
            pltpu.sync_copy(idx_hbm.at[pl.ds(base + ci * SC_GATHER_ROWS, SC_GATHER_ROWS)], idx_bufs[slot])
            return pltpu.async_copy(table_hbm.at[idx_bufs[slot]], row_bufs[slot], sems[slot])

        pending = start(0)
        for ci in range(n_chunks):
            following = start(ci + 1) if ci + 1 < n_chunks else None
            pending.wait()
            pltpu.sync_copy(row_bufs[ci % 2], out_hbm.at[pl.ds(base + ci * SC_GATHER_ROWS, SC_GATHER_ROWS)])
            pending = following

    return pl.kernel(
        body,
        out_type=jax.ShapeDtypeStruct((b, w), F32),
        mesh=mesh,
        scratch_types=[pltpu.VMEM((SC_GATHER_ROWS,), jnp.int32)] * 2
        + [pltpu.VMEM((SC_GATHER_ROWS, w), F32)] * 2 + [pltpu.SemaphoreType.DMA] * 2,
        name="sc_row_gather",
    )(table, idx)


def _sc_row_scatter2(table, pos, n_out):
    t, w = table.shape
    n_workers = SC_CORES * SC_SUBCORES
    assert t % (n_workers * SC_GATHER_ROWS) == 0, (t, n_workers, SC_GATHER_ROWS)
    t_per_w = t // n_workers
    n_chunks = t_per_w // SC_GATHER_ROWS
    mesh = plsc.VectorSubcoreMesh(core_axis_name="c", subcore_axis_name="s",
                                  num_cores=SC_CORES, num_subcores=SC_SUBCORES)

    def body(table_hbm, pos_hbm, out_hbm, i0a, i1a, i0b, i1b, rows_a, rows_b, sem_a, sem_b):
        wid = lax.axis_index("s") * SC_CORES + lax.axis_index("c")
        base = wid * t_per_w
        idx0, idx1, row_bufs, sems = (i0a, i0b), (i1a, i1b), (rows_a, rows_b), (sem_a, sem_b)

        def drain(pair):
            if pair is not None:
                pair[0].wait()
                pair[1].wait()

        pending = [None, None]
        for ci in range(n_chunks):
            slot = ci % 2
            drain(pending[slot])
            off = base + ci * SC_GATHER_ROWS
            pltpu.sync_copy(pos_hbm.at[pl.ds(off, SC_GATHER_ROWS)], idx0[slot])
            pltpu.sync_copy(pos_hbm.at[pl.ds(t + off, SC_GATHER_ROWS)], idx1[slot])
            pltpu.sync_copy(table_hbm.at[pl.ds(off, SC_GATHER_ROWS)], row_bufs[slot])
            pending[slot] = (pltpu.async_copy(row_bufs[slot], out_hbm.at[idx0[slot]], sems[slot]),
                             pltpu.async_copy(row_bufs[slot], out_hbm.at[idx1[slot]], sems[slot]))
        drain(pending[0])
        drain(pending[1])

    return pl.kernel(
        body,
        out_type=jax.ShapeDtypeStruct((n_out, w), F32),
        mesh=mesh,
        scratch_types=[pltpu.VMEM((SC_GATHER_ROWS,), jnp.int32)] * 4
        + [pltpu.VMEM((SC_GATHER_ROWS, w), F32)] * 2 + [pltpu.SemaphoreType.DMA] * 2,
        name="sc_row_scatter",
    )(table, pos)


def _moe_plan_kernel(route_ref, tri_ref, utri_ref, pos_ref, cnt_ref, run_scr, off_scr):
    phase = pl.program_id(0)
    blk = pl.program_id(1)
    route = route_ref[...]
    lane = lax.broadcasted_iota(jnp.int32, route.shape, 1).astype(F32)
    oh1 = jnp.where(lane == route[:, ROUTE_E1:ROUTE_E1 + 1], 1.0, 0.0)
    oh2 = jnp.where(lane == route[:, ROUTE_E2:ROUTE_E2 + 1], 1.0, 0.0)
    oh = oh1 + oh2

    @pl.when(blk == 0)
    def _():
        run_scr[...] = jnp.zeros(run_scr.shape, F32)

    @pl.when(phase == 0)
    def _():
        run_scr[...] += jnp.sum(oh, axis=0, keepdims=True)

        @pl.when(blk == pl.num_programs(1) - 1)
        def _():
            counts = run_scr[...]
            cnt_ref[...] = counts
            tiles_per = jnp.floor((counts + (MOE_TILE - 1.0)) * (1.0 / MOE_TILE))
            tile_end = _dot(jnp.broadcast_to(tiles_per, (8, LANES)).astype(BF16), utri_ref[...])[0:1]
            off_scr[...] = (tile_end - tiles_per) * float(MOE_TILE)

    @pl.when(phase == 1)
    def _():
        incl = _dot(tri_ref[...], oh.astype(BF16))
        before = incl - oh + run_scr[...] + off_scr[...]
        p1 = jnp.sum(before * oh1, axis=1, keepdims=True)
        p2 = jnp.sum(before * oh2, axis=1, keepdims=True)
        pos = jnp.where(lane == 0.0, p1, 0.0) + jnp.where(lane == 1.0, p2, 0.0)
        pos_ref[...] = pos.astype(jnp.int32)
        run_scr[...] += incl[incl.shape[0] - 1:, :]


def _moe_plan(route, n_tiles, tm):
    t = route.shape[0]
    tri = jnp.asarray(np.tril(np.ones((tm, tm), np.float32)), dtype=BF16)
    utri = jnp.asarray(np.triu(np.ones((LANES, LANES), np.float32)), dtype=BF16)
    pos, counts = pl.pallas_call(
        _moe_plan_kernel,
        grid=(2, t // tm),
        in_specs=[
            pl.BlockSpec((tm, LANES), lambda p, i: (i, 0)),
            pl.BlockSpec((tm, tm), lambda p, i: (0, 0)),
            pl.BlockSpec((LANES, LANES), lambda p, i: (0, 0)),
        ],
        out_specs=[
            pl.BlockSpec((tm, LANES), lambda p, i: (i * p, 0)),
            pl.BlockSpec((1, LANES), lambda p, i: (0, 0)),
        ],
        out_shape=[jax.ShapeDtypeStruct((t, LANES), jnp.int32),
                   jax.ShapeDtypeStruct((1, LANES), F32)],
        scratch_shapes=[pltpu.VMEM((1, LANES), F32), pltpu.VMEM((1, LANES), F32)],
        compiler_params=_cparams(("arbitrary", "arbitrary")),
        name="moe_plan",
    )(route, tri, utri)
    counts = counts[0, :N_EXPERTS].astype(jnp.int32)
    tile_end = jnp.cumsum((counts + MOE_TILE - 1) // MOE_TILE)
    n_used = tile_end[-1]
    tile_ids = jnp.minimum(jnp.arange(n_tiles, dtype=jnp.int32), n_used - 1)
    tile_expert = jnp.sum((tile_end[None, :] <= tile_ids[:, None]).astype(jnp.int32), axis=1)
    return pos[:, :2].T, tile_expert, n_used.reshape(1)


def _moe_ffn_kernel(te_ref, nused_ref, xs_ref, wg_ref, wu_ref, wd_ref, o_ref, acc, h_scr, *, splits):
    i = pl.program_id(0)
    f = pl.program_id(1)
    last_f = pl.num_programs(1) - 1
    used = i < nused_ref[0]

    @pl.when(used)
    def _():
        @pl.when(f == 0)
        def _():
            acc[...] = jnp.zeros(acc.shape, F32)
            h_scr[...] = _unpack_bf16_pairs(xs_ref[...]).astype(BF16)

        h = h_scr[...]
        for lo, hi in splits:
            a = _dot(h, wg_ref[0, :, lo:hi].astype(BF16))
            u = _dot(h, wu_ref[0, :, lo:hi].astype(BF16))
            acc[...] += _dot((_silu(a) * u).astype(BF16), wd_ref[0, lo:hi, :].astype(BF16))

        @pl.when(f == last_f)
        def _():
            o_ref[...] = _pack_bf16_pairs(acc[...])

    @pl.when(jnp.logical_not(used) & (f == last_f))
    def _():
        o_ref[...] = jnp.zeros(o_ref.shape, F32)


def _moe_ffn(xs, tile_expert, n_used, wg, wu, wd):
    rows, half = xs.shape
    d = 2 * half
    ff = wg.shape[2]
    if ff % MOE_FF_TILE == 0:
        tf, splits = MOE_FF_TILE, MOE_FF_SPLITS
    else:
        tf, splits = ff, ((0, ff),)
    n_tiles = rows // MOE_TILE
    grid_spec = pltpu.PrefetchScalarGridSpec(
        num_scalar_prefetch=2,
        grid=(n_tiles, ff // tf),
        in_specs=[
            pl.BlockSpec((MOE_TILE, half), lambda i, f, te, nu: (i, 0)),
            pl.BlockSpec((1, d, tf), lambda i, f, te, nu: (te[i], 0, f)),
            pl.BlockSpec((1, d, tf), lambda i, f, te, nu: (te[i], 0, f)),
            pl.BlockSpec((1, tf, d), lambda i, f, te, nu: (te[i], f, 0)),
        ],
        out_specs=pl.BlockSpec((MOE_TILE, half), lambda i, f, te, nu: (i, 0)),
        scratch_shapes=[pltpu.VMEM((MOE_TILE, d), F32), pltpu.VMEM((MOE_TILE, d), BF16)],
    )
    return pl.pallas_call(
        functools.partial(_moe_ffn_kernel, splits=splits),
        grid_spec=grid_spec,
        out_shape=jax.ShapeDtypeStruct((rows, half), F32),
        compiler_params=_cparams(("arbitrary", "arbitrary")),
        name="moe_ffn",
    )(tile_expert, n_used, xs, wg, wu, wd)


def _combine_kernel(*refs, final_norm):
    x1_ref, y0_ref, y1_ref, route_ref, mod_ref = refs[:5]
    fin_ref = refs[5] if final_norm else None
    o_ref = refs[-1]
    route = route_ref[...]
    w1 = route[:, ROUTE_W1:ROUTE_W1 + 1]
    w2 = route[:, ROUTE_W2:ROUTE_W2 + 1]
    y = w1 * _unpack_bf16_pairs(y0_ref[...]) + w2 * _unpack_bf16_pairs(y1_ref[...])
    x2 = x1_ref[...] + mod_ref[0][5:6] * y
    if final_norm:
        x2 = _rms(x2, fin_ref[...])
    o_ref[...] = x2


def _combine(x1, yg, route, mods, row_fn, fin_g, tm):
    t, d = x1.shape
    nb = t // tm
    final_norm = fin_g is not None
    in_specs = [
        pl.BlockSpec((tm, d), lambda i: (i, 0)),
        pl.BlockSpec((tm, d // 2), lambda i: (i, 0)),
        pl.BlockSpec((tm, d // 2), lambda i: (i + nb, 0)),
        pl.BlockSpec((tm, LANES), lambda i: (i, 0)),
        pl.BlockSpec((1, 6, d), lambda i: (row_fn(i), 0, 0)),
    ]
    args = [x1, yg, yg, route, mods]
    if final_norm:
        in_specs.append(pl.BlockSpec((1, d), lambda i: (0, 0)))
        args.append(fin_g)
    return pl.pallas_call(
        functools.partial(_combine_kernel, final_norm=final_norm),
        grid=(nb,),
        in_specs=in_specs,
        out_specs=pl.BlockSpec((tm, d), lambda i: (i, 0)),
        out_shape=jax.ShapeDtypeStruct((t, d), F32),
        compiler_params=_cparams(("arbitrary",)),
        name="moe_combine",
    )(*args)


def _moe(h2, x1, route, mods, row_fn, wg, wu, wd, fin_g, tm):
    t = h2.shape[0]
    n_tiles = -(-2 * t // MOE_TILE) + N_EXPERTS
    pos, tile_expert, n_used = _moe_plan(route, n_tiles, _pick_tile(t, ATTN_TILE))
    pos = pos.reshape(-1)
    xs = _sc_row_scatter2(h2, pos, n_tiles * MOE_TILE)
    ys = _moe_ffn(xs, tile_expert, n_used, wg, wu, wd)
    yg = _sc_row_gather(ys, pos)
    return _combine(x1, yg, route, mods, row_fn, fin_g, tm)


def _rope_partner():
    j = np.arange(MLA_ROPE)
    return np.where((j % 32) < 16, j + 16, j - 16)


def _prep_in_weight(w):
    d = w.shape[0]
    widths = (GLA_QK_W, GLA_QK_W, GLA_V_W, GLA_GATE_RANK, GLA_GATE_RANK, GLA_V_W,
              MLA_Q_RANK, MLA_KV_RANK, MLA_ROPE)
    offs = np.concatenate([[0], np.cumsum(widths)])
    part = lambda i: w[:, offs[i]:offs[i + 1]]
    assert w.shape[1] == offs[-1] and P_MISC + MISC_GB + GLA_GATE_RANK <= P_WIDTH
    cols = [part(0), part(1), part(2), part(5), part(6), part(7), part(8), part(3), part(4),
            jnp.zeros((d, P_WIDTH - int(offs[-1])), w.dtype)]
    return jnp.concatenate(cols, axis=1).astype(BF16)


def _prep_gate_weight(w_g2, b_g2):
    ws = []
    for z, off in ((0, MISC_GF), (1, MISC_GB)):
        ws.append(jnp.zeros((LANES, GLA_QK_W), F32).at[off:off + GLA_GATE_RANK].set(w_g2[z]))
    return jnp.stack(ws).astype(BF16), b_g2.reshape(2, 1, GLA_QK_W)


def _prep_mla_weights(w_uq, w_ukv):
    partner = _rope_partner()
    wq = w_uq.reshape(MLA_Q_RANK, MLA_HEADS, MLA_QK)
    wqn = wq[:, :, :MLA_NOPE].reshape(MLA_Q_RANK, MLA_HEADS * MLA_NOPE)
    rope = wq[:, :, MLA_NOPE:]
    pad = jnp.zeros((MLA_Q_RANK, MLA_HEADS, LANES - MLA_ROPE), w_uq.dtype)
    wqr = jnp.concatenate([rope, pad], axis=2).reshape(MLA_Q_RANK, MLA_HEADS * LANES)
    wqs = jnp.concatenate([rope[:, :, partner], pad], axis=2).reshape(MLA_Q_RANK, MLA_HEADS * LANES)
    wkv = w_ukv.reshape(MLA_KV_RANK, MLA_HEADS, MLA_NOPE + MLA_V)
    wknt = wkv[:, :, :MLA_NOPE].reshape(MLA_KV_RANK, MLA_HEADS * MLA_NOPE).T
    wv = wkv[:, :, MLA_NOPE:].reshape(MLA_KV_RANK, MLA_HEADS * MLA_V)
    perm = np.zeros((LANES, LANES), np.float32)
    perm[partner, np.arange(MLA_ROPE)] = 1.0
    eye = np.eye(MLA_ROPE, LANES, dtype=np.float32)
    return (wqn.astype(BF16), wqr.astype(BF16), wqs.astype(BF16), wknt.astype(BF16),
            wv.astype(BF16), jnp.asarray(perm, BF16), jnp.asarray(eye, BF16))


def _rope_tables(n_tok):
    rows = n_tok // GRID_W
    row = np.repeat(np.arange(rows, dtype=np.float32), GRID_W)
    col = np.tile(np.arange(GRID_W, dtype=np.float32), rows)
    nfreq = MLA_ROPE // 4
    inv = np.float32(ROPE_BASE) ** (-np.arange(nfreq, dtype=np.float32) / np.float32(nfreq))
    ar = (row[:, None] * inv).astype(np.float32)
    ac = (col[:, None] * inv).astype(np.float32)
    zero = np.zeros((n_tok, LANES - MLA_ROPE), np.float32)
    cos = np.concatenate([np.cos(ar), np.cos(ar), np.cos(ac), np.cos(ac), zero], axis=1)
    sin = np.concatenate([-np.sin(ar), np.sin(ar), -np.sin(ac), np.sin(ac), zero], axis=1)
    return jnp.asarray(cos, F32), jnp.asarray(sin, F32)


def _identity_tables(n_tok):
    cos = jnp.concatenate([jnp.ones((n_tok, MLA_ROPE), F32),
                           jnp.zeros((n_tok, LANES - MLA_ROPE), F32)], axis=1)
    return cos, jnp.zeros((n_tok, LANES), F32)


def _pick_tile(n, pref):
    t = min(n, pref)
    while n % t:
        t //= 2
    return t


def _pick_ff_tile(ff):
    best = LANES
    for m in range(1, ff // LANES + 1):
        if ff % (m * LANES) == 0 and m * LANES <= FFN_MAX_FF_TILE:
            best = m * LANES
    return best


@jax.jit
def _forward(x, c, ctx, c_ctx, w_mod, b_mod, ln1_g, ln2_g, w_in, w_gla_g2, b_gla_g2, gla_norm_g,
             mla_q_norm_g, w_uq, mla_kv_norm_g, w_ukv, w_out, ffn_w_gate, ffn_w_up, ffn_w_down,
             router_w, exp_w_gate, exp_w_up, exp_w_down, final_norm_g):
    batch, seq, d = x.shape
    n_ctx = ctx.shape[1]
    depth = w_mod.shape[0]
    assert d == D_MODEL and batch < 8, (d, batch)

    cvec = jnp.zeros((8, d), F32).at[:batch].set(c).at[batch].set(c_ctx)
    mods_all = _modulation(cvec, w_mod, b_mod).reshape(depth, 8, 6, d)

    xl = x.reshape(batch * seq, d)
    xc = ctx.reshape(batch * n_ctx, d)

    tm_l = _pick_tile(seq, TOKEN_TILE)
    tm_c = _pick_tile(n_ctx, CTX_TILE)
    tk_l = _pick_tile(seq, ATTN_TILE)
    cb_l = _pick_tile(seq, GLA_BLOCK)
    cb_c = _pick_tile(n_ctx, GLA_BLOCK)
    row_l = lambda tm: (lambda i: i // (seq // tm))
    row_c = lambda i: batch

    rope_l = _rope_tables(seq)
    rope_c = _identity_tables(tm_c)
    zero_state = jnp.zeros((2, batch, GLA_QK_W, GLA_DV), F32)

    for i in range(depth):
        need_ctx = i < depth - 1
        last = i == depth - 1
        mods = mods_all[i]
        ln1 = ln1_g[i].reshape(1, d)
        ln2 = ln2_g[i].reshape(1, d)
        w_in_r = _prep_in_weight(w_in[i])
        gates = _prep_gate_weight(w_gla_g2[i], b_gla_g2[i])
        mla_w = _prep_mla_weights(w_uq[i], w_ukv[i])
        qg = mla_q_norm_g[i].reshape(1, MLA_Q_RANK)
        kvg = mla_kv_norm_g[i].reshape(1, MLA_KV_RANK)
        gg = gla_norm_g[i].reshape(1, GLA_DV)
        wo = w_out[i].astype(BF16)

        p_l, gate_l = _inproj(xl, mods, row_l(tk_l), ln1, w_in_r, *gates, tk_l, cb_l)
        p_c, gate_c = _inproj(xc, mods, row_c, ln1, w_in_r, *gates, tm_c, cb_c)

        oc_f, oc_b, s_ctx = _gla(p_c, gate_c, *gates, zero_state, batch=batch, cb=cb_c)
        ol_f, ol_b, _ = _gla(p_l, gate_l, *gates, s_ctx, batch=batch, cb=cb_l)

        q_l, kt_l, v_l = _mlaprep(p_l, *rope_l, qg, kvg, mla_w, batch=batch, tm=tk_l)
        q_c, kt_c, v_c = _mlaprep(p_c, *rope_c, qg, kvg, mla_w, batch=batch, tm=tm_c)
        m_l = _attention_pipelined(q_l, kt_l, v_l, kt_c, v_c, tq=tk_l, n_sub=1)
        m_l = m_l.reshape(batch * seq, MLA_V_W)

        if i % 2 == 0:
            j = i // 2
            router = None
            wg = ffn_w_gate[j].astype(BF16)
            wu = ffn_w_up[j].astype(BF16)
            wd = ffn_w_down[j].astype(BF16)
        else:
            j = i // 2
            rw = jnp.zeros((d, LANES), F32).at[:, :N_EXPERTS].set(router_w[j])
            rw_hi = rw.astype(BF16)
            router = (rw_hi, (rw - rw_hi.astype(F32)).astype(BF16))
            wg, wu, wd = exp_w_gate[j], exp_w_up[j], exp_w_down[j]
        fin = final_norm_g.reshape(1, d) if last else None

        if router is None:
            xl = _outproj_ffn(xl, ol_f, ol_b, p_l, m_l, mods, row_l(tm_l), gg, wo, ln2,
                              wg, wu, wd, fin, tm_l)
        else:
            outs = _outproj(xl, ol_f, ol_b, p_l, m_l, mods, row_l(tm_l), gg, wo, ln2, router, tm_l)
            xl = _moe(outs[1], outs[0], outs[2], mods, row_l(tk_l), wg, wu, wd, fin, tk_l)

        if need_ctx:
            m_c = _attention(q_c, [(kt_c, v_c)], tq=tm_c, n_sub=1).reshape(batch * n_ctx, MLA_V_W)
            if router is None:
                xc = _outproj_ffn(xc, oc_f, oc_b, p_c, m_c, mods, row_c, gg, wo, ln2,
                                  wg, wu, wd, None, _pick_tile(batch * n_ctx, TOKEN_TILE))
            else:
                outs_c = _outproj(xc, oc_f, oc_b, p_c, m_c, mods, row_c, gg, wo, ln2, router, tm_c)
                tm_fc = _pick_tile(batch * n_ctx, TOKEN_TILE)
                r_c = outs_c[2]
                lane = jnp.arange(LANES, dtype=F32)[None, :]
                comb_c = (jnp.where(lane == r_c[:, ROUTE_E1:ROUTE_E1 + 1], r_c[:, ROUTE_W1:ROUTE_W1 + 1], 0.0)
                          + jnp.where(lane == r_c[:, ROUTE_E2:ROUTE_E2 + 1], r_c[:, ROUTE_W2:ROUTE_W2 + 1], 0.0))
                bits = lax.bitcast_convert_type(outs_c[1], jnp.uint32)
                h2_c = jnp.concatenate([lax.bitcast_convert_type(bits << 16, F32),
                                        lax.bitcast_convert_type(bits & jnp.uint32(0xFFFF0000), F32)], axis=1)
                xc = _ffn(h2_c.astype(BF16), outs_c[0], mods, row_c, comb_c,
                          wg.astype(BF16), wu.astype(BF16), wd.astype(BF16), None, tm_fc,
                          _pick_ff_tile(wg.shape[2]))

    return xl.reshape(batch, seq, d)


def kernel(x, c, ctx, c_ctx, w_mod, b_mod, ln1_g, ln2_g, w_in, w_gla_g2, b_gla_g2, gla_norm_g,
           mla_q_norm_g, w_uq, mla_kv_norm_g, w_ukv, w_out, ffn_w_gate, ffn_w_up, ffn_w_down,
           router_w, exp_w_gate, exp_w_up, exp_w_down, final_norm_g):
    return _forward(x, c, ctx, c_ctx, w_mod, b_mod, ln1_g, ln2_g, w_in, w_gla_g2, b_gla_g2,
                    gla_norm_g, mla_q_norm_g, w_uq, mla_kv_norm_g, w_ukv, w_out, ffn_w_gate,
                    ffn_w_up, ffn_w_down, router_w, exp_w_gate, exp_w_up, exp_w_down, final_norm_g)
```

```python
import functools

import numpy as np
import jax
import jax.numpy as jnp
from jax import lax
from jax.experimental import pallas as pl
from jax.experimental.pallas import tpu as pltpu
from jax.experimental.pallas import tpu_sc as plsc

F32 = jnp.float32
BF16 = jnp.bfloat16

D_MODEL = 1024
EPS = 1e-6
GRID_W = 64

GLA_HEADS = 4
GLA_DK = 64
GLA_DV = 128
GLA_GATE_RANK = 16
GLA_GATE_NORM = 16.0
GLA_CHUNK = 64
GLA_QK_W = GLA_HEADS * GLA_DK
GLA_V_W = GLA_HEADS * GLA_DV
GLA_EXP_CLAMP = 80.0
GLA_PRE_SAFE = 29.0

MLA_HEADS = 4
MLA_NOPE = 128
MLA_ROPE = 64
MLA_V = 128
MLA_QK = MLA_NOPE + MLA_ROPE
MLA_Q_RANK = 256
MLA_KV_RANK = 128
MLA_SCALE = MLA_QK ** -0.5
MLA_Q_SCALE = MLA_SCALE * 1.4426950408889634
MLA_V_W = MLA_HEADS * MLA_V
MLA_V_EXT = 2 * MLA_V
ROPE_BASE = 10000.0

N_EXPERTS = 8
LANES = 128
ROUTE_E1, ROUTE_E2, ROUTE_W1, ROUTE_W2 = 0, 1, 2, 3

SC_CORES = 2
SC_SUBCORES = 16
SC_GATHER_ROWS = 64
MOE_TILE = 1024
MOE_FF_TILE = 512
MOE_FF_SPLITS = ((0, 256), (256, 512))

P_Q, P_K, P_V, P_R, P_CQ, P_CKV, P_MISC = 0, 256, 512, 1024, 1536, 1792, 1920
P_WIDTH = 2048
MISC_KR, MISC_GF, MISC_GB = 0, 64, 80

VMEM_LIMIT = 56 * 1024 * 1024

TOKEN_TILE = 512
CTX_TILE = 256
ATTN_TILE = 1024
GLA_BLOCK = 256
DENSE_FF_GROUP = 1024
FFN_MAX_FF_TILE = 1408


def _cparams(sem):
    return pltpu.CompilerParams(dimension_semantics=sem, vmem_limit_bytes=VMEM_LIMIT)


def _rms(x, g):
    return x * lax.rsqrt(jnp.mean(x * x, axis=-1, keepdims=True) + EPS) * g


def _silu(x):
    return x / (1.0 + jnp.exp(-x))


def _dot(a, b):
    return jnp.dot(a, b, preferred_element_type=F32)


def _dot_nt(a, b):
    return lax.dot_general(a, b, (((1,), (1,)), ((), ())), preferred_element_type=F32)


def _dot_tn(a, b):
    return lax.dot_general(a, b, (((0,), (0,)), ((), ())), preferred_element_type=F32)


def _pack_bf16_pairs(x):
    w = x.shape[1] // 2
    words = pltpu.pack_elementwise([x[:, :w], x[:, w:]], packed_dtype=BF16)
    return lax.bitcast_convert_type(words, F32)


def _unpack_bf16_pairs(p):
    words = lax.bitcast_convert_type(p, jnp.int32)
    lo = pltpu.unpack_elementwise(words, index=0, packed_dtype=BF16, unpacked_dtype=F32)
    hi = pltpu.unpack_elementwise(words, index=1, packed_dtype=BF16, unpacked_dtype=F32)
    return jnp.concatenate([lo, hi], axis=1)


def _mod_kernel(c_ref, w_ref, b_ref, o_ref):
    s = _silu(c_ref[...]).astype(BF16)
    o_ref[0] = _dot(s, w_ref[0].astype(BF16)) + b_ref[0]


def _modulation(cvec, w_mod, b_mod):
    depth, d, n = w_mod.shape
    tn = 1536
    return pl.pallas_call(
        _mod_kernel,
        grid=(depth, n // tn),
        in_specs=[
            pl.BlockSpec((8, d), lambda l, j: (0, 0)),
            pl.BlockSpec((1, d, tn), lambda l, j: (l, 0, j)),
            pl.BlockSpec((1, 1, tn), lambda l, j: (l, 0, j)),
        ],
        out_specs=pl.BlockSpec((1, 8, tn), lambda l, j: (l, 0, j)),
        out_shape=jax.ShapeDtypeStruct((depth, 8, n), F32),
        compiler_params=_cparams(("arbitrary", "arbitrary")),
        name="modulation",
    )(cvec, w_mod, b_mod.reshape(depth, 1, n))


def _inproj_kernel(x_ref, mod_ref, g_ref, w_ref, wg_ref, bg_ref, o_ref, gate_ref, *, gate_block):
    m = mod_ref[0]
    tm = x_ref.shape[0]
    n_groups = 2 if tm % (2 * gate_block) == 0 else 1
    rows_per_group = tm // n_groups
    sub = lax.broadcasted_iota(jnp.int32, (8, LANES), 0)
    for g in range(n_groups):
        rows = slice(g * rows_per_group, (g + 1) * rows_per_group)
        h = _rms(x_ref[rows, :], g_ref[...]) * (1.0 + m[1:2]) + m[0:1]
        p = _dot(h.astype(BF16), w_ref[...]).astype(BF16)
        o_ref[rows, :] = p
        misc = p[:, P_MISC:P_MISC + LANES]
        neg = [-(_dot(misc, wg_ref[z]) + bg_ref[z]) for z in range(2)]
        for j in range(rows_per_group // gate_block):
            blk_rows = slice(j * gate_block, (j + 1) * gate_block)
            worst = [jnp.max(jnp.max(n[blk_rows], axis=0, keepdims=True), axis=1, keepdims=True)
                     for n in neg]
            gate_ref[g * (rows_per_group // gate_block) + j] = jnp.where(
                sub == 0, worst[0], jnp.where(sub == 1, worst[1], 0.0))


def _inproj(x, mods, row_fn, ln_g, w, wg, bg, tm, gate_block):
    t, d = x.shape
    assert tm % gate_block == 0
    full = lambda a: pl.BlockSpec(a.shape, lambda i: (0,) * a.ndim)
    return pl.pallas_call(
        functools.partial(_inproj_kernel, gate_block=gate_block),
        grid=(t // tm,),
        in_specs=[
            pl.BlockSpec((tm, d), lambda i: (i, 0)),
            pl.BlockSpec((1, 6, d), lambda i: (row_fn(i), 0, 0)),
            pl.BlockSpec((1, d), lambda i: (0, 0)),
            pl.BlockSpec((d, P_WIDTH), lambda i: (0, 0)),
            full(wg), full(bg),
        ],
        out_specs=[pl.BlockSpec((tm, P_WIDTH), lambda i: (i, 0)),
                   pl.BlockSpec((tm // gate_block, 8, LANES), lambda i: (i, 0, 0))],
        out_shape=[jax.ShapeDtypeStruct((t, P_WIDTH), BF16),
                   jax.ShapeDtypeStruct((t // gate_block, 8, LANES), F32)],
        compiler_params=_cparams(("arbitrary",)),
        name="inproj",
    )(x, mods, ln_g, w, wg, bg)


def _gla_direction(q_ref, k_ref, v_ref, pre, tri, s_scr, o_ref, *, reverse, n_chunks, exact):
    c_len = GLA_CHUNK
    g = (jnp.minimum(pre, 0.0) - jnp.log(1.0 + jnp.exp(-jnp.abs(pre)))) * (1.0 / GLA_GATE_NORM)
    g_hi = g.astype(BF16)
    g_lo = (g - g_hi.astype(F32)).astype(BF16)
    cum = _dot(tri, g_hi) + _dot(tri, g_lo)
    tot_rows = jnp.concatenate(
        [cum[c * c_len:c * c_len + 1] if reverse else cum[(c + 1) * c_len - 1:(c + 1) * c_len]
         for c in range(n_chunks)] + [jnp.zeros((8 - n_chunks, GLA_QK_W), F32)], axis=0)
    t_hi = tot_rows.astype(BF16)
    t_lo = (tot_rows - t_hi.astype(F32)).astype(BF16)
    eye = (lax.broadcasted_iota(jnp.int32, (GLA_QK_W, GLA_QK_W), 0)
           == lax.broadcasted_iota(jnp.int32, (GLA_QK_W, GLA_QK_W), 1))
    eye = jnp.where(eye, 1.0, 0.0).astype(BF16)
    tot_cols = _dot_nt(eye, t_hi) + _dot_nt(eye, t_lo)

    lane = lax.broadcasted_iota(jnp.int32, (c_len, GLA_QK_W), 1)
    head_masks = [(lane >= h * GLA_DK) & (lane < (h + 1) * GLA_DK) for h in range(GLA_HEADS)]
    row = lax.broadcasted_iota(jnp.int32, (GLA_HEADS * c_len, c_len), 0) % c_len
    col = lax.broadcasted_iota(jnp.int32, (GLA_HEADS * c_len, c_len), 1)
    pair_mask = (col >= row) if reverse else (col <= row)

    def stack_heads(a):
        return jnp.concatenate([jnp.where(mk, a, 0.0) for mk in head_masks], axis=0).astype(BF16)

    def exact_levels():
        n = n_chunks * c_len
        r = lax.broadcasted_iota(jnp.int32, (n, n), 0)
        u = lax.broadcasted_iota(jnp.int32, (n, n), 1)
        same_chunk = (r // c_len) == (u // c_len)
        t = r % c_len
        uu = u % c_len
        levels = []
        b = c_len
        while b >= 2:
            half = b // 2
            mid = (t // b) * b + half
            same = same_chunk & ((t // b) == (uu // b))
            if reverse:
                sel_q = same & (t < mid) & (uu >= t) & (uu < mid)
                sel_k = same & (t >= mid) & (uu >= mid) & (uu < t)
            else:
                sel_q = same & (t >= mid) & (uu >= mid) & (uu <= t)
                sel_k = same & (t < mid) & (uu > t) & (uu < mid)
            mq = jnp.where(sel_q, 1.0, 0.0).astype(BF16)
            mk_ = jnp.where(sel_k, 1.0, 0.0).astype(BF16)
            e_q = _dot(mq, g_hi) + _dot(mq, g_lo)
            e_k = _dot(mk_, g_hi) + _dot(mk_, g_lo)
            tq_ = row
            if reverse:
                own = ((tq_ // b) == (col // b)) & ((tq_ % b) < half) & ((col % b) >= half)
            else:
                own = ((tq_ // b) == (col // b)) & ((tq_ % b) >= half) & ((col % b) < half)
            levels.append((e_q, e_k, own))
            b = half
        return levels

    levels = exact_levels() if exact else None

    def intra_scores(sl, xc, qc, kc):
        if not exact:
            ref = xc[c_len // 2:c_len // 2 + 1]
            q_mid = qc * jnp.exp(jnp.minimum(xc - ref, GLA_EXP_CLAMP))
            k_mid = (kc * jnp.exp(jnp.minimum(ref - xc, GLA_EXP_CLAMP))).astype(BF16)
            return jnp.where(pair_mask, _dot_nt(stack_heads(q_mid), k_mid), 0.0)
        attn = jnp.where(row == col, _dot_nt(stack_heads(qc), kc.astype(BF16)), 0.0)
        for e_q, e_k, own in levels:
            a = _dot_nt(stack_heads(qc * jnp.exp(e_q[sl])), (kc * jnp.exp(e_k[sl])).astype(BF16))
            attn = attn + jnp.where(own, a, 0.0)
        return attn

    def step(c):
        sl = slice(c * c_len, (c + 1) * c_len)
        xc = cum[sl]
        tot = tot_rows[c:c + 1]
        qc = q_ref[sl, :].astype(F32) * (GLA_DK ** -0.5)
        kc = k_ref[sl, :].astype(F32)
        vc = v_ref[sl, :]
        q_dec = qc * jnp.exp(xc)
        k_dec = kc * jnp.exp(tot - xc)

        attn = intra_scores(sl, xc, qc, kc).astype(BF16)
        s_prev = s_scr[...]
        o_inter = _dot(stack_heads(q_dec), s_prev.astype(BF16))
        kv = []
        for h in range(GLA_HEADS):
            rs = slice(h * c_len, (h + 1) * c_len)
            vs = slice(h * GLA_DV, (h + 1) * GLA_DV)
            o_h = o_inter[rs] + _dot(attn[rs], vc[:, vs])
            o_ref[sl, vs] = o_h.astype(BF16)
            kv.append(_dot_tn(k_dec[:, h * GLA_DK:(h + 1) * GLA_DK].astype(BF16), vc[:, vs]))
        s_scr[...] = s_prev * jnp.exp(tot_cols[:, c:c + 1]) + jnp.concatenate(kv, axis=0)

    return step


def _gla_kernel(gate_ref, qf_ref, kf_ref, vf_ref, mf_ref, qb_ref, kb_ref, vb_ref, mb_ref,
                wg_ref, bg_ref, tri_ref, s0_ref, of_ref, ob_ref, sfin_ref, sf_scr, sb_scr, *, n_chunks):
    blk = pl.program_id(1)
    n_blk = pl.num_programs(1)

    @pl.when(blk == 0)
    def _():
        sf_scr[...] = s0_ref[0, 0]
        sb_scr[...] = s0_ref[1, 0]

    fwd_block = pl.program_id(0) * n_blk + blk
    bwd_block = pl.program_id(0) * n_blk + (n_blk - 1 - blk)
    extreme = jnp.maximum(gate_ref[2 * fwd_block], gate_ref[2 * bwd_block + 1]) > GLA_PRE_SAFE

    def run(exact):
        pre_f = _dot(mf_ref[...], wg_ref[0]) + bg_ref[0]
        pre_b = _dot(mb_ref[...], wg_ref[1]) + bg_ref[1]
        fwd = _gla_direction(qf_ref, kf_ref, vf_ref, pre_f, tri_ref[0], sf_scr, of_ref,
                             reverse=False, n_chunks=n_chunks, exact=exact)
        bwd = _gla_direction(qb_ref, kb_ref, vb_ref, pre_b, tri_ref[1], sb_scr, ob_ref,
                             reverse=True, n_chunks=n_chunks, exact=exact)
        for c in range(n_chunks):
            fwd(c)
            bwd(n_chunks - 1 - c)

    @pl.when(jnp.logical_not(extreme))
    def _():
        run(False)

    @pl.when(extreme)
    def _():
        run(True)

    @pl.when(blk == pl.num_programs(1) - 1)
    def _():
        sfin_ref[0, 0] = sf_scr[...]
        sfin_ref[1, 0] = sb_scr[...]


def _block_diag_tri(n_chunks):
    c = GLA_CHUNK
    eye = np.eye(n_chunks, dtype=np.float32)
    lower = np.kron(eye, np.tril(np.ones((c, c), np.float32)))
    upper = np.kron(eye, np.triu(np.ones((c, c), np.float32)))
    return jnp.asarray(np.stack([lower, upper]), dtype=BF16)


def _gla(p, gate_bound, wg, bg, s0, *, batch, cb):
    t_all = p.shape[0]
    nblk = t_all // batch // cb
    n_chunks = cb // GLA_CHUNK
    assert n_chunks <= 8 and gate_bound.shape[0] == batch * nblk
    gate = gate_bound[:, 0:2, 0].reshape(-1)

    fw = lambda b, i, gate_ref: b * nblk + i
    bw = lambda b, i, gate_ref: b * nblk + (nblk - 1 - i)
    full = lambda a: pl.BlockSpec(a.shape, lambda b, i, gate_ref: (0,) * a.ndim)
    tri = _block_diag_tri(n_chunks)

    def token_specs(tok):
        return [
            pl.BlockSpec((cb, GLA_QK_W), lambda b, i, g: (tok(b, i, g), P_Q // GLA_QK_W)),
            pl.BlockSpec((cb, GLA_QK_W), lambda b, i, g: (tok(b, i, g), P_K // GLA_QK_W)),
            pl.BlockSpec((cb, GLA_V_W), lambda b, i, g: (tok(b, i, g), P_V // GLA_V_W)),
            pl.BlockSpec((cb, LANES), lambda b, i, g: (tok(b, i, g), P_MISC // LANES)),
        ]

    state_spec = pl.BlockSpec((2, 1, GLA_QK_W, GLA_DV), lambda b, i, g: (0, b, 0, 0))
    grid_spec = pltpu.PrefetchScalarGridSpec(
        num_scalar_prefetch=1,
        grid=(batch, nblk),
        in_specs=token_specs(fw) + token_specs(bw) + [full(wg), full(bg), full(tri), state_spec],
        out_specs=[
            pl.BlockSpec((cb, GLA_V_W), lambda b, i, g: (fw(b, i, g), 0)),
            pl.BlockSpec((cb, GLA_V_W), lambda b, i, g: (bw(b, i, g), 0)),
            state_spec,
        ],
        scratch_shapes=[pltpu.VMEM((GLA_QK_W, GLA_DV), F32), pltpu.VMEM((GLA_QK_W, GLA_DV), F32)],
    )
    return pl.pallas_call(
        functools.partial(_gla_kernel, n_chunks=n_chunks),
        grid_spec=grid_spec,
        out_shape=[
            jax.ShapeDtypeStruct((t_all, GLA_V_W), BF16),
            jax.ShapeDtypeStruct((t_all, GLA_V_W), BF16),
            jax.ShapeDtypeStruct((2, batch, GLA_QK_W, GLA_DV), F32),
        ],
        compiler_params=_cparams(("arbitrary", "arbitrary")),
        name="gla",
    )(gate, p, p, p, p, p, p, p, p, wg, bg, tri, s0)


def _mlaprep_kernel(cq_ref, ckv_ref, misc_ref, cos_ref, sin_ref, qg_ref, kvg_ref,
                    wqn_ref, wqr_ref, wqs_ref, wknt_ref, wv_ref, perm_ref, eye_ref,
                    q_ref, kt_ref, v_ref):
    tm = cq_ref.shape[0]
    n_groups = 2 if tm % 256 == 0 else 1
    for g in range(n_groups):
        rows = slice(g * tm // n_groups, (g + 1) * tm // n_groups)
        cos = cos_ref[rows, :]
        sin = sin_ref[rows, :]
        cqn = _rms(cq_ref[rows, :].astype(F32), qg_ref[...]).astype(BF16)
        qn = _dot(cqn, wqn_ref[...])
        qr = _dot(cqn, wqr_ref[...])
        qs = _dot(cqn, wqs_ref[...])
        for h in range(MLA_HEADS):
            ls = slice(h * LANES, (h + 1) * LANES)
            q_ref[0, h, rows, 0:MLA_NOPE] = (qn[:, ls] * MLA_Q_SCALE).astype(BF16)
            rot = qr[:, ls] * cos + qs[:, ls] * sin
            q_ref[0, h, rows, MLA_NOPE:MLA_QK] = (rot[:, 0:MLA_ROPE] * MLA_Q_SCALE).astype(BF16)

        ckvn = _rms(ckv_ref[rows, :].astype(F32), kvg_ref[...]).astype(BF16)
        knt = _dot_nt(wknt_ref[...], ckvn)
        vv = _dot(ckvn, wv_ref[...])
        misc = misc_ref[rows, :]
        kr = misc.astype(F32) * cos + _dot(misc, perm_ref[...]) * sin
        krt = _dot_nt(eye_ref[...], kr.astype(BF16)).astype(BF16)
        for h in range(MLA_HEADS):
            kt_ref[0, h, 0, 0:MLA_NOPE, rows] = knt[h * MLA_NOPE:(h + 1) * MLA_NOPE].astype(BF16)
            kt_ref[0, h, 0, MLA_NOPE:MLA_QK, rows] = krt
            v_ref[0, h, rows, 0:MLA_V] = vv[:, h * MLA_V:(h + 1) * MLA_V].astype(BF16)
            v_ref[0, h, rows, MLA_V:MLA_V_EXT] = jnp.ones((vv.shape[0], MLA_V), BF16)


def _mlaprep(p, cos, sin, qg, kvg, wts, *, batch, tm):
    t_all = p.shape[0]
    t = t_all // batch
    nb = t // tm
    ntab = cos.shape[0] // tm
    wqn, wqr, wqs, wknt, wv, perm, eye = wts
    full = lambda a: pl.BlockSpec(a.shape, lambda b, i: (0,) * a.ndim)
    return pl.pallas_call(
        _mlaprep_kernel,
        grid=(batch, nb),
        in_specs=[
            pl.BlockSpec((tm, MLA_Q_RANK), lambda b, i: (b * nb + i, P_CQ // MLA_Q_RANK)),
            pl.BlockSpec((tm, MLA_KV_RANK), lambda b, i: (b * nb + i, P_CKV // MLA_KV_RANK)),
            pl.BlockSpec((tm, LANES), lambda b, i: (b * nb + i, P_MISC // LANES)),
            pl.BlockSpec((tm, LANES), lambda b, i: (i % ntab, 0)),
            pl.BlockSpec((tm, LANES), lambda b, i: (i % ntab, 0)),
            full(qg), full(kvg), full(wqn), full(wqr), full(wqs), full(wknt), full(wv),
            full(perm), full(eye),
        ],
        out_specs=[
            pl.BlockSpec((1, MLA_HEADS, tm, MLA_QK), lambda b, i: (b, 0, i, 0)),
            pl.BlockSpec((1, MLA_HEADS, 1, MLA_QK, tm), lambda b, i: (b, 0, i, 0, 0)),
            pl.BlockSpec((1, MLA_HEADS, tm, MLA_V_EXT), lambda b, i: (b, 0, i, 0)),
        ],
        out_shape=[
            jax.ShapeDtypeStruct((batch, MLA_HEADS, t, MLA_QK), BF16),
            jax.ShapeDtypeStruct((batch, MLA_HEADS, nb, MLA_QK, tm), BF16),
            jax.ShapeDtypeStruct((batch, MLA_HEADS, t, MLA_V_EXT), BF16),
        ],
        compiler_params=_cparams(("arbitrary", "arbitrary")),
        name="mlaprep",
    )(p, p, p, cos, sin, qg, kvg, wqn, wqr, wqs, wknt, wv, perm, eye)


def _attn_kernel(*refs, n_seg, n_sub):
    q_ref = refs[0]
    kt_refs = refs[1:1 + 2 * n_seg:2]
    v_refs = refs[2:2 + 2 * n_seg:2]
    o_ref = refs[1 + 2 * n_seg]
    m_scr, acc_scr = refs[2 + 2 * n_seg:]

    rows_per_sub = q_ref.shape[2] // n_sub
    m_scr[...] = jnp.full(m_scr.shape, -jnp.inf, F32)
    acc_scr[...] = jnp.zeros(acc_scr.shape, F32)

    for kt_ref, v_ref in zip(kt_refs, v_refs):
        n_blocks, tk = kt_ref.shape[2], kt_ref.shape[4]

        def step(j, carry, kt_ref=kt_ref, v_ref=v_ref, tk=tk):
            kt = kt_ref[0, 0, j]
            v_blk = v_ref[0, 0, pl.ds(pl.multiple_of(j * tk, tk), tk), :]
            for u in range(n_sub):
                rows = slice(u * rows_per_sub, (u + 1) * rows_per_sub)
                s = _dot(q_ref[0, 0, rows, :], kt)
                m_prev = m_scr[rows, :]
                m_next = jnp.maximum(m_prev, jnp.max(s, axis=1, keepdims=True))
                p = jnp.exp2((s - jnp.concatenate([m_next] * (tk // LANES), axis=1)).astype(BF16))
                alpha = jnp.exp2(m_prev - m_next)
                acc_scr[rows, :] = (jnp.concatenate([alpha] * (MLA_V_EXT // LANES), axis=1)
                                    * acc_scr[rows, :] + _dot(p, v_blk))
                m_scr[rows, :] = m_next
            return carry

        lax.fori_loop(0, n_blocks, step, 0)

    o_ref[0] = (acc_scr[:, 0:MLA_V] / acc_scr[:, MLA_V:MLA_V_EXT]).astype(BF16)


def _attn_pipe_kernel(q_ref, kt_ref, v_ref, ktt_ref, vt_ref, o_ref,
                      m_scr, acc_scr, s0_scr, s1_scr, st_scr, *, n_sub):
    n_blocks, tk = kt_ref.shape[2], kt_ref.shape[4]
    tq = m_scr.shape[0]
    n_q = q_ref.shape[2] // tq
    rows_per_sub = tq // n_sub
    subs = [slice(u * rows_per_sub, (u + 1) * rows_per_sub) for u in range(n_sub)]
    bufs = (s0_scr, s1_scr)

    def reset():
        m_scr[...] = jnp.full(m_scr.shape, -jnp.inf, F32)
        acc_scr[...] = jnp.zeros(acc_scr.shape, F32)

    def scores(qi, kt, s_ref):
        for rows in subs:
            q_rows = pl.ds(pl.multiple_of(qi * tq + rows.start, rows_per_sub), rows_per_sub)
            s_ref[rows, :] = _dot(q_ref[0, 0, q_rows, :], kt)

    def softmax_pv(s_ref, v_blk):
        width = s_ref.shape[1]
        for rows in subs:
            s = s_ref[rows, :]
            m_prev = m_scr[rows, :]
            m_next = jnp.maximum(m_prev, jnp.max(s, axis=1, keepdims=True))
            p = jnp.exp2((s - jnp.concatenate([m_next] * (width // LANES), axis=1)).astype(BF16))
            alpha = jnp.exp2(m_prev - m_next)
            acc_scr[rows, :] = (jnp.concatenate([alpha] * (MLA_V_EXT // LANES), axis=1)
                                * acc_scr[rows, :] + _dot(p, v_blk))
            m_scr[rows, :] = m_next

    reset()
    scores(0, kt_ref[0, 0, 0], bufs[0])

    def query_block(qi, carry):
        for j in range(n_blocks):
            softmax_pv(bufs[j % 2], v_ref[0, 0, j * tk:(j + 1) * tk, :])
            if j + 1 < n_blocks:
                scores(qi, kt_ref[0, 0, j + 1], bufs[(j + 1) % 2])
            else:
                scores(qi, ktt_ref[0, 0, 0], st_scr)
        softmax_pv(st_scr, vt_ref[0, 0])
        scores(jnp.minimum(qi + 1, n_q - 1), kt_ref[0, 0, 0], bufs[0])
        o_rows = pl.ds(pl.multiple_of(qi * tq, tq), tq)
        o_ref[0, o_rows, :] = (acc_scr[:, 0:MLA_V] / acc_scr[:, MLA_V:MLA_V_EXT]).astype(BF16)
        reset()
        return carry

    lax.fori_loop(0, n_q, query_block, 0)


def _attention_pipelined(q, kt, v, kt_tail, v_tail, *, tq, n_sub):
    b, h, t, dqk = q.shape
    tk, tt = kt.shape[4], kt_tail.shape[4]
    assert kt_tail.shape[2] == 1
    return pl.pallas_call(
        functools.partial(_attn_pipe_kernel, n_sub=n_sub),
        grid=(b, h),
        in_specs=[
            pl.BlockSpec((1, 1, t, dqk), lambda bi, hi: (bi, hi, 0, 0)),
            pl.BlockSpec((1, 1) + kt.shape[2:], lambda bi, hi: (bi, hi, 0, 0, 0)),
            pl.BlockSpec((1, 1) + v.shape[2:], lambda bi, hi: (bi, hi, 0, 0)),
            pl.BlockSpec((1, 1) + kt_tail.shape[2:], lambda bi, hi: (bi, hi, 0, 0, 0)),
            pl.BlockSpec((1, 1) + v_tail.shape[2:], lambda bi, hi: (bi, hi, 0, 0)),
        ],
        out_specs=pl.BlockSpec((1, t, MLA_V), lambda bi, hi: (bi, 0, hi)),
        out_shape=jax.ShapeDtypeStruct((b, t, h * MLA_V), BF16),
        scratch_shapes=[pltpu.VMEM((tq, LANES), F32), pltpu.VMEM((tq, MLA_V_EXT), F32),
                        pltpu.VMEM((tq, tk), F32), pltpu.VMEM((tq, tk), F32),
                        pltpu.VMEM((tq, tt), F32)],
        compiler_params=_cparams(("arbitrary", "arbitrary")),
        name="mla_attention_pipe",
    )(q, kt, v, kt_tail, v_tail)


def _attention(q, segs, *, tq, n_sub):
    b, h, t, dqk = q.shape
    in_specs = [pl.BlockSpec((1, 1, tq, dqk), lambda bi, hi, qi: (bi, hi, qi, 0))]
    args = [q]
    for kt, v in segs:
        in_specs.append(pl.BlockSpec((1, 1) + kt.shape[2:], lambda bi, hi, qi: (bi, hi, 0, 0, 0)))
        in_specs.append(pl.BlockSpec((1, 1) + v.shape[2:], lambda bi, hi, qi: (bi, hi, 0, 0)))
        args += [kt, v]
    return pl.pallas_call(
        functools.partial(_attn_kernel, n_seg=len(segs), n_sub=n_sub),
        grid=(b, h, t // tq),
        in_specs=in_specs,
        out_specs=pl.BlockSpec((1, tq, MLA_V), lambda bi, hi, qi: (bi, qi, hi)),
        out_shape=jax.ShapeDtypeStruct((b, t, h * MLA_V), BF16),
        scratch_shapes=[pltpu.VMEM((tq, LANES), F32), pltpu.VMEM((tq, MLA_V_EXT), F32)],
        compiler_params=_cparams(("arbitrary", "arbitrary", "arbitrary")),
        name="mla_attention",
    )(*args)


def _mix_residual_norm(x_ref, of_ref, ob_ref, r_ref, mla_ref, m, gg_ref, wo_ref, ln2_ref,
                       rows=slice(None)):
    o = of_ref[rows, :].astype(F32) + ob_ref[rows, :].astype(F32)
    gg = gg_ref[...]
    y = jnp.concatenate(
        [_rms(o[:, h * GLA_DV:(h + 1) * GLA_DV], gg) for h in range(GLA_HEADS)], axis=1)
    mix = (y * _silu(r_ref[rows, :].astype(F32))).astype(BF16)
    yo = _dot(mix, wo_ref[0:GLA_V_W, :]) + _dot(mla_ref[rows, :], wo_ref[GLA_V_W:, :])
    x1 = x_ref[rows, :] + m[2:3] * yo
    h2 = _rms(x1, ln2_ref[...]) * (1.0 + m[4:5]) + m[3:4]
    return x1, h2


def _outproj_ffn_kernel(*refs, groups, final_norm):
    (x_ref, of_ref, ob_ref, r_ref, mla_ref, mod_ref, gg_ref, wo_ref, ln2_ref,
     wg_ref, wu_ref, wd_ref) = refs[:12]
    fin_ref = refs[12] if final_norm else None
    o_ref = refs[-1]
    m = mod_ref[0]
    x1, h2 = _mix_residual_norm(x_ref, of_ref, ob_ref, r_ref, mla_ref, m, gg_ref, wo_ref, ln2_ref)
    h = h2.astype(BF16)
    y = None
    for lo, hi in groups:
        a = _dot(h, wg_ref[:, lo:hi])
        u = _dot(h, wu_ref[:, lo:hi])
        part = _dot((_silu(a) * u).astype(BF16), wd_ref[lo:hi, :])
        y = part if y is None else y + part
    x2 = x1 + m[5:6] * y
    if final_norm:
        x2 = _rms(x2, fin_ref[...])
    o_ref[...] = x2


def _outproj_ffn(x, o_f, o_b, p, mla, mods, row_fn, gg, wo, ln2, wg, wu, wd, fin_g, tm):
    t, d = x.shape
    ff = wg.shape[1]
    final_norm = fin_g is not None
    step = min(ff, DENSE_FF_GROUP)
    groups = tuple((lo, min(lo + step, ff)) for lo in range(0, ff, step))
    resident = lambda a: pl.BlockSpec(a.shape, lambda i: (0,) * a.ndim, pipeline_mode=pl.Buffered(1))
    in_specs = [
        pl.BlockSpec((tm, d), lambda i: (i, 0)),
        pl.BlockSpec((tm, GLA_V_W), lambda i: (i, 0)),
        pl.BlockSpec((tm, GLA_V_W), lambda i: (i, 0)),
        pl.BlockSpec((tm, GLA_V_W), lambda i: (i, P_R // GLA_V_W)),
        pl.BlockSpec((tm, MLA_V_W), lambda i: (i, 0)),
        pl.BlockSpec((1, 6, d), lambda i: (row_fn(i), 0, 0)),
        resident(gg), resident(wo), resident(ln2), resident(wg), resident(wu), resident(wd),
    ]
    args = [x, o_f, o_b, p, mla, mods, gg, wo, ln2, wg, wu, wd]
    if final_norm:
        in_specs.append(resident(fin_g))
        args.append(fin_g)
    return pl.pallas_call(
        functools.partial(_outproj_ffn_kernel, groups=groups, final_norm=final_norm),
        grid=(t // tm,),
        in_specs=in_specs,
        out_specs=pl.BlockSpec((tm, d), lambda i: (i, 0)),
        out_shape=jax.ShapeDtypeStruct((t, d), F32),
        compiler_params=_cparams(("arbitrary",)),
        name="outproj_ffn",
    )(*args)


def _outproj_kernel(*refs, with_router):
    (x_ref, of_ref, ob_ref, r_ref, mla_ref, mod_ref, gg_ref, wo_ref, ln2_ref) = refs[:9]
    if with_router:
        rwh_ref, rwl_ref, x1_ref, h2_ref, comb_ref = refs[9:]
    else:
        x1_ref, h2_ref = refs[9:]
    m = mod_ref[0]
    tm = x_ref.shape[0]
    n_groups = 2 if tm % 32 == 0 else 1
    for g in range(n_groups):
        rows = slice(g * tm // n_groups, (g + 1) * tm // n_groups)
        x1, h2 = _mix_residual_norm(x_ref, of_ref, ob_ref, r_ref, mla_ref, m, gg_ref, wo_ref,
                                    ln2_ref, rows)
        x1_ref[rows, :] = x1
        if not with_router:
            h2_ref[rows, :] = h2.astype(BF16)
            continue
        h2_ref[rows, :] = _pack_bf16_pairs(h2)
        h_hi = h2.astype(BF16)
        h_lo = (h2 - h_hi.astype(F32)).astype(BF16)
        logits = _dot(h_hi, rwh_ref[...]) + _dot(h_lo, rwh_ref[...]) + _dot(h_hi, rwl_ref[...])
        lane = lax.broadcasted_iota(jnp.int32, logits.shape, 1).astype(F32)
        neg = jnp.float32(-jnp.inf)
        logits = jnp.where(lane < N_EXPERTS, logits, neg)
        m1 = jnp.max(logits, axis=1, keepdims=True)
        i1 = jnp.min(jnp.where(logits == m1, lane, float(LANES)), axis=1, keepdims=True)
        rest = jnp.where(lane == i1, neg, logits)
        m2 = jnp.max(rest, axis=1, keepdims=True)
        i2 = jnp.min(jnp.where(rest == m2, lane, float(LANES)), axis=1, keepdims=True)
        e2 = jnp.exp(m2 - m1)
        w1 = 1.0 / (1.0 + e2)
        comb_ref[rows, :] = (jnp.where(lane == ROUTE_E1, i1, 0.0) + jnp.where(lane == ROUTE_E2, i2, 0.0)
                             + jnp.where(lane == ROUTE_W1, w1, 0.0)
                             + jnp.where(lane == ROUTE_W2, e2 * w1, 0.0))


def _outproj(x, o_f, o_b, p, mla, mods, row_fn, gg, wo, ln2, router, tm):
    t, d = x.shape
    with_router = router is not None
    full = lambda a: pl.BlockSpec(a.shape, lambda i: (0,) * a.ndim)
    in_specs = [
        pl.BlockSpec((tm, d), lambda i: (i, 0)),
        pl.BlockSpec((tm, GLA_V_W), lambda i: (i, 0)),
        pl.BlockSpec((tm, GLA_V_W), lambda i: (i, 0)),
        pl.BlockSpec((tm, GLA_V_W), lambda i: (i, P_R // GLA_V_W)),
        pl.BlockSpec((tm, MLA_V_W), lambda i: (i, 0)),
        pl.BlockSpec((1, 6, d), lambda i: (row_fn(i), 0, 0)),
        full(gg), full(wo), full(ln2),
    ]
    args = [x, o_f, o_b, p, mla, mods, gg, wo, ln2]
    h2_shape = jax.ShapeDtypeStruct((t, d // 2), F32) if with_router else jax.ShapeDtypeStruct((t, d), BF16)
    out_specs = [pl.BlockSpec((tm, d), lambda i: (i, 0)),
                 pl.BlockSpec((tm, h2_shape.shape[1]), lambda i: (i, 0))]
    out_shape = [jax.ShapeDtypeStruct((t, d), F32), h2_shape]
    if with_router:
        in_specs += [full(router[0]), full(router[1])]
        args += list(router)
        out_specs.append(pl.BlockSpec((tm, LANES), lambda i: (i, 0)))
        out_shape.append(jax.ShapeDtypeStruct((t, LANES), F32))
    return pl.pallas_call(
        functools.partial(_outproj_kernel, with_router=with_router),
        grid=(t // tm,),
        in_specs=in_specs,
        out_specs=out_specs,
        out_shape=out_shape,
        compiler_params=_cparams(("arbitrary",)),
        name="outproj",
    )(*args)


def _ffn_kernel(*refs, with_comb, final_norm):
    h_ref, x1_ref, mod_ref = refs[:3]
    k = 3
    comb_ref = fin_ref = None
    if with_comb:
        comb_ref = refs[k]
        k += 1
    wg_ref, wu_ref, wd_ref = refs[k:k + 3]
    k += 3
    if final_norm:
        fin_ref = refs[k]
        k += 1
    o_ref, acc = refs[k:]
    e = pl.program_id(1)
    f = pl.program_id(2)

    @pl.when((e == 0) & (f == 0))
    def _():
        acc[...] = jnp.zeros(acc.shape, F32)

    h = h_ref[...]
    a = _dot(h, wg_ref[0])
    u = _dot(h, wu_ref[0])
    act = _silu(a) * u
    if with_comb:
        comb = comb_ref[...]
        lane = lax.broadcasted_iota(jnp.int32, comb.shape, 1)
        act = act * jnp.sum(jnp.where(lane == e, comb, 0.0), axis=1, keepdims=True)
    acc[...] += _dot(act.astype(BF16), wd_ref[0])

    @pl.when((e == pl.num_programs(1) - 1) & (f == pl.num_programs(2) - 1))
    def _():
        x2 = x1_ref[...] + mod_ref[0][5:6] * acc[...]
        if final_norm:
            x2 = _rms(x2, fin_ref[...])
        o_ref[...] = x2


def _ffn(h2, x1, mods, row_fn, comb, wg, wu, wd, fin_g, tm, tf):
    t, d = x1.shape
    n_e, _, ff = wg.shape
    with_comb = comb is not None
    final_norm = fin_g is not None
    in_specs = [
        pl.BlockSpec((tm, d), lambda i, e, f: (i, 0)),
        pl.BlockSpec((tm, d), lambda i, e, f: (i, 0)),
        pl.BlockSpec((1, 6, d), lambda i, e, f: (row_fn(i), 0, 0)),
    ]
    args = [h2, x1, mods]
    if with_comb:
        in_specs.append(pl.BlockSpec((tm, LANES), lambda i, e, f: (i, 0)))
        args.append(comb)
    in_specs += [
        pl.BlockSpec((1, d, tf), lambda i, e, f: (e, 0, f)),
        pl.BlockSpec((1, d, tf), lambda i, e, f: (e, 0, f)),
        pl.BlockSpec((1, tf, d), lambda i, e, f: (e, f, 0)),
    ]
    args += [wg, wu, wd]
    if final_norm:
        in_specs.append(pl.BlockSpec((1, d), lambda i, e, f: (0, 0)))
        args.append(fin_g)
    return pl.pallas_call(
        functools.partial(_ffn_kernel, with_comb=with_comb, final_norm=final_norm),
        grid=(t // tm, n_e, ff // tf),
        in_specs=in_specs,
        out_specs=pl.BlockSpec((tm, d), lambda i, e, f: (i, 0)),
        out_shape=jax.ShapeDtypeStruct((t, d), F32),
        scratch_shapes=[pltpu.VMEM((tm, d), F32)],
        compiler_params=_cparams(("arbitrary", "arbitrary", "arbitrary")),
        name="ffn",
    )(*args)


def _sc_row_gather(table, idx):
    _, w = table.shape
    b = idx.shape[0]
    n_workers = SC_CORES * SC_SUBCORES
    assert b % (n_workers * SC_GATHER_ROWS) == 0, (b, n_workers, SC_GATHER_ROWS)
    b_per_w = b // n_workers
    n_chunks = b_per_w // SC_GATHER_ROWS
    mesh = plsc.VectorSubcoreMesh(core_axis_name="c", subcore_axis_name="s",
                                  num_cores=SC_CORES, num_subcores=SC_SUBCORES)

    def body(table_hbm, idx_hbm, out_hbm, idx_a, idx_b, rows_a, rows_b, sem_a, sem_b):
        wid = lax.axis_index("s") * SC_CORES + lax.axis_index("c")
        base = wid * b_per_w
        idx_bufs, row_bufs, sems = (idx_a, idx_b), (rows_a, rows_b), (sem_a, sem_b)

        def start(ci):
            slot = ci % 2
            pltpu.sync_copy(idx_hbm.at[pl.ds(base + ci * SC_GATHER_ROWS, SC_GATHER_ROWS)], idx_bufs[slot])
            return pltpu.async_copy(table_hbm.at[idx_bufs[slot]], row_bufs[slot], sems[slot])

        pending = start(0)
        for ci in range(n_chunks):
            following = start(ci + 1) if ci + 1 < n_chunks else None
            pending.wait()
            pltpu.sync_copy(row_bufs[ci % 2], out_hbm.at[pl.ds(base + ci * SC_GATHER_ROWS, SC_GATHER_ROWS)])
            pending = following

    return pl.kernel(
        body,
        out_type=jax.ShapeDtypeStruct((b, w), F32),
        mesh=mesh,
        scratch_types=[pltpu.VMEM((SC_GATHER_ROWS,), jnp.int32)] * 2
        + [pltpu.VMEM((SC_GATHER_ROWS, w), F32)] * 2 + [pltpu.SemaphoreType.DMA] * 2,
        name="sc_row_gather",
    )(table, idx)


def _sc_row_scatter2(table, pos, n_out):
    t, w = table.shape
    n_workers = SC_CORES * SC_SUBCORES
    assert t % (n_workers * SC_GATHER_ROWS) == 0, (t, n_workers, SC_GATHER_ROWS)
    t_per_w = t // n_workers
    n_chunks = t_per_w // SC_GATHER_ROWS
    mesh = plsc.VectorSubcoreMesh(core_axis_name="c", subcore_axis_name="s",
                                  num_cores=SC_CORES, num_subcores=SC_SUBCORES)

    def body(table_hbm, pos_hbm, out_hbm, i0a, i1a, i0b, i1b, rows_a, rows_b, sem_a, sem_b):
        wid = lax.axis_index("s") * SC_CORES + lax.axis_index("c")
        base = wid * t_per_w
        idx0, idx1, row_bufs, sems = (i0a, i0b), (i1a, i1b), (rows_a, rows_b), (sem_a, sem_b)

        def drain(pair):
            if pair is not None:
                pair[0].wait()
                pair[1].wait()

        pending = [None, None]
        for ci in range(n_chunks):
            slot = ci % 2
            drain(pending[slot])
            off = base + ci * SC_GATHER_ROWS
            pltpu.sync_copy(pos_hbm.at[pl.ds(off, SC_GATHER_ROWS)], idx0[slot])
            pltpu.sync_copy(pos_hbm.at[pl.ds(t + off, SC_GATHER_ROWS)], idx1[slot])
            pltpu.sync_copy(table_hbm.at[pl.ds(off, SC_GATHER_ROWS)], row_bufs[slot])
            pending[slot] = (pltpu.async_copy(row_bufs[slot], out_hbm.at[idx0[slot]], sems[slot]),
                             pltpu.async_copy(row_bufs[slot], out_hbm.at[idx1[slot]], sems[slot]))
        drain(pending[0])
        drain(pending[1])

    return pl.kernel(
        body,
        out_type=jax.ShapeDtypeStruct((n_out, w), F32),
        mesh=mesh,
        scratch_types=[pltpu.VMEM((SC_GATHER_ROWS,), jnp.int32)] * 4
        + [pltpu.VMEM((SC_GATHER_ROWS, w), F32)] * 2 + [pltpu.SemaphoreType.DMA] * 2,
        name="sc_row_scatter",
    )(table, pos)


def _moe_plan_kernel(route_ref, tri_ref, utri_ref, pos_ref, cnt_ref, run_scr, off_scr):
    phase = pl.program_id(0)
    blk = pl.program_id(1)
    route = route_ref[...]
    lane = lax.broadcasted_iota(jnp.int32, route.shape, 1).astype(F32)
    oh1 = jnp.where(lane == route[:, ROUTE_E1:ROUTE_E1 + 1], 1.0, 0.0)
    oh2 = jnp.where(lane == route[:, ROUTE_E2:ROUTE_E2 + 1], 1.0, 0.0)
    oh = oh1 + oh2

    @pl.when(blk == 0)
    def _():
        run_scr[...] = jnp.zeros(run_scr.shape, F32)

    @pl.when(phase == 0)
    def _():
        run_scr[...] += jnp.sum(oh, axis=0, keepdims=True)

        @pl.when(blk == pl.num_programs(1) - 1)
        def _():
            counts = run_scr[...]
            cnt_ref[...] = counts
            tiles_per = jnp.floor((counts + (MOE_TILE - 1.0)) * (1.0 / MOE_TILE))
            tile_end = _dot(jnp.broadcast_to(tiles_per, (8, LANES)).astype(BF16), utri_ref[...])[0:1]
            off_scr[...] = (tile_end - tiles_per) * float(MOE_TILE)

    @pl.when(phase == 1)
    def _():
        incl = _dot(tri_ref[...], oh.astype(BF16))
        before = incl - oh + run_scr[...] + off_scr[...]
        p1 = jnp.sum(before * oh1, axis=1, keepdims=True)
        p2 = jnp.sum(before * oh2, axis=1, keepdims=True)
        pos = jnp.where(lane == 0.0, p1, 0.0) + jnp.where(lane == 1.0, p2, 0.0)
        pos_ref[...] = pos.astype(jnp.int32)
        run_scr[...] += incl[incl.shape[0] - 1:, :]


def _moe_plan(route, n_tiles, tm):
    t = route.shape[0]
    tri = jnp.asarray(np.tril(np.ones((tm, tm), np.float32)), dtype=BF16)
    utri = jnp.asarray(np.triu(np.ones((LANES, LANES), np.float32)), dtype=BF16)
    pos, counts = pl.pallas_call(
        _moe_plan_kernel,
        grid=(2, t // tm),
        in_specs=[
            pl.BlockSpec((tm, LANES), lambda p, i: (i, 0)),
            pl.BlockSpec((tm, tm), lambda p, i: (0, 0)),
            pl.BlockSpec((LANES, LANES), lambda p, i: (0, 0)),
        ],
        out_specs=[
            pl.BlockSpec((tm, LANES), lambda p, i: (i * p, 0)),
            pl.BlockSpec((1, LANES), lambda p, i: (0, 0)),
        ],
        out_shape=[jax.ShapeDtypeStruct((t, LANES), jnp.int32),
                   jax.ShapeDtypeStruct((1, LANES), F32)],
        scratch_shapes=[pltpu.VMEM((1, LANES), F32), pltpu.VMEM((1, LANES), F32)],
        compiler_params=_cparams(("arbitrary", "arbitrary")),
        name="moe_plan",
    )(route, tri, utri)
    counts = counts[0, :N_EXPERTS].astype(jnp.int32)
    tile_end = jnp.cumsum((counts + MOE_TILE - 1) // MOE_TILE)
    n_used = tile_end[-1]
    tile_ids = jnp.minimum(jnp.arange(n_tiles, dtype=jnp.int32), n_used - 1)
    tile_expert = jnp.sum((tile_end[None, :] <= tile_ids[:, None]).astype(jnp.int32), axis=1)
    return pos[:, :2].T, tile_expert, n_used.reshape(1)


def _moe_ffn_kernel(te_ref, nused_ref, xs_ref, wg_ref, wu_ref, wd_ref, o_ref, acc, h_scr, *, splits):
    i = pl.program_id(0)
    f = pl.program_id(1)
    last_f = pl.num_programs(1) - 1
    used = i < nused_ref[0]

    @pl.when(used)
    def _():
        @pl.when(f == 0)
        def _():
            acc[...] = jnp.zeros(acc.shape, F32)
            h_scr[...] = _unpack_bf16_pairs(xs_ref[...]).astype(BF16)

        h = h_scr[...]
        for lo, hi in splits:
            a = _dot(h, wg_ref[0, :, lo:hi].astype(BF16))
            u = _dot(h, wu_ref[0, :, lo:hi].astype(BF16))
            acc[...] += _dot((_silu(a) * u).astype(BF16), wd_ref[0, lo:hi, :].astype(BF16))

        @pl.when(f == last_f)
        def _():
            o_ref[...] = _pack_bf16_pairs(acc[...])

    @pl.when(jnp.logical_not(used) & (f == last_f))
    def _():
        o_ref[...] = jnp.zeros(o_ref.shape, F32)


def _moe_ffn(xs, tile_expert, n_used, wg, wu, wd):
    rows, half = xs.shape
    d = 2 * half
    ff = wg.shape[2]
    if ff % MOE_FF_TILE == 0:
        tf, splits = MOE_FF_TILE, MOE_FF_SPLITS
    else:
        tf, splits = ff, ((0, ff),)
    n_tiles = rows // MOE_TILE
    grid_spec = pltpu.PrefetchScalarGridSpec(
        num_scalar_prefetch=2,
        grid=(n_tiles, ff // tf),
        in_specs=[
            pl.BlockSpec((MOE_TILE, half), lambda i, f, te, nu: (i, 0)),
            pl.BlockSpec((1, d, tf), lambda i, f, te, nu: (te[i], 0, f)),
            pl.BlockSpec((1, d, tf), lambda i, f, te, nu: (te[i], 0, f)),
            pl.BlockSpec((1, tf, d), lambda i, f, te, nu: (te[i], f, 0)),
        ],
        out_specs=pl.BlockSpec((MOE_TILE, half), lambda i, f, te, nu: (i, 0)),
        scratch_shapes=[pltpu.VMEM((MOE_TILE, d), F32), pltpu.VMEM((MOE_TILE, d), BF16)],
    )
    return pl.pallas_call(
        functools.partial(_moe_ffn_kernel, splits=splits),
        grid_spec=grid_spec,
        out_shape=jax.ShapeDtypeStruct((rows, half), F32),
        compiler_params=_cparams(("arbitrary", "arbitrary")),
        name="moe_ffn",
    )(tile_expert, n_used, xs, wg, wu, wd)


def _combine_kernel(*refs, final_norm):
    x1_ref, y0_ref, y1_ref, route_ref, mod_ref = refs[:5]
    fin_ref = refs[5] if final_norm else None
    o_ref = refs[-1]
    route = route_ref[...]
    w1 = route[:, ROUTE_W1:ROUTE_W1 + 1]
    w2 = route[:, ROUTE_W2:ROUTE_W2 + 1]
    y = w1 * _unpack_bf16_pairs(y0_ref[...]) + w2 * _unpack_bf16_pairs(y1_ref[...])
    x2 = x1_ref[...] + mod_ref[0][5:6] * y
    if final_norm:
        x2 = _rms(x2, fin_ref[...])
    o_ref[...] = x2


def _combine(x1, yg, route, mods, row_fn, fin_g, tm):
    t, d = x1.shape
    nb = t // tm
    final_norm = fin_g is not None
    in_specs = [
        pl.BlockSpec((tm, d), lambda i: (i, 0)),
        pl.BlockSpec((tm, d // 2), lambda i: (i, 0)),
        pl.BlockSpec((tm, d // 2), lambda i: (i + nb, 0)),
        pl.BlockSpec((tm, LANES), lambda i: (i, 0)),
        pl.BlockSpec((1, 6, d), lambda i: (row_fn(i), 0, 0)),
    ]
    args = [x1, yg, yg, route, mods]
    if final_norm:
        in_specs.append(pl.BlockSpec((1, d), lambda i: (0, 0)))
        args.append(fin_g)
    return pl.pallas_call(
        functools.partial(_combine_kernel, final_norm=final_norm),
        grid=(nb,),
        in_specs=in_specs,
        out_specs=pl.BlockSpec((tm, d), lambda i: (i, 0)),
        out_shape=jax.ShapeDtypeStruct((t, d), F32),
        compiler_params=_cparams(("arbitrary",)),
        name="moe_combine",
    )(*args)


def _moe(h2, x1, route, mods, row_fn, wg, wu, wd, fin_g, tm):
    t = h2.shape[0]
    n_tiles = -(-2 * t // MOE_TILE) + N_EXPERTS
    pos, tile_expert, n_used = _moe_plan(route, n_tiles, _pick_tile(t, ATTN_TILE))
    pos = pos.reshape(-1)
    xs = _sc_row_scatter2(h2, pos, n_tiles * MOE_TILE)
    ys = _moe_ffn(xs, tile_expert, n_used, wg, wu, wd)
    yg = _sc_row_gather(ys, pos)
    return _combine(x1, yg, route, mods, row_fn, fin_g, tm)


def _rope_partner():
    j = np.arange(MLA_ROPE)
    return np.where((j % 32) < 16, j + 16, j - 16)


def _prep_in_weight(w):
    d = w.shape[0]
    widths = (GLA_QK_W, GLA_QK_W, GLA_V_W, GLA_GATE_RANK, GLA_GATE_RANK, GLA_V_W,
              MLA_Q_RANK, MLA_KV_RANK, MLA_ROPE)
    offs = np.concatenate([[0], np.cumsum(widths)])
    part = lambda i: w[:, offs[i]:offs[i + 1]]
    assert w.shape[1] == offs[-1] and P_MISC + MISC_GB + GLA_GATE_RANK <= P_WIDTH
    cols = [part(0), part(1), part(2), part(5), part(6), part(7), part(8), part(3), part(4),
            jnp.zeros((d, P_WIDTH - int(offs[-1])), w.dtype)]
    return jnp.concatenate(cols, axis=1).astype(BF16)


def _prep_gate_weight(w_g2, b_g2):
    ws = []
    for z, off in ((0, MISC_GF), (1, MISC_GB)):
        ws.append(jnp.zeros((LANES, GLA_QK_W), F32).at[off:off + GLA_GATE_RANK].set(w_g2[z]))
    return jnp.stack(ws).astype(BF16), b_g2.reshape(2, 1, GLA_QK_W)


def _prep_mla_weights(w_uq, w_ukv):
    partner = _rope_partner()
    wq = w_uq.reshape(MLA_Q_RANK, MLA_HEADS, MLA_QK)
    wqn = wq[:, :, :MLA_NOPE].reshape(MLA_Q_RANK, MLA_HEADS * MLA_NOPE)
    rope = wq[:, :, MLA_NOPE:]
    pad = jnp.zeros((MLA_Q_RANK, MLA_HEADS, LANES - MLA_ROPE), w_uq.dtype)
    wqr = jnp.concatenate([rope, pad], axis=2).reshape(MLA_Q_RANK, MLA_HEADS * LANES)
    wqs = jnp.concatenate([rope[:, :, partner], pad], axis=2).reshape(MLA_Q_RANK, MLA_HEADS * LANES)
    wkv = w_ukv.reshape(MLA_KV_RANK, MLA_HEADS, MLA_NOPE + MLA_V)
    wknt = wkv[:, :, :MLA_NOPE].reshape(MLA_KV_RANK, MLA_HEADS * MLA_NOPE).T
    wv = wkv[:, :, MLA_NOPE:].reshape(MLA_KV_RANK, MLA_HEADS * MLA_V)
    perm = np.zeros((LANES, LANES), np.float32)
    perm[partner, np.arange(MLA_ROPE)] = 1.0
    eye = np.eye(MLA_ROPE, LANES, dtype=np.float32)
    return (wqn.astype(BF16), wqr.astype(BF16), wqs.astype(BF16), wknt.astype(BF16),
            wv.astype(BF16), jnp.asarray(perm, BF16), jnp.asarray(eye, BF16))


def _rope_tables(n_tok):
    rows = n_tok // GRID_W
    row = np.repeat(np.arange(rows, dtype=np.float32), GRID_W)
    col = np.tile(np.arange(GRID_W, dtype=np.float32), rows)
    nfreq = MLA_ROPE // 4
    inv = np.float32(ROPE_BASE) ** (-np.arange(nfreq, dtype=np.float32) / np.float32(nfreq))
    ar = (row[:, None] * inv).astype(np.float32)
    ac = (col[:, None] * inv).astype(np.float32)
    zero = np.zeros((n_tok, LANES - MLA_ROPE), np.float32)
    cos = np.concatenate([np.cos(ar), np.cos(ar), np.cos(ac), np.cos(ac), zero], axis=1)
    sin = np.concatenate([-np.sin(ar), np.sin(ar), -np.sin(ac), np.sin(ac), zero], axis=1)
    return jnp.asarray(cos, F32), jnp.asarray(sin, F32)


def _identity_tables(n_tok):
    cos = jnp.concatenate([jnp.ones((n_tok, MLA_ROPE), F32),
                           jnp.zeros((n_tok, LANES - MLA_ROPE), F32)], axis=1)
    return cos, jnp.zeros((n_tok, LANES), F32)


def _pick_tile(n, pref):
    t = min(n, pref)
    while n % t:
        t //= 2
    return t


def _pick_ff_tile(ff):
    best = LANES
    for m in range(1, ff // LANES + 1):
        if ff % (m * LANES) == 0 and m * LANES <= FFN_MAX_FF_TILE:
            best = m * LANES
    return best


@jax.jit
def _forward(x, c, ctx, c_ctx, w_mod, b_mod, ln1_g, ln2_g, w_in, w_gla_g2, b_gla_g2, gla_norm_g,
             mla_q_norm_g, w_uq, mla_kv_norm_g, w_ukv, w_out, ffn_w_gate, ffn_w_up, ffn_w_down,
             router_w, exp_w_gate, exp_w_up, exp_w_down, final_norm_g):
    batch, seq, d = x.shape
    n_ctx = ctx.shape[1]
    depth = w_mod.shape[0]
    assert d == D_MODEL and batch < 8, (d, batch)

    cvec = jnp.zeros((8, d), F32).at[:batch].set(c).at[batch].set(c_ctx)
    mods_all = _modulation(cvec, w_mod, b_mod).reshape(depth, 8, 6, d)

    xl = x.reshape(batch * seq, d)
    xc = ctx.reshape(batch * n_ctx, d)

    tm_l = _pick_tile(seq, TOKEN_TILE)
    tm_c = _pick_tile(n_ctx, CTX_TILE)
    tk_l = _pick_tile(seq, ATTN_TILE)
    cb_l = _pick_tile(seq, GLA_BLOCK)
    cb_c = _pick_tile(n_ctx, GLA_BLOCK)
    row_l = lambda tm: (lambda i: i // (seq // tm))
    row_c = lambda i: batch

    rope_l = _rope_tables(seq)
    rope_c = _identity_tables(tm_c)
    zero_state = jnp.zeros((2, batch, GLA_QK_W, GLA_DV), F32)

    for i in range(depth):
        need_ctx = i < depth - 1
        last = i == depth - 1
        mods = mods_all[i]
        ln1 = ln1_g[i].reshape(1, d)
        ln2 = ln2_g[i].reshape(1, d)
        w_in_r = _prep_in_weight(w_in[i])
        gates = _prep_gate_weight(w_gla_g2[i], b_gla_g2[i])
        mla_w = _prep_mla_weights(w_uq[i], w_ukv[i])
        qg = mla_q_norm_g[i].reshape(1, MLA_Q_RANK)
        kvg = mla_kv_norm_g[i].reshape(1, MLA_KV_RANK)
        gg = gla_norm_g[i].reshape(1, GLA_DV)
        wo = w_out[i].astype(BF16)

        p_l, gate_l = _inproj(xl, mods, row_l(tk_l), ln1, w_in_r, *gates, tk_l, cb_l)
        p_c, gate_c = _inproj(xc, mods, row_c, ln1, w_in_r, *gates, tm_c, cb_c)

        oc_f, oc_b, s_ctx = _gla(p_c, gate_c, *gates, zero_state, batch=batch, cb=cb_c)
        ol_f, ol_b, _ = _gla(p_l, gate_l, *gates, s_ctx, batch=batch, cb=cb_l)

        q_l, kt_l, v_l = _mlaprep(p_l, *rope_l, qg, kvg, mla_w, batch=batch, tm=tk_l)
        q_c, kt_c, v_c = _mlaprep(p_c, *rope_c, qg, kvg, mla_w, batch=batch, tm=tm_c)
        m_l = _attention_pipelined(q_l, kt_l, v_l, kt_c, v_c, tq=tk_l, n_sub=1)
        m_l = m_l.reshape(batch * seq, MLA_V_W)

        if i % 2 == 0:
            j = i // 2
            router = None
            wg = ffn_w_gate[j].astype(BF16)
            wu = ffn_w_up[j].astype(BF16)
            wd = ffn_w_down[j].astype(BF16)
        else:
            j = i // 2
            rw = jnp.zeros((d, LANES), F32).at[:, :N_EXPERTS].set(router_w[j])
            rw_hi = rw.astype(BF16)
            router = (rw_hi, (rw - rw_hi.astype(F32)).astype(BF16))
            wg, wu, wd = exp_w_gate[j], exp_w_up[j], exp_w_down[j]
        fin = final_norm_g.reshape(1, d) if last else None

        if router is None:
            xl = _outproj_ffn(xl, ol_f, ol_b, p_l, m_l, mods, row_l(tm_l), gg, wo, ln2,
                              wg, wu, wd, fin, tm_l)
        else:
            outs = _outproj(xl, ol_f, ol_b, p_l, m_l, mods, row_l(tm_l), gg, wo, ln2, router, tm_l)
            xl = _moe(outs[1], outs[0], outs[2], mods, row_l(tk_l), wg, wu, wd, fin, tk_l)

        if need_ctx:
            m_c = _attention(q_c, [(kt_c, v_c)], tq=tm_c, n_sub=1).reshape(batch * n_ctx, MLA_V_W)
            if router is None:
                xc = _outproj_ffn(xc, oc_f, oc_b, p_c, m_c, mods, row_c, gg, wo, ln2,
                                  wg, wu, wd, None, _pick_tile(batch * n_ctx, TOKEN_TILE))
            else:
                outs_c = _outproj(xc, oc_f, oc_b, p_c, m_c, mods, row_c, gg, wo, ln2, router, tm_c)
                tm_fc = _pick_tile(batch * n_ctx, TOKEN_TILE)
                r_c = outs_c[2]
                lane = jnp.arange(LANES, dtype=F32)[None, :]
                comb_c = (jnp.where(lane == r_c[:, ROUTE_E1:ROUTE_E1 + 1], r_c[:, ROUTE_W1:ROUTE_W1 + 1], 0.0)
                          + jnp.where(lane == r_c[:, ROUTE_E2:ROUTE_E2 + 1], r_c[:, ROUTE_W2:ROUTE_W2 + 1], 0.0))
                bits = lax.bitcast_convert_type(outs_c[1], jnp.uint32)
                h2_c = jnp.concatenate([lax.bitcast_convert_type(bits << 16, F32),
                                        lax.bitcast_convert_type(bits & jnp.uint32(0xFFFF0000), F32)], axis=1)
                xc = _ffn(h2_c.astype(BF16), outs_c[0], mods, row_c, comb_c,
                          wg.astype(BF16), wu.astype(BF16), wd.astype(BF16), None, tm_fc,
                          _pick_ff_tile(wg.shape[2]))

    return xl.reshape(batch, seq, d)


def kernel(x, c, ctx, c_ctx, w_mod, b_mod, ln1_g, ln2_g, w_in, w_gla_g2, b_gla_g2, gla_norm_g,
           mla_q_norm_g, w_uq, mla_kv_norm_g, w_ukv, w_out, ffn_w_gate, ffn_w_up, ffn_w_down,
           router_w, exp_w_gate, exp_w_up, exp_w_down, final_norm_g):
    return _forward(x, c, ctx, c_ctx, w_mod, b_mod, ln1_g, ln2_g, w_in, w_gla_g2, b_gla_g2,
                    gla_norm_g, mla_q_norm_g, w_uq, mla_kv_norm_g, w_ukv, w_out, ffn_w_gate,
                    ffn_w_up, ffn_w_down, router_w, exp_w_gate, exp_w_up, exp_w_down, final_norm_g)
```

```python
import functools

import numpy as np
import jax
import jax.numpy as jnp
from jax import lax
from jax.experimental import pallas as pl
from jax.experimental.pallas import tpu as pltpu
from jax.experimental.pallas import tpu_sc as plsc

F32 = jnp.float32
BF16 = jnp.bfloat16

D_MODEL = 1024
EPS = 1e-6
GRID_W = 64

GLA_HEADS = 4
GLA_DK = 64
GLA_DV = 128
GLA_GATE_RANK = 16
GLA_GATE_NORM = 16.0
GLA_CHUNK = 64
GLA_QK_W = GLA_HEADS * GLA_DK
GLA_V_W = GLA_HEADS * GLA_DV
GLA_EXP_CLAMP = 80.0
GLA_PRE_SAFE = 29.0

MLA_HEADS = 4
MLA_NOPE = 128
MLA_ROPE = 64
MLA_V = 128
MLA_QK = MLA_NOPE + MLA_ROPE
MLA_Q_RANK = 256
MLA_KV_RANK = 128
MLA_SCALE = MLA_QK ** -0.5
MLA_Q_SCALE = MLA_SCALE * 1.4426950408889634
MLA_V_W = MLA_HEADS * MLA_V
MLA_V_EXT = 2 * MLA_V
ROPE_BASE = 10000.0

N_EXPERTS = 8
LANES = 128
ROUTE_E1, ROUTE_E2, ROUTE_W1, ROUTE_W2 = 0, 1, 2, 3

SC_CORES = 2
SC_SUBCORES = 16
SC_GATHER_ROWS = 64
MOE_TILE = 1024
MOE_FF_TILE = 512
MOE_FF_SPLITS = ((0, 256), (256, 512))

P_Q, P_K, P_V, P_R, P_CQ, P_CKV, P_MISC = 0, 256, 512, 1024, 1536, 1792, 1920
P_WIDTH = 2048
MISC_KR, MISC_GF, MISC_GB = 0, 64, 80

VMEM_LIMIT = 56 * 1024 * 1024

TOKEN_TILE = 512
CTX_TILE = 256
ATTN_TILE = 1024
GLA_BLOCK = 256
DENSE_FF_GROUP = 1024
FFN_MAX_FF_TILE = 1408


def _cparams(sem):
    return pltpu.CompilerParams(dimension_semantics=sem, vmem_limit_bytes=VMEM_LIMIT)


def _rms(x, g):
    return x * lax.rsqrt(jnp.mean(x * x, axis=-1, keepdims=True) + EPS) * g


def _silu(x):
    return x / (1.0 + jnp.exp(-x))


def _dot(a, b):
    return jnp.dot(a, b, preferred_element_type=F32)


def _dot_nt(a, b):
    return lax.dot_general(a, b, (((1,), (1,)), ((), ())), preferred_element_type=F32)


def _dot_tn(a, b):
    return lax.dot_general(a, b, (((0,), (0,)), ((), ())), preferred_element_type=F32)


def _pack_bf16_pairs(x):
    w = x.shape[1] // 2
    words = pltpu.pack_elementwise([x[:, :w], x[:, w:]], packed_dtype=BF16)
    return lax.bitcast_convert_type(words, F32)


def _unpack_bf16_pairs(p):
    words = lax.bitcast_convert_type(p, jnp.int32)
    lo = pltpu.unpack_elementwise(words, index=0, packed_dtype=BF16, unpacked_dtype=F32)
    hi = pltpu.unpack_elementwise(words, index=1, packed_dtype=BF16, unpacked_dtype=F32)
    return jnp.concatenate([lo, hi], axis=1)


def _mod_kernel(c_ref, w_ref, b_ref, o_ref):
    s = _silu(c_ref[...]).astype(BF16)
    o_ref[0] = _dot(s, w_ref[0].astype(BF16)) + b_ref[0]


def _modulation(cvec, w_mod, b_mod):
    depth, d, n = w_mod.shape
    tn = 1536
    return pl.pallas_call(
        _mod_kernel,
        grid=(depth, n // tn),
        in_specs=[
            pl.BlockSpec((8, d), lambda l, j: (0, 0)),
            pl.BlockSpec((1, d, tn), lambda l, j: (l, 0, j)),
            pl.BlockSpec((1, 1, tn), lambda l, j: (l, 0, j)),
        ],
        out_specs=pl.BlockSpec((1, 8, tn), lambda l, j: (l, 0, j)),
        out_shape=jax.ShapeDtypeStruct((depth, 8, n), F32),
        compiler_params=_cparams(("arbitrary", "arbitrary")),
        name="modulation",
    )(cvec, w_mod, b_mod.reshape(depth, 1, n))


def _inproj_kernel(x_ref, mod_ref, g_ref, w_ref, wg_ref, bg_ref, o_ref, gate_ref, *, gate_block):
    m = mod_ref[0]
    tm = x_ref.shape[0]
    n_groups = 2 if tm % (2 * gate_block) == 0 else 1
    rows_per_group = tm // n_groups
    sub = lax.broadcasted_iota(jnp.int32, (8, LANES), 0)
    for g in range(n_groups):
        rows = slice(g * rows_per_group, (g + 1) * rows_per_group)
        h = _rms(x_ref[rows, :], g_ref[...]) * (1.0 + m[1:2]) + m[0:1]
        p = _dot(h.astype(BF16), w_ref[...]).astype(BF16)
        o_ref[rows, :] = p
        misc = p[:, P_MISC:P_MISC + LANES]
        neg = [-(_dot(misc, wg_ref[z]) + bg_ref[z]) for z in range(2)]
        for j in range(rows_per_group // gate_block):
            blk_rows = slice(j * gate_block, (j + 1) * gate_block)
            worst = [jnp.max(jnp.max(n[blk_rows], axis=0, keepdims=True), axis=1, keepdims=True)
                     for n in neg]
            gate_ref[g * (rows_per_group // gate_block) + j] = jnp.where(
                sub == 0, worst[0], jnp.where(sub == 1, worst[1], 0.0))


def _inproj(x, mods, row_fn, ln_g, w, wg, bg, tm, gate_block):
    t, d = x.shape
    assert tm % gate_block == 0
    full = lambda a: pl.BlockSpec(a.shape, lambda i: (0,) * a.ndim)
    return pl.pallas_call(
        functools.partial(_inproj_kernel, gate_block=gate_block),
        grid=(t // tm,),
        in_specs=[
            pl.BlockSpec((tm, d), lambda i: (i, 0)),
            pl.BlockSpec((1, 6, d), lambda i: (row_fn(i), 0, 0)),
            pl.BlockSpec((1, d), lambda i: (0, 0)),
            pl.BlockSpec((d, P_WIDTH), lambda i: (0, 0)),
            full(wg), full(bg),
        ],
        out_specs=[pl.BlockSpec((tm, P_WIDTH), lambda i: (i, 0)),
                   pl.BlockSpec((tm // gate_block, 8, LANES), lambda i: (i, 0, 0))],
        out_shape=[jax.ShapeDtypeStruct((t, P_WIDTH), BF16),
                   jax.ShapeDtypeStruct((t // gate_block, 8, LANES), F32)],
        compiler_params=_cparams(("arbitrary",)),
        name="inproj",
    )(x, mods, ln_g, w, wg, bg)


def _gla_direction(q_ref, k_ref, v_ref, pre, tri, s_scr, o_ref, *, reverse, n_chunks, exact):
    c_len = GLA_CHUNK
    g = (jnp.minimum(pre, 0.0) - jnp.log(1.0 + jnp.exp(-jnp.abs(pre)))) * (1.0 / GLA_GATE_NORM)
    g_hi = g.astype(BF16)
    g_lo = (g - g_hi.astype(F32)).astype(BF16)
    cum = _dot(tri, g_hi) + _dot(tri, g_lo)
    tot_rows = jnp.concatenate(
        [cum[c * c_len:c * c_len + 1] if reverse else cum[(c + 1) * c_len - 1:(c + 1) * c_len]
         for c in range(n_chunks)] + [jnp.zeros((8 - n_chunks, GLA_QK_W), F32)], axis=0)
    t_hi = tot_rows.astype(BF16)
    t_lo = (tot_rows - t_hi.astype(F32)).astype(BF16)
    eye = (lax.broadcasted_iota(jnp.int32, (GLA_QK_W, GLA_QK_W), 0)
           == lax.broadcasted_iota(jnp.int32, (GLA_QK_W, GLA_QK_W), 1))
    eye = jnp.where(eye, 1.0, 0.0).astype(BF16)
    tot_cols = _dot_nt(eye, t_hi) + _dot_nt(eye, t_lo)

    lane = lax.broadcasted_iota(jnp.int32, (c_len, GLA_QK_W), 1)
    head_masks = [(lane >= h * GLA_DK) & (lane < (h + 1) * GLA_DK) for h in range(GLA_HEADS)]
    row = lax.broadcasted_iota(jnp.int32, (GLA_HEADS * c_len, c_len), 0) % c_len
    col = lax.broadcasted_iota(jnp.int32, (GLA_HEADS * c_len, c_len), 1)
    pair_mask = (col >= row) if reverse else (col <= row)

    def stack_heads(a):
        return jnp.concatenate([jnp.where(mk, a, 0.0) for mk in head_masks], axis=0).astype(BF16)

    def exact_levels():
        n = n_chunks * c_len
        r = lax.broadcasted_iota(jnp.int32, (n, n), 0)
        u = lax.broadcasted_iota(jnp.int32, (n, n), 1)
        same_chunk = (r // c_len) == (u // c_len)
        t = r % c_len
        uu = u % c_len
        levels = []
        b = c_len
        while b >= 2:
            half = b // 2
            mid = (t // b) * b + half
            same = same_chunk & ((t // b) == (uu // b))
            if reverse:
                sel_q = same & (t < mid) & (uu >= t) & (uu < mid)
                sel_k = same & (t >= mid) & (uu >= mid) & (uu < t)
            else:
                sel_q = same & (t >= mid) & (uu >= mid) & (uu <= t)
                sel_k = same & (t < mid) & (uu > t) & (uu < mid)
            mq = jnp.where(sel_q, 1.0, 0.0).astype(BF16)
            mk_ = jnp.where(sel_k, 1.0, 0.0).astype(BF16)
            e_q = _dot(mq, g_hi) + _dot(mq, g_lo)
            e_k = _dot(mk_, g_hi) + _dot(mk_, g_lo)
            tq_ = row
            if reverse:
                own = ((tq_ // b) == (col // b)) & ((tq_ % b) < half) & ((col % b) >= half)
            else:
                own = ((tq_ // b) == (col // b)) & ((tq_ % b) >= half) & ((col % b) < half)
            levels.append((e_q, e_k, own))
            b = half
        return levels

    levels = exact_levels() if exact else None

    def intra_scores(sl, xc, qc, kc):
        if not exact:
            ref = xc[c_len // 2:c_len // 2 + 1]
            q_mid = qc * jnp.exp(jnp.minimum(xc - ref, GLA_EXP_CLAMP))
            k_mid = (kc * jnp.exp(jnp.minimum(ref - xc, GLA_EXP_CLAMP))).astype(BF16)
            return jnp.where(pair_mask, _dot_nt(stack_heads(q_mid), k_mid), 0.0)
        attn = jnp.where(row == col, _dot_nt(stack_heads(qc), kc.astype(BF16)), 0.0)
        for e_q, e_k, own in levels:
            a = _dot_nt(stack_heads(qc * jnp.exp(e_q[sl])), (kc * jnp.exp(e_k[sl])).astype(BF16))
            attn = attn + jnp.where(own, a, 0.0)
        return attn

    def step(c):
        sl = slice(c * c_len, (c + 1) * c_len)
        xc = cum[sl]
        tot = tot_rows[c:c + 1]
        qc = q_ref[sl, :].astype(F32) * (GLA_DK ** -0.5)
        kc = k_ref[sl, :].astype(F32)
        vc = v_ref[sl, :]
        q_dec = qc * jnp.exp(xc)
        k_dec = kc * jnp.exp(tot - xc)

        attn = intra_scores(sl, xc, qc, kc).astype(BF16)
        s_prev = s_scr[...]
        o_inter = _dot(stack_heads(q_dec), s_prev.astype(BF16))
        kv = []
        for h in range(GLA_HEADS):
            rs = slice(h * c_len, (h + 1) * c_len)
            vs = slice(h * GLA_DV, (h + 1) * GLA_DV)
            o_h = o_inter[rs] + _dot(attn[rs], vc[:, vs])
            o_ref[sl, vs] = o_h.astype(BF16)
            kv.append(_dot_tn(k_dec[:, h * GLA_DK:(h + 1) * GLA_DK].astype(BF16), vc[:, vs]))
        s_scr[...] = s_prev * jnp.exp(tot_cols[:, c:c + 1]) + jnp.concatenate(kv, axis=0)

    return step


def _gla_kernel(gate_ref, qf_ref, kf_ref, vf_ref, mf_ref, qb_ref, kb_ref, vb_ref, mb_ref,
                wg_ref, bg_ref, tri_ref, s0_ref, of_ref, ob_ref, sfin_ref, sf_scr, sb_scr, *, n_chunks):
    blk = pl.program_id(1)
    n_blk = pl.num_programs(1)

    @pl.when(blk == 0)
    def _():
        sf_scr[...] = s0_ref[0, 0]
        sb_scr[...] = s0_ref[1, 0]

    fwd_block = pl.program_id(0) * n_blk + blk
    bwd_block = pl.program_id(0) * n_blk + (n_blk - 1 - blk)
    extreme = jnp.maximum(gate_ref[2 * fwd_block], gate_ref[2 * bwd_block + 1]) > GLA_PRE_SAFE

    def run(exact):
        pre_f = _dot(mf_ref[...], wg_ref[0]) + bg_ref[0]
        pre_b = _dot(mb_ref[...], wg_ref[1]) + bg_ref[1]
        fwd = _gla_direction(qf_ref, kf_ref, vf_ref, pre_f, tri_ref[0], sf_scr, of_ref,
                             reverse=False, n_chunks=n_chunks, exact=exact)
        bwd = _gla_direction(qb_ref, kb_ref, vb_ref, pre_b, tri_ref[1], sb_scr, ob_ref,
                             reverse=True, n_chunks=n_chunks, exact=exact)
        for c in range(n_chunks):
            fwd(c)
            bwd(n_chunks - 1 - c)

    @pl.when(jnp.logical_not(extreme))
    def _():
        run(False)

    @pl.when(extreme)
    def _():
        run(True)

    @pl.when(blk == pl.num_programs(1) - 1)
    def _():
        sfin_ref[0, 0] = sf_scr[...]
        sfin_ref[1, 0] = sb_scr[...]


def _block_diag_tri(n_chunks):
    c = GLA_CHUNK
    eye = np.eye(n_chunks, dtype=np.float32)
    lower = np.kron(eye, np.tril(np.ones((c, c), np.float32)))
    upper = np.kron(eye, np.triu(np.ones((c, c), np.float32)))
    return jnp.asarray(np.stack([lower, upper]), dtype=BF16)


def _gla(p, gate_bound, wg, bg, s0, *, batch, cb):
    t_all = p.shape[0]
    nblk = t_all // batch // cb
    n_chunks = cb // GLA_CHUNK
    assert n_chunks <= 8 and gate_bound.shape[0] == batch * nblk
    gate = gate_bound[:, 0:2, 0].reshape(-1)

    fw = lambda b, i, gate_ref: b * nblk + i
    bw = lambda b, i, gate_ref: b * nblk + (nblk - 1 - i)
    full = lambda a: pl.BlockSpec(a.shape, lambda b, i, gate_ref: (0,) * a.ndim)
    tri = _block_diag_tri(n_chunks)

    def token_specs(tok):
        return [
            pl.BlockSpec((cb, GLA_QK_W), lambda b, i, g: (tok(b, i, g), P_Q // GLA_QK_W)),
            pl.BlockSpec((cb, GLA_QK_W), lambda b, i, g: (tok(b, i, g), P_K // GLA_QK_W)),
            pl.BlockSpec((cb, GLA_V_W), lambda b, i, g: (tok(b, i, g), P_V // GLA_V_W)),
            pl.BlockSpec((cb, LANES), lambda b, i, g: (tok(b, i, g), P_MISC // LANES)),
        ]

    state_spec = pl.BlockSpec((2, 1, GLA_QK_W, GLA_DV), lambda b, i, g: (0, b, 0, 0))
    grid_spec = pltpu.PrefetchScalarGridSpec(
        num_scalar_prefetch=1,
        grid=(batch, nblk),
        in_specs=token_specs(fw) + token_specs(bw) + [full(wg), full(bg), full(tri), state_spec],
        out_specs=[
            pl.BlockSpec((cb, GLA_V_W), lambda b, i, g: (fw(b, i, g), 0)),
            pl.BlockSpec((cb, GLA_V_W), lambda b, i, g: (bw(b, i, g), 0)),
            state_spec,
        ],
        scratch_shapes=[pltpu.VMEM((GLA_QK_W, GLA_DV), F32), pltpu.VMEM((GLA_QK_W, GLA_DV), F32)],
    )
    return pl.pallas_call(
        functools.partial(_gla_kernel, n_chunks=n_chunks),
        grid_spec=grid_spec,
        out_shape=[
            jax.ShapeDtypeStruct((t_all, GLA_V_W), BF16),
            jax.ShapeDtypeStruct((t_all, GLA_V_W), BF16),
            jax.ShapeDtypeStruct((2, batch, GLA_QK_W, GLA_DV), F32),
        ],
        compiler_params=_cparams(("arbitrary", "arbitrary")),
        name="gla",
    )(gate, p, p, p, p, p, p, p, p, wg, bg, tri, s0)


def _mlaprep_kernel(cq_ref, ckv_ref, misc_ref, cos_ref, sin_ref, qg_ref, kvg_ref,
                    wqn_ref, wqr_ref, wqs_ref, wknt_ref, wv_ref, perm_ref, eye_ref,
                    q_ref, kt_ref, v_ref):
    tm = cq_ref.shape[0]
    n_groups = 2 if tm % 256 == 0 else 1
    for g in range(n_groups):
        rows = slice(g * tm // n_groups, (g + 1) * tm // n_groups)
        cos = cos_ref[rows, :]
        sin = sin_ref[rows, :]
        cqn = _rms(cq_ref[rows, :].astype(F32), qg_ref[...]).astype(BF16)
        qn = _dot(cqn, wqn_ref[...])
        qr = _dot(cqn, wqr_ref[...])
        qs = _dot(cqn, wqs_ref[...])
        for h in range(MLA_HEADS):
            ls = slice(h * LANES, (h + 1) * LANES)
            q_ref[0, h, rows, 0:MLA_NOPE] = (qn[:, ls] * MLA_Q_SCALE).astype(BF16)
            rot = qr[:, ls] * cos + qs[:, ls] * sin
            q_ref[0, h, rows, MLA_NOPE:MLA_QK] = (rot[:, 0:MLA_ROPE] * MLA_Q_SCALE).astype(BF16)

        ckvn = _rms(ckv_ref[rows, :].astype(F32), kvg_ref[...]).astype(BF16)
        knt = _dot_nt(wknt_ref[...], ckvn)
        vv = _dot(ckvn, wv_ref[...])
        misc = misc_ref[rows, :]
        kr = misc.astype(F32) * cos + _dot(misc, perm_ref[...]) * sin
        krt = _dot_nt(eye_ref[...], kr.astype(BF16)).astype(BF16)
        for h in range(MLA_HEADS):
            kt_ref[0, h, 0, 0:MLA_NOPE, rows] = knt[h * MLA_NOPE:(h + 1) * MLA_NOPE].astype(BF16)
            kt_ref[0, h, 0, MLA_NOPE:MLA_QK, rows] = krt
            v_ref[0, h, rows, 0:MLA_V] = vv[:, h * MLA_V:(h + 1) * MLA_V].astype(BF16)
            v_ref[0, h, rows, MLA_V:MLA_V_EXT] = jnp.ones((vv.shape[0], MLA_V), BF16)


def _mlaprep(p, cos, sin, qg, kvg, wts, *, batch, tm):
    t_all = p.shape[0]
    t = t_all // batch
    nb = t // tm
    ntab = cos.shape[0] // tm
    wqn, wqr, wqs, wknt, wv, perm, eye = wts
    full = lambda a: pl.BlockSpec(a.shape, lambda b, i: (0,) * a.ndim)
    return pl.pallas_call(
        _mlaprep_kernel,
        grid=(batch, nb),
        in_specs=[
            pl.BlockSpec((tm, MLA_Q_RANK), lambda b, i: (b * nb + i, P_CQ // MLA_Q_RANK)),
            pl.BlockSpec((tm, MLA_KV_RANK), lambda b, i: (b * nb + i, P_CKV // MLA_KV_RANK)),
            pl.BlockSpec((tm, LANES), lambda b, i: (b * nb + i, P_MISC // LANES)),
            pl.BlockSpec((tm, LANES), lambda b, i: (i % ntab, 0)),
            pl.BlockSpec((tm, LANES), lambda b, i: (i % ntab, 0)),
            full(qg), full(kvg), full(wqn), full(wqr), full(wqs), full(wknt), full(wv),
            full(perm), full(eye),
        ],
        out_specs=[
            pl.BlockSpec((1, MLA_HEADS, tm, MLA_QK), lambda b, i: (b, 0, i, 0)),
            pl.BlockSpec((1, MLA_HEADS, 1, MLA_QK, tm), lambda b, i: (b, 0, i, 0, 0)),
            pl.BlockSpec((1, MLA_HEADS, tm, MLA_V_EXT), lambda b, i: (b, 0, i, 0)),
        ],
        out_shape=[
            jax.ShapeDtypeStruct((batch, MLA_HEADS, t, MLA_QK), BF16),
            jax.ShapeDtypeStruct((batch, MLA_HEADS, nb, MLA_QK, tm), BF16),
            jax.ShapeDtypeStruct((batch, MLA_HEADS, t, MLA_V_EXT), BF16),
        ],
        compiler_params=_cparams(("arbitrary", "arbitrary")),
        name="mlaprep",
    )(p, p, p, cos, sin, qg, kvg, wqn, wqr, wqs, wknt, wv, perm, eye)


def _attn_kernel(*refs, n_seg, n_sub):
    q_ref = refs[0]
    kt_refs = refs[1:1 + 2 * n_seg:2]
    v_refs = refs[2:2 + 2 * n_seg:2]
    o_ref = refs[1 + 2 * n_seg]
    m_scr, acc_scr = refs[2 + 2 * n_seg:]

    rows_per_sub = q_ref.shape[2] // n_sub
    m_scr[...] = jnp.full(m_scr.shape, -jnp.inf, F32)
    acc_scr[...] = jnp.zeros(acc_scr.shape, F32)

    for kt_ref, v_ref in zip(kt_refs, v_refs):
        n_blocks, tk = kt_ref.shape[2], kt_ref.shape[4]

        def step(j, carry, kt_ref=kt_ref, v_ref=v_ref, tk=tk):
            kt = kt_ref[0, 0, j]
            v_blk = v_ref[0, 0, pl.ds(pl.multiple_of(j * tk, tk), tk), :]
            for u in range(n_sub):
                rows = slice(u * rows_per_sub, (u + 1) * rows_per_sub)
                s = _dot(q_ref[0, 0, rows, :], kt)
                m_prev = m_scr[rows, :]
                m_next = jnp.maximum(m_prev, jnp.max(s, axis=1, keepdims=True))
                p = jnp.exp2((s - jnp.concatenate([m_next] * (tk // LANES), axis=1)).astype(BF16))
                alpha = jnp.exp2(m_prev - m_next)
                acc_scr[rows, :] = (jnp.concatenate([alpha] * (MLA_V_EXT // LANES), axis=1)
                                    * acc_scr[rows, :] + _dot(p, v_blk))
                m_scr[rows, :] = m_next
            return carry

        lax.fori_loop(0, n_blocks, step, 0)

    o_ref[0] = (acc_scr[:, 0:MLA_V] / acc_scr[:, MLA_V:MLA_V_EXT]).astype(BF16)


def _attn_pipe_kernel(q_ref, kt_ref, v_ref, ktt_ref, vt_ref, o_ref,
                      m_scr, acc_scr, s0_scr, s1_scr, st_scr, *, n_sub):
    n_blocks, tk = kt_ref.shape[2], kt_ref.shape[4]
    tq = m_scr.shape[0]
    n_q = q_ref.shape[2] // tq
    rows_per_sub = tq // n_sub
    subs = [slice(u * rows_per_sub, (u + 1) * rows_per_sub) for u in range(n_sub)]
    bufs = (s0_scr, s1_scr)

    def reset():
        m_scr[...] = jnp.full(m_scr.shape, -jnp.inf, F32)
        acc_scr[...] = jnp.zeros(acc_scr.shape, F32)

    def scores(qi, kt, s_ref):
        for rows in subs:
            q_rows = pl.ds(pl.multiple_of(qi * tq + rows.start, rows_per_sub), rows_per_sub)
            s_ref[rows, :] = _dot(q_ref[0, 0, q_rows, :], kt)

    def softmax_pv(s_ref, v_blk):
        width = s_ref.shape[1]
        for rows in subs:
            s = s_ref[rows, :]
            m_prev = m_scr[rows, :]
            m_next = jnp.maximum(m_prev, jnp.max(s, axis=1, keepdims=True))
            m_scr[rows, :] = m_next
            alpha = jnp.exp2(m_prev - m_next)
            p = jnp.exp2((s - jnp.concatenate([m_next] * (width // LANES), axis=1)).astype(BF16))
            acc_scr[rows, :] = (jnp.concatenate([alpha] * (MLA_V_EXT // LANES), axis=1)
                                * acc_scr[rows, :] + _dot(p, v_blk))

    reset()
    scores(0, kt_ref[0, 0, 0], bufs[0])

    def query_block(qi, carry):
        for j in range(n_blocks):
            softmax_pv(bufs[j % 2], v_ref[0, 0, j * tk:(j + 1) * tk, :])
            if j + 1 < n_blocks:
                scores(qi, kt_ref[0, 0, j + 1], bufs[(j + 1) % 2])
            else:
                scores(qi, ktt_ref[0, 0, 0], st_scr)
        softmax_pv(st_scr, vt_ref[0, 0])
        scores(jnp.minimum(qi + 1, n_q - 1), kt_ref[0, 0, 0], bufs[0])
        o_rows = pl.ds(pl.multiple_of(qi * tq, tq), tq)
        o_ref[0, o_rows, :] = (acc_scr[:, 0:MLA_V] / acc_scr[:, MLA_V:MLA_V_EXT]).astype(BF16)
        reset()
        return carry

    lax.fori_loop(0, n_q, query_block, 0)


def _attention_pipelined(q, kt, v, kt_tail, v_tail, *, tq, n_sub):
    b, h, t, dqk = q.shape
    tk, tt = kt.shape[4], kt_tail.shape[4]
    assert kt_tail.shape[2] == 1
    return pl.pallas_call(
        functools.partial(_attn_pipe_kernel, n_sub=n_sub),
        grid=(b, h),
        in_specs=[
            pl.BlockSpec((1, 1, t, dqk), lambda bi, hi: (bi, hi, 0, 0)),
            pl.BlockSpec((1, 1) + kt.shape[2:], lambda bi, hi: (bi, hi, 0, 0, 0)),
            pl.BlockSpec((1, 1) + v.shape[2:], lambda bi, hi: (bi, hi, 0, 0)),
            pl.BlockSpec((1, 1) + kt_tail.shape[2:], lambda bi, hi: (bi, hi, 0, 0, 0)),
            pl.BlockSpec((1, 1) + v_tail.shape[2:], lambda bi, hi: (bi, hi, 0, 0)),
        ],
        out_specs=pl.BlockSpec((1, t, MLA_V), lambda bi, hi: (bi, 0, hi)),
        out_shape=jax.ShapeDtypeStruct((b, t, h * MLA_V), BF16),
        scratch_shapes=[pltpu.VMEM((tq, LANES), F32), pltpu.VMEM((tq, MLA_V_EXT), F32),
                        pltpu.VMEM((tq, tk), F32), pltpu.VMEM((tq, tk), F32),
                        pltpu.VMEM((tq, tt), F32)],
        compiler_params=_cparams(("arbitrary", "arbitrary")),
        name="mla_attention_pipe",
    )(q, kt, v, kt_tail, v_tail)


def _attention(q, segs, *, tq, n_sub):
    b, h, t, dqk = q.shape
    in_specs = [pl.BlockSpec((1, 1, tq, dqk), lambda bi, hi, qi: (bi, hi, qi, 0))]
    args = [q]
    for kt, v in segs:
        in_specs.append(pl.BlockSpec((1, 1) + kt.shape[2:], lambda bi, hi, qi: (bi, hi, 0, 0, 0)))
        in_specs.append(pl.BlockSpec((1, 1) + v.shape[2:], lambda bi, hi, qi: (bi, hi, 0, 0)))
        args += [kt, v]
    return pl.pallas_call(
        functools.partial(_attn_kernel, n_seg=len(segs), n_sub=n_sub),
        grid=(b, h, t // tq),
        in_specs=in_specs,
        out_specs=pl.BlockSpec((1, tq, MLA_V), lambda bi, hi, qi: (bi, qi, hi)),
        out_shape=jax.ShapeDtypeStruct((b, t, h * MLA_V), BF16),
        scratch_shapes=[pltpu.VMEM((tq, LANES), F32), pltpu.VMEM((tq, MLA_V_EXT), F32)],
        compiler_params=_cparams(("arbitrary", "arbitrary", "arbitrary")),
        name="mla_attention",
    )(*args)


def _mix_residual_norm(x_ref, of_ref, ob_ref, r_ref, mla_ref, m, gg_ref, wo_ref, ln2_ref,
                       rows=slice(None)):
    o = of_ref[rows, :].astype(F32) + ob_ref[rows, :].astype(F32)
    gg = gg_ref[...]
    y = jnp.concatenate(
        [_rms(o[:, h * GLA_DV:(h + 1) * GLA_DV], gg) for h in range(GLA_HEADS)], axis=1)
    mix = (y * _silu(r_ref[rows, :].astype(F32))).astype(BF16)
    yo = _dot(mix, wo_ref[0:GLA_V_W, :]) + _dot(mla_ref[rows, :], wo_ref[GLA_V_W:, :])
    x1 = x_ref[rows, :] + m[2:3] * yo
    h2 = _rms(x1, ln2_ref[...]) * (1.0 + m[4:5]) + m[3:4]
    return x1, h2


def _outproj_ffn_kernel(*refs, groups, final_norm):
    (x_ref, of_ref, ob_ref, r_ref, mla_ref, mod_ref, gg_ref, wo_ref, ln2_ref,
     wg_ref, wu_ref, wd_ref) = refs[:12]
    fin_ref = refs[12] if final_norm else None
    o_ref = refs[-1]
    m = mod_ref[0]
    x1, h2 = _mix_residual_norm(x_ref, of_ref, ob_ref, r_ref, mla_ref, m, gg_ref, wo_ref, ln2_ref)
    h = h2.astype(BF16)
    y = None
    for lo, hi in groups:
        a = _dot(h, wg_ref[:, lo:hi])
        u = _dot(h, wu_ref[:, lo:hi])
        part = _dot((_silu(a) * u).astype(BF16), wd_ref[lo:hi, :])
        y = part if y is None else y + part
    x2 = x1 + m[5:6] * y
    if final_norm:
        x2 = _rms(x2, fin_ref[...])
    o_ref[...] = x2


def _outproj_ffn(x, o_f, o_b, p, mla, mods, row_fn, gg, wo, ln2, wg, wu, wd, fin_g, tm):
    t, d = x.shape
    ff = wg.shape[1]
    final_norm = fin_g is not None
    step = min(ff, DENSE_FF_GROUP)
    groups = tuple((lo, min(lo + step, ff)) for lo in range(0, ff, step))
    resident = lambda a: pl.BlockSpec(a.shape, lambda i: (0,) * a.ndim, pipeline_mode=pl.Buffered(1))
    in_specs = [
        pl.BlockSpec((tm, d), lambda i: (i, 0)),
        pl.BlockSpec((tm, GLA_V_W), lambda i: (i, 0)),
        pl.BlockSpec((tm, GLA_V_W), lambda i: (i, 0)),
        pl.BlockSpec((tm, GLA_V_W), lambda i: (i, P_R // GLA_V_W)),
        pl.BlockSpec((tm, MLA_V_W), lambda i: (i, 0)),
        pl.BlockSpec((1, 6, d), lambda i: (row_fn(i), 0, 0)),
        resident(gg), resident(wo), resident(ln2), resident(wg), resident(wu), resident(wd),
    ]
    args = [x, o_f, o_b, p, mla, mods, gg, wo, ln2, wg, wu, wd]
    if final_norm:
        in_specs.append(resident(fin_g))
        args.append(fin_g)
    return pl.pallas_call(
        functools.partial(_outproj_ffn_kernel, groups=groups, final_norm=final_norm),
        grid=(t // tm,),
        in_specs=in_specs,
        out_specs=pl.BlockSpec((tm, d), lambda i: (i, 0)),
        out_shape=jax.ShapeDtypeStruct((t, d), F32),
        compiler_params=_cparams(("arbitrary",)),
        name="outproj_ffn",
    )(*args)


def _outproj_kernel(*refs, with_router):
    (x_ref, of_ref, ob_ref, r_ref, mla_ref, mod_ref, gg_ref, wo_ref, ln2_ref) = refs[:9]
    if with_router:
        rwh_ref, rwl_ref, x1_ref, h2_ref, comb_ref = refs[9:]
    else:
        x1_ref, h2_ref = refs[9:]
    m = mod_ref[0]
    tm = x_ref.shape[0]
    n_groups = 2 if tm % 32 == 0 else 1
    for g in range(n_groups):
        rows = slice(g * tm // n_groups, (g + 1) * tm // n_groups)
        x1, h2 = _mix_residual_norm(x_ref, of_ref, ob_ref, r_ref, mla_ref, m, gg_ref, wo_ref,
                                    ln2_ref, rows)
        x1_ref[rows, :] = x1
        if not with_router:
            h2_ref[rows, :] = h2.astype(BF16)
            continue
        h2_ref[rows, :] = _pack_bf16_pairs(h2)
        h_hi = h2.astype(BF16)
        h_lo = (h2 - h_hi.astype(F32)).astype(BF16)
        logits = _dot(h_hi, rwh_ref[...]) + _dot(h_lo, rwh_ref[...]) + _dot(h_hi, rwl_ref[...])
        lane = lax.broadcasted_iota(jnp.int32, logits.shape, 1).astype(F32)
        neg = jnp.float32(-jnp.inf)
        logits = jnp.where(lane < N_EXPERTS, logits, neg)
        m1 = jnp.max(logits, axis=1, keepdims=True)
        i1 = jnp.min(jnp.where(logits == m1, lane, float(LANES)), axis=1, keepdims=True)
        rest = jnp.where(lane == i1, neg, logits)
        m2 = jnp.max(rest, axis=1, keepdims=True)
        i2 = jnp.min(jnp.where(rest == m2, lane, float(LANES)), axis=1, keepdims=True)
        e2 = jnp.exp(m2 - m1)
        w1 = 1.0 / (1.0 + e2)
        comb_ref[rows, :] = (jnp.where(lane == ROUTE_E1, i1, 0.0) + jnp.where(lane == ROUTE_E2, i2, 0.0)
                             + jnp.where(lane == ROUTE_W1, w1, 0.0)
                             + jnp.where(lane == ROUTE_W2, e2 * w1, 0.0))


def _outproj(x, o_f, o_b, p, mla, mods, row_fn, gg, wo, ln2, router, tm):
    t, d = x.shape
    with_router = router is not None
    full = lambda a: pl.BlockSpec(a.shape, lambda i: (0,) * a.ndim)
    in_specs = [
        pl.BlockSpec((tm, d), lambda i: (i, 0)),
        pl.BlockSpec((tm, GLA_V_W), lambda i: (i, 0)),
        pl.BlockSpec((tm, GLA_V_W), lambda i: (i, 0)),
        pl.BlockSpec((tm, GLA_V_W), lambda i: (i, P_R // GLA_V_W)),
        pl.BlockSpec((tm, MLA_V_W), lambda i: (i, 0)),
        pl.BlockSpec((1, 6, d), lambda i: (row_fn(i), 0, 0)),
        full(gg), full(wo), full(ln2),
    ]
    args = [x, o_f, o_b, p, mla, mods, gg, wo, ln2]
    h2_shape = jax.ShapeDtypeStruct((t, d // 2), F32) if with_router else jax.ShapeDtypeStruct((t, d), BF16)
    out_specs = [pl.BlockSpec((tm, d), lambda i: (i, 0)),
                 pl.BlockSpec((tm, h2_shape.shape[1]), lambda i: (i, 0))]
    out_shape = [jax.ShapeDtypeStruct((t, d), F32), h2_shape]
    if with_router:
        in_specs += [full(router[0]), full(router[1])]
        args += list(router)
        out_specs.append(pl.BlockSpec((tm, LANES), lambda i: (i, 0)))
        out_shape.append(jax.ShapeDtypeStruct((t, LANES), F32))
    return pl.pallas_call(
        functools.partial(_outproj_kernel, with_router=with_router),
        grid=(t // tm,),
        in_specs=in_specs,
        out_specs=out_specs,
        out_shape=out_shape,
        compiler_params=_cparams(("arbitrary",)),
        name="outproj",
    )(*args)


def _ffn_kernel(*refs, with_comb, final_norm):
    h_ref, x1_ref, mod_ref = refs[:3]
    k = 3
    comb_ref = fin_ref = None
    if with_comb:
        comb_ref = refs[k]
        k += 1
    wg_ref, wu_ref, wd_ref = refs[k:k + 3]
    k += 3
    if final_norm:
        fin_ref = refs[k]
        k += 1
    o_ref, acc = refs[k:]
    e = pl.program_id(1)
    f = pl.program_id(2)

    @pl.when((e == 0) & (f == 0))
    def _():
        acc[...] = jnp.zeros(acc.shape, F32)

    h = h_ref[...]
    a = _dot(h, wg_ref[0])
    u = _dot(h, wu_ref[0])
    act = _silu(a) * u
    if with_comb:
        comb = comb_ref[...]
        lane = lax.broadcasted_iota(jnp.int32, comb.shape, 1)
        act = act * jnp.sum(jnp.where(lane == e, comb, 0.0), axis=1, keepdims=True)
    acc[...] += _dot(act.astype(BF16), wd_ref[0])

    @pl.when((e == pl.num_programs(1) - 1) & (f == pl.num_programs(2) - 1))
    def _():
        x2 = x1_ref[...] + mod_ref[0][5:6] * acc[...]
        if final_norm:
            x2 = _rms(x2, fin_ref[...])
        o_ref[...] = x2


def _ffn(h2, x1, mods, row_fn, comb, wg, wu, wd, fin_g, tm, tf):
    t, d = x1.shape
    n_e, _, ff = wg.shape
    with_comb = comb is not None
    final_norm = fin_g is not None
    in_specs = [
        pl.BlockSpec((tm, d), lambda i, e, f: (i, 0)),
        pl.BlockSpec((tm, d), lambda i, e, f: (i, 0)),
        pl.BlockSpec((1, 6, d), lambda i, e, f: (row_fn(i), 0, 0)),
    ]
    args = [h2, x1, mods]
    if with_comb:
        in_specs.append(pl.BlockSpec((tm, LANES), lambda i, e, f: (i, 0)))
        args.append(comb)
    in_specs += [
        pl.BlockSpec((1, d, tf), lambda i, e, f: (e, 0, f)),
        pl.BlockSpec((1, d, tf), lambda i, e, f: (e, 0, f)),
        pl.BlockSpec((1, tf, d), lambda i, e, f: (e, f, 0)),
    ]
    args += [wg, wu, wd]
    if final_norm:
        in_specs.append(pl.BlockSpec((1, d), lambda i, e, f: (0, 0)))
        args.append(fin_g)
    return pl.pallas_call(
        functools.partial(_ffn_kernel, with_comb=with_comb, final_norm=final_norm),
        grid=(t // tm, n_e, ff // tf),
        in_specs=in_specs,
        out_specs=pl.BlockSpec((tm, d), lambda i, e, f: (i, 0)),
        out_shape=jax.ShapeDtypeStruct((t, d), F32),
        scratch_shapes=[pltpu.VMEM((tm, d), F32)],
        compiler_params=_cparams(("arbitrary", "arbitrary", "arbitrary")),
        name="ffn",
    )(*args)


def _sc_row_gather(table, idx):
    _, w = table.shape
    b = idx.shape[0]
    n_workers = SC_CORES * SC_SUBCORES
    assert b % (n_workers * SC_GATHER_ROWS) == 0, (b, n_workers, SC_GATHER_ROWS)
    b_per_w = b // n_workers
    n_chunks = b_per_w // SC_GATHER_ROWS
    mesh = plsc.VectorSubcoreMesh(core_axis_name="c", subcore_axis_name="s",
                                  num_cores=SC_CORES, num_subcores=SC_SUBCORES)

    def body(table_hbm, idx_hbm, out_hbm, idx_a, idx_b, rows_a, rows_b, sem_a, sem_b):
        wid = lax.axis_index("s") * SC_CORES + lax.axis_index("c")
        base = wid * b_per_w
        idx_bufs, row_bufs, sems = (idx_a, idx_b), (rows_a, rows_b), (sem_a, sem_b)

        def start(ci):
            slot = ci % 2
            pltpu.sync_copy(idx_hbm.at[pl.ds(base + ci * SC_GATHER_ROWS, SC_GATHER_ROWS)], idx_bufs[slot])
            return pltpu.async_copy(table_hbm.at[idx_bufs[slot]], row_bufs[slot], sems[slot])

        pending = start(0)
        for ci in range(n_chunks):
            following = start(ci + 1) if ci + 1 < n_chunks else None
            pending.wait()
            pltpu.sync_copy(row_bufs[ci % 2], out_hbm.at[pl.ds(base + ci * SC_GATHER_ROWS, SC_GATHER_ROWS)])
            pending = following

    return pl.kernel(
        body,
        out_type=jax.ShapeDtypeStruct((b, w), F32),
        mesh=mesh,
        scratch_types=[pltpu.VMEM((SC_GATHER_ROWS,), jnp.int32)] * 2
        + [pltpu.VMEM((SC_GATHER_ROWS, w), F32)] * 2 + [pltpu.SemaphoreType.DMA] * 2,
        name="sc_row_gather",
    )(table, idx)


def _sc_row_scatter2(table, pos, n_out):
    t, w = table.shape
    n_workers = SC_CORES * SC_SUBCORES
    assert t % (n_workers * SC_GATHER_ROWS) == 0, (t, n_workers, SC_GATHER_ROWS)
    t_per_w = t // n_workers
    n_chunks = t_per_w // SC_GATHER_ROWS
    mesh = plsc.VectorSubcoreMesh(core_axis_name="c", subcore_axis_name="s",
                                  num_cores=SC_CORES, num_subcores=SC_SUBCORES)

    def body(table_hbm, pos_hbm, out_hbm, i0a, i1a, i0b, i1b, rows_a, rows_b, sem_a, sem_b):
        wid = lax.axis_index("s") * SC_CORES + lax.axis_index("c")
        base = wid * t_per_w
        idx0, idx1, row_bufs, sems = (i0a, i0b), (i1a, i1b), (rows_a, rows_b), (sem_a, sem_b)

        def drain(pair):
            if pair is not None:
                pair[0].wait()
                pair[1].wait()

        pending = [None, None]
        for ci in range(n_chunks):
            slot = ci % 2
            drain(pending[slot])
            off = base + ci * SC_GATHER_ROWS
            pltpu.sync_copy(pos_hbm.at[pl.ds(off, SC_GATHER_ROWS)], idx0[slot])
            pltpu.sync_copy(pos_hbm.at[pl.ds(t + off, SC_GATHER_ROWS)], idx1[slot])
            pltpu.sync_copy(table_hbm.at[pl.ds(off, SC_GATHER_ROWS)], row_bufs[slot])
            pending[slot] = (pltpu.async_copy(row_bufs[slot], out_hbm.at[idx0[slot]], sems[slot]),
                             pltpu.async_copy(row_bufs[slot], out_hbm.at[idx1[slot]], sems[slot]))
        drain(pending[0])
        drain(pending[1])

    return pl.kernel(
        body,
        out_type=jax.ShapeDtypeStruct((n_out, w), F32),
        mesh=mesh,
        scratch_types=[pltpu.VMEM((SC_GATHER_ROWS,), jnp.int32)] * 4
        + [pltpu.VMEM((SC_GATHER_ROWS, w), F32)] * 2 + [pltpu.SemaphoreType.DMA] * 2,
        name="sc_row_scatter",
    )(table, pos)


def _moe_plan_kernel(route_ref, tri_ref, utri_ref, pos_ref, cnt_ref, run_scr, off_scr):
    phase = pl.program_id(0)
    blk = pl.program_id(1)
    route = route_ref[...]
    lane = lax.broadcasted_iota(jnp.int32, route.shape, 1).astype(F32)
    oh1 = jnp.where(lane == route[:, ROUTE_E1:ROUTE_E1 + 1], 1.0, 0.0)
    oh2 = jnp.where(lane == route[:, ROUTE_E2:ROUTE_E2 + 1], 1.0, 0.0)
    oh = oh1 + oh2

    @pl.when(blk == 0)
    def _():
        run_scr[...] = jnp.zeros(run_scr.shape, F32)

    @pl.when(phase == 0)
    def _():
        run_scr[...] += jnp.sum(oh, axis=0, keepdims=True)

        @pl.when(blk == pl.num_programs(1) - 1)
        def _():
            counts = run_scr[...]
            cnt_ref[...] = counts
            tiles_per = jnp.floor((counts + (MOE_TILE - 1.0)) * (1.0 / MOE_TILE))
            tile_end = _dot(jnp.broadcast_to(tiles_per, (8, LANES)).astype(BF16), utri_ref[...])[0:1]
            off_scr[...] = (tile_end - tiles_per) * float(MOE_TILE)

    @pl.when(phase == 1)
    def _():
        incl = _dot(tri_ref[...], oh.astype(BF16))
        before = incl - oh + run_scr[...] + off_scr[...]
        p1 = jnp.sum(before * oh1, axis=1, keepdims=True)
        p2 = jnp.sum(before * oh2, axis=1, keepdims=True)
        pos = jnp.where(lane == 0.0, p1, 0.0) + jnp.where(lane == 1.0, p2, 0.0)
        pos_ref[...] = pos.astype(jnp.int32)
        run_scr[...] += incl[incl.shape[0] - 1:, :]


def _moe_plan(route, n_tiles, tm):
    t = route.shape[0]
    tri = jnp.asarray(np.tril(np.ones((tm, tm), np.float32)), dtype=BF16)
    utri = jnp.asarray(np.triu(np.ones((LANES, LANES), np.float32)), dtype=BF16)
    pos, counts = pl.pallas_call(
        _moe_plan_kernel,
        grid=(2, t // tm),
        in_specs=[
            pl.BlockSpec((tm, LANES), lambda p, i: (i, 0)),
            pl.BlockSpec((tm, tm), lambda p, i: (0, 0)),
            pl.BlockSpec((LANES, LANES), lambda p, i: (0, 0)),
        ],
        out_specs=[
            pl.BlockSpec((tm, LANES), lambda p, i: (i * p, 0)),
            pl.BlockSpec((1, LANES), lambda p, i: (0, 0)),
        ],
        out_shape=[jax.ShapeDtypeStruct((t, LANES), jnp.int32),
                   jax.ShapeDtypeStruct((1, LANES), F32)],
        scratch_shapes=[pltpu.VMEM((1, LANES), F32), pltpu.VMEM((1, LANES), F32)],
        compiler_params=_cparams(("arbitrary", "arbitrary")),
        name="moe_plan",
    )(route, tri, utri)
    counts = counts[0, :N_EXPERTS].astype(jnp.int32)
    tile_end = jnp.cumsum((counts + MOE_TILE - 1) // MOE_TILE)
    n_used = tile_end[-1]
    tile_ids = jnp.minimum(jnp.arange(n_tiles, dtype=jnp.int32), n_used - 1)
    tile_expert = jnp.sum((tile_end[None, :] <= tile_ids[:, None]).astype(jnp.int32), axis=1)
    return pos[:, :2].T, tile_expert, n_used.reshape(1)


def _moe_ffn_kernel(te_ref, nused_ref, xs_ref, wg_ref, wu_ref, wd_ref, o_ref, acc, h_scr, *, splits):
    i = pl.program_id(0)
    f = pl.program_id(1)
    last_f = pl.num_programs(1) - 1
    used = i < nused_ref[0]

    @pl.when(used)
    def _():
        @pl.when(f == 0)
        def _():
            acc[...] = jnp.zeros(acc.shape, F32)
            h_scr[...] = _unpack_bf16_pairs(xs_ref[...]).astype(BF16)

        h = h_scr[...]
        for lo, hi in splits:
            a = _dot(h, wg_ref[0, :, lo:hi].astype(BF16))
            u = _dot(h, wu_ref[0, :, lo:hi].astype(BF16))
            acc[...] += _dot((_silu(a) * u).astype(BF16), wd_ref[0, lo:hi, :].astype(BF16))

        @pl.when(f == last_f)
        def _():
            o_ref[...] = _pack_bf16_pairs(acc[...])

    @pl.when(jnp.logical_not(used) & (f == last_f))
    def _():
        o_ref[...] = jnp.zeros(o_ref.shape, F32)


def _moe_ffn(xs, tile_expert, n_used, wg, wu, wd):
    rows, half = xs.shape
    d = 2 * half
    ff = wg.shape[2]
    if ff % MOE_FF_TILE == 0:
        tf, splits = MOE_FF_TILE, MOE_FF_SPLITS
    else:
        tf, splits = ff, ((0, ff),)
    n_tiles = rows // MOE_TILE
    grid_spec = pltpu.PrefetchScalarGridSpec(
        num_scalar_prefetch=2,
        grid=(n_tiles, ff // tf),
        in_specs=[
            pl.BlockSpec((MOE_TILE, half), lambda i, f, te, nu: (i, 0)),
            pl.BlockSpec((1, d, tf), lambda i, f, te, nu: (te[i], 0, f)),
            pl.BlockSpec((1, d, tf), lambda i, f, te, nu: (te[i], 0, f)),
            pl.BlockSpec((1, tf, d), lambda i, f, te, nu: (te[i], f, 0)),
        ],
        out_specs=pl.BlockSpec((MOE_TILE, half), lambda i, f, te, nu: (i, 0)),
        scratch_shapes=[pltpu.VMEM((MOE_TILE, d), F32), pltpu.VMEM((MOE_TILE, d), BF16)],
    )
    return pl.pallas_call(
        functools.partial(_moe_ffn_kernel, splits=splits),
        grid_spec=grid_spec,
        out_shape=jax.ShapeDtypeStruct((rows, half), F32),
        compiler_params=_cparams(("arbitrary", "arbitrary")),
        name="moe_ffn",
    )(tile_expert, n_used, xs, wg, wu, wd)


def _combine_kernel(*refs, final_norm):
    x1_ref, y0_ref, y1_ref, route_ref, mod_ref = refs[:5]
    fin_ref = refs[5] if final_norm else None
    o_ref = refs[-1]
    route = route_ref[...]
    w1 = route[:, ROUTE_W1:ROUTE_W1 + 1]
    w2 = route[:, ROUTE_W2:ROUTE_W2 + 1]
    y = w1 * _unpack_bf16_pairs(y0_ref[...]) + w2 * _unpack_bf16_pairs(y1_ref[...])
    x2 = x1_ref[...] + mod_ref[0][5:6] * y
    if final_norm:
        x2 = _rms(x2, fin_ref[...])
    o_ref[...] = x2


def _combine(x1, yg, route, mods, row_fn, fin_g, tm):
    t, d = x1.shape
    nb = t // tm
    final_norm = fin_g is not None
    in_specs = [
        pl.BlockSpec((tm, d), lambda i: (i, 0)),
        pl.BlockSpec((tm, d // 2), lambda i: (i, 0)),
        pl.BlockSpec((tm, d // 2), lambda i: (i + nb, 0)),
        pl.BlockSpec((tm, LANES), lambda i: (i, 0)),
        pl.BlockSpec((1, 6, d), lambda i: (row_fn(i), 0, 0)),
    ]
    args = [x1, yg, yg, route, mods]
    if final_norm:
        in_specs.append(pl.BlockSpec((1, d), lambda i: (0, 0)))
        args.append(fin_g)
    return pl.pallas_call(
        functools.partial(_combine_kernel, final_norm=final_norm),
        grid=(nb,),
        in_specs=in_specs,
        out_specs=pl.BlockSpec((tm, d), lambda i: (i, 0)),
        out_shape=jax.ShapeDtypeStruct((t, d), F32),
        compiler_params=_cparams(("arbitrary",)),
        name="moe_combine",
    )(*args)


def _moe(h2, x1, route, mods, row_fn, wg, wu, wd, fin_g, tm):
    t = h2.shape[0]
    n_tiles = -(-2 * t // MOE_TILE) + N_EXPERTS
    pos, tile_expert, n_used = _moe_plan(route, n_tiles, _pick_tile(t, ATTN_TILE))
    pos = pos.reshape(-1)
    xs = _sc_row_scatter2(h2, pos, n_tiles * MOE_TILE)
    ys = _moe_ffn(xs, tile_expert, n_used, wg, wu, wd)
    yg = _sc_row_gather(ys, pos)
    return _combine(x1, yg, route, mods, row_fn, fin_g, tm)


def _rope_partner():
    j = np.arange(MLA_ROPE)
    return np.where((j % 32) < 16, j + 16, j - 16)


def _prep_in_weight(w):
    d = w.shape[0]
    widths = (GLA_QK_W, GLA_QK_W, GLA_V_W, GLA_GATE_RANK, GLA_GATE_RANK, GLA_V_W,
              MLA_Q_RANK, MLA_KV_RANK, MLA_ROPE)
    offs = np.concatenate([[0], np.cumsum(widths)])
    part = lambda i: w[:, offs[i]:offs[i + 1]]
    assert w.shape[1] == offs[-1] and P_MISC + MISC_GB + GLA_GATE_RANK <= P_WIDTH
    cols = [part(0), part(1), part(2), part(5), part(6), part(7), part(8), part(3), part(4),
            jnp.zeros((d, P_WIDTH - int(offs[-1])), w.dtype)]
    return jnp.concatenate(cols, axis=1).astype(BF16)


def _prep_gate_weight(w_g2, b_g2):
    ws = []
    for z, off in ((0, MISC_GF), (1, MISC_GB)):
        ws.append(jnp.zeros((LANES, GLA_QK_W), F32).at[off:off + GLA_GATE_RANK].set(w_g2[z]))
    return jnp.stack(ws).astype(BF16), b_g2.reshape(2, 1, GLA_QK_W)


def _prep_mla_weights(w_uq, w_ukv):
    partner = _rope_partner()
    wq = w_uq.reshape(MLA_Q_RANK, MLA_HEADS, MLA_QK)
    wqn = wq[:, :, :MLA_NOPE].reshape(MLA_Q_RANK, MLA_HEADS * MLA_NOPE)
    rope = wq[:, :, MLA_NOPE:]
    pad = jnp.zeros((MLA_Q_RANK, MLA_HEADS, LANES - MLA_ROPE), w_uq.dtype)
    wqr = jnp.concatenate([rope, pad], axis=2).reshape(MLA_Q_RANK, MLA_HEADS * LANES)
    wqs = jnp.concatenate([rope[:, :, partner], pad], axis=2).reshape(MLA_Q_RANK, MLA_HEADS * LANES)
    wkv = w_ukv.reshape(MLA_KV_RANK, MLA_HEADS, MLA_NOPE + MLA_V)
    wknt = wkv[:, :, :MLA_NOPE].reshape(MLA_KV_RANK, MLA_HEADS * MLA_NOPE).T
    wv = wkv[:, :, MLA_NOPE:].reshape(MLA_KV_RANK, MLA_HEADS * MLA_V)
    perm = np.zeros((LANES, LANES), np.float32)
    perm[partner, np.arange(MLA_ROPE)] = 1.0
    eye = np.eye(MLA_ROPE, LANES, dtype=np.float32)
    return (wqn.astype(BF16), wqr.astype(BF16), wqs.astype(BF16), wknt.astype(BF16),
            wv.astype(BF16), jnp.asarray(perm, BF16), jnp.asarray(eye, BF16))


def _rope_tables(n_tok):
    rows = n_tok // GRID_W
    row = np.repeat(np.arange(rows, dtype=np.float32), GRID_W)
    col = np.tile(np.arange(GRID_W, dtype=np.float32), rows)
    nfreq = MLA_ROPE // 4
    inv = np.float32(ROPE_BASE) ** (-np.arange(nfreq, dtype=np.float32) / np.float32(nfreq))
    ar = (row[:, None] * inv).astype(np.float32)
    ac = (col[:, None] * inv).astype(np.float32)
    zero = np.zeros((n_tok, LANES - MLA_ROPE), np.float32)
    cos = np.concatenate([np.cos(ar), np.cos(ar), np.cos(ac), np.cos(ac), zero], axis=1)
    sin = np.concatenate([-np.sin(ar), np.sin(ar), -np.sin(ac), np.sin(ac), zero], axis=1)
    return jnp.asarray(cos, F32), jnp.asarray(sin, F32)


def _identity_tables(n_tok):
    cos = jnp.concatenate([jnp.ones((n_tok, MLA_ROPE), F32),
                           jnp.zeros((n_tok, LANES - MLA_ROPE), F32)], axis=1)
    return cos, jnp.zeros((n_tok, LANES), F32)


def _pick_tile(n, pref):
    t = min(n, pref)
    while n % t:
        t //= 2
    return t


def _pick_ff_tile(ff):
    best = LANES
    for m in range(1, ff // LANES + 1):
        if ff % (m * LANES) == 0 and m * LANES <= FFN_MAX_FF_TILE:
            best = m * LANES
    return best


@jax.jit
def _forward(x, c, ctx, c_ctx, w_mod, b_mod, ln1_g, ln2_g, w_in, w_gla_g2, b_gla_g2, gla_norm_g,
             mla_q_norm_g, w_uq, mla_kv_norm_g, w_ukv, w_out, ffn_w_gate, ffn_w_up, ffn_w_down,
             router_w, exp_w_gate, exp_w_up, exp_w_down, final_norm_g):
    batch, seq, d = x.shape
    n_ctx = ctx.shape[1]
    depth = w_mod.shape[0]
    assert d == D_MODEL and batch < 8, (d, batch)

    cvec = jnp.zeros((8, d), F32).at[:batch].set(c).at[batch].set(c_ctx)
    mods_all = _modulation(cvec, w_mod, b_mod).reshape(depth, 8, 6, d)

    xl = x.reshape(batch * seq, d)
    xc = ctx.reshape(batch * n_ctx, d)

    tm_l = _pick_tile(seq, TOKEN_TILE)
    tm_c = _pick_tile(n_ctx, CTX_TILE)
    tk_l = _pick_tile(seq, ATTN_TILE)
    cb_l = _pick_tile(seq, GLA_BLOCK)
    cb_c = _pick_tile(n_ctx, GLA_BLOCK)
    row_l = lambda tm: (lambda i: i // (seq // tm))
    row_c = lambda i: batch

    rope_l = _rope_tables(seq)
    rope_c = _identity_tables(tm_c)
    zero_state = jnp.zeros((2, batch, GLA_QK_W, GLA_DV), F32)

    for i in range(depth):
        need_ctx = i < depth - 1
        last = i == depth - 1
        mods = mods_all[i]
        ln1 = ln1_g[i].reshape(1, d)
        ln2 = ln2_g[i].reshape(1, d)
        w_in_r = _prep_in_weight(w_in[i])
        gates = _prep_gate_weight(w_gla_g2[i], b_gla_g2[i])
        mla_w = _prep_mla_weights(w_uq[i], w_ukv[i])
        qg = mla_q_norm_g[i].reshape(1, MLA_Q_RANK)
        kvg = mla_kv_norm_g[i].reshape(1, MLA_KV_RANK)
        gg = gla_norm_g[i].reshape(1, GLA_DV)
        wo = w_out[i].astype(BF16)

        p_l, gate_l = _inproj(xl, mods, row_l(tk_l), ln1, w_in_r, *gates, tk_l, cb_l)
        p_c, gate_c = _inproj(xc, mods, row_c, ln1, w_in_r, *gates, tm_c, cb_c)

        oc_f, oc_b, s_ctx = _gla(p_c, gate_c, *gates, zero_state, batch=batch, cb=cb_c)
        ol_f, ol_b, _ = _gla(p_l, gate_l, *gates, s_ctx, batch=batch, cb=cb_l)

        q_l, kt_l, v_l = _mlaprep(p_l, *rope_l, qg, kvg, mla_w, batch=batch, tm=tk_l)
        q_c, kt_c, v_c = _mlaprep(p_c, *rope_c, qg, kvg, mla_w, batch=batch, tm=tm_c)
        m_l = _attention_pipelined(q_l, kt_l, v_l, kt_c, v_c, tq=tk_l, n_sub=1)
        m_l = m_l.reshape(batch * seq, MLA_V_W)

        if i % 2 == 0:
            j = i // 2
            router = None
            wg = ffn_w_gate[j].astype(BF16)
            wu = ffn_w_up[j].astype(BF16)
            wd = ffn_w_down[j].astype(BF16)
        else:
            j = i // 2
            rw = jnp.zeros((d, LANES), F32).at[:, :N_EXPERTS].set(router_w[j])
            rw_hi = rw.astype(BF16)
            router = (rw_hi, (rw - rw_hi.astype(F32)).astype(BF16))
            wg, wu, wd = exp_w_gate[j], exp_w_up[j], exp_w_down[j]
        fin = final_norm_g.reshape(1, d) if last else None

        if router is None:
            xl = _outproj_ffn(xl, ol_f, ol_b, p_l, m_l, mods, row_l(tm_l), gg, wo, ln2,
                              wg, wu, wd, fin, tm_l)
        else:
            outs = _outproj(xl, ol_f, ol_b, p_l, m_l, mods, row_l(tm_l), gg, wo, ln2, router, tm_l)
            xl = _moe(outs[1], outs[0], outs[2], mods, row_l(tk_l), wg, wu, wd, fin, tk_l)

        if need_ctx:
            m_c = _attention(q_c, [(kt_c, v_c)], tq=tm_c, n_sub=1).reshape(batch * n_ctx, MLA_V_W)
            if router is None:
                xc = _outproj_ffn(xc, oc_f, oc_b, p_c, m_c, mods, row_c, gg, wo, ln2,
                                  wg, wu, wd, None, _pick_tile(batch * n_ctx, TOKEN_TILE))
            else:
                outs_c = _outproj(xc, oc_f, oc_b, p_c, m_c, mods, row_c, gg, wo, ln2, router, tm_c)
                tm_fc = _pick_tile(batch * n_ctx, TOKEN_TILE)
                r_c = outs_c[2]
                lane = jnp.arange(LANES, dtype=F32)[None, :]
                comb_c = (jnp.where(lane == r_c[:, ROUTE_E1:ROUTE_E1 + 1], r_c[:, ROUTE_W1:ROUTE_W1 + 1], 0.0)
                          + jnp.where(lane == r_c[:, ROUTE_E2:ROUTE_E2 + 1], r_c[:, ROUTE_W2:ROUTE_W2 + 1], 0.0))
                bits = lax.bitcast_convert_type(outs_c[1], jnp.uint32)
                h2_c = jnp.concatenate([lax.bitcast_convert_type(bits << 16, F32),
                                        lax.bitcast_convert_type(bits & jnp.uint32(0xFFFF0000), F32)], axis=1)
                xc = _ffn(h2_c.astype(BF16), outs_c[0], mods, row_c, comb_c,
                          wg.astype(BF16), wu.astype(BF16), wd.astype(BF16), None, tm_fc,
                          _pick_ff_tile(wg.shape[2]))

    return xl.reshape(batch, seq, d)


def kernel(x, c, ctx, c_ctx, w_mod, b_mod, ln1_g, ln2_g, w_in, w_gla_g2, b_gla_g2, gla_norm_g,
           mla_q_norm_g, w_uq, mla_kv_norm_g, w_ukv, w_out, ffn_w_gate, ffn_w_up, ffn_w_down,
           router_w, exp_w_gate, exp_w_up, exp_w_down, final_norm_g):
    return _forward(x, c, ctx, c_ctx, w_mod, b_mod, ln1_g, ln2_g, w_in, w_gla_g2, b_gla_g2,
                    gla_norm_g, mla_q_norm_g, w_uq, mla_kv_norm_g, w_ukv, w_out, ffn_w_gate,
                    ffn_w_up, ffn_w_down, router_w, exp_w_gate, exp_w_up, exp_w_down, final_norm_g)
```

```python
import functools

import numpy as np
import jax
import jax.numpy as jnp
from jax import lax
from jax.experimental import pallas as pl
from jax.experimental.pallas import tpu as pltpu
from jax.experimental.pallas import tpu_sc as plsc

F32 = jnp.float32
BF16 = jnp.bfloat16

D_MODEL = 1024
EPS = 1e-6
GRID_W = 64

GLA_HEADS = 4
GLA_DK = 64
GLA_DV = 128
GLA_GATE_RANK = 16
GLA_GATE_NORM = 16.0
GLA_CHUNK = 64
GLA_QK_W = GLA_HEADS * GLA_DK
GLA_V_W = GLA_HEADS * GLA_DV
GLA_EXP_CLAMP = 80.0
GLA_PRE_SAFE = 29.0

MLA_HEADS = 4
MLA_NOPE = 128
MLA_ROPE = 64
MLA_V = 128
MLA_QK = MLA_NOPE + MLA_ROPE
MLA_Q_RANK = 256
MLA_KV_RANK = 128
MLA_SCALE = MLA_QK ** -0.5
MLA_Q_SCALE = MLA_SCALE * 1.4426950408889634
MLA_V_W = MLA_HEADS * MLA_V
MLA_V_EXT = 2 * MLA_V
ROPE_BASE = 10000.0

N_EXPERTS = 8
LANES = 128
ROUTE_E1, ROUTE_E2, ROUTE_W1, ROUTE_W2 = 0, 1, 2, 3

SC_CORES = 2
SC_SUBCORES = 16
SC_GATHER_ROWS = 64
MOE_TILE = 1024
MOE_FF_TILE = 512
MOE_FF_SPLITS = ((0, 256), (256, 512))

P_Q, P_K, P_V, P_R, P_CQ, P_CKV, P_MISC = 0, 256, 512, 1024, 1536, 1792, 1920
P_WIDTH = 2048
MISC_KR, MISC_GF, MISC_GB = 0, 64, 80

VMEM_LIMIT = 56 * 1024 * 1024

TOKEN_TILE = 512
CTX_TILE = 256
ATTN_TILE = 1024
GLA_BLOCK = 256
DENSE_FF_GROUP = 1024
FFN_MAX_FF_TILE = 1408


def _cparams(sem):
    return pltpu.CompilerParams(dimension_semantics=sem, vmem_limit_bytes=VMEM_LIMIT)


def _rms(x, g):
    return x * lax.rsqrt(jnp.mean(x * x, axis=-1, keepdims=True) + EPS) * g


def _silu(x):
    return x / (1.0 + jnp.exp(-x))


def _dot(a, b):
    return jnp.dot(a, b, preferred_element_type=F32)


def _dot_nt(a, b):
    return lax.dot_general(a, b, (((1,), (1,)), ((), ())), preferred_element_type=F32)


def _dot_tn(a, b):
    return lax.dot_general(a, b, (((0,), (0,)), ((), ())), preferred_element_type=F32)


def _pack_bf16_pairs(x):
    w = x.shape[1] // 2
    words = pltpu.pack_elementwise([x[:, :w], x[:, w:]], packed_dtype=BF16)
    return lax.bitcast_convert_type(words, F32)


def _unpack_bf16_pairs(p):
    words = lax.bitcast_convert_type(p, jnp.int32)
    lo = pltpu.unpack_elementwise(words, index=0, packed_dtype=BF16, unpacked_dtype=F32)
    hi = pltpu.unpack_elementwise(words, index=1, packed_dtype=BF16, unpacked_dtype=F32)
    return jnp.concatenate([lo, hi], axis=1)


def _mod_kernel(c_ref, w_ref, b_ref, o_ref):
    s = _silu(c_ref[...]).astype(BF16)
    o_ref[0] = _dot(s, w_ref[0].astype(BF16)) + b_ref[0]


def _modulation(cvec, w_mod, b_mod):
    depth, d, n = w_mod.shape
    tn = 1536
    return pl.pallas_call(
        _mod_kernel,
        grid=(depth, n // tn),
        in_specs=[
            pl.BlockSpec((8, d), lambda l, j: (0, 0)),
            pl.BlockSpec((1, d, tn), lambda l, j: (l, 0, j)),
            pl.BlockSpec((1, 1, tn), lambda l, j: (l, 0, j)),
        ],
        out_specs=pl.BlockSpec((1, 8, tn), lambda l, j: (l, 0, j)),
        out_shape=jax.ShapeDtypeStruct((depth, 8, n), F32),
        compiler_params=_cparams(("arbitrary", "arbitrary")),
        name="modulation",
    )(cvec, w_mod, b_mod.reshape(depth, 1, n))


def _inproj_kernel(x_ref, mod_ref, g_ref, w_ref, wg_ref, bg_ref, o_ref, gate_ref, *, gate_block):
    m = mod_ref[0]
    tm = x_ref.shape[0]
    n_groups = 2 if tm % (2 * gate_block) == 0 else 1
    rows_per_group = tm // n_groups
    sub = lax.broadcasted_iota(jnp.int32, (8, LANES), 0)
    for g in range(n_groups):
        rows = slice(g * rows_per_group, (g + 1) * rows_per_group)
        h = _rms(x_ref[rows, :], g_ref[...]) * (1.0 + m[1:2]) + m[0:1]
        p = _dot(h.astype(BF16), w_ref[...]).astype(BF16)
        o_ref[rows, :] = p
        misc = p[:, P_MISC:P_MISC + LANES]
        neg = [-(_dot(misc, wg_ref[z]) + bg_ref[z]) for z in range(2)]
        for j in range(rows_per_group // gate_block):
            blk_rows = slice(j * gate_block, (j + 1) * gate_block)
            worst = [jnp.max(jnp.max(n[blk_rows], axis=0, keepdims=True), axis=1, keepdims=True)
                     for n in neg]
            gate_ref[g * (rows_per_group // gate_block) + j] = jnp.where(
                sub == 0, worst[0], jnp.where(sub == 1, worst[1], 0.0))


def _inproj(x, mods, row_fn, ln_g, w, wg, bg, tm, gate_block):
    t, d = x.shape
    assert tm % gate_block == 0
    full = lambda a: pl.BlockSpec(a.shape, lambda i: (0,) * a.ndim)
    return pl.pallas_call(
        functools.partial(_inproj_kernel, gate_block=gate_block),
        grid=(t // tm,),
        in_specs=[
            pl.BlockSpec((tm, d), lambda i: (i, 0)),
            pl.BlockSpec((1, 6, d), lambda i: (row_fn(i), 0, 0)),
            pl.BlockSpec((1, d), lambda i: (0, 0)),
            pl.BlockSpec((d, P_WIDTH), lambda i: (0, 0)),
            full(wg), full(bg),
        ],
        out_specs=[pl.BlockSpec((tm, P_WIDTH), lambda i: (i, 0)),
                   pl.BlockSpec((tm // gate_block, 8, LANES), lambda i: (i, 0, 0))],
        out_shape=[jax.ShapeDtypeStruct((t, P_WIDTH), BF16),
                   jax.ShapeDtypeStruct((t // gate_block, 8, LANES), F32)],
        compiler_params=_cparams(("arbitrary",)),
        name="inproj",
    )(x, mods, ln_g, w, wg, bg)


def _gla_direction(q_ref, k_ref, v_ref, pre, tri, s_scr, o_ref, *, reverse, n_chunks, exact):
    c_len = GLA_CHUNK
    g = (jnp.minimum(pre, 0.0) - jnp.log(1.0 + jnp.exp(-jnp.abs(pre)))) * (1.0 / GLA_GATE_NORM)
    g_hi = g.astype(BF16)
    g_lo = (g - g_hi.astype(F32)).astype(BF16)
    cum = _dot(tri, g_hi) + _dot(tri, g_lo)
    tot_rows = jnp.concatenate(
        [cum[c * c_len:c * c_len + 1] if reverse else cum[(c + 1) * c_len - 1:(c + 1) * c_len]
         for c in range(n_chunks)] + [jnp.zeros((8 - n_chunks, GLA_QK_W), F32)], axis=0)
    t_hi = tot_rows.astype(BF16)
    t_lo = (tot_rows - t_hi.astype(F32)).astype(BF16)
    eye = (lax.broadcasted_iota(jnp.int32, (GLA_QK_W, GLA_QK_W), 0)
           == lax.broadcasted_iota(jnp.int32, (GLA_QK_W, GLA_QK_W), 1))
    eye = jnp.where(eye, 1.0, 0.0).astype(BF16)
    tot_cols = _dot_nt(eye, t_hi) + _dot_nt(eye, t_lo)

    lane = lax.broadcasted_iota(jnp.int32, (c_len, GLA_QK_W), 1)
    head_masks = [(lane >= h * GLA_DK) & (lane < (h + 1) * GLA_DK) for h in range(GLA_HEADS)]
    row = lax.broadcasted_iota(jnp.int32, (GLA_HEADS * c_len, c_len), 0) % c_len
    col = lax.broadcasted_iota(jnp.int32, (GLA_HEADS * c_len, c_len), 1)
    pair_mask = (col >= row) if reverse else (col <= row)

    def stack_heads(a):
        return jnp.concatenate([jnp.where(mk, a, 0.0) for mk in head_masks], axis=0).astype(BF16)

    def exact_levels():
        n = n_chunks * c_len
        r = lax.broadcasted_iota(jnp.int32, (n, n), 0)
        u = lax.broadcasted_iota(jnp.int32, (n, n), 1)
        same_chunk = (r // c_len) == (u // c_len)
        t = r % c_len
        uu = u % c_len
        levels = []
        b = c_len
        while b >= 2:
            half = b // 2
            mid = (t // b) * b + half
            same = same_chunk & ((t // b) == (uu // b))
            if reverse:
                sel_q = same & (t < mid) & (uu >= t) & (uu < mid)
                sel_k = same & (t >= mid) & (uu >= mid) & (uu < t)
            else:
                sel_q = same & (t >= mid) & (uu >= mid) & (uu <= t)
                sel_k = same & (t < mid) & (uu > t) & (uu < mid)
            mq = jnp.where(sel_q, 1.0, 0.0).astype(BF16)
            mk_ = jnp.where(sel_k, 1.0, 0.0).astype(BF16)
            e_q = _dot(mq, g_hi) + _dot(mq, g_lo)
            e_k = _dot(mk_, g_hi) + _dot(mk_, g_lo)
            tq_ = row
            if reverse:
                own = ((tq_ // b) == (col // b)) & ((tq_ % b) < half) & ((col % b) >= half)
            else:
                own = ((tq_ // b) == (col // b)) & ((tq_ % b) >= half) & ((col % b) < half)
            levels.append((e_q, e_k, own))
            b = half
        return levels

    levels = exact_levels() if exact else None

    def intra_scores(sl, xc, qc, kc):
        if not exact:
            ref = xc[c_len // 2:c_len // 2 + 1]
            q_mid = qc * jnp.exp(jnp.minimum(xc - ref, GLA_EXP_CLAMP))
            k_mid = (kc * jnp.exp(jnp.minimum(ref - xc, GLA_EXP_CLAMP))).astype(BF16)
            return jnp.where(pair_mask, _dot_nt(stack_heads(q_mid), k_mid), 0.0)
        attn = jnp.where(row == col, _dot_nt(stack_heads(qc), kc.astype(BF16)), 0.0)
        for e_q, e_k, own in levels:
            a = _dot_nt(stack_heads(qc * jnp.exp(e_q[sl])), (kc * jnp.exp(e_k[sl])).astype(BF16))
            attn = attn + jnp.where(own, a, 0.0)
        return attn

    def step(c):
        sl = slice(c * c_len, (c + 1) * c_len)
        xc = cum[sl]
        tot = tot_rows[c:c + 1]
        qc = q_ref[sl, :].astype(F32) * (GLA_DK ** -0.5)
        kc = k_ref[sl, :].astype(F32)
        vc = v_ref[sl, :]
        q_dec = qc * jnp.exp(xc)
        k_dec = kc * jnp.exp(tot - xc)

        s_prev = s_scr[...]
        o_inter = _dot(stack_heads(q_dec), s_prev.astype(BF16))
        attn = intra_scores(sl, xc, qc, kc).astype(BF16)
        kv = []
        for h in range(GLA_HEADS):
            rs = slice(h * c_len, (h + 1) * c_len)
            vs = slice(h * GLA_DV, (h + 1) * GLA_DV)
            o_h = o_inter[rs] + _dot(attn[rs], vc[:, vs])
            o_ref[sl, vs] = o_h.astype(BF16)
            kv.append(_dot_tn(k_dec[:, h * GLA_DK:(h + 1) * GLA_DK].astype(BF16), vc[:, vs]))
        s_scr[...] = s_prev * jnp.exp(tot_cols[:, c:c + 1]) + jnp.concatenate(kv, axis=0)

    return step


def _gla_kernel(gate_ref, qf_ref, kf_ref, vf_ref, mf_ref, qb_ref, kb_ref, vb_ref, mb_ref,
                wg_ref, bg_ref, tri_ref, s0_ref, of_ref, ob_ref, sfin_ref, sf_scr, sb_scr, *, n_chunks):
    blk = pl.program_id(1)
    n_blk = pl.num_programs(1)

    @pl.when(blk == 0)
    def _():
        sf_scr[...] = s0_ref[0, 0]
        sb_scr[...] = s0_ref[1, 0]

    fwd_block = pl.program_id(0) * n_blk + blk
    bwd_block = pl.program_id(0) * n_blk + (n_blk - 1 - blk)
    extreme = jnp.maximum(gate_ref[2 * fwd_block], gate_ref[2 * bwd_block + 1]) > GLA_PRE_SAFE

    def run(exact):
        pre_f = _dot(mf_ref[...], wg_ref[0]) + bg_ref[0]
        pre_b = _dot(mb_ref[...], wg_ref[1]) + bg_ref[1]
        fwd = _gla_direction(qf_ref, kf_ref, vf_ref, pre_f, tri_ref[0], sf_scr, of_ref,
                             reverse=False, n_chunks=n_chunks, exact=exact)
        bwd = _gla_direction(qb_ref, kb_ref, vb_ref, pre_b, tri_ref[1], sb_scr, ob_ref,
                             reverse=True, n_chunks=n_chunks, exact=exact)
        for c in range(n_chunks):
            fwd(c)
            bwd(n_chunks - 1 - c)

    @pl.when(jnp.logical_not(extreme))
    def _():
        run(False)

    @pl.when(extreme)
    def _():
        run(True)

    @pl.when(blk == pl.num_programs(1) - 1)
    def _():
        sfin_ref[0, 0] = sf_scr[...]
        sfin_ref[1, 0] = sb_scr[...]


def _block_diag_tri(n_chunks):
    c = GLA_CHUNK
    eye = np.eye(n_chunks, dtype=np.float32)
    lower = np.kron(eye, np.tril(np.ones((c, c), np.float32)))
    upper = np.kron(eye, np.triu(np.ones((c, c), np.float32)))
    return jnp.asarray(np.stack([lower, upper]), dtype=BF16)


def _gla(p, gate_bound, wg, bg, s0, *, batch, cb):
    t_all = p.shape[0]
    nblk = t_all // batch // cb
    n_chunks = cb // GLA_CHUNK
    assert n_chunks <= 8 and gate_bound.shape[0] == batch * nblk
    gate = gate_bound[:, 0:2, 0].reshape(-1)

    fw = lambda b, i, gate_ref: b * nblk + i
    bw = lambda b, i, gate_ref: b * nblk + (nblk - 1 - i)
    full = lambda a: pl.BlockSpec(a.shape, lambda b, i, gate_ref: (0,) * a.ndim)
    tri = _block_diag_tri(n_chunks)

    def token_specs(tok):
        return [
            pl.BlockSpec((cb, GLA_QK_W), lambda b, i, g: (tok(b, i, g), P_Q // GLA_QK_W)),
            pl.BlockSpec((cb, GLA_QK_W), lambda b, i, g: (tok(b, i, g), P_K // GLA_QK_W)),
            pl.BlockSpec((cb, GLA_V_W), lambda b, i, g: (tok(b, i, g), P_V // GLA_V_W)),
            pl.BlockSpec((cb, LANES), lambda b, i, g: (tok(b, i, g), P_MISC // LANES)),
        ]

    state_spec = pl.BlockSpec((2, 1, GLA_QK_W, GLA_DV), lambda b, i, g: (0, b, 0, 0))
    grid_spec = pltpu.PrefetchScalarGridSpec(
        num_scalar_prefetch=1,
        grid=(batch, nblk),
        in_specs=token_specs(fw) + token_specs(bw) + [full(wg), full(bg), full(tri), state_spec],
        out_specs=[
            pl.BlockSpec((cb, GLA_V_W), lambda b, i, g: (fw(b, i, g), 0)),
            pl.BlockSpec((cb, GLA_V_W), lambda b, i, g: (bw(b, i, g), 0)),
            state_spec,
        ],
        scratch_shapes=[pltpu.VMEM((GLA_QK_W, GLA_DV), F32), pltpu.VMEM((GLA_QK_W, GLA_DV), F32)],
    )
    return pl.pallas_call(
        functools.partial(_gla_kernel, n_chunks=n_chunks),
        grid_spec=grid_spec,
        out_shape=[
            jax.ShapeDtypeStruct((t_all, GLA_V_W), BF16),
            jax.ShapeDtypeStruct((t_all, GLA_V_W), BF16),
            jax.ShapeDtypeStruct((2, batch, GLA_QK_W, GLA_DV), F32),
        ],
        compiler_params=_cparams(("arbitrary", "arbitrary")),
        name="gla",
    )(gate, p, p, p, p, p, p, p, p, wg, bg, tri, s0)


def _mlaprep_kernel(cq_ref, ckv_ref, misc_ref, cos_ref, sin_ref, qg_ref, kvg_ref,
                    wqn_ref, wqr_ref, wqs_ref, wknt_ref, wv_ref, perm_ref, eye_ref,
                    q_ref, kt_ref, v_ref):
    tm = cq_ref.shape[0]
    n_groups = 2 if tm % 256 == 0 else 1
    for g in range(n_groups):
        rows = slice(g * tm // n_groups, (g + 1) * tm // n_groups)
        cos = cos_ref[rows, :]
        sin = sin_ref[rows, :]
        cqn = _rms(cq_ref[rows, :].astype(F32), qg_ref[...]).astype(BF16)
        qn = _dot(cqn, wqn_ref[...])
        qr = _dot(cqn, wqr_ref[...])
        qs = _dot(cqn, wqs_ref[...])
        for h in range(MLA_HEADS):
            ls = slice(h * LANES, (h + 1) * LANES)
            q_ref[0, h, rows, 0:MLA_NOPE] = (qn[:, ls] * MLA_Q_SCALE).astype(BF16)
            rot = qr[:, ls] * cos + qs[:, ls] * sin
            q_ref[0, h, rows, MLA_NOPE:MLA_QK] = (rot[:, 0:MLA_ROPE] * MLA_Q_SCALE).astype(BF16)

        ckvn = _rms(ckv_ref[rows, :].astype(F32), kvg_ref[...]).astype(BF16)
        knt = _dot_nt(wknt_ref[...], ckvn)
        vv = _dot(ckvn, wv_ref[...])
        misc = misc_ref[rows, :]
        kr = misc.astype(F32) * cos + _dot(misc, perm_ref[...]) * sin
        krt = _dot_nt(eye_ref[...], kr.astype(BF16)).astype(BF16)
        for h in range(MLA_HEADS):
            kt_ref[0, h, 0, 0:MLA_NOPE, rows] = knt[h * MLA_NOPE:(h + 1) * MLA_NOPE].astype(BF16)
            kt_ref[0, h, 0, MLA_NOPE:MLA_QK, rows] = krt
            v_ref[0, h, rows, 0:MLA_V] = vv[:, h * MLA_V:(h + 1) * MLA_V].astype(BF16)
            v_ref[0, h, rows, MLA_V:MLA_V_EXT] = jnp.ones((vv.shape[0], MLA_V), BF16)


def _mlaprep(p, cos, sin, qg, kvg, wts, *, batch, tm):
    t_all = p.shape[0]
    t = t_all // batch
    nb = t // tm
    ntab = cos.shape[0] // tm
    wqn, wqr, wqs, wknt, wv, perm, eye = wts
    full = lambda a: pl.BlockSpec(a.shape, lambda b, i: (0,) * a.ndim)
    return pl.pallas_call(
        _mlaprep_kernel,
        grid=(batch, nb),
        in_specs=[
            pl.BlockSpec((tm, MLA_Q_RANK), lambda b, i: (b * nb + i, P_CQ // MLA_Q_RANK)),
            pl.BlockSpec((tm, MLA_KV_RANK), lambda b, i: (b * nb + i, P_CKV // MLA_KV_RANK)),
            pl.BlockSpec((tm, LANES), lambda b, i: (b * nb + i, P_MISC // LANES)),
            pl.BlockSpec((tm, LANES), lambda b, i: (i % ntab, 0)),
            pl.BlockSpec((tm, LANES), lambda b, i: (i % ntab, 0)),
            full(qg), full(kvg), full(wqn), full(wqr), full(wqs), full(wknt), full(wv),
            full(perm), full(eye),
        ],
        out_specs=[
            pl.BlockSpec((1, MLA_HEADS, tm, MLA_QK), lambda b, i: (b, 0, i, 0)),
            pl.BlockSpec((1, MLA_HEADS, 1, MLA_QK, tm), lambda b, i: (b, 0, i, 0, 0)),
            pl.BlockSpec((1, MLA_HEADS, tm, MLA_V_EXT), lambda b, i: (b, 0, i, 0)),
        ],
        out_shape=[
            jax.ShapeDtypeStruct((batch, MLA_HEADS, t, MLA_QK), BF16),
            jax.ShapeDtypeStruct((batch, MLA_HEADS, nb, MLA_QK, tm), BF16),
            jax.ShapeDtypeStruct((batch, MLA_HEADS, t, MLA_V_EXT), BF16),
        ],
        compiler_params=_cparams(("arbitrary", "arbitrary")),
        name="mlaprep",
    )(p, p, p, cos, sin, qg, kvg, wqn, wqr, wqs, wknt, wv, perm, eye)


def _attn_kernel(*refs, n_seg, n_sub):
    q_ref = refs[0]
    kt_refs = refs[1:1 + 2 * n_seg:2]
    v_refs = refs[2:2 + 2 * n_seg:2]
    o_ref = refs[1 + 2 * n_seg]
    m_scr, acc_scr = refs[2 + 2 * n_seg:]

    rows_per_sub = q_ref.shape[2] // n_sub
    m_scr[...] = jnp.full(m_scr.shape, -jnp.inf, F32)
    acc_scr[...] = jnp.zeros(acc_scr.shape, F32)

    for kt_ref, v_ref in zip(kt_refs, v_refs):
        n_blocks, tk = kt_ref.shape[2], kt_ref.shape[4]

        def step(j, carry, kt_ref=kt_ref, v_ref=v_ref, tk=tk):
            kt = kt_ref[0, 0, j]
            v_blk = v_ref[0, 0, pl.ds(pl.multiple_of(j * tk, tk), tk), :]
            for u in range(n_sub):
                rows = slice(u * rows_per_sub, (u + 1) * rows_per_sub)
                s = _dot(q_ref[0, 0, rows, :], kt)
                m_prev = m_scr[rows, :]
                m_next = jnp.maximum(m_prev, jnp.max(s, axis=1, keepdims=True))
                p = jnp.exp2((s - jnp.concatenate([m_next] * (tk // LANES), axis=1)).astype(BF16))
                alpha = jnp.exp2(m_prev - m_next)
                acc_scr[rows, :] = (jnp.concatenate([alpha] * (MLA_V_EXT // LANES), axis=1)
                                    * acc_scr[rows, :] + _dot(p, v_blk))
                m_scr[rows, :] = m_next
            return carry

        lax.fori_loop(0, n_blocks, step, 0)

    o_ref[0] = (acc_scr[:, 0:MLA_V] / acc_scr[:, MLA_V:MLA_V_EXT]).astype(BF16)


def _attn_pipe_kernel(q_ref, kt_ref, v_ref, ktt_ref, vt_ref, o_ref,
                      m_scr, acc_scr, s0_scr, s1_scr, st_scr, *, n_sub):
    n_blocks, tk = kt_ref.shape[2], kt_ref.shape[4]
    tq = m_scr.shape[0]
    n_q = q_ref.shape[2] // tq
    rows_per_sub = tq // n_sub
    subs = [slice(u * rows_per_sub, (u + 1) * rows_per_sub) for u in range(n_sub)]
    bufs = (s0_scr, s1_scr)

    def reset():
        m_scr[...] = jnp.full(m_scr.shape, -jnp.inf, F32)
        acc_scr[...] = jnp.zeros(acc_scr.shape, F32)

    def scores(qi, kt, s_ref):
        for rows in subs:
            q_rows = pl.ds(pl.multiple_of(qi * tq + rows.start, rows_per_sub), rows_per_sub)
            s_ref[rows, :] = _dot(q_ref[0, 0, q_rows, :], kt)

    def softmax_pv(s_ref, v_blk):
        width = s_ref.shape[1]
        for rows in subs:
            s = s_ref[rows, :]
            m_prev = m_scr[rows, :]
            m_next = jnp.maximum(m_prev, jnp.max(s, axis=1, keepdims=True))
            m_scr[rows, :] = m_next
            alpha = jnp.exp2(m_prev - m_next)
            p = jnp.exp2((s - jnp.concatenate([m_next] * (width // LANES), axis=1)).astype(BF16))
            acc_scr[rows, :] = (jnp.concatenate([alpha] * (MLA_V_EXT // LANES), axis=1)
                                * acc_scr[rows, :] + _dot(p, v_blk))

    reset()
    scores(0, kt_ref[0, 0, 0], bufs[0])

    def query_block(qi, carry):
        for j in range(n_blocks):
            softmax_pv(bufs[j % 2], v_ref[0, 0, j * tk:(j + 1) * tk, :])
            if j + 1 < n_blocks:
                scores(qi, kt_ref[0, 0, j + 1], bufs[(j + 1) % 2])
            else:
                scores(qi, ktt_ref[0, 0, 0], st_scr)
        softmax_pv(st_scr, vt_ref[0, 0])
        scores(jnp.minimum(qi + 1, n_q - 1), kt_ref[0, 0, 0], bufs[0])
        o_rows = pl.ds(pl.multiple_of(qi * tq, tq), tq)
        o_ref[0, o_rows, :] = (acc_scr[:, 0:MLA_V] / acc_scr[:, MLA_V:MLA_V_EXT]).astype(BF16)
        reset()
        return carry

    lax.fori_loop(0, n_q, query_block, 0)


def _attention_pipelined(q, kt, v, kt_tail, v_tail, *, tq, n_sub):
    b, h, t, dqk = q.shape
    tk, tt = kt.shape[4], kt_tail.shape[4]
    assert kt_tail.shape[2] == 1
    return pl.pallas_call(
        functools.partial(_attn_pipe_kernel, n_sub=n_sub),
        grid=(b, h),
        in_specs=[
            pl.BlockSpec((1, 1, t, dqk), lambda bi, hi: (bi, hi, 0, 0)),
            pl.BlockSpec((1, 1) + kt.shape[2:], lambda bi, hi: (bi, hi, 0, 0, 0)),
            pl.BlockSpec((1, 1) + v.shape[2:], lambda bi, hi: (bi, hi, 0, 0)),
            pl.BlockSpec((1, 1) + kt_tail.shape[2:], lambda bi, hi: (bi, hi, 0, 0, 0)),
            pl.BlockSpec((1, 1) + v_tail.shape[2:], lambda bi, hi: (bi, hi, 0, 0)),
        ],
        out_specs=pl.BlockSpec((1, t, MLA_V), lambda bi, hi: (bi, 0, hi)),
        out_shape=jax.ShapeDtypeStruct((b, t, h * MLA_V), BF16),
        scratch_shapes=[pltpu.VMEM((tq, LANES), F32), pltpu.VMEM((tq, MLA_V_EXT), F32),
                        pltpu.VMEM((tq, tk), F32), pltpu.VMEM((tq, tk), F32),
                        pltpu.VMEM((tq, tt), F32)],
        compiler_params=_cparams(("arbitrary", "arbitrary")),
        name="mla_attention_pipe",
    )(q, kt, v, kt_tail, v_tail)


def _attention(q, segs, *, tq, n_sub):
    b, h, t, dqk = q.shape
    in_specs = [pl.BlockSpec((1, 1, tq, dqk), lambda bi, hi, qi: (bi, hi, qi, 0))]
    args = [q]
    for kt, v in segs:
        in_specs.append(pl.BlockSpec((1, 1) + kt.shape[2:], lambda bi, hi, qi: (bi, hi, 0, 0, 0)))
        in_specs.append(pl.BlockSpec((1, 1) + v.shape[2:], lambda bi, hi, qi: (bi, hi, 0, 0)))
        args += [kt, v]
    return pl.pallas_call(
        functools.partial(_attn_kernel, n_seg=len(segs), n_sub=n_sub),
        grid=(b, h, t // tq),
        in_specs=in_specs,
        out_specs=pl.BlockSpec((1, tq, MLA_V), lambda bi, hi, qi: (bi, qi, hi)),
        out_shape=jax.ShapeDtypeStruct((b, t, h * MLA_V), BF16),
        scratch_shapes=[pltpu.VMEM((tq, LANES), F32), pltpu.VMEM((tq, MLA_V_EXT), F32)],
        compiler_params=_cparams(("arbitrary", "arbitrary", "arbitrary")),
        name="mla_attention",
    )(*args)


def _mix_residual_norm(x_ref, of_ref, ob_ref, r_ref, mla_ref, m, gg_ref, wo_ref, ln2_ref,
                       rows=slice(None)):
    o = of_ref[rows, :].astype(F32) + ob_ref[rows, :].astype(F32)
    gg = gg_ref[...]
    y = jnp.concatenate(
        [_rms(o[:, h * GLA_DV:(h + 1) * GLA_DV], gg) for h in range(GLA_HEADS)], axis=1)
    mix = (y * _silu(r_ref[rows, :].astype(F32))).astype(BF16)
    yo = _dot(mix, wo_ref[0:GLA_V_W, :]) + _dot(mla_ref[rows, :], wo_ref[GLA_V_W:, :])
    x1 = x_ref[rows, :] + m[2:3] * yo
    h2 = _rms(x1, ln2_ref[...]) * (1.0 + m[4:5]) + m[3:4]
    return x1, h2


def _outproj_ffn_kernel(*refs, groups, final_norm):
    (x_ref, of_ref, ob_ref, r_ref, mla_ref, mod_ref, gg_ref, wo_ref, ln2_ref,
     wg_ref, wu_ref, wd_ref) = refs[:12]
    fin_ref = refs[12] if final_norm else None
    o_ref = refs[-1]
    m = mod_ref[0]
    x1, h2 = _mix_residual_norm(x_ref, of_ref, ob_ref, r_ref, mla_ref, m, gg_ref, wo_ref, ln2_ref)
    h = h2.astype(BF16)
    y = None
    for lo, hi in groups:
        a = _dot(h, wg_ref[:, lo:hi])
        u = _dot(h, wu_ref[:, lo:hi])
        part = _dot((_silu(a) * u).astype(BF16), wd_ref[lo:hi, :])
        y = part if y is None else y + part
    x2 = x1 + m[5:6] * y
    if final_norm:
        x2 = _rms(x2, fin_ref[...])
    o_ref[...] = x2


def _outproj_ffn(x, o_f, o_b, p, mla, mods, row_fn, gg, wo, ln2, wg, wu, wd, fin_g, tm):
    t, d = x.shape
    ff = wg.shape[1]
    final_norm = fin_g is not None
    step = min(ff, DENSE_FF_GROUP)
    groups = tuple((lo, min(lo + step, ff)) for lo in range(0, ff, step))
    resident = lambda a: pl.BlockSpec(a.shape, lambda i: (0,) * a.ndim, pipeline_mode=pl.Buffered(1))
    in_specs = [
        pl.BlockSpec((tm, d), lambda i: (i, 0)),
        pl.BlockSpec((tm, GLA_V_W), lambda i: (i, 0)),
        pl.BlockSpec((tm, GLA_V_W), lambda i: (i, 0)),
        pl.BlockSpec((tm, GLA_V_W), lambda i: (i, P_R // GLA_V_W)),
        pl.BlockSpec((tm, MLA_V_W), lambda i: (i, 0)),
        pl.BlockSpec((1, 6, d), lambda i: (row_fn(i), 0, 0)),
        resident(gg), resident(wo), resident(ln2), resident(wg), resident(wu), resident(wd),
    ]
    args = [x, o_f, o_b, p, mla, mods, gg, wo, ln2, wg, wu, wd]
    if final_norm:
        in_specs.append(resident(fin_g))
        args.append(fin_g)
    return pl.pallas_call(
        functools.partial(_outproj_ffn_kernel, groups=groups, final_norm=final_norm),
        grid=(t // tm,),
        in_specs=in_specs,
        out_specs=pl.BlockSpec((tm, d), lambda i: (i, 0)),
        out_shape=jax.ShapeDtypeStruct((t, d), F32),
        compiler_params=_cparams(("arbitrary",)),
        name="outproj_ffn",
    )(*args)


def _outproj_kernel(*refs, with_router):
    (x_ref, of_ref, ob_ref, r_ref, mla_ref, mod_ref, gg_ref, wo_ref, ln2_ref) = refs[:9]
    if with_router:
        rwh_ref, rwl_ref, x1_ref, h2_ref, comb_ref = refs[9:]
    else:
        x1_ref, h2_ref = refs[9:]
    m = mod_ref[0]
    tm = x_ref.shape[0]
    n_groups = 2 if tm % 32 == 0 else 1
    for g in range(n_groups):
        rows = slice(g * tm // n_groups, (g + 1) * tm // n_groups)
        x1, h2 = _mix_residual_norm(x_ref, of_ref, ob_ref, r_ref, mla_ref, m, gg_ref, wo_ref,
                                    ln2_ref, rows)
        x1_ref[rows, :] = x1
        if not with_router:
            h2_ref[rows, :] = h2.astype(BF16)
            continue
        h2_ref[rows, :] = _pack_bf16_pairs(h2)
        h_hi = h2.astype(BF16)
        h_lo = (h2 - h_hi.astype(F32)).astype(BF16)
        logits = _dot(h_hi, rwh_ref[...]) + _dot(h_lo, rwh_ref[...]) + _dot(h_hi, rwl_ref[...])
        lane = lax.broadcasted_iota(jnp.int32, logits.shape, 1).astype(F32)
        neg = jnp.float32(-jnp.inf)
        logits = jnp.where(lane < N_EXPERTS, logits, neg)
        m1 = jnp.max(logits, axis=1, keepdims=True)
        i1 = jnp.min(jnp.where(logits == m1, lane, float(LANES)), axis=1, keepdims=True)
        rest = jnp.where(lane == i1, neg, logits)
        m2 = jnp.max(rest, axis=1, keepdims=True)
        i2 = jnp.min(jnp.where(rest == m2, lane, float(LANES)), axis=1, keepdims=True)
        e2 = jnp.exp(m2 - m1)
        w1 = 1.0 / (1.0 + e2)
        comb_ref[rows, :] = (jnp.where(lane == ROUTE_E1, i1, 0.0) + jnp.where(lane == ROUTE_E2, i2, 0.0)
                             + jnp.where(lane == ROUTE_W1, w1, 0.0)
                             + jnp.where(lane == ROUTE_W2, e2 * w1, 0.0))


def _outproj(x, o_f, o_b, p, mla, mods, row_fn, gg, wo, ln2, router, tm):
    t, d = x.shape
    with_router = router is not None
    full = lambda a: pl.BlockSpec(a.shape, lambda i: (0,) * a.ndim)
    in_specs = [
        pl.BlockSpec((tm, d), lambda i: (i, 0)),
        pl.BlockSpec((tm, GLA_V_W), lambda i: (i, 0)),
        pl.BlockSpec((tm, GLA_V_W), lambda i: (i, 0)),
        pl.BlockSpec((tm, GLA_V_W), lambda i: (i, P_R // GLA_V_W)),
        pl.BlockSpec((tm, MLA_V_W), lambda i: (i, 0)),
        pl.BlockSpec((1, 6, d), lambda i: (row_fn(i), 0, 0)),
        full(gg), full(wo), full(ln2),
    ]
    args = [x, o_f, o_b, p, mla, mods, gg, wo, ln2]
    h2_shape = jax.ShapeDtypeStruct((t, d // 2), F32) if with_router else jax.ShapeDtypeStruct((t, d), BF16)
    out_specs = [pl.BlockSpec((tm, d), lambda i: (i, 0)),
                 pl.BlockSpec((tm, h2_shape.shape[1]), lambda i: (i, 0))]
    out_shape = [jax.ShapeDtypeStruct((t, d), F32), h2_shape]
    if with_router:
        in_specs += [full(router[0]), full(router[1])]
        args += list(router)
        out_specs.append(pl.BlockSpec((tm, LANES), lambda i: (i, 0)))
        out_shape.append(jax.ShapeDtypeStruct((t, LANES), F32))
    return pl.pallas_call(
        functools.partial(_outproj_kernel, with_router=with_router),
        grid=(t // tm,),
        in_specs=in_specs,
        out_specs=out_specs,
        out_shape=out_shape,
        compiler_params=_cparams(("arbitrary",)),
        name="outproj",
    )(*args)


def _ffn_kernel(*refs, with_comb, final_norm):
    h_ref, x1_ref, mod_ref = refs[:3]
    k = 3
    comb_ref = fin_ref = None
    if with_comb:
        comb_ref = refs[k]
        k += 1
    wg_ref, wu_ref, wd_ref = refs[k:k + 3]
    k += 3
    if final_norm:
        fin_ref = refs[k]
        k += 1
    o_ref, acc = refs[k:]
    e = pl.program_id(1)
    f = pl.program_id(2)

    @pl.when((e == 0) & (f == 0))
    def _():
        acc[...] = jnp.zeros(acc.shape, F32)

    h = h_ref[...]
    a = _dot(h, wg_ref[0])
    u = _dot(h, wu_ref[0])
    act = _silu(a) * u
    if with_comb:
        comb = comb_ref[...]
        lane = lax.broadcasted_iota(jnp.int32, comb.shape, 1)
        act = act * jnp.sum(jnp.where(lane == e, comb, 0.0), axis=1, keepdims=True)
    acc[...] += _dot(act.astype(BF16), wd_ref[0])

    @pl.when((e == pl.num_programs(1) - 1) & (f == pl.num_programs(2) - 1))
    def _():
        x2 = x1_ref[...] + mod_ref[0][5:6] * acc[...]
        if final_norm:
            x2 = _rms(x2, fin_ref[...])
        o_ref[...] = x2


def _ffn(h2, x1, mods, row_fn, comb, wg, wu, wd, fin_g, tm, tf):
    t, d = x1.shape
    n_e, _, ff = wg.shape
    with_comb = comb is not None
    final_norm = fin_g is not None
    in_specs = [
        pl.BlockSpec((tm, d), lambda i, e, f: (i, 0)),
        pl.BlockSpec((tm, d), lambda i, e, f: (i, 0)),
        pl.BlockSpec((1, 6, d), lambda i, e, f: (row_fn(i), 0, 0)),
    ]
    args = [h2, x1, mods]
    if with_comb:
        in_specs.append(pl.BlockSpec((tm, LANES), lambda i, e, f: (i, 0)))
        args.append(comb)
    in_specs += [
        pl.BlockSpec((1, d, tf), lambda i, e, f: (e, 0, f)),
        pl.BlockSpec((1, d, tf), lambda i, e, f: (e, 0, f)),
        pl.BlockSpec((1, tf, d), lambda i, e, f: (e, f, 0)),
    ]
    args += [wg, wu, wd]
    if final_norm:
        in_specs.append(pl.BlockSpec((1, d), lambda i, e, f: (0, 0)))
        args.append(fin_g)
    return pl.pallas_call(
        functools.partial(_ffn_kernel, with_comb=with_comb, final_norm=final_norm),
        grid=(t // tm, n_e, ff // tf),
        in_specs=in_specs,
        out_specs=pl.BlockSpec((tm, d), lambda i, e, f: (i, 0)),
        out_shape=jax.ShapeDtypeStruct((t, d), F32),
        scratch_shapes=[pltpu.VMEM((tm, d), F32)],
        compiler_params=_cparams(("arbitrary", "arbitrary", "arbitrary")),
        name="ffn",
    )(*args)


def _sc_row_gather(table, idx):
    _, w = table.shape
    b = idx.shape[0]
    n_workers = SC_CORES * SC_SUBCORES
    assert b % (n_workers * SC_GATHER_ROWS) == 0, (b, n_workers, SC_GATHER_ROWS)
    b_per_w = b // n_workers
    n_chunks = b_per_w // SC_GATHER_ROWS
    mesh = plsc.VectorSubcoreMesh(core_axis_name="c", subcore_axis_name="s",
                                  num_cores=SC_CORES, num_subcores=SC_SUBCORES)

    def body(table_hbm, idx_hbm, out_hbm, idx_a, idx_b, rows_a, rows_b, sem_a, sem_b):
        wid = lax.axis_index("s") * SC_CORES + lax.axis_index("c")
        base = wid * b_per_w
        idx_bufs, row_bufs, sems = (idx_a, idx_b), (rows_a, rows_b), (sem_a, sem_b)

        def start(ci):
            slot = ci % 2
            pltpu.sync_copy(idx_hbm.at[pl.ds(base + ci * SC_GATHER_ROWS, SC_GATHER_ROWS)], idx_bufs[slot])
            return pltpu.async_copy(table_hbm.at[idx_bufs[slot]], row_bufs[slot], sems[slot])

        pending = start(0)
        for ci in range(n_chunks):
            following = start(ci + 1) if ci + 1 < n_chunks else None
            pending.wait()
            pltpu.sync_copy(row_bufs[ci % 2], out_hbm.at[pl.ds(base + ci * SC_GATHER_ROWS, SC_GATHER_ROWS)])
            pending = following

    return pl.kernel(
        body,
        out_type=jax.ShapeDtypeStruct((b, w), F32),
        mesh=mesh,
        scratch_types=[pltpu.VMEM((SC_GATHER_ROWS,), jnp.int32)] * 2
        + [pltpu.VMEM((SC_GATHER_ROWS, w), F32)] * 2 + [pltpu.SemaphoreType.DMA] * 2,
        name="sc_row_gather",
    )(table, idx)


def _sc_row_scatter2(table, pos, n_out):
    t, w = table.shape
    n_workers = SC_CORES * SC_SUBCORES
    assert t % (n_workers * SC_GATHER_ROWS) == 0, (t, n_workers, SC_GATHER_ROWS)
    t_per_w = t // n_workers
    n_chunks = t_per_w // SC_GATHER_ROWS
    mesh = plsc.VectorSubcoreMesh(core_axis_name="c", subcore_axis_name="s",
                                  num_cores=SC_CORES, num_subcores=SC_SUBCORES)

    def body(table_hbm, pos_hbm, out_hbm, i0a, i1a, i0b, i1b, rows_a, rows_b, sem_a, sem_b):
        wid = lax.axis_index("s") * SC_CORES + lax.axis_index("c")
        base = wid * t_per_w
        idx0, idx1, row_bufs, sems = (i0a, i0b), (i1a, i1b), (rows_a, rows_b), (sem_a, sem_b)

        def drain(pair):
            if pair is not None:
                pair[0].wait()
                pair[1].wait()

        pending = [None, None]
        for ci in range(n_chunks):
            slot = ci % 2
            drain(pending[slot])
            off = base + ci * SC_GATHER_ROWS
            pltpu.sync_copy(pos_hbm.at[pl.ds(off, SC_GATHER_ROWS)], idx0[slot])
            pltpu.sync_copy(pos_hbm.at[pl.ds(t + off, SC_GATHER_ROWS)], idx1[slot])
            pltpu.sync_copy(table_hbm.at[pl.ds(off, SC_GATHER_ROWS)], row_bufs[slot])
            pending[slot] = (pltpu.async_copy(row_bufs[slot], out_hbm.at[idx0[slot]], sems[slot]),
                             pltpu.async_copy(row_bufs[slot], out_hbm.at[idx1[slot]], sems[slot]))
        drain(pending[0])
        drain(pending[1])

    return pl.kernel(
        body,
        out_type=jax.ShapeDtypeStruct((n_out, w), F32),
        mesh=mesh,
        scratch_types=[pltpu.VMEM((SC_GATHER_ROWS,), jnp.int32)] * 4
        + [pltpu.VMEM((SC_GATHER_ROWS, w), F32)] * 2 + [pltpu.SemaphoreType.DMA] * 2,
        name="sc_row_scatter",
    )(table, pos)


def _moe_plan_kernel(route_ref, tri_ref, utri_ref, pos_ref, cnt_ref, run_scr, off_scr):
    phase = pl.program_id(0)
    blk = pl.program_id(1)
    route = route_ref[...]
    lane = lax.broadcasted_iota(jnp.int32, route.shape, 1).astype(F32)
    oh1 = jnp.where(lane == route[:, ROUTE_E1:ROUTE_E1 + 1], 1.0, 0.0)
    oh2 = jnp.where(lane == route[:, ROUTE_E2:ROUTE_E2 + 1], 1.0, 0.0)
    oh = oh1 + oh2

    @pl.when(blk == 0)
    def _():
        run_scr[...] = jnp.zeros(run_scr.shape, F32)

    @pl.when(phase == 0)
    def _():
        run_scr[...] += jnp.sum(oh, axis=0, keepdims=True)

        @pl.when(blk == pl.num_programs(1) - 1)
        def _():
            counts = run_scr[...]
            cnt_ref[...] = counts
            tiles_per = jnp.floor((counts + (MOE_TILE - 1.0)) * (1.0 / MOE_TILE))
            tile_end = _dot(jnp.broadcast_to(tiles_per, (8, LANES)).astype(BF16), utri_ref[...])[0:1]
            off_scr[...] = (tile_end - tiles_per) * float(MOE_TILE)

    @pl.when(phase == 1)
    def _():
        incl = _dot(tri_ref[...], oh.astype(BF16))
        before = incl - oh + run_scr[...] + off_scr[...]
        p1 = jnp.sum(before * oh1, axis=1, keepdims=True)
        p2 = jnp.sum(before * oh2, axis=1, keepdims=True)
        pos = jnp.where(lane == 0.0, p1, 0.0) + jnp.where(lane == 1.0, p2, 0.0)
        pos_ref[...] = pos.astype(jnp.int32)
        run_scr[...] += incl[incl.shape[0] - 1:, :]


def _moe_plan(route, n_tiles, tm):
    t = route.shape[0]
    tri = jnp.asarray(np.tril(np.ones((tm, tm), np.float32)), dtype=BF16)
    utri = jnp.asarray(np.triu(np.ones((LANES, LANES), np.float32)), dtype=BF16)
    pos, counts = pl.pallas_call(
        _moe_plan_kernel,
        grid=(2, t // tm),
        in_specs=[
            pl.BlockSpec((tm, LANES), lambda p, i: (i, 0)),
            pl.BlockSpec((tm, tm), lambda p, i: (0, 0)),
            pl.BlockSpec((LANES, LANES), lambda p, i: (0, 0)),
        ],
        out_specs=[
            pl.BlockSpec((tm, LANES), lambda p, i: (i * p, 0)),
            pl.BlockSpec((1, LANES), lambda p, i: (0, 0)),
        ],
        out_shape=[jax.ShapeDtypeStruct((t, LANES), jnp.int32),
                   jax.ShapeDtypeStruct((1, LANES), F32)],
        scratch_shapes=[pltpu.VMEM((1, LANES), F32), pltpu.VMEM((1, LANES), F32)],
        compiler_params=_cparams(("arbitrary", "arbitrary")),
        name="moe_plan",
    )(route, tri, utri)
    counts = counts[0, :N_EXPERTS].astype(jnp.int32)
    tile_end = jnp.cumsum((counts + MOE_TILE - 1) // MOE_TILE)
    n_used = tile_end[-1]
    tile_ids = jnp.minimum(jnp.arange(n_tiles, dtype=jnp.int32), n_used - 1)
    tile_expert = jnp.sum((tile_end[None, :] <= tile_ids[:, None]).astype(jnp.int32), axis=1)
    return pos[:, :2].T, tile_expert, n_used.reshape(1)


def _moe_ffn_kernel(te_ref, nused_ref, xs_ref, wg_ref, wu_ref, wd_ref, o_ref, acc, h_scr, *, splits):
    i = pl.program_id(0)
    f = pl.program_id(1)
    last_f = pl.num_programs(1) - 1
    used = i < nused_ref[0]

    @pl.when(used)
    def _():
        @pl.when(f == 0)
        def _():
            acc[...] = jnp.zeros(acc.shape, F32)
            h_scr[...] = _unpack_bf16_pairs(xs_ref[...]).astype(BF16)

        h = h_scr[...]
        for lo, hi in splits:
            a = _dot(h, wg_ref[0, :, lo:hi].astype(BF16))
            u = _dot(h, wu_ref[0, :, lo:hi].astype(BF16))
            acc[...] += _dot((_silu(a) * u).astype(BF16), wd_ref[0, lo:hi, :].astype(BF16))

        @pl.when(f == last_f)
        def _():
            o_ref[...] = _pack_bf16_pairs(acc[...])

    @pl.when(jnp.logical_not(used) & (f == last_f))
    def _():
        o_ref[...] = jnp.zeros(o_ref.shape, F32)


def _moe_ffn(xs, tile_expert, n_used, wg, wu, wd):
    rows, half = xs.shape
    d = 2 * half
    ff = wg.shape[2]
    if ff % MOE_FF_TILE == 0:
        tf, splits = MOE_FF_TILE, MOE_FF_SPLITS
    else:
        tf, splits = ff, ((0, ff),)
    n_tiles = rows // MOE_TILE
    grid_spec = pltpu.PrefetchScalarGridSpec(
        num_scalar_prefetch=2,
        grid=(n_tiles, ff // tf),
        in_specs=[
            pl.BlockSpec((MOE_TILE, half), lambda i, f, te, nu: (i, 0)),
            pl.BlockSpec((1, d, tf), lambda i, f, te, nu: (te[i], 0, f)),
            pl.BlockSpec((1, d, tf), lambda i, f, te, nu: (te[i], 0, f)),
            pl.BlockSpec((1, tf, d), lambda i, f, te, nu: (te[i], f, 0)),
        ],
        out_specs=pl.BlockSpec((MOE_TILE, half), lambda i, f, te, nu: (i, 0)),
        scratch_shapes=[pltpu.VMEM((MOE_TILE, d), F32), pltpu.VMEM((MOE_TILE, d), BF16)],
    )
    return pl.pallas_call(
        functools.partial(_moe_ffn_kernel, splits=splits),
        grid_spec=grid_spec,
        out_shape=jax.ShapeDtypeStruct((rows, half), F32),
        compiler_params=_cparams(("arbitrary", "arbitrary")),
        name="moe_ffn",
    )(tile_expert, n_used, xs, wg, wu, wd)


def _combine_kernel(*refs, final_norm):
    x1_ref, y0_ref, y1_ref, route_ref, mod_ref = refs[:5]
    fin_ref = refs[5] if final_norm else None
    o_ref = refs[-1]
    route = route_ref[...]
    w1 = route[:, ROUTE_W1:ROUTE_W1 + 1]
    w2 = route[:, ROUTE_W2:ROUTE_W2 + 1]
    y = w1 * _unpack_bf16_pairs(y0_ref[...]) + w2 * _unpack_bf16_pairs(y1_ref[...])
    x2 = x1_ref[...] + mod_ref[0][5:6] * y
    if final_norm:
        x2 = _rms(x2, fin_ref[...])
    o_ref[...] = x2


def _combine(x1, yg, route, mods, row_fn, fin_g, tm):
    t, d = x1.shape
    nb = t // tm
    final_norm = fin_g is not None
    in_specs = [
        pl.BlockSpec((tm, d), lambda i: (i, 0)),
        pl.BlockSpec((tm, d // 2), lambda i: (i, 0)),
        pl.BlockSpec((tm, d // 2), lambda i: (i + nb, 0)),
        pl.BlockSpec((tm, LANES), lambda i: (i, 0)),
        pl.BlockSpec((1, 6, d), lambda i: (row_fn(i), 0, 0)),
    ]
    args = [x1, yg, yg, route, mods]
    if final_norm:
        in_specs.append(pl.BlockSpec((1, d), lambda i: (0, 0)))
        args.append(fin_g)
    return pl.pallas_call(
        functools.partial(_combine_kernel, final_norm=final_norm),
        grid=(nb,),
        in_specs=in_specs,
        out_specs=pl.BlockSpec((tm, d), lambda i: (i, 0)),
        out_shape=jax.ShapeDtypeStruct((t, d), F32),
        compiler_params=_cparams(("arbitrary",)),
        name="moe_combine",
    )(*args)


def _moe(h2, x1, route, mods, row_fn, wg, wu, wd, fin_g, tm):
    t = h2.shape[0]
    n_tiles = -(-2 * t // MOE_TILE) + N_EXPERTS
    pos, tile_expert, n_used = _moe_plan(route, n_tiles, _pick_tile(t, ATTN_TILE))
    pos = pos.reshape(-1)
    xs = _sc_row_scatter2(h2, pos, n_tiles * MOE_TILE)
    ys = _moe_ffn(xs, tile_expert, n_used, wg, wu, wd)
    yg = _sc_row_gather(ys, pos)
    return _combine(x1, yg, route, mods, row_fn, fin_g, tm)


def _rope_partner():
    j = np.arange(MLA_ROPE)
    return np.where((j % 32) < 16, j + 16, j - 16)


def _prep_in_weight(w):
    d = w.shape[0]
    widths = (GLA_QK_W, GLA_QK_W, GLA_V_W, GLA_GATE_RANK, GLA_GATE_RANK, GLA_V_W,
              MLA_Q_RANK, MLA_KV_RANK, MLA_ROPE)
    offs = np.concatenate([[0], np.cumsum(widths)])
    part = lambda i: w[:, offs[i]:offs[i + 1]]
    assert w.shape[1] == offs[-1] and P_MISC + MISC_GB + GLA_GATE_RANK <= P_WIDTH
    cols = [part(0), part(1), part(2), part(5), part(6), part(7), part(8), part(3), part(4),
            jnp.zeros((d, P_WIDTH - int(offs[-1])), w.dtype)]
    return jnp.concatenate(cols, axis=1).astype(BF16)


def _prep_gate_weight(w_g2, b_g2):
    ws = []
    for z, off in ((0, MISC_GF), (1, MISC_GB)):
        ws.append(jnp.zeros((LANES, GLA_QK_W), F32).at[off:off + GLA_GATE_RANK].set(w_g2[z]))
    return jnp.stack(ws).astype(BF16), b_g2.reshape(2, 1, GLA_QK_W)


def _prep_mla_weights(w_uq, w_ukv):
    partner = _rope_partner()
    wq = w_uq.reshape(MLA_Q_RANK, MLA_HEADS, MLA_QK)
    wqn = wq[:, :, :MLA_NOPE].reshape(MLA_Q_RANK, MLA_HEADS * MLA_NOPE)
    rope = wq[:, :, MLA_NOPE:]
    pad = jnp.zeros((MLA_Q_RANK, MLA_HEADS, LANES - MLA_ROPE), w_uq.dtype)
    wqr = jnp.concatenate([rope, pad], axis=2).reshape(MLA_Q_RANK, MLA_HEADS * LANES)
    wqs = jnp.concatenate([rope[:, :, partner], pad], axis=2).reshape(MLA_Q_RANK, MLA_HEADS * LANES)
    wkv = w_ukv.reshape(MLA_KV_RANK, MLA_HEADS, MLA_NOPE + MLA_V)
    wknt = wkv[:, :, :MLA_NOPE].reshape(MLA_KV_RANK, MLA_HEADS * MLA_NOPE).T
    wv = wkv[:, :, MLA_NOPE:].reshape(MLA_KV_RANK, MLA_HEADS * MLA_V)
    perm = np.zeros((LANES, LANES), np.float32)
    perm[partner, np.arange(MLA_ROPE)] = 1.0
    eye = np.eye(MLA_ROPE, LANES, dtype=np.float32)
    return (wqn.astype(BF16), wqr.astype(BF16), wqs.astype(BF16), wknt.astype(BF16),
            wv.astype(BF16), jnp.asarray(perm, BF16), jnp.asarray(eye, BF16))


def _rope_tables(n_tok):
    rows = n_tok // GRID_W
    row = np.repeat(np.arange(rows, dtype=np.float32), GRID_W)
    col = np.tile(np.arange(GRID_W, dtype=np.float32), rows)
    nfreq = MLA_ROPE // 4
    inv = np.float32(ROPE_BASE) ** (-np.arange(nfreq, dtype=np.float32) / np.float32(nfreq))
    ar = (row[:, None] * inv).astype(np.float32)
    ac = (col[:, None] * inv).astype(np.float32)
    zero = np.zeros((n_tok, LANES - MLA_ROPE), np.float32)
    cos = np.concatenate([np.cos(ar), np.cos(ar), np.cos(ac), np.cos(ac), zero], axis=1)
    sin = np.concatenate([-np.sin(ar), np.sin(ar), -np.sin(ac), np.sin(ac), zero], axis=1)
    return jnp.asarray(cos, F32), jnp.asarray(sin, F32)


def _identity_tables(n_tok):
    cos = jnp.concatenate([jnp.ones((n_tok, MLA_ROPE), F32),
                           jnp.zeros((n_tok, LANES - MLA_ROPE), F32)], axis=1)
    return cos, jnp.zeros((n_tok, LANES), F32)


def _pick_tile(n, pref):
    t = min(n, pref)
    while n % t:
        t //= 2
    return t


def _pick_ff_tile(ff):
    best = LANES
    for m in range(1, ff // LANES + 1):
        if ff % (m * LANES) == 0 and m * LANES <= FFN_MAX_FF_TILE:
            best = m * LANES
    return best


@jax.jit
def _forward(x, c, ctx, c_ctx, w_mod, b_mod, ln1_g, ln2_g, w_in, w_gla_g2, b_gla_g2, gla_norm_g,
             mla_q_norm_g, w_uq, mla_kv_norm_g, w_ukv, w_out, ffn_w_gate, ffn_w_up, ffn_w_down,
             router_w, exp_w_gate, exp_w_up, exp_w_down, final_norm_g):
    batch, seq, d = x.shape
    n_ctx = ctx.shape[1]
    depth = w_mod.shape[0]
    assert d == D_MODEL and batch < 8, (d, batch)

    cvec = jnp.zeros((8, d), F32).at[:batch].set(c).at[batch].set(c_ctx)
    mods_all = _modulation(cvec, w_mod, b_mod).reshape(depth, 8, 6, d)

    xl = x.reshape(batch * seq, d)
    xc = ctx.reshape(batch * n_ctx, d)

    tm_l = _pick_tile(seq, TOKEN_TILE)
    tm_c = _pick_tile(n_ctx, CTX_TILE)
    tk_l = _pick_tile(seq, ATTN_TILE)
    cb_l = _pick_tile(seq, GLA_BLOCK)
    cb_c = _pick_tile(n_ctx, GLA_BLOCK)
    row_l = lambda tm: (lambda i: i // (seq // tm))
    row_c = lambda i: batch

    rope_l = _rope_tables(seq)
    rope_c = _identity_tables(tm_c)
    zero_state = jnp.zeros((2, batch, GLA_QK_W, GLA_DV), F32)

    for i in range(depth):
        need_ctx = i < depth - 1
        last = i == depth - 1
        mods = mods_all[i]
        ln1 = ln1_g[i].reshape(1, d)
        ln2 = ln2_g[i].reshape(1, d)
        w_in_r = _prep_in_weight(w_in[i])
        gates = _prep_gate_weight(w_gla_g2[i], b_gla_g2[i])
        mla_w = _prep_mla_weights(w_uq[i], w_ukv[i])
        qg = mla_q_norm_g[i].reshape(1, MLA_Q_RANK)
        kvg = mla_kv_norm_g[i].reshape(1, MLA_KV_RANK)
        gg = gla_norm_g[i].reshape(1, GLA_DV)
        wo = w_out[i].astype(BF16)

        p_l, gate_l = _inproj(xl, mods, row_l(tk_l), ln1, w_in_r, *gates, tk_l, cb_l)
        p_c, gate_c = _inproj(xc, mods, row_c, ln1, w_in_r, *gates, tm_c, cb_c)

        oc_f, oc_b, s_ctx = _gla(p_c, gate_c, *gates, zero_state, batch=batch, cb=cb_c)
        ol_f, ol_b, _ = _gla(p_l, gate_l, *gates, s_ctx, batch=batch, cb=cb_l)

        q_l, kt_l, v_l = _mlaprep(p_l, *rope_l, qg, kvg, mla_w, batch=batch, tm=tk_l)
        q_c, kt_c, v_c = _mlaprep(p_c, *rope_c, qg, kvg, mla_w, batch=batch, tm=tm_c)
        m_l = _attention_pipelined(q_l, kt_l, v_l, kt_c, v_c, tq=tk_l, n_sub=1)
        m_l = m_l.reshape(batch * seq, MLA_V_W)

        if i % 2 == 0:
            j = i // 2
            router = None
            wg = ffn_w_gate[j].astype(BF16)
            wu = ffn_w_up[j].astype(BF16)
            wd = ffn_w_down[j].astype(BF16)
        else:
            j = i // 2
            rw = jnp.zeros((d, LANES), F32).at[:, :N_EXPERTS].set(router_w[j])
            rw_hi = rw.astype(BF16)
            router = (rw_hi, (rw - rw_hi.astype(F32)).astype(BF16))
            wg, wu, wd = exp_w_gate[j], exp_w_up[j], exp_w_down[j]
        fin = final_norm_g.reshape(1, d) if last else None

        if router is None:
            xl = _outproj_ffn(xl, ol_f, ol_b, p_l, m_l, mods, row_l(tm_l), gg, wo, ln2,
                              wg, wu, wd, fin, tm_l)
        else:
            outs = _outproj(xl, ol_f, ol_b, p_l, m_l, mods, row_l(tm_l), gg, wo, ln2, router, tm_l)
            xl = _moe(outs[1], outs[0], outs[2], mods, row_l(tk_l), wg, wu, wd, fin, tk_l)

        if need_ctx:
            m_c = _attention(q_c, [(kt_c, v_c)], tq=tm_c, n_sub=1).reshape(batch * n_ctx, MLA_V_W)
            if router is None:
                xc = _outproj_ffn(xc, oc_f, oc_b, p_c, m_c, mods, row_c, gg, wo, ln2,
                                  wg, wu, wd, None, _pick_tile(batch * n_ctx, TOKEN_TILE))
            else:
                outs_c = _outproj(xc, oc_f, oc_b, p_c, m_c, mods, row_c, gg, wo, ln2, router, tm_c)
                tm_fc = _pick_tile(batch * n_ctx, TOKEN_TILE)
                r_c = outs_c[2]
                lane = jnp.arange(LANES, dtype=F32)[None, :]
                comb_c = (jnp.where(lane == r_c[:, ROUTE_E1:ROUTE_E1 + 1], r_c[:, ROUTE_W1:ROUTE_W1 + 1], 0.0)
                          + jnp.where(lane == r_c[:, ROUTE_E2:ROUTE_E2 + 1], r_c[:, ROUTE_W2:ROUTE_W2 + 1], 0.0))
                bits = lax.bitcast_convert_type(outs_c[1], jnp.uint32)
                h2_c = jnp.concatenate([lax.bitcast_convert_type(bits << 16, F32),
                                        lax.bitcast_convert_type(bits & jnp.uint32(0xFFFF0000), F32)], axis=1)
                xc = _ffn(h2_c.astype(BF16), outs_c[0], mods, row_c, comb_c,
                          wg.astype(BF16), wu.astype(BF16), wd.astype(BF16), None, tm_fc,
                          _pick_ff_tile(wg.shape[2]))

    return xl.reshape(batch, seq, d)


def kernel(x, c, ctx, c_ctx, w_mod, b_mod, ln1_g, ln2_g, w_in, w_gla_g2, b_gla_g2, gla_norm_g,
           mla_q_norm_g, w_uq, mla_kv_norm_g, w_ukv, w_out, ffn_w_gate, ffn_w_up, ffn_w_down,
           router_w, exp_w_gate, exp_w_up, exp_w_down, final_norm_g):
    return _forward(x, c, ctx, c_ctx, w_mod, b_mod, ln1_g, ln2_g, w_in, w_gla_g2, b_gla_g2,
                    gla_norm_g, mla_q_norm_g, w_uq, mla_kv_norm_g, w_ukv, w_out, ffn_w_gate,
                    ffn_w_up, ffn_w_down, router_w, exp_w_gate, exp_w_up, exp_w_down, final_norm_g)
```
